```python
import jax, jax.numpy as jnp
from jax import lax
import numpy as np

D_MODEL = 2048
BATCH = 8
SEQ = 4096
DEPTH = 2

N_HEADS_MLA = 8
QK_NOPE_DIM = 128
QK_ROPE_DIM = 64
V_HEAD_DIM = 128
Q_LORA_RANK = 512
KV_LORA_RANK = 256
D_MLA = N_HEADS_MLA * V_HEAD_DIM
ROPE_THETA = 10000.0
Q_BLOCK = 128
POOL_WINDOWS = (2, 4, 8, 16)
N_POOL_GROUPS = 4
POOL_GROUP_DIM = 128
D_POOL = N_POOL_GROUPS * POOL_GROUP_DIM
N_CONV_HEADS = 4
CONV_HEAD_DIM = 128
D_CONV = N_CONV_HEADS * CONV_HEAD_DIM
CONV_WIDTH = 3
D_MIX = D_MLA + D_POOL + D_CONV
SPLIT_SIZES = (Q_LORA_RANK, KV_LORA_RANK, QK_ROPE_DIM, D_MLA, D_POOL, D_POOL, D_CONV, D_CONV, D_CONV, D_CONV)
D_IN_PROJ = sum(SPLIT_SIZES)
LN_EPS = 1e-5
RMS_EPS = 1e-6
DEEPNORM_ALPHA = (2 * DEPTH) ** 0.25
DEEPNORM_BETA = (8 * DEPTH) ** -0.25

kernel_name = "hybrid_mla_pool_shortconv_deepnorm"


def layernorm(x, g, b):
    xf = x.astype(jnp.float32)
    mu = jnp.mean(xf, axis=-1, keepdims=True)
    var = jnp.mean(jnp.square(xf - mu), axis=-1, keepdims=True)
    y = (xf - mu) * lax.rsqrt(var + LN_EPS) * g.astype(jnp.float32) + b.astype(jnp.float32)
    return y.astype(x.dtype)


def rmsnorm(x, g):
    xf = x.astype(jnp.float32)
    y = xf * lax.rsqrt(jnp.mean(jnp.square(xf), axis=-1, keepdims=True) + RMS_EPS) * g.astype(jnp.float32)
    return y.astype(x.dtype)


def apply_rope(x, cos, sin):
    half = QK_ROPE_DIM // 2
    xf = x.astype(jnp.float32)
    x1, x2 = xf[..., :half], xf[..., half:]
    return jnp.concatenate([x1 * cos - x2 * sin, x2 * cos + x1 * sin], axis=-1).astype(x.dtype)


def mla_mixer(q_lat, kv_lat, k_rope, positions, q_norm_g, kv_norm_g, w_uq, w_ukv):
    Bn, S, _ = q_lat.shape
    H = N_HEADS_MLA
    q = (rmsnorm(q_lat, q_norm_g) @ w_uq).reshape(Bn, S, H, QK_NOPE_DIM + QK_ROPE_DIM)
    q_nope, q_rope = q[..., :QK_NOPE_DIM], q[..., QK_NOPE_DIM:]
    kv = (rmsnorm(kv_lat, kv_norm_g) @ w_ukv).reshape(Bn, S, H, QK_NOPE_DIM + V_HEAD_DIM)
    k_nope, v = kv[..., :QK_NOPE_DIM], kv[..., QK_NOPE_DIM:]
    half = QK_ROPE_DIM // 2
    inv_freq = ROPE_THETA ** (-jnp.arange(half, dtype=jnp.float32) / half)
    ang = positions.astype(jnp.float32)[..., None] * inv_freq
    cos, sin = jnp.cos(ang), jnp.sin(ang)
    q_rope = apply_rope(q_rope, cos[:, :, None, :], sin[:, :, None, :])
    k_rope = apply_rope(k_rope, cos, sin)
    scale = (QK_NOPE_DIM + QK_ROPE_DIM) ** -0.5
    nb = S // Q_BLOCK
    qn_blocks = q_nope.reshape(Bn, nb, Q_BLOCK, H, QK_NOPE_DIM).transpose(1, 0, 2, 3, 4)
    qr_blocks = q_rope.reshape(Bn, nb, Q_BLOCK, H, QK_ROPE_DIM).transpose(1, 0, 2, 3, 4)
    key_idx = jnp.arange(S)

    def attend(args):
        qn, qr, blk = args
        s = (jnp.einsum('bqhd,bkhd->bhqk', qn, k_nope).astype(jnp.float32)
             + jnp.einsum('bqhr,bkr->bhqk', qr, k_rope).astype(jnp.float32)) * scale
        q_idx = blk * Q_BLOCK + jnp.arange(Q_BLOCK)
        causal = key_idx[None, :] <= q_idx[:, None]
        s = jnp.where(causal[None, None], s, -jnp.inf)
        p = jax.nn.softmax(s, axis=-1).astype(v.dtype)
        return jnp.einsum('bhqk,bkhd->bqhd', p, v)

    out = lax.map(attend, (qn_blocks, qr_blocks, jnp.arange(nb)))
    return out.transpose(1, 0, 2, 3, 4).reshape(Bn, S, D_MLA)


def pool_mixer(h, w_pool, pool_scale):
    Bn, S, _ = h.shape
    hg = h.reshape(Bn, S, N_POOL_GROUPS, POOL_GROUP_DIM).astype(jnp.float32)
    cs = jnp.cumsum(hg, axis=1)
    t1 = jnp.arange(1, S + 1, dtype=jnp.float32)
    means = []
    for g, w in enumerate(POOL_WINDOWS):
        c = cs[:, :, g]
        lag = jnp.pad(c, ((0, 0), (w, 0), (0, 0)))[:, :S]
        means.append((c - lag) / jnp.minimum(t1, float(w))[None, :, None])
    pooled = (jnp.stack(means, axis=2) - hg).astype(h.dtype)
    y = jnp.einsum('bsgc,gcd->bsgd', pooled, w_pool).reshape(Bn, S, D_POOL)
    return y * pool_scale


def conv_mixer(h, b_gate, c_gate, conv_w):
    u = c_gate * h
    y = lax.conv_general_dilated(u, conv_w[:, None, :], window_strides=(1,),
                                 padding=((CONV_WIDTH - 1, 0),),
                                 dimension_numbers=('NWC', 'WIO', 'NWC'),
                                 feature_group_count=D_CONV)
    return b_gate * y


def hybrid_layer(x, positions, w_in, q_norm_g, kv_norm_g, w_uq, w_ukv, w_pool, pool_scale,
                 conv_w, w_out, b_out, ln_g, ln_b):
    split_idx = np.cumsum(SPLIT_SIZES)[:-1].tolist()
    proj = x @ w_in
    (q_lat, kv_lat, k_rope, g_mla, p_in, g_pool, c_h, c_b, c_c, g_conv) = jnp.split(proj, split_idx, axis=-1)
    y_mla = mla_mixer(q_lat, kv_lat, k_rope, positions, q_norm_g, kv_norm_g, w_uq, w_ukv) * jax.nn.silu(g_mla)
    y_pool = pool_mixer(p_in, w_pool, pool_scale) * jax.nn.silu(g_pool)
    y_conv = conv_mixer(c_h, c_b, c_c, conv_w) * jax.nn.silu(g_conv)
    mix = jnp.concatenate([y_mla, y_pool, y_conv], axis=-1)
    out = mix @ w_out + b_out
    return layernorm(DEEPNORM_ALPHA * x + out, ln_g, ln_b)


def _fwd_setup_inputs(seed: int = 0) -> dict:
    key = jax.random.key(seed)
    ks = jax.random.split(key, 16)
    f32 = jnp.float32
    nrm = lambda k, shape, s: jax.random.normal(k, shape, f32) * s
    x = jax.random.normal(ks[0], (BATCH, SEQ, D_MODEL), f32)
    positions = jnp.broadcast_to(jnp.arange(SEQ, dtype=jnp.int32), (BATCH, SEQ))
    return {
        "x": x,
        "positions": positions,
        "emb_ln_g": 1.0 + nrm(ks[1], (D_MODEL,), 0.01),
        "emb_ln_b": nrm(ks[2], (D_MODEL,), 0.01),
        "w_in": nrm(ks[3], (DEPTH, D_MODEL, D_IN_PROJ), D_MODEL ** -0.5),
        "q_norm_g": 1.0 + nrm(ks[4], (DEPTH, Q_LORA_RANK), 0.01),
        "kv_norm_g": 1.0 + nrm(ks[5], (DEPTH, KV_LORA_RANK), 0.01),
        "w_uq": nrm(ks[6], (DEPTH, Q_LORA_RANK, N_HEADS_MLA * (QK_NOPE_DIM + QK_ROPE_DIM)), Q_LORA_RANK ** -0.5),
        "w_ukv": nrm(ks[7], (DEPTH, KV_LORA_RANK, N_HEADS_MLA * (QK_NOPE_DIM + V_HEAD_DIM)), KV_LORA_RANK ** -0.5),
        "w_pool": nrm(ks[8], (DEPTH, N_POOL_GROUPS, POOL_GROUP_DIM, POOL_GROUP_DIM), POOL_GROUP_DIM ** -0.5),
        "pool_scale": 1.0 + nrm(ks[9], (DEPTH, D_POOL), 0.1),
        "conv_w": nrm(ks[10], (DEPTH, CONV_WIDTH, D_CONV), CONV_WIDTH ** -0.5),
        "w_out": nrm(ks[11], (DEPTH, D_MIX, D_MODEL), DEEPNORM_BETA * D_MIX ** -0.5),
        "b_out": nrm(ks[12], (DEPTH, D_MODEL), 0.01),
        "ln_g": 1.0 + nrm(ks[13], (DEPTH, D_MODEL), 0.01),
        "ln_b": nrm(ks[14], (DEPTH, D_MODEL), 0.01),
    }


def _fwd_reference(x, positions, emb_ln_g, emb_ln_b, w_in, q_norm_g, kv_norm_g, w_uq, w_ukv, w_pool,
              pool_scale, conv_w, w_out, b_out, ln_g, ln_b):
    h = layernorm(x, emb_ln_g, emb_ln_b)
    for l in range(DEPTH):
        h = hybrid_layer(h, positions, w_in[l], q_norm_g[l], kv_norm_g[l], w_uq[l], w_ukv[l], w_pool[l],
                         pool_scale[l], conv_w[l], w_out[l], b_out[l], ln_g[l], ln_b[l])
    return h


import jax as _jax
import jax.numpy as _jnp

TWIN_FORMAT = 'train_step'
FWD_PARAMS = ['x', 'positions', 'emb_ln_g', 'emb_ln_b', 'w_in', 'q_norm_g', 'kv_norm_g', 'w_uq', 'w_ukv', 'w_pool', 'pool_scale', 'conv_w', 'w_out', 'b_out', 'ln_g', 'ln_b']
TWIN_WEIGHTS = ['emb_ln_g', 'emb_ln_b', 'w_in', 'q_norm_g', 'kv_norm_g', 'w_uq', 'w_ukv', 'w_pool', 'pool_scale', 'conv_w', 'w_out', 'b_out', 'ln_g', 'ln_b']
TWIN_DIFF_INPUT = 'x'
TWIN_INPUTS = ['x', 'positions', 'emb_ln_g', 'emb_ln_b', 'w_in', 'q_norm_g', 'kv_norm_g', 'w_uq', 'w_ukv', 'w_pool', 'pool_scale', 'conv_w', 'w_out', 'b_out', 'ln_g', 'ln_b', 'loss_target', 'm_emb_ln_g', 'm_emb_ln_b', 'm_w_in', 'm_q_norm_g', 'm_kv_norm_g', 'm_w_uq', 'm_w_ukv', 'm_w_pool', 'm_pool_scale', 'm_conv_w', 'm_w_out', 'm_b_out', 'm_ln_g', 'm_ln_b', 'v_emb_ln_g', 'v_emb_ln_b', 'v_w_in', 'v_q_norm_g', 'v_kv_norm_g', 'v_w_uq', 'v_w_ukv', 'v_w_pool', 'v_pool_scale', 'v_conv_w', 'v_w_out', 'v_b_out', 'v_ln_g', 'v_ln_b']
TWIN_OUTPUTS = ['loss', 'grad_x', 'grad_emb_ln_g', 'grad_emb_ln_b', 'grad_w_in', 'grad_q_norm_g', 'grad_kv_norm_g', 'grad_w_uq', 'grad_w_ukv', 'grad_w_pool', 'grad_pool_scale', 'grad_conv_w', 'grad_w_out', 'grad_b_out', 'grad_ln_g', 'grad_ln_b', 'delta_emb_ln_g', 'delta_emb_ln_b', 'delta_w_in', 'delta_q_norm_g', 'delta_kv_norm_g', 'delta_w_uq', 'delta_w_ukv', 'delta_w_pool', 'delta_pool_scale', 'delta_conv_w', 'delta_w_out', 'delta_b_out', 'delta_ln_g', 'delta_ln_b', 'new_m_emb_ln_g', 'new_m_emb_ln_b', 'new_m_w_in', 'new_m_q_norm_g', 'new_m_kv_norm_g', 'new_m_w_uq', 'new_m_w_ukv', 'new_m_w_pool', 'new_m_pool_scale', 'new_m_conv_w', 'new_m_w_out', 'new_m_b_out', 'new_m_ln_g', 'new_m_ln_b', 'new_v_emb_ln_g', 'new_v_emb_ln_b', 'new_v_w_in', 'new_v_q_norm_g', 'new_v_kv_norm_g', 'new_v_w_uq', 'new_v_w_ukv', 'new_v_w_pool', 'new_v_pool_scale', 'new_v_conv_w', 'new_v_w_out', 'new_v_b_out', 'new_v_ln_g', 'new_v_ln_b']
TWIN_LEAF_KINDS = {'loss': 'loss', 'grad_x': 'grad_x', 'grad_emb_ln_g': 'grad_w', 'grad_emb_ln_b': 'grad_w', 'grad_w_in': 'grad_w', 'grad_q_norm_g': 'grad_w', 'grad_kv_norm_g': 'grad_w', 'grad_w_uq': 'grad_w', 'grad_w_ukv': 'grad_w', 'grad_w_pool': 'grad_w', 'grad_pool_scale': 'grad_w', 'grad_conv_w': 'grad_w', 'grad_w_out': 'grad_w', 'grad_b_out': 'grad_w', 'grad_ln_g': 'grad_w', 'grad_ln_b': 'grad_w', 'delta_emb_ln_g': 'delta_w', 'delta_emb_ln_b': 'delta_w', 'delta_w_in': 'delta_w', 'delta_q_norm_g': 'delta_w', 'delta_kv_norm_g': 'delta_w', 'delta_w_uq': 'delta_w', 'delta_w_ukv': 'delta_w', 'delta_w_pool': 'delta_w', 'delta_pool_scale': 'delta_w', 'delta_conv_w': 'delta_w', 'delta_w_out': 'delta_w', 'delta_b_out': 'delta_w', 'delta_ln_g': 'delta_w', 'delta_ln_b': 'delta_w', 'new_m_emb_ln_g': 'new_m', 'new_m_emb_ln_b': 'new_m', 'new_m_w_in': 'new_m', 'new_m_q_norm_g': 'new_m', 'new_m_kv_norm_g': 'new_m', 'new_m_w_uq': 'new_m', 'new_m_w_ukv': 'new_m', 'new_m_w_pool': 'new_m', 'new_m_pool_scale': 'new_m', 'new_m_conv_w': 'new_m', 'new_m_w_out': 'new_m', 'new_m_b_out': 'new_m', 'new_m_ln_g': 'new_m', 'new_m_ln_b': 'new_m', 'new_v_emb_ln_g': 'new_v', 'new_v_emb_ln_b': 'new_v', 'new_v_w_in': 'new_v', 'new_v_q_norm_g': 'new_v', 'new_v_kv_norm_g': 'new_v', 'new_v_w_uq': 'new_v', 'new_v_w_ukv': 'new_v', 'new_v_w_pool': 'new_v', 'new_v_pool_scale': 'new_v', 'new_v_conv_w': 'new_v', 'new_v_w_out': 'new_v', 'new_v_b_out': 'new_v', 'new_v_ln_g': 'new_v', 'new_v_ln_b': 'new_v'}


def _forward(args):
    return _fwd_reference(*[args[k] for k in FWD_PARAMS])


def _output_shape():
    def fwd():
        inp = _fwd_setup_inputs(0)
        return _fwd_reference(*[inp[k] for k in FWD_PARAMS])
    out = _jax.eval_shape(fwd)
    return out.shape, out.dtype

N_MICROBATCH = 1
ADAM_LR = 0.001
ADAM_B1 = 0.9
ADAM_B2 = 0.999
ADAM_EPS = 1e-08
ADAM_WD = 0.01
ADAM_STEP = 10
PER_EXAMPLE_BATCH_AXIS = {'x': 0, 'positions': 0, 'loss_target': 0}
SHARED_INPUTS = []
_WEIGHT_DTYPES = {'emb_ln_g': _jnp.float32, 'emb_ln_b': _jnp.float32, 'w_in': _jnp.float32, 'q_norm_g': _jnp.float32, 'kv_norm_g': _jnp.float32, 'w_uq': _jnp.float32, 'w_ukv': _jnp.float32, 'w_pool': _jnp.float32, 'pool_scale': _jnp.float32, 'conv_w': _jnp.float32, 'w_out': _jnp.float32, 'b_out': _jnp.float32, 'ln_g': _jnp.float32, 'ln_b': _jnp.float32}
MOMENT_SCALE = {'emb_ln_g': 2.622708e-01, 'emb_ln_b': 1.482202e-01, 'w_in': 1.455915e-02, 'q_norm_g': 4.815075e-03, 'kv_norm_g': 9.864977e-03, 'w_uq': 2.778057e-03, 'w_ukv': 3.499290e-03, 'w_pool': 1.661451e-02, 'pool_scale': 1.637547e-02, 'conv_w': 1.988607e-02, 'w_out': 2.533438e-02, 'b_out': 1.022201e-01, 'ln_g': 1.130577e+01, 'ln_b': 2.262855e-01}


def _to_microbatches(a, axis):
    t = _jnp.moveaxis(a, axis, 0)
    t = t.reshape((N_MICROBATCH, t.shape[0] // N_MICROBATCH) + t.shape[1:])
    return _jnp.moveaxis(t, 1, axis + 1)


def setup_inputs(seed: int = 0) -> dict:
    inp = _fwd_setup_inputs(seed)
    key = _jax.random.fold_in(_jax.random.key(seed), 7919)
    shape, _ = _output_shape()
    out = dict(inp)
    out["loss_target"] = _jax.random.normal(_jax.random.fold_in(key, 0), shape, _jnp.float32)
    for i, name in enumerate(TWIN_WEIGHTS):
        w = inp[name].astype(_jnp.float32)
        if MOMENT_SCALE is None:
            s = _jnp.sqrt(_jnp.mean(_jnp.square(w)) + 1e-30)
        else:
            s = MOMENT_SCALE[name]
        km, kv = _jax.random.split(_jax.random.fold_in(key, i + 1))
        out[name] = w
        out["m_" + name] = s * _jax.random.normal(km, w.shape, _jnp.float32)
        out["v_" + name] = (s * s) * _jax.random.uniform(kv, w.shape, _jnp.float32, 0.5, 1.5)
    if N_MICROBATCH > 1:
        for name, axis in PER_EXAMPLE_BATCH_AXIS.items():
            out[name] = _to_microbatches(out[name], axis)
    return {'x': out['x'], 'positions': out['positions'], 'emb_ln_g': out['emb_ln_g'], 'emb_ln_b': out['emb_ln_b'], 'w_in': out['w_in'], 'q_norm_g': out['q_norm_g'], 'kv_norm_g': out['kv_norm_g'], 'w_uq': out['w_uq'], 'w_ukv': out['w_ukv'], 'w_pool': out['w_pool'], 'pool_scale': out['pool_scale'], 'conv_w': out['conv_w'], 'w_out': out['w_out'], 'b_out': out['b_out'], 'ln_g': out['ln_g'], 'ln_b': out['ln_b'], 'loss_target': out['loss_target'], 'm_emb_ln_g': out['m_emb_ln_g'], 'm_emb_ln_b': out['m_emb_ln_b'], 'm_w_in': out['m_w_in'], 'm_q_norm_g': out['m_q_norm_g'], 'm_kv_norm_g': out['m_kv_norm_g'], 'm_w_uq': out['m_w_uq'], 'm_w_ukv': out['m_w_ukv'], 'm_w_pool': out['m_w_pool'], 'm_pool_scale': out['m_pool_scale'], 'm_conv_w': out['m_conv_w'], 'm_w_out': out['m_w_out'], 'm_b_out': out['m_b_out'], 'm_ln_g': out['m_ln_g'], 'm_ln_b': out['m_ln_b'], 'v_emb_ln_g': out['v_emb_ln_g'], 'v_emb_ln_b': out['v_emb_ln_b'], 'v_w_in': out['v_w_in'], 'v_q_norm_g': out['v_q_norm_g'], 'v_kv_norm_g': out['v_kv_norm_g'], 'v_w_uq': out['v_w_uq'], 'v_w_ukv': out['v_w_ukv'], 'v_w_pool': out['v_w_pool'], 'v_pool_scale': out['v_pool_scale'], 'v_conv_w': out['v_conv_w'], 'v_w_out': out['v_w_out'], 'v_b_out': out['v_b_out'], 'v_ln_g': out['v_ln_g'], 'v_ln_b': out['v_ln_b']}


def _loss(weights, diff, rest, loss_target):
    with _jax.named_scope("forward"):
        args = {**rest, TWIN_DIFF_INPUT: diff, **{k: w.astype(_WEIGHT_DTYPES[k]) for k, w in weights.items()}}
        y = _forward(args)
    with _jax.named_scope("loss_head"):
        err = _jnp.square(y.astype(_jnp.float32) - loss_target)
        return 0.5 * _jnp.sum(_jnp.mean(err, axis=-1)) if err.ndim else 0.5 * err


def _adamw(w, g, m, v):
    m = ADAM_B1 * m + (1.0 - ADAM_B1) * g
    v = ADAM_B2 * v + (1.0 - ADAM_B2) * _jnp.square(g)
    m_hat = m / (1.0 - ADAM_B1 ** ADAM_STEP)
    v_hat = v / (1.0 - ADAM_B2 ** ADAM_STEP)
    delta = -ADAM_LR * (m_hat / (_jnp.sqrt(v_hat) + ADAM_EPS) + ADAM_WD * w)
    return delta, m, v


def reference(x, positions, emb_ln_g, emb_ln_b, w_in, q_norm_g, kv_norm_g, w_uq, w_ukv, w_pool, pool_scale, conv_w, w_out, b_out, ln_g, ln_b, loss_target, m_emb_ln_g, m_emb_ln_b, m_w_in, m_q_norm_g, m_kv_norm_g, m_w_uq, m_w_ukv, m_w_pool, m_pool_scale, m_conv_w, m_w_out, m_b_out, m_ln_g, m_ln_b, v_emb_ln_g, v_emb_ln_b, v_w_in, v_q_norm_g, v_kv_norm_g, v_w_uq, v_w_ukv, v_w_pool, v_pool_scale, v_conv_w, v_w_out, v_b_out, v_ln_g, v_ln_b):
    given = dict(x=x, positions=positions, emb_ln_g=emb_ln_g, emb_ln_b=emb_ln_b, w_in=w_in, q_norm_g=q_norm_g, kv_norm_g=kv_norm_g, w_uq=w_uq, w_ukv=w_ukv, w_pool=w_pool, pool_scale=pool_scale, conv_w=conv_w, w_out=w_out, b_out=b_out, ln_g=ln_g, ln_b=ln_b, loss_target=loss_target, m_emb_ln_g=m_emb_ln_g, m_emb_ln_b=m_emb_ln_b, m_w_in=m_w_in, m_q_norm_g=m_q_norm_g, m_kv_norm_g=m_kv_norm_g, m_w_uq=m_w_uq, m_w_ukv=m_w_ukv, m_w_pool=m_w_pool, m_pool_scale=m_pool_scale, m_conv_w=m_conv_w, m_w_out=m_w_out, m_b_out=m_b_out, m_ln_g=m_ln_g, m_ln_b=m_ln_b, v_emb_ln_g=v_emb_ln_g, v_emb_ln_b=v_emb_ln_b, v_w_in=v_w_in, v_q_norm_g=v_q_norm_g, v_kv_norm_g=v_kv_norm_g, v_w_uq=v_w_uq, v_w_ukv=v_w_ukv, v_w_pool=v_w_pool, v_pool_scale=v_pool_scale, v_conv_w=v_conv_w, v_w_out=v_w_out, v_b_out=v_b_out, v_ln_g=v_ln_g, v_ln_b=v_ln_b)
    weights = {n: given[n] for n in TWIN_WEIGHTS}
    shared = {n: given[n] for n in SHARED_INPUTS}
    per_example = {n: given[n] for n in ['x', 'positions']}
    grad_fn = _jax.value_and_grad(_loss, argnums=(0, 1))

    def one_microbatch(ex, loss_target):
        ex = dict(ex)
        diff = ex.pop(TWIN_DIFF_INPUT)
        return grad_fn(weights, diff, {**shared, **ex}, loss_target)

    if N_MICROBATCH == 1:
        loss, (grad_w, grad_x) = one_microbatch(per_example, given["loss_target"])
    else:
        def body(carry, xs):
            loss_sum, grad_sum = carry
            l_k, (gw_k, gx_k) = one_microbatch(xs[0], xs[1])
            with _jax.named_scope("update"):
                return (loss_sum + l_k, _jax.tree.map(_jnp.add, grad_sum, gw_k)), gx_k

        init = (_jnp.zeros((), _jnp.float32), _jax.tree.map(_jnp.zeros_like, weights))
        (loss, grad_w), grad_x = _jax.lax.scan(body, init, (per_example, given["loss_target"]))
    with _jax.named_scope("update"):
        delta_w, new_m, new_v = {}, {}, {}
        for n in TWIN_WEIGHTS:
            delta_w[n], new_m[n], new_v[n] = _adamw(weights[n], grad_w[n], given["m_" + n], given["v_" + n])
    return (loss, grad_x, *[grad_w[n] for n in TWIN_WEIGHTS], *[delta_w[n] for n in TWIN_WEIGHTS],
            *[new_m[n] for n in TWIN_WEIGHTS], *[new_v[n] for n in TWIN_WEIGHTS])
```

```python
import jax
import jax.numpy as jnp
from jax import lax
from jax.experimental import pallas as pl
from jax.experimental.pallas import tpu as pltpu

F32 = jnp.float32
BF16 = jnp.bfloat16

N_DEV = 8
D_MODEL = 2048
DEPTH = 2
N_HEADS = 8
NOPE = 128
ROPE = 64
V_DIM = 128
Q_LORA = 512
KV_LORA = 256
D_MLA = N_HEADS * V_DIM
D_POOL = 512
D_CONV = 512
POOL_WINDOWS = (2, 4, 8, 16)
POOL_GROUP = 128
CONV_WIDTH = 3
D_MIX = D_MLA + D_POOL + D_CONV
D_IN_PROJ = 4928
ROPE_THETA = 10000.0
LN_EPS = 1e-5
RMS_EPS = 1e-6
ALPHA = (2 * DEPTH) ** 0.25
ATTN_SCALE = (NOPE + ROPE) ** -0.5
ADAM_LR = 0.001
ADAM_B1 = 0.9
ADAM_B2 = 0.999
ADAM_EPS = 1e-08
ADAM_WD = 0.01
ADAM_STEP = 10

O_GMLA, O_QLAT, O_PIN, O_GPOOL, O_CH, O_CB, O_CC, O_GCONV, O_KVLAT, O_KROPE = (
    0, 1024, 1536, 2048, 2560, 3072, 3584, 4096, 4608, 4864)
NPP = 5120
QC = NOPE + 2 * ROPE
HALO = 16

VMEM_LIMIT = 48 * 1024 * 1024
MESH_ID = pl.DeviceIdType.MESH


def _params(sem=None):
    return pltpu.CompilerParams(dimension_semantics=sem, vmem_limit_bytes=VMEM_LIMIT)


def _sigmoid(x):
    return 1.0 / (1.0 + jnp.exp(-x))


def _tile(dim, target):
    if dim <= target:
        return dim
    t = target - target % 128
    while dim % t:
        t -= 128
    return t


_DIMS = {"nn": (((1,), (0,)), ((), ())), "nt": (((1,), (1,)), ((), ())), "tn": (((0,), (0,)), ((), ()))}


def _mm(a, b, mode, out_dtype, name, res=None, bias=None, alpha=1.0, tm=1024, tn=1024, tk=512):
    if mode == "nn":
        (M, K), (K2, N) = a.shape, b.shape
    elif mode == "nt":
        (M, K), (N, K2) = a.shape, b.shape
    else:
        (K, M), (K2, N) = a.shape, b.shape
    assert K == K2
    tm, tn, tk = _tile(M, tm), _tile(N, tn), _tile(K, tk)
    nk = K // tk
    has_res, has_bias = res is not None, bias is not None

    def body(*refs):
        a_ref, b_ref = refs[0], refs[1]
        pos = 2
        res_ref = bias_ref = None
        if has_res:
            res_ref = refs[pos]
            pos += 1
        if has_bias:
            bias_ref = refs[pos]
            pos += 1
        o_ref, acc_ref = refs[pos], refs[pos + 1]
        k = pl.program_id(2)

        @pl.when(k == 0)
        def _():
            acc_ref[...] = jnp.zeros_like(acc_ref)

        acc_ref[...] += lax.dot_general(a_ref[...].astype(BF16), b_ref[...].astype(BF16), _DIMS[mode],
                                        preferred_element_type=F32)

        @pl.when(k == nk - 1)
        def _():
            r = acc_ref[...]
            if has_bias:
                r = r + bias_ref[...]
            if has_res:
                r = alpha * res_ref[...] + r
            o_ref[...] = r.astype(out_dtype)

    if mode == "nn":
        in_specs = [pl.BlockSpec((tm, tk), lambda i, j, k: (i, k)), pl.BlockSpec((tk, tn), lambda i, j, k: (k, j))]
    elif mode == "nt":
        in_specs = [pl.BlockSpec((tm, tk), lambda i, j, k: (i, k)), pl.BlockSpec((tn, tk), lambda i, j, k: (j, k))]
    else:
        in_specs = [pl.BlockSpec((tk, tm), lambda i, j, k: (k, i)), pl.BlockSpec((tk, tn), lambda i, j, k: (k, j))]
    args = [a, b]
    if has_res:
        in_specs.append(pl.BlockSpec((tm, tn), lambda i, j, k: (i, j)))
        args.append(res)
    if has_bias:
        in_specs.append(pl.BlockSpec((1, tn), lambda i, j, k: (0, j)))
        args.append(bias)
    return pl.pallas_call(
        body, name=name,
        out_shape=jax.ShapeDtypeStruct((M, N), out_dtype),
        grid=(M // tm, N // tn, nk),
        in_specs=in_specs,
        out_specs=pl.BlockSpec((tm, tn), lambda i, j, k: (i, j)),
        scratch_shapes=[pltpu.VMEM((tm, tn), F32)],
        compiler_params=_params(("parallel", "parallel", "arbitrary")),
    )(*args)


def _ln_fwd(z, g, b, name, tq=256):
    T, D = z.shape

    def body(z_ref, g_ref, b_ref, y_ref, yb_ref):
        zv = z_ref[...]
        mu = jnp.mean(zv, axis=1, keepdims=True)
        zc = zv - mu
        var = jnp.mean(zc * zc, axis=1, keepdims=True)
        y = zc * lax.rsqrt(var + LN_EPS) * g_ref[...] + b_ref[...]
        y_ref[...] = y
        yb_ref[...] = y.astype(BF16)

    row = pl.BlockSpec((tq, D), lambda i: (i, 0))
    vec = pl.BlockSpec((1, D), lambda i: (0, 0))
    return pl.pallas_call(
        body, name=name,
        out_shape=(jax.ShapeDtypeStruct((T, D), F32), jax.ShapeDtypeStruct((T, D), BF16)),
        grid=(T // tq,), in_specs=[row, vec, vec], out_specs=(row, row),
        compiler_params=_params(("parallel",)),
    )(z, g, b)


def _ln_bwd(dy, z, g, name, tq=256):
    T, D = z.shape

    def body(dy_ref, z_ref, g_ref, dz_ref, dzb_ref, dg_ref, db_ref, ds_ref):
        @pl.when(pl.program_id(0) == 0)
        def _():
            dg_ref[...] = jnp.zeros_like(dg_ref)
            db_ref[...] = jnp.zeros_like(db_ref)
            ds_ref[...] = jnp.zeros_like(ds_ref)

        zv, dyv = z_ref[...], dy_ref[...]
        mu = jnp.mean(zv, axis=1, keepdims=True)
        zc = zv - mu
        var = jnp.mean(zc * zc, axis=1, keepdims=True)
        rstd = lax.rsqrt(var + LN_EPS)
        xh = zc * rstd
        u = dyv * g_ref[...]
        dz = rstd * (u - jnp.mean(u, axis=1, keepdims=True) - xh * jnp.mean(u * xh, axis=1, keepdims=True))
        dz_ref[...] = dz
        dzb_ref[...] = dz.astype(BF16)
        dg_ref[...] += jnp.sum(dyv * xh, axis=0, keepdims=True)
        db_ref[...] += jnp.sum(dyv, axis=0, keepdims=True)
        ds_ref[...] += jnp.sum(dz, axis=0, keepdims=True)

    row = pl.BlockSpec((tq, D), lambda i: (i, 0))
    vec = pl.BlockSpec((1, D), lambda i: (0, 0))
    vshape = jax.ShapeDtypeStruct((1, D), F32)
    return pl.pallas_call(
        body, name=name,
        out_shape=(jax.ShapeDtypeStruct((T, D), F32), jax.ShapeDtypeStruct((T, D), BF16), vshape, vshape, vshape),
        grid=(T // tq,), in_specs=[row, row, vec], out_specs=(row, row, vec, vec, vec),
        compiler_params=_params(("arbitrary",)),
    )(dy, z, g)


def _loss_head(y, target, name, tq=256):
    T, D = y.shape

    def body(y_ref, t_ref, s_ref, dy_ref):
        @pl.when(pl.program_id(0) == 0)
        def _():
            s_ref[...] = jnp.zeros_like(s_ref)

        err = y_ref[...] - t_ref[...]
        s_ref[...] += jnp.sum(err * err)
        dy_ref[...] = err * (1.0 / D)

    row = pl.BlockSpec((tq, D), lambda i: (i, 0))
    acc = pl.BlockSpec((8, 128), lambda i: (0, 0))
    return pl.pallas_call(
        body, name=name,
        out_shape=(jax.ShapeDtypeStruct((8, 128), F32), jax.ShapeDtypeStruct((T, D), F32)),
        grid=(T // tq,), in_specs=[row, row], out_specs=(acc, row),
        compiler_params=_params(("arbitrary",)),
    )(y, target)


def _pblock(tq, width, offset):
    assert offset % width == 0
    blk = offset // width
    return pl.BlockSpec((tq, width), lambda i: (i, blk))


def _mix_fwd(proj, q_g, kv_g, w_pool, pool_scale, conv_w, name, tq=256):
    T = proj.shape[0]

    def body(ql_ref, kvl_ref, pin_ref, gp_ref, ch_ref, cb_ref, cc_ref, gc_ref, qg_ref, kvg_ref, wp_ref, ps_ref,
             cw_ref, qn_ref, kvn_ref, pooled_ref, cv_ref, ypc_ref, extp, extu):
        i = pl.program_id(0)
        for x_ref, g_ref, o_ref in ((ql_ref, qg_ref, qn_ref), (kvl_ref, kvg_ref, kvn_ref)):
            x = x_ref[...]
            r = lax.rsqrt(jnp.mean(x * x, axis=1, keepdims=True) + RMS_EPS)
            o_ref[...] = (x * r * g_ref[...]).astype(BF16)

        @pl.when(i == 0)
        def _():
            extp[0:HALO, :] = jnp.zeros((HALO, D_POOL), F32)
            extu[0:HALO, :] = jnp.zeros((HALO, D_CONV), F32)

        @pl.when(i > 0)
        def _():
            extp[0:HALO, :] = extp[tq:tq + HALO, :]
            extu[0:HALO, :] = extu[tq:tq + HALO, :]

        pin = pin_ref[...]
        extp[HALO:, :] = pin
        u = cc_ref[...] * ch_ref[...]
        extu[HALO:, :] = u
        t1 = (i * tq + lax.broadcasted_iota(jnp.int32, (tq, 1), 0) + 1).astype(F32)
        for g, w in enumerate(POOL_WINDOWS):
            cols = slice(g * POOL_GROUP, (g + 1) * POOL_GROUP)
            s = extp[:, cols]
            k = 1
            while k < w:
                s = s + pltpu.roll(s, k, 0)
                k *= 2
            mean = s[HALO:, :] / jnp.minimum(t1, float(w))
            pooled = (mean - pin[:, cols]).astype(BF16)
            pooled_ref[:, cols] = pooled
            r = jnp.dot(pooled, wp_ref[g], preferred_element_type=F32)
            gp = gp_ref[:, cols]
            ypc_ref[:, cols] = (r * ps_ref[:, cols] * (gp * _sigmoid(gp))).astype(BF16)
        eu = extu[...]
        u1 = pltpu.roll(eu, 1, 0)[HALO:, :]
        u2 = pltpu.roll(eu, 2, 0)[HALO:, :]
        cv = cw_ref[0:1, :] * u2 + cw_ref[1:2, :] * u1 + cw_ref[2:3, :] * u
        cv_ref[...] = cv
        gc = gc_ref[...]
        ypc_ref[:, D_POOL:] = (cb_ref[...] * cv * (gc * _sigmoid(gc))).astype(BF16)

    full = lambda shape: pl.BlockSpec(shape, lambda i: (0,) * len(shape))
    row = lambda w: pl.BlockSpec((tq, w), lambda i: (i, 0))
    return pl.pallas_call(
        body, name=name,
        out_shape=(jax.ShapeDtypeStruct((T, Q_LORA), BF16), jax.ShapeDtypeStruct((T, KV_LORA), BF16),
                   jax.ShapeDtypeStruct((T, D_POOL), BF16), jax.ShapeDtypeStruct((T, D_CONV), F32),
                   jax.ShapeDtypeStruct((T, D_POOL + D_CONV), BF16)),
        grid=(T // tq,),
        in_specs=[_pblock(tq, Q_LORA, O_QLAT), _pblock(tq, KV_LORA, O_KVLAT), _pblock(tq, 512, O_PIN),
                  _pblock(tq, 512, O_GPOOL), _pblock(tq, 512, O_CH), _pblock(tq, 512, O_CB), _pblock(tq, 512, O_CC),
                  _pblock(tq, 512, O_GCONV), full((1, Q_LORA)), full((1, KV_LORA)), full((4, 128, 128)),
                  full((1, D_POOL)), full((8, D_CONV))],
        out_specs=(row(Q_LORA), row(KV_LORA), row(D_POOL), row(D_CONV), row(D_POOL + D_CONV)),
        scratch_shapes=[pltpu.VMEM((tq + HALO, D_POOL), F32), pltpu.VMEM((tq + HALO, D_CONV), F32)],
        compiler_params=_params(("arbitrary",)),
    )(proj, proj, proj, proj, proj, proj, proj, proj, q_g, kv_g, w_pool, pool_scale, conv_w)


def _mix_bwd(dmix, proj, o, pooled, cv, w_pool, pool_scale, conv_w, name, tq=256):
    T = proj.shape[0]
    nt = T // tq
    n_ext = tq + HALO

    def body(dym_ref, dyp_ref, dyc_ref, gm_ref, gp_ref, ch_ref, cb_ref, cc_ref, gc_ref, o_ref, pooled_ref, cv_ref,
             wp_ref, ps_ref, cw_ref, do_ref, delta_ref, dgm_ref, dmid_ref, dwp_ref, dps_ref, dcw_ref, exte, extd):
        i = pl.program_id(0)
        tile = nt - 1 - i

        @pl.when(i == 0)
        def _():
            dwp_ref[...] = jnp.zeros_like(dwp_ref)
            dps_ref[...] = jnp.zeros_like(dps_ref)
            dcw_ref[...] = jnp.zeros_like(dcw_ref)
            exte[tq:, :] = jnp.zeros((HALO, D_POOL), F32)
            extd[tq:, :] = jnp.zeros((HALO, D_CONV), F32)

        @pl.when(i > 0)
        def _():
            exte[tq:, :] = exte[0:HALO, :]
            extd[tq:, :] = extd[0:HALO, :]

        gm = gm_ref[...]
        sig = _sigmoid(gm)
        dym = dym_ref[...]
        ov = o_ref[...]
        do = dym * (gm * sig)
        do_ref[...] = do.astype(BF16)
        prod = do * ov
        for h in range(N_HEADS):
            delta_ref[h] = jnp.sum(prod[:, h * V_DIM:(h + 1) * V_DIM], axis=1, keepdims=True)
        dgm_ref[...] = (dym * ov * (sig * (1.0 + gm * (1.0 - sig)))).astype(BF16)

        t1 = (tile * tq + lax.broadcasted_iota(jnp.int32, (tq, 1), 0) + 1).astype(F32)
        for g, w in enumerate(POOL_WINDOWS):
            cols = slice(g * POOL_GROUP, (g + 1) * POOL_GROUP)
            pg = pooled_ref[:, cols]
            r = jnp.dot(pg, wp_ref[g], preferred_element_type=F32)
            gp = gp_ref[:, cols]
            sg = _sigmoid(gp)
            sl = gp * sg
            dyg = dyp_ref[:, cols]
            ps = ps_ref[:, cols]
            dmid_ref[:, 512 + g * POOL_GROUP:512 + (g + 1) * POOL_GROUP] = (
                dyg * (r * ps) * (sg * (1.0 + gp * (1.0 - sg)))).astype(BF16)
            dps_ref[:, cols] += jnp.sum(dyg * r * sl, axis=0, keepdims=True)
            dr = (dyg * ps * sl).astype(BF16)
            dwp_ref[g] += lax.dot_general(pg, dr, _DIMS["tn"], preferred_element_type=F32)
            dpooled = lax.dot_general(dr, wp_ref[g], _DIMS["nt"], preferred_element_type=F32)
            exte[0:tq, cols] = dpooled / jnp.minimum(t1, float(w))
            s = exte[:, cols]
            k = 1
            while k < w:
                s = s + pltpu.roll(s, n_ext - k, 0)
                k *= 2
            dmid_ref[:, cols] = (s[0:tq, :] - dpooled).astype(BF16)

        gc = gc_ref[...]
        sg = _sigmoid(gc)
        sl = gc * sg
        dyc = dyc_ref[...]
        cb, cc, ch, cvv = cb_ref[...], cc_ref[...], ch_ref[...], cv_ref[...]
        dcv = dyc * cb * sl
        dmid_ref[:, 2560:3072] = (dyc * (cb * cvv) * (sg * (1.0 + gc * (1.0 - sg)))).astype(BF16)
        dmid_ref[:, 1536:2048] = (dyc * cvv * sl).astype(BF16)
        extd[0:tq, :] = dcv
        ed = extd[...]
        d1 = pltpu.roll(ed, n_ext - 1, 0)[0:tq, :]
        d2 = pltpu.roll(ed, n_ext - 2, 0)[0:tq, :]
        du = cw_ref[2:3, :] * dcv + cw_ref[1:2, :] * d1 + cw_ref[0:1, :] * d2
        u = cc * ch
        dcw_ref[0:1, :] += jnp.sum(u * d2, axis=0, keepdims=True)
        dcw_ref[1:2, :] += jnp.sum(u * d1, axis=0, keepdims=True)
        dcw_ref[2:3, :] += jnp.sum(u * dcv, axis=0, keepdims=True)
        dmid_ref[:, 1024:1536] = (du * cc).astype(BF16)
        dmid_ref[:, 2048:2560] = (du * ch).astype(BF16)

    def rblock(width, offset):
        assert offset % width == 0
        blk = offset // width
        return pl.BlockSpec((tq, width), lambda i: (nt - 1 - i, blk))

    full = lambda shape: pl.BlockSpec(shape, lambda i: (0,) * len(shape))
    return pl.pallas_call(
        body, name=name,
        out_shape=(jax.ShapeDtypeStruct((T, D_MLA), BF16), jax.ShapeDtypeStruct((N_HEADS, T, 1), F32),
                   jax.ShapeDtypeStruct((T, D_MLA), BF16), jax.ShapeDtypeStruct((T, 3072), BF16),
                   jax.ShapeDtypeStruct((4, 128, 128), F32), jax.ShapeDtypeStruct((1, D_POOL), F32),
                   jax.ShapeDtypeStruct((8, D_CONV), F32)),
        grid=(nt,),
        in_specs=[rblock(1024, 0), rblock(512, 1024), rblock(512, 1536),
                  rblock(1024, O_GMLA), rblock(512, O_GPOOL), rblock(512, O_CH), rblock(512, O_CB),
                  rblock(512, O_CC), rblock(512, O_GCONV), rblock(1024, 0), rblock(512, 0), rblock(512, 0),
                  full((4, 128, 128)), full((1, D_POOL)), full((8, D_CONV))],
        out_specs=(rblock(1024, 0), pl.BlockSpec((N_HEADS, tq, 1), lambda i: (0, nt - 1 - i, 0)), rblock(1024, 0),
                   rblock(3072, 0), full((4, 128, 128)), full((1, D_POOL)), full((8, D_CONV))),
        scratch_shapes=[pltpu.VMEM((n_ext, D_POOL), F32), pltpu.VMEM((n_ext, D_CONV), F32)],
        compiler_params=_params(("arbitrary",)),
    )(dmix, dmix, dmix, proj, proj, proj, proj, proj, proj, o, pooled, cv, w_pool, pool_scale, conv_w)


def _rms_bwd(proj, dqn, dkvn, q_g, kv_g, name, tq=256):
    T = proj.shape[0]

    def body(ql_ref, kvl_ref, dqn_ref, dkvn_ref, qg_ref, kvg_ref, dql_ref, dkvl_ref, dqg_ref, dkvg_ref):
        @pl.when(pl.program_id(0) == 0)
        def _():
            dqg_ref[...] = jnp.zeros_like(dqg_ref)
            dkvg_ref[...] = jnp.zeros_like(dkvg_ref)

        for x_ref, dy_ref, g_ref, dx_ref, dg_ref in ((ql_ref, dqn_ref, qg_ref, dql_ref, dqg_ref),
                                                     (kvl_ref, dkvn_ref, kvg_ref, dkvl_ref, dkvg_ref)):
            x, dy = x_ref[...], dy_ref[...]
            r = lax.rsqrt(jnp.mean(x * x, axis=1, keepdims=True) + RMS_EPS)
            xr = x * r
            u = dy * g_ref[...]
            dx_ref[...] = (r * (u - xr * jnp.mean(u * xr, axis=1, keepdims=True))).astype(BF16)
            dg_ref[...] += jnp.sum(dy * xr, axis=0, keepdims=True)

    row = lambda w: pl.BlockSpec((tq, w), lambda i: (i, 0))
    vec = lambda w: pl.BlockSpec((1, w), lambda i: (0, 0))
    return pl.pallas_call(
        body, name=name,
        out_shape=(jax.ShapeDtypeStruct((T, Q_LORA), BF16), jax.ShapeDtypeStruct((T, KV_LORA), BF16),
                   jax.ShapeDtypeStruct((1, Q_LORA), F32), jax.ShapeDtypeStruct((1, KV_LORA), F32)),
        grid=(T // tq,),
        in_specs=[_pblock(tq, Q_LORA, O_QLAT), _pblock(tq, KV_LORA, O_KVLAT), row(Q_LORA), row(KV_LORA),
                  vec(Q_LORA), vec(KV_LORA)],
        out_specs=(row(Q_LORA), row(KV_LORA), vec(Q_LORA), vec(KV_LORA)),
        compiler_params=_params(("arbitrary",)),
    )(proj, proj, dqn, dkvn, q_g, kv_g)


def _swap_halves(x, lo):
    return jnp.where(lo, pltpu.roll(x, 96, 1), pltpu.roll(x, 32, 1))


def _rope_fwd(q, kv, proj, cos_t, sin_t, name, tq=256):
    T = q.shape[0]

    def body(qn_ref, qr_ref, kv_ref, kr_ref, c_ref, s_ref, qc_ref, kc_ref):
        C, S = c_ref[...], s_ref[...]
        lane = lax.broadcasted_iota(jnp.int32, (tq, 128), 1)
        lo = (lane % ROPE) < (ROPE // 2)
        first = lane < ROPE

        def rope(x):
            return x * C + _swap_halves(x, lo) * S

        kr = jnp.where(first, rope(kr_ref[...]), 0.0).astype(BF16)
        for j in range(N_HEADS // 2):
            r = rope(qr_ref[:, j * 128:(j + 1) * 128])
            pair = (jnp.where(first, r, 0.0), jnp.where(first, pltpu.roll(r, 64, 1), 0.0))
            for hh in range(2):
                h = 2 * j + hh
                qc_ref[h, :, 0:NOPE] = qn_ref[:, h * NOPE:(h + 1) * NOPE].astype(BF16)
                qc_ref[h, :, NOPE:QC] = pair[hh].astype(BF16)
        for h in range(N_HEADS):
            kc_ref[h, :, 0:NOPE] = kv_ref[:, h * 256:h * 256 + NOPE]
            kc_ref[h, :, NOPE:QC] = kr

    out = jax.ShapeDtypeStruct((N_HEADS, T, QC), BF16)
    hblock = pl.BlockSpec((N_HEADS, tq, QC), lambda i: (0, i, 0))
    return pl.pallas_call(
        body, name=name, out_shape=(out, out), grid=(T // tq,),
        in_specs=[pl.BlockSpec((tq, 1024), lambda i: (i, 0)), pl.BlockSpec((tq, 512), lambda i: (i, 2)),
                  pl.BlockSpec((tq, 2048), lambda i: (i, 0)), _pblock(tq, 128, O_KROPE),
                  pl.BlockSpec((tq, 128), lambda i: (i, 0)), pl.BlockSpec((tq, 128), lambda i: (i, 0))],
        out_specs=(hblock, hblock),
        compiler_params=_params(("parallel",)),
    )(q, q, kv, proj, cos_t, sin_t)


def _rope_bwd(dqc, dkr, cos_t, sin_t, name, tq=256):
    T = dqc.shape[1]

    def body(dqc_ref, dkr_ref, c_ref, s_ref, dq_ref, dk_ref):
        C, S = c_ref[...], s_ref[...]
        lane = lax.broadcasted_iota(jnp.int32, (tq, 128), 1)
        lo = (lane % ROPE) < (ROPE // 2)
        first = lane < ROPE

        def unrope(dy):
            return dy * C - _swap_halves(dy, lo) * S

        acc = dkr_ref[0]
        for h in range(1, N_HEADS):
            acc = acc + dkr_ref[h]
        dk_ref[:, 0:128] = jnp.where(first, unrope(acc), 0.0).astype(BF16)
        dk_ref[:, 128:256] = jnp.zeros((tq, 128), BF16)
        for j in range(N_HEADS // 2):
            d0 = dqc_ref[2 * j, :, NOPE:QC]
            d1 = dqc_ref[2 * j + 1, :, NOPE:QC]
            comb = jnp.where(first, d0, pltpu.roll(d1, 64, 1))
            dq_ref[:, 1024 + j * 128:1024 + (j + 1) * 128] = unrope(comb).astype(BF16)
        for h in range(N_HEADS):
            dq_ref[:, h * NOPE:(h + 1) * NOPE] = dqc_ref[h, :, 0:NOPE].astype(BF16)

    tab = pl.BlockSpec((tq, 128), lambda i: (i, 0))
    return pl.pallas_call(
        body, name=name,
        out_shape=(jax.ShapeDtypeStruct((T, 1536), BF16), jax.ShapeDtypeStruct((T, 256), BF16)),
        grid=(T // tq,),
        in_specs=[pl.BlockSpec((N_HEADS, tq, QC), lambda i: (0, i, 0)),
                  pl.BlockSpec((N_HEADS, tq, 128), lambda i: (0, i, 0)), tab, tab],
        out_specs=(pl.BlockSpec((tq, 1536), lambda i: (i, 0)), pl.BlockSpec((tq, 256), lambda i: (i, 0))),
        compiler_params=_params(("parallel",)),
    )(dqc, dkr, cos_t, sin_t)


def _causal_scores(q, k, i, j, tq, tk):
    s = lax.dot_general(q, k, _DIMS["nt"], preferred_element_type=F32) * ATTN_SCALE
    row = i * tq + lax.broadcasted_iota(jnp.int32, (tq, tk), 0)
    col = j * tk + lax.broadcasted_iota(jnp.int32, (tq, tk), 1)
    return jnp.where(col <= row, s, -jnp.inf)


def _flash_fwd(qc, kc, kv, proj, name, tq=512, tk=512):
    H, T, _ = qc.shape
    nq, nk = T // tq, T // tk
    last = lambda i: ((i + 1) * tq - 1) // tk

    def body(q_ref, k_ref, v_ref, g_ref, o_ref, y_ref, lse_ref, m_sc, l_sc, acc_sc):
        i, j = pl.program_id(1), pl.program_id(2)

        @pl.when(j == 0)
        def _():
            m_sc[...] = jnp.full_like(m_sc, -jnp.inf)
            l_sc[...] = jnp.zeros_like(l_sc)
            acc_sc[...] = jnp.zeros_like(acc_sc)

        @pl.when(j <= last(i))
        def _():
            s = _causal_scores(q_ref[0], k_ref[0], i, j, tq, tk)
            m_prev = m_sc[...]
            m_new = jnp.maximum(m_prev, jnp.max(s, axis=1, keepdims=True))
            a = jnp.exp(m_prev - m_new)
            p = jnp.exp(s - m_new)
            l_sc[...] = a * l_sc[...] + jnp.sum(p, axis=1, keepdims=True)
            acc_sc[...] = a * acc_sc[...] + jnp.dot(p.astype(BF16), v_ref[...], preferred_element_type=F32)
            m_sc[...] = m_new

        @pl.when(j == last(i))
        def _():
            l = l_sc[...]
            o = acc_sc[...] / l
            o_ref[...] = o
            lse_ref[0] = m_sc[...] + jnp.log(l)
            g = g_ref[...]
            y_ref[...] = (o * (g * _sigmoid(g))).astype(BF16)

    kmap = lambda h, i, j: (h, jnp.minimum(j, last(i)), 0)
    return pl.pallas_call(
        body, name=name,
        out_shape=(jax.ShapeDtypeStruct((T, D_MLA), F32), jax.ShapeDtypeStruct((T, D_MLA), BF16),
                   jax.ShapeDtypeStruct((H, T, 1), F32)),
        grid=(H, nq, nk),
        in_specs=[pl.BlockSpec((1, tq, QC), lambda h, i, j: (h, i, 0)),
                  pl.BlockSpec((1, tk, QC), kmap),
                  pl.BlockSpec((tk, V_DIM), lambda h, i, j: (jnp.minimum(j, last(i)), 2 * h + 1)),
                  pl.BlockSpec((tq, V_DIM), lambda h, i, j: (i, h))],
        out_specs=(pl.BlockSpec((tq, V_DIM), lambda h, i, j: (i, h)),
                   pl.BlockSpec((tq, V_DIM), lambda h, i, j: (i, h)),
                   pl.BlockSpec((1, tq, 1), lambda h, i, j: (h, i, 0))),
        scratch_shapes=[pltpu.VMEM((tq, 1), F32), pltpu.VMEM((tq, 1), F32), pltpu.VMEM((tq, V_DIM), F32)],
        compiler_params=_params(("parallel", "parallel", "arbitrary")),
    )(qc, kc, kv, proj)


def _flash_bwd_dkv(qc, kc, kv, do, lse, delta, name, tq=512, tk=512):
    H, T, _ = qc.shape
    nq, nk = T // tq, T // tk
    first = lambda j: (j * tk) // tq

    def body(q_ref, k_ref, v_ref, do_ref, lse_ref, dl_ref, dkv_ref, dkr_ref, dk_sc, dv_sc):
        j, i = pl.program_id(1), pl.program_id(2)

        @pl.when(i == 0)
        def _():
            dk_sc[...] = jnp.zeros_like(dk_sc)
            dv_sc[...] = jnp.zeros_like(dv_sc)

        @pl.when(i >= first(j))
        def _():
            q = q_ref[0]
            dov = do_ref[...]
            p = jnp.exp(_causal_scores(q, k_ref[0], i, j, tq, tk) - lse_ref[0])
            dv_sc[...] += lax.dot_general(p.astype(BF16), dov, _DIMS["tn"], preferred_element_type=F32)
            dp = lax.dot_general(dov, v_ref[...], _DIMS["nt"], preferred_element_type=F32)
            ds = (p * (dp - dl_ref[0]) * ATTN_SCALE).astype(BF16)
            dk_sc[...] += lax.dot_general(ds, q, _DIMS["tn"], preferred_element_type=F32)

        @pl.when(i == nq - 1)
        def _():
            dk = dk_sc[...]
            dkv_ref[:, 0:NOPE] = dk[:, 0:NOPE].astype(BF16)
            dkv_ref[:, NOPE:] = dv_sc[...].astype(BF16)
            dkr_ref[0] = dk[:, NOPE:]

    qmap = lambda h, j, i: (h, jnp.maximum(i, first(j)), 0)
    return pl.pallas_call(
        body, name=name,
        out_shape=(jax.ShapeDtypeStruct((T, 2 * D_MLA), BF16), jax.ShapeDtypeStruct((H, T, 128), F32)),
        grid=(H, nk, nq),
        in_specs=[pl.BlockSpec((1, tq, QC), qmap),
                  pl.BlockSpec((1, tk, QC), lambda h, j, i: (h, j, 0)),
                  pl.BlockSpec((tk, V_DIM), lambda h, j, i: (j, 2 * h + 1)),
                  pl.BlockSpec((tq, V_DIM), lambda h, j, i: (jnp.maximum(i, first(j)), h)),
                  pl.BlockSpec((1, tq, 1), qmap), pl.BlockSpec((1, tq, 1), qmap)],
        out_specs=(pl.BlockSpec((tk, 256), lambda h, j, i: (j, h)),
                   pl.BlockSpec((1, tk, 128), lambda h, j, i: (h, j, 0))),
        scratch_shapes=[pltpu.VMEM((tk, QC), F32), pltpu.VMEM((tk, V_DIM), F32)],
        compiler_params=_params(("parallel", "parallel", "arbitrary")),
    )(qc, kc, kv, do, lse, delta)


def _flash_bwd_dq(qc, kc, kv, do, lse, delta, name, tq=512, tk=512):
    H, T, _ = qc.shape
    nq, nk = T // tq, T // tk
    last = lambda i: ((i + 1) * tq - 1) // tk

    def body(q_ref, k_ref, v_ref, do_ref, lse_ref, dl_ref, dq_ref, dq_sc):
        i, j = pl.program_id(1), pl.program_id(2)

        @pl.when(j == 0)
        def _():
            dq_sc[...] = jnp.zeros_like(dq_sc)

        @pl.when(j <= last(i))
        def _():
            k = k_ref[0]
            dov = do_ref[...]
            p = jnp.exp(_causal_scores(q_ref[0], k, i, j, tq, tk) - lse_ref[0])
            dp = lax.dot_general(dov, v_ref[...], _DIMS["nt"], preferred_element_type=F32)
            ds = (p * (dp - dl_ref[0]) * ATTN_SCALE).astype(BF16)
            dq_sc[...] += jnp.dot(ds, k, preferred_element_type=F32)

        @pl.when(j == last(i))
        def _():
            dq_ref[0] = dq_sc[...]

    qmap = lambda h, i, j: (h, i, 0)
    return pl.pallas_call(
        body, name=name,
        out_shape=jax.ShapeDtypeStruct((H, T, QC), F32),
        grid=(H, nq, nk),
        in_specs=[pl.BlockSpec((1, tq, QC), qmap),
                  pl.BlockSpec((1, tk, QC), lambda h, i, j: (h, jnp.minimum(j, last(i)), 0)),
                  pl.BlockSpec((tk, V_DIM), lambda h, i, j: (jnp.minimum(j, last(i)), 2 * h + 1)),
                  pl.BlockSpec((tq, V_DIM), lambda h, i, j: (i, h)),
                  pl.BlockSpec((1, tq, 1), qmap), pl.BlockSpec((1, tq, 1), qmap)],
        out_specs=pl.BlockSpec((1, tq, QC), qmap),
        scratch_shapes=[pltpu.VMEM((tq, QC), F32)],
        compiler_params=_params(("parallel", "parallel", "arbitrary")),
    )(qc, kc, kv, do, lse, delta)


def _adamw(land, w, m, v, name, rows):
    R, C = w.shape
    assert R % rows == 0
    c1 = 1.0 - ADAM_B1 ** ADAM_STEP
    c2 = 1.0 - ADAM_B2 ** ADAM_STEP

    def body(land_ref, w_ref, m_ref, v_ref, g_ref, d_ref, nm_ref, nv_ref):
        g = land_ref[0]
        for s in range(1, N_DEV):
            g = g + land_ref[s]
        nm = ADAM_B1 * m_ref[...] + (1.0 - ADAM_B1) * g
        nv = ADAM_B2 * v_ref[...] + (1.0 - ADAM_B2) * (g * g)
        g_ref[...] = g
        nm_ref[...] = nm
        nv_ref[...] = nv
        d_ref[...] = -ADAM_LR * ((nm / c1) / (jnp.sqrt(nv / c2) + ADAM_EPS) + ADAM_WD * w_ref[...])

    blk = pl.BlockSpec((rows, C), lambda i: (i, 0))
    out = jax.ShapeDtypeStruct((R, C), F32)
    return pl.pallas_call(
        body, name=name, out_shape=(out, out, out, out), grid=(R // rows,),
        in_specs=[pl.BlockSpec((N_DEV, rows, C), lambda i: (0, i, 0)), blk, blk, blk],
        out_specs=(blk, blk, blk, blk),
        compiler_params=_params(("parallel",)),
    )(land, w, m, v)


def _mesh_pos():
    return lax.axis_index("x"), lax.axis_index("y"), lax.axis_index("c")


def _all_gather(arrays, name):
    n = len(arrays)

    def body(*refs):
        ins, outs = refs[:n], refs[n:2 * n]
        send_sems, recv_sems, local_sems = refs[2 * n:]
        x, y, c = _mesh_pos()
        me, sibling = (x, y, c), (x, y, 1 - c)
        chips = [(1 - x, y), (x, 1 - y), (1 - x, 1 - y)]

        def slot(a, pos):
            px, py, pc = pos
            return outs[a].at[4 * px + 2 * py + pc]

        def copy(a, k, block, to, src=None):
            return pltpu.make_async_remote_copy(
                src_ref=slot(a, block) if src is None else src, dst_ref=slot(a, block),
                send_sem=send_sems.at[a * 7 + k], recv_sem=recv_sems.at[a * 7 + k],
                device_id=to, device_id_type=MESH_ID)

        mine, first, passed = [], [], []
        for a in range(n):
            cp = pltpu.make_async_copy(ins[a], slot(a, me), local_sems.at[a])
            cp.start()
            mine.append(cp)
            cps = [copy(a, 0, me, sibling, src=ins[a])]
            cps += [copy(a, 1 + j, me, (*chip, c), src=ins[a]) for j, chip in enumerate(chips)]
            for cp in cps:
                cp.start()
            first += cps
        for j, chip in enumerate(chips):
            for a in range(n):
                copy(a, 1 + j, (*chip, c), me).wait_recv()
                cp = copy(a, 4 + j, (*chip, c), sibling)
                cp.start()
                passed.append(cp)
        for a in range(n):
            copy(a, 0, sibling, me).wait_recv()
            for j, chip in enumerate(chips):
                copy(a, 4 + j, (*chip, 1 - c), me).wait_recv()
        for cp in first + passed:
            cp.wait_send()
        for cp in mine:
            cp.wait()

    hbm = pl.BlockSpec(memory_space=pltpu.HBM)
    return pl.pallas_call(
        body, name=name,
        out_shape=tuple(jax.ShapeDtypeStruct((N_DEV,) + a.shape, a.dtype) for a in arrays),
        in_specs=[hbm] * n, out_specs=tuple([hbm] * n),
        scratch_shapes=[pltpu.SemaphoreType.DMA((7 * n,)), pltpu.SemaphoreType.DMA((7 * n,)),
                        pltpu.SemaphoreType.DMA((n,))],
    )(*arrays)


def _exchange(arrays, name):
    n = len(arrays)

    def body(*refs):
        ins, outs = refs[:n], refs[n:2 * n]
        send_sems, recv_sems, local_sems = refs[2 * n:]
        x, y, c = _mesh_pos()
        my_idx = 4 * x + 2 * y + c
        copies, local = [], []
        for a in range(n):
            cp = pltpu.make_async_copy(ins[a].at[my_idx], outs[a].at[my_idx], local_sems.at[a])
            cp.start()
            local.append(cp)
            for k in range(1, N_DEV):
                px = 1 - x if k & 4 else x
                py = 1 - y if k & 2 else y
                pc = 1 - c if k & 1 else c
                cp = pltpu.make_async_remote_copy(
                    src_ref=ins[a].at[4 * px + 2 * py + pc], dst_ref=outs[a].at[my_idx],
                    send_sem=send_sems.at[a * 7 + k - 1], recv_sem=recv_sems.at[a * 7 + k - 1],
                    device_id=(px, py, pc), device_id_type=MESH_ID)
                cp.start()
                copies.append(cp)
        for cp in copies:
            cp.wait()
        for cp in local:
            cp.wait()

    hbm = pl.BlockSpec(memory_space=pltpu.HBM)
    return pl.pallas_call(
        body, name=name,
        out_shape=tuple(jax.ShapeDtypeStruct(a.shape, a.dtype) for a in arrays),
        in_specs=[hbm] * n, out_specs=tuple([hbm] * n),
        scratch_shapes=[pltpu.SemaphoreType.DMA((7 * n,)), pltpu.SemaphoreType.DMA((7 * n,)),
                        pltpu.SemaphoreType.DMA((n,))],
    )(*arrays)


def _cat_blocks(g, axis):
    return jnp.concatenate([g[d] for d in range(N_DEV)], axis=axis)


def _permute_w_in(w):
    q_lat, kv_lat, k_rope, rest = w[:, 0:512], w[:, 512:768], w[:, 768:832], w[:, 832:]
    g_mla, others = rest[:, 0:1024], rest[:, 1024:]
    pad = jnp.zeros((w.shape[0], NPP - D_IN_PROJ), w.dtype)
    return jnp.concatenate([g_mla, q_lat, others, kv_lat, k_rope, pad], axis=1)


def _unpermute_w_in(w):
    return jnp.concatenate([w[:, O_QLAT:O_PIN], w[:, O_KVLAT:O_KROPE], w[:, O_KROPE:O_KROPE + ROPE],
                            w[:, O_GMLA:O_QLAT], w[:, O_PIN:O_KVLAT]], axis=1)


def _permute_w_uq(w):
    w3 = w.reshape(w.shape[0], N_HEADS, NOPE + ROPE)
    return jnp.concatenate([w3[:, :, :NOPE].reshape(w.shape[0], -1), w3[:, :, NOPE:].reshape(w.shape[0], -1)], axis=1)


def _unpermute_w_uq(w):
    nope = w[:, :N_HEADS * NOPE].reshape(w.shape[0], N_HEADS, NOPE)
    rope = w[:, N_HEADS * NOPE:].reshape(w.shape[0], N_HEADS, ROPE)
    return jnp.concatenate([nope, rope], axis=2).reshape(w.shape[0], -1)


_SMALL = (("emb_ln_g", 16), ("emb_ln_b", 16), ("q_norm_g", 8), ("kv_norm_g", 8), ("w_pool", 1024),
          ("pool_scale", 8), ("b_out", 32), ("ln_g", 32), ("ln_b", 32))
SMALL_ROWS = sum(r for _, r in _SMALL)


def _pack_small(d):
    parts = []
    for name, rows in _SMALL:
        flat = d[name].reshape(-1)
        flat = jnp.pad(flat, (0, rows * 128 - flat.shape[0]))
        parts.append(flat.reshape(rows, 128))
    return jnp.concatenate(parts, axis=0)


def _unpack_small(packed, shapes):
    out, r0 = {}, 0
    for name, rows in _SMALL:
        size = 1
        for s in shapes[name]:
            size *= s
        out[name] = packed[r0:r0 + rows].reshape(-1)[:size].reshape(shapes[name])
        r0 += rows
    return out


def _rope_tables(positions):
    half = ROPE // 2
    inv_freq = ROPE_THETA ** (-jnp.arange(half, dtype=F32) / half)
    ang = positions.astype(F32)[:, None] * inv_freq
    cos, sin = jnp.cos(ang), jnp.sin(ang)
    return jnp.concatenate([cos, cos, cos, cos], axis=1), jnp.concatenate([-sin, sin, -sin, sin], axis=1)


def _local_step(x, positions, target, W):
    cos_t, sin_t = _rope_tables(positions)
    h, hb = _ln_fwd(x, W["emb_ln_g"], W["emb_ln_b"], "emb_ln_fwd")
    saved = []
    for l in range(DEPTH):
        proj = _mm(hb, W["w_in"][l], "nn", F32, "proj_fwd")
        qn, kvn, pooled, cv, ypc = _mix_fwd(proj, W["q_norm_g"][l], W["kv_norm_g"][l], W["w_pool"][l],
                                            W["pool_scale"][l], W["conv_w"][l], "mix_fwd")
        q = _mm(qn, W["w_uq"][l], "nn", F32, "q_up_fwd")
        kv = _mm(kvn, W["w_ukv"][l], "nn", BF16, "kv_up_fwd")
        qc, kc = _rope_fwd(q, kv, proj, cos_t, sin_t, "rope_fwd")
        o, ymla, lse = _flash_fwd(qc, kc, kv, proj, "flash_fwd")
        mix = jnp.concatenate([ymla, ypc], axis=1)
        z = _mm(mix, W["w_out"][l], "nn", F32, "out_fwd", res=h, bias=W["b_out"][l], alpha=ALPHA)
        saved.append((hb, proj, qn, kvn, pooled, cv, kv, qc, kc, o, lse, mix, z))
        h, hb = _ln_fwd(z, W["ln_g"][l], W["ln_b"][l], "ln_fwd")
    sq, dh = _loss_head(h, target, "loss_head")

    grads = {k: [None] * DEPTH for k in ("w_in", "q_norm_g", "kv_norm_g", "w_uq", "w_ukv", "w_pool", "pool_scale",
                                         "conv_w", "w_out", "b_out", "ln_g", "ln_b")}
    for l in reversed(range(DEPTH)):
        hb_in, proj, qn, kvn, pooled, cv, kv, qc, kc, o, lse, mix, z = saved[l]
        dz, dzb, grads["ln_g"][l], grads["ln_b"][l], grads["b_out"][l] = _ln_bwd(dh, z, W["ln_g"][l], "ln_bwd")
        dmix = _mm(dzb, W["w_out"][l], "nt", F32, "out_bwd_x")
        grads["w_out"][l] = _mm(mix, dzb, "tn", F32, "out_bwd_w")
        do, delta, dgm, dmid, grads["w_pool"][l], grads["pool_scale"][l], grads["conv_w"][l] = _mix_bwd(
            dmix, proj, o, pooled, cv, W["w_pool"][l], W["pool_scale"][l], W["conv_w"][l], "mix_bwd")
        dkv, dkr = _flash_bwd_dkv(qc, kc, kv, do, lse, delta, "flash_bwd_dkv")
        dqc = _flash_bwd_dq(qc, kc, kv, do, lse, delta, "flash_bwd_dq")
        dq, dkrope = _rope_bwd(dqc, dkr, cos_t, sin_t, "rope_bwd")
        dqn = _mm(dq, W["w_uq"][l], "nt", F32, "q_up_bwd_x")
        grads["w_uq"][l] = _mm(qn, dq, "tn", F32, "q_up_bwd_w")
        dkvn = _mm(dkv, W["w_ukv"][l], "nt", F32, "kv_up_bwd_x")
        grads["w_ukv"][l] = _mm(kvn, dkv, "tn", F32, "kv_up_bwd_w")
        dql, dkvl, grads["q_norm_g"][l], grads["kv_norm_g"][l] = _rms_bwd(
            proj, dqn, dkvn, W["q_norm_g"][l], W["kv_norm_g"][l], "rms_bwd")
        dproj = jnp.concatenate([dgm, dql, dmid, dkvl, dkrope], axis=1)
        dh = _mm(dproj, W["w_in"][l], "nt", F32, "proj_bwd_x", res=dz, alpha=ALPHA)
        grads["w_in"][l] = _mm(hb_in, dproj, "tn", F32, "proj_bwd_w")
    grad_x, _, grads["emb_ln_g"], grads["emb_ln_b"], _ = _ln_bwd(dh, x, W["emb_ln_g"], "emb_ln_bwd")
    return sq, grad_x, grads


def kernel(x, positions, emb_ln_g, emb_ln_b, w_in, q_norm_g, kv_norm_g, w_uq, w_ukv, w_pool, pool_scale, conv_w, w_out, b_out, ln_g, ln_b, loss_target, m_emb_ln_g, m_emb_ln_b, m_w_in, m_q_norm_g, m_kv_norm_g, m_w_uq, m_w_ukv, m_w_pool, m_pool_scale, m_conv_w, m_w_out, m_b_out, m_ln_g, m_ln_b, v_emb_ln_g, v_emb_ln_b, v_w_in, v_q_norm_g, v_kv_norm_g, v_w_uq, v_w_ukv, v_w_pool, v_pool_scale, v_conv_w, v_w_out, v_b_out, v_ln_g, v_ln_b):
    weights = dict(emb_ln_g=emb_ln_g, emb_ln_b=emb_ln_b, w_in=w_in, q_norm_g=q_norm_g, kv_norm_g=kv_norm_g,
                   w_uq=w_uq, w_ukv=w_ukv, w_pool=w_pool, pool_scale=pool_scale, conv_w=conv_w, w_out=w_out,
                   b_out=b_out, ln_g=ln_g, ln_b=ln_b)
    mom1 = dict(emb_ln_g=m_emb_ln_g, emb_ln_b=m_emb_ln_b, w_in=m_w_in, q_norm_g=m_q_norm_g, kv_norm_g=m_kv_norm_g,
                w_uq=m_w_uq, w_ukv=m_w_ukv, w_pool=m_w_pool, pool_scale=m_pool_scale, conv_w=m_conv_w,
                w_out=m_w_out, b_out=m_b_out, ln_g=m_ln_g, ln_b=m_ln_b)
    mom2 = dict(emb_ln_g=v_emb_ln_g, emb_ln_b=v_emb_ln_b, w_in=v_w_in, q_norm_g=v_q_norm_g, kv_norm_g=v_kv_norm_g,
                w_uq=v_w_uq, w_ukv=v_w_ukv, w_pool=v_w_pool, pool_scale=v_pool_scale, conv_w=v_conv_w,
                w_out=v_w_out, b_out=v_b_out, ln_g=v_ln_g, ln_b=v_ln_b)

    conv_pad = jnp.zeros((8, 128), F32).at[0:DEPTH * CONV_WIDTH, 0:64].set(conv_w.reshape(DEPTH * CONV_WIDTH, 64))
    g_in, g_uq, g_ukv, g_out, g_conv = _all_gather(
        [w_in.astype(BF16), w_uq.astype(BF16), w_ukv.astype(BF16), w_out.astype(BF16), conv_pad], "weights_all_gather")
    conv_full = _cat_blocks(g_conv[:, 0:DEPTH * CONV_WIDTH, 0:64], 1).reshape(DEPTH, CONV_WIDTH, D_CONV)
    conv_full = jnp.pad(conv_full, ((0, 0), (0, 8 - CONV_WIDTH), (0, 0)))
    W = dict(
        emb_ln_g=emb_ln_g.reshape(1, -1), emb_ln_b=emb_ln_b.reshape(1, -1),
        w_in=[_permute_w_in(_cat_blocks(g_in[:, l], 1)) for l in range(DEPTH)],
        w_uq=[_permute_w_uq(_cat_blocks(g_uq[:, l], 1)) for l in range(DEPTH)],
        w_ukv=[_cat_blocks(g_ukv[:, l], 1) for l in range(DEPTH)],
        w_out=[_cat_blocks(g_out[:, l], 0) for l in range(DEPTH)],
        conv_w=[conv_full[l] for l in range(DEPTH)],
        q_norm_g=[q_norm_g[l].reshape(1, -1) for l in range(DEPTH)],
        kv_norm_g=[kv_norm_g[l].reshape(1, -1) for l in range(DEPTH)],
        w_pool=[w_pool[l].astype(BF16) for l in range(DEPTH)],
        pool_scale=[pool_scale[l].reshape(1, -1) for l in range(DEPTH)],
        b_out=[b_out[l].reshape(1, -1) for l in range(DEPTH)],
        ln_g=[ln_g[l].reshape(1, -1) for l in range(DEPTH)],
        ln_b=[ln_b[l].reshape(1, -1) for l in range(DEPTH)],
    )

    sq, grad_x, G = _local_step(x[0], positions[0], loss_target[0], W)
    loss = lax.psum(sq[0, 0] * (0.5 / D_MODEL), ("x", "y", "c"))

    d_in = jnp.stack([_unpermute_w_in(G["w_in"][l]) for l in range(DEPTH)])
    d_in = d_in.reshape(DEPTH, D_MODEL, N_DEV, -1).transpose(2, 0, 1, 3)
    d_uq = jnp.stack([_unpermute_w_uq(G["w_uq"][l]) for l in range(DEPTH)])
    d_uq = d_uq.reshape(DEPTH, Q_LORA, N_DEV, -1).transpose(2, 0, 1, 3)
    d_ukv = jnp.stack(G["w_ukv"]).reshape(DEPTH, KV_LORA, N_DEV, -1).transpose(2, 0, 1, 3)
    d_out = jnp.stack(G["w_out"]).reshape(DEPTH, N_DEV, -1, D_MODEL).transpose(1, 0, 2, 3)
    d_conv = jnp.stack([G["conv_w"][l][0:CONV_WIDTH] for l in range(DEPTH)])
    d_conv = d_conv.reshape(DEPTH * CONV_WIDTH, N_DEV, 64).transpose(1, 0, 2)
    d_conv = jnp.zeros((N_DEV, 8, 128), F32).at[:, 0:DEPTH * CONV_WIDTH, 0:64].set(d_conv)
    small = dict(emb_ln_g=G["emb_ln_g"], emb_ln_b=G["emb_ln_b"])
    for k in ("q_norm_g", "kv_norm_g", "w_pool", "pool_scale", "b_out", "ln_g", "ln_b"):
        small[k] = jnp.stack(G[k])
    d_small = jnp.broadcast_to(_pack_small(small)[None], (N_DEV, SMALL_ROWS, 128))
    l_in, l_uq, l_ukv, l_out, l_conv, l_small = _exchange([d_in, d_uq, d_ukv, d_out, d_conv, d_small],
                                                          "gradient_exchange")

    res = {}

    def adam(name, land, rows):
        shape = weights[name].shape
        r2 = lambda a: a.reshape(-1, shape[-1])
        outs = _adamw(land.reshape(N_DEV, -1, shape[-1]), r2(weights[name]), r2(mom1[name]), r2(mom2[name]),
                      "adamw_" + name, rows)
        res[name] = tuple(o.reshape(shape) for o in outs)

    adam("w_in", l_in, 256)
    adam("w_uq", l_uq, 256)
    adam("w_ukv", l_ukv, 256)
    adam("w_out", l_out, 128)
    conv_shard = lambda a: jnp.zeros((8, 128), F32).at[0:DEPTH * CONV_WIDTH, 0:64].set(a.reshape(-1, 64))
    conv_res = _adamw(l_conv, conv_shard(conv_w), conv_shard(m_conv_w), conv_shard(v_conv_w), "adamw_conv_w", 8)
    res["conv_w"] = tuple(o[0:DEPTH * CONV_WIDTH, 0:64].reshape(DEPTH, CONV_WIDTH, 64) for o in conv_res)
    small_res = _adamw(l_small, _pack_small(weights), _pack_small(mom1), _pack_small(mom2), "adamw_small", 392)
    shapes = {k: weights[k].shape for k, _ in _SMALL}
    unpacked = [_unpack_small(o, shapes) for o in small_res]
    for k, _ in _SMALL:
        res[k] = tuple(u[k] for u in unpacked)

    order = ("emb_ln_g", "emb_ln_b", "w_in", "q_norm_g", "kv_norm_g", "w_uq", "w_ukv", "w_pool", "pool_scale",
             "conv_w", "w_out", "b_out", "ln_g", "ln_b")
    return (loss, grad_x[None], *[res[k][0] for k in order], *[res[k][1] for k in order],
            *[res[k][2] for k in order], *[res[k][3] for k in order])
```

```python
import jax
import jax.numpy as jnp
from jax import lax
from jax.experimental import pallas as pl
from jax.experimental.pallas import tpu as pltpu

F32 = jnp.float32
BF16 = jnp.bfloat16

N_DEV = 8
D_MODEL = 2048
DEPTH = 2
N_HEADS = 8
NOPE = 128
ROPE = 64
V_DIM = 128
Q_LORA = 512
KV_LORA = 256
D_MLA = N_HEADS * V_DIM
D_POOL = 512
D_CONV = 512
POOL_WINDOWS = (2, 4, 8, 16)
POOL_GROUP = 128
CONV_WIDTH = 3
D_MIX = D_MLA + D_POOL + D_CONV
D_IN_PROJ = 4928
ROPE_THETA = 10000.0
LN_EPS = 1e-5
RMS_EPS = 1e-6
ALPHA = (2 * DEPTH) ** 0.25
ATTN_SCALE = (NOPE + ROPE) ** -0.5
ADAM_LR = 0.001
ADAM_B1 = 0.9
ADAM_B2 = 0.999
ADAM_EPS = 1e-08
ADAM_WD = 0.01
ADAM_STEP = 10

O_GMLA, O_QLAT, O_PIN, O_GPOOL, O_CH, O_CB, O_CC, O_GCONV, O_KVLAT, O_KROPE = (
    0, 1024, 1536, 2048, 2560, 3072, 3584, 4096, 4608, 4864)
NPP = 5120
QC = NOPE + 2 * ROPE
HALO = 16
ATT_TILE = 512
ATT_CH = 256
LOG2E = 1.4426950408889634
EXP2_SCALE = ATTN_SCALE * LOG2E

VMEM_LIMIT = 48 * 1024 * 1024
MESH_ID = pl.DeviceIdType.MESH


def _params(sem=None):
    return pltpu.CompilerParams(dimension_semantics=sem, vmem_limit_bytes=VMEM_LIMIT)


def _sigmoid(x):
    return 1.0 / (1.0 + jnp.exp(-x))


def _tile(dim, target):
    if dim <= target:
        return dim
    t = target - target % 128
    while dim % t:
        t -= 128
    return t


_DIMS = {"nn": (((1,), (0,)), ((), ())), "nt": (((1,), (1,)), ((), ())), "tn": (((0,), (0,)), ((), ()))}


def _mm(a, b, mode, out_dtype, name, res=None, bias=None, alpha=1.0, tm=1024, tn=1024, tk=512):
    if mode == "nn":
        (M, K), (K2, N) = a.shape, b.shape
    elif mode == "nt":
        (M, K), (N, K2) = a.shape, b.shape
    else:
        (K, M), (K2, N) = a.shape, b.shape
    assert K == K2
    tm, tn, tk = _tile(M, tm), _tile(N, tn), _tile(K, tk)
    nk = K // tk
    has_res, has_bias = res is not None, bias is not None

    def body(*refs):
        a_ref, b_ref = refs[0], refs[1]
        pos = 2
        res_ref = bias_ref = None
        if has_res:
            res_ref = refs[pos]
            pos += 1
        if has_bias:
            bias_ref = refs[pos]
            pos += 1
        o_ref, acc_ref = refs[pos], refs[pos + 1]
        k = pl.program_id(2)

        @pl.when(k == 0)
        def _():
            acc_ref[...] = jnp.zeros_like(acc_ref)

        acc_ref[...] += lax.dot_general(a_ref[...].astype(BF16), b_ref[...].astype(BF16), _DIMS[mode],
                                        preferred_element_type=F32)

        @pl.when(k == nk - 1)
        def _():
            r = acc_ref[...]
            if has_bias:
                r = r + bias_ref[...]
            if has_res:
                r = alpha * res_ref[...] + r
            o_ref[...] = r.astype(out_dtype)

    if mode == "nn":
        in_specs = [pl.BlockSpec((tm, tk), lambda i, j, k: (i, k)), pl.BlockSpec((tk, tn), lambda i, j, k: (k, j))]
    elif mode == "nt":
        in_specs = [pl.BlockSpec((tm, tk), lambda i, j, k: (i, k)), pl.BlockSpec((tn, tk), lambda i, j, k: (j, k))]
    else:
        in_specs = [pl.BlockSpec((tk, tm), lambda i, j, k: (k, i)), pl.BlockSpec((tk, tn), lambda i, j, k: (k, j))]
    args = [a, b]
    if has_res:
        in_specs.append(pl.BlockSpec((tm, tn), lambda i, j, k: (i, j)))
        args.append(res)
    if has_bias:
        in_specs.append(pl.BlockSpec((1, tn), lambda i, j, k: (0, j)))
        args.append(bias)
    return pl.pallas_call(
        body, name=name,
        out_shape=jax.ShapeDtypeStruct((M, N), out_dtype),
        grid=(M // tm, N // tn, nk),
        in_specs=in_specs,
        out_specs=pl.BlockSpec((tm, tn), lambda i, j, k: (i, j)),
        scratch_shapes=[pltpu.VMEM((tm, tn), F32)],
        compiler_params=_params(("parallel", "parallel", "arbitrary")),
    )(*args)


def _ln_fwd(z, g, b, name, tq=256):
    T, D = z.shape

    def body(z_ref, g_ref, b_ref, y_ref, yb_ref):
        zv = z_ref[...]
        mu = jnp.mean(zv, axis=1, keepdims=True)
        zc = zv - mu
        var = jnp.mean(zc * zc, axis=1, keepdims=True)
        y = zc * lax.rsqrt(var + LN_EPS) * g_ref[...] + b_ref[...]
        y_ref[...] = y
        yb_ref[...] = y.astype(BF16)

    row = pl.BlockSpec((tq, D), lambda i: (i, 0))
    vec = pl.BlockSpec((1, D), lambda i: (0, 0))
    return pl.pallas_call(
        body, name=name,
        out_shape=(jax.ShapeDtypeStruct((T, D), F32), jax.ShapeDtypeStruct((T, D), BF16)),
        grid=(T // tq,), in_specs=[row, vec, vec], out_specs=(row, row),
        compiler_params=_params(("parallel",)),
    )(z, g, b)


def _ln_bwd(dy, z, g, name, tq=256):
    T, D = z.shape

    def body(dy_ref, z_ref, g_ref, dz_ref, dzb_ref, dg_ref, db_ref, ds_ref):
        @pl.when(pl.program_id(0) == 0)
        def _():
            dg_ref[...] = jnp.zeros_like(dg_ref)
            db_ref[...] = jnp.zeros_like(db_ref)
            ds_ref[...] = jnp.zeros_like(ds_ref)

        zv, dyv = z_ref[...], dy_ref[...]
        mu = jnp.mean(zv, axis=1, keepdims=True)
        zc = zv - mu
        var = jnp.mean(zc * zc, axis=1, keepdims=True)
        rstd = lax.rsqrt(var + LN_EPS)
        xh = zc * rstd
        u = dyv * g_ref[...]
        dz = rstd * (u - jnp.mean(u, axis=1, keepdims=True) - xh * jnp.mean(u * xh, axis=1, keepdims=True))
        dz_ref[...] = dz
        dzb_ref[...] = dz.astype(BF16)
        dg_ref[...] += jnp.sum(dyv * xh, axis=0, keepdims=True)
        db_ref[...] += jnp.sum(dyv, axis=0, keepdims=True)
        ds_ref[...] += jnp.sum(dz, axis=0, keepdims=True)

    row = pl.BlockSpec((tq, D), lambda i: (i, 0))
    vec = pl.BlockSpec((1, D), lambda i: (0, 0))
    vshape = jax.ShapeDtypeStruct((1, D), F32)
    return pl.pallas_call(
        body, name=name,
        out_shape=(jax.ShapeDtypeStruct((T, D), F32), jax.ShapeDtypeStruct((T, D), BF16), vshape, vshape, vshape),
        grid=(T // tq,), in_specs=[row, row, vec], out_specs=(row, row, vec, vec, vec),
        compiler_params=_params(("arbitrary",)),
    )(dy, z, g)


def _loss_head(y, target, name, tq=256):
    T, D = y.shape

    def body(y_ref, t_ref, s_ref, dy_ref):
        @pl.when(pl.program_id(0) == 0)
        def _():
            s_ref[...] = jnp.zeros_like(s_ref)

        err = y_ref[...] - t_ref[...]
        s_ref[...] += jnp.sum(err * err)
        dy_ref[...] = err * (1.0 / D)

    row = pl.BlockSpec((tq, D), lambda i: (i, 0))
    acc = pl.BlockSpec((8, 128), lambda i: (0, 0))
    return pl.pallas_call(
        body, name=name,
        out_shape=(jax.ShapeDtypeStruct((8, 128), F32), jax.ShapeDtypeStruct((T, D), F32)),
        grid=(T // tq,), in_specs=[row, row], out_specs=(acc, row),
        compiler_params=_params(("arbitrary",)),
    )(y, target)


def _pblock(tq, width, offset):
    assert offset % width == 0
    blk = offset // width
    return pl.BlockSpec((tq, width), lambda i: (i, blk))


def _mix_fwd(proj, q_g, kv_g, w_pool, pool_scale, conv_w, name, tq=256):
    T = proj.shape[0]

    def body(ql_ref, kvl_ref, pin_ref, gp_ref, ch_ref, cb_ref, cc_ref, gc_ref, qg_ref, kvg_ref, wp_ref, ps_ref,
             cw_ref, qn_ref, kvn_ref, pooled_ref, cv_ref, ypc_ref, extp, extu):
        i = pl.program_id(0)
        for x_ref, g_ref, o_ref in ((ql_ref, qg_ref, qn_ref), (kvl_ref, kvg_ref, kvn_ref)):
            x = x_ref[...]
            r = lax.rsqrt(jnp.mean(x * x, axis=1, keepdims=True) + RMS_EPS)
            o_ref[...] = (x * r * g_ref[...]).astype(BF16)

        @pl.when(i == 0)
        def _():
            extp[0:HALO, :] = jnp.zeros((HALO, D_POOL), F32)
            extu[0:HALO, :] = jnp.zeros((HALO, D_CONV), F32)

        @pl.when(i > 0)
        def _():
            extp[0:HALO, :] = extp[tq:tq + HALO, :]
            extu[0:HALO, :] = extu[tq:tq + HALO, :]

        pin = pin_ref[...]
        extp[HALO:, :] = pin
        u = cc_ref[...] * ch_ref[...]
        extu[HALO:, :] = u
        t1 = (i * tq + lax.broadcasted_iota(jnp.int32, (tq, 1), 0) + 1).astype(F32)
        for g, w in enumerate(POOL_WINDOWS):
            cols = slice(g * POOL_GROUP, (g + 1) * POOL_GROUP)
            s = extp[:, cols]
            k = 1
            while k < w:
                s = s + pltpu.roll(s, k, 0)
                k *= 2
            mean = s[HALO:, :] / jnp.minimum(t1, float(w))
            pooled = (mean - pin[:, cols]).astype(BF16)
            pooled_ref[:, cols] = pooled
            r = jnp.dot(pooled, wp_ref[g], preferred_element_type=F32)
            gp = gp_ref[:, cols]
            ypc_ref[:, cols] = (r * ps_ref[:, cols] * (gp * _sigmoid(gp))).astype(BF16)
        eu = extu[...]
        u1 = pltpu.roll(eu, 1, 0)[HALO:, :]
        u2 = pltpu.roll(eu, 2, 0)[HALO:, :]
        cv = cw_ref[0:1, :] * u2 + cw_ref[1:2, :] * u1 + cw_ref[2:3, :] * u
        cv_ref[...] = cv
        gc = gc_ref[...]
        ypc_ref[:, D_POOL:] = (cb_ref[...] * cv * (gc * _sigmoid(gc))).astype(BF16)

    full = lambda shape: pl.BlockSpec(shape, lambda i: (0,) * len(shape))
    row = lambda w: pl.BlockSpec((tq, w), lambda i: (i, 0))
    return pl.pallas_call(
        body, name=name,
        out_shape=(jax.ShapeDtypeStruct((T, Q_LORA), BF16), jax.ShapeDtypeStruct((T, KV_LORA), BF16),
                   jax.ShapeDtypeStruct((T, D_POOL), BF16), jax.ShapeDtypeStruct((T, D_CONV), F32),
                   jax.ShapeDtypeStruct((T, D_POOL + D_CONV), BF16)),
        grid=(T // tq,),
        in_specs=[_pblock(tq, Q_LORA, O_QLAT), _pblock(tq, KV_LORA, O_KVLAT), _pblock(tq, 512, O_PIN),
                  _pblock(tq, 512, O_GPOOL), _pblock(tq, 512, O_CH), _pblock(tq, 512, O_CB), _pblock(tq, 512, O_CC),
                  _pblock(tq, 512, O_GCONV), full((1, Q_LORA)), full((1, KV_LORA)), full((4, 128, 128)),
                  full((1, D_POOL)), full((8, D_CONV))],
        out_specs=(row(Q_LORA), row(KV_LORA), row(D_POOL), row(D_CONV), row(D_POOL + D_CONV)),
        scratch_shapes=[pltpu.VMEM((tq + HALO, D_POOL), F32), pltpu.VMEM((tq + HALO, D_CONV), F32)],
        compiler_params=_params(("arbitrary",)),
    )(proj, proj, proj, proj, proj, proj, proj, proj, q_g, kv_g, w_pool, pool_scale, conv_w)


def _mix_bwd(dmix, proj, o, pooled, cv, w_pool, pool_scale, conv_w, name, tq=ATT_CH):
    T = proj.shape[0]
    nt = T // tq
    n_ext = tq + HALO

    def body(dym_ref, dyp_ref, dyc_ref, gm_ref, gp_ref, ch_ref, cb_ref, cc_ref, gc_ref, o_ref, pooled_ref, cv_ref,
             wp_ref, ps_ref, cw_ref, do_ref, delta_ref, dgm_ref, dmid_ref, dwp_ref, dps_ref, dcw_ref, exte, extd):
        i = pl.program_id(0)
        tile = nt - 1 - i

        @pl.when(i == 0)
        def _():
            dwp_ref[...] = jnp.zeros_like(dwp_ref)
            dps_ref[...] = jnp.zeros_like(dps_ref)
            dcw_ref[...] = jnp.zeros_like(dcw_ref)
            exte[tq:, :] = jnp.zeros((HALO, D_POOL), F32)
            extd[tq:, :] = jnp.zeros((HALO, D_CONV), F32)

        @pl.when(i > 0)
        def _():
            exte[tq:, :] = exte[0:HALO, :]
            extd[tq:, :] = extd[0:HALO, :]

        gm = gm_ref[...]
        sig = _sigmoid(gm)
        dym = dym_ref[...]
        ov = o_ref[...]
        do = dym * (gm * sig)
        do_ref[...] = do.astype(BF16)
        prod = do * ov
        ones = jnp.ones((8, V_DIM), F32)
        for h in range(N_HEADS):
            rows = lax.dot_general(ones, prod[:, h * V_DIM:(h + 1) * V_DIM], _DIMS["nt"],
                                   precision=lax.Precision.HIGHEST, preferred_element_type=F32)
            delta_ref[h, 0] = rows[0:1, :]
        dgm_ref[...] = (dym * ov * (sig * (1.0 + gm * (1.0 - sig)))).astype(BF16)

        t1 = (tile * tq + lax.broadcasted_iota(jnp.int32, (tq, 1), 0) + 1).astype(F32)
        for g, w in enumerate(POOL_WINDOWS):
            cols = slice(g * POOL_GROUP, (g + 1) * POOL_GROUP)
            pg = pooled_ref[:, cols]
            r = jnp.dot(pg, wp_ref[g], preferred_element_type=F32)
            gp = gp_ref[:, cols]
            sg = _sigmoid(gp)
            sl = gp * sg
            dyg = dyp_ref[:, cols]
            ps = ps_ref[:, cols]
            dmid_ref[:, 512 + g * POOL_GROUP:512 + (g + 1) * POOL_GROUP] = (
                dyg * (r * ps) * (sg * (1.0 + gp * (1.0 - sg)))).astype(BF16)
            dps_ref[:, cols] += jnp.sum(dyg * r * sl, axis=0, keepdims=True)
            dr = (dyg * ps * sl).astype(BF16)
            dwp_ref[g] += lax.dot_general(pg, dr, _DIMS["tn"], preferred_element_type=F32)
            dpooled = lax.dot_general(dr, wp_ref[g], _DIMS["nt"], preferred_element_type=F32)
            exte[0:tq, cols] = dpooled / jnp.minimum(t1, float(w))
            s = exte[:, cols]
            k = 1
            while k < w:
                s = s + pltpu.roll(s, n_ext - k, 0)
                k *= 2
            dmid_ref[:, cols] = (s[0:tq, :] - dpooled).astype(BF16)

        gc = gc_ref[...]
        sg = _sigmoid(gc)
        sl = gc * sg
        dyc = dyc_ref[...]
        cb, cc, ch, cvv = cb_ref[...], cc_ref[...], ch_ref[...], cv_ref[...]
        dcv = dyc * cb * sl
        dmid_ref[:, 2560:3072] = (dyc * (cb * cvv) * (sg * (1.0 + gc * (1.0 - sg)))).astype(BF16)
        dmid_ref[:, 1536:2048] = (dyc * cvv * sl).astype(BF16)
        extd[0:tq, :] = dcv
        ed = extd[...]
        d1 = pltpu.roll(ed, n_ext - 1, 0)[0:tq, :]
        d2 = pltpu.roll(ed, n_ext - 2, 0)[0:tq, :]
        du = cw_ref[2:3, :] * dcv + cw_ref[1:2, :] * d1 + cw_ref[0:1, :] * d2
        u = cc * ch
        dcw_ref[0:1, :] += jnp.sum(u * d2, axis=0, keepdims=True)
        dcw_ref[1:2, :] += jnp.sum(u * d1, axis=0, keepdims=True)
        dcw_ref[2:3, :] += jnp.sum(u * dcv, axis=0, keepdims=True)
        dmid_ref[:, 1024:1536] = (du * cc).astype(BF16)
        dmid_ref[:, 2048:2560] = (du * ch).astype(BF16)

    def rblock(width, offset):
        assert offset % width == 0
        blk = offset // width
        return pl.BlockSpec((tq, width), lambda i: (nt - 1 - i, blk))

    full = lambda shape: pl.BlockSpec(shape, lambda i: (0,) * len(shape))
    return pl.pallas_call(
        body, name=name,
        out_shape=(jax.ShapeDtypeStruct((T, D_MLA), BF16), jax.ShapeDtypeStruct((N_HEADS, nt, 1, tq), F32),
                   jax.ShapeDtypeStruct((T, D_MLA), BF16), jax.ShapeDtypeStruct((T, 3072), BF16),
                   jax.ShapeDtypeStruct((4, 128, 128), F32), jax.ShapeDtypeStruct((1, D_POOL), F32),
                   jax.ShapeDtypeStruct((8, D_CONV), F32)),
        grid=(nt,),
        in_specs=[rblock(1024, 0), rblock(512, 1024), rblock(512, 1536),
                  rblock(1024, O_GMLA), rblock(512, O_GPOOL), rblock(512, O_CH), rblock(512, O_CB),
                  rblock(512, O_CC), rblock(512, O_GCONV), rblock(1024, 0), rblock(512, 0), rblock(512, 0),
                  full((4, 128, 128)), full((1, D_POOL)), full((8, D_CONV))],
        out_specs=(rblock(1024, 0), pl.BlockSpec((N_HEADS, 1, 1, tq), lambda i: (0, nt - 1 - i, 0, 0)),
                   rblock(1024, 0),
                   rblock(3072, 0), full((4, 128, 128)), full((1, D_POOL)), full((8, D_CONV))),
        scratch_shapes=[pltpu.VMEM((n_ext, D_POOL), F32), pltpu.VMEM((n_ext, D_CONV), F32)],
        compiler_params=_params(("arbitrary",)),
    )(dmix, dmix, dmix, proj, proj, proj, proj, proj, proj, o, pooled, cv, w_pool, pool_scale, conv_w)


def _rms_bwd(proj, dqn, dkvn, q_g, kv_g, name, tq=256):
    T = proj.shape[0]

    def body(ql_ref, kvl_ref, dqn_ref, dkvn_ref, qg_ref, kvg_ref, dql_ref, dkvl_ref, dqg_ref, dkvg_ref):
        @pl.when(pl.program_id(0) == 0)
        def _():
            dqg_ref[...] = jnp.zeros_like(dqg_ref)
            dkvg_ref[...] = jnp.zeros_like(dkvg_ref)

        for x_ref, dy_ref, g_ref, dx_ref, dg_ref in ((ql_ref, dqn_ref, qg_ref, dql_ref, dqg_ref),
                                                     (kvl_ref, dkvn_ref, kvg_ref, dkvl_ref, dkvg_ref)):
            x, dy = x_ref[...], dy_ref[...]
            r = lax.rsqrt(jnp.mean(x * x, axis=1, keepdims=True) + RMS_EPS)
            xr = x * r
            u = dy * g_ref[...]
            dx_ref[...] = (r * (u - xr * jnp.mean(u * xr, axis=1, keepdims=True))).astype(BF16)
            dg_ref[...] += jnp.sum(dy * xr, axis=0, keepdims=True)

    row = lambda w: pl.BlockSpec((tq, w), lambda i: (i, 0))
    vec = lambda w: pl.BlockSpec((1, w), lambda i: (0, 0))
    return pl.pallas_call(
        body, name=name,
        out_shape=(jax.ShapeDtypeStruct((T, Q_LORA), BF16), jax.ShapeDtypeStruct((T, KV_LORA), BF16),
                   jax.ShapeDtypeStruct((1, Q_LORA), F32), jax.ShapeDtypeStruct((1, KV_LORA), F32)),
        grid=(T // tq,),
        in_specs=[_pblock(tq, Q_LORA, O_QLAT), _pblock(tq, KV_LORA, O_KVLAT), row(Q_LORA), row(KV_LORA),
                  vec(Q_LORA), vec(KV_LORA)],
        out_specs=(row(Q_LORA), row(KV_LORA), vec(Q_LORA), vec(KV_LORA)),
        compiler_params=_params(("arbitrary",)),
    )(proj, proj, dqn, dkvn, q_g, kv_g)


def _swap_halves(x, lo):
    return jnp.where(lo, pltpu.roll(x, 96, 1), pltpu.roll(x, 32, 1))


def _rope_fwd(q, kv, proj, cos_t, sin_t, name, tq=256):
    T = q.shape[0]

    def body(qn_ref, qr_ref, kv_ref, kr_ref, c_ref, s_ref, qc_ref, kc_ref):
        C, S = c_ref[...], s_ref[...]
        lane = lax.broadcasted_iota(jnp.int32, (tq, 128), 1)
        lo = (lane % ROPE) < (ROPE // 2)
        first = lane < ROPE

        def rope(x):
            return x * C + _swap_halves(x, lo) * S

        kr = jnp.where(first, rope(kr_ref[...]), 0.0).astype(BF16)
        for j in range(N_HEADS // 2):
            r = rope(qr_ref[:, j * 128:(j + 1) * 128])
            pair = (jnp.where(first, r, 0.0), jnp.where(first, pltpu.roll(r, 64, 1), 0.0))
            for hh in range(2):
                h = 2 * j + hh
                qc_ref[h, :, 0:NOPE] = qn_ref[:, h * NOPE:(h + 1) * NOPE].astype(BF16)
                qc_ref[h, :, NOPE:QC] = pair[hh].astype(BF16)
        for h in range(N_HEADS):
            kc_ref[h, :, 0:NOPE] = kv_ref[:, h * 256:h * 256 + NOPE]
            kc_ref[h, :, NOPE:QC] = kr

    out = jax.ShapeDtypeStruct((N_HEADS, T, QC), BF16)
    hblock = pl.BlockSpec((N_HEADS, tq, QC), lambda i: (0, i, 0))
    return pl.pallas_call(
        body, name=name, out_shape=(out, out), grid=(T // tq,),
        in_specs=[pl.BlockSpec((tq, 1024), lambda i: (i, 0)), pl.BlockSpec((tq, 512), lambda i: (i, 2)),
                  pl.BlockSpec((tq, 2048), lambda i: (i, 0)), _pblock(tq, 128, O_KROPE),
                  pl.BlockSpec((tq, 128), lambda i: (i, 0)), pl.BlockSpec((tq, 128), lambda i: (i, 0))],
        out_specs=(hblock, hblock),
        compiler_params=_params(("parallel",)),
    )(q, q, kv, proj, cos_t, sin_t)


def _rope_bwd(dqc, dkr, cos_t, sin_t, name, tq=256):
    T = dqc.shape[1]

    def body(dqc_ref, dkr_ref, c_ref, s_ref, dq_ref, dk_ref):
        C, S = c_ref[...], s_ref[...]
        lane = lax.broadcasted_iota(jnp.int32, (tq, 128), 1)
        lo = (lane % ROPE) < (ROPE // 2)
        first = lane < ROPE

        def unrope(dy):
            return dy * C - _swap_halves(dy, lo) * S

        acc = dkr_ref[0]
        for h in range(1, N_HEADS):
            acc = acc + dkr_ref[h]
        dk_ref[:, 0:128] = jnp.where(first, unrope(acc), 0.0).astype(BF16)
        dk_ref[:, 128:256] = jnp.zeros((tq, 128), BF16)
        for j in range(N_HEADS // 2):
            d0 = dqc_ref[2 * j, :, NOPE:QC]
            d1 = dqc_ref[2 * j + 1, :, NOPE:QC]
            comb = jnp.where(first, d0, pltpu.roll(d1, 64, 1))
            dq_ref[:, 1024 + j * 128:1024 + (j + 1) * 128] = unrope(comb).astype(BF16)
        for h in range(N_HEADS):
            dq_ref[:, h * NOPE:(h + 1) * NOPE] = dqc_ref[h, :, 0:NOPE].astype(BF16)

    tab = pl.BlockSpec((tq, 128), lambda i: (i, 0))
    return pl.pallas_call(
        body, name=name,
        out_shape=(jax.ShapeDtypeStruct((T, 1536), BF16), jax.ShapeDtypeStruct((T, 256), BF16)),
        grid=(T // tq,),
        in_specs=[pl.BlockSpec((N_HEADS, tq, QC), lambda i: (0, i, 0)),
                  pl.BlockSpec((N_HEADS, tq, 128), lambda i: (0, i, 0)), tab, tab],
        out_specs=(pl.BlockSpec((tq, 1536), lambda i: (i, 0)), pl.BlockSpec((tq, 256), lambda i: (i, 0))),
        compiler_params=_params(("parallel",)),
    )(dqc, dkr, cos_t, sin_t)


def _flash_fwd(qc, kc, kv, proj, name):
    H, T, _ = qc.shape
    tq, ch = ATT_TILE, ATT_CH
    per = tq // ch
    nch = T // ch

    assert per == 2

    def body(q_ref, k_ref, v_ref, g_ref, o_ref, y_ref, lse_ref, vt_sc, s_sc, acc_sc):
        i = pl.program_id(1)

        @pl.when(i == 0)
        def _():
            for c in range(nch):
                vt_sc[c] = v_ref[c * ch:(c + 1) * ch, :].astype(F32).T.astype(BF16)

        q = q_ref[0]

        def issue(c, slot):
            s_sc[slot] = lax.dot_general(k_ref[0, pl.ds(pl.multiple_of(c * ch, ch), ch), :], q, _DIMS["nt"],
                                         preferred_element_type=F32)

        def softmax_pv(c, slot, m, l, masked):
            s = s_sc[slot]
            if masked:
                krow = c * ch + lax.broadcasted_iota(jnp.int32, s.shape, 0)
                qcol = i * tq + lax.broadcasted_iota(jnp.int32, s.shape, 1)
                s = jnp.where(krow <= qcol, s, -jnp.inf)
            m_new = jnp.maximum(m, jnp.max(s, axis=0, keepdims=True))
            p = jnp.exp2((s - m_new) * EXP2_SCALE)
            a = jnp.exp2((m - m_new) * EXP2_SCALE)
            l = a * l + jnp.sum(p, axis=0, keepdims=True)
            acc_sc[...] = a * acc_sc[...] + jnp.dot(vt_sc[c], p.astype(BF16), preferred_element_type=F32)
            return m_new, l

        def pair(t, carry):
            issue(2 * t + 1, 1)
            carry = softmax_pv(2 * t, 0, *carry, False)
            issue(2 * t + 2, 0)
            return softmax_pv(2 * t + 1, 1, *carry, False)

        acc_sc[...] = jnp.zeros_like(acc_sc)
        issue(0, 0)
        carry = lax.fori_loop(0, i, pair, (jnp.full((1, tq), -jnp.inf, F32), jnp.zeros((1, tq), F32)))
        issue(2 * i + 1, 1)
        carry = softmax_pv(2 * i, 0, *carry, True)
        m, l = softmax_pv(2 * i + 1, 1, *carry, True)
        o = (acc_sc[...] / l).T
        o_ref[...] = o
        lse = m * ATTN_SCALE + jnp.log(l)
        for r in range(per):
            lse_ref[0, r] = lse[:, r * ch:(r + 1) * ch]
        g = g_ref[...]
        y_ref[...] = (o * (g * _sigmoid(g))).astype(BF16)

    return pl.pallas_call(
        body, name=name,
        out_shape=(jax.ShapeDtypeStruct((T, D_MLA), F32), jax.ShapeDtypeStruct((T, D_MLA), BF16),
                   jax.ShapeDtypeStruct((H, nch, 1, ch), F32)),
        grid=(H, T // tq),
        in_specs=[pl.BlockSpec((1, tq, QC), lambda h, i: (h, i, 0)),
                  pl.BlockSpec((1, T, QC), lambda h, i: (h, 0, 0)),
                  pl.BlockSpec((T, V_DIM), lambda h, i: (0, 2 * h + 1)),
                  pl.BlockSpec((tq, V_DIM), lambda h, i: (i, h))],
        out_specs=(pl.BlockSpec((tq, V_DIM), lambda h, i: (i, h)),
                   pl.BlockSpec((tq, V_DIM), lambda h, i: (i, h)),
                   pl.BlockSpec((1, per, 1, ch), lambda h, i: (h, i, 0, 0))),
        scratch_shapes=[pltpu.VMEM((nch, V_DIM, ch), BF16), pltpu.VMEM((2, ch, tq), F32),
                        pltpu.VMEM((V_DIM, tq), F32)],
        compiler_params=_params(("parallel", "arbitrary")),
    )(qc, kc, kv, proj)


def _flash_bwd(qc, kc, kv, do, lse, delta, name):
    H, T, _ = qc.shape
    tk, ch = ATT_TILE, ATT_CH
    per = tk // ch
    assert per == 2
    nk, nch = T // tk, T // ch

    def body(q_ref, k_ref, v_ref, do_ref, lse_ref, dl_ref, dq_ref, dkv_ref, dkr_ref, dqt_sc, dk_sc, dv_sc, s_sc,
             dp_sc):
        j = pl.program_id(1)

        @pl.when(j == 0)
        def _():
            dqt_sc[...] = jnp.zeros_like(dqt_sc)

        dk_sc[...] = jnp.zeros_like(dk_sc)
        dv_sc[...] = jnp.zeros_like(dv_sc)
        k = k_ref[0]
        v = v_ref[...]
        kt = k.astype(F32).T.astype(BF16)

        def operands(c):
            q0 = pl.multiple_of(c * ch, ch)
            return q_ref[0, pl.ds(q0, ch), :], do_ref[pl.ds(q0, ch), :]

        def early(c, slot):
            q, dov = operands(c)
            s_sc[slot] = lax.dot_general(k, q, _DIMS["nt"], preferred_element_type=F32)
            dp_sc[slot] = lax.dot_general(v, dov, _DIMS["nt"], preferred_element_type=F32)

        def late(c, slot, masked):
            q, dov = operands(c)
            s, dp = s_sc[slot], dp_sc[slot]
            if masked:
                krow = j * tk + lax.broadcasted_iota(jnp.int32, s.shape, 0)
                qcol = c * ch + lax.broadcasted_iota(jnp.int32, s.shape, 1)
                s = jnp.where(krow <= qcol, s, -jnp.inf)
            p = jnp.exp2(s * EXP2_SCALE - lse_ref[0, c] * LOG2E)
            ds = (p * (dp - dl_ref[0, c]) * ATTN_SCALE).astype(BF16)
            dv_sc[...] += jnp.dot(p.astype(BF16), dov, preferred_element_type=F32)
            dk_sc[...] += jnp.dot(ds, q, preferred_element_type=F32)
            dqt_sc[c] += jnp.dot(kt, ds, preferred_element_type=F32)

        c0 = per * j
        early(c0, 0)
        early(c0 + 1, 1)
        late(c0, 0, True)

        @pl.when(j < nk - 1)
        def _():
            early(c0 + 2, 0)

        late(c0 + 1, 1, True)

        def pair(t, carry):
            a = 2 * t
            early(a + 1, 1)
            late(a, 0, False)

            @pl.when(t < nk - 1)
            def _():
                early(a + 2, 0)

            late(a + 1, 1, False)
            return carry

        lax.fori_loop(j + 1, nk, pair, 0)
        dk = dk_sc[...]
        dkv_ref[:, 0:NOPE] = dk[:, 0:NOPE].astype(BF16)
        dkv_ref[:, NOPE:] = dv_sc[...].astype(BF16)
        dkr_ref[0] = dk[:, NOPE:]

        @pl.when(j == nk - 1)
        def _():
            for c in range(nch):
                dq_ref[0, c * ch:(c + 1) * ch, :] = dqt_sc[c].T

    head = lambda h, j: (h, 0, 0)
    stat = pl.BlockSpec((1, nch, 1, ch), lambda h, j: (h, 0, 0, 0))
    return pl.pallas_call(
        body, name=name,
        out_shape=(jax.ShapeDtypeStruct((H, T, QC), F32), jax.ShapeDtypeStruct((T, 2 * D_MLA), BF16),
                   jax.ShapeDtypeStruct((H, T, 128), F32)),
        grid=(H, nk),
        in_specs=[pl.BlockSpec((1, T, QC), head),
                  pl.BlockSpec((1, tk, QC), lambda h, j: (h, j, 0)),
                  pl.BlockSpec((tk, V_DIM), lambda h, j: (j, 2 * h + 1)),
                  pl.BlockSpec((T, V_DIM), lambda h, j: (0, h)),
                  stat, stat],
        out_specs=(pl.BlockSpec((1, T, QC), head),
                   pl.BlockSpec((tk, 256), lambda h, j: (j, h)),
                   pl.BlockSpec((1, tk, 128), lambda h, j: (h, j, 0))),
        scratch_shapes=[pltpu.VMEM((nch, QC, ch), F32), pltpu.VMEM((tk, QC), F32), pltpu.VMEM((tk, V_DIM), F32),
                        pltpu.VMEM((2, tk, ch), F32), pltpu.VMEM((2, tk, ch), F32)],
        compiler_params=_params(("parallel", "arbitrary")),
    )(qc, kc, kv, do, lse, delta)


def _adamw(land, w, m, v, name, rows):
    R, C = w.shape
    assert R % rows == 0
    c1 = 1.0 - ADAM_B1 ** ADAM_STEP
    c2 = 1.0 - ADAM_B2 ** ADAM_STEP

    def body(land_ref, w_ref, m_ref, v_ref, g_ref, d_ref, nm_ref, nv_ref):
        g = land_ref[0]
        for s in range(1, N_DEV):
            g = g + land_ref[s]
        nm = ADAM_B1 * m_ref[...] + (1.0 - ADAM_B1) * g
        nv = ADAM_B2 * v_ref[...] + (1.0 - ADAM_B2) * (g * g)
        g_ref[...] = g
        nm_ref[...] = nm
        nv_ref[...] = nv
        d_ref[...] = -ADAM_LR * ((nm / c1) / (jnp.sqrt(nv / c2) + ADAM_EPS) + ADAM_WD * w_ref[...])

    blk = pl.BlockSpec((rows, C), lambda i: (i, 0))
    out = jax.ShapeDtypeStruct((R, C), F32)
    return pl.pallas_call(
        body, name=name, out_shape=(out, out, out, out), grid=(R // rows,),
        in_specs=[pl.BlockSpec((N_DEV, rows, C), lambda i: (0, i, 0)), blk, blk, blk],
        out_specs=(blk, blk, blk, blk),
        compiler_params=_params(("parallel",)),
    )(land, w, m, v)


def _mesh_pos():
    return lax.axis_index("x"), lax.axis_index("y"), lax.axis_index("c")


def _all_gather(arrays, name):
    n = len(arrays)

    def body(*refs):
        ins, outs = refs[:n], refs[n:2 * n]
        send_sems, recv_sems, local_sems = refs[2 * n:]
        x, y, c = _mesh_pos()
        me, sibling = (x, y, c), (x, y, 1 - c)
        chips = [(1 - x, y), (x, 1 - y), (1 - x, 1 - y)]

        def slot(a, pos):
            px, py, pc = pos
            return outs[a].at[4 * px + 2 * py + pc]

        def copy(a, k, block, to, src=None):
            return pltpu.make_async_remote_copy(
                src_ref=slot(a, block) if src is None else src, dst_ref=slot(a, block),
                send_sem=send_sems.at[a * 7 + k], recv_sem=recv_sems.at[a * 7 + k],
                device_id=to, device_id_type=MESH_ID)

        mine, first, passed = [], [], []
        for a in range(n):
            cp = pltpu.make_async_copy(ins[a], slot(a, me), local_sems.at[a])
            cp.start()
            mine.append(cp)
            cps = [copy(a, 0, me, sibling, src=ins[a])]
            cps += [copy(a, 1 + j, me, (*chip, c), src=ins[a]) for j, chip in enumerate(chips)]
            for cp in cps:
                cp.start()
            first += cps
        for j, chip in enumerate(chips):
            for a in range(n):
                copy(a, 1 + j, (*chip, c), me).wait_recv()
                cp = copy(a, 4 + j, (*chip, c), sibling)
                cp.start()
                passed.append(cp)
        for a in range(n):
            copy(a, 0, sibling, me).wait_recv()
            for j, chip in enumerate(chips):
                copy(a, 4 + j, (*chip, 1 - c), me).wait_recv()
        for cp in first + passed:
            cp.wait_send()
        for cp in mine:
            cp.wait()

    hbm = pl.BlockSpec(memory_space=pltpu.HBM)
    return pl.pallas_call(
        body, name=name,
        out_shape=tuple(jax.ShapeDtypeStruct((N_DEV,) + a.shape, a.dtype) for a in arrays),
        in_specs=[hbm] * n, out_specs=tuple([hbm] * n),
        scratch_shapes=[pltpu.SemaphoreType.DMA((7 * n,)), pltpu.SemaphoreType.DMA((7 * n,)),
                        pltpu.SemaphoreType.DMA((n,))],
    )(*arrays)


def _exchange(arrays, name):
    n = len(arrays)

    def body(*refs):
        ins, outs = refs[:n], refs[n:2 * n]
        send_sems, recv_sems, local_sems = refs[2 * n:]
        x, y, c = _mesh_pos()
        my_idx = 4 * x + 2 * y + c
        copies, local = [], []
        for a in range(n):
            cp = pltpu.make_async_copy(ins[a].at[my_idx], outs[a].at[my_idx], local_sems.at[a])
            cp.start()
            local.append(cp)
            for k in range(1, N_DEV):
                px = 1 - x if k & 4 else x
                py = 1 - y if k & 2 else y
                pc = 1 - c if k & 1 else c
                cp = pltpu.make_async_remote_copy(
                    src_ref=ins[a].at[4 * px + 2 * py + pc], dst_ref=outs[a].at[my_idx],
                    send_sem=send_sems.at[a * 7 + k - 1], recv_sem=recv_sems.at[a * 7 + k - 1],
                    device_id=(px, py, pc), device_id_type=MESH_ID)
                cp.start()
                copies.append(cp)
        for cp in copies:
            cp.wait()
        for cp in local:
            cp.wait()

    hbm = pl.BlockSpec(memory_space=pltpu.HBM)
    return pl.pallas_call(
        body, name=name,
        out_shape=tuple(jax.ShapeDtypeStruct(a.shape, a.dtype) for a in arrays),
        in_specs=[hbm] * n, out_specs=tuple([hbm] * n),
        scratch_shapes=[pltpu.SemaphoreType.DMA((7 * n,)), pltpu.SemaphoreType.DMA((7 * n,)),
                        pltpu.SemaphoreType.DMA((n,))],
    )(*arrays)


def _cat_blocks(g, axis):
    return jnp.concatenate([g[d] for d in range(N_DEV)], axis=axis)


def _permute_w_in(w):
    q_lat, kv_lat, k_rope, rest = w[:, 0:512], w[:, 512:768], w[:, 768:832], w[:, 832:]
    g_mla, others = rest[:, 0:1024], rest[:, 1024:]
    pad = jnp.zeros((w.shape[0], NPP - D_IN_PROJ), w.dtype)
    return jnp.concatenate([g_mla, q_lat, others, kv_lat, k_rope, pad], axis=1)


def _unpermute_w_in(w):
    return jnp.concatenate([w[:, O_QLAT:O_PIN], w[:, O_KVLAT:O_KROPE], w[:, O_KROPE:O_KROPE + ROPE],
                            w[:, O_GMLA:O_QLAT], w[:, O_PIN:O_KVLAT]], axis=1)


def _permute_w_uq(w):
    w3 = w.reshape(w.shape[0], N_HEADS, NOPE + ROPE)
    return jnp.concatenate([w3[:, :, :NOPE].reshape(w.shape[0], -1), w3[:, :, NOPE:].reshape(w.shape[0], -1)], axis=1)


def _unpermute_w_uq(w):
    nope = w[:, :N_HEADS * NOPE].reshape(w.shape[0], N_HEADS, NOPE)
    rope = w[:, N_HEADS * NOPE:].reshape(w.shape[0], N_HEADS, ROPE)
    return jnp.concatenate([nope, rope], axis=2).reshape(w.shape[0], -1)


_SMALL = (("emb_ln_g", 16), ("emb_ln_b", 16), ("q_norm_g", 8), ("kv_norm_g", 8), ("w_pool", 1024),
          ("pool_scale", 8), ("b_out", 32), ("ln_g", 32), ("ln_b", 32))
SMALL_ROWS = sum(r for _, r in _SMALL)


def _pack_small(d):
    parts = []
    for name, rows in _SMALL:
        flat = d[name].reshape(-1)
        flat = jnp.pad(flat, (0, rows * 128 - flat.shape[0]))
        parts.append(flat.reshape(rows, 128))
    return jnp.concatenate(parts, axis=0)


def _unpack_small(packed, shapes):
    out, r0 = {}, 0
    for name, rows in _SMALL:
        size = 1
        for s in shapes[name]:
            size *= s
        out[name] = packed[r0:r0 + rows].reshape(-1)[:size].reshape(shapes[name])
        r0 += rows
    return out


def _rope_tables(positions):
    half = ROPE // 2
    inv_freq = ROPE_THETA ** (-jnp.arange(half, dtype=F32) / half)
    ang = positions.astype(F32)[:, None] * inv_freq
    cos, sin = jnp.cos(ang), jnp.sin(ang)
    return jnp.concatenate([cos, cos, cos, cos], axis=1), jnp.concatenate([-sin, sin, -sin, sin], axis=1)


def _local_step(x, positions, target, W):
    cos_t, sin_t = _rope_tables(positions)
    h, hb = _ln_fwd(x, W["emb_ln_g"], W["emb_ln_b"], "emb_ln_fwd")
    saved = []
    for l in range(DEPTH):
        proj = _mm(hb, W["w_in"][l], "nn", F32, "proj_fwd")
        qn, kvn, pooled, cv, ypc = _mix_fwd(proj, W["q_norm_g"][l], W["kv_norm_g"][l], W["w_pool"][l],
                                            W["pool_scale"][l], W["conv_w"][l], "mix_fwd")
        q = _mm(qn, W["w_uq"][l], "nn", F32, "q_up_fwd")
        kv = _mm(kvn, W["w_ukv"][l], "nn", BF16, "kv_up_fwd")
        qc, kc = _rope_fwd(q, kv, proj, cos_t, sin_t, "rope_fwd")
        o, ymla, lse = _flash_fwd(qc, kc, kv, proj, "flash_fwd")
        mix = jnp.concatenate([ymla, ypc], axis=1)
        z = _mm(mix, W["w_out"][l], "nn", F32, "out_fwd", res=h, bias=W["b_out"][l], alpha=ALPHA)
        saved.append((hb, proj, qn, kvn, pooled, cv, kv, qc, kc, o, lse, mix, z))
        h, hb = _ln_fwd(z, W["ln_g"][l], W["ln_b"][l], "ln_fwd")
    sq, dh = _loss_head(h, target, "loss_head")

    grads = {k: [None] * DEPTH for k in ("w_in", "q_norm_g", "kv_norm_g", "w_uq", "w_ukv", "w_pool", "pool_scale",
                                         "conv_w", "w_out", "b_out", "ln_g", "ln_b")}
    for l in reversed(range(DEPTH)):
        hb_in, proj, qn, kvn, pooled, cv, kv, qc, kc, o, lse, mix, z = saved[l]
        dz, dzb, grads["ln_g"][l], grads["ln_b"][l], grads["b_out"][l] = _ln_bwd(dh, z, W["ln_g"][l], "ln_bwd")
        dmix = _mm(dzb, W["w_out"][l], "nt", F32, "out_bwd_x")
        grads["w_out"][l] = _mm(mix, dzb, "tn", F32, "out_bwd_w")
        do, delta, dgm, dmid, grads["w_pool"][l], grads["pool_scale"][l], grads["conv_w"][l] = _mix_bwd(
            dmix, proj, o, pooled, cv, W["w_pool"][l], W["pool_scale"][l], W["conv_w"][l], "mix_bwd")
        dqc, dkv, dkr = _flash_bwd(qc, kc, kv, do, lse, delta, "flash_bwd")
        dq, dkrope = _rope_bwd(dqc, dkr, cos_t, sin_t, "rope_bwd")
        dqn = _mm(dq, W["w_uq"][l], "nt", F32, "q_up_bwd_x")
        grads["w_uq"][l] = _mm(qn, dq, "tn", F32, "q_up_bwd_w")
        dkvn = _mm(dkv, W["w_ukv"][l], "nt", F32, "kv_up_bwd_x")
        grads["w_ukv"][l] = _mm(kvn, dkv, "tn", F32, "kv_up_bwd_w")
        dql, dkvl, grads["q_norm_g"][l], grads["kv_norm_g"][l] = _rms_bwd(
            proj, dqn, dkvn, W["q_norm_g"][l], W["kv_norm_g"][l], "rms_bwd")
        dproj = jnp.concatenate([dgm, dql, dmid, dkvl, dkrope], axis=1)
        dh = _mm(dproj, W["w_in"][l], "nt", F32, "proj_bwd_x", res=dz, alpha=ALPHA)
        grads["w_in"][l] = _mm(hb_in, dproj, "tn", F32, "proj_bwd_w")
    grad_x, _, grads["emb_ln_g"], grads["emb_ln_b"], _ = _ln_bwd(dh, x, W["emb_ln_g"], "emb_ln_bwd")
    return sq, grad_x, grads


def kernel(x, positions, emb_ln_g, emb_ln_b, w_in, q_norm_g, kv_norm_g, w_uq, w_ukv, w_pool, pool_scale, conv_w, w_out, b_out, ln_g, ln_b, loss_target, m_emb_ln_g, m_emb_ln_b, m_w_in, m_q_norm_g, m_kv_norm_g, m_w_uq, m_w_ukv, m_w_pool, m_pool_scale, m_conv_w, m_w_out, m_b_out, m_ln_g, m_ln_b, v_emb_ln_g, v_emb_ln_b, v_w_in, v_q_norm_g, v_kv_norm_g, v_w_uq, v_w_ukv, v_w_pool, v_pool_scale, v_conv_w, v_w_out, v_b_out, v_ln_g, v_ln_b):
    weights = dict(emb_ln_g=emb_ln_g, emb_ln_b=emb_ln_b, w_in=w_in, q_norm_g=q_norm_g, kv_norm_g=kv_norm_g,
                   w_uq=w_uq, w_ukv=w_ukv, w_pool=w_pool, pool_scale=pool_scale, conv_w=conv_w, w_out=w_out,
                   b_out=b_out, ln_g=ln_g, ln_b=ln_b)
    mom1 = dict(emb_ln_g=m_emb_ln_g, emb_ln_b=m_emb_ln_b, w_in=m_w_in, q_norm_g=m_q_norm_g, kv_norm_g=m_kv_norm_g,
                w_uq=m_w_uq, w_ukv=m_w_ukv, w_pool=m_w_pool, pool_scale=m_pool_scale, conv_w=m_conv_w,
                w_out=m_w_out, b_out=m_b_out, ln_g=m_ln_g, ln_b=m_ln_b)
    mom2 = dict(emb_ln_g=v_emb_ln_g, emb_ln_b=v_emb_ln_b, w_in=v_w_in, q_norm_g=v_q_norm_g, kv_norm_g=v_kv_norm_g,
                w_uq=v_w_uq, w_ukv=v_w_ukv, w_pool=v_w_pool, pool_scale=v_pool_scale, conv_w=v_conv_w,
                w_out=v_w_out, b_out=v_b_out, ln_g=v_ln_g, ln_b=v_ln_b)

    conv_pad = jnp.zeros((8, 128), F32).at[0:DEPTH * CONV_WIDTH, 0:64].set(conv_w.reshape(DEPTH * CONV_WIDTH, 64))
    g_in, g_uq, g_ukv, g_out, g_conv = _all_gather(
        [w_in.astype(BF16), w_uq.astype(BF16), w_ukv.astype(BF16), w_out.astype(BF16), conv_pad], "weights_all_gather")
    conv_full = _cat_blocks(g_conv[:, 0:DEPTH * CONV_WIDTH, 0:64], 1).reshape(DEPTH, CONV_WIDTH, D_CONV)
    conv_full = jnp.pad(conv_full, ((0, 0), (0, 8 - CONV_WIDTH), (0, 0)))
    W = dict(
        emb_ln_g=emb_ln_g.reshape(1, -1), emb_ln_b=emb_ln_b.reshape(1, -1),
        w_in=[_permute_w_in(_cat_blocks(g_in[:, l], 1)) for l in range(DEPTH)],
        w_uq=[_permute_w_uq(_cat_blocks(g_uq[:, l], 1)) for l in range(DEPTH)],
        w_ukv=[_cat_blocks(g_ukv[:, l], 1) for l in range(DEPTH)],
        w_out=[_cat_blocks(g_out[:, l], 0) for l in range(DEPTH)],
        conv_w=[conv_full[l] for l in range(DEPTH)],
        q_norm_g=[q_norm_g[l].reshape(1, -1) for l in range(DEPTH)],
        kv_norm_g=[kv_norm_g[l].reshape(1, -1) for l in range(DEPTH)],
        w_pool=[w_pool[l].astype(BF16) for l in range(DEPTH)],
        pool_scale=[pool_scale[l].reshape(1, -1) for l in range(DEPTH)],
        b_out=[b_out[l].reshape(1, -1) for l in range(DEPTH)],
        ln_g=[ln_g[l].reshape(1, -1) for l in range(DEPTH)],
        ln_b=[ln_b[l].reshape(1, -1) for l in range(DEPTH)],
    )

    sq, grad_x, G = _local_step(x[0], positions[0], loss_target[0], W)
    loss = lax.psum(sq[0, 0] * (0.5 / D_MODEL), ("x", "y", "c"))

    d_in = jnp.stack([_unpermute_w_in(G["w_in"][l]) for l in range(DEPTH)])
    d_in = d_in.reshape(DEPTH, D_MODEL, N_DEV, -1).transpose(2, 0, 1, 3)
    d_uq = jnp.stack([_unpermute_w_uq(G["w_uq"][l]) for l in range(DEPTH)])
    d_uq = d_uq.reshape(DEPTH, Q_LORA, N_DEV, -1).transpose(2, 0, 1, 3)
    d_ukv = jnp.stack(G["w_ukv"]).reshape(DEPTH, KV_LORA, N_DEV, -1).transpose(2, 0, 1, 3)
    d_out = jnp.stack(G["w_out"]).reshape(DEPTH, N_DEV, -1, D_MODEL).transpose(1, 0, 2, 3)
    d_conv = jnp.stack([G["conv_w"][l][0:CONV_WIDTH] for l in range(DEPTH)])
    d_conv = d_conv.reshape(DEPTH * CONV_WIDTH, N_DEV, 64).transpose(1, 0, 2)
    d_conv = jnp.zeros((N_DEV, 8, 128), F32).at[:, 0:DEPTH * CONV_WIDTH, 0:64].set(d_conv)
    small = dict(emb_ln_g=G["emb_ln_g"], emb_ln_b=G["emb_ln_b"])
    for k in ("q_norm_g", "kv_norm_g", "w_pool", "pool_scale", "b_out", "ln_g", "ln_b"):
        small[k] = jnp.stack(G[k])
    d_small = jnp.broadcast_to(_pack_small(small)[None], (N_DEV, SMALL_ROWS, 128))
    l_in, l_uq, l_ukv, l_out, l_conv, l_small = _exchange([d_in, d_uq, d_ukv, d_out, d_conv, d_small],
                                                          "gradient_exchange")

    res = {}

    def adam(name, land, rows):
        shape = weights[name].shape
        r2 = lambda a: a.reshape(-1, shape[-1])
        outs = _adamw(land.reshape(N_DEV, -1, shape[-1]), r2(weights[name]), r2(mom1[name]), r2(mom2[name]),
                      "adamw_" + name, rows)
        res[name] = tuple(o.reshape(shape) for o in outs)

    adam("w_in", l_in, 256)
    adam("w_uq", l_uq, 256)
    adam("w_ukv", l_ukv, 256)
    adam("w_out", l_out, 128)
    conv_shard = lambda a: jnp.zeros((8, 128), F32).at[0:DEPTH * CONV_WIDTH, 0:64].set(a.reshape(-1, 64))
    conv_res = _adamw(l_conv, conv_shard(conv_w), conv_shard(m_conv_w), conv_shard(v_conv_w), "adamw_conv_w", 8)
    res["conv_w"] = tuple(o[0:DEPTH * CONV_WIDTH, 0:64].reshape(DEPTH, CONV_WIDTH, 64) for o in conv_res)
    small_res = _adamw(l_small, _pack_small(weights), _pack_small(mom1), _pack_small(mom2), "adamw_small", 392)
    shapes = {k: weights[k].shape for k, _ in _SMALL}
    unpacked = [_unpack_small(o, shapes) for o in small_res]
    for k, _ in _SMALL:
        res[k] = tuple(u[k] for u in unpacked)

    order = ("emb_ln_g", "emb_ln_b", "w_in", "q_norm_g", "kv_norm_g", "w_uq", "w_ukv", "w_pool", "pool_scale",
             "conv_w", "w_out", "b_out", "ln_g", "ln_b")
    return (loss, grad_x[None], *[res[k][0] for k in order], *[res[k][1] for k in order],
            *[res[k][2] for k in order], *[res[k][3] for k in order])
```

```python
import jax
import jax.numpy as jnp
from jax import lax
from jax.experimental import pallas as pl
from jax.experimental.pallas import tpu as pltpu

F32 = jnp.float32
BF16 = jnp.bfloat16

N_DEV = 8
D_MODEL = 2048
DEPTH = 2
N_HEADS = 8
NOPE = 128
ROPE = 64
V_DIM = 128
Q_LORA = 512
KV_LORA = 256
D_MLA = N_HEADS * V_DIM
D_POOL = 512
D_CONV = 512
POOL_WINDOWS = (2, 4, 8, 16)
POOL_GROUP = 128
CONV_WIDTH = 3
D_MIX = D_MLA + D_POOL + D_CONV
D_IN_PROJ = 4928
ROPE_THETA = 10000.0
LN_EPS = 1e-5
RMS_EPS = 1e-6
ALPHA = (2 * DEPTH) ** 0.25
ATTN_SCALE = (NOPE + ROPE) ** -0.5
ADAM_LR = 0.001
ADAM_B1 = 0.9
ADAM_B2 = 0.999
ADAM_EPS = 1e-08
ADAM_WD = 0.01
ADAM_STEP = 10

O_GMLA, O_QLAT, O_PIN, O_GPOOL, O_CH, O_CB, O_CC, O_GCONV, O_KVLAT, O_KROPE = (
    0, 1024, 1536, 2048, 2560, 3072, 3584, 4096, 4608, 4864)
NPP = 5120
QC = NOPE + 2 * ROPE
HALO = 16
ATT_TILE = 512
ATT_CH = 256
LOG2E = 1.4426950408889634
EXP2_SCALE = ATTN_SCALE * LOG2E

GRAD_XFER = BF16
VMEM_LIMIT = 48 * 1024 * 1024
MESH_ID = pl.DeviceIdType.MESH


def _params(sem=None):
    return pltpu.CompilerParams(dimension_semantics=sem, vmem_limit_bytes=VMEM_LIMIT)


def _sigmoid(x):
    return 1.0 / (1.0 + jnp.exp(-x))


def _tile(dim, target):
    if dim <= target:
        return dim
    t = target - target % 128
    while dim % t:
        t -= 128
    return t


_DIMS = {"nn": (((1,), (0,)), ((), ())), "nt": (((1,), (1,)), ((), ())), "tn": (((0,), (0,)), ((), ()))}


def _mm(a, b, mode, out_dtype, name, res=None, bias=None, alpha=1.0, tm=1024, tn=1024, tk=512):
    if mode == "nn":
        (M, K), (K2, N) = a.shape, b.shape
    elif mode == "nt":
        (M, K), (N, K2) = a.shape, b.shape
    else:
        (K, M), (K2, N) = a.shape, b.shape
    assert K == K2
    tm, tn, tk = _tile(M, tm), _tile(N, tn), _tile(K, tk)
    nk = K // tk
    has_res, has_bias = res is not None, bias is not None

    def body(*refs):
        a_ref, b_ref = refs[0], refs[1]
        pos = 2
        res_ref = bias_ref = None
        if has_res:
            res_ref = refs[pos]
            pos += 1
        if has_bias:
            bias_ref = refs[pos]
            pos += 1
        o_ref, acc_ref = refs[pos], refs[pos + 1]
        k = pl.program_id(2)

        @pl.when(k == 0)
        def _():
            acc_ref[...] = jnp.zeros_like(acc_ref)

        acc_ref[...] += lax.dot_general(a_ref[...].astype(BF16), b_ref[...].astype(BF16), _DIMS[mode],
                                        preferred_element_type=F32)

        @pl.when(k == nk - 1)
        def _():
            r = acc_ref[...]
            if has_bias:
                r = r + bias_ref[...]
            if has_res:
                r = alpha * res_ref[...] + r
            o_ref[...] = r.astype(out_dtype)

    if mode == "nn":
        in_specs = [pl.BlockSpec((tm, tk), lambda i, j, k: (i, k)), pl.BlockSpec((tk, tn), lambda i, j, k: (k, j))]
    elif mode == "nt":
        in_specs = [pl.BlockSpec((tm, tk), lambda i, j, k: (i, k)), pl.BlockSpec((tn, tk), lambda i, j, k: (j, k))]
    else:
        in_specs = [pl.BlockSpec((tk, tm), lambda i, j, k: (k, i)), pl.BlockSpec((tk, tn), lambda i, j, k: (k, j))]
    args = [a, b]
    if has_res:
        in_specs.append(pl.BlockSpec((tm, tn), lambda i, j, k: (i, j)))
        args.append(res)
    if has_bias:
        in_specs.append(pl.BlockSpec((1, tn), lambda i, j, k: (0, j)))
        args.append(bias)
    return pl.pallas_call(
        body, name=name,
        out_shape=jax.ShapeDtypeStruct((M, N), out_dtype),
        grid=(M // tm, N // tn, nk),
        in_specs=in_specs,
        out_specs=pl.BlockSpec((tm, tn), lambda i, j, k: (i, j)),
        scratch_shapes=[pltpu.VMEM((tm, tn), F32)],
        compiler_params=_params(("parallel", "parallel", "arbitrary")),
    )(*args)


def _ln_fwd(z, g, b, name, tq=256):
    T, D = z.shape

    def body(z_ref, g_ref, b_ref, y_ref, yb_ref):
        zv = z_ref[...]
        mu = jnp.mean(zv, axis=1, keepdims=True)
        zc = zv - mu
        var = jnp.mean(zc * zc, axis=1, keepdims=True)
        y = zc * lax.rsqrt(var + LN_EPS) * g_ref[...] + b_ref[...]
        y_ref[...] = y
        yb_ref[...] = y.astype(BF16)

    row = pl.BlockSpec((tq, D), lambda i: (i, 0))
    vec = pl.BlockSpec((1, D), lambda i: (0, 0))
    return pl.pallas_call(
        body, name=name,
        out_shape=(jax.ShapeDtypeStruct((T, D), F32), jax.ShapeDtypeStruct((T, D), BF16)),
        grid=(T // tq,), in_specs=[row, vec, vec], out_specs=(row, row),
        compiler_params=_params(("parallel",)),
    )(z, g, b)


def _ln_bwd(dy, z, g, name, tq=256):
    T, D = z.shape

    def body(dy_ref, z_ref, g_ref, dz_ref, dzb_ref, dg_ref, db_ref, ds_ref):
        @pl.when(pl.program_id(0) == 0)
        def _():
            dg_ref[...] = jnp.zeros_like(dg_ref)
            db_ref[...] = jnp.zeros_like(db_ref)
            ds_ref[...] = jnp.zeros_like(ds_ref)

        zv, dyv = z_ref[...], dy_ref[...]
        mu = jnp.mean(zv, axis=1, keepdims=True)
        zc = zv - mu
        var = jnp.mean(zc * zc, axis=1, keepdims=True)
        rstd = lax.rsqrt(var + LN_EPS)
        xh = zc * rstd
        u = dyv * g_ref[...]
        dz = rstd * (u - jnp.mean(u, axis=1, keepdims=True) - xh * jnp.mean(u * xh, axis=1, keepdims=True))
        dz_ref[...] = dz
        dzb_ref[...] = dz.astype(BF16)
        dg_ref[...] += jnp.sum(dyv * xh, axis=0, keepdims=True)
        db_ref[...] += jnp.sum(dyv, axis=0, keepdims=True)
        ds_ref[...] += jnp.sum(dz, axis=0, keepdims=True)

    row = pl.BlockSpec((tq, D), lambda i: (i, 0))
    vec = pl.BlockSpec((1, D), lambda i: (0, 0))
    vshape = jax.ShapeDtypeStruct((1, D), F32)
    return pl.pallas_call(
        body, name=name,
        out_shape=(jax.ShapeDtypeStruct((T, D), F32), jax.ShapeDtypeStruct((T, D), BF16), vshape, vshape, vshape),
        grid=(T // tq,), in_specs=[row, row, vec], out_specs=(row, row, vec, vec, vec),
        compiler_params=_params(("arbitrary",)),
    )(dy, z, g)


def _loss_head(y, target, name, tq=256):
    T, D = y.shape

    def body(y_ref, t_ref, s_ref, dy_ref):
        @pl.when(pl.program_id(0) == 0)
        def _():
            s_ref[...] = jnp.zeros_like(s_ref)

        err = y_ref[...] - t_ref[...]
        s_ref[...] += jnp.sum(err * err)
        dy_ref[...] = err * (1.0 / D)

    row = pl.BlockSpec((tq, D), lambda i: (i, 0))
    acc = pl.BlockSpec((8, 128), lambda i: (0, 0))
    return pl.pallas_call(
        body, name=name,
        out_shape=(jax.ShapeDtypeStruct((8, 128), F32), jax.ShapeDtypeStruct((T, D), F32)),
        grid=(T // tq,), in_specs=[row, row], out_specs=(acc, row),
        compiler_params=_params(("arbitrary",)),
    )(y, target)


def _pblock(tq, width, offset):
    assert offset % width == 0
    blk = offset // width
    return pl.BlockSpec((tq, width), lambda i: (i, blk))


def _mix_fwd(proj, q_g, kv_g, w_pool, pool_scale, conv_w, name, tq=256):
    T = proj.shape[0]

    def body(ql_ref, kvl_ref, pin_ref, gp_ref, ch_ref, cb_ref, cc_ref, gc_ref, qg_ref, kvg_ref, wp_ref, ps_ref,
             cw_ref, qn_ref, kvn_ref, pooled_ref, cv_ref, ypc_ref, extp, extu):
        i = pl.program_id(0)
        for x_ref, g_ref, o_ref in ((ql_ref, qg_ref, qn_ref), (kvl_ref, kvg_ref, kvn_ref)):
            x = x_ref[...]
            r = lax.rsqrt(jnp.mean(x * x, axis=1, keepdims=True) + RMS_EPS)
            o_ref[...] = (x * r * g_ref[...]).astype(BF16)

        @pl.when(i == 0)
        def _():
            extp[0:HALO, :] = jnp.zeros((HALO, D_POOL), F32)
            extu[0:HALO, :] = jnp.zeros((HALO, D_CONV), F32)

        @pl.when(i > 0)
        def _():
            extp[0:HALO, :] = extp[tq:tq + HALO, :]
            extu[0:HALO, :] = extu[tq:tq + HALO, :]

        pin = pin_ref[...]
        extp[HALO:, :] = pin
        u = cc_ref[...] * ch_ref[...]
        extu[HALO:, :] = u
        t1 = (i * tq + lax.broadcasted_iota(jnp.int32, (tq, 1), 0) + 1).astype(F32)
        for g, w in enumerate(POOL_WINDOWS):
            cols = slice(g * POOL_GROUP, (g + 1) * POOL_GROUP)
            s = extp[:, cols]
            k = 1
            while k < w:
                s = s + pltpu.roll(s, k, 0)
                k *= 2
            mean = s[HALO:, :] / jnp.minimum(t1, float(w))
            pooled = (mean - pin[:, cols]).astype(BF16)
            pooled_ref[:, cols] = pooled
            r = jnp.dot(pooled, wp_ref[g], preferred_element_type=F32)
            gp = gp_ref[:, cols]
            ypc_ref[:, cols] = (r * ps_ref[:, cols] * (gp * _sigmoid(gp))).astype(BF16)
        eu = extu[...]
        u1 = pltpu.roll(eu, 1, 0)[HALO:, :]
        u2 = pltpu.roll(eu, 2, 0)[HALO:, :]
        cv = cw_ref[0:1, :] * u2 + cw_ref[1:2, :] * u1 + cw_ref[2:3, :] * u
        cv_ref[...] = cv
        gc = gc_ref[...]
        ypc_ref[:, D_POOL:] = (cb_ref[...] * cv * (gc * _sigmoid(gc))).astype(BF16)

    full = lambda shape: pl.BlockSpec(shape, lambda i: (0,) * len(shape))
    row = lambda w: pl.BlockSpec((tq, w), lambda i: (i, 0))
    return pl.pallas_call(
        body, name=name,
        out_shape=(jax.ShapeDtypeStruct((T, Q_LORA), BF16), jax.ShapeDtypeStruct((T, KV_LORA), BF16),
                   jax.ShapeDtypeStruct((T, D_POOL), BF16), jax.ShapeDtypeStruct((T, D_CONV), F32),
                   jax.ShapeDtypeStruct((T, D_POOL + D_CONV), BF16)),
        grid=(T // tq,),
        in_specs=[_pblock(tq, Q_LORA, O_QLAT), _pblock(tq, KV_LORA, O_KVLAT), _pblock(tq, 512, O_PIN),
                  _pblock(tq, 512, O_GPOOL), _pblock(tq, 512, O_CH), _pblock(tq, 512, O_CB), _pblock(tq, 512, O_CC),
                  _pblock(tq, 512, O_GCONV), full((1, Q_LORA)), full((1, KV_LORA)), full((4, 128, 128)),
                  full((1, D_POOL)), full((8, D_CONV))],
        out_specs=(row(Q_LORA), row(KV_LORA), row(D_POOL), row(D_CONV), row(D_POOL + D_CONV)),
        scratch_shapes=[pltpu.VMEM((tq + HALO, D_POOL), F32), pltpu.VMEM((tq + HALO, D_CONV), F32)],
        compiler_params=_params(("arbitrary",)),
    )(proj, proj, proj, proj, proj, proj, proj, proj, q_g, kv_g, w_pool, pool_scale, conv_w)


def _mix_bwd(dmix, proj, o, pooled, cv, w_pool, pool_scale, conv_w, name, tq=ATT_CH):
    T = proj.shape[0]
    nt = T // tq
    n_ext = tq + HALO

    def body(dym_ref, dyp_ref, dyc_ref, gm_ref, gp_ref, ch_ref, cb_ref, cc_ref, gc_ref, o_ref, pooled_ref, cv_ref,
             wp_ref, ps_ref, cw_ref, do_ref, delta_ref, dgm_ref, dmid_ref, dwp_ref, dps_ref, dcw_ref, exte, extd):
        i = pl.program_id(0)
        tile = nt - 1 - i

        @pl.when(i == 0)
        def _():
            dwp_ref[...] = jnp.zeros_like(dwp_ref)
            dps_ref[...] = jnp.zeros_like(dps_ref)
            dcw_ref[...] = jnp.zeros_like(dcw_ref)
            exte[tq:, :] = jnp.zeros((HALO, D_POOL), F32)
            extd[tq:, :] = jnp.zeros((HALO, D_CONV), F32)

        @pl.when(i > 0)
        def _():
            exte[tq:, :] = exte[0:HALO, :]
            extd[tq:, :] = extd[0:HALO, :]

        gm = gm_ref[...]
        sig = _sigmoid(gm)
        dym = dym_ref[...]
        ov = o_ref[...]
        do = dym * (gm * sig)
        do_ref[...] = do.astype(BF16)
        prod = do * ov
        ones = jnp.ones((8, V_DIM), F32)
        for h in range(N_HEADS):
            rows = lax.dot_general(ones, prod[:, h * V_DIM:(h + 1) * V_DIM], _DIMS["nt"],
                                   precision=lax.Precision.HIGHEST, preferred_element_type=F32)
            delta_ref[h, 0] = rows[0:1, :]
        dgm_ref[...] = (dym * ov * (sig * (1.0 + gm * (1.0 - sig)))).astype(BF16)

        t1 = (tile * tq + lax.broadcasted_iota(jnp.int32, (tq, 1), 0) + 1).astype(F32)
        for g, w in enumerate(POOL_WINDOWS):
            cols = slice(g * POOL_GROUP, (g + 1) * POOL_GROUP)
            pg = pooled_ref[:, cols]
            r = jnp.dot(pg, wp_ref[g], preferred_element_type=F32)
            gp = gp_ref[:, cols]
            sg = _sigmoid(gp)
            sl = gp * sg
            dyg = dyp_ref[:, cols]
            ps = ps_ref[:, cols]
            dmid_ref[:, 512 + g * POOL_GROUP:512 + (g + 1) * POOL_GROUP] = (
                dyg * (r * ps) * (sg * (1.0 + gp * (1.0 - sg)))).astype(BF16)
            dps_ref[:, cols] += jnp.sum(dyg * r * sl, axis=0, keepdims=True)
            dr = (dyg * ps * sl).astype(BF16)
            dwp_ref[g] += lax.dot_general(pg, dr, _DIMS["tn"], preferred_element_type=F32)
            dpooled = lax.dot_general(dr, wp_ref[g], _DIMS["nt"], preferred_element_type=F32)
            exte[0:tq, cols] = dpooled / jnp.minimum(t1, float(w))
            s = exte[:, cols]
            k = 1
            while k < w:
                s = s + pltpu.roll(s, n_ext - k, 0)
                k *= 2
            dmid_ref[:, cols] = (s[0:tq, :] - dpooled).astype(BF16)

        gc = gc_ref[...]
        sg = _sigmoid(gc)
        sl = gc * sg
        dyc = dyc_ref[...]
        cb, cc, ch, cvv = cb_ref[...], cc_ref[...], ch_ref[...], cv_ref[...]
        dcv = dyc * cb * sl
        dmid_ref[:, 2560:3072] = (dyc * (cb * cvv) * (sg * (1.0 + gc * (1.0 - sg)))).astype(BF16)
        dmid_ref[:, 1536:2048] = (dyc * cvv * sl).astype(BF16)
        extd[0:tq, :] = dcv
        ed = extd[...]
        d1 = pltpu.roll(ed, n_ext - 1, 0)[0:tq, :]
        d2 = pltpu.roll(ed, n_ext - 2, 0)[0:tq, :]
        du = cw_ref[2:3, :] * dcv + cw_ref[1:2, :] * d1 + cw_ref[0:1, :] * d2
        u = cc * ch
        dcw_ref[0:1, :] += jnp.sum(u * d2, axis=0, keepdims=True)
        dcw_ref[1:2, :] += jnp.sum(u * d1, axis=0, keepdims=True)
        dcw_ref[2:3, :] += jnp.sum(u * dcv, axis=0, keepdims=True)
        dmid_ref[:, 1024:1536] = (du * cc).astype(BF16)
        dmid_ref[:, 2048:2560] = (du * ch).astype(BF16)

    def rblock(width, offset):
        assert offset % width == 0
        blk = offset // width
        return pl.BlockSpec((tq, width), lambda i: (nt - 1 - i, blk))

    full = lambda shape: pl.BlockSpec(shape, lambda i: (0,) * len(shape))
    return pl.pallas_call(
        body, name=name,
        out_shape=(jax.ShapeDtypeStruct((T, D_MLA), BF16), jax.ShapeDtypeStruct((N_HEADS, nt, 1, tq), F32),
                   jax.ShapeDtypeStruct((T, D_MLA), BF16), jax.ShapeDtypeStruct((T, 3072), BF16),
                   jax.ShapeDtypeStruct((4, 128, 128), F32), jax.ShapeDtypeStruct((1, D_POOL), F32),
                   jax.ShapeDtypeStruct((8, D_CONV), F32)),
        grid=(nt,),
        in_specs=[rblock(1024, 0), rblock(512, 1024), rblock(512, 1536),
                  rblock(1024, O_GMLA), rblock(512, O_GPOOL), rblock(512, O_CH), rblock(512, O_CB),
                  rblock(512, O_CC), rblock(512, O_GCONV), rblock(1024, 0), rblock(512, 0), rblock(512, 0),
                  full((4, 128, 128)), full((1, D_POOL)), full((8, D_CONV))],
        out_specs=(rblock(1024, 0), pl.BlockSpec((N_HEADS, 1, 1, tq), lambda i: (0, nt - 1 - i, 0, 0)),
                   rblock(1024, 0),
                   rblock(3072, 0), full((4, 128, 128)), full((1, D_POOL)), full((8, D_CONV))),
        scratch_shapes=[pltpu.VMEM((n_ext, D_POOL), F32), pltpu.VMEM((n_ext, D_CONV), F32)],
        compiler_params=_params(("arbitrary",)),
    )(dmix, dmix, dmix, proj, proj, proj, proj, proj, proj, o, pooled, cv, w_pool, pool_scale, conv_w)


def _rms_bwd(proj, dqn, dkvn, q_g, kv_g, name, tq=256):
    T = proj.shape[0]

    def body(ql_ref, kvl_ref, dqn_ref, dkvn_ref, qg_ref, kvg_ref, dql_ref, dkvl_ref, dqg_ref, dkvg_ref):
        @pl.when(pl.program_id(0) == 0)
        def _():
            dqg_ref[...] = jnp.zeros_like(dqg_ref)
            dkvg_ref[...] = jnp.zeros_like(dkvg_ref)

        for x_ref, dy_ref, g_ref, dx_ref, dg_ref in ((ql_ref, dqn_ref, qg_ref, dql_ref, dqg_ref),
                                                     (kvl_ref, dkvn_ref, kvg_ref, dkvl_ref, dkvg_ref)):
            x, dy = x_ref[...], dy_ref[...]
            r = lax.rsqrt(jnp.mean(x * x, axis=1, keepdims=True) + RMS_EPS)
            xr = x * r
            u = dy * g_ref[...]
            dx_ref[...] = (r * (u - xr * jnp.mean(u * xr, axis=1, keepdims=True))).astype(BF16)
            dg_ref[...] += jnp.sum(dy * xr, axis=0, keepdims=True)

    row = lambda w: pl.BlockSpec((tq, w), lambda i: (i, 0))
    vec = lambda w: pl.BlockSpec((1, w), lambda i: (0, 0))
    return pl.pallas_call(
        body, name=name,
        out_shape=(jax.ShapeDtypeStruct((T, Q_LORA), BF16), jax.ShapeDtypeStruct((T, KV_LORA), BF16),
                   jax.ShapeDtypeStruct((1, Q_LORA), F32), jax.ShapeDtypeStruct((1, KV_LORA), F32)),
        grid=(T // tq,),
        in_specs=[_pblock(tq, Q_LORA, O_QLAT), _pblock(tq, KV_LORA, O_KVLAT), row(Q_LORA), row(KV_LORA),
                  vec(Q_LORA), vec(KV_LORA)],
        out_specs=(row(Q_LORA), row(KV_LORA), vec(Q_LORA), vec(KV_LORA)),
        compiler_params=_params(("arbitrary",)),
    )(proj, proj, dqn, dkvn, q_g, kv_g)


def _swap_halves(x, lo):
    return jnp.where(lo, pltpu.roll(x, 96, 1), pltpu.roll(x, 32, 1))


def _rope_fwd(q, kv, proj, cos_t, sin_t, name, tq=256):
    T = q.shape[0]

    def body(qn_ref, qr_ref, kv_ref, kr_ref, c_ref, s_ref, qc_ref, kc_ref):
        C, S = c_ref[...], s_ref[...]
        lane = lax.broadcasted_iota(jnp.int32, (tq, 128), 1)
        lo = (lane % ROPE) < (ROPE // 2)
        first = lane < ROPE

        def rope(x):
            return x * C + _swap_halves(x, lo) * S

        kr = jnp.where(first, rope(kr_ref[...]), 0.0).astype(BF16)
        for j in range(N_HEADS // 2):
            r = rope(qr_ref[:, j * 128:(j + 1) * 128])
            pair = (jnp.where(first, r, 0.0), jnp.where(first, pltpu.roll(r, 64, 1), 0.0))
            for hh in range(2):
                h = 2 * j + hh
                qc_ref[h, :, 0:NOPE] = qn_ref[:, h * NOPE:(h + 1) * NOPE].astype(BF16)
                qc_ref[h, :, NOPE:QC] = pair[hh].astype(BF16)
        for h in range(N_HEADS):
            kc_ref[h, :, 0:NOPE] = kv_ref[:, h * 256:h * 256 + NOPE]
            kc_ref[h, :, NOPE:QC] = kr

    out = jax.ShapeDtypeStruct((N_HEADS, T, QC), BF16)
    hblock = pl.BlockSpec((N_HEADS, tq, QC), lambda i: (0, i, 0))
    return pl.pallas_call(
        body, name=name, out_shape=(out, out), grid=(T // tq,),
        in_specs=[pl.BlockSpec((tq, 1024), lambda i: (i, 0)), pl.BlockSpec((tq, 512), lambda i: (i, 2)),
                  pl.BlockSpec((tq, 2048), lambda i: (i, 0)), _pblock(tq, 128, O_KROPE),
                  pl.BlockSpec((tq, 128), lambda i: (i, 0)), pl.BlockSpec((tq, 128), lambda i: (i, 0))],
        out_specs=(hblock, hblock),
        compiler_params=_params(("parallel",)),
    )(q, q, kv, proj, cos_t, sin_t)


def _rope_bwd(dqc, dkr, cos_t, sin_t, name, tq=256):
    T = dqc.shape[1]

    def body(dqc_ref, dkr_ref, c_ref, s_ref, dq_ref, dk_ref):
        C, S = c_ref[...], s_ref[...]
        lane = lax.broadcasted_iota(jnp.int32, (tq, 128), 1)
        lo = (lane % ROPE) < (ROPE // 2)
        first = lane < ROPE

        def unrope(dy):
            return dy * C - _swap_halves(dy, lo) * S

        acc = dkr_ref[0]
        for h in range(1, N_HEADS):
            acc = acc + dkr_ref[h]
        dk_ref[:, 0:128] = jnp.where(first, unrope(acc), 0.0).astype(BF16)
        dk_ref[:, 128:256] = jnp.zeros((tq, 128), BF16)
        for j in range(N_HEADS // 2):
            d0 = dqc_ref[2 * j, :, NOPE:QC]
            d1 = dqc_ref[2 * j + 1, :, NOPE:QC]
            comb = jnp.where(first, d0, pltpu.roll(d1, 64, 1))
            dq_ref[:, 1024 + j * 128:1024 + (j + 1) * 128] = unrope(comb).astype(BF16)
        for h in range(N_HEADS):
            dq_ref[:, h * NOPE:(h + 1) * NOPE] = dqc_ref[h, :, 0:NOPE].astype(BF16)

    tab = pl.BlockSpec((tq, 128), lambda i: (i, 0))
    return pl.pallas_call(
        body, name=name,
        out_shape=(jax.ShapeDtypeStruct((T, 1536), BF16), jax.ShapeDtypeStruct((T, 256), BF16)),
        grid=(T // tq,),
        in_specs=[pl.BlockSpec((N_HEADS, tq, QC), lambda i: (0, i, 0)),
                  pl.BlockSpec((N_HEADS, tq, 128), lambda i: (0, i, 0)), tab, tab],
        out_specs=(pl.BlockSpec((tq, 1536), lambda i: (i, 0)), pl.BlockSpec((tq, 256), lambda i: (i, 0))),
        compiler_params=_params(("parallel",)),
    )(dqc, dkr, cos_t, sin_t)


def _flash_fwd(qc, kc, kv, proj, name):
    H, T, _ = qc.shape
    tt = ATT_TILE
    nt = T // tt
    sp = tt // ATT_CH

    def body(q_ref, k_ref, v_ref, g_ref, o_ref, y_ref, lse_ref, vt_sc, s_sc, acc_sc, m_sc, l_sc):
        i = pl.program_id(1)

        @pl.when(i == 0)
        def _():
            for c in range(nt):
                vt_sc[c] = v_ref[c * tt:(c + 1) * tt, :].astype(F32).T.astype(BF16)

        q = q_ref[0]

        def issue(c, slot):
            s_sc[slot] = lax.dot_general(k_ref[0, pl.ds(pl.multiple_of(c * tt, tt), tt), :], q, _DIMS["nt"],
                                         preferred_element_type=F32)

        def softmax_pv(c, slot, masked):
            s = s_sc[slot]
            if masked:
                krow = c * tt + lax.broadcasted_iota(jnp.int32, s.shape, 0)
                qcol = i * tt + lax.broadcasted_iota(jnp.int32, s.shape, 1)
                s = jnp.where(krow <= qcol, s, -jnp.inf)
            m = m_sc[...]
            m_new = jnp.maximum(m, jnp.max(s, axis=0, keepdims=True))
            p = jnp.exp2((s - m_new) * EXP2_SCALE)
            a = jnp.exp2((m - m_new) * EXP2_SCALE)
            l_sc[...] = a * l_sc[...] + jnp.sum(p, axis=0, keepdims=True)
            acc_sc[...] = a * acc_sc[...] + jnp.dot(vt_sc[c], p.astype(BF16), preferred_element_type=F32)
            m_sc[...] = m_new

        m_sc[...] = jnp.full_like(m_sc, -jnp.inf)
        l_sc[...] = jnp.zeros_like(l_sc)
        acc_sc[...] = jnp.zeros_like(acc_sc)
        issue(0, 0)

        def pair(t, carry):
            issue(2 * t + 1, 1)
            softmax_pv(2 * t, 0, False)
            issue(2 * t + 2, 0)
            softmax_pv(2 * t + 1, 1, False)
            return carry

        lax.fori_loop(0, i // 2, pair, 0)

        @pl.when(i % 2 == 1)
        def _():
            issue(i, 1)
            softmax_pv(i - 1, 0, False)
            softmax_pv(i, 1, True)

        @pl.when(i % 2 == 0)
        def _():
            softmax_pv(i, 0, True)

        l = l_sc[...]
        o = (acc_sc[...] / l).T
        o_ref[...] = o
        lse = m_sc[...] * ATTN_SCALE + jnp.log(l)
        for r in range(sp):
            lse_ref[0, r] = lse[:, r * ATT_CH:(r + 1) * ATT_CH]
        g = g_ref[...]
        y_ref[...] = (o * (g * _sigmoid(g))).astype(BF16)

    return pl.pallas_call(
        body, name=name,
        out_shape=(jax.ShapeDtypeStruct((T, D_MLA), F32), jax.ShapeDtypeStruct((T, D_MLA), BF16),
                   jax.ShapeDtypeStruct((H, T // ATT_CH, 1, ATT_CH), F32)),
        grid=(H, nt),
        in_specs=[pl.BlockSpec((1, tt, QC), lambda h, i: (h, i, 0)),
                  pl.BlockSpec((1, T, QC), lambda h, i: (h, 0, 0)),
                  pl.BlockSpec((T, V_DIM), lambda h, i: (0, 2 * h + 1)),
                  pl.BlockSpec((tt, V_DIM), lambda h, i: (i, h))],
        out_specs=(pl.BlockSpec((tt, V_DIM), lambda h, i: (i, h)),
                   pl.BlockSpec((tt, V_DIM), lambda h, i: (i, h)),
                   pl.BlockSpec((1, sp, 1, ATT_CH), lambda h, i: (h, i, 0, 0))),
        scratch_shapes=[pltpu.VMEM((nt, V_DIM, tt), BF16), pltpu.VMEM((2, tt, tt), F32),
                        pltpu.VMEM((V_DIM, tt), F32), pltpu.VMEM((1, tt), F32), pltpu.VMEM((1, tt), F32)],
        compiler_params=_params(("parallel", "arbitrary")),
    )(qc, kc, kv, proj)


def _flash_bwd(qc, kc, kv, do, lse, delta, name):
    H, T, _ = qc.shape
    tt = ATT_TILE
    nt = T // tt
    sp = tt // ATT_CH

    def body(q_ref, k_ref, v_ref, do_ref, lse_ref, dl_ref, dq_ref, dkv_ref, dkr_ref, dqt_sc, dk_sc, dv_sc, s_sc,
             dp_sc):
        j = pl.program_id(1)

        @pl.when(j == 0)
        def _():
            dqt_sc[...] = jnp.zeros_like(dqt_sc)

        dk_sc[...] = jnp.zeros_like(dk_sc)
        dv_sc[...] = jnp.zeros_like(dv_sc)
        k = k_ref[0]
        v = v_ref[...]
        kt = k.astype(F32).T.astype(BF16)

        def operands(c):
            q0 = pl.multiple_of(c * tt, tt)
            return q_ref[0, pl.ds(q0, tt), :], do_ref[pl.ds(q0, tt), :]

        def stat_row(ref, c):
            return jnp.concatenate([ref[0, sp * c + r] for r in range(sp)], axis=1)

        def early(c, slot):
            q, dov = operands(c)
            s_sc[slot] = lax.dot_general(k, q, _DIMS["nt"], preferred_element_type=F32)
            dp_sc[slot] = lax.dot_general(v, dov, _DIMS["nt"], preferred_element_type=F32)

        def late(c, slot, masked):
            q, dov = operands(c)
            s, dp = s_sc[slot], dp_sc[slot]
            if masked:
                krow = j * tt + lax.broadcasted_iota(jnp.int32, s.shape, 0)
                qcol = c * tt + lax.broadcasted_iota(jnp.int32, s.shape, 1)
                s = jnp.where(krow <= qcol, s, -jnp.inf)
            p = jnp.exp2(s * EXP2_SCALE - stat_row(lse_ref, c) * LOG2E)
            ds = (p * (dp - stat_row(dl_ref, c)) * ATTN_SCALE).astype(BF16)
            dv_sc[...] += jnp.dot(p.astype(BF16), dov, preferred_element_type=F32)
            dk_sc[...] += jnp.dot(ds, q, preferred_element_type=F32)
            dqt_sc[c] += jnp.dot(kt, ds, preferred_element_type=F32)

        early(j, 0)

        @pl.when(j < nt - 1)
        def _():
            early(j + 1, 1)

        late(j, 0, True)
        n_rest = nt - 1 - j

        def pair(u, carry):
            a = j + 1 + 2 * u
            early(a + 1, 0)
            late(a, 1, False)

            @pl.when(a + 2 <= nt - 1)
            def _():
                early(a + 2, 1)

            late(a + 1, 0, False)
            return carry

        lax.fori_loop(0, n_rest // 2, pair, 0)

        @pl.when(n_rest % 2 == 1)
        def _():
            late(nt - 1, 1, False)

        dk = dk_sc[...]
        dkv_ref[:, 0:NOPE] = dk[:, 0:NOPE].astype(BF16)
        dkv_ref[:, NOPE:] = dv_sc[...].astype(BF16)
        dkr_ref[0] = dk[:, NOPE:]

        @pl.when(j == nt - 1)
        def _():
            for c in range(nt):
                dq_ref[0, c * tt:(c + 1) * tt, :] = dqt_sc[c].T

    head = lambda h, j: (h, 0, 0)
    stat = pl.BlockSpec((1, T // ATT_CH, 1, ATT_CH), lambda h, j: (h, 0, 0, 0))
    return pl.pallas_call(
        body, name=name,
        out_shape=(jax.ShapeDtypeStruct((H, T, QC), F32), jax.ShapeDtypeStruct((T, 2 * D_MLA), BF16),
                   jax.ShapeDtypeStruct((H, T, 128), F32)),
        grid=(H, nt),
        in_specs=[pl.BlockSpec((1, T, QC), head),
                  pl.BlockSpec((1, tt, QC), lambda h, j: (h, j, 0)),
                  pl.BlockSpec((tt, V_DIM), lambda h, j: (j, 2 * h + 1)),
                  pl.BlockSpec((T, V_DIM), lambda h, j: (0, h)),
                  stat, stat],
        out_specs=(pl.BlockSpec((1, T, QC), head),
                   pl.BlockSpec((tt, 256), lambda h, j: (j, h)),
                   pl.BlockSpec((1, tt, 128), lambda h, j: (h, j, 0))),
        scratch_shapes=[pltpu.VMEM((nt, QC, tt), F32), pltpu.VMEM((tt, QC), F32), pltpu.VMEM((tt, V_DIM), F32),
                        pltpu.VMEM((2, tt, tt), F32), pltpu.VMEM((2, tt, tt), F32)],
        compiler_params=_params(("parallel", "arbitrary")),
    )(qc, kc, kv, do, lse, delta)


def _adamw(land, w, m, v, name, rows):
    R, C = w.shape
    assert R % rows == 0
    c1 = 1.0 - ADAM_B1 ** ADAM_STEP
    c2 = 1.0 - ADAM_B2 ** ADAM_STEP

    def body(land_ref, w_ref, m_ref, v_ref, g_ref, d_ref, nm_ref, nv_ref):
        g = land_ref[0].astype(F32)
        for s in range(1, N_DEV):
            g = g + land_ref[s].astype(F32)
        nm = ADAM_B1 * m_ref[...] + (1.0 - ADAM_B1) * g
        nv = ADAM_B2 * v_ref[...] + (1.0 - ADAM_B2) * (g * g)
        g_ref[...] = g
        nm_ref[...] = nm
        nv_ref[...] = nv
        d_ref[...] = -ADAM_LR * ((nm / c1) / (jnp.sqrt(nv / c2) + ADAM_EPS) + ADAM_WD * w_ref[...])

    blk = pl.BlockSpec((rows, C), lambda i: (i, 0))
    out = jax.ShapeDtypeStruct((R, C), F32)
    return pl.pallas_call(
        body, name=name, out_shape=(out, out, out, out), grid=(R // rows,),
        in_specs=[pl.BlockSpec((N_DEV, rows, C), lambda i: (0, i, 0)), blk, blk, blk],
        out_specs=(blk, blk, blk, blk),
        compiler_params=_params(("parallel",)),
    )(land, w, m, v)


def _mesh_pos():
    return lax.axis_index("x"), lax.axis_index("y"), lax.axis_index("c")


def _all_gather(arrays, name):
    n = len(arrays)

    def body(*refs):
        ins, outs = refs[:n], refs[n:2 * n]
        send_sems, recv_sems, local_sems = refs[2 * n:]
        x, y, c = _mesh_pos()
        me, sibling = (x, y, c), (x, y, 1 - c)
        chips = [(1 - x, y), (x, 1 - y), (1 - x, 1 - y)]

        def slot(a, pos):
            px, py, pc = pos
            return outs[a].at[4 * px + 2 * py + pc]

        def copy(a, k, block, to, src=None):
            return pltpu.make_async_remote_copy(
                src_ref=slot(a, block) if src is None else src, dst_ref=slot(a, block),
                send_sem=send_sems.at[a * 7 + k], recv_sem=recv_sems.at[a * 7 + k],
                device_id=to, device_id_type=MESH_ID)

        mine, first, passed = [], [], []
        for a in range(n):
            cp = pltpu.make_async_copy(ins[a], slot(a, me), local_sems.at[a])
            cp.start()
            mine.append(cp)
            cps = [copy(a, 0, me, sibling, src=ins[a])]
            cps += [copy(a, 1 + j, me, (*chip, c), src=ins[a]) for j, chip in enumerate(chips)]
            for cp in cps:
                cp.start()
            first += cps
        for j, chip in enumerate(chips):
            for a in range(n):
                copy(a, 1 + j, (*chip, c), me).wait_recv()
                cp = copy(a, 4 + j, (*chip, c), sibling)
                cp.start()
                passed.append(cp)
        for a in range(n):
            copy(a, 0, sibling, me).wait_recv()
            for j, chip in enumerate(chips):
                copy(a, 4 + j, (*chip, 1 - c), me).wait_recv()
        for cp in first + passed:
            cp.wait_send()
        for cp in mine:
            cp.wait()

    hbm = pl.BlockSpec(memory_space=pltpu.HBM)
    return pl.pallas_call(
        body, name=name,
        out_shape=tuple(jax.ShapeDtypeStruct((N_DEV,) + a.shape, a.dtype) for a in arrays),
        in_specs=[hbm] * n, out_specs=tuple([hbm] * n),
        scratch_shapes=[pltpu.SemaphoreType.DMA((7 * n,)), pltpu.SemaphoreType.DMA((7 * n,)),
                        pltpu.SemaphoreType.DMA((n,))],
    )(*arrays)


def _exchange(arrays, name):
    n = len(arrays)

    def body(*refs):
        ins, outs = refs[:n], refs[n:2 * n]
        send_sems, recv_sems, local_sems = refs[2 * n:]
        x, y, c = _mesh_pos()
        my_idx = 4 * x + 2 * y + c
        copies, local = [], []
        for a in range(n):
            cp = pltpu.make_async_copy(ins[a].at[my_idx], outs[a].at[my_idx], local_sems.at[a])
            cp.start()
            local.append(cp)
            for k in range(1, N_DEV):
                px = 1 - x if k & 4 else x
                py = 1 - y if k & 2 else y
                pc = 1 - c if k & 1 else c
                cp = pltpu.make_async_remote_copy(
                    src_ref=ins[a].at[4 * px + 2 * py + pc], dst_ref=outs[a].at[my_idx],
                    send_sem=send_sems.at[a * 7 + k - 1], recv_sem=recv_sems.at[a * 7 + k - 1],
                    device_id=(px, py, pc), device_id_type=MESH_ID)
                cp.start()
                copies.append(cp)
        for cp in copies:
            cp.wait()
        for cp in local:
            cp.wait()

    hbm = pl.BlockSpec(memory_space=pltpu.HBM)
    return pl.pallas_call(
        body, name=name,
        out_shape=tuple(jax.ShapeDtypeStruct(a.shape, a.dtype) for a in arrays),
        in_specs=[hbm] * n, out_specs=tuple([hbm] * n),
        scratch_shapes=[pltpu.SemaphoreType.DMA((7 * n,)), pltpu.SemaphoreType.DMA((7 * n,)),
                        pltpu.SemaphoreType.DMA((n,))],
    )(*arrays)


def _cat_blocks(g, axis):
    return jnp.concatenate([g[d] for d in range(N_DEV)], axis=axis)


def _permute_w_in(w):
    q_lat, kv_lat, k_rope, rest = w[:, 0:512], w[:, 512:768], w[:, 768:832], w[:, 832:]
    g_mla, others = rest[:, 0:1024], rest[:, 1024:]
    pad = jnp.zeros((w.shape[0], NPP - D_IN_PROJ), w.dtype)
    return jnp.concatenate([g_mla, q_lat, others, kv_lat, k_rope, pad], axis=1)


def _unpermute_w_in(w):
    return jnp.concatenate([w[:, O_QLAT:O_PIN], w[:, O_KVLAT:O_KROPE], w[:, O_KROPE:O_KROPE + ROPE],
                            w[:, O_GMLA:O_QLAT], w[:, O_PIN:O_KVLAT]], axis=1)


def _permute_w_uq(w):
    w3 = w.reshape(w.shape[0], N_HEADS, NOPE + ROPE)
    return jnp.concatenate([w3[:, :, :NOPE].reshape(w.shape[0], -1), w3[:, :, NOPE:].reshape(w.shape[0], -1)], axis=1)


def _unpermute_w_uq(w):
    nope = w[:, :N_HEADS * NOPE].reshape(w.shape[0], N_HEADS, NOPE)
    rope = w[:, N_HEADS * NOPE:].reshape(w.shape[0], N_HEADS, ROPE)
    return jnp.concatenate([nope, rope], axis=2).reshape(w.shape[0], -1)


_SMALL = (("emb_ln_g", 16), ("emb_ln_b", 16), ("q_norm_g", 8), ("kv_norm_g", 8), ("w_pool", 1024),
          ("pool_scale", 8), ("b_out", 32), ("ln_g", 32), ("ln_b", 32))
SMALL_ROWS = sum(r for _, r in _SMALL)


def _pack_small(d):
    parts = []
    for name, rows in _SMALL:
        flat = d[name].reshape(-1)
        flat = jnp.pad(flat, (0, rows * 128 - flat.shape[0]))
        parts.append(flat.reshape(rows, 128))
    return jnp.concatenate(parts, axis=0)


def _unpack_small(packed, shapes):
    out, r0 = {}, 0
    for name, rows in _SMALL:
        size = 1
        for s in shapes[name]:
            size *= s
        out[name] = packed[r0:r0 + rows].reshape(-1)[:size].reshape(shapes[name])
        r0 += rows
    return out


def _rope_tables(positions):
    half = ROPE // 2
    inv_freq = ROPE_THETA ** (-jnp.arange(half, dtype=F32) / half)
    ang = positions.astype(F32)[:, None] * inv_freq
    cos, sin = jnp.cos(ang), jnp.sin(ang)
    return jnp.concatenate([cos, cos, cos, cos], axis=1), jnp.concatenate([-sin, sin, -sin, sin], axis=1)


def _local_step(x, positions, target, W):
    cos_t, sin_t = _rope_tables(positions)
    h, hb = _ln_fwd(x, W["emb_ln_g"], W["emb_ln_b"], "emb_ln_fwd")
    saved = []
    for l in range(DEPTH):
        proj = _mm(hb, W["w_in"][l], "nn", F32, "proj_fwd")
        qn, kvn, pooled, cv, ypc = _mix_fwd(proj, W["q_norm_g"][l], W["kv_norm_g"][l], W["w_pool"][l],
                                            W["pool_scale"][l], W["conv_w"][l], "mix_fwd")
        q = _mm(qn, W["w_uq"][l], "nn", F32, "q_up_fwd")
        kv = _mm(kvn, W["w_ukv"][l], "nn", BF16, "kv_up_fwd")
        qc, kc = _rope_fwd(q, kv, proj, cos_t, sin_t, "rope_fwd")
        o, ymla, lse = _flash_fwd(qc, kc, kv, proj, "flash_fwd")
        mix = jnp.concatenate([ymla, ypc], axis=1)
        z = _mm(mix, W["w_out"][l], "nn", F32, "out_fwd", res=h, bias=W["b_out"][l], alpha=ALPHA)
        saved.append((hb, proj, qn, kvn, pooled, cv, kv, qc, kc, o, lse, mix, z))
        h, hb = _ln_fwd(z, W["ln_g"][l], W["ln_b"][l], "ln_fwd")
    sq, dh = _loss_head(h, target, "loss_head")

    grads = {k: [None] * DEPTH for k in ("w_in", "q_norm_g", "kv_norm_g", "w_uq", "w_ukv", "w_pool", "pool_scale",
                                         "conv_w", "w_out", "b_out", "ln_g", "ln_b")}
    for l in reversed(range(DEPTH)):
        hb_in, proj, qn, kvn, pooled, cv, kv, qc, kc, o, lse, mix, z = saved[l]
        dz, dzb, grads["ln_g"][l], grads["ln_b"][l], grads["b_out"][l] = _ln_bwd(dh, z, W["ln_g"][l], "ln_bwd")
        dmix = _mm(dzb, W["w_out"][l], "nt", F32, "out_bwd_x")
        grads["w_out"][l] = _mm(mix, dzb, "tn", GRAD_XFER, "out_bwd_w")
        do, delta, dgm, dmid, grads["w_pool"][l], grads["pool_scale"][l], grads["conv_w"][l] = _mix_bwd(
            dmix, proj, o, pooled, cv, W["w_pool"][l], W["pool_scale"][l], W["conv_w"][l], "mix_bwd")
        dqc, dkv, dkr = _flash_bwd(qc, kc, kv, do, lse, delta, "flash_bwd")
        dq, dkrope = _rope_bwd(dqc, dkr, cos_t, sin_t, "rope_bwd")
        dqn = _mm(dq, W["w_uq"][l], "nt", F32, "q_up_bwd_x")
        grads["w_uq"][l] = _mm(qn, dq, "tn", GRAD_XFER, "q_up_bwd_w")
        dkvn = _mm(dkv, W["w_ukv"][l], "nt", F32, "kv_up_bwd_x")
        grads["w_ukv"][l] = _mm(kvn, dkv, "tn", GRAD_XFER, "kv_up_bwd_w")
        dql, dkvl, grads["q_norm_g"][l], grads["kv_norm_g"][l] = _rms_bwd(
            proj, dqn, dkvn, W["q_norm_g"][l], W["kv_norm_g"][l], "rms_bwd")
        dproj = jnp.concatenate([dgm, dql, dmid, dkvl, dkrope], axis=1)
        dh = _mm(dproj, W["w_in"][l], "nt", F32, "proj_bwd_x", res=dz, alpha=ALPHA)
        grads["w_in"][l] = _mm(hb_in, dproj, "tn", GRAD_XFER, "proj_bwd_w")
    grad_x, _, grads["emb_ln_g"], grads["emb_ln_b"], _ = _ln_bwd(dh, x, W["emb_ln_g"], "emb_ln_bwd")
    return sq, grad_x, grads


def kernel(x, positions, emb_ln_g, emb_ln_b, w_in, q_norm_g, kv_norm_g, w_uq, w_ukv, w_pool, pool_scale, conv_w, w_out, b_out, ln_g, ln_b, loss_target, m_emb_ln_g, m_emb_ln_b, m_w_in, m_q_norm_g, m_kv_norm_g, m_w_uq, m_w_ukv, m_w_pool, m_pool_scale, m_conv_w, m_w_out, m_b_out, m_ln_g, m_ln_b, v_emb_ln_g, v_emb_ln_b, v_w_in, v_q_norm_g, v_kv_norm_g, v_w_uq, v_w_ukv, v_w_pool, v_pool_scale, v_conv_w, v_w_out, v_b_out, v_ln_g, v_ln_b):
    weights = dict(emb_ln_g=emb_ln_g, emb_ln_b=emb_ln_b, w_in=w_in, q_norm_g=q_norm_g, kv_norm_g=kv_norm_g,
                   w_uq=w_uq, w_ukv=w_ukv, w_pool=w_pool, pool_scale=pool_scale, conv_w=conv_w, w_out=w_out,
                   b_out=b_out, ln_g=ln_g, ln_b=ln_b)
    mom1 = dict(emb_ln_g=m_emb_ln_g, emb_ln_b=m_emb_ln_b, w_in=m_w_in, q_norm_g=m_q_norm_g, kv_norm_g=m_kv_norm_g,
                w_uq=m_w_uq, w_ukv=m_w_ukv, w_pool=m_w_pool, pool_scale=m_pool_scale, conv_w=m_conv_w,
                w_out=m_w_out, b_out=m_b_out, ln_g=m_ln_g, ln_b=m_ln_b)
    mom2 = dict(emb_ln_g=v_emb_ln_g, emb_ln_b=v_emb_ln_b, w_in=v_w_in, q_norm_g=v_q_norm_g, kv_norm_g=v_kv_norm_g,
                w_uq=v_w_uq, w_ukv=v_w_ukv, w_pool=v_w_pool, pool_scale=v_pool_scale, conv_w=v_conv_w,
                w_out=v_w_out, b_out=v_b_out, ln_g=v_ln_g, ln_b=v_ln_b)

    conv_pad = jnp.zeros((8, 128), F32).at[0:DEPTH * CONV_WIDTH, 0:64].set(conv_w.reshape(DEPTH * CONV_WIDTH, 64))
    g_in, g_uq, g_ukv, g_out, g_conv = _all_gather(
        [w_in.astype(BF16), w_uq.astype(BF16), w_ukv.astype(BF16), w_out.astype(BF16), conv_pad], "weights_all_gather")
    conv_full = _cat_blocks(g_conv[:, 0:DEPTH * CONV_WIDTH, 0:64], 1).reshape(DEPTH, CONV_WIDTH, D_CONV)
    conv_full = jnp.pad(conv_full, ((0, 0), (0, 8 - CONV_WIDTH), (0, 0)))
    W = dict(
        emb_ln_g=emb_ln_g.reshape(1, -1), emb_ln_b=emb_ln_b.reshape(1, -1),
        w_in=[_permute_w_in(_cat_blocks(g_in[:, l], 1)) for l in range(DEPTH)],
        w_uq=[_permute_w_uq(_cat_blocks(g_uq[:, l], 1)) for l in range(DEPTH)],
        w_ukv=[_cat_blocks(g_ukv[:, l], 1) for l in range(DEPTH)],
        w_out=[_cat_blocks(g_out[:, l], 0) for l in range(DEPTH)],
        conv_w=[conv_full[l] for l in range(DEPTH)],
        q_norm_g=[q_norm_g[l].reshape(1, -1) for l in range(DEPTH)],
        kv_norm_g=[kv_norm_g[l].reshape(1, -1) for l in range(DEPTH)],
        w_pool=[w_pool[l].astype(BF16) for l in range(DEPTH)],
        pool_scale=[pool_scale[l].reshape(1, -1) for l in range(DEPTH)],
        b_out=[b_out[l].reshape(1, -1) for l in range(DEPTH)],
        ln_g=[ln_g[l].reshape(1, -1) for l in range(DEPTH)],
        ln_b=[ln_b[l].reshape(1, -1) for l in range(DEPTH)],
    )

    sq, grad_x, G = _local_step(x[0], positions[0], loss_target[0], W)
    loss = lax.psum(sq[0, 0] * (0.5 / D_MODEL), ("x", "y", "c"))

    d_in = jnp.stack([_unpermute_w_in(G["w_in"][l]) for l in range(DEPTH)])
    d_in = d_in.reshape(DEPTH, D_MODEL, N_DEV, -1).transpose(2, 0, 1, 3)
    d_uq = jnp.stack([_unpermute_w_uq(G["w_uq"][l]) for l in range(DEPTH)])
    d_uq = d_uq.reshape(DEPTH, Q_LORA, N_DEV, -1).transpose(2, 0, 1, 3)
    d_ukv = jnp.stack(G["w_ukv"]).reshape(DEPTH, KV_LORA, N_DEV, -1).transpose(2, 0, 1, 3)
    d_out = jnp.stack(G["w_out"]).reshape(DEPTH, N_DEV, -1, D_MODEL).transpose(1, 0, 2, 3)
    d_conv = jnp.stack([G["conv_w"][l][0:CONV_WIDTH] for l in range(DEPTH)])
    d_conv = d_conv.reshape(DEPTH * CONV_WIDTH, N_DEV, 64).transpose(1, 0, 2)
    d_conv = jnp.zeros((N_DEV, 8, 128), F32).at[:, 0:DEPTH * CONV_WIDTH, 0:64].set(d_conv)
    small = dict(emb_ln_g=G["emb_ln_g"], emb_ln_b=G["emb_ln_b"])
    for k in ("q_norm_g", "kv_norm_g", "w_pool", "pool_scale", "b_out", "ln_g", "ln_b"):
        small[k] = jnp.stack(G[k])
    d_small = jnp.broadcast_to(_pack_small(small)[None], (N_DEV, SMALL_ROWS, 128))
    l_in, l_uq, l_ukv, l_out, l_conv, l_small = _exchange([d_in, d_uq, d_ukv, d_out, d_conv, d_small],
                                                          "gradient_exchange")

    res = {}

    def adam(name, land, rows):
        shape = weights[name].shape
        r2 = lambda a: a.reshape(-1, shape[-1])
        outs = _adamw(land.reshape(N_DEV, -1, shape[-1]), r2(weights[name]), r2(mom1[name]), r2(mom2[name]),
                      "adamw_" + name, rows)
        res[name] = tuple(o.reshape(shape) for o in outs)

    adam("w_in", l_in, 256)
    adam("w_uq", l_uq, 256)
    adam("w_ukv", l_ukv, 256)
    adam("w_out", l_out, 128)
    conv_shard = lambda a: jnp.zeros((8, 128), F32).at[0:DEPTH * CONV_WIDTH, 0:64].set(a.reshape(-1, 64))
    conv_res = _adamw(l_conv, conv_shard(conv_w), conv_shard(m_conv_w), conv_shard(v_conv_w), "adamw_conv_w", 8)
    res["conv_w"] = tuple(o[0:DEPTH * CONV_WIDTH, 0:64].reshape(DEPTH, CONV_WIDTH, 64) for o in conv_res)
    small_res = _adamw(l_small, _pack_small(weights), _pack_small(mom1), _pack_small(mom2), "adamw_small", 392)
    shapes = {k: weights[k].shape for k, _ in _SMALL}
    unpacked = [_unpack_small(o, shapes) for o in small_res]
    for k, _ in _SMALL:
        res[k] = tuple(u[k] for u in unpacked)

    order = ("emb_ln_g", "emb_ln_b", "w_in", "q_norm_g", "kv_norm_g", "w_uq", "w_ukv", "w_pool", "pool_scale",
             "conv_w", "w_out", "b_out", "ln_g", "ln_b")
    return (loss, grad_x[None], *[res[k][0] for k in order], *[res[k][1] for k in order],
            *[res[k][2] for k in order], *[res[k][3] for k in order])
```

```python
import jax
import jax.numpy as jnp
from jax import lax
from jax.experimental import pallas as pl
from jax.experimental.pallas import tpu as pltpu

F32 = jnp.float32
BF16 = jnp.bfloat16

N_DEV = 8
D_MODEL = 2048
DEPTH = 2
N_HEADS = 8
NOPE = 128
ROPE = 64
V_DIM = 128
Q_LORA = 512
KV_LORA = 256
D_MLA = N_HEADS * V_DIM
D_POOL = 512
D_CONV = 512
POOL_WINDOWS = (2, 4, 8, 16)
POOL_GROUP = 128
CONV_WIDTH = 3
D_MIX = D_MLA + D_POOL + D_CONV
D_IN_PROJ = 4928
ROPE_THETA = 10000.0
LN_EPS = 1e-5
RMS_EPS = 1e-6
ALPHA = (2 * DEPTH) ** 0.25
ATTN_SCALE = (NOPE + ROPE) ** -0.5
ADAM_LR = 0.001
ADAM_B1 = 0.9
ADAM_B2 = 0.999
ADAM_EPS = 1e-08
ADAM_WD = 0.01
ADAM_STEP = 10

O_GMLA, O_QLAT, O_PIN, O_GPOOL, O_CH, O_CB, O_CC, O_GCONV, O_KVLAT, O_KROPE = (
    0, 1024, 1536, 2048, 2560, 3072, 3584, 4096, 4608, 4864)
NPP = 5120
QC = NOPE + 2 * ROPE
HALO = 16
ATT_TILE = 512
ATT_CH = 256
LOG2E = 1.4426950408889634
EXP2_SCALE = ATTN_SCALE * LOG2E

GRAD_XFER = BF16
VMEM_LIMIT = 48 * 1024 * 1024
MESH_ID = pl.DeviceIdType.MESH


def _params(sem=None):
    return pltpu.CompilerParams(dimension_semantics=sem, vmem_limit_bytes=VMEM_LIMIT)


def _sigmoid(x):
    return 1.0 / (1.0 + jnp.exp(-x))


def _tile(dim, target):
    if dim <= target:
        return dim
    t = target - target % 128
    while dim % t:
        t -= 128
    return t


_DIMS = {"nn": (((1,), (0,)), ((), ())), "nt": (((1,), (1,)), ((), ())), "tn": (((0,), (0,)), ((), ()))}


def _mm(a, b, mode, out_dtype, name, res=None, bias=None, alpha=1.0, tm=1024, tn=1024, tk=512):
    if mode == "nn":
        (M, K), (K2, N) = a.shape, b.shape
    elif mode == "nt":
        (M, K), (N, K2) = a.shape, b.shape
    else:
        (K, M), (K2, N) = a.shape, b.shape
    assert K == K2
    tm, tn, tk = _tile(M, tm), _tile(N, tn), _tile(K, tk)
    nk = K // tk
    has_res, has_bias = res is not None, bias is not None

    def body(*refs):
        a_ref, b_ref = refs[0], refs[1]
        pos = 2
        res_ref = bias_ref = None
        if has_res:
            res_ref = refs[pos]
            pos += 1
        if has_bias:
            bias_ref = refs[pos]
            pos += 1
        o_ref, acc_ref = refs[pos], refs[pos + 1]
        k = pl.program_id(2)

        @pl.when(k == 0)
        def _():
            acc_ref[...] = jnp.zeros_like(acc_ref)

        acc_ref[...] += lax.dot_general(a_ref[...].astype(BF16), b_ref[...].astype(BF16), _DIMS[mode],
                                        preferred_element_type=F32)

        @pl.when(k == nk - 1)
        def _():
            r = acc_ref[...]
            if has_bias:
                r = r + bias_ref[...]
            if has_res:
                r = alpha * res_ref[...] + r
            o_ref[...] = r.astype(out_dtype)

    if mode == "nn":
        in_specs = [pl.BlockSpec((tm, tk), lambda i, j, k: (i, k)), pl.BlockSpec((tk, tn), lambda i, j, k: (k, j))]
    elif mode == "nt":
        in_specs = [pl.BlockSpec((tm, tk), lambda i, j, k: (i, k)), pl.BlockSpec((tn, tk), lambda i, j, k: (j, k))]
    else:
        in_specs = [pl.BlockSpec((tk, tm), lambda i, j, k: (k, i)), pl.BlockSpec((tk, tn), lambda i, j, k: (k, j))]
    args = [a, b]
    if has_res:
        in_specs.append(pl.BlockSpec((tm, tn), lambda i, j, k: (i, j)))
        args.append(res)
    if has_bias:
        in_specs.append(pl.BlockSpec((1, tn), lambda i, j, k: (0, j)))
        args.append(bias)
    return pl.pallas_call(
        body, name=name,
        out_shape=jax.ShapeDtypeStruct((M, N), out_dtype),
        grid=(M // tm, N // tn, nk),
        in_specs=in_specs,
        out_specs=pl.BlockSpec((tm, tn), lambda i, j, k: (i, j)),
        scratch_shapes=[pltpu.VMEM((tm, tn), F32)],
        compiler_params=_params(("parallel", "parallel", "arbitrary")),
    )(*args)


def _ln_fwd(z, g, b, name, tq=256):
    T, D = z.shape

    def body(z_ref, g_ref, b_ref, y_ref, yb_ref):
        zv = z_ref[...]
        mu = jnp.mean(zv, axis=1, keepdims=True)
        zc = zv - mu
        var = jnp.mean(zc * zc, axis=1, keepdims=True)
        y = zc * lax.rsqrt(var + LN_EPS) * g_ref[...] + b_ref[...]
        y_ref[...] = y
        yb_ref[...] = y.astype(BF16)

    row = pl.BlockSpec((tq, D), lambda i: (i, 0))
    vec = pl.BlockSpec((1, D), lambda i: (0, 0))
    return pl.pallas_call(
        body, name=name,
        out_shape=(jax.ShapeDtypeStruct((T, D), F32), jax.ShapeDtypeStruct((T, D), BF16)),
        grid=(T // tq,), in_specs=[row, vec, vec], out_specs=(row, row),
        compiler_params=_params(("parallel",)),
    )(z, g, b)


def _ln_bwd(dy, z, g, name, tq=256):
    T, D = z.shape

    def body(dy_ref, z_ref, g_ref, dz_ref, dzb_ref, dg_ref, db_ref, ds_ref):
        @pl.when(pl.program_id(0) == 0)
        def _():
            dg_ref[...] = jnp.zeros_like(dg_ref)
            db_ref[...] = jnp.zeros_like(db_ref)
            ds_ref[...] = jnp.zeros_like(ds_ref)

        zv, dyv = z_ref[...], dy_ref[...]
        mu = jnp.mean(zv, axis=1, keepdims=True)
        zc = zv - mu
        var = jnp.mean(zc * zc, axis=1, keepdims=True)
        rstd = lax.rsqrt(var + LN_EPS)
        xh = zc * rstd
        u = dyv * g_ref[...]
        dz = rstd * (u - jnp.mean(u, axis=1, keepdims=True) - xh * jnp.mean(u * xh, axis=1, keepdims=True))
        dz_ref[...] = dz
        dzb_ref[...] = dz.astype(BF16)
        dg_ref[...] += jnp.sum(dyv * xh, axis=0, keepdims=True)
        db_ref[...] += jnp.sum(dyv, axis=0, keepdims=True)
        ds_ref[...] += jnp.sum(dz, axis=0, keepdims=True)

    row = pl.BlockSpec((tq, D), lambda i: (i, 0))
    vec = pl.BlockSpec((1, D), lambda i: (0, 0))
    vshape = jax.ShapeDtypeStruct((1, D), F32)
    return pl.pallas_call(
        body, name=name,
        out_shape=(jax.ShapeDtypeStruct((T, D), F32), jax.ShapeDtypeStruct((T, D), BF16), vshape, vshape, vshape),
        grid=(T // tq,), in_specs=[row, row, vec], out_specs=(row, row, vec, vec, vec),
        compiler_params=_params(("arbitrary",)),
    )(dy, z, g)


def _loss_head(y, target, name, tq=256):
    T, D = y.shape

    def body(y_ref, t_ref, s_ref, dy_ref):
        @pl.when(pl.program_id(0) == 0)
        def _():
            s_ref[...] = jnp.zeros_like(s_ref)

        err = y_ref[...] - t_ref[...]
        s_ref[...] += jnp.sum(err * err)
        dy_ref[...] = err * (1.0 / D)

    row = pl.BlockSpec((tq, D), lambda i: (i, 0))
    acc = pl.BlockSpec((8, 128), lambda i: (0, 0))
    return pl.pallas_call(
        body, name=name,
        out_shape=(jax.ShapeDtypeStruct((8, 128), F32), jax.ShapeDtypeStruct((T, D), F32)),
        grid=(T // tq,), in_specs=[row, row], out_specs=(acc, row),
        compiler_params=_params(("arbitrary",)),
    )(y, target)


def _pblock(tq, width, offset):
    assert offset % width == 0
    blk = offset // width
    return pl.BlockSpec((tq, width), lambda i: (i, blk))


def _mix_fwd(proj, q_g, kv_g, w_pool, pool_scale, conv_w, name, tq=256):
    T = proj.shape[0]

    def body(ql_ref, kvl_ref, pin_ref, gp_ref, ch_ref, cb_ref, cc_ref, gc_ref, qg_ref, kvg_ref, wp_ref, ps_ref,
             cw_ref, qn_ref, kvn_ref, pooled_ref, cv_ref, ypc_ref, extp, extu):
        i = pl.program_id(0)
        for x_ref, g_ref, o_ref in ((ql_ref, qg_ref, qn_ref), (kvl_ref, kvg_ref, kvn_ref)):
            x = x_ref[...]
            r = lax.rsqrt(jnp.mean(x * x, axis=1, keepdims=True) + RMS_EPS)
            o_ref[...] = (x * r * g_ref[...]).astype(BF16)

        @pl.when(i == 0)
        def _():
            extp[0:HALO, :] = jnp.zeros((HALO, D_POOL), F32)
            extu[0:HALO, :] = jnp.zeros((HALO, D_CONV), F32)

        @pl.when(i > 0)
        def _():
            extp[0:HALO, :] = extp[tq:tq + HALO, :]
            extu[0:HALO, :] = extu[tq:tq + HALO, :]

        pin = pin_ref[...]
        extp[HALO:, :] = pin
        u = cc_ref[...] * ch_ref[...]
        extu[HALO:, :] = u
        t1 = (i * tq + lax.broadcasted_iota(jnp.int32, (tq, 1), 0) + 1).astype(F32)
        for g, w in enumerate(POOL_WINDOWS):
            cols = slice(g * POOL_GROUP, (g + 1) * POOL_GROUP)
            s = extp[:, cols]
            k = 1
            while k < w:
                s = s + pltpu.roll(s, k, 0)
                k *= 2
            mean = s[HALO:, :] / jnp.minimum(t1, float(w))
            pooled = (mean - pin[:, cols]).astype(BF16)
            pooled_ref[:, cols] = pooled
            r = jnp.dot(pooled, wp_ref[g], preferred_element_type=F32)
            gp = gp_ref[:, cols]
            ypc_ref[:, cols] = (r * ps_ref[:, cols] * (gp * _sigmoid(gp))).astype(BF16)
        eu = extu[...]
        u1 = pltpu.roll(eu, 1, 0)[HALO:, :]
        u2 = pltpu.roll(eu, 2, 0)[HALO:, :]
        cv = cw_ref[0:1, :] * u2 + cw_ref[1:2, :] * u1 + cw_ref[2:3, :] * u
        cv_ref[...] = cv
        gc = gc_ref[...]
        ypc_ref[:, D_POOL:] = (cb_ref[...] * cv * (gc * _sigmoid(gc))).astype(BF16)

    full = lambda shape: pl.BlockSpec(shape, lambda i: (0,) * len(shape))
    row = lambda w: pl.BlockSpec((tq, w), lambda i: (i, 0))
    return pl.pallas_call(
        body, name=name,
        out_shape=(jax.ShapeDtypeStruct((T, Q_LORA), BF16), jax.ShapeDtypeStruct((T, KV_LORA), BF16),
                   jax.ShapeDtypeStruct((T, D_POOL), BF16), jax.ShapeDtypeStruct((T, D_CONV), F32),
                   jax.ShapeDtypeStruct((T, D_POOL + D_CONV), BF16)),
        grid=(T // tq,),
        in_specs=[_pblock(tq, Q_LORA, O_QLAT), _pblock(tq, KV_LORA, O_KVLAT), _pblock(tq, 512, O_PIN),
                  _pblock(tq, 512, O_GPOOL), _pblock(tq, 512, O_CH), _pblock(tq, 512, O_CB), _pblock(tq, 512, O_CC),
                  _pblock(tq, 512, O_GCONV), full((1, Q_LORA)), full((1, KV_LORA)), full((4, 128, 128)),
                  full((1, D_POOL)), full((8, D_CONV))],
        out_specs=(row(Q_LORA), row(KV_LORA), row(D_POOL), row(D_CONV), row(D_POOL + D_CONV)),
        scratch_shapes=[pltpu.VMEM((tq + HALO, D_POOL), F32), pltpu.VMEM((tq + HALO, D_CONV), F32)],
        compiler_params=_params(("arbitrary",)),
    )(proj, proj, proj, proj, proj, proj, proj, proj, q_g, kv_g, w_pool, pool_scale, conv_w)


def _mix_bwd(dmix, proj, o, pooled, cv, w_pool, pool_scale, conv_w, name, tq=ATT_CH):
    T = proj.shape[0]
    nt = T // tq
    n_ext = tq + HALO

    def body(dym_ref, dyp_ref, dyc_ref, gm_ref, gp_ref, ch_ref, cb_ref, cc_ref, gc_ref, o_ref, pooled_ref, cv_ref,
             wp_ref, ps_ref, cw_ref, do_ref, delta_ref, dgm_ref, dmid_ref, dwp_ref, dps_ref, dcw_ref, exte, extd):
        i = pl.program_id(0)
        tile = nt - 1 - i

        @pl.when(i == 0)
        def _():
            dwp_ref[...] = jnp.zeros_like(dwp_ref)
            dps_ref[...] = jnp.zeros_like(dps_ref)
            dcw_ref[...] = jnp.zeros_like(dcw_ref)
            exte[tq:, :] = jnp.zeros((HALO, D_POOL), F32)
            extd[tq:, :] = jnp.zeros((HALO, D_CONV), F32)

        @pl.when(i > 0)
        def _():
            exte[tq:, :] = exte[0:HALO, :]
            extd[tq:, :] = extd[0:HALO, :]

        gm = gm_ref[...]
        sig = _sigmoid(gm)
        dym = dym_ref[...]
        ov = o_ref[...]
        do = dym * (gm * sig)
        do_ref[...] = do.astype(BF16)
        prod = do * ov
        ones = jnp.ones((8, V_DIM), F32)
        for h in range(N_HEADS):
            rows = lax.dot_general(ones, prod[:, h * V_DIM:(h + 1) * V_DIM], _DIMS["nt"],
                                   precision=lax.Precision.HIGHEST, preferred_element_type=F32)
            delta_ref[h, 0] = rows[0:1, :]
        dgm_ref[...] = (dym * ov * (sig * (1.0 + gm * (1.0 - sig)))).astype(BF16)

        t1 = (tile * tq + lax.broadcasted_iota(jnp.int32, (tq, 1), 0) + 1).astype(F32)
        for g, w in enumerate(POOL_WINDOWS):
            cols = slice(g * POOL_GROUP, (g + 1) * POOL_GROUP)
            pg = pooled_ref[:, cols]
            r = jnp.dot(pg, wp_ref[g], preferred_element_type=F32)
            gp = gp_ref[:, cols]
            sg = _sigmoid(gp)
            sl = gp * sg
            dyg = dyp_ref[:, cols]
            ps = ps_ref[:, cols]
            dmid_ref[:, 512 + g * POOL_GROUP:512 + (g + 1) * POOL_GROUP] = (
                dyg * (r * ps) * (sg * (1.0 + gp * (1.0 - sg)))).astype(BF16)
            dps_ref[:, cols] += jnp.sum(dyg * r * sl, axis=0, keepdims=True)
            dr = (dyg * ps * sl).astype(BF16)
            dwp_ref[g] += lax.dot_general(pg, dr, _DIMS["tn"], preferred_element_type=F32)
            dpooled = lax.dot_general(dr, wp_ref[g], _DIMS["nt"], preferred_element_type=F32)
            exte[0:tq, cols] = dpooled / jnp.minimum(t1, float(w))
            s = exte[:, cols]
            k = 1
            while k < w:
                s = s + pltpu.roll(s, n_ext - k, 0)
                k *= 2
            dmid_ref[:, cols] = (s[0:tq, :] - dpooled).astype(BF16)

        gc = gc_ref[...]
        sg = _sigmoid(gc)
        sl = gc * sg
        dyc = dyc_ref[...]
        cb, cc, ch, cvv = cb_ref[...], cc_ref[...], ch_ref[...], cv_ref[...]
        dcv = dyc * cb * sl
        dmid_ref[:, 2560:3072] = (dyc * (cb * cvv) * (sg * (1.0 + gc * (1.0 - sg)))).astype(BF16)
        dmid_ref[:, 1536:2048] = (dyc * cvv * sl).astype(BF16)
        extd[0:tq, :] = dcv
        ed = extd[...]
        d1 = pltpu.roll(ed, n_ext - 1, 0)[0:tq, :]
        d2 = pltpu.roll(ed, n_ext - 2, 0)[0:tq, :]
        du = cw_ref[2:3, :] * dcv + cw_ref[1:2, :] * d1 + cw_ref[0:1, :] * d2
        u = cc * ch
        dcw_ref[0:1, :] += jnp.sum(u * d2, axis=0, keepdims=True)
        dcw_ref[1:2, :] += jnp.sum(u * d1, axis=0, keepdims=True)
        dcw_ref[2:3, :] += jnp.sum(u * dcv, axis=0, keepdims=True)
        dmid_ref[:, 1024:1536] = (du * cc).astype(BF16)
        dmid_ref[:, 2048:2560] = (du * ch).astype(BF16)

    def rblock(width, offset):
        assert offset % width == 0
        blk = offset // width
        return pl.BlockSpec((tq, width), lambda i: (nt - 1 - i, blk))

    full = lambda shape: pl.BlockSpec(shape, lambda i: (0,) * len(shape))
    return pl.pallas_call(
        body, name=name,
        out_shape=(jax.ShapeDtypeStruct((T, D_MLA), BF16), jax.ShapeDtypeStruct((N_HEADS, nt, 1, tq), F32),
                   jax.ShapeDtypeStruct((T, D_MLA), BF16), jax.ShapeDtypeStruct((T, 3072), BF16),
                   jax.ShapeDtypeStruct((4, 128, 128), F32), jax.ShapeDtypeStruct((1, D_POOL), F32),
                   jax.ShapeDtypeStruct((8, D_CONV), F32)),
        grid=(nt,),
        in_specs=[rblock(1024, 0), rblock(512, 1024), rblock(512, 1536),
                  rblock(1024, O_GMLA), rblock(512, O_GPOOL), rblock(512, O_CH), rblock(512, O_CB),
                  rblock(512, O_CC), rblock(512, O_GCONV), rblock(1024, 0), rblock(512, 0), rblock(512, 0),
                  full((4, 128, 128)), full((1, D_POOL)), full((8, D_CONV))],
        out_specs=(rblock(1024, 0), pl.BlockSpec((N_HEADS, 1, 1, tq), lambda i: (0, nt - 1 - i, 0, 0)),
                   rblock(1024, 0),
                   rblock(3072, 0), full((4, 128, 128)), full((1, D_POOL)), full((8, D_CONV))),
        scratch_shapes=[pltpu.VMEM((n_ext, D_POOL), F32), pltpu.VMEM((n_ext, D_CONV), F32)],
        compiler_params=_params(("arbitrary",)),
    )(dmix, dmix, dmix, proj, proj, proj, proj, proj, proj, o, pooled, cv, w_pool, pool_scale, conv_w)


def _rms_bwd(proj, dqn, dkvn, q_g, kv_g, name, tq=256):
    T = proj.shape[0]

    def body(ql_ref, kvl_ref, dqn_ref, dkvn_ref, qg_ref, kvg_ref, dql_ref, dkvl_ref, dqg_ref, dkvg_ref):
        @pl.when(pl.program_id(0) == 0)
        def _():
            dqg_ref[...] = jnp.zeros_like(dqg_ref)
            dkvg_ref[...] = jnp.zeros_like(dkvg_ref)

        for x_ref, dy_ref, g_ref, dx_ref, dg_ref in ((ql_ref, dqn_ref, qg_ref, dql_ref, dqg_ref),
                                                     (kvl_ref, dkvn_ref, kvg_ref, dkvl_ref, dkvg_ref)):
            x, dy = x_ref[...], dy_ref[...]
            r = lax.rsqrt(jnp.mean(x * x, axis=1, keepdims=True) + RMS_EPS)
            xr = x * r
            u = dy * g_ref[...]
            dx_ref[...] = (r * (u - xr * jnp.mean(u * xr, axis=1, keepdims=True))).astype(BF16)
            dg_ref[...] += jnp.sum(dy * xr, axis=0, keepdims=True)

    row = lambda w: pl.BlockSpec((tq, w), lambda i: (i, 0))
    vec = lambda w: pl.BlockSpec((1, w), lambda i: (0, 0))
    return pl.pallas_call(
        body, name=name,
        out_shape=(jax.ShapeDtypeStruct((T, Q_LORA), BF16), jax.ShapeDtypeStruct((T, KV_LORA), BF16),
                   jax.ShapeDtypeStruct((1, Q_LORA), F32), jax.ShapeDtypeStruct((1, KV_LORA), F32)),
        grid=(T // tq,),
        in_specs=[_pblock(tq, Q_LORA, O_QLAT), _pblock(tq, KV_LORA, O_KVLAT), row(Q_LORA), row(KV_LORA),
                  vec(Q_LORA), vec(KV_LORA)],
        out_specs=(row(Q_LORA), row(KV_LORA), vec(Q_LORA), vec(KV_LORA)),
        compiler_params=_params(("arbitrary",)),
    )(proj, proj, dqn, dkvn, q_g, kv_g)


def _swap_halves(x, lo):
    return jnp.where(lo, pltpu.roll(x, 96, 1), pltpu.roll(x, 32, 1))


def _rope_fwd(q, kv, proj, cos_t, sin_t, name, tq=256):
    T = q.shape[0]

    def body(qn_ref, qr_ref, kv_ref, kr_ref, c_ref, s_ref, qc_ref, kc_ref):
        C, S = c_ref[...], s_ref[...]
        lane = lax.broadcasted_iota(jnp.int32, (tq, 128), 1)
        lo = (lane % ROPE) < (ROPE // 2)
        first = lane < ROPE

        def rope(x):
            return x * C + _swap_halves(x, lo) * S

        kr = jnp.where(first, rope(kr_ref[...]), 0.0).astype(BF16)
        for j in range(N_HEADS // 2):
            r = rope(qr_ref[:, j * 128:(j + 1) * 128])
            pair = (jnp.where(first, r, 0.0), jnp.where(first, pltpu.roll(r, 64, 1), 0.0))
            for hh in range(2):
                h = 2 * j + hh
                qc_ref[h, :, 0:NOPE] = qn_ref[:, h * NOPE:(h + 1) * NOPE].astype(BF16)
                qc_ref[h, :, NOPE:QC] = pair[hh].astype(BF16)
        for h in range(N_HEADS):
            kc_ref[h, :, 0:NOPE] = kv_ref[:, h * 256:h * 256 + NOPE]
            kc_ref[h, :, NOPE:QC] = kr

    out = jax.ShapeDtypeStruct((N_HEADS, T, QC), BF16)
    hblock = pl.BlockSpec((N_HEADS, tq, QC), lambda i: (0, i, 0))
    return pl.pallas_call(
        body, name=name, out_shape=(out, out), grid=(T // tq,),
        in_specs=[pl.BlockSpec((tq, 1024), lambda i: (i, 0)), pl.BlockSpec((tq, 512), lambda i: (i, 2)),
                  pl.BlockSpec((tq, 2048), lambda i: (i, 0)), _pblock(tq, 128, O_KROPE),
                  pl.BlockSpec((tq, 128), lambda i: (i, 0)), pl.BlockSpec((tq, 128), lambda i: (i, 0))],
        out_specs=(hblock, hblock),
        compiler_params=_params(("parallel",)),
    )(q, q, kv, proj, cos_t, sin_t)


def _rope_bwd(dqc, dkr, cos_t, sin_t, name, tq=256):
    T = dqc.shape[1]

    def body(dqc_ref, dkr_ref, c_ref, s_ref, dq_ref, dk_ref):
        C, S = c_ref[...], s_ref[...]
        lane = lax.broadcasted_iota(jnp.int32, (tq, 128), 1)
        lo = (lane % ROPE) < (ROPE // 2)
        first = lane < ROPE

        def unrope(dy):
            return dy * C - _swap_halves(dy, lo) * S

        acc = dkr_ref[0]
        for h in range(1, N_HEADS):
            acc = acc + dkr_ref[h]
        dk_ref[:, 0:128] = jnp.where(first, unrope(acc), 0.0).astype(BF16)
        dk_ref[:, 128:256] = jnp.zeros((tq, 128), BF16)
        for j in range(N_HEADS // 2):
            d0 = dqc_ref[2 * j, :, NOPE:QC]
            d1 = dqc_ref[2 * j + 1, :, NOPE:QC]
            comb = jnp.where(first, d0, pltpu.roll(d1, 64, 1))
            dq_ref[:, 1024 + j * 128:1024 + (j + 1) * 128] = unrope(comb).astype(BF16)
        for h in range(N_HEADS):
            dq_ref[:, h * NOPE:(h + 1) * NOPE] = dqc_ref[h, :, 0:NOPE].astype(BF16)

    tab = pl.BlockSpec((tq, 128), lambda i: (i, 0))
    return pl.pallas_call(
        body, name=name,
        out_shape=(jax.ShapeDtypeStruct((T, 1536), BF16), jax.ShapeDtypeStruct((T, 256), BF16)),
        grid=(T // tq,),
        in_specs=[pl.BlockSpec((N_HEADS, tq, QC), lambda i: (0, i, 0)),
                  pl.BlockSpec((N_HEADS, tq, 128), lambda i: (0, i, 0)), tab, tab],
        out_specs=(pl.BlockSpec((tq, 1536), lambda i: (i, 0)), pl.BlockSpec((tq, 256), lambda i: (i, 0))),
        compiler_params=_params(("parallel",)),
    )(dqc, dkr, cos_t, sin_t)


def _flash_fwd(qc, kc, kv, proj, name):
    H, T, _ = qc.shape
    tt = ATT_TILE
    nt = T // tt
    sp = tt // ATT_CH

    def body(q_ref, k_ref, v_ref, g_ref, o_ref, y_ref, lse_ref, vt_sc, s_sc, acc_sc, m_sc, l_sc):
        i = pl.program_id(1)

        @pl.when(i == 0)
        def _():
            for c in range(nt):
                vt_sc[c] = v_ref[c * tt:(c + 1) * tt, :].astype(F32).T.astype(BF16)

        q = q_ref[0]

        def issue(c, slot):
            s_sc[slot] = lax.dot_general(k_ref[0, pl.ds(pl.multiple_of(c * tt, tt), tt), :], q, _DIMS["nt"],
                                         preferred_element_type=F32)

        def softmax_pv(c, slot, masked):
            s = s_sc[slot]
            if masked:
                krow = c * tt + lax.broadcasted_iota(jnp.int32, s.shape, 0)
                qcol = i * tt + lax.broadcasted_iota(jnp.int32, s.shape, 1)
                s = jnp.where(krow <= qcol, s, -jnp.inf)
            m = m_sc[...]
            m_new = jnp.maximum(m, jnp.max(s, axis=0, keepdims=True))
            p = jnp.exp2((s - m_new) * EXP2_SCALE)
            a = jnp.exp2((m - m_new) * EXP2_SCALE)
            l_sc[...] = a * l_sc[...] + jnp.sum(p, axis=0, keepdims=True)
            acc_sc[...] = a * acc_sc[...] + jnp.dot(vt_sc[c], p.astype(BF16), preferred_element_type=F32)
            m_sc[...] = m_new

        m_sc[...] = jnp.full_like(m_sc, -jnp.inf)
        l_sc[...] = jnp.zeros_like(l_sc)
        acc_sc[...] = jnp.zeros_like(acc_sc)
        issue(0, 0)

        def pair(t, carry):
            issue(2 * t + 1, 1)
            softmax_pv(2 * t, 0, False)
            issue(2 * t + 2, 0)
            softmax_pv(2 * t + 1, 1, False)
            return carry

        lax.fori_loop(0, i // 2, pair, 0)

        @pl.when(i % 2 == 1)
        def _():
            issue(i, 1)
            softmax_pv(i - 1, 0, False)
            softmax_pv(i, 1, True)

        @pl.when(i % 2 == 0)
        def _():
            softmax_pv(i, 0, True)

        l = l_sc[...]
        o = (acc_sc[...] / l).T
        o_ref[...] = o
        lse = m_sc[...] * ATTN_SCALE + jnp.log(l)
        for r in range(sp):
            lse_ref[0, r] = lse[:, r * ATT_CH:(r + 1) * ATT_CH]
        g = g_ref[...]
        y_ref[...] = (o * (g * _sigmoid(g))).astype(BF16)

    return pl.pallas_call(
        body, name=name,
        out_shape=(jax.ShapeDtypeStruct((T, D_MLA), F32), jax.ShapeDtypeStruct((T, D_MLA), BF16),
                   jax.ShapeDtypeStruct((H, T // ATT_CH, 1, ATT_CH), F32)),
        grid=(H, nt),
        in_specs=[pl.BlockSpec((1, tt, QC), lambda h, i: (h, i, 0)),
                  pl.BlockSpec((1, T, QC), lambda h, i: (h, 0, 0)),
                  pl.BlockSpec((T, V_DIM), lambda h, i: (0, 2 * h + 1)),
                  pl.BlockSpec((tt, V_DIM), lambda h, i: (i, h))],
        out_specs=(pl.BlockSpec((tt, V_DIM), lambda h, i: (i, h)),
                   pl.BlockSpec((tt, V_DIM), lambda h, i: (i, h)),
                   pl.BlockSpec((1, sp, 1, ATT_CH), lambda h, i: (h, i, 0, 0))),
        scratch_shapes=[pltpu.VMEM((nt, V_DIM, tt), BF16), pltpu.VMEM((2, tt, tt), F32),
                        pltpu.VMEM((V_DIM, tt), F32), pltpu.VMEM((1, tt), F32), pltpu.VMEM((1, tt), F32)],
        compiler_params=_params(("parallel", "arbitrary")),
    )(qc, kc, kv, proj)


def _flash_bwd(qc, kc, kv, do, lse, delta, name):
    H, T, _ = qc.shape
    tt = ATT_TILE
    nt = T // tt
    sp = tt // ATT_CH

    def body(q_ref, k_ref, v_ref, do_ref, lse_ref, dl_ref, dq_ref, dkv_ref, dkr_ref, dqt_sc, dk_sc, dv_sc, s_sc,
             dp_sc):
        j = pl.program_id(1)

        @pl.when(j == 0)
        def _():
            dqt_sc[...] = jnp.zeros_like(dqt_sc)

        dk_sc[...] = jnp.zeros_like(dk_sc)
        dv_sc[...] = jnp.zeros_like(dv_sc)
        k = k_ref[0]
        v = v_ref[...]
        kt = k.astype(F32).T.astype(BF16)

        def operands(c):
            q0 = pl.multiple_of(c * tt, tt)
            return q_ref[0, pl.ds(q0, tt), :], do_ref[pl.ds(q0, tt), :]

        def stat_row(ref, c):
            return jnp.concatenate([ref[0, sp * c + r] for r in range(sp)], axis=1)

        def early(c, slot):
            q, dov = operands(c)
            s_sc[slot] = lax.dot_general(k, q, _DIMS["nt"], preferred_element_type=F32)
            dp_sc[slot] = lax.dot_general(v, dov, _DIMS["nt"], preferred_element_type=F32)

        def late(c, slot, masked):
            q, dov = operands(c)
            s, dp = s_sc[slot], dp_sc[slot]
            if masked:
                krow = j * tt + lax.broadcasted_iota(jnp.int32, s.shape, 0)
                qcol = c * tt + lax.broadcasted_iota(jnp.int32, s.shape, 1)
                s = jnp.where(krow <= qcol, s, -jnp.inf)
            p = jnp.exp2(s * EXP2_SCALE - stat_row(lse_ref, c) * LOG2E)
            ds = (p * (dp - stat_row(dl_ref, c)) * ATTN_SCALE).astype(BF16)
            dv_sc[...] += jnp.dot(p.astype(BF16), dov, preferred_element_type=F32)
            dk_sc[...] += jnp.dot(ds, q, preferred_element_type=F32)
            dqt_sc[c] += jnp.dot(kt, ds, preferred_element_type=F32)

        early(j, 0)

        @pl.when(j < nt - 1)
        def _():
            early(j + 1, 1)

        late(j, 0, True)
        n_rest = nt - 1 - j

        def pair(u, carry):
            a = j + 1 + 2 * u
            early(a + 1, 0)
            late(a, 1, False)

            @pl.when(a + 2 <= nt - 1)
            def _():
                early(a + 2, 1)

            late(a + 1, 0, False)
            return carry

        lax.fori_loop(0, n_rest // 2, pair, 0)

        @pl.when(n_rest % 2 == 1)
        def _():
            late(nt - 1, 1, False)

        dk = dk_sc[...]
        dkv_ref[:, 0:NOPE] = dk[:, 0:NOPE].astype(BF16)
        dkv_ref[:, NOPE:] = dv_sc[...].astype(BF16)
        dkr_ref[0] = dk[:, NOPE:]

        @pl.when(j == nt - 1)
        def _():
            for c in range(nt):
                dq_ref[0, c * tt:(c + 1) * tt, :] = dqt_sc[c].T

    head = lambda h, j: (h, 0, 0)
    stat = pl.BlockSpec((1, T // ATT_CH, 1, ATT_CH), lambda h, j: (h, 0, 0, 0))
    return pl.pallas_call(
        body, name=name,
        out_shape=(jax.ShapeDtypeStruct((H, T, QC), F32), jax.ShapeDtypeStruct((T, 2 * D_MLA), BF16),
                   jax.ShapeDtypeStruct((H, T, 128), F32)),
        grid=(H, nt),
        in_specs=[pl.BlockSpec((1, T, QC), head),
                  pl.BlockSpec((1, tt, QC), lambda h, j: (h, j, 0)),
                  pl.BlockSpec((tt, V_DIM), lambda h, j: (j, 2 * h + 1)),
                  pl.BlockSpec((T, V_DIM), lambda h, j: (0, h)),
                  stat, stat],
        out_specs=(pl.BlockSpec((1, T, QC), head),
                   pl.BlockSpec((tt, 256), lambda h, j: (j, h)),
                   pl.BlockSpec((1, tt, 128), lambda h, j: (h, j, 0))),
        scratch_shapes=[pltpu.VMEM((nt, QC, tt), F32), pltpu.VMEM((tt, QC), F32), pltpu.VMEM((tt, V_DIM), F32),
                        pltpu.VMEM((2, tt, tt), F32), pltpu.VMEM((2, tt, tt), F32)],
        compiler_params=_params(("parallel", "arbitrary")),
    )(qc, kc, kv, do, lse, delta)


def _adamw(land, w, m, v, name, rows, own=None):
    R, C = w.shape
    assert R % rows == 0
    n_land = land.shape[0]
    c1 = 1.0 - ADAM_B1 ** ADAM_STEP
    c2 = 1.0 - ADAM_B2 ** ADAM_STEP
    has_own = own is not None

    def body(*refs):
        land_ref = refs[0]
        w_ref, m_ref, v_ref, g_ref, d_ref, nm_ref, nv_ref = refs[-7:]
        if has_own:
            g = refs[1][...].astype(F32) + land_ref[0].astype(F32)
        else:
            g = land_ref[0].astype(F32)
        for s in range(1, n_land):
            g = g + land_ref[s].astype(F32)
        nm = ADAM_B1 * m_ref[...] + (1.0 - ADAM_B1) * g
        nv = ADAM_B2 * v_ref[...] + (1.0 - ADAM_B2) * (g * g)
        g_ref[...] = g
        nm_ref[...] = nm
        nv_ref[...] = nv
        d_ref[...] = -ADAM_LR * ((nm / c1) / (jnp.sqrt(nv / c2) + ADAM_EPS) + ADAM_WD * w_ref[...])

    blk = pl.BlockSpec((rows, C), lambda i: (i, 0))
    out = jax.ShapeDtypeStruct((R, C), F32)
    return pl.pallas_call(
        body, name=name, out_shape=(out, out, out, out), grid=(R // rows,),
        in_specs=[pl.BlockSpec((n_land, rows, C), lambda i: (0, i, 0))] + [blk] * (4 if has_own else 3),
        out_specs=(blk, blk, blk, blk),
        compiler_params=_params(("parallel",)),
    )(land, *([own] if has_own else []), w, m, v)


def _mesh_pos():
    return lax.axis_index("x"), lax.axis_index("y"), lax.axis_index("c")


def _all_gather(arrays, name):
    n = len(arrays)

    def body(*refs):
        ins, outs = refs[:n], refs[n:2 * n]
        send_sems, recv_sems, local_sems = refs[2 * n:]
        x, y, c = _mesh_pos()
        me, sibling = (x, y, c), (x, y, 1 - c)
        chips = [(1 - x, y), (x, 1 - y), (1 - x, 1 - y)]

        def slot(a, pos):
            px, py, pc = pos
            return outs[a].at[4 * px + 2 * py + pc]

        def copy(a, k, block, to, src=None):
            return pltpu.make_async_remote_copy(
                src_ref=slot(a, block) if src is None else src, dst_ref=slot(a, block),
                send_sem=send_sems.at[a * 7 + k], recv_sem=recv_sems.at[a * 7 + k],
                device_id=to, device_id_type=MESH_ID)

        mine, first, passed = [], [], []
        for a in range(n):
            cp = pltpu.make_async_copy(ins[a], slot(a, me), local_sems.at[a])
            cp.start()
            mine.append(cp)
            cps = [copy(a, 0, me, sibling, src=ins[a])]
            cps += [copy(a, 1 + j, me, (*chip, c), src=ins[a]) for j, chip in enumerate(chips)]
            for cp in cps:
                cp.start()
            first += cps
        for j, chip in enumerate(chips):
            for a in range(n):
                copy(a, 1 + j, (*chip, c), me).wait_recv()
                cp = copy(a, 4 + j, (*chip, c), sibling)
                cp.start()
                passed.append(cp)
        for a in range(n):
            copy(a, 0, sibling, me).wait_recv()
            for j, chip in enumerate(chips):
                copy(a, 4 + j, (*chip, 1 - c), me).wait_recv()
        for cp in first + passed:
            cp.wait_send()
        for cp in mine:
            cp.wait()

    hbm = pl.BlockSpec(memory_space=pltpu.HBM)
    return pl.pallas_call(
        body, name=name,
        out_shape=tuple(jax.ShapeDtypeStruct((N_DEV,) + a.shape, a.dtype) for a in arrays),
        in_specs=[hbm] * n, out_specs=tuple([hbm] * n),
        scratch_shapes=[pltpu.SemaphoreType.DMA((7 * n,)), pltpu.SemaphoreType.DMA((7 * n,)),
                        pltpu.SemaphoreType.DMA((n,))],
    )(*arrays)


def _exchange(arrays, name):
    n = len(arrays)

    def body(*refs):
        ins, outs = refs[:n], refs[n:2 * n]
        send_sems, recv_sems, local_sems = refs[2 * n:]
        x, y, c = _mesh_pos()
        my_idx = 4 * x + 2 * y + c
        copies, local = [], []
        for a in range(n):
            cp = pltpu.make_async_copy(ins[a].at[my_idx], outs[a].at[my_idx], local_sems.at[a])
            cp.start()
            local.append(cp)
            for k in range(1, N_DEV):
                px = 1 - x if k & 4 else x
                py = 1 - y if k & 2 else y
                pc = 1 - c if k & 1 else c
                cp = pltpu.make_async_remote_copy(
                    src_ref=ins[a].at[4 * px + 2 * py + pc], dst_ref=outs[a].at[my_idx],
                    send_sem=send_sems.at[a * 7 + k - 1], recv_sem=recv_sems.at[a * 7 + k - 1],
                    device_id=(px, py, pc), device_id_type=MESH_ID)
                cp.start()
                copies.append(cp)
        for cp in copies:
            cp.wait()
        for cp in local:
            cp.wait()

    hbm = pl.BlockSpec(memory_space=pltpu.HBM)
    return pl.pallas_call(
        body, name=name,
        out_shape=tuple(jax.ShapeDtypeStruct(a.shape, a.dtype) for a in arrays),
        in_specs=[hbm] * n, out_specs=tuple([hbm] * n),
        scratch_shapes=[pltpu.SemaphoreType.DMA((7 * n,)), pltpu.SemaphoreType.DMA((7 * n,)),
                        pltpu.SemaphoreType.DMA((n,))],
    )(*arrays)


_HBM = pl.BlockSpec(memory_space=pltpu.HBM)
_SEM = pl.BlockSpec(memory_space=pltpu.SEMAPHORE)
_EFFECT = pltpu.SideEffectType.DATAFLOW_SIDE_EFFECTING
N_PEERS = N_DEV - 1


def _peer(k):
    x, y, c = _mesh_pos()
    return (1 - x if k & 4 else x, 1 - y if k & 2 else y, 1 - c if k & 1 else c)


def _split_start(srcs, scatter, after, name):
    n = len(srcs)
    zones = [jax.ShapeDtypeStruct(((N_PEERS,) + s.shape[1:]) if scatter else ((N_DEV,) + s.shape), s.dtype)
             for s in srcs]

    def body(*refs):
        src, zone = refs[:n], refs[n:2 * n]
        outs = refs[2 * n + 1:]
        send, recv, token = outs[:n], outs[n:2 * n], outs[4 * n]
        x, y, c = _mesh_pos()
        my_idx = 4 * x + 2 * y + c
        for a in range(n):
            for k in range(1, N_DEV):
                px, py, pc = _peer(k)
                pltpu.make_async_remote_copy(
                    src_ref=src[a].at[4 * px + 2 * py + pc] if scatter else src[a],
                    dst_ref=zone[a].at[k - 1] if scatter else zone[a].at[my_idx],
                    send_sem=send[a], recv_sem=recv[a], device_id=(px, py, pc), device_id_type=MESH_ID).start()
        token[...] = jnp.zeros_like(token)

    hbm = lambda a: pltpu.with_memory_space_constraint(a, pltpu.HBM)
    outs = pl.pallas_call(
        body, name=name,
        out_shape=tuple([pltpu.SemaphoreType.DMA(())] * (2 * n)
                        + [pltpu.HBM(s.shape, s.dtype) for s in srcs]
                        + [pltpu.HBM(z.shape, z.dtype) for z in zones]
                        + [jax.ShapeDtypeStruct((8, 128), F32)]),
        in_specs=[_HBM] * (2 * n) + [pl.BlockSpec(memory_space=pl.ANY)],
        out_specs=tuple([_SEM] * (2 * n) + [_HBM] * (2 * n) + [pl.BlockSpec(memory_space=pltpu.VMEM)]),
        input_output_aliases={**{a: 2 * n + a for a in range(n)}, **{n + a: 3 * n + a for a in range(n)}},
        compiler_params=pltpu.CompilerParams(has_side_effects=_EFFECT),
    )(*[hbm(s) for s in srcs], *[hbm(lax.empty(z.shape, z.dtype)) for z in zones], after)
    return outs[:n], outs[n:2 * n], outs[2 * n:3 * n], outs[3 * n:4 * n], outs[4 * n]


def _split_wait(send, recv, srcs, zones, after, name):
    n = len(srcs)

    def body(*refs):
        zone = refs[n:2 * n]
        send_sems, recv_sems = refs[2 * n:3 * n], refs[3 * n:4 * n]
        x, y, c = _mesh_pos()
        for a in range(n):
            seven = zone[a].at[pl.ds(0, N_PEERS)]
            cp = pltpu.make_async_remote_copy(src_ref=seven, dst_ref=seven, send_sem=send_sems[a],
                                              recv_sem=recv_sems[a], device_id=(x, y, 1 - c),
                                              device_id_type=MESH_ID)
            cp.wait_send()
            cp.wait_recv()

    outs = pl.pallas_call(
        body, name=name,
        out_shape=tuple([pltpu.HBM(s.shape, s.dtype) for s in srcs] + [pltpu.HBM(z.shape, z.dtype) for z in zones]),
        in_specs=[_HBM] * (2 * n) + [_SEM] * (2 * n) + [pl.BlockSpec(memory_space=pl.ANY)],
        out_specs=tuple([_HBM] * (2 * n)),
        input_output_aliases={a: a for a in range(2 * n)},
        compiler_params=pltpu.CompilerParams(has_side_effects=_EFFECT),
    )(*srcs, *zones, *send, *recv, after)
    return outs[:n], outs[n:]


def _cat_blocks(g, axis):
    return jnp.concatenate([g[d] for d in range(N_DEV)], axis=axis)


def _permute_w_in(w):
    q_lat, kv_lat, k_rope, rest = w[:, 0:512], w[:, 512:768], w[:, 768:832], w[:, 832:]
    g_mla, others = rest[:, 0:1024], rest[:, 1024:]
    pad = jnp.zeros((w.shape[0], NPP - D_IN_PROJ), w.dtype)
    return jnp.concatenate([g_mla, q_lat, others, kv_lat, k_rope, pad], axis=1)


def _unpermute_w_in(w):
    return jnp.concatenate([w[:, O_QLAT:O_PIN], w[:, O_KVLAT:O_KROPE], w[:, O_KROPE:O_KROPE + ROPE],
                            w[:, O_GMLA:O_QLAT], w[:, O_PIN:O_KVLAT]], axis=1)


def _permute_w_uq(w):
    w3 = w.reshape(w.shape[0], N_HEADS, NOPE + ROPE)
    return jnp.concatenate([w3[:, :, :NOPE].reshape(w.shape[0], -1), w3[:, :, NOPE:].reshape(w.shape[0], -1)], axis=1)


def _unpermute_w_uq(w):
    nope = w[:, :N_HEADS * NOPE].reshape(w.shape[0], N_HEADS, NOPE)
    rope = w[:, N_HEADS * NOPE:].reshape(w.shape[0], N_HEADS, ROPE)
    return jnp.concatenate([nope, rope], axis=2).reshape(w.shape[0], -1)


_SMALL = (("emb_ln_g", 16), ("emb_ln_b", 16), ("q_norm_g", 8), ("kv_norm_g", 8), ("w_pool", 1024),
          ("pool_scale", 8), ("b_out", 32), ("ln_g", 32), ("ln_b", 32))
SMALL_ROWS = sum(r for _, r in _SMALL)


def _pack_small(d):
    parts = []
    for name, rows in _SMALL:
        flat = d[name].reshape(-1)
        flat = jnp.pad(flat, (0, rows * 128 - flat.shape[0]))
        parts.append(flat.reshape(rows, 128))
    return jnp.concatenate(parts, axis=0)


def _unpack_small(packed, shapes):
    out, r0 = {}, 0
    for name, rows in _SMALL:
        size = 1
        for s in shapes[name]:
            size *= s
        out[name] = packed[r0:r0 + rows].reshape(-1)[:size].reshape(shapes[name])
        r0 += rows
    return out


def _rope_tables(positions):
    half = ROPE // 2
    inv_freq = ROPE_THETA ** (-jnp.arange(half, dtype=F32) / half)
    ang = positions.astype(F32)[:, None] * inv_freq
    cos, sin = jnp.cos(ang), jnp.sin(ang)
    return jnp.concatenate([cos, cos, cos, cos], axis=1), jnp.concatenate([-sin, sin, -sin, sin], axis=1)


def _tie(value, token):
    if token is None:
        return value
    return lax.optimization_barrier((value, token))[0]


def _local_step(x, positions, target, emb_g, emb_b, layer_weights, on_sharded_grads):
    cos_t, sin_t = _rope_tables(positions)
    h, hb = _ln_fwd(x, emb_g, emb_b, "emb_ln_fwd")
    saved = []
    for l in range(DEPTH):
        W = layer_weights(l, h)
        proj = _mm(hb, W["w_in"], "nn", F32, "proj_fwd")
        qn, kvn, pooled, cv, ypc = _mix_fwd(proj, W["q_norm_g"], W["kv_norm_g"], W["w_pool"], W["pool_scale"],
                                            W["conv_w"], "mix_fwd")
        q = _mm(qn, W["w_uq"], "nn", F32, "q_up_fwd")
        kv = _mm(kvn, W["w_ukv"], "nn", BF16, "kv_up_fwd")
        qc, kc = _rope_fwd(q, kv, proj, cos_t, sin_t, "rope_fwd")
        o, ymla, lse = _flash_fwd(qc, kc, kv, proj, "flash_fwd")
        mix = jnp.concatenate([ymla, ypc], axis=1)
        z = _mm(mix, W["w_out"], "nn", F32, "out_fwd", res=h, bias=W["b_out"], alpha=ALPHA)
        saved.append((W, hb, proj, qn, kvn, pooled, cv, kv, qc, kc, o, lse, mix, z))
        h, hb = _ln_fwd(z, W["ln_g"], W["ln_b"], "ln_fwd")
    sq, dh = _loss_head(h, target, "loss_head")

    grads = {k: [None] * DEPTH for k in ("q_norm_g", "kv_norm_g", "w_pool", "pool_scale", "conv_w", "b_out", "ln_g",
                                         "ln_b")}
    for l in reversed(range(DEPTH)):
        W, hb_in, proj, qn, kvn, pooled, cv, kv, qc, kc, o, lse, mix, z = saved[l]
        sharded = {}
        dz, dzb, grads["ln_g"][l], grads["ln_b"][l], grads["b_out"][l] = _ln_bwd(dh, z, W["ln_g"], "ln_bwd")
        dmix = _mm(dzb, W["w_out"], "nt", F32, "out_bwd_x")
        sharded["w_out"] = _mm(mix, dzb, "tn", GRAD_XFER, "out_bwd_w")
        do, delta, dgm, dmid, grads["w_pool"][l], grads["pool_scale"][l], grads["conv_w"][l] = _mix_bwd(
            dmix, proj, o, pooled, cv, W["w_pool"], W["pool_scale"], W["conv_w"], "mix_bwd")
        dqc, dkv, dkr = _flash_bwd(qc, kc, kv, do, lse, delta, "flash_bwd")
        dq, dkrope = _rope_bwd(dqc, dkr, cos_t, sin_t, "rope_bwd")
        dqn = _mm(dq, W["w_uq"], "nt", F32, "q_up_bwd_x")
        sharded["w_uq"] = _mm(qn, dq, "tn", GRAD_XFER, "q_up_bwd_w")
        dkvn = _mm(dkv, W["w_ukv"], "nt", F32, "kv_up_bwd_x")
        sharded["w_ukv"] = _mm(kvn, dkv, "tn", GRAD_XFER, "kv_up_bwd_w")
        dql, dkvl, grads["q_norm_g"][l], grads["kv_norm_g"][l] = _rms_bwd(
            proj, dqn, dkvn, W["q_norm_g"], W["kv_norm_g"], "rms_bwd")
        dproj = jnp.concatenate([dgm, dql, dmid, dkvl, dkrope], axis=1)
        sharded["w_in"] = _mm(hb_in, dproj, "tn", GRAD_XFER, "proj_bwd_w")
        token = on_sharded_grads(l, sharded)
        dh = _mm(_tie(dproj, token), W["w_in"], "nt", F32, "proj_bwd_x", res=dz, alpha=ALPHA)
    grad_x, _, grads["emb_ln_g"], grads["emb_ln_b"], _ = _ln_bwd(dh, x, emb_g, "emb_ln_bwd")
    return sq, grad_x, grads


def kernel(x, positions, emb_ln_g, emb_ln_b, w_in, q_norm_g, kv_norm_g, w_uq, w_ukv, w_pool, pool_scale, conv_w, w_out, b_out, ln_g, ln_b, loss_target, m_emb_ln_g, m_emb_ln_b, m_w_in, m_q_norm_g, m_kv_norm_g, m_w_uq, m_w_ukv, m_w_pool, m_pool_scale, m_conv_w, m_w_out, m_b_out, m_ln_g, m_ln_b, v_emb_ln_g, v_emb_ln_b, v_w_in, v_q_norm_g, v_kv_norm_g, v_w_uq, v_w_ukv, v_w_pool, v_pool_scale, v_conv_w, v_w_out, v_b_out, v_ln_g, v_ln_b):
    weights = dict(emb_ln_g=emb_ln_g, emb_ln_b=emb_ln_b, w_in=w_in, q_norm_g=q_norm_g, kv_norm_g=kv_norm_g,
                   w_uq=w_uq, w_ukv=w_ukv, w_pool=w_pool, pool_scale=pool_scale, conv_w=conv_w, w_out=w_out,
                   b_out=b_out, ln_g=ln_g, ln_b=ln_b)
    mom1 = dict(emb_ln_g=m_emb_ln_g, emb_ln_b=m_emb_ln_b, w_in=m_w_in, q_norm_g=m_q_norm_g, kv_norm_g=m_kv_norm_g,
                w_uq=m_w_uq, w_ukv=m_w_ukv, w_pool=m_w_pool, pool_scale=m_pool_scale, conv_w=m_conv_w,
                w_out=m_w_out, b_out=m_b_out, ln_g=m_ln_g, ln_b=m_ln_b)
    mom2 = dict(emb_ln_g=v_emb_ln_g, emb_ln_b=v_emb_ln_b, w_in=v_w_in, q_norm_g=v_q_norm_g, kv_norm_g=v_kv_norm_g,
                w_uq=v_w_uq, w_ukv=v_w_ukv, w_pool=v_w_pool, pool_scale=v_pool_scale, conv_w=v_conv_w,
                w_out=v_w_out, b_out=v_b_out, ln_g=v_ln_g, ln_b=v_ln_b)

    big = ("w_in", "w_uq", "w_ukv", "w_out")
    my_idx = 4 * lax.axis_index("x") + 2 * lax.axis_index("y") + lax.axis_index("c")

    conv_pad = jnp.zeros((8, 128), F32).at[0:DEPTH * CONV_WIDTH, 0:64].set(conv_w.reshape(DEPTH * CONV_WIDTH, 64))
    gathered0 = _all_gather([weights[k][0].astype(BF16) for k in big] + [conv_pad], "weights0_all_gather")
    w1 = _split_start([weights[k][1].astype(BF16) for k in big], False, gathered0[0], "weights1_start")
    conv_full = _cat_blocks(gathered0[4][:, 0:DEPTH * CONV_WIDTH, 0:64], 1).reshape(DEPTH, CONV_WIDTH, D_CONV)
    conv_full = jnp.pad(conv_full, ((0, 0), (0, 8 - CONV_WIDTH), (0, 0)))

    def layer_weights(l, ready):
        if l == 0:
            blocks = gathered0[:4]
        else:
            own, zones = _split_wait(*w1[:4], ready, "weights1_wait")
            blocks = [lax.dynamic_update_index_in_dim(z, o, my_idx, 0) for z, o in zip(zones, own)]
        return dict(
            w_in=_permute_w_in(_cat_blocks(blocks[0], 1)), w_uq=_permute_w_uq(_cat_blocks(blocks[1], 1)),
            w_ukv=_cat_blocks(blocks[2], 1), w_out=_cat_blocks(blocks[3], 0), conv_w=conv_full[l],
            q_norm_g=q_norm_g[l].reshape(1, -1), kv_norm_g=kv_norm_g[l].reshape(1, -1),
            w_pool=w_pool[l].astype(BF16), pool_scale=pool_scale[l].reshape(1, -1), b_out=b_out[l].reshape(1, -1),
            ln_g=ln_g[l].reshape(1, -1), ln_b=ln_b[l].reshape(1, -1))

    in_flight = {}

    def on_sharded_grads(l, g):
        by_dest = [
            _unpermute_w_in(g["w_in"]).reshape(D_MODEL, N_DEV, -1).transpose(1, 0, 2),
            _unpermute_w_uq(g["w_uq"]).reshape(Q_LORA, N_DEV, -1).transpose(1, 0, 2),
            g["w_ukv"].reshape(KV_LORA, N_DEV, -1).transpose(1, 0, 2),
            g["w_out"].reshape(N_DEV, -1, D_MODEL),
        ]
        started = _split_start(by_dest, True, by_dest[3], "grads%d_start" % l)
        in_flight[l] = started[:4]
        return started[4]

    sq, grad_x, G = _local_step(_tie(x[0], w1[4]), positions[0], loss_target[0], emb_ln_g.reshape(1, -1),
                                emb_ln_b.reshape(1, -1), layer_weights, on_sharded_grads)
    loss = lax.psum(sq[0, 0] * (0.5 / D_MODEL), ("x", "y", "c"))

    res = {}
    landed = {}
    for l in reversed(range(DEPTH)):
        srcs, zones = _split_wait(*in_flight[l], grad_x, "grads%d_wait" % l)
        landed[l] = ([lax.dynamic_index_in_dim(s, my_idx, 0, keepdims=False) for s in srcs], zones)
    for a, (name, rows) in enumerate((("w_in", 256), ("w_uq", 256), ("w_ukv", 256), ("w_out", 128))):
        per_layer = [_adamw(landed[l][1][a], weights[name][l], mom1[name][l], mom2[name][l], "adamw_" + name, rows,
                            own=landed[l][0][a]) for l in range(DEPTH)]
        res[name] = tuple(jnp.stack([per_layer[l][i] for l in range(DEPTH)]) for i in range(4))

    d_conv = jnp.stack([G["conv_w"][l][0:CONV_WIDTH] for l in range(DEPTH)])
    d_conv = d_conv.reshape(DEPTH * CONV_WIDTH, N_DEV, 64).transpose(1, 0, 2)
    d_conv = jnp.zeros((N_DEV, 8, 128), F32).at[:, 0:DEPTH * CONV_WIDTH, 0:64].set(d_conv)
    small = dict(emb_ln_g=G["emb_ln_g"], emb_ln_b=G["emb_ln_b"])
    for k in ("q_norm_g", "kv_norm_g", "w_pool", "pool_scale", "b_out", "ln_g", "ln_b"):
        small[k] = jnp.stack(G[k])
    d_small = jnp.broadcast_to(_pack_small(small)[None], (N_DEV, SMALL_ROWS, 128))
    l_conv, l_small = _exchange([d_conv, d_small], "small_gradient_exchange")
    conv_shard = lambda a: jnp.zeros((8, 128), F32).at[0:DEPTH * CONV_WIDTH, 0:64].set(a.reshape(-1, 64))
    conv_res = _adamw(l_conv, conv_shard(conv_w), conv_shard(m_conv_w), conv_shard(v_conv_w), "adamw_conv_w", 8)
    res["conv_w"] = tuple(o[0:DEPTH * CONV_WIDTH, 0:64].reshape(DEPTH, CONV_WIDTH, 64) for o in conv_res)
    small_res = _adamw(l_small, _pack_small(weights), _pack_small(mom1), _pack_small(mom2), "adamw_small", 392)
    shapes = {k: weights[k].shape for k, _ in _SMALL}
    unpacked = [_unpack_small(o, shapes) for o in small_res]
    for k, _ in _SMALL:
        res[k] = tuple(u[k] for u in unpacked)

    order = ("emb_ln_g", "emb_ln_b", "w_in", "q_norm_g", "kv_norm_g", "w_uq", "w_ukv", "w_pool", "pool_scale",
             "conv_w", "w_out", "b_out", "ln_g", "ln_b")
    return (loss, grad_x[None], *[res[k][0] for k in order], *[res[k][1] for k in order],
            *[res[k][2] for k in order], *[res[k][3] for k in order])
```

```python
import jax
import jax.numpy as jnp
from jax import lax
from jax.experimental import pallas as pl
from jax.experimental.pallas import tpu as pltpu

F32 = jnp.float32
BF16 = jnp.bfloat16

N_DEV = 8
D_MODEL = 2048
DEPTH = 2
N_HEADS = 8
NOPE = 128
ROPE = 64
V_DIM = 128
Q_LORA = 512
KV_LORA = 256
D_MLA = N_HEADS * V_DIM
D_POOL = 512
D_CONV = 512
POOL_WINDOWS = (2, 4, 8, 16)
POOL_GROUP = 128
CONV_WIDTH = 3
D_MIX = D_MLA + D_POOL + D_CONV
D_IN_PROJ = 4928
ROPE_THETA = 10000.0
LN_EPS = 1e-5
RMS_EPS = 1e-6
ALPHA = (2 * DEPTH) ** 0.25
ATTN_SCALE = (NOPE + ROPE) ** -0.5
ADAM_LR = 0.001
ADAM_B1 = 0.9
ADAM_B2 = 0.999
ADAM_EPS = 1e-08
ADAM_WD = 0.01
ADAM_STEP = 10

O_GMLA, O_QLAT, O_PIN, O_GPOOL, O_CH, O_CB, O_CC, O_GCONV, O_KVLAT, O_KROPE = (
    0, 1024, 1536, 2048, 2560, 3072, 3584, 4096, 4608, 4864)
NPP = 5120
QC = NOPE + 2 * ROPE
HALO = 16
ATT_TILE = 512
ATT_CH = 256
LOG2E = 1.4426950408889634
EXP2_SCALE = ATTN_SCALE * LOG2E

GRAD_XFER = BF16
VMEM_LIMIT = 48 * 1024 * 1024
MESH_ID = pl.DeviceIdType.MESH


def _params(sem=None):
    return pltpu.CompilerParams(dimension_semantics=sem, vmem_limit_bytes=VMEM_LIMIT)


def _sigmoid(x):
    return 1.0 / (1.0 + jnp.exp(-x))


def _tile(dim, target):
    if dim <= target:
        return dim
    t = target - target % 128
    while dim % t:
        t -= 128
    return t


_DIMS = {"nn": (((1,), (0,)), ((), ())), "nt": (((1,), (1,)), ((), ())), "tn": (((0,), (0,)), ((), ()))}


def _mm(a, b, mode, out_dtype, name, res=None, bias=None, alpha=1.0, tm=1024, tn=1024, tk=512, after=None):
    if mode == "nn":
        (M, K), (K2, N) = a.shape, b.shape
    elif mode == "nt":
        (M, K), (N, K2) = a.shape, b.shape
    else:
        (K, M), (K2, N) = a.shape, b.shape
    assert K == K2
    tm, tn, tk = _tile(M, tm), _tile(N, tn), _tile(K, tk)
    nk = K // tk
    has_res, has_bias = res is not None, bias is not None

    def body(*refs):
        a_ref, b_ref = refs[0], refs[1]
        pos = 2
        res_ref = bias_ref = None
        if has_res:
            res_ref = refs[pos]
            pos += 1
        if has_bias:
            bias_ref = refs[pos]
            pos += 1
        o_ref, acc_ref = refs[-2], refs[-1]
        k = pl.program_id(2)

        @pl.when(k == 0)
        def _():
            acc_ref[...] = jnp.zeros_like(acc_ref)

        acc_ref[...] += lax.dot_general(a_ref[...].astype(BF16), b_ref[...].astype(BF16), _DIMS[mode],
                                        preferred_element_type=F32)

        @pl.when(k == nk - 1)
        def _():
            r = acc_ref[...]
            if has_bias:
                r = r + bias_ref[...]
            if has_res:
                r = alpha * res_ref[...] + r
            o_ref[...] = r.astype(out_dtype)

    if mode == "nn":
        in_specs = [pl.BlockSpec((tm, tk), lambda i, j, k: (i, k)), pl.BlockSpec((tk, tn), lambda i, j, k: (k, j))]
    elif mode == "nt":
        in_specs = [pl.BlockSpec((tm, tk), lambda i, j, k: (i, k)), pl.BlockSpec((tn, tk), lambda i, j, k: (j, k))]
    else:
        in_specs = [pl.BlockSpec((tk, tm), lambda i, j, k: (k, i)), pl.BlockSpec((tk, tn), lambda i, j, k: (k, j))]
    args = [a, b]
    if has_res:
        in_specs.append(pl.BlockSpec((tm, tn), lambda i, j, k: (i, j)))
        args.append(res)
    if has_bias:
        in_specs.append(pl.BlockSpec((1, tn), lambda i, j, k: (0, j)))
        args.append(bias)
    if after is not None:
        in_specs.append(pl.BlockSpec((8, 128), lambda i, j, k: (0, 0)))
        args.append(after)
    return pl.pallas_call(
        body, name=name,
        out_shape=jax.ShapeDtypeStruct((M, N), out_dtype),
        grid=(M // tm, N // tn, nk),
        in_specs=in_specs,
        out_specs=pl.BlockSpec((tm, tn), lambda i, j, k: (i, j)),
        scratch_shapes=[pltpu.VMEM((tm, tn), F32)],
        compiler_params=_params(("parallel", "parallel", "arbitrary")),
    )(*args)


def _ln_fwd(z, g, b, name, tq=256, after=None):
    T, D = z.shape

    def body(z_ref, g_ref, b_ref, *rest):
        y_ref, yb_ref = rest[-2:]
        zv = z_ref[...]
        mu = jnp.mean(zv, axis=1, keepdims=True)
        zc = zv - mu
        var = jnp.mean(zc * zc, axis=1, keepdims=True)
        y = zc * lax.rsqrt(var + LN_EPS) * g_ref[...] + b_ref[...]
        y_ref[...] = y
        yb_ref[...] = y.astype(BF16)

    row = pl.BlockSpec((tq, D), lambda i: (i, 0))
    vec = pl.BlockSpec((1, D), lambda i: (0, 0))
    return pl.pallas_call(
        body, name=name,
        out_shape=(jax.ShapeDtypeStruct((T, D), F32), jax.ShapeDtypeStruct((T, D), BF16)),
        grid=(T // tq,),
        in_specs=[row, vec, vec] + ([pl.BlockSpec((8, 128), lambda i: (0, 0))] if after is not None else []),
        out_specs=(row, row),
        compiler_params=_params(("parallel",)),
    )(z, g, b, *([after] if after is not None else []))


def _ln_bwd(dy, z, g, name, tq=256):
    T, D = z.shape

    def body(dy_ref, z_ref, g_ref, dz_ref, dzb_ref, dg_ref, db_ref, ds_ref):
        @pl.when(pl.program_id(0) == 0)
        def _():
            dg_ref[...] = jnp.zeros_like(dg_ref)
            db_ref[...] = jnp.zeros_like(db_ref)
            ds_ref[...] = jnp.zeros_like(ds_ref)

        zv, dyv = z_ref[...], dy_ref[...]
        mu = jnp.mean(zv, axis=1, keepdims=True)
        zc = zv - mu
        var = jnp.mean(zc * zc, axis=1, keepdims=True)
        rstd = lax.rsqrt(var + LN_EPS)
        xh = zc * rstd
        u = dyv * g_ref[...]
        dz = rstd * (u - jnp.mean(u, axis=1, keepdims=True) - xh * jnp.mean(u * xh, axis=1, keepdims=True))
        dz_ref[...] = dz
        dzb_ref[...] = dz.astype(BF16)
        dg_ref[...] += jnp.sum(dyv * xh, axis=0, keepdims=True)
        db_ref[...] += jnp.sum(dyv, axis=0, keepdims=True)
        ds_ref[...] += jnp.sum(dz, axis=0, keepdims=True)

    row = pl.BlockSpec((tq, D), lambda i: (i, 0))
    vec = pl.BlockSpec((1, D), lambda i: (0, 0))
    vshape = jax.ShapeDtypeStruct((1, D), F32)
    return pl.pallas_call(
        body, name=name,
        out_shape=(jax.ShapeDtypeStruct((T, D), F32), jax.ShapeDtypeStruct((T, D), BF16), vshape, vshape, vshape),
        grid=(T // tq,), in_specs=[row, row, vec], out_specs=(row, row, vec, vec, vec),
        compiler_params=_params(("arbitrary",)),
    )(dy, z, g)


def _loss_head(y, target, name, tq=256):
    T, D = y.shape

    def body(y_ref, t_ref, s_ref, dy_ref):
        @pl.when(pl.program_id(0) == 0)
        def _():
            s_ref[...] = jnp.zeros_like(s_ref)

        err = y_ref[...] - t_ref[...]
        s_ref[...] += jnp.sum(err * err)
        dy_ref[...] = err * (1.0 / D)

    row = pl.BlockSpec((tq, D), lambda i: (i, 0))
    acc = pl.BlockSpec((8, 128), lambda i: (0, 0))
    return pl.pallas_call(
        body, name=name,
        out_shape=(jax.ShapeDtypeStruct((8, 128), F32), jax.ShapeDtypeStruct((T, D), F32)),
        grid=(T // tq,), in_specs=[row, row], out_specs=(acc, row),
        compiler_params=_params(("arbitrary",)),
    )(y, target)


def _pblock(tq, width, offset):
    assert offset % width == 0
    blk = offset // width
    return pl.BlockSpec((tq, width), lambda i: (i, blk))


def _mix_fwd(proj, q_g, kv_g, w_pool, pool_scale, conv_w, name, tq=256):
    T = proj.shape[0]

    def body(ql_ref, kvl_ref, pin_ref, gp_ref, ch_ref, cb_ref, cc_ref, gc_ref, qg_ref, kvg_ref, wp_ref, ps_ref,
             cw_ref, qn_ref, kvn_ref, pooled_ref, cv_ref, ypc_ref, extp, extu):
        i = pl.program_id(0)
        for x_ref, g_ref, o_ref in ((ql_ref, qg_ref, qn_ref), (kvl_ref, kvg_ref, kvn_ref)):
            x = x_ref[...]
            r = lax.rsqrt(jnp.mean(x * x, axis=1, keepdims=True) + RMS_EPS)
            o_ref[...] = (x * r * g_ref[...]).astype(BF16)

        @pl.when(i == 0)
        def _():
            extp[0:HALO, :] = jnp.zeros((HALO, D_POOL), F32)
            extu[0:HALO, :] = jnp.zeros((HALO, D_CONV), F32)

        @pl.when(i > 0)
        def _():
            extp[0:HALO, :] = extp[tq:tq + HALO, :]
            extu[0:HALO, :] = extu[tq:tq + HALO, :]

        pin = pin_ref[...]
        extp[HALO:, :] = pin
        u = cc_ref[...] * ch_ref[...]
        extu[HALO:, :] = u
        t1 = (i * tq + lax.broadcasted_iota(jnp.int32, (tq, 1), 0) + 1).astype(F32)
        for g, w in enumerate(POOL_WINDOWS):
            cols = slice(g * POOL_GROUP, (g + 1) * POOL_GROUP)
            s = extp[:, cols]
            k = 1
            while k < w:
                s = s + pltpu.roll(s, k, 0)
                k *= 2
            mean = s[HALO:, :] / jnp.minimum(t1, float(w))
            pooled = (mean - pin[:, cols]).astype(BF16)
            pooled_ref[:, cols] = pooled
            r = jnp.dot(pooled, wp_ref[g], preferred_element_type=F32)
            gp = gp_ref[:, cols]
            ypc_ref[:, cols] = (r * ps_ref[:, cols] * (gp * _sigmoid(gp))).astype(BF16)
        eu = extu[...]
        u1 = pltpu.roll(eu, 1, 0)[HALO:, :]
        u2 = pltpu.roll(eu, 2, 0)[HALO:, :]
        cv = cw_ref[0:1, :] * u2 + cw_ref[1:2, :] * u1 + cw_ref[2:3, :] * u
        cv_ref[...] = cv
        gc = gc_ref[...]
        ypc_ref[:, D_POOL:] = (cb_ref[...] * cv * (gc * _sigmoid(gc))).astype(BF16)

    full = lambda shape: pl.BlockSpec(shape, lambda i: (0,) * len(shape))
    row = lambda w: pl.BlockSpec((tq, w), lambda i: (i, 0))
    return pl.pallas_call(
        body, name=name,
        out_shape=(jax.ShapeDtypeStruct((T, Q_LORA), BF16), jax.ShapeDtypeStruct((T, KV_LORA), BF16),
                   jax.ShapeDtypeStruct((T, D_POOL), BF16), jax.ShapeDtypeStruct((T, D_CONV), F32),
                   jax.ShapeDtypeStruct((T, D_POOL + D_CONV), BF16)),
        grid=(T // tq,),
        in_specs=[_pblock(tq, Q_LORA, O_QLAT), _pblock(tq, KV_LORA, O_KVLAT), _pblock(tq, 512, O_PIN),
                  _pblock(tq, 512, O_GPOOL), _pblock(tq, 512, O_CH), _pblock(tq, 512, O_CB), _pblock(tq, 512, O_CC),
                  _pblock(tq, 512, O_GCONV), full((1, Q_LORA)), full((1, KV_LORA)), full((4, 128, 128)),
                  full((1, D_POOL)), full((8, D_CONV))],
        out_specs=(row(Q_LORA), row(KV_LORA), row(D_POOL), row(D_CONV), row(D_POOL + D_CONV)),
        scratch_shapes=[pltpu.VMEM((tq + HALO, D_POOL), F32), pltpu.VMEM((tq + HALO, D_CONV), F32)],
        compiler_params=_params(("arbitrary",)),
    )(proj, proj, proj, proj, proj, proj, proj, proj, q_g, kv_g, w_pool, pool_scale, conv_w)


def _mix_bwd(dmix, proj, o, pooled, cv, w_pool, pool_scale, conv_w, name, tq=ATT_CH):
    T = proj.shape[0]
    nt = T // tq
    n_ext = tq + HALO

    def body(dym_ref, dyp_ref, dyc_ref, gm_ref, gp_ref, ch_ref, cb_ref, cc_ref, gc_ref, o_ref, pooled_ref, cv_ref,
             wp_ref, ps_ref, cw_ref, do_ref, delta_ref, dgm_ref, dmid_ref, dwp_ref, dps_ref, dcw_ref, exte, extd):
        i = pl.program_id(0)
        tile = nt - 1 - i

        @pl.when(i == 0)
        def _():
            dwp_ref[...] = jnp.zeros_like(dwp_ref)
            dps_ref[...] = jnp.zeros_like(dps_ref)
            dcw_ref[...] = jnp.zeros_like(dcw_ref)
            exte[tq:, :] = jnp.zeros((HALO, D_POOL), F32)
            extd[tq:, :] = jnp.zeros((HALO, D_CONV), F32)

        @pl.when(i > 0)
        def _():
            exte[tq:, :] = exte[0:HALO, :]
            extd[tq:, :] = extd[0:HALO, :]

        gm = gm_ref[...]
        sig = _sigmoid(gm)
        dym = dym_ref[...]
        ov = o_ref[...]
        do = dym * (gm * sig)
        do_ref[...] = do.astype(BF16)
        prod = do * ov
        ones = jnp.ones((8, V_DIM), F32)
        for h in range(N_HEADS):
            rows = lax.dot_general(ones, prod[:, h * V_DIM:(h + 1) * V_DIM], _DIMS["nt"],
                                   precision=lax.Precision.HIGHEST, preferred_element_type=F32)
            delta_ref[h, 0] = rows[0:1, :]
        dgm_ref[...] = (dym * ov * (sig * (1.0 + gm * (1.0 - sig)))).astype(BF16)

        t1 = (tile * tq + lax.broadcasted_iota(jnp.int32, (tq, 1), 0) + 1).astype(F32)
        for g, w in enumerate(POOL_WINDOWS):
            cols = slice(g * POOL_GROUP, (g + 1) * POOL_GROUP)
            pg = pooled_ref[:, cols]
            r = jnp.dot(pg, wp_ref[g], preferred_element_type=F32)
            gp = gp_ref[:, cols]
            sg = _sigmoid(gp)
            sl = gp * sg
            dyg = dyp_ref[:, cols]
            ps = ps_ref[:, cols]
            dmid_ref[:, 512 + g * POOL_GROUP:512 + (g + 1) * POOL_GROUP] = (
                dyg * (r * ps) * (sg * (1.0 + gp * (1.0 - sg)))).astype(BF16)
            dps_ref[:, cols] += jnp.sum(dyg * r * sl, axis=0, keepdims=True)
            dr = (dyg * ps * sl).astype(BF16)
            dwp_ref[g] += lax.dot_general(pg, dr, _DIMS["tn"], preferred_element_type=F32)
            dpooled = lax.dot_general(dr, wp_ref[g], _DIMS["nt"], preferred_element_type=F32)
            exte[0:tq, cols] = dpooled / jnp.minimum(t1, float(w))
            s = exte[:, cols]
            k = 1
            while k < w:
                s = s + pltpu.roll(s, n_ext - k, 0)
                k *= 2
            dmid_ref[:, cols] = (s[0:tq, :] - dpooled).astype(BF16)

        gc = gc_ref[...]
        sg = _sigmoid(gc)
        sl = gc * sg
        dyc = dyc_ref[...]
        cb, cc, ch, cvv = cb_ref[...], cc_ref[...], ch_ref[...], cv_ref[...]
        dcv = dyc * cb * sl
        dmid_ref[:, 2560:3072] = (dyc * (cb * cvv) * (sg * (1.0 + gc * (1.0 - sg)))).astype(BF16)
        dmid_ref[:, 1536:2048] = (dyc * cvv * sl).astype(BF16)
        extd[0:tq, :] = dcv
        ed = extd[...]
        d1 = pltpu.roll(ed, n_ext - 1, 0)[0:tq, :]
        d2 = pltpu.roll(ed, n_ext - 2, 0)[0:tq, :]
        du = cw_ref[2:3, :] * dcv + cw_ref[1:2, :] * d1 + cw_ref[0:1, :] * d2
        u = cc * ch
        dcw_ref[0:1, :] += jnp.sum(u * d2, axis=0, keepdims=True)
        dcw_ref[1:2, :] += jnp.sum(u * d1, axis=0, keepdims=True)
        dcw_ref[2:3, :] += jnp.sum(u * dcv, axis=0, keepdims=True)
        dmid_ref[:, 1024:1536] = (du * cc).astype(BF16)
        dmid_ref[:, 2048:2560] = (du * ch).astype(BF16)

    def rblock(width, offset):
        assert offset % width == 0
        blk = offset // width
        return pl.BlockSpec((tq, width), lambda i: (nt - 1 - i, blk))

    full = lambda shape: pl.BlockSpec(shape, lambda i: (0,) * len(shape))
    return pl.pallas_call(
        body, name=name,
        out_shape=(jax.ShapeDtypeStruct((T, D_MLA), BF16), jax.ShapeDtypeStruct((N_HEADS, nt, 1, tq), F32),
                   jax.ShapeDtypeStruct((T, D_MLA), BF16), jax.ShapeDtypeStruct((T, 3072), BF16),
                   jax.ShapeDtypeStruct((4, 128, 128), F32), jax.ShapeDtypeStruct((1, D_POOL), F32),
                   jax.ShapeDtypeStruct((8, D_CONV), F32)),
        grid=(nt,),
        in_specs=[rblock(1024, 0), rblock(512, 1024), rblock(512, 1536),
                  rblock(1024, O_GMLA), rblock(512, O_GPOOL), rblock(512, O_CH), rblock(512, O_CB),
                  rblock(512, O_CC), rblock(512, O_GCONV), rblock(1024, 0), rblock(512, 0), rblock(512, 0),
                  full((4, 128, 128)), full((1, D_POOL)), full((8, D_CONV))],
        out_specs=(rblock(1024, 0), pl.BlockSpec((N_HEADS, 1, 1, tq), lambda i: (0, nt - 1 - i, 0, 0)),
                   rblock(1024, 0),
                   rblock(3072, 0), full((4, 128, 128)), full((1, D_POOL)), full((8, D_CONV))),
        scratch_shapes=[pltpu.VMEM((n_ext, D_POOL), F32), pltpu.VMEM((n_ext, D_CONV), F32)],
        compiler_params=_params(("arbitrary",)),
    )(dmix, dmix, dmix, proj, proj, proj, proj, proj, proj, o, pooled, cv, w_pool, pool_scale, conv_w)


def _rms_bwd(proj, dqn, dkvn, q_g, kv_g, name, tq=256):
    T = proj.shape[0]

    def body(ql_ref, kvl_ref, dqn_ref, dkvn_ref, qg_ref, kvg_ref, dql_ref, dkvl_ref, dqg_ref, dkvg_ref):
        @pl.when(pl.program_id(0) == 0)
        def _():
            dqg_ref[...] = jnp.zeros_like(dqg_ref)
            dkvg_ref[...] = jnp.zeros_like(dkvg_ref)

        for x_ref, dy_ref, g_ref, dx_ref, dg_ref in ((ql_ref, dqn_ref, qg_ref, dql_ref, dqg_ref),
                                                     (kvl_ref, dkvn_ref, kvg_ref, dkvl_ref, dkvg_ref)):
            x, dy = x_ref[...], dy_ref[...]
            r = lax.rsqrt(jnp.mean(x * x, axis=1, keepdims=True) + RMS_EPS)
            xr = x * r
            u = dy * g_ref[...]
            dx_ref[...] = (r * (u - xr * jnp.mean(u * xr, axis=1, keepdims=True))).astype(BF16)
            dg_ref[...] += jnp.sum(dy * xr, axis=0, keepdims=True)

    row = lambda w: pl.BlockSpec((tq, w), lambda i: (i, 0))
    vec = lambda w: pl.BlockSpec((1, w), lambda i: (0, 0))
    return pl.pallas_call(
        body, name=name,
        out_shape=(jax.ShapeDtypeStruct((T, Q_LORA), BF16), jax.ShapeDtypeStruct((T, KV_LORA), BF16),
                   jax.ShapeDtypeStruct((1, Q_LORA), F32), jax.ShapeDtypeStruct((1, KV_LORA), F32)),
        grid=(T // tq,),
        in_specs=[_pblock(tq, Q_LORA, O_QLAT), _pblock(tq, KV_LORA, O_KVLAT), row(Q_LORA), row(KV_LORA),
                  vec(Q_LORA), vec(KV_LORA)],
        out_specs=(row(Q_LORA), row(KV_LORA), vec(Q_LORA), vec(KV_LORA)),
        compiler_params=_params(("arbitrary",)),
    )(proj, proj, dqn, dkvn, q_g, kv_g)


def _swap_halves(x, lo):
    return jnp.where(lo, pltpu.roll(x, 96, 1), pltpu.roll(x, 32, 1))


def _rope_fwd(q, kv, proj, cos_t, sin_t, name, tq=256):
    T = q.shape[0]

    def body(qn_ref, qr_ref, kv_ref, kr_ref, c_ref, s_ref, qc_ref, kc_ref):
        C, S = c_ref[...], s_ref[...]
        lane = lax.broadcasted_iota(jnp.int32, (tq, 128), 1)
        lo = (lane % ROPE) < (ROPE // 2)
        first = lane < ROPE

        def rope(x):
            return x * C + _swap_halves(x, lo) * S

        kr = jnp.where(first, rope(kr_ref[...]), 0.0).astype(BF16)
        for j in range(N_HEADS // 2):
            r = rope(qr_ref[:, j * 128:(j + 1) * 128])
            pair = (jnp.where(first, r, 0.0), jnp.where(first, pltpu.roll(r, 64, 1), 0.0))
            for hh in range(2):
                h = 2 * j + hh
                qc_ref[h, :, 0:NOPE] = qn_ref[:, h * NOPE:(h + 1) * NOPE].astype(BF16)
                qc_ref[h, :, NOPE:QC] = pair[hh].astype(BF16)
        for h in range(N_HEADS):
            kc_ref[h, :, 0:NOPE] = kv_ref[:, h * 256:h * 256 + NOPE]
            kc_ref[h, :, NOPE:QC] = kr

    out = jax.ShapeDtypeStruct((N_HEADS, T, QC), BF16)
    hblock = pl.BlockSpec((N_HEADS, tq, QC), lambda i: (0, i, 0))
    return pl.pallas_call(
        body, name=name, out_shape=(out, out), grid=(T // tq,),
        in_specs=[pl.BlockSpec((tq, 1024), lambda i: (i, 0)), pl.BlockSpec((tq, 512), lambda i: (i, 2)),
                  pl.BlockSpec((tq, 2048), lambda i: (i, 0)), _pblock(tq, 128, O_KROPE),
                  pl.BlockSpec((tq, 128), lambda i: (i, 0)), pl.BlockSpec((tq, 128), lambda i: (i, 0))],
        out_specs=(hblock, hblock),
        compiler_params=_params(("parallel",)),
    )(q, q, kv, proj, cos_t, sin_t)


def _rope_bwd(dqc, dkr, cos_t, sin_t, name, tq=256):
    T = dqc.shape[1]

    def body(dqc_ref, dkr_ref, c_ref, s_ref, dq_ref, dk_ref):
        C, S = c_ref[...], s_ref[...]
        lane = lax.broadcasted_iota(jnp.int32, (tq, 128), 1)
        lo = (lane % ROPE) < (ROPE // 2)
        first = lane < ROPE

        def unrope(dy):
            return dy * C - _swap_halves(dy, lo) * S

        acc = dkr_ref[0]
        for h in range(1, N_HEADS):
            acc = acc + dkr_ref[h]
        dk_ref[:, 0:128] = jnp.where(first, unrope(acc), 0.0).astype(BF16)
        dk_ref[:, 128:256] = jnp.zeros((tq, 128), BF16)
        for j in range(N_HEADS // 2):
            d0 = dqc_ref[2 * j, :, NOPE:QC]
            d1 = dqc_ref[2 * j + 1, :, NOPE:QC]
            comb = jnp.where(first, d0, pltpu.roll(d1, 64, 1))
            dq_ref[:, 1024 + j * 128:1024 + (j + 1) * 128] = unrope(comb).astype(BF16)
        for h in range(N_HEADS):
            dq_ref[:, h * NOPE:(h + 1) * NOPE] = dqc_ref[h, :, 0:NOPE].astype(BF16)

    tab = pl.BlockSpec((tq, 128), lambda i: (i, 0))
    return pl.pallas_call(
        body, name=name,
        out_shape=(jax.ShapeDtypeStruct((T, 1536), BF16), jax.ShapeDtypeStruct((T, 256), BF16)),
        grid=(T // tq,),
        in_specs=[pl.BlockSpec((N_HEADS, tq, QC), lambda i: (0, i, 0)),
                  pl.BlockSpec((N_HEADS, tq, 128), lambda i: (0, i, 0)), tab, tab],
        out_specs=(pl.BlockSpec((tq, 1536), lambda i: (i, 0)), pl.BlockSpec((tq, 256), lambda i: (i, 0))),
        compiler_params=_params(("parallel",)),
    )(dqc, dkr, cos_t, sin_t)


def _flash_fwd(qc, kc, kv, proj, name):
    H, T, _ = qc.shape
    tt = ATT_TILE
    nt = T // tt
    sp = tt // ATT_CH

    def body(q_ref, k_ref, v_ref, g_ref, o_ref, y_ref, lse_ref, vt_sc, s_sc, acc_sc, m_sc, l_sc):
        i = pl.program_id(1)

        @pl.when(i == 0)
        def _():
            for c in range(nt):
                vt_sc[c] = v_ref[c * tt:(c + 1) * tt, :].astype(F32).T.astype(BF16)

        q = q_ref[0]

        def issue(c, slot):
            s_sc[slot] = lax.dot_general(k_ref[0, pl.ds(pl.multiple_of(c * tt, tt), tt), :], q, _DIMS["nt"],
                                         preferred_element_type=F32)

        def softmax_pv(c, slot, masked):
            s = s_sc[slot]
            if masked:
                krow = c * tt + lax.broadcasted_iota(jnp.int32, s.shape, 0)
                qcol = i * tt + lax.broadcasted_iota(jnp.int32, s.shape, 1)
                s = jnp.where(krow <= qcol, s, -jnp.inf)
            m = m_sc[...]
            m_new = jnp.maximum(m, jnp.max(s, axis=0, keepdims=True))
            p = jnp.exp2((s - m_new) * EXP2_SCALE)
            a = jnp.exp2((m - m_new) * EXP2_SCALE)
            l_sc[...] = a * l_sc[...] + jnp.sum(p, axis=0, keepdims=True)
            acc_sc[...] = a * acc_sc[...] + jnp.dot(vt_sc[c], p.astype(BF16), preferred_element_type=F32)
            m_sc[...] = m_new

        m_sc[...] = jnp.full_like(m_sc, -jnp.inf)
        l_sc[...] = jnp.zeros_like(l_sc)
        acc_sc[...] = jnp.zeros_like(acc_sc)
        issue(0, 0)

        def pair(t, carry):
            issue(2 * t + 1, 1)
            softmax_pv(2 * t, 0, False)
            issue(2 * t + 2, 0)
            softmax_pv(2 * t + 1, 1, False)
            return carry

        lax.fori_loop(0, i // 2, pair, 0)

        @pl.when(i % 2 == 1)
        def _():
            issue(i, 1)
            softmax_pv(i - 1, 0, False)
            softmax_pv(i, 1, True)

        @pl.when(i % 2 == 0)
        def _():
            softmax_pv(i, 0, True)

        l = l_sc[...]
        o = (acc_sc[...] / l).T
        o_ref[...] = o
        lse = m_sc[...] * ATTN_SCALE + jnp.log(l)
        for r in range(sp):
            lse_ref[0, r] = lse[:, r * ATT_CH:(r + 1) * ATT_CH]
        g = g_ref[...]
        y_ref[...] = (o * (g * _sigmoid(g))).astype(BF16)

    return pl.pallas_call(
        body, name=name,
        out_shape=(jax.ShapeDtypeStruct((T, D_MLA), F32), jax.ShapeDtypeStruct((T, D_MLA), BF16),
                   jax.ShapeDtypeStruct((H, T // ATT_CH, 1, ATT_CH), F32)),
        grid=(H, nt),
        in_specs=[pl.BlockSpec((1, tt, QC), lambda h, i: (h, i, 0)),
                  pl.BlockSpec((1, T, QC), lambda h, i: (h, 0, 0)),
                  pl.BlockSpec((T, V_DIM), lambda h, i: (0, 2 * h + 1)),
                  pl.BlockSpec((tt, V_DIM), lambda h, i: (i, h))],
        out_specs=(pl.BlockSpec((tt, V_DIM), lambda h, i: (i, h)),
                   pl.BlockSpec((tt, V_DIM), lambda h, i: (i, h)),
                   pl.BlockSpec((1, sp, 1, ATT_CH), lambda h, i: (h, i, 0, 0))),
        scratch_shapes=[pltpu.VMEM((nt, V_DIM, tt), BF16), pltpu.VMEM((2, tt, tt), F32),
                        pltpu.VMEM((V_DIM, tt), F32), pltpu.VMEM((1, tt), F32), pltpu.VMEM((1, tt), F32)],
        compiler_params=_params(("parallel", "arbitrary")),
    )(qc, kc, kv, proj)


def _flash_bwd(qc, kc, kv, do, lse, delta, name):
    H, T, _ = qc.shape
    tt = ATT_TILE
    nt = T // tt
    sp = tt // ATT_CH

    def body(q_ref, k_ref, v_ref, do_ref, lse_ref, dl_ref, dq_ref, dkv_ref, dkr_ref, dqt_sc, dk_sc, dv_sc, s_sc,
             dp_sc):
        j = pl.program_id(1)

        @pl.when(j == 0)
        def _():
            dqt_sc[...] = jnp.zeros_like(dqt_sc)

        dk_sc[...] = jnp.zeros_like(dk_sc)
        dv_sc[...] = jnp.zeros_like(dv_sc)
        k = k_ref[0]
        v = v_ref[...]
        kt = k.astype(F32).T.astype(BF16)

        def operands(c):
            q0 = pl.multiple_of(c * tt, tt)
            return q_ref[0, pl.ds(q0, tt), :], do_ref[pl.ds(q0, tt), :]

        def stat_row(ref, c):
            return jnp.concatenate([ref[0, sp * c + r] for r in range(sp)], axis=1)

        def early(c, slot):
            q, dov = operands(c)
            s_sc[slot] = lax.dot_general(k, q, _DIMS["nt"], preferred_element_type=F32)
            dp_sc[slot] = lax.dot_general(v, dov, _DIMS["nt"], preferred_element_type=F32)

        def late(c, slot, masked):
            q, dov = operands(c)
            s, dp = s_sc[slot], dp_sc[slot]
            if masked:
                krow = j * tt + lax.broadcasted_iota(jnp.int32, s.shape, 0)
                qcol = c * tt + lax.broadcasted_iota(jnp.int32, s.shape, 1)
                s = jnp.where(krow <= qcol, s, -jnp.inf)
            p = jnp.exp2(s * EXP2_SCALE - stat_row(lse_ref, c) * LOG2E)
            ds = (p * (dp - stat_row(dl_ref, c)) * ATTN_SCALE).astype(BF16)
            dv_sc[...] += jnp.dot(p.astype(BF16), dov, preferred_element_type=F32)
            dk_sc[...] += jnp.dot(ds, q, preferred_element_type=F32)
            dqt_sc[c] += jnp.dot(kt, ds, preferred_element_type=F32)

        early(j, 0)

        @pl.when(j < nt - 1)
        def _():
            early(j + 1, 1)

        late(j, 0, True)
        n_rest = nt - 1 - j

        def pair(u, carry):
            a = j + 1 + 2 * u
            early(a + 1, 0)
            late(a, 1, False)

            @pl.when(a + 2 <= nt - 1)
            def _():
                early(a + 2, 1)

            late(a + 1, 0, False)
            return carry

        lax.fori_loop(0, n_rest // 2, pair, 0)

        @pl.when(n_rest % 2 == 1)
        def _():
            late(nt - 1, 1, False)

        dk = dk_sc[...]
        dkv_ref[:, 0:NOPE] = dk[:, 0:NOPE].astype(BF16)
        dkv_ref[:, NOPE:] = dv_sc[...].astype(BF16)
        dkr_ref[0] = dk[:, NOPE:]

        @pl.when(j == nt - 1)
        def _():
            for c in range(nt):
                dq_ref[0, c * tt:(c + 1) * tt, :] = dqt_sc[c].T

    head = lambda h, j: (h, 0, 0)
    stat = pl.BlockSpec((1, T // ATT_CH, 1, ATT_CH), lambda h, j: (h, 0, 0, 0))
    return pl.pallas_call(
        body, name=name,
        out_shape=(jax.ShapeDtypeStruct((H, T, QC), F32), jax.ShapeDtypeStruct((T, 2 * D_MLA), BF16),
                   jax.ShapeDtypeStruct((H, T, 128), F32)),
        grid=(H, nt),
        in_specs=[pl.BlockSpec((1, T, QC), head),
                  pl.BlockSpec((1, tt, QC), lambda h, j: (h, j, 0)),
                  pl.BlockSpec((tt, V_DIM), lambda h, j: (j, 2 * h + 1)),
                  pl.BlockSpec((T, V_DIM), lambda h, j: (0, h)),
                  stat, stat],
        out_specs=(pl.BlockSpec((1, T, QC), head),
                   pl.BlockSpec((tt, 256), lambda h, j: (j, h)),
                   pl.BlockSpec((1, tt, 128), lambda h, j: (h, j, 0))),
        scratch_shapes=[pltpu.VMEM((nt, QC, tt), F32), pltpu.VMEM((tt, QC), F32), pltpu.VMEM((tt, V_DIM), F32),
                        pltpu.VMEM((2, tt, tt), F32), pltpu.VMEM((2, tt, tt), F32)],
        compiler_params=_params(("parallel", "arbitrary")),
    )(qc, kc, kv, do, lse, delta)


def _adamw(land, w, m, v, name, rows, own=None):
    R, C = w.shape
    assert R % rows == 0
    n_land = land.shape[0]
    c1 = 1.0 - ADAM_B1 ** ADAM_STEP
    c2 = 1.0 - ADAM_B2 ** ADAM_STEP
    has_own = own is not None

    def body(*refs):
        land_ref = refs[0]
        w_ref, m_ref, v_ref, g_ref, d_ref, nm_ref, nv_ref = refs[-7:]
        if has_own:
            g = refs[1][...].astype(F32) + land_ref[0].astype(F32)
        else:
            g = land_ref[0].astype(F32)
        for s in range(1, n_land):
            g = g + land_ref[s].astype(F32)
        nm = ADAM_B1 * m_ref[...] + (1.0 - ADAM_B1) * g
        nv = ADAM_B2 * v_ref[...] + (1.0 - ADAM_B2) * (g * g)
        g_ref[...] = g
        nm_ref[...] = nm
        nv_ref[...] = nv
        d_ref[...] = -ADAM_LR * ((nm / c1) / (jnp.sqrt(nv / c2) + ADAM_EPS) + ADAM_WD * w_ref[...])

    blk = pl.BlockSpec((rows, C), lambda i: (i, 0))
    out = jax.ShapeDtypeStruct((R, C), F32)
    return pl.pallas_call(
        body, name=name, out_shape=(out, out, out, out), grid=(R // rows,),
        in_specs=[pl.BlockSpec((n_land, rows, C), lambda i: (0, i, 0))] + [blk] * (4 if has_own else 3),
        out_specs=(blk, blk, blk, blk),
        compiler_params=_params(("parallel",)),
    )(land, *([own] if has_own else []), w, m, v)


def _mesh_pos():
    return lax.axis_index("x"), lax.axis_index("y"), lax.axis_index("c")


def _all_gather(arrays, name):
    n = len(arrays)

    def body(*refs):
        ins, outs = refs[:n], refs[n:2 * n]
        send_sems, recv_sems, local_sems = refs[2 * n:]
        x, y, c = _mesh_pos()
        me, sibling = (x, y, c), (x, y, 1 - c)
        chips = [(1 - x, y), (x, 1 - y), (1 - x, 1 - y)]

        def slot(a, pos):
            px, py, pc = pos
            return outs[a].at[4 * px + 2 * py + pc]

        def copy(a, k, block, to, src=None):
            return pltpu.make_async_remote_copy(
                src_ref=slot(a, block) if src is None else src, dst_ref=slot(a, block),
                send_sem=send_sems.at[a * 7 + k], recv_sem=recv_sems.at[a * 7 + k],
                device_id=to, device_id_type=MESH_ID)

        mine, first, passed = [], [], []
        for a in range(n):
            cp = pltpu.make_async_copy(ins[a], slot(a, me), local_sems.at[a])
            cp.start()
            mine.append(cp)
            cps = [copy(a, 0, me, sibling, src=ins[a])]
            cps += [copy(a, 1 + j, me, (*chip, c), src=ins[a]) for j, chip in enumerate(chips)]
            for cp in cps:
                cp.start()
            first += cps
        for j, chip in enumerate(chips):
            for a in range(n):
                copy(a, 1 + j, (*chip, c), me).wait_recv()
                cp = copy(a, 4 + j, (*chip, c), sibling)
                cp.start()
                passed.append(cp)
        for a in range(n):
            copy(a, 0, sibling, me).wait_recv()
            for j, chip in enumerate(chips):
                copy(a, 4 + j, (*chip, 1 - c), me).wait_recv()
        for cp in first + passed:
            cp.wait_send()
        for cp in mine:
            cp.wait()

    hbm = pl.BlockSpec(memory_space=pltpu.HBM)
    return pl.pallas_call(
        body, name=name,
        out_shape=tuple(jax.ShapeDtypeStruct((N_DEV,) + a.shape, a.dtype) for a in arrays),
        in_specs=[hbm] * n, out_specs=tuple([hbm] * n),
        scratch_shapes=[pltpu.SemaphoreType.DMA((7 * n,)), pltpu.SemaphoreType.DMA((7 * n,)),
                        pltpu.SemaphoreType.DMA((n,))],
    )(*arrays)


def _exchange(arrays, name):
    n = len(arrays)

    def body(*refs):
        ins, outs = refs[:n], refs[n:2 * n]
        send_sems, recv_sems, local_sems = refs[2 * n:]
        x, y, c = _mesh_pos()
        my_idx = 4 * x + 2 * y + c
        copies, local = [], []
        for a in range(n):
            cp = pltpu.make_async_copy(ins[a].at[my_idx], outs[a].at[my_idx], local_sems.at[a])
            cp.start()
            local.append(cp)
            for k in range(1, N_DEV):
                px = 1 - x if k & 4 else x
                py = 1 - y if k & 2 else y
                pc = 1 - c if k & 1 else c
                cp = pltpu.make_async_remote_copy(
                    src_ref=ins[a].at[4 * px + 2 * py + pc], dst_ref=outs[a].at[my_idx],
                    send_sem=send_sems.at[a * 7 + k - 1], recv_sem=recv_sems.at[a * 7 + k - 1],
                    device_id=(px, py, pc), device_id_type=MESH_ID)
                cp.start()
                copies.append(cp)
        for cp in copies:
            cp.wait()
        for cp in local:
            cp.wait()

    hbm = pl.BlockSpec(memory_space=pltpu.HBM)
    return pl.pallas_call(
        body, name=name,
        out_shape=tuple(jax.ShapeDtypeStruct(a.shape, a.dtype) for a in arrays),
        in_specs=[hbm] * n, out_specs=tuple([hbm] * n),
        scratch_shapes=[pltpu.SemaphoreType.DMA((7 * n,)), pltpu.SemaphoreType.DMA((7 * n,)),
                        pltpu.SemaphoreType.DMA((n,))],
    )(*arrays)


_HBM = pl.BlockSpec(memory_space=pltpu.HBM)
_SEM = pl.BlockSpec(memory_space=pltpu.SEMAPHORE)
_EFFECT = pltpu.SideEffectType.DATAFLOW_SIDE_EFFECTING
N_PEERS = N_DEV - 1


def _peer(k):
    x, y, c = _mesh_pos()
    return (1 - x if k & 4 else x, 1 - y if k & 2 else y, 1 - c if k & 1 else c)


def _split_start(srcs, scatter, after, name):
    n = len(srcs)
    zones = [jax.ShapeDtypeStruct(((N_PEERS,) + s.shape[1:]) if scatter else ((N_DEV,) + s.shape), s.dtype)
             for s in srcs]

    def body(*refs):
        src, zone = refs[:n], refs[n:2 * n]
        outs = refs[2 * n + 1:]
        send, recv, token = outs[:n], outs[n:2 * n], outs[4 * n]
        x, y, c = _mesh_pos()
        my_idx = 4 * x + 2 * y + c
        for a in range(n):
            for k in range(1, N_DEV):
                px, py, pc = _peer(k)
                pltpu.make_async_remote_copy(
                    src_ref=src[a].at[4 * px + 2 * py + pc] if scatter else src[a],
                    dst_ref=zone[a].at[k - 1] if scatter else zone[a].at[my_idx],
                    send_sem=send[a], recv_sem=recv[a], device_id=(px, py, pc), device_id_type=MESH_ID).start()
        token[...] = jnp.zeros_like(token)

    hbm = lambda a: pltpu.with_memory_space_constraint(a, pltpu.HBM)
    outs = pl.pallas_call(
        body, name=name,
        out_shape=tuple([pltpu.SemaphoreType.DMA(())] * (2 * n)
                        + [pltpu.HBM(s.shape, s.dtype) for s in srcs]
                        + [pltpu.HBM(z.shape, z.dtype) for z in zones]
                        + [jax.ShapeDtypeStruct((8, 128), F32)]),
        in_specs=[_HBM] * (2 * n) + [pl.BlockSpec(memory_space=pl.ANY)],
        out_specs=tuple([_SEM] * (2 * n) + [_HBM] * (2 * n) + [pl.BlockSpec(memory_space=pltpu.VMEM)]),
        input_output_aliases={**{a: 2 * n + a for a in range(n)}, **{n + a: 3 * n + a for a in range(n)}},
        compiler_params=pltpu.CompilerParams(has_side_effects=_EFFECT),
    )(*[hbm(s) for s in srcs], *[hbm(lax.empty(z.shape, z.dtype)) for z in zones], after)
    return outs[:n], outs[n:2 * n], outs[2 * n:3 * n], outs[3 * n:4 * n], outs[4 * n]


def _split_wait(send, recv, srcs, zones, after, name):
    n = len(srcs)

    def body(*refs):
        zone = refs[n:2 * n]
        send_sems, recv_sems = refs[2 * n:3 * n], refs[3 * n:4 * n]
        x, y, c = _mesh_pos()
        for a in range(n):
            seven = zone[a].at[pl.ds(0, N_PEERS)]
            cp = pltpu.make_async_remote_copy(src_ref=seven, dst_ref=seven, send_sem=send_sems[a],
                                              recv_sem=recv_sems[a], device_id=(x, y, 1 - c),
                                              device_id_type=MESH_ID)
            cp.wait_send()
            cp.wait_recv()

    outs = pl.pallas_call(
        body, name=name,
        out_shape=tuple([pltpu.HBM(s.shape, s.dtype) for s in srcs] + [pltpu.HBM(z.shape, z.dtype) for z in zones]),
        in_specs=[_HBM] * (2 * n) + [_SEM] * (2 * n) + [pl.BlockSpec(memory_space=pl.ANY)],
        out_specs=tuple([_HBM] * (2 * n)),
        input_output_aliases={a: a for a in range(2 * n)},
        compiler_params=pltpu.CompilerParams(has_side_effects=_EFFECT),
    )(*srcs, *zones, *send, *recv, after)
    return outs[:n], outs[n:]


def _cat_blocks(g, axis):
    return jnp.concatenate([g[d] for d in range(N_DEV)], axis=axis)


def _permute_w_in(w):
    q_lat, kv_lat, k_rope, rest = w[:, 0:512], w[:, 512:768], w[:, 768:832], w[:, 832:]
    g_mla, others = rest[:, 0:1024], rest[:, 1024:]
    pad = jnp.zeros((w.shape[0], NPP - D_IN_PROJ), w.dtype)
    return jnp.concatenate([g_mla, q_lat, others, kv_lat, k_rope, pad], axis=1)


def _unpermute_w_in(w):
    return jnp.concatenate([w[:, O_QLAT:O_PIN], w[:, O_KVLAT:O_KROPE], w[:, O_KROPE:O_KROPE + ROPE],
                            w[:, O_GMLA:O_QLAT], w[:, O_PIN:O_KVLAT]], axis=1)


def _permute_w_uq(w):
    w3 = w.reshape(w.shape[0], N_HEADS, NOPE + ROPE)
    return jnp.concatenate([w3[:, :, :NOPE].reshape(w.shape[0], -1), w3[:, :, NOPE:].reshape(w.shape[0], -1)], axis=1)


def _unpermute_w_uq(w):
    nope = w[:, :N_HEADS * NOPE].reshape(w.shape[0], N_HEADS, NOPE)
    rope = w[:, N_HEADS * NOPE:].reshape(w.shape[0], N_HEADS, ROPE)
    return jnp.concatenate([nope, rope], axis=2).reshape(w.shape[0], -1)


_SMALL = (("emb_ln_g", 16), ("emb_ln_b", 16), ("q_norm_g", 8), ("kv_norm_g", 8), ("w_pool", 1024),
          ("pool_scale", 8), ("b_out", 32), ("ln_g", 32), ("ln_b", 32))
SMALL_ROWS = sum(r for _, r in _SMALL)


def _pack_small(d):
    parts = []
    for name, rows in _SMALL:
        flat = d[name].reshape(-1)
        flat = jnp.pad(flat, (0, rows * 128 - flat.shape[0]))
        parts.append(flat.reshape(rows, 128))
    return jnp.concatenate(parts, axis=0)


def _unpack_small(packed, shapes):
    out, r0 = {}, 0
    for name, rows in _SMALL:
        size = 1
        for s in shapes[name]:
            size *= s
        out[name] = packed[r0:r0 + rows].reshape(-1)[:size].reshape(shapes[name])
        r0 += rows
    return out


def _rope_tables(positions):
    half = ROPE // 2
    inv_freq = ROPE_THETA ** (-jnp.arange(half, dtype=F32) / half)
    ang = positions.astype(F32)[:, None] * inv_freq
    cos, sin = jnp.cos(ang), jnp.sin(ang)
    return jnp.concatenate([cos, cos, cos, cos], axis=1), jnp.concatenate([-sin, sin, -sin, sin], axis=1)


def _local_step(x, positions, target, emb_g, emb_b, layer_weights, on_sharded_grads, first_after=None):
    cos_t, sin_t = _rope_tables(positions)
    h, hb = _ln_fwd(x, emb_g, emb_b, "emb_ln_fwd", after=first_after)
    saved = []
    for l in range(DEPTH):
        W = layer_weights(l, h)
        proj = _mm(hb, W["w_in"], "nn", F32, "proj_fwd")
        qn, kvn, pooled, cv, ypc = _mix_fwd(proj, W["q_norm_g"], W["kv_norm_g"], W["w_pool"], W["pool_scale"],
                                            W["conv_w"], "mix_fwd")
        q = _mm(qn, W["w_uq"], "nn", F32, "q_up_fwd")
        kv = _mm(kvn, W["w_ukv"], "nn", BF16, "kv_up_fwd")
        qc, kc = _rope_fwd(q, kv, proj, cos_t, sin_t, "rope_fwd")
        o, ymla, lse = _flash_fwd(qc, kc, kv, proj, "flash_fwd")
        mix = jnp.concatenate([ymla, ypc], axis=1)
        z = _mm(mix, W["w_out"], "nn", F32, "out_fwd", res=h, bias=W["b_out"], alpha=ALPHA)
        saved.append((W, hb, proj, qn, kvn, pooled, cv, kv, qc, kc, o, lse, mix, z))
        h, hb = _ln_fwd(z, W["ln_g"], W["ln_b"], "ln_fwd")
    sq, dh = _loss_head(h, target, "loss_head")

    grads = {k: [None] * DEPTH for k in ("q_norm_g", "kv_norm_g", "w_pool", "pool_scale", "conv_w", "b_out", "ln_g",
                                         "ln_b")}
    for l in reversed(range(DEPTH)):
        W, hb_in, proj, qn, kvn, pooled, cv, kv, qc, kc, o, lse, mix, z = saved[l]
        sharded = {}
        dz, dzb, grads["ln_g"][l], grads["ln_b"][l], grads["b_out"][l] = _ln_bwd(dh, z, W["ln_g"], "ln_bwd")
        dmix = _mm(dzb, W["w_out"], "nt", F32, "out_bwd_x")
        sharded["w_out"] = _mm(mix, dzb, "tn", GRAD_XFER, "out_bwd_w")
        do, delta, dgm, dmid, grads["w_pool"][l], grads["pool_scale"][l], grads["conv_w"][l] = _mix_bwd(
            dmix, proj, o, pooled, cv, W["w_pool"], W["pool_scale"], W["conv_w"], "mix_bwd")
        dqc, dkv, dkr = _flash_bwd(qc, kc, kv, do, lse, delta, "flash_bwd")
        dq, dkrope = _rope_bwd(dqc, dkr, cos_t, sin_t, "rope_bwd")
        dqn = _mm(dq, W["w_uq"], "nt", F32, "q_up_bwd_x")
        sharded["w_uq"] = _mm(qn, dq, "tn", GRAD_XFER, "q_up_bwd_w")
        dkvn = _mm(dkv, W["w_ukv"], "nt", F32, "kv_up_bwd_x")
        sharded["w_ukv"] = _mm(kvn, dkv, "tn", GRAD_XFER, "kv_up_bwd_w")
        dql, dkvl, grads["q_norm_g"][l], grads["kv_norm_g"][l] = _rms_bwd(
            proj, dqn, dkvn, W["q_norm_g"], W["kv_norm_g"], "rms_bwd")
        dproj = jnp.concatenate([dgm, dql, dmid, dkvl, dkrope], axis=1)
        sharded["w_in"] = _mm(hb_in, dproj, "tn", GRAD_XFER, "proj_bwd_w")
        token = on_sharded_grads(l, sharded)
        dh = _mm(dproj, W["w_in"], "nt", F32, "proj_bwd_x", res=dz, alpha=ALPHA, after=token)
    grad_x, _, grads["emb_ln_g"], grads["emb_ln_b"], _ = _ln_bwd(dh, x, emb_g, "emb_ln_bwd")
    return sq, grad_x, grads


def kernel(x, positions, emb_ln_g, emb_ln_b, w_in, q_norm_g, kv_norm_g, w_uq, w_ukv, w_pool, pool_scale, conv_w, w_out, b_out, ln_g, ln_b, loss_target, m_emb_ln_g, m_emb_ln_b, m_w_in, m_q_norm_g, m_kv_norm_g, m_w_uq, m_w_ukv, m_w_pool, m_pool_scale, m_conv_w, m_w_out, m_b_out, m_ln_g, m_ln_b, v_emb_ln_g, v_emb_ln_b, v_w_in, v_q_norm_g, v_kv_norm_g, v_w_uq, v_w_ukv, v_w_pool, v_pool_scale, v_conv_w, v_w_out, v_b_out, v_ln_g, v_ln_b):
    weights = dict(emb_ln_g=emb_ln_g, emb_ln_b=emb_ln_b, w_in=w_in, q_norm_g=q_norm_g, kv_norm_g=kv_norm_g,
                   w_uq=w_uq, w_ukv=w_ukv, w_pool=w_pool, pool_scale=pool_scale, conv_w=conv_w, w_out=w_out,
                   b_out=b_out, ln_g=ln_g, ln_b=ln_b)
    mom1 = dict(emb_ln_g=m_emb_ln_g, emb_ln_b=m_emb_ln_b, w_in=m_w_in, q_norm_g=m_q_norm_g, kv_norm_g=m_kv_norm_g,
                w_uq=m_w_uq, w_ukv=m_w_ukv, w_pool=m_w_pool, pool_scale=m_pool_scale, conv_w=m_conv_w,
                w_out=m_w_out, b_out=m_b_out, ln_g=m_ln_g, ln_b=m_ln_b)
    mom2 = dict(emb_ln_g=v_emb_ln_g, emb_ln_b=v_emb_ln_b, w_in=v_w_in, q_norm_g=v_q_norm_g, kv_norm_g=v_kv_norm_g,
                w_uq=v_w_uq, w_ukv=v_w_ukv, w_pool=v_w_pool, pool_scale=v_pool_scale, conv_w=v_conv_w,
                w_out=v_w_out, b_out=v_b_out, ln_g=v_ln_g, ln_b=v_ln_b)

    big = ("w_in", "w_uq", "w_ukv", "w_out")
    my_idx = 4 * lax.axis_index("x") + 2 * lax.axis_index("y") + lax.axis_index("c")

    conv_pad = jnp.zeros((8, 128), F32).at[0:DEPTH * CONV_WIDTH, 0:64].set(conv_w.reshape(DEPTH * CONV_WIDTH, 64))
    gathered0 = _all_gather([weights[k][0].astype(BF16) for k in big] + [conv_pad], "weights0_all_gather")
    w1 = _split_start([weights[k][1].astype(BF16) for k in big], False, gathered0[0], "weights1_start")
    conv_full = _cat_blocks(gathered0[4][:, 0:DEPTH * CONV_WIDTH, 0:64], 1).reshape(DEPTH, CONV_WIDTH, D_CONV)
    conv_full = jnp.pad(conv_full, ((0, 0), (0, 8 - CONV_WIDTH), (0, 0)))

    def layer_weights(l, ready):
        if l == 0:
            blocks = gathered0[:4]
        else:
            own, zones = _split_wait(*w1[:4], ready, "weights1_wait")
            blocks = [lax.dynamic_update_index_in_dim(z, o, my_idx, 0) for z, o in zip(zones, own)]
        return dict(
            w_in=_permute_w_in(_cat_blocks(blocks[0], 1)), w_uq=_permute_w_uq(_cat_blocks(blocks[1], 1)),
            w_ukv=_cat_blocks(blocks[2], 1), w_out=_cat_blocks(blocks[3], 0), conv_w=conv_full[l],
            q_norm_g=q_norm_g[l].reshape(1, -1), kv_norm_g=kv_norm_g[l].reshape(1, -1),
            w_pool=w_pool[l].astype(BF16), pool_scale=pool_scale[l].reshape(1, -1), b_out=b_out[l].reshape(1, -1),
            ln_g=ln_g[l].reshape(1, -1), ln_b=ln_b[l].reshape(1, -1))

    in_flight = {}

    def on_sharded_grads(l, g):
        by_dest = [
            _unpermute_w_in(g["w_in"]).reshape(D_MODEL, N_DEV, -1).transpose(1, 0, 2),
            _unpermute_w_uq(g["w_uq"]).reshape(Q_LORA, N_DEV, -1).transpose(1, 0, 2),
            g["w_ukv"].reshape(KV_LORA, N_DEV, -1).transpose(1, 0, 2),
            g["w_out"].reshape(N_DEV, -1, D_MODEL),
        ]
        started = _split_start(by_dest, True, by_dest[3], "grads%d_start" % l)
        in_flight[l] = started[:4]
        return started[4]

    sq, grad_x, G = _local_step(x[0], positions[0], loss_target[0], emb_ln_g.reshape(1, -1),
                                emb_ln_b.reshape(1, -1), layer_weights, on_sharded_grads, first_after=w1[4])
    loss = lax.psum(sq[0, 0] * (0.5 / D_MODEL), ("x", "y", "c"))

    res = {}
    landed = {}
    for l in reversed(range(DEPTH)):
        srcs, zones = _split_wait(*in_flight[l], grad_x, "grads%d_wait" % l)
        landed[l] = ([lax.dynamic_index_in_dim(s, my_idx, 0, keepdims=False) for s in srcs], zones)
    for a, (name, rows) in enumerate((("w_in", 256), ("w_uq", 256), ("w_ukv", 256), ("w_out", 128))):
        per_layer = [_adamw(landed[l][1][a], weights[name][l], mom1[name][l], mom2[name][l], "adamw_" + name, rows,
                            own=landed[l][0][a]) for l in range(DEPTH)]
        res[name] = tuple(jnp.stack([per_layer[l][i] for l in range(DEPTH)]) for i in range(4))

    d_conv = jnp.stack([G["conv_w"][l][0:CONV_WIDTH] for l in range(DEPTH)])
    d_conv = d_conv.reshape(DEPTH * CONV_WIDTH, N_DEV, 64).transpose(1, 0, 2)
    d_conv = jnp.zeros((N_DEV, 8, 128), F32).at[:, 0:DEPTH * CONV_WIDTH, 0:64].set(d_conv)
    small = dict(emb_ln_g=G["emb_ln_g"], emb_ln_b=G["emb_ln_b"])
    for k in ("q_norm_g", "kv_norm_g", "w_pool", "pool_scale", "b_out", "ln_g", "ln_b"):
        small[k] = jnp.stack(G[k])
    d_small = jnp.broadcast_to(_pack_small(small)[None], (N_DEV, SMALL_ROWS, 128))
    l_conv, l_small = _exchange([d_conv, d_small], "small_gradient_exchange")
    conv_shard = lambda a: jnp.zeros((8, 128), F32).at[0:DEPTH * CONV_WIDTH, 0:64].set(a.reshape(-1, 64))
    conv_res = _adamw(l_conv, conv_shard(conv_w), conv_shard(m_conv_w), conv_shard(v_conv_w), "adamw_conv_w", 8)
    res["conv_w"] = tuple(o[0:DEPTH * CONV_WIDTH, 0:64].reshape(DEPTH, CONV_WIDTH, 64) for o in conv_res)
    small_res = _adamw(l_small, _pack_small(weights), _pack_small(mom1), _pack_small(mom2), "adamw_small", 392)
    shapes = {k: weights[k].shape for k, _ in _SMALL}
    unpacked = [_unpack_small(o, shapes) for o in small_res]
    for k, _ in _SMALL:
        res[k] = tuple(u[k] for u in unpacked)

    order = ("emb_ln_g", "emb_ln_b", "w_in", "q_norm_g", "kv_norm_g", "w_uq", "w_ukv", "w_pool", "pool_scale",
             "conv_w", "w_out", "b_out", "ln_g", "ln_b")
    return (loss, grad_x[None], *[res[k][0] for k in order], *[res[k][1] for k in order],
            *[res[k][2] for k in order], *[res[k][3] for k in order])
```

```python
import jax
import jax.numpy as jnp
from jax import lax
from jax.experimental import pallas as pl
from jax.experimental.pallas import tpu as pltpu

F32 = jnp.float32
BF16 = jnp.bfloat16

N_DEV = 8
D_MODEL = 2048
DEPTH = 2
N_HEADS = 8
NOPE = 128
ROPE = 64
V_DIM = 128
Q_LORA = 512
KV_LORA = 256
D_MLA = N_HEADS * V_DIM
D_POOL = 512
D_CONV = 512
POOL_WINDOWS = (2, 4, 8, 16)
POOL_GROUP = 128
CONV_WIDTH = 3
D_MIX = D_MLA + D_POOL + D_CONV
D_IN_PROJ = 4928
ROPE_THETA = 10000.0
LN_EPS = 1e-5
RMS_EPS = 1e-6
ALPHA = (2 * DEPTH) ** 0.25
ATTN_SCALE = (NOPE + ROPE) ** -0.5
ADAM_LR = 0.001
ADAM_B1 = 0.9
ADAM_B2 = 0.999
ADAM_EPS = 1e-08
ADAM_WD = 0.01
ADAM_STEP = 10

O_GMLA, O_PIN, O_GPOOL, O_CH, O_CB, O_CC, O_GCONV, O_QLAT, O_KVLAT, O_KROPE = (
    0, 1024, 1536, 2048, 2560, 3072, 3584, 4096, 4608, 4864)
NPP = 5120
N_GATED = O_QLAT
QC = NOPE + 2 * ROPE
HALO = 16
ATT_TILE = 512
ATT_CH = 256
LOG2E = 1.4426950408889634
EXP2_SCALE = ATTN_SCALE * LOG2E

GRAD_XFER = BF16
VMEM_LIMIT = 48 * 1024 * 1024
MESH_ID = pl.DeviceIdType.MESH


def _params(sem=None):
    return pltpu.CompilerParams(dimension_semantics=sem, vmem_limit_bytes=VMEM_LIMIT)


def _sigmoid(x):
    return 1.0 / (1.0 + jnp.exp(-x))


def _tile(dim, target):
    if dim <= target:
        return dim
    t = target - target % 128
    while dim % t:
        t -= 128
    return t


_DIMS = {"nn": (((1,), (0,)), ((), ())), "nt": (((1,), (1,)), ((), ())), "tn": (((0,), (0,)), ((), ()))}


def _mm(a, b, mode, out_dtype, name, res=None, bias=None, alpha=1.0, tm=1024, tn=1024, tk=512, after=None):
    if mode == "nn":
        (M, K), (K2, N) = a.shape, b.shape
    elif mode == "nt":
        (M, K), (N, K2) = a.shape, b.shape
    else:
        (K, M), (K2, N) = a.shape, b.shape
    assert K == K2
    tm, tn, tk = _tile(M, tm), _tile(N, tn), _tile(K, tk)
    nk = K // tk
    has_res, has_bias = res is not None, bias is not None

    def body(*refs):
        a_ref, b_ref = refs[0], refs[1]
        pos = 2
        res_ref = bias_ref = None
        if has_res:
            res_ref = refs[pos]
            pos += 1
        if has_bias:
            bias_ref = refs[pos]
            pos += 1
        o_ref, acc_ref = refs[-2], refs[-1]
        k = pl.program_id(2)

        @pl.when(k == 0)
        def _():
            acc_ref[...] = jnp.zeros_like(acc_ref)

        acc_ref[...] += lax.dot_general(a_ref[...].astype(BF16), b_ref[...].astype(BF16), _DIMS[mode],
                                        preferred_element_type=F32)

        @pl.when(k == nk - 1)
        def _():
            r = acc_ref[...]
            if has_bias:
                r = r + bias_ref[...]
            if has_res:
                r = alpha * res_ref[...] + r
            o_ref[...] = r.astype(out_dtype)

    if mode == "nn":
        in_specs = [pl.BlockSpec((tm, tk), lambda i, j, k: (i, k)), pl.BlockSpec((tk, tn), lambda i, j, k: (k, j))]
    elif mode == "nt":
        in_specs = [pl.BlockSpec((tm, tk), lambda i, j, k: (i, k)), pl.BlockSpec((tn, tk), lambda i, j, k: (j, k))]
    else:
        in_specs = [pl.BlockSpec((tk, tm), lambda i, j, k: (k, i)), pl.BlockSpec((tk, tn), lambda i, j, k: (k, j))]
    args = [a, b]
    if has_res:
        in_specs.append(pl.BlockSpec((tm, tn), lambda i, j, k: (i, j)))
        args.append(res)
    if has_bias:
        in_specs.append(pl.BlockSpec((1, tn), lambda i, j, k: (0, j)))
        args.append(bias)
    if after is not None:
        in_specs.append(pl.BlockSpec((8, 128), lambda i, j, k: (0, 0)))
        args.append(after)
    return pl.pallas_call(
        body, name=name,
        out_shape=jax.ShapeDtypeStruct((M, N), out_dtype),
        grid=(M // tm, N // tn, nk),
        in_specs=in_specs,
        out_specs=pl.BlockSpec((tm, tn), lambda i, j, k: (i, j)),
        scratch_shapes=[pltpu.VMEM((tm, tn), F32)],
        compiler_params=_params(("parallel", "parallel", "arbitrary")),
    )(*args)


def _ln_fwd(z, g, b, name, tq=256, after=None):
    T, D = z.shape

    def body(z_ref, g_ref, b_ref, *rest):
        y_ref, yb_ref = rest[-2:]
        zv = z_ref[...]
        mu = jnp.mean(zv, axis=1, keepdims=True)
        zc = zv - mu
        var = jnp.mean(zc * zc, axis=1, keepdims=True)
        y = zc * lax.rsqrt(var + LN_EPS) * g_ref[...] + b_ref[...]
        y_ref[...] = y
        yb_ref[...] = y.astype(BF16)

    row = pl.BlockSpec((tq, D), lambda i: (i, 0))
    vec = pl.BlockSpec((1, D), lambda i: (0, 0))
    return pl.pallas_call(
        body, name=name,
        out_shape=(jax.ShapeDtypeStruct((T, D), F32), jax.ShapeDtypeStruct((T, D), BF16)),
        grid=(T // tq,),
        in_specs=[row, vec, vec] + ([pl.BlockSpec((8, 128), lambda i: (0, 0))] if after is not None else []),
        out_specs=(row, row),
        compiler_params=_params(("parallel",)),
    )(z, g, b, *([after] if after is not None else []))


def _ln_bwd(dy, z, g, name, tq=256):
    T, D = z.shape

    def body(dy_ref, z_ref, g_ref, dz_ref, dzb_ref, dg_ref, db_ref, ds_ref):
        @pl.when(pl.program_id(0) == 0)
        def _():
            dg_ref[...] = jnp.zeros_like(dg_ref)
            db_ref[...] = jnp.zeros_like(db_ref)
            ds_ref[...] = jnp.zeros_like(ds_ref)

        zv, dyv = z_ref[...], dy_ref[...]
        mu = jnp.mean(zv, axis=1, keepdims=True)
        zc = zv - mu
        var = jnp.mean(zc * zc, axis=1, keepdims=True)
        rstd = lax.rsqrt(var + LN_EPS)
        xh = zc * rstd
        u = dyv * g_ref[...]
        dz = rstd * (u - jnp.mean(u, axis=1, keepdims=True) - xh * jnp.mean(u * xh, axis=1, keepdims=True))
        dz_ref[...] = dz
        dzb_ref[...] = dz.astype(BF16)
        dg_ref[...] += jnp.sum(dyv * xh, axis=0, keepdims=True)
        db_ref[...] += jnp.sum(dyv, axis=0, keepdims=True)
        ds_ref[...] += jnp.sum(dz, axis=0, keepdims=True)

    row = pl.BlockSpec((tq, D), lambda i: (i, 0))
    vec = pl.BlockSpec((1, D), lambda i: (0, 0))
    vshape = jax.ShapeDtypeStruct((1, D), F32)
    return pl.pallas_call(
        body, name=name,
        out_shape=(jax.ShapeDtypeStruct((T, D), F32), jax.ShapeDtypeStruct((T, D), BF16), vshape, vshape, vshape),
        grid=(T // tq,), in_specs=[row, row, vec], out_specs=(row, row, vec, vec, vec),
        compiler_params=_params(("arbitrary",)),
    )(dy, z, g)


def _loss_head(y, target, name, tq=256):
    T, D = y.shape

    def body(y_ref, t_ref, s_ref, dy_ref):
        @pl.when(pl.program_id(0) == 0)
        def _():
            s_ref[...] = jnp.zeros_like(s_ref)

        err = y_ref[...] - t_ref[...]
        s_ref[...] += jnp.sum(err * err)
        dy_ref[...] = err * (1.0 / D)

    row = pl.BlockSpec((tq, D), lambda i: (i, 0))
    acc = pl.BlockSpec((8, 128), lambda i: (0, 0))
    return pl.pallas_call(
        body, name=name,
        out_shape=(jax.ShapeDtypeStruct((8, 128), F32), jax.ShapeDtypeStruct((T, D), F32)),
        grid=(T // tq,), in_specs=[row, row], out_specs=(acc, row),
        compiler_params=_params(("arbitrary",)),
    )(y, target)


def _pblock(tq, width, offset):
    assert offset % width == 0
    blk = offset // width
    return pl.BlockSpec((tq, width), lambda i: (i, blk))


def _mix_fwd(proj, q_g, kv_g, w_pool, pool_scale, conv_w, name, tq=256):
    T = proj.shape[0]

    def body(ql_ref, kvl_ref, pin_ref, gp_ref, ch_ref, cb_ref, cc_ref, gc_ref, qg_ref, kvg_ref, wp_ref, ps_ref,
             cw_ref, qn_ref, kvn_ref, pooled_ref, cv_ref, ypc_ref, extp, extu):
        i = pl.program_id(0)
        for x_ref, g_ref, o_ref in ((ql_ref, qg_ref, qn_ref), (kvl_ref, kvg_ref, kvn_ref)):
            x = x_ref[...]
            r = lax.rsqrt(jnp.mean(x * x, axis=1, keepdims=True) + RMS_EPS)
            o_ref[...] = (x * r * g_ref[...]).astype(BF16)

        @pl.when(i == 0)
        def _():
            extp[0:HALO, :] = jnp.zeros((HALO, D_POOL), F32)
            extu[0:HALO, :] = jnp.zeros((HALO, D_CONV), F32)

        @pl.when(i > 0)
        def _():
            extp[0:HALO, :] = extp[tq:tq + HALO, :]
            extu[0:HALO, :] = extu[tq:tq + HALO, :]

        pin = pin_ref[...]
        extp[HALO:, :] = pin
        u = cc_ref[...] * ch_ref[...]
        extu[HALO:, :] = u
        t1 = (i * tq + lax.broadcasted_iota(jnp.int32, (tq, 1), 0) + 1).astype(F32)
        for g, w in enumerate(POOL_WINDOWS):
            cols = slice(g * POOL_GROUP, (g + 1) * POOL_GROUP)
            s = extp[:, cols]
            k = 1
            while k < w:
                s = s + pltpu.roll(s, k, 0)
                k *= 2
            mean = s[HALO:, :] / jnp.minimum(t1, float(w))
            pooled = (mean - pin[:, cols]).astype(BF16)
            pooled_ref[:, cols] = pooled
            r = jnp.dot(pooled, wp_ref[g], preferred_element_type=F32)
            gp = gp_ref[:, cols]
            ypc_ref[:, cols] = (r * ps_ref[:, cols] * (gp * _sigmoid(gp))).astype(BF16)
        eu = extu[...]
        u1 = pltpu.roll(eu, 1, 0)[HALO:, :]
        u2 = pltpu.roll(eu, 2, 0)[HALO:, :]
        cv = cw_ref[0:1, :] * u2 + cw_ref[1:2, :] * u1 + cw_ref[2:3, :] * u
        cv_ref[...] = cv
        gc = gc_ref[...]
        ypc_ref[:, D_POOL:] = (cb_ref[...] * cv * (gc * _sigmoid(gc))).astype(BF16)

    full = lambda shape: pl.BlockSpec(shape, lambda i: (0,) * len(shape))
    row = lambda w: pl.BlockSpec((tq, w), lambda i: (i, 0))
    return pl.pallas_call(
        body, name=name,
        out_shape=(jax.ShapeDtypeStruct((T, Q_LORA), BF16), jax.ShapeDtypeStruct((T, KV_LORA), BF16),
                   jax.ShapeDtypeStruct((T, D_POOL), BF16), jax.ShapeDtypeStruct((T, D_CONV), F32),
                   jax.ShapeDtypeStruct((T, D_MIX), BF16)),
        grid=(T // tq,),
        in_specs=[_pblock(tq, Q_LORA, O_QLAT), _pblock(tq, KV_LORA, O_KVLAT), _pblock(tq, 512, O_PIN),
                  _pblock(tq, 512, O_GPOOL), _pblock(tq, 512, O_CH), _pblock(tq, 512, O_CB), _pblock(tq, 512, O_CC),
                  _pblock(tq, 512, O_GCONV), full((1, Q_LORA)), full((1, KV_LORA)), full((4, 128, 128)),
                  full((1, D_POOL)), full((8, D_CONV))],
        out_specs=(row(Q_LORA), row(KV_LORA), row(D_POOL), row(D_CONV),
                   pl.BlockSpec((tq, D_POOL + D_CONV), lambda i: (i, D_MLA // (D_POOL + D_CONV)))),
        scratch_shapes=[pltpu.VMEM((tq + HALO, D_POOL), F32), pltpu.VMEM((tq + HALO, D_CONV), F32)],
        compiler_params=_params(("arbitrary",)),
    )(proj, proj, proj, proj, proj, proj, proj, proj, q_g, kv_g, w_pool, pool_scale, conv_w)


def _mix_bwd(dmix, proj, o, pooled, cv, w_pool, pool_scale, conv_w, name, tq=ATT_CH):
    T = proj.shape[0]
    nt = T // tq
    n_ext = tq + HALO

    def body(dym_ref, dyp_ref, dyc_ref, gm_ref, gp_ref, ch_ref, cb_ref, cc_ref, gc_ref, o_ref, pooled_ref, cv_ref,
             wp_ref, ps_ref, cw_ref, do_ref, delta_ref, dg_ref, dwp_ref, dps_ref, dcw_ref, exte, extd):
        i = pl.program_id(0)
        tile = nt - 1 - i

        @pl.when(i == 0)
        def _():
            dwp_ref[...] = jnp.zeros_like(dwp_ref)
            dps_ref[...] = jnp.zeros_like(dps_ref)
            dcw_ref[...] = jnp.zeros_like(dcw_ref)
            exte[tq:, :] = jnp.zeros((HALO, D_POOL), F32)
            extd[tq:, :] = jnp.zeros((HALO, D_CONV), F32)

        @pl.when(i > 0)
        def _():
            exte[tq:, :] = exte[0:HALO, :]
            extd[tq:, :] = extd[0:HALO, :]

        gm = gm_ref[...]
        sig = _sigmoid(gm)
        dym = dym_ref[...]
        ov = o_ref[...]
        do = dym * (gm * sig)
        do_ref[...] = do.astype(BF16)
        prod = do * ov
        ones = jnp.ones((8, V_DIM), F32)
        for h in range(N_HEADS):
            rows = lax.dot_general(ones, prod[:, h * V_DIM:(h + 1) * V_DIM], _DIMS["nt"],
                                   precision=lax.Precision.HIGHEST, preferred_element_type=F32)
            delta_ref[h, 0] = rows[0:1, :]
        dg_ref[:, O_GMLA:O_PIN] = (dym * ov * (sig * (1.0 + gm * (1.0 - sig)))).astype(BF16)

        t1 = (tile * tq + lax.broadcasted_iota(jnp.int32, (tq, 1), 0) + 1).astype(F32)
        for g, w in enumerate(POOL_WINDOWS):
            cols = slice(g * POOL_GROUP, (g + 1) * POOL_GROUP)
            pg = pooled_ref[:, cols]
            r = jnp.dot(pg, wp_ref[g], preferred_element_type=F32)
            gp = gp_ref[:, cols]
            sg = _sigmoid(gp)
            sl = gp * sg
            dyg = dyp_ref[:, cols]
            ps = ps_ref[:, cols]
            dg_ref[:, O_GPOOL + g * POOL_GROUP:O_GPOOL + (g + 1) * POOL_GROUP] = (
                dyg * (r * ps) * (sg * (1.0 + gp * (1.0 - sg)))).astype(BF16)
            dps_ref[:, cols] += jnp.sum(dyg * r * sl, axis=0, keepdims=True)
            dr = (dyg * ps * sl).astype(BF16)
            dwp_ref[g] += lax.dot_general(pg, dr, _DIMS["tn"], preferred_element_type=F32)
            dpooled = lax.dot_general(dr, wp_ref[g], _DIMS["nt"], preferred_element_type=F32)
            exte[0:tq, cols] = dpooled / jnp.minimum(t1, float(w))
            s = exte[:, cols]
            k = 1
            while k < w:
                s = s + pltpu.roll(s, n_ext - k, 0)
                k *= 2
            dg_ref[:, O_PIN + g * POOL_GROUP:O_PIN + (g + 1) * POOL_GROUP] = (s[0:tq, :] - dpooled).astype(BF16)

        gc = gc_ref[...]
        sg = _sigmoid(gc)
        sl = gc * sg
        dyc = dyc_ref[...]
        cb, cc, ch, cvv = cb_ref[...], cc_ref[...], ch_ref[...], cv_ref[...]
        dcv = dyc * cb * sl
        dg_ref[:, O_GCONV:O_GCONV + D_CONV] = (dyc * (cb * cvv) * (sg * (1.0 + gc * (1.0 - sg)))).astype(BF16)
        dg_ref[:, O_CB:O_CB + D_CONV] = (dyc * cvv * sl).astype(BF16)
        extd[0:tq, :] = dcv
        ed = extd[...]
        d1 = pltpu.roll(ed, n_ext - 1, 0)[0:tq, :]
        d2 = pltpu.roll(ed, n_ext - 2, 0)[0:tq, :]
        du = cw_ref[2:3, :] * dcv + cw_ref[1:2, :] * d1 + cw_ref[0:1, :] * d2
        u = cc * ch
        dcw_ref[0:1, :] += jnp.sum(u * d2, axis=0, keepdims=True)
        dcw_ref[1:2, :] += jnp.sum(u * d1, axis=0, keepdims=True)
        dcw_ref[2:3, :] += jnp.sum(u * dcv, axis=0, keepdims=True)
        dg_ref[:, O_CH:O_CH + D_CONV] = (du * cc).astype(BF16)
        dg_ref[:, O_CC:O_CC + D_CONV] = (du * ch).astype(BF16)

    def rblock(width, offset):
        assert offset % width == 0
        blk = offset // width
        return pl.BlockSpec((tq, width), lambda i: (nt - 1 - i, blk))

    full = lambda shape: pl.BlockSpec(shape, lambda i: (0,) * len(shape))
    return pl.pallas_call(
        body, name=name,
        out_shape=(jax.ShapeDtypeStruct((T, D_MLA), BF16), jax.ShapeDtypeStruct((N_HEADS, nt, 1, tq), F32),
                   jax.ShapeDtypeStruct((T, NPP), BF16),
                   jax.ShapeDtypeStruct((4, 128, 128), F32), jax.ShapeDtypeStruct((1, D_POOL), F32),
                   jax.ShapeDtypeStruct((8, D_CONV), F32)),
        grid=(nt,),
        in_specs=[rblock(1024, 0), rblock(512, 1024), rblock(512, 1536),
                  rblock(1024, O_GMLA), rblock(512, O_GPOOL), rblock(512, O_CH), rblock(512, O_CB),
                  rblock(512, O_CC), rblock(512, O_GCONV), rblock(1024, 0), rblock(512, 0), rblock(512, 0),
                  full((4, 128, 128)), full((1, D_POOL)), full((8, D_CONV))],
        out_specs=(rblock(1024, 0), pl.BlockSpec((N_HEADS, 1, 1, tq), lambda i: (0, nt - 1 - i, 0, 0)),
                   rblock(N_GATED, 0), full((4, 128, 128)), full((1, D_POOL)), full((8, D_CONV))),
        scratch_shapes=[pltpu.VMEM((n_ext, D_POOL), F32), pltpu.VMEM((n_ext, D_CONV), F32)],
        compiler_params=_params(("arbitrary",)),
    )(dmix, dmix, dmix, proj, proj, proj, proj, proj, proj, o, pooled, cv, w_pool, pool_scale, conv_w)


def _rms_bwd(proj, dqn, dkvn, dkrope, dproj, q_g, kv_g, name, tq=256):
    T = proj.shape[0]
    n_lat = NPP - N_GATED

    def body(ql_ref, kvl_ref, dqn_ref, dkvn_ref, dkr_ref, _, qg_ref, kvg_ref, dlat_ref, dqg_ref, dkvg_ref):
        @pl.when(pl.program_id(0) == 0)
        def _():
            dqg_ref[...] = jnp.zeros_like(dqg_ref)
            dkvg_ref[...] = jnp.zeros_like(dkvg_ref)

        for x_ref, dy_ref, g_ref, c0, dg_ref in ((ql_ref, dqn_ref, qg_ref, 0, dqg_ref),
                                                 (kvl_ref, dkvn_ref, kvg_ref, Q_LORA, dkvg_ref)):
            x, dy = x_ref[...], dy_ref[...]
            r = lax.rsqrt(jnp.mean(x * x, axis=1, keepdims=True) + RMS_EPS)
            xr = x * r
            u = dy * g_ref[...]
            dlat_ref[:, c0:c0 + x.shape[1]] = (r * (u - xr * jnp.mean(u * xr, axis=1, keepdims=True))).astype(BF16)
            dg_ref[...] += jnp.sum(dy * xr, axis=0, keepdims=True)
        dlat_ref[:, Q_LORA + KV_LORA:] = dkr_ref[...]

    row = lambda w: pl.BlockSpec((tq, w), lambda i: (i, 0))
    vec = lambda w: pl.BlockSpec((1, w), lambda i: (0, 0))
    assert N_GATED % n_lat == 0
    return pl.pallas_call(
        body, name=name,
        out_shape=(jax.ShapeDtypeStruct((T, NPP), BF16),
                   jax.ShapeDtypeStruct((1, Q_LORA), F32), jax.ShapeDtypeStruct((1, KV_LORA), F32)),
        grid=(T // tq,),
        in_specs=[_pblock(tq, Q_LORA, O_QLAT), _pblock(tq, KV_LORA, O_KVLAT), row(Q_LORA), row(KV_LORA),
                  row(n_lat - Q_LORA - KV_LORA), pl.BlockSpec(memory_space=pl.ANY), vec(Q_LORA), vec(KV_LORA)],
        out_specs=(pl.BlockSpec((tq, n_lat), lambda i: (i, N_GATED // n_lat)), vec(Q_LORA), vec(KV_LORA)),
        input_output_aliases={5: 0},
        compiler_params=_params(("arbitrary",)),
    )(proj, proj, dqn, dkvn, dkrope, dproj, q_g, kv_g)


def _swap_halves(x, lo):
    return jnp.where(lo, pltpu.roll(x, 96, 1), pltpu.roll(x, 32, 1))


def _rope_fwd(q, kv, proj, cos_t, sin_t, name, tq=256):
    T = q.shape[0]

    def body(qn_ref, qr_ref, kv_ref, kr_ref, c_ref, s_ref, qc_ref, kc_ref):
        C, S = c_ref[...], s_ref[...]
        lane = lax.broadcasted_iota(jnp.int32, (tq, 128), 1)
        lo = (lane % ROPE) < (ROPE // 2)
        first = lane < ROPE

        def rope(x):
            return x * C + _swap_halves(x, lo) * S

        kr = jnp.where(first, rope(kr_ref[...]), 0.0).astype(BF16)
        for j in range(N_HEADS // 2):
            r = rope(qr_ref[:, j * 128:(j + 1) * 128])
            pair = (jnp.where(first, r, 0.0), jnp.where(first, pltpu.roll(r, 64, 1), 0.0))
            for hh in range(2):
                h = 2 * j + hh
                qc_ref[h, :, 0:NOPE] = qn_ref[:, h * NOPE:(h + 1) * NOPE].astype(BF16)
                qc_ref[h, :, NOPE:QC] = pair[hh].astype(BF16)
        for h in range(N_HEADS):
            kc_ref[h, :, 0:NOPE] = kv_ref[:, h * 256:h * 256 + NOPE]
            kc_ref[h, :, NOPE:QC] = kr

    out = jax.ShapeDtypeStruct((N_HEADS, T, QC), BF16)
    hblock = pl.BlockSpec((N_HEADS, tq, QC), lambda i: (0, i, 0))
    return pl.pallas_call(
        body, name=name, out_shape=(out, out), grid=(T // tq,),
        in_specs=[pl.BlockSpec((tq, 1024), lambda i: (i, 0)), pl.BlockSpec((tq, 512), lambda i: (i, 2)),
                  pl.BlockSpec((tq, 2048), lambda i: (i, 0)), _pblock(tq, 128, O_KROPE),
                  pl.BlockSpec((tq, 128), lambda i: (i, 0)), pl.BlockSpec((tq, 128), lambda i: (i, 0))],
        out_specs=(hblock, hblock),
        compiler_params=_params(("parallel",)),
    )(q, q, kv, proj, cos_t, sin_t)


def _rope_bwd(dqc, dkr, cos_t, sin_t, name, tq=256):
    T = dqc.shape[1]

    def body(dqc_ref, dkr_ref, c_ref, s_ref, dq_ref, dk_ref):
        C, S = c_ref[...], s_ref[...]
        lane = lax.broadcasted_iota(jnp.int32, (tq, 128), 1)
        lo = (lane % ROPE) < (ROPE // 2)
        first = lane < ROPE

        def unrope(dy):
            return dy * C - _swap_halves(dy, lo) * S

        acc = dkr_ref[0]
        for h in range(1, N_HEADS):
            acc = acc + dkr_ref[h]
        dk_ref[:, 0:128] = jnp.where(first, unrope(acc), 0.0).astype(BF16)
        dk_ref[:, 128:256] = jnp.zeros((tq, 128), BF16)
        for j in range(N_HEADS // 2):
            d0 = dqc_ref[2 * j, :, NOPE:QC]
            d1 = dqc_ref[2 * j + 1, :, NOPE:QC]
            comb = jnp.where(first, d0, pltpu.roll(d1, 64, 1))
            dq_ref[:, 1024 + j * 128:1024 + (j + 1) * 128] = unrope(comb).astype(BF16)
        for h in range(N_HEADS):
            dq_ref[:, h * NOPE:(h + 1) * NOPE] = dqc_ref[h, :, 0:NOPE].astype(BF16)

    tab = pl.BlockSpec((tq, 128), lambda i: (i, 0))
    return pl.pallas_call(
        body, name=name,
        out_shape=(jax.ShapeDtypeStruct((T, 1536), BF16), jax.ShapeDtypeStruct((T, 256), BF16)),
        grid=(T // tq,),
        in_specs=[pl.BlockSpec((N_HEADS, tq, QC), lambda i: (0, i, 0)),
                  pl.BlockSpec((N_HEADS, tq, 128), lambda i: (0, i, 0)), tab, tab],
        out_specs=(pl.BlockSpec((tq, 1536), lambda i: (i, 0)), pl.BlockSpec((tq, 256), lambda i: (i, 0))),
        compiler_params=_params(("parallel",)),
    )(dqc, dkr, cos_t, sin_t)


def _flash_fwd(qc, kc, kv, proj, mix, name):
    H, T, _ = qc.shape
    tt = ATT_TILE
    nt = T // tt
    sp = tt // ATT_CH

    def body(q_ref, k_ref, v_ref, g_ref, _, o_ref, y_ref, lse_ref, vt_sc, s_sc, acc_sc, m_sc, l_sc):
        i = pl.program_id(1)

        @pl.when(i == 0)
        def _():
            for c in range(nt):
                vt_sc[c] = v_ref[c * tt:(c + 1) * tt, :].astype(F32).T.astype(BF16)

        q = q_ref[0]

        def issue(c, slot):
            s_sc[slot] = lax.dot_general(k_ref[0, pl.ds(pl.multiple_of(c * tt, tt), tt), :], q, _DIMS["nt"],
                                         preferred_element_type=F32)

        def softmax_pv(c, slot, masked):
            s = s_sc[slot]
            if masked:
                krow = c * tt + lax.broadcasted_iota(jnp.int32, s.shape, 0)
                qcol = i * tt + lax.broadcasted_iota(jnp.int32, s.shape, 1)
                s = jnp.where(krow <= qcol, s, -jnp.inf)
            m = m_sc[...]
            m_new = jnp.maximum(m, jnp.max(s, axis=0, keepdims=True))
            p = jnp.exp2((s - m_new) * EXP2_SCALE)
            a = jnp.exp2((m - m_new) * EXP2_SCALE)
            l_sc[...] = a * l_sc[...] + jnp.sum(p, axis=0, keepdims=True)
            acc_sc[...] = a * acc_sc[...] + jnp.dot(vt_sc[c], p.astype(BF16), preferred_element_type=F32)
            m_sc[...] = m_new

        m_sc[...] = jnp.full_like(m_sc, -jnp.inf)
        l_sc[...] = jnp.zeros_like(l_sc)
        acc_sc[...] = jnp.zeros_like(acc_sc)
        issue(0, 0)

        def pair(t, carry):
            issue(2 * t + 1, 1)
            softmax_pv(2 * t, 0, False)
            issue(2 * t + 2, 0)
            softmax_pv(2 * t + 1, 1, False)
            return carry

        lax.fori_loop(0, i // 2, pair, 0)

        @pl.when(i % 2 == 1)
        def _():
            issue(i, 1)
            softmax_pv(i - 1, 0, False)
            softmax_pv(i, 1, True)

        @pl.when(i % 2 == 0)
        def _():
            softmax_pv(i, 0, True)

        l = l_sc[...]
        o = (acc_sc[...] / l).T
        o_ref[...] = o
        lse = m_sc[...] * ATTN_SCALE + jnp.log(l)
        for r in range(sp):
            lse_ref[0, r] = lse[:, r * ATT_CH:(r + 1) * ATT_CH]
        g = g_ref[...]
        y_ref[...] = (o * (g * _sigmoid(g))).astype(BF16)

    return pl.pallas_call(
        body, name=name,
        out_shape=(jax.ShapeDtypeStruct((T, D_MLA), F32), jax.ShapeDtypeStruct((T, D_MIX), BF16),
                   jax.ShapeDtypeStruct((H, T // ATT_CH, 1, ATT_CH), F32)),
        grid=(H, nt),
        in_specs=[pl.BlockSpec((1, tt, QC), lambda h, i: (h, i, 0)),
                  pl.BlockSpec((1, T, QC), lambda h, i: (h, 0, 0)),
                  pl.BlockSpec((T, V_DIM), lambda h, i: (0, 2 * h + 1)),
                  pl.BlockSpec((tt, V_DIM), lambda h, i: (i, h)),
                  pl.BlockSpec(memory_space=pl.ANY)],
        input_output_aliases={4: 1},
        out_specs=(pl.BlockSpec((tt, V_DIM), lambda h, i: (i, h)),
                   pl.BlockSpec((tt, V_DIM), lambda h, i: (i, h)),
                   pl.BlockSpec((1, sp, 1, ATT_CH), lambda h, i: (h, i, 0, 0))),
        scratch_shapes=[pltpu.VMEM((nt, V_DIM, tt), BF16), pltpu.VMEM((2, tt, tt), F32),
                        pltpu.VMEM((V_DIM, tt), F32), pltpu.VMEM((1, tt), F32), pltpu.VMEM((1, tt), F32)],
        compiler_params=_params(("parallel", "arbitrary")),
    )(qc, kc, kv, proj, mix)


def _flash_bwd(qc, kc, kv, do, lse, delta, name):
    H, T, _ = qc.shape
    tt = ATT_TILE
    nt = T // tt
    sp = tt // ATT_CH

    def body(q_ref, k_ref, v_ref, do_ref, lse_ref, dl_ref, dq_ref, dkv_ref, dkr_ref, dqt_sc, dk_sc, dv_sc, s_sc,
             dp_sc):
        j = pl.program_id(1)

        @pl.when(j == 0)
        def _():
            dqt_sc[...] = jnp.zeros_like(dqt_sc)

        dk_sc[...] = jnp.zeros_like(dk_sc)
        dv_sc[...] = jnp.zeros_like(dv_sc)
        k = k_ref[0]
        v = v_ref[...]
        kt = k.astype(F32).T.astype(BF16)

        def operands(c):
            q0 = pl.multiple_of(c * tt, tt)
            return q_ref[0, pl.ds(q0, tt), :], do_ref[pl.ds(q0, tt), :]

        def stat_row(ref, c):
            return jnp.concatenate([ref[0, sp * c + r] for r in range(sp)], axis=1)

        def early(c, slot):
            q, dov = operands(c)
            s_sc[slot] = lax.dot_general(k, q, _DIMS["nt"], preferred_element_type=F32)
            dp_sc[slot] = lax.dot_general(v, dov, _DIMS["nt"], preferred_element_type=F32)

        def late(c, slot, masked):
            q, dov = operands(c)
            s, dp = s_sc[slot], dp_sc[slot]
            if masked:
                krow = j * tt + lax.broadcasted_iota(jnp.int32, s.shape, 0)
                qcol = c * tt + lax.broadcasted_iota(jnp.int32, s.shape, 1)
                s = jnp.where(krow <= qcol, s, -jnp.inf)
            p = jnp.exp2(s * EXP2_SCALE - stat_row(lse_ref, c) * LOG2E)
            ds = (p * (dp - stat_row(dl_ref, c)) * ATTN_SCALE).astype(BF16)
            dv_sc[...] += jnp.dot(p.astype(BF16), dov, preferred_element_type=F32)
            dk_sc[...] += jnp.dot(ds, q, preferred_element_type=F32)
            dqt_sc[c] += jnp.dot(kt, ds, preferred_element_type=F32)

        early(j, 0)

        @pl.when(j < nt - 1)
        def _():
            early(j + 1, 1)

        late(j, 0, True)
        n_rest = nt - 1 - j

        def pair(u, carry):
            a = j + 1 + 2 * u
            early(a + 1, 0)
            late(a, 1, False)

            @pl.when(a + 2 <= nt - 1)
            def _():
                early(a + 2, 1)

            late(a + 1, 0, False)
            return carry

        lax.fori_loop(0, n_rest // 2, pair, 0)

        @pl.when(n_rest % 2 == 1)
        def _():
            late(nt - 1, 1, False)

        dk = dk_sc[...]
        dkv_ref[:, 0:NOPE] = dk[:, 0:NOPE].astype(BF16)
        dkv_ref[:, NOPE:] = dv_sc[...].astype(BF16)
        dkr_ref[0] = dk[:, NOPE:]

        @pl.when(j == nt - 1)
        def _():
            for c in range(nt):
                dq_ref[0, c * tt:(c + 1) * tt, :] = dqt_sc[c].T

    head = lambda h, j: (h, 0, 0)
    stat = pl.BlockSpec((1, T // ATT_CH, 1, ATT_CH), lambda h, j: (h, 0, 0, 0))
    return pl.pallas_call(
        body, name=name,
        out_shape=(jax.ShapeDtypeStruct((H, T, QC), F32), jax.ShapeDtypeStruct((T, 2 * D_MLA), BF16),
                   jax.ShapeDtypeStruct((H, T, 128), F32)),
        grid=(H, nt),
        in_specs=[pl.BlockSpec((1, T, QC), head),
                  pl.BlockSpec((1, tt, QC), lambda h, j: (h, j, 0)),
                  pl.BlockSpec((tt, V_DIM), lambda h, j: (j, 2 * h + 1)),
                  pl.BlockSpec((T, V_DIM), lambda h, j: (0, h)),
                  stat, stat],
        out_specs=(pl.BlockSpec((1, T, QC), head),
                   pl.BlockSpec((tt, 256), lambda h, j: (j, h)),
                   pl.BlockSpec((1, tt, 128), lambda h, j: (h, j, 0))),
        scratch_shapes=[pltpu.VMEM((nt, QC, tt), F32), pltpu.VMEM((tt, QC), F32), pltpu.VMEM((tt, V_DIM), F32),
                        pltpu.VMEM((2, tt, tt), F32), pltpu.VMEM((2, tt, tt), F32)],
        compiler_params=_params(("parallel", "arbitrary")),
    )(qc, kc, kv, do, lse, delta)


def _adamw(lands, w, m, v, name, rows):
    L, R, C = w.shape
    assert R % rows == 0 and len(lands) == L
    nb = R // rows
    c1 = 1.0 - ADAM_B1 ** ADAM_STEP
    c2 = 1.0 - ADAM_B2 ** ADAM_STEP

    def body(*refs):
        land_refs = refs[:L]
        w_ref, m_ref, v_ref, g_ref, d_ref, nm_ref, nv_ref, g_sc = refs[L:]
        for ll in range(L):
            @pl.when(pl.program_id(0) == ll)
            def _(land_ref=land_refs[ll]):
                g = land_ref[0].astype(F32)
                for s in range(1, N_DEV):
                    g = g + land_ref[s].astype(F32)
                g_sc[...] = g

        g = g_sc[...]
        nm = ADAM_B1 * m_ref[0] + (1.0 - ADAM_B1) * g
        nv = ADAM_B2 * v_ref[0] + (1.0 - ADAM_B2) * (g * g)
        g_ref[0] = g
        nm_ref[0] = nm
        nv_ref[0] = nv
        d_ref[0] = -ADAM_LR * ((nm / c1) / (jnp.sqrt(nv / c2) + ADAM_EPS) + ADAM_WD * w_ref[0])

    def land_spec(ll):
        return pl.BlockSpec((N_DEV, rows, C),
                            lambda l, i: (0, jnp.where(l < ll, 0, jnp.where(l > ll, nb - 1, i)), 0))

    blk = pl.BlockSpec((1, rows, C), lambda l, i: (l, i, 0))
    out = jax.ShapeDtypeStruct((L, R, C), F32)
    return pl.pallas_call(
        body, name=name, out_shape=(out, out, out, out), grid=(L, nb),
        in_specs=[land_spec(ll) for ll in range(L)] + [blk, blk, blk],
        out_specs=(blk, blk, blk, blk),
        scratch_shapes=[pltpu.VMEM((rows, C), F32)],
        compiler_params=_params(("arbitrary", "arbitrary")),
    )(*lands, w, m, v)


def _mesh_pos():
    return lax.axis_index("x"), lax.axis_index("y"), lax.axis_index("c")


def _all_gather(arrays, name):
    n = len(arrays)

    def body(*refs):
        ins, outs = refs[:n], refs[n:2 * n]
        send_sems, recv_sems, local_sems = refs[2 * n:]
        x, y, c = _mesh_pos()
        me, sibling = (x, y, c), (x, y, 1 - c)
        chips = [(1 - x, y), (x, 1 - y), (1 - x, 1 - y)]

        def slot(a, pos):
            px, py, pc = pos
            return outs[a].at[4 * px + 2 * py + pc]

        def copy(a, k, block, to, src=None):
            return pltpu.make_async_remote_copy(
                src_ref=slot(a, block) if src is None else src, dst_ref=slot(a, block),
                send_sem=send_sems.at[a * 7 + k], recv_sem=recv_sems.at[a * 7 + k],
                device_id=to, device_id_type=MESH_ID)

        mine, first, passed = [], [], []
        for a in range(n):
            cp = pltpu.make_async_copy(ins[a], slot(a, me), local_sems.at[a])
            cp.start()
            mine.append(cp)
            cps = [copy(a, 0, me, sibling, src=ins[a])]
            cps += [copy(a, 1 + j, me, (*chip, c), src=ins[a]) for j, chip in enumerate(chips)]
            for cp in cps:
                cp.start()
            first += cps
        for j, chip in enumerate(chips):
            for a in range(n):
                copy(a, 1 + j, (*chip, c), me).wait_recv()
                cp = copy(a, 4 + j, (*chip, c), sibling)
                cp.start()
                passed.append(cp)
        for a in range(n):
            copy(a, 0, sibling, me).wait_recv()
            for j, chip in enumerate(chips):
                copy(a, 4 + j, (*chip, 1 - c), me).wait_recv()
        for cp in first + passed:
            cp.wait_send()
        for cp in mine:
            cp.wait()

    hbm = pl.BlockSpec(memory_space=pltpu.HBM)
    return pl.pallas_call(
        body, name=name,
        out_shape=tuple(jax.ShapeDtypeStruct((N_DEV,) + a.shape, a.dtype) for a in arrays),
        in_specs=[hbm] * n, out_specs=tuple([hbm] * n),
        scratch_shapes=[pltpu.SemaphoreType.DMA((7 * n,)), pltpu.SemaphoreType.DMA((7 * n,)),
                        pltpu.SemaphoreType.DMA((n,))],
    )(*arrays)


def _exchange(arrays, name):
    n = len(arrays)

    def body(*refs):
        ins, outs = refs[:n], refs[n:2 * n]
        send_sems, recv_sems, local_sems = refs[2 * n:]
        x, y, c = _mesh_pos()
        my_idx = 4 * x + 2 * y + c
        copies, local = [], []
        for a in range(n):
            cp = pltpu.make_async_copy(ins[a].at[my_idx], outs[a].at[my_idx], local_sems.at[a])
            cp.start()
            local.append(cp)
            for k in range(1, N_DEV):
                px = 1 - x if k & 4 else x
                py = 1 - y if k & 2 else y
                pc = 1 - c if k & 1 else c
                cp = pltpu.make_async_remote_copy(
                    src_ref=ins[a].at[4 * px + 2 * py + pc], dst_ref=outs[a].at[my_idx],
                    send_sem=send_sems.at[a * 7 + k - 1], recv_sem=recv_sems.at[a * 7 + k - 1],
                    device_id=(px, py, pc), device_id_type=MESH_ID)
                cp.start()
                copies.append(cp)
        for cp in copies:
            cp.wait()
        for cp in local:
            cp.wait()

    hbm = pl.BlockSpec(memory_space=pltpu.HBM)
    return pl.pallas_call(
        body, name=name,
        out_shape=tuple(jax.ShapeDtypeStruct(a.shape, a.dtype) for a in arrays),
        in_specs=[hbm] * n, out_specs=tuple([hbm] * n),
        scratch_shapes=[pltpu.SemaphoreType.DMA((7 * n,)), pltpu.SemaphoreType.DMA((7 * n,)),
                        pltpu.SemaphoreType.DMA((n,))],
    )(*arrays)


_HBM = pl.BlockSpec(memory_space=pltpu.HBM)
_SEM = pl.BlockSpec(memory_space=pltpu.SEMAPHORE)
_EFFECT = pltpu.SideEffectType.DATAFLOW_SIDE_EFFECTING
N_PEERS = N_DEV - 1


def _peer(k):
    x, y, c = _mesh_pos()
    return (1 - x if k & 4 else x, 1 - y if k & 2 else y, 1 - c if k & 1 else c)


def _split_start(srcs, scatter, after, name):
    n = len(srcs)
    zones = [jax.ShapeDtypeStruct(s.shape if scatter else ((N_DEV,) + s.shape), s.dtype) for s in srcs]

    def body(*refs):
        src, zone = refs[:n], refs[n:2 * n]
        outs = refs[2 * n + 1:]
        send, recv, token = outs[:n], outs[n:2 * n], outs[4 * n]
        x, y, c = _mesh_pos()
        my_idx = 4 * x + 2 * y + c
        for a in range(n):
            pltpu.make_async_copy(src[a].at[my_idx] if scatter else src[a],
                                  zone[a].at[N_PEERS] if scatter else zone[a].at[my_idx], recv[a]).start()
            for k in range(1, N_DEV):
                px, py, pc = _peer(k)
                pltpu.make_async_remote_copy(
                    src_ref=src[a].at[4 * px + 2 * py + pc] if scatter else src[a],
                    dst_ref=zone[a].at[k - 1] if scatter else zone[a].at[my_idx],
                    send_sem=send[a], recv_sem=recv[a], device_id=(px, py, pc), device_id_type=MESH_ID).start()
        token[...] = jnp.zeros_like(token)

    hbm = lambda a: pltpu.with_memory_space_constraint(a, pltpu.HBM)
    outs = pl.pallas_call(
        body, name=name,
        out_shape=tuple([pltpu.SemaphoreType.DMA(())] * (2 * n)
                        + [pltpu.HBM(s.shape, s.dtype) for s in srcs]
                        + [pltpu.HBM(z.shape, z.dtype) for z in zones]
                        + [jax.ShapeDtypeStruct((8, 128), F32)]),
        in_specs=[_HBM] * (2 * n) + [pl.BlockSpec(memory_space=pl.ANY)],
        out_specs=tuple([_SEM] * (2 * n) + [_HBM] * (2 * n) + [pl.BlockSpec(memory_space=pltpu.VMEM)]),
        input_output_aliases={**{a: 2 * n + a for a in range(n)}, **{n + a: 3 * n + a for a in range(n)}},
        compiler_params=pltpu.CompilerParams(has_side_effects=_EFFECT),
    )(*[hbm(s) for s in srcs], *[hbm(lax.empty(z.shape, z.dtype)) for z in zones], after)
    return outs[:n], outs[n:2 * n], outs[2 * n:3 * n], outs[3 * n:4 * n], outs[4 * n]


def _split_wait(send, recv, srcs, zones, after, name):
    n = len(srcs)

    def body(*refs):
        zone = refs[n:2 * n]
        send_sems, recv_sems = refs[2 * n:3 * n], refs[3 * n:4 * n]
        x, y, c = _mesh_pos()
        for a in range(n):
            seven = zone[a].at[pl.ds(0, N_PEERS)]
            pltpu.make_async_remote_copy(src_ref=seven, dst_ref=seven, send_sem=send_sems[a], recv_sem=recv_sems[a],
                                         device_id=(x, y, 1 - c), device_id_type=MESH_ID).wait_send()
            pltpu.make_async_remote_copy(src_ref=zone[a], dst_ref=zone[a], send_sem=send_sems[a],
                                         recv_sem=recv_sems[a], device_id=(x, y, 1 - c),
                                         device_id_type=MESH_ID).wait_recv()

    outs = pl.pallas_call(
        body, name=name,
        out_shape=tuple([pltpu.HBM(s.shape, s.dtype) for s in srcs] + [pltpu.HBM(z.shape, z.dtype) for z in zones]),
        in_specs=[_HBM] * (2 * n) + [_SEM] * (2 * n) + [pl.BlockSpec(memory_space=pl.ANY)],
        out_specs=tuple([_HBM] * (2 * n)),
        input_output_aliases={a: a for a in range(2 * n)},
        compiler_params=pltpu.CompilerParams(has_side_effects=_EFFECT),
    )(*srcs, *zones, *send, *recv, after)
    return outs[:n], outs[n:]


def _cat_blocks(g, axis):
    return jnp.concatenate([g[d] for d in range(N_DEV)], axis=axis)


N_LATENT = Q_LORA + KV_LORA + ROPE
W_SHARD = D_IN_PROJ // N_DEV


def _ref_cols(lo, hi):
    out = []
    if lo < N_LATENT:
        out.append((N_GATED + lo, N_GATED + min(hi, N_LATENT)))
    if hi > N_LATENT:
        out.append((max(lo, N_LATENT) - N_LATENT, hi - N_LATENT))
    return out


def _permute_w_in(blocks):
    pieces = []
    for lo, hi in ((N_LATENT, D_IN_PROJ), (0, N_LATENT)):
        for d in range(N_DEV):
            a, b = max(lo, d * W_SHARD), min(hi, (d + 1) * W_SHARD)
            if a < b:
                pieces.append(blocks[d][:, a - d * W_SHARD:b - d * W_SHARD])
    pieces.append(jnp.zeros((blocks.shape[1], NPP - D_IN_PROJ), blocks.dtype))
    return jnp.concatenate(pieces, axis=1)


def _split_w_in(w):
    slabs = []
    for d in range(N_DEV):
        parts = [w[:, a:b] for a, b in _ref_cols(d * W_SHARD, (d + 1) * W_SHARD)]
        slabs.append(parts[0] if len(parts) == 1 else jnp.concatenate(parts, axis=1))
    return jnp.stack(slabs)


def _permute_w_uq(w):
    w3 = w.reshape(w.shape[0], N_HEADS, NOPE + ROPE)
    return jnp.concatenate([w3[:, :, :NOPE].reshape(w.shape[0], -1), w3[:, :, NOPE:].reshape(w.shape[0], -1)], axis=1)


def _unpermute_w_uq(w):
    nope = w[:, :N_HEADS * NOPE].reshape(w.shape[0], N_HEADS, NOPE)
    rope = w[:, N_HEADS * NOPE:].reshape(w.shape[0], N_HEADS, ROPE)
    return jnp.concatenate([nope, rope], axis=2).reshape(w.shape[0], -1)


_SMALL = (("emb_ln_g", 16), ("emb_ln_b", 16), ("q_norm_g", 8), ("kv_norm_g", 8), ("w_pool", 1024),
          ("pool_scale", 8), ("b_out", 32), ("ln_g", 32), ("ln_b", 32))
SMALL_ROWS = sum(r for _, r in _SMALL)


def _pack_small(d):
    parts = []
    for name, rows in _SMALL:
        flat = d[name].reshape(-1)
        flat = jnp.pad(flat, (0, rows * 128 - flat.shape[0]))
        parts.append(flat.reshape(rows, 128))
    return jnp.concatenate(parts, axis=0)


def _unpack_small(packed, shapes):
    out, r0 = {}, 0
    for name, rows in _SMALL:
        size = 1
        for s in shapes[name]:
            size *= s
        out[name] = packed[r0:r0 + rows].reshape(-1)[:size].reshape(shapes[name])
        r0 += rows
    return out


def _rope_tables(positions):
    half = ROPE // 2
    inv_freq = ROPE_THETA ** (-jnp.arange(half, dtype=F32) / half)
    ang = positions.astype(F32)[:, None] * inv_freq
    cos, sin = jnp.cos(ang), jnp.sin(ang)
    return jnp.concatenate([cos, cos, cos, cos], axis=1), jnp.concatenate([-sin, sin, -sin, sin], axis=1)


def _local_step(x, positions, target, emb_g, emb_b, layer_weights, on_sharded_grads, first_after=None):
    cos_t, sin_t = _rope_tables(positions)
    h, hb = _ln_fwd(x, emb_g, emb_b, "emb_ln_fwd", after=first_after)
    saved = []
    for l in range(DEPTH):
        W = layer_weights(l, h)
        proj = _mm(hb, W["w_in"], "nn", F32, "proj_fwd")
        qn, kvn, pooled, cv, mix = _mix_fwd(proj, W["q_norm_g"], W["kv_norm_g"], W["w_pool"], W["pool_scale"],
                                            W["conv_w"], "mix_fwd")
        q = _mm(qn, W["w_uq"], "nn", F32, "q_up_fwd")
        kv = _mm(kvn, W["w_ukv"], "nn", BF16, "kv_up_fwd")
        qc, kc = _rope_fwd(q, kv, proj, cos_t, sin_t, "rope_fwd")
        o, mix, lse = _flash_fwd(qc, kc, kv, proj, mix, "flash_fwd")
        z = _mm(mix, W["w_out"], "nn", F32, "out_fwd", res=h, bias=W["b_out"], alpha=ALPHA)
        saved.append((W, hb, proj, qn, kvn, pooled, cv, kv, qc, kc, o, lse, mix, z))
        h, hb = _ln_fwd(z, W["ln_g"], W["ln_b"], "ln_fwd")
    sq, dh = _loss_head(h, target, "loss_head")

    grads = {k: [None] * DEPTH for k in ("q_norm_g", "kv_norm_g", "w_pool", "pool_scale", "conv_w", "b_out", "ln_g",
                                         "ln_b")}
    for l in reversed(range(DEPTH)):
        W, hb_in, proj, qn, kvn, pooled, cv, kv, qc, kc, o, lse, mix, z = saved[l]
        sharded = {}
        dz, dzb, grads["ln_g"][l], grads["ln_b"][l], grads["b_out"][l] = _ln_bwd(dh, z, W["ln_g"], "ln_bwd")
        dmix = _mm(dzb, W["w_out"], "nt", F32, "out_bwd_x")
        sharded["w_out"] = _mm(mix, dzb, "tn", GRAD_XFER, "out_bwd_w")
        do, delta, dproj, grads["w_pool"][l], grads["pool_scale"][l], grads["conv_w"][l] = _mix_bwd(
            dmix, proj, o, pooled, cv, W["w_pool"], W["pool_scale"], W["conv_w"], "mix_bwd")
        dqc, dkv, dkr = _flash_bwd(qc, kc, kv, do, lse, delta, "flash_bwd")
        dq, dkrope = _rope_bwd(dqc, dkr, cos_t, sin_t, "rope_bwd")
        dqn = _mm(dq, W["w_uq"], "nt", F32, "q_up_bwd_x")
        sharded["w_uq"] = _mm(qn, dq, "tn", GRAD_XFER, "q_up_bwd_w")
        dkvn = _mm(dkv, W["w_ukv"], "nt", F32, "kv_up_bwd_x")
        sharded["w_ukv"] = _mm(kvn, dkv, "tn", GRAD_XFER, "kv_up_bwd_w")
        dproj, grads["q_norm_g"][l], grads["kv_norm_g"][l] = _rms_bwd(
            proj, dqn, dkvn, dkrope, dproj, W["q_norm_g"], W["kv_norm_g"], "rms_bwd")
        sharded["w_in"] = _mm(hb_in, dproj, "tn", GRAD_XFER, "proj_bwd_w")
        token = on_sharded_grads(l, sharded)
        dh = _mm(dproj, W["w_in"], "nt", F32, "proj_bwd_x", res=dz, alpha=ALPHA, after=token)
    grad_x, _, grads["emb_ln_g"], grads["emb_ln_b"], _ = _ln_bwd(dh, x, emb_g, "emb_ln_bwd")
    return sq, grad_x, grads


def kernel(x, positions, emb_ln_g, emb_ln_b, w_in, q_norm_g, kv_norm_g, w_uq, w_ukv, w_pool, pool_scale, conv_w, w_out, b_out, ln_g, ln_b, loss_target, m_emb_ln_g, m_emb_ln_b, m_w_in, m_q_norm_g, m_kv_norm_g, m_w_uq, m_w_ukv, m_w_pool, m_pool_scale, m_conv_w, m_w_out, m_b_out, m_ln_g, m_ln_b, v_emb_ln_g, v_emb_ln_b, v_w_in, v_q_norm_g, v_kv_norm_g, v_w_uq, v_w_ukv, v_w_pool, v_pool_scale, v_conv_w, v_w_out, v_b_out, v_ln_g, v_ln_b):
    weights = dict(emb_ln_g=emb_ln_g, emb_ln_b=emb_ln_b, w_in=w_in, q_norm_g=q_norm_g, kv_norm_g=kv_norm_g,
                   w_uq=w_uq, w_ukv=w_ukv, w_pool=w_pool, pool_scale=pool_scale, conv_w=conv_w, w_out=w_out,
                   b_out=b_out, ln_g=ln_g, ln_b=ln_b)
    mom1 = dict(emb_ln_g=m_emb_ln_g, emb_ln_b=m_emb_ln_b, w_in=m_w_in, q_norm_g=m_q_norm_g, kv_norm_g=m_kv_norm_g,
                w_uq=m_w_uq, w_ukv=m_w_ukv, w_pool=m_w_pool, pool_scale=m_pool_scale, conv_w=m_conv_w,
                w_out=m_w_out, b_out=m_b_out, ln_g=m_ln_g, ln_b=m_ln_b)
    mom2 = dict(emb_ln_g=v_emb_ln_g, emb_ln_b=v_emb_ln_b, w_in=v_w_in, q_norm_g=v_q_norm_g, kv_norm_g=v_kv_norm_g,
                w_uq=v_w_uq, w_ukv=v_w_ukv, w_pool=v_w_pool, pool_scale=v_pool_scale, conv_w=v_conv_w,
                w_out=v_w_out, b_out=v_b_out, ln_g=v_ln_g, ln_b=v_ln_b)

    big = ("w_in", "w_uq", "w_ukv", "w_out")

    conv_pad = jnp.zeros((8, 128), F32).at[0:DEPTH * CONV_WIDTH, 0:64].set(conv_w.reshape(DEPTH * CONV_WIDTH, 64))
    gathered0 = _all_gather([weights[k][0].astype(BF16) for k in big] + [conv_pad], "weights0_all_gather")
    w1 = _split_start([weights[k][1].astype(BF16) for k in big], False, gathered0[0], "weights1_start")
    conv_full = _cat_blocks(gathered0[4][:, 0:DEPTH * CONV_WIDTH, 0:64], 1).reshape(DEPTH, CONV_WIDTH, D_CONV)
    conv_full = jnp.pad(conv_full, ((0, 0), (0, 8 - CONV_WIDTH), (0, 0)))

    def layer_weights(l, ready):
        if l == 0:
            blocks = gathered0[:4]
        else:
            blocks = _split_wait(*w1[:4], ready, "weights1_wait")[1]
        return dict(
            w_in=_permute_w_in(blocks[0]), w_uq=_permute_w_uq(_cat_blocks(blocks[1], 1)),
            w_ukv=_cat_blocks(blocks[2], 1), w_out=_cat_blocks(blocks[3], 0), conv_w=conv_full[l],
            q_norm_g=q_norm_g[l].reshape(1, -1), kv_norm_g=kv_norm_g[l].reshape(1, -1),
            w_pool=w_pool[l].astype(BF16), pool_scale=pool_scale[l].reshape(1, -1), b_out=b_out[l].reshape(1, -1),
            ln_g=ln_g[l].reshape(1, -1), ln_b=ln_b[l].reshape(1, -1))

    in_flight = {}

    def on_sharded_grads(l, g):
        by_dest = [
            _split_w_in(g["w_in"]),
            _unpermute_w_uq(g["w_uq"]).reshape(Q_LORA, N_DEV, -1).transpose(1, 0, 2),
            g["w_ukv"].reshape(KV_LORA, N_DEV, -1).transpose(1, 0, 2),
            g["w_out"].reshape(N_DEV, -1, D_MODEL),
        ]
        started = _split_start(by_dest, True, by_dest[3], "grads%d_start" % l)
        in_flight[l] = started[:4]
        return started[4]

    sq, grad_x, G = _local_step(x[0], positions[0], loss_target[0], emb_ln_g.reshape(1, -1),
                                emb_ln_b.reshape(1, -1), layer_weights, on_sharded_grads, first_after=w1[4])
    loss = lax.psum(sq[0, 0] * (0.5 / D_MODEL), ("x", "y", "c"))

    res = {}
    landed = {}
    for l in reversed(range(DEPTH)):
        landed[l] = _split_wait(*in_flight[l], grad_x, "grads%d_wait" % l)[1]
    for a, (name, rows) in enumerate((("w_in", 256), ("w_uq", 256), ("w_ukv", 256), ("w_out", 128))):
        res[name] = _adamw([landed[l][a] for l in range(DEPTH)], weights[name], mom1[name], mom2[name],
                           "adamw_" + name, rows)

    d_conv = jnp.stack([G["conv_w"][l][0:CONV_WIDTH] for l in range(DEPTH)])
    d_conv = d_conv.reshape(DEPTH * CONV_WIDTH, N_DEV, 64).transpose(1, 0, 2)
    d_conv = jnp.zeros((N_DEV, 8, 128), F32).at[:, 0:DEPTH * CONV_WIDTH, 0:64].set(d_conv)
    small = dict(emb_ln_g=G["emb_ln_g"], emb_ln_b=G["emb_ln_b"])
    for k in ("q_norm_g", "kv_norm_g", "w_pool", "pool_scale", "b_out", "ln_g", "ln_b"):
        small[k] = jnp.stack(G[k])
    d_small = jnp.broadcast_to(_pack_small(small)[None], (N_DEV, SMALL_ROWS, 128))
    l_conv, l_small = _exchange([d_conv, d_small], "small_gradient_exchange")
    conv_shard = lambda a: jnp.zeros((8, 128), F32).at[0:DEPTH * CONV_WIDTH, 0:64].set(a.reshape(-1, 64))
    conv_res = _adamw([l_conv], conv_shard(conv_w)[None], conv_shard(m_conv_w)[None], conv_shard(v_conv_w)[None],
                      "adamw_conv_w", 8)
    res["conv_w"] = tuple(o[0, 0:DEPTH * CONV_WIDTH, 0:64].reshape(DEPTH, CONV_WIDTH, 64) for o in conv_res)
    small_res = _adamw([l_small], _pack_small(weights)[None], _pack_small(mom1)[None], _pack_small(mom2)[None],
                       "adamw_small", 392)
    shapes = {k: weights[k].shape for k, _ in _SMALL}
    unpacked = [_unpack_small(o[0], shapes) for o in small_res]
    for k, _ in _SMALL:
        res[k] = tuple(u[k] for u in unpacked)

    order = ("emb_ln_g", "emb_ln_b", "w_in", "q_norm_g", "kv_norm_g", "w_uq", "w_ukv", "w_pool", "pool_scale",
             "conv_w", "w_out", "b_out", "ln_g", "ln_b")
    return (loss, grad_x[None], *[res[k][0] for k in order], *[res[k][1] for k in order],
            *[res[k][2] for k in order], *[res[k][3] for k in order])
```

```python
import jax
import jax.numpy as jnp
from jax import lax
from jax.experimental import pallas as pl
from jax.experimental.pallas import tpu as pltpu

F32 = jnp.float32
BF16 = jnp.bfloat16

N_DEV = 8
D_MODEL = 2048
DEPTH = 2
N_HEADS = 8
NOPE = 128
ROPE = 64
V_DIM = 128
Q_LORA = 512
KV_LORA = 256
D_MLA = N_HEADS * V_DIM
D_POOL = 512
D_CONV = 512
POOL_WINDOWS = (2, 4, 8, 16)
POOL_GROUP = 128
CONV_WIDTH = 3
D_MIX = D_MLA + D_POOL + D_CONV
D_IN_PROJ = 4928
ROPE_THETA = 10000.0
LN_EPS = 1e-5
RMS_EPS = 1e-6
ALPHA = (2 * DEPTH) ** 0.25
ATTN_SCALE = (NOPE + ROPE) ** -0.5
ADAM_LR = 0.001
ADAM_B1 = 0.9
ADAM_B2 = 0.999
ADAM_EPS = 1e-08
ADAM_WD = 0.01
ADAM_STEP = 10

O_GMLA, O_PIN, O_GPOOL, O_CH, O_CB, O_CC, O_GCONV, O_QLAT, O_KVLAT, O_KROPE = (
    0, 1024, 1536, 2048, 2560, 3072, 3584, 4096, 4608, 4864)
NPP = 5120
N_GATED = O_QLAT
QC = NOPE + 2 * ROPE
HALO = 16
ATT_TILE = 512
ATT_CH = 256
LOG2E = 1.4426950408889634
EXP2_SCALE = ATTN_SCALE * LOG2E

GRAD_XFER = BF16
VMEM_LIMIT = 48 * 1024 * 1024
MESH_ID = pl.DeviceIdType.MESH


def _params(sem=None):
    return pltpu.CompilerParams(dimension_semantics=sem, vmem_limit_bytes=VMEM_LIMIT)


def _sigmoid(x):
    return 1.0 / (1.0 + jnp.exp(-x))


def _tile(dim, target):
    if dim <= target:
        return dim
    t = target - target % 128
    while dim % t:
        t -= 128
    return t


_DIMS = {"nn": (((1,), (0,)), ((), ())), "nt": (((1,), (1,)), ((), ())), "tn": (((0,), (0,)), ((), ()))}


def _mm(a, b, mode, out_dtype, name, res=None, bias=None, alpha=1.0, tm=1024, tn=1024, tk=2048, after=None):
    if mode == "nn":
        (M, K), (K2, N) = a.shape, b.shape
    elif mode == "nt":
        (M, K), (N, K2) = a.shape, b.shape
    else:
        (K, M), (K2, N) = a.shape, b.shape
    assert K == K2
    tm, tn, tk = _tile(M, tm), _tile(N, tn), _tile(K, tk)
    nk = K // tk
    has_res, has_bias = res is not None, bias is not None

    def body(*refs):
        a_ref, b_ref = refs[0], refs[1]
        pos = 2
        res_ref = bias_ref = None
        if has_res:
            res_ref = refs[pos]
            pos += 1
        if has_bias:
            bias_ref = refs[pos]
            pos += 1
        def finish(r, o_ref):
            if has_bias:
                r = r + bias_ref[...]
            if has_res:
                r = alpha * res_ref[...] + r
            o_ref[...] = r.astype(out_dtype)

        part = lax.dot_general(a_ref[...].astype(BF16), b_ref[...].astype(BF16), _DIMS[mode],
                               preferred_element_type=F32)
        if nk == 1:
            finish(part, refs[-1])
            return
        o_ref, acc_ref = refs[-2], refs[-1]
        k = pl.program_id(2)

        @pl.when(k == 0)
        def _():
            acc_ref[...] = part

        @pl.when(jnp.logical_and(k > 0, k < nk - 1))
        def _():
            acc_ref[...] += part

        @pl.when(k == nk - 1)
        def _():
            finish(acc_ref[...] + part, o_ref)

    if mode == "nn":
        in_specs = [pl.BlockSpec((tm, tk), lambda i, j, k: (i, k)), pl.BlockSpec((tk, tn), lambda i, j, k: (k, j))]
    elif mode == "nt":
        in_specs = [pl.BlockSpec((tm, tk), lambda i, j, k: (i, k)), pl.BlockSpec((tn, tk), lambda i, j, k: (j, k))]
    else:
        in_specs = [pl.BlockSpec((tk, tm), lambda i, j, k: (k, i)), pl.BlockSpec((tk, tn), lambda i, j, k: (k, j))]
    args = [a, b]
    if has_res:
        in_specs.append(pl.BlockSpec((tm, tn), lambda i, j, k: (i, j)))
        args.append(res)
    if has_bias:
        in_specs.append(pl.BlockSpec((1, tn), lambda i, j, k: (0, j)))
        args.append(bias)
    if after is not None:
        in_specs.append(pl.BlockSpec((8, 128), lambda i, j, k: (0, 0)))
        args.append(after)
    return pl.pallas_call(
        body, name=name,
        out_shape=jax.ShapeDtypeStruct((M, N), out_dtype),
        grid=(M // tm, N // tn, nk),
        in_specs=in_specs,
        out_specs=pl.BlockSpec((tm, tn), lambda i, j, k: (i, j)),
        scratch_shapes=[pltpu.VMEM((tm, tn), F32)] if nk > 1 else [],
        compiler_params=_params(("parallel", "parallel", "arbitrary")),
    )(*args)


def _ln_fwd(z, g, b, name, tq=256, after=None):
    T, D = z.shape

    def body(z_ref, g_ref, b_ref, *rest):
        y_ref, yb_ref = rest[-2:]
        zv = z_ref[...]
        mu = jnp.mean(zv, axis=1, keepdims=True)
        zc = zv - mu
        var = jnp.mean(zc * zc, axis=1, keepdims=True)
        y = zc * lax.rsqrt(var + LN_EPS) * g_ref[...] + b_ref[...]
        y_ref[...] = y
        yb_ref[...] = y.astype(BF16)

    row = pl.BlockSpec((tq, D), lambda i: (i, 0))
    vec = pl.BlockSpec((1, D), lambda i: (0, 0))
    return pl.pallas_call(
        body, name=name,
        out_shape=(jax.ShapeDtypeStruct((T, D), F32), jax.ShapeDtypeStruct((T, D), BF16)),
        grid=(T // tq,),
        in_specs=[row, vec, vec] + ([pl.BlockSpec((8, 128), lambda i: (0, 0))] if after is not None else []),
        out_specs=(row, row),
        compiler_params=_params(("parallel",)),
    )(z, g, b, *([after] if after is not None else []))


def _ln_bwd(dy, z, g, name, tq=256):
    T, D = z.shape

    def body(dy_ref, z_ref, g_ref, dz_ref, dzb_ref, dg_ref, db_ref, ds_ref):
        @pl.when(pl.program_id(0) == 0)
        def _():
            dg_ref[...] = jnp.zeros_like(dg_ref)
            db_ref[...] = jnp.zeros_like(db_ref)
            ds_ref[...] = jnp.zeros_like(ds_ref)

        zv, dyv = z_ref[...], dy_ref[...]
        mu = jnp.mean(zv, axis=1, keepdims=True)
        zc = zv - mu
        var = jnp.mean(zc * zc, axis=1, keepdims=True)
        rstd = lax.rsqrt(var + LN_EPS)
        xh = zc * rstd
        u = dyv * g_ref[...]
        dz = rstd * (u - jnp.mean(u, axis=1, keepdims=True) - xh * jnp.mean(u * xh, axis=1, keepdims=True))
        dz_ref[...] = dz
        dzb_ref[...] = dz.astype(BF16)
        dg_ref[...] += jnp.sum(dyv * xh, axis=0, keepdims=True)
        db_ref[...] += jnp.sum(dyv, axis=0, keepdims=True)
        ds_ref[...] += jnp.sum(dz, axis=0, keepdims=True)

    row = pl.BlockSpec((tq, D), lambda i: (i, 0))
    vec = pl.BlockSpec((1, D), lambda i: (0, 0))
    vshape = jax.ShapeDtypeStruct((1, D), F32)
    return pl.pallas_call(
        body, name=name,
        out_shape=(jax.ShapeDtypeStruct((T, D), F32), jax.ShapeDtypeStruct((T, D), BF16), vshape, vshape, vshape),
        grid=(T // tq,), in_specs=[row, row, vec], out_specs=(row, row, vec, vec, vec),
        compiler_params=_params(("arbitrary",)),
    )(dy, z, g)


def _loss_head(y, target, name, tq=256):
    T, D = y.shape

    def body(y_ref, t_ref, s_ref, dy_ref):
        @pl.when(pl.program_id(0) == 0)
        def _():
            s_ref[...] = jnp.zeros_like(s_ref)

        err = y_ref[...] - t_ref[...]
        s_ref[...] += jnp.sum(err * err)
        dy_ref[...] = err * (1.0 / D)

    row = pl.BlockSpec((tq, D), lambda i: (i, 0))
    acc = pl.BlockSpec((8, 128), lambda i: (0, 0))
    return pl.pallas_call(
        body, name=name,
        out_shape=(jax.ShapeDtypeStruct((8, 128), F32), jax.ShapeDtypeStruct((T, D), F32)),
        grid=(T // tq,), in_specs=[row, row], out_specs=(acc, row),
        compiler_params=_params(("arbitrary",)),
    )(y, target)


def _pblock(tq, width, offset):
    assert offset % width == 0
    blk = offset // width
    return pl.BlockSpec((tq, width), lambda i: (i, blk))


def _mix_fwd(proj, q_g, kv_g, w_pool, pool_scale, conv_w, name, tq=256):
    T = proj.shape[0]

    def body(ql_ref, kvl_ref, pin_ref, gp_ref, ch_ref, cb_ref, cc_ref, gc_ref, qg_ref, kvg_ref, wp_ref, ps_ref,
             cw_ref, qn_ref, kvn_ref, pooled_ref, cv_ref, ypc_ref, extp, extu):
        i = pl.program_id(0)
        for x_ref, g_ref, o_ref in ((ql_ref, qg_ref, qn_ref), (kvl_ref, kvg_ref, kvn_ref)):
            x = x_ref[...]
            r = lax.rsqrt(jnp.mean(x * x, axis=1, keepdims=True) + RMS_EPS)
            o_ref[...] = (x * r * g_ref[...]).astype(BF16)

        @pl.when(i == 0)
        def _():
            extp[0:HALO, :] = jnp.zeros((HALO, D_POOL), F32)
            extu[0:HALO, :] = jnp.zeros((HALO, D_CONV), F32)

        @pl.when(i > 0)
        def _():
            extp[0:HALO, :] = extp[tq:tq + HALO, :]
            extu[0:HALO, :] = extu[tq:tq + HALO, :]

        pin = pin_ref[...]
        extp[HALO:, :] = pin
        u = cc_ref[...] * ch_ref[...]
        extu[HALO:, :] = u
        t1 = (i * tq + lax.broadcasted_iota(jnp.int32, (tq, 1), 0) + 1).astype(F32)
        for g, w in enumerate(POOL_WINDOWS):
            cols = slice(g * POOL_GROUP, (g + 1) * POOL_GROUP)
            s = extp[:, cols]
            k = 1
            while k < w:
                s = s + pltpu.roll(s, k, 0)
                k *= 2
            mean = s[HALO:, :] / jnp.minimum(t1, float(w))
            pooled = (mean - pin[:, cols]).astype(BF16)
            pooled_ref[:, cols] = pooled
            r = jnp.dot(pooled, wp_ref[g], preferred_element_type=F32)
            gp = gp_ref[:, cols]
            ypc_ref[:, cols] = (r * ps_ref[:, cols] * (gp * _sigmoid(gp))).astype(BF16)
        eu = extu[...]
        u1 = pltpu.roll(eu, 1, 0)[HALO:, :]
        u2 = pltpu.roll(eu, 2, 0)[HALO:, :]
        cv = cw_ref[0:1, :] * u2 + cw_ref[1:2, :] * u1 + cw_ref[2:3, :] * u
        cv_ref[...] = cv
        gc = gc_ref[...]
        ypc_ref[:, D_POOL:] = (cb_ref[...] * cv * (gc * _sigmoid(gc))).astype(BF16)

    full = lambda shape: pl.BlockSpec(shape, lambda i: (0,) * len(shape))
    row = lambda w: pl.BlockSpec((tq, w), lambda i: (i, 0))
    return pl.pallas_call(
        body, name=name,
        out_shape=(jax.ShapeDtypeStruct((T, Q_LORA), BF16), jax.ShapeDtypeStruct((T, KV_LORA), BF16),
                   jax.ShapeDtypeStruct((T, D_POOL), BF16), jax.ShapeDtypeStruct((T, D_CONV), F32),
                   jax.ShapeDtypeStruct((T, D_MIX), BF16)),
        grid=(T // tq,),
        in_specs=[_pblock(tq, Q_LORA, O_QLAT), _pblock(tq, KV_LORA, O_KVLAT), _pblock(tq, 512, O_PIN),
                  _pblock(tq, 512, O_GPOOL), _pblock(tq, 512, O_CH), _pblock(tq, 512, O_CB), _pblock(tq, 512, O_CC),
                  _pblock(tq, 512, O_GCONV), full((1, Q_LORA)), full((1, KV_LORA)), full((4, 128, 128)),
                  full((1, D_POOL)), full((8, D_CONV))],
        out_specs=(row(Q_LORA), row(KV_LORA), row(D_POOL), row(D_CONV),
                   pl.BlockSpec((tq, D_POOL + D_CONV), lambda i: (i, D_MLA // (D_POOL + D_CONV)))),
        scratch_shapes=[pltpu.VMEM((tq + HALO, D_POOL), F32), pltpu.VMEM((tq + HALO, D_CONV), F32)],
        compiler_params=_params(("arbitrary",)),
    )(proj, proj, proj, proj, proj, proj, proj, proj, q_g, kv_g, w_pool, pool_scale, conv_w)


def _mix_bwd(dmix, proj, o, pooled, cv, w_pool, pool_scale, conv_w, name, tq=ATT_CH):
    T = proj.shape[0]
    nt = T // tq
    n_ext = tq + HALO

    def body(dym_ref, dyp_ref, dyc_ref, gm_ref, gp_ref, ch_ref, cb_ref, cc_ref, gc_ref, o_ref, pooled_ref, cv_ref,
             wp_ref, ps_ref, cw_ref, do_ref, delta_ref, dg_ref, dwp_ref, dps_ref, dcw_ref, exte, extd):
        i = pl.program_id(0)
        tile = nt - 1 - i

        @pl.when(i == 0)
        def _():
            dwp_ref[...] = jnp.zeros_like(dwp_ref)
            dps_ref[...] = jnp.zeros_like(dps_ref)
            dcw_ref[...] = jnp.zeros_like(dcw_ref)
            exte[tq:, :] = jnp.zeros((HALO, D_POOL), F32)
            extd[tq:, :] = jnp.zeros((HALO, D_CONV), F32)

        @pl.when(i > 0)
        def _():
            exte[tq:, :] = exte[0:HALO, :]
            extd[tq:, :] = extd[0:HALO, :]

        gm = gm_ref[...]
        sig = _sigmoid(gm)
        dym = dym_ref[...]
        ov = o_ref[...]
        do = dym * (gm * sig)
        do_ref[...] = do.astype(BF16)
        prod = do * ov
        ones = jnp.ones((8, V_DIM), F32)
        for h in range(N_HEADS):
            rows = lax.dot_general(ones, prod[:, h * V_DIM:(h + 1) * V_DIM], _DIMS["nt"],
                                   precision=lax.Precision.HIGHEST, preferred_element_type=F32)
            delta_ref[h, 0] = rows[0:1, :]
        dg_ref[:, O_GMLA:O_PIN] = (dym * ov * (sig * (1.0 + gm * (1.0 - sig)))).astype(BF16)

        t1 = (tile * tq + lax.broadcasted_iota(jnp.int32, (tq, 1), 0) + 1).astype(F32)
        for g, w in enumerate(POOL_WINDOWS):
            cols = slice(g * POOL_GROUP, (g + 1) * POOL_GROUP)
            pg = pooled_ref[:, cols]
            r = jnp.dot(pg, wp_ref[g], preferred_element_type=F32)
            gp = gp_ref[:, cols]
            sg = _sigmoid(gp)
            sl = gp * sg
            dyg = dyp_ref[:, cols]
            ps = ps_ref[:, cols]
            dg_ref[:, O_GPOOL + g * POOL_GROUP:O_GPOOL + (g + 1) * POOL_GROUP] = (
                dyg * (r * ps) * (sg * (1.0 + gp * (1.0 - sg)))).astype(BF16)
            dps_ref[:, cols] += jnp.sum(dyg * r * sl, axis=0, keepdims=True)
            dr = (dyg * ps * sl).astype(BF16)
            dwp_ref[g] += lax.dot_general(pg, dr, _DIMS["tn"], preferred_element_type=F32)
            dpooled = lax.dot_general(dr, wp_ref[g], _DIMS["nt"], preferred_element_type=F32)
            exte[0:tq, cols] = dpooled / jnp.minimum(t1, float(w))
            s = exte[:, cols]
            k = 1
            while k < w:
                s = s + pltpu.roll(s, n_ext - k, 0)
                k *= 2
            dg_ref[:, O_PIN + g * POOL_GROUP:O_PIN + (g + 1) * POOL_GROUP] = (s[0:tq, :] - dpooled).astype(BF16)

        gc = gc_ref[...]
        sg = _sigmoid(gc)
        sl = gc * sg
        dyc = dyc_ref[...]
        cb, cc, ch, cvv = cb_ref[...], cc_ref[...], ch_ref[...], cv_ref[...]
        dcv = dyc * cb * sl
        dg_ref[:, O_GCONV:O_GCONV + D_CONV] = (dyc * (cb * cvv) * (sg * (1.0 + gc * (1.0 - sg)))).astype(BF16)
        dg_ref[:, O_CB:O_CB + D_CONV] = (dyc * cvv * sl).astype(BF16)
        extd[0:tq, :] = dcv
        ed = extd[...]
        d1 = pltpu.roll(ed, n_ext - 1, 0)[0:tq, :]
        d2 = pltpu.roll(ed, n_ext - 2, 0)[0:tq, :]
        du = cw_ref[2:3, :] * dcv + cw_ref[1:2, :] * d1 + cw_ref[0:1, :] * d2
        u = cc * ch
        dcw_ref[0:1, :] += jnp.sum(u * d2, axis=0, keepdims=True)
        dcw_ref[1:2, :] += jnp.sum(u * d1, axis=0, keepdims=True)
        dcw_ref[2:3, :] += jnp.sum(u * dcv, axis=0, keepdims=True)
        dg_ref[:, O_CH:O_CH + D_CONV] = (du * cc).astype(BF16)
        dg_ref[:, O_CC:O_CC + D_CONV] = (du * ch).astype(BF16)

    def rblock(width, offset):
        assert offset % width == 0
        blk = offset // width
        return pl.BlockSpec((tq, width), lambda i: (nt - 1 - i, blk))

    full = lambda shape: pl.BlockSpec(shape, lambda i: (0,) * len(shape))
    return pl.pallas_call(
        body, name=name,
        out_shape=(jax.ShapeDtypeStruct((T, D_MLA), BF16), jax.ShapeDtypeStruct((N_HEADS, nt, 1, tq), F32),
                   jax.ShapeDtypeStruct((T, NPP), BF16),
                   jax.ShapeDtypeStruct((4, 128, 128), F32), jax.ShapeDtypeStruct((1, D_POOL), F32),
                   jax.ShapeDtypeStruct((8, D_CONV), F32)),
        grid=(nt,),
        in_specs=[rblock(1024, 0), rblock(512, 1024), rblock(512, 1536),
                  rblock(1024, O_GMLA), rblock(512, O_GPOOL), rblock(512, O_CH), rblock(512, O_CB),
                  rblock(512, O_CC), rblock(512, O_GCONV), rblock(1024, 0), rblock(512, 0), rblock(512, 0),
                  full((4, 128, 128)), full((1, D_POOL)), full((8, D_CONV))],
        out_specs=(rblock(1024, 0), pl.BlockSpec((N_HEADS, 1, 1, tq), lambda i: (0, nt - 1 - i, 0, 0)),
                   rblock(N_GATED, 0), full((4, 128, 128)), full((1, D_POOL)), full((8, D_CONV))),
        scratch_shapes=[pltpu.VMEM((n_ext, D_POOL), F32), pltpu.VMEM((n_ext, D_CONV), F32)],
        compiler_params=_params(("arbitrary",)),
    )(dmix, dmix, dmix, proj, proj, proj, proj, proj, proj, o, pooled, cv, w_pool, pool_scale, conv_w)


def _rms_bwd(proj, dqn, dkvn, dkrope, dproj, q_g, kv_g, name, tq=256):
    T = proj.shape[0]
    n_lat = NPP - N_GATED

    def body(ql_ref, kvl_ref, dqn_ref, dkvn_ref, dkr_ref, _, qg_ref, kvg_ref, dlat_ref, dqg_ref, dkvg_ref):
        @pl.when(pl.program_id(0) == 0)
        def _():
            dqg_ref[...] = jnp.zeros_like(dqg_ref)
            dkvg_ref[...] = jnp.zeros_like(dkvg_ref)

        for x_ref, dy_ref, g_ref, c0, dg_ref in ((ql_ref, dqn_ref, qg_ref, 0, dqg_ref),
                                                 (kvl_ref, dkvn_ref, kvg_ref, Q_LORA, dkvg_ref)):
            x, dy = x_ref[...], dy_ref[...]
            r = lax.rsqrt(jnp.mean(x * x, axis=1, keepdims=True) + RMS_EPS)
            xr = x * r
            u = dy * g_ref[...]
            dlat_ref[:, c0:c0 + x.shape[1]] = (r * (u - xr * jnp.mean(u * xr, axis=1, keepdims=True))).astype(BF16)
            dg_ref[...] += jnp.sum(dy * xr, axis=0, keepdims=True)
        dlat_ref[:, Q_LORA + KV_LORA:] = dkr_ref[...]

    row = lambda w: pl.BlockSpec((tq, w), lambda i: (i, 0))
    vec = lambda w: pl.BlockSpec((1, w), lambda i: (0, 0))
    assert N_GATED % n_lat == 0
    return pl.pallas_call(
        body, name=name,
        out_shape=(jax.ShapeDtypeStruct((T, NPP), BF16),
                   jax.ShapeDtypeStruct((1, Q_LORA), F32), jax.ShapeDtypeStruct((1, KV_LORA), F32)),
        grid=(T // tq,),
        in_specs=[_pblock(tq, Q_LORA, O_QLAT), _pblock(tq, KV_LORA, O_KVLAT), row(Q_LORA), row(KV_LORA),
                  row(n_lat - Q_LORA - KV_LORA), pl.BlockSpec(memory_space=pl.ANY), vec(Q_LORA), vec(KV_LORA)],
        out_specs=(pl.BlockSpec((tq, n_lat), lambda i: (i, N_GATED // n_lat)), vec(Q_LORA), vec(KV_LORA)),
        input_output_aliases={5: 0},
        compiler_params=_params(("arbitrary",)),
    )(proj, proj, dqn, dkvn, dkrope, dproj, q_g, kv_g)


def _swap_halves(x, lo):
    return jnp.where(lo, pltpu.roll(x, 96, 1), pltpu.roll(x, 32, 1))


def _rope_fwd(q, kv, proj, cos_t, sin_t, name, tq=256):
    T = q.shape[0]

    def body(qn_ref, qr_ref, kv_ref, kr_ref, c_ref, s_ref, qc_ref, kc_ref):
        C, S = c_ref[...], s_ref[...]
        lane = lax.broadcasted_iota(jnp.int32, (tq, 128), 1)
        lo = (lane % ROPE) < (ROPE // 2)
        first = lane < ROPE

        def rope(x):
            return x * C + _swap_halves(x, lo) * S

        kr = jnp.where(first, rope(kr_ref[...]), 0.0).astype(BF16)
        for j in range(N_HEADS // 2):
            r = rope(qr_ref[:, j * 128:(j + 1) * 128])
            pair = (jnp.where(first, r, 0.0), jnp.where(first, pltpu.roll(r, 64, 1), 0.0))
            for hh in range(2):
                h = 2 * j + hh
                qc_ref[h, :, 0:NOPE] = qn_ref[:, h * NOPE:(h + 1) * NOPE].astype(BF16)
                qc_ref[h, :, NOPE:QC] = pair[hh].astype(BF16)
        for h in range(N_HEADS):
            kc_ref[h, :, 0:NOPE] = kv_ref[:, h * 256:h * 256 + NOPE]
            kc_ref[h, :, NOPE:QC] = kr

    out = jax.ShapeDtypeStruct((N_HEADS, T, QC), BF16)
    hblock = pl.BlockSpec((N_HEADS, tq, QC), lambda i: (0, i, 0))
    return pl.pallas_call(
        body, name=name, out_shape=(out, out), grid=(T // tq,),
        in_specs=[pl.BlockSpec((tq, 1024), lambda i: (i, 0)), pl.BlockSpec((tq, 512), lambda i: (i, 2)),
                  pl.BlockSpec((tq, 2048), lambda i: (i, 0)), _pblock(tq, 128, O_KROPE),
                  pl.BlockSpec((tq, 128), lambda i: (i, 0)), pl.BlockSpec((tq, 128), lambda i: (i, 0))],
        out_specs=(hblock, hblock),
        compiler_params=_params(("parallel",)),
    )(q, q, kv, proj, cos_t, sin_t)


def _rope_bwd(dqc, dkr, cos_t, sin_t, name, tq=256):
    T = dqc.shape[1]

    def body(dqc_ref, dkr_ref, c_ref, s_ref, dq_ref, dk_ref):
        C, S = c_ref[...], s_ref[...]
        lane = lax.broadcasted_iota(jnp.int32, (tq, 128), 1)
        lo = (lane % ROPE) < (ROPE // 2)
        first = lane < ROPE

        def unrope(dy):
            return dy * C - _swap_halves(dy, lo) * S

        acc = dkr_ref[0]
        for h in range(1, N_HEADS):
            acc = acc + dkr_ref[h]
        dk_ref[:, 0:128] = jnp.where(first, unrope(acc), 0.0).astype(BF16)
        dk_ref[:, 128:256] = jnp.zeros((tq, 128), BF16)
        for j in range(N_HEADS // 2):
            d0 = dqc_ref[2 * j, :, NOPE:QC]
            d1 = dqc_ref[2 * j + 1, :, NOPE:QC]
            comb = jnp.where(first, d0, pltpu.roll(d1, 64, 1))
            dq_ref[:, 1024 + j * 128:1024 + (j + 1) * 128] = unrope(comb).astype(BF16)
        for h in range(N_HEADS):
            dq_ref[:, h * NOPE:(h + 1) * NOPE] = dqc_ref[h, :, 0:NOPE].astype(BF16)

    tab = pl.BlockSpec((tq, 128), lambda i: (i, 0))
    return pl.pallas_call(
        body, name=name,
        out_shape=(jax.ShapeDtypeStruct((T, 1536), BF16), jax.ShapeDtypeStruct((T, 256), BF16)),
        grid=(T // tq,),
        in_specs=[pl.BlockSpec((N_HEADS, tq, QC), lambda i: (0, i, 0)),
                  pl.BlockSpec((N_HEADS, tq, 128), lambda i: (0, i, 0)), tab, tab],
        out_specs=(pl.BlockSpec((tq, 1536), lambda i: (i, 0)), pl.BlockSpec((tq, 256), lambda i: (i, 0))),
        compiler_params=_params(("parallel",)),
    )(dqc, dkr, cos_t, sin_t)


def _flash_fwd(qc, kc, kv, proj, mix, name):
    H, T, _ = qc.shape
    tt = ATT_TILE
    nt = T // tt
    sp = tt // ATT_CH

    def body(q_ref, k_ref, v_ref, g_ref, _, o_ref, y_ref, lse_ref, vt_sc, s_sc, acc_sc, m_sc, l_sc):
        i = pl.program_id(1)

        @pl.when(i == 0)
        def _():
            for c in range(nt):
                vt_sc[c] = v_ref[c * tt:(c + 1) * tt, :].astype(F32).T.astype(BF16)

        q = q_ref[0]

        def issue(c, slot):
            s_sc[slot] = lax.dot_general(k_ref[0, pl.ds(pl.multiple_of(c * tt, tt), tt), :], q, _DIMS["nt"],
                                         preferred_element_type=F32)

        def softmax_pv(c, slot, masked):
            s = s_sc[slot]
            if masked:
                krow = c * tt + lax.broadcasted_iota(jnp.int32, s.shape, 0)
                qcol = i * tt + lax.broadcasted_iota(jnp.int32, s.shape, 1)
                s = jnp.where(krow <= qcol, s, -jnp.inf)
            m = m_sc[...]
            m_new = jnp.maximum(m, jnp.max(s, axis=0, keepdims=True))
            p = jnp.exp2((s - m_new) * EXP2_SCALE)
            a = jnp.exp2((m - m_new) * EXP2_SCALE)
            l_sc[...] = a * l_sc[...] + jnp.sum(p, axis=0, keepdims=True)
            acc_sc[...] = a * acc_sc[...] + jnp.dot(vt_sc[c], p.astype(BF16), preferred_element_type=F32)
            m_sc[...] = m_new

        m_sc[...] = jnp.full_like(m_sc, -jnp.inf)
        l_sc[...] = jnp.zeros_like(l_sc)
        acc_sc[...] = jnp.zeros_like(acc_sc)
        issue(0, 0)

        def pair(t, carry):
            issue(2 * t + 1, 1)
            softmax_pv(2 * t, 0, False)
            issue(2 * t + 2, 0)
            softmax_pv(2 * t + 1, 1, False)
            return carry

        lax.fori_loop(0, i // 2, pair, 0)

        @pl.when(i % 2 == 1)
        def _():
            issue(i, 1)
            softmax_pv(i - 1, 0, False)
            softmax_pv(i, 1, True)

        @pl.when(i % 2 == 0)
        def _():
            softmax_pv(i, 0, True)

        l = l_sc[...]
        o = (acc_sc[...] / l).T
        o_ref[...] = o
        lse = m_sc[...] * ATTN_SCALE + jnp.log(l)
        for r in range(sp):
            lse_ref[0, r] = lse[:, r * ATT_CH:(r + 1) * ATT_CH]
        g = g_ref[...]
        y_ref[...] = (o * (g * _sigmoid(g))).astype(BF16)

    return pl.pallas_call(
        body, name=name,
        out_shape=(jax.ShapeDtypeStruct((T, D_MLA), F32), jax.ShapeDtypeStruct((T, D_MIX), BF16),
                   jax.ShapeDtypeStruct((H, T // ATT_CH, 1, ATT_CH), F32)),
        grid=(H, nt),
        in_specs=[pl.BlockSpec((1, tt, QC), lambda h, i: (h, i, 0)),
                  pl.BlockSpec((1, T, QC), lambda h, i: (h, 0, 0)),
                  pl.BlockSpec((T, V_DIM), lambda h, i: (0, 2 * h + 1)),
                  pl.BlockSpec((tt, V_DIM), lambda h, i: (i, h)),
                  pl.BlockSpec(memory_space=pl.ANY)],
        input_output_aliases={4: 1},
        out_specs=(pl.BlockSpec((tt, V_DIM), lambda h, i: (i, h)),
                   pl.BlockSpec((tt, V_DIM), lambda h, i: (i, h)),
                   pl.BlockSpec((1, sp, 1, ATT_CH), lambda h, i: (h, i, 0, 0))),
        scratch_shapes=[pltpu.VMEM((nt, V_DIM, tt), BF16), pltpu.VMEM((2, tt, tt), F32),
                        pltpu.VMEM((V_DIM, tt), F32), pltpu.VMEM((1, tt), F32), pltpu.VMEM((1, tt), F32)],
        compiler_params=_params(("parallel", "arbitrary")),
    )(qc, kc, kv, proj, mix)


def _flash_bwd(qc, kc, kv, do, lse, delta, name):
    H, T, _ = qc.shape
    tt = ATT_TILE
    nt = T // tt
    sp = tt // ATT_CH

    def body(q_ref, k_ref, v_ref, do_ref, lse_ref, dl_ref, dq_ref, dkv_ref, dkr_ref, dqt_sc, dk_sc, dv_sc, s_sc,
             dp_sc):
        j = pl.program_id(1)

        @pl.when(j == 0)
        def _():
            dqt_sc[...] = jnp.zeros_like(dqt_sc)

        dk_sc[...] = jnp.zeros_like(dk_sc)
        dv_sc[...] = jnp.zeros_like(dv_sc)
        k = k_ref[0]
        v = v_ref[...]
        kt = k.astype(F32).T.astype(BF16)

        def operands(c):
            q0 = pl.multiple_of(c * tt, tt)
            return q_ref[0, pl.ds(q0, tt), :], do_ref[pl.ds(q0, tt), :]

        def stat_row(ref, c):
            return jnp.concatenate([ref[0, sp * c + r] for r in range(sp)], axis=1)

        def early(c, slot):
            q, dov = operands(c)
            s_sc[slot] = lax.dot_general(k, q, _DIMS["nt"], preferred_element_type=F32)
            dp_sc[slot] = lax.dot_general(v, dov, _DIMS["nt"], preferred_element_type=F32)

        def late(c, slot, masked):
            q, dov = operands(c)
            s, dp = s_sc[slot], dp_sc[slot]
            if masked:
                krow = j * tt + lax.broadcasted_iota(jnp.int32, s.shape, 0)
                qcol = c * tt + lax.broadcasted_iota(jnp.int32, s.shape, 1)
                s = jnp.where(krow <= qcol, s, -jnp.inf)
            p = jnp.exp2(s * EXP2_SCALE - stat_row(lse_ref, c) * LOG2E)
            ds = (p * (dp - stat_row(dl_ref, c)) * ATTN_SCALE).astype(BF16)
            dv_sc[...] += jnp.dot(p.astype(BF16), dov, preferred_element_type=F32)
            dk_sc[...] += jnp.dot(ds, q, preferred_element_type=F32)
            dqt_sc[c] += jnp.dot(kt, ds, preferred_element_type=F32)

        early(j, 0)

        @pl.when(j < nt - 1)
        def _():
            early(j + 1, 1)

        late(j, 0, True)
        n_rest = nt - 1 - j

        def pair(u, carry):
            a = j + 1 + 2 * u
            early(a + 1, 0)
            late(a, 1, False)

            @pl.when(a + 2 <= nt - 1)
            def _():
                early(a + 2, 1)

            late(a + 1, 0, False)
            return carry

        lax.fori_loop(0, n_rest // 2, pair, 0)

        @pl.when(n_rest % 2 == 1)
        def _():
            late(nt - 1, 1, False)

        dk = dk_sc[...]
        dkv_ref[:, 0:NOPE] = dk[:, 0:NOPE].astype(BF16)
        dkv_ref[:, NOPE:] = dv_sc[...].astype(BF16)
        dkr_ref[0] = dk[:, NOPE:]

        @pl.when(j == nt - 1)
        def _():
            for c in range(nt):
                dq_ref[0, c * tt:(c + 1) * tt, :] = dqt_sc[c].T

    head = lambda h, j: (h, 0, 0)
    stat = pl.BlockSpec((1, T // ATT_CH, 1, ATT_CH), lambda h, j: (h, 0, 0, 0))
    return pl.pallas_call(
        body, name=name,
        out_shape=(jax.ShapeDtypeStruct((H, T, QC), F32), jax.ShapeDtypeStruct((T, 2 * D_MLA), BF16),
                   jax.ShapeDtypeStruct((H, T, 128), F32)),
        grid=(H, nt),
        in_specs=[pl.BlockSpec((1, T, QC), head),
                  pl.BlockSpec((1, tt, QC), lambda h, j: (h, j, 0)),
                  pl.BlockSpec((tt, V_DIM), lambda h, j: (j, 2 * h + 1)),
                  pl.BlockSpec((T, V_DIM), lambda h, j: (0, h)),
                  stat, stat],
        out_specs=(pl.BlockSpec((1, T, QC), head),
                   pl.BlockSpec((tt, 256), lambda h, j: (j, h)),
                   pl.BlockSpec((1, tt, 128), lambda h, j: (h, j, 0))),
        scratch_shapes=[pltpu.VMEM((nt, QC, tt), F32), pltpu.VMEM((tt, QC), F32), pltpu.VMEM((tt, V_DIM), F32),
                        pltpu.VMEM((2, tt, tt), F32), pltpu.VMEM((2, tt, tt), F32)],
        compiler_params=_params(("parallel", "arbitrary")),
    )(qc, kc, kv, do, lse, delta)


def _adamw(lands, w, m, v, name, rows):
    L, R, C = w.shape
    assert R % rows == 0 and len(lands) == L
    nb = R // rows
    c1 = 1.0 - ADAM_B1 ** ADAM_STEP
    c2 = 1.0 - ADAM_B2 ** ADAM_STEP

    def body(*refs):
        land_refs = refs[:L]
        w_ref, m_ref, v_ref, g_ref, d_ref, nm_ref, nv_ref, g_sc = refs[L:]
        for ll in range(L):
            @pl.when(pl.program_id(0) == ll)
            def _(land_ref=land_refs[ll]):
                g = land_ref[0].astype(F32)
                for s in range(1, N_DEV):
                    g = g + land_ref[s].astype(F32)
                g_sc[...] = g

        g = g_sc[...]
        nm = ADAM_B1 * m_ref[0] + (1.0 - ADAM_B1) * g
        nv = ADAM_B2 * v_ref[0] + (1.0 - ADAM_B2) * (g * g)
        g_ref[0] = g
        nm_ref[0] = nm
        nv_ref[0] = nv
        d_ref[0] = -ADAM_LR * ((nm / c1) / (jnp.sqrt(nv / c2) + ADAM_EPS) + ADAM_WD * w_ref[0])

    def land_spec(ll):
        return pl.BlockSpec((N_DEV, rows, C),
                            lambda l, i: (0, jnp.where(l < ll, 0, jnp.where(l > ll, nb - 1, i)), 0))

    blk = pl.BlockSpec((1, rows, C), lambda l, i: (l, i, 0))
    out = jax.ShapeDtypeStruct((L, R, C), F32)
    return pl.pallas_call(
        body, name=name, out_shape=(out, out, out, out), grid=(L, nb),
        in_specs=[land_spec(ll) for ll in range(L)] + [blk, blk, blk],
        out_specs=(blk, blk, blk, blk),
        scratch_shapes=[pltpu.VMEM((rows, C), F32)],
        compiler_params=_params(("arbitrary", "arbitrary")),
    )(*lands, w, m, v)


def _mesh_pos():
    return lax.axis_index("x"), lax.axis_index("y"), lax.axis_index("c")


def _all_gather(arrays, name):
    n = len(arrays)

    def body(*refs):
        ins, outs = refs[:n], refs[n:2 * n]
        send_sems, recv_sems, local_sems = refs[2 * n:]
        x, y, c = _mesh_pos()
        me, sibling = (x, y, c), (x, y, 1 - c)
        chips = [(1 - x, y), (x, 1 - y), (1 - x, 1 - y)]

        def slot(a, pos):
            px, py, pc = pos
            return outs[a].at[4 * px + 2 * py + pc]

        def copy(a, k, block, to, src=None):
            return pltpu.make_async_remote_copy(
                src_ref=slot(a, block) if src is None else src, dst_ref=slot(a, block),
                send_sem=send_sems.at[a * 7 + k], recv_sem=recv_sems.at[a * 7 + k],
                device_id=to, device_id_type=MESH_ID)

        mine, first, passed = [], [], []
        for a in range(n):
            cp = pltpu.make_async_copy(ins[a], slot(a, me), local_sems.at[a])
            cp.start()
            mine.append(cp)
            cps = [copy(a, 0, me, sibling, src=ins[a])]
            cps += [copy(a, 1 + j, me, (*chip, c), src=ins[a]) for j, chip in enumerate(chips)]
            for cp in cps:
                cp.start()
            first += cps
        for j, chip in enumerate(chips):
            for a in range(n):
                copy(a, 1 + j, (*chip, c), me).wait_recv()
                cp = copy(a, 4 + j, (*chip, c), sibling)
                cp.start()
                passed.append(cp)
        for a in range(n):
            copy(a, 0, sibling, me).wait_recv()
            for j, chip in enumerate(chips):
                copy(a, 4 + j, (*chip, 1 - c), me).wait_recv()
        for cp in first + passed:
            cp.wait_send()
        for cp in mine:
            cp.wait()

    hbm = pl.BlockSpec(memory_space=pltpu.HBM)
    return pl.pallas_call(
        body, name=name,
        out_shape=tuple(jax.ShapeDtypeStruct((N_DEV,) + a.shape, a.dtype) for a in arrays),
        in_specs=[hbm] * n, out_specs=tuple([hbm] * n),
        scratch_shapes=[pltpu.SemaphoreType.DMA((7 * n,)), pltpu.SemaphoreType.DMA((7 * n,)),
                        pltpu.SemaphoreType.DMA((n,))],
    )(*arrays)


def _exchange(arrays, name):
    n = len(arrays)

    def body(*refs):
        ins, outs = refs[:n], refs[n:2 * n]
        send_sems, recv_sems, local_sems = refs[2 * n:]
        x, y, c = _mesh_pos()
        my_idx = 4 * x + 2 * y + c
        copies, local = [], []
        for a in range(n):
            cp = pltpu.make_async_copy(ins[a].at[my_idx], outs[a].at[my_idx], local_sems.at[a])
            cp.start()
            local.append(cp)
            for k in range(1, N_DEV):
                px = 1 - x if k & 4 else x
                py = 1 - y if k & 2 else y
                pc = 1 - c if k & 1 else c
                cp = pltpu.make_async_remote_copy(
                    src_ref=ins[a].at[4 * px + 2 * py + pc], dst_ref=outs[a].at[my_idx],
                    send_sem=send_sems.at[a * 7 + k - 1], recv_sem=recv_sems.at[a * 7 + k - 1],
                    device_id=(px, py, pc), device_id_type=MESH_ID)
                cp.start()
                copies.append(cp)
        for cp in copies:
            cp.wait()
        for cp in local:
            cp.wait()

    hbm = pl.BlockSpec(memory_space=pltpu.HBM)
    return pl.pallas_call(
        body, name=name,
        out_shape=tuple(jax.ShapeDtypeStruct(a.shape, a.dtype) for a in arrays),
        in_specs=[hbm] * n, out_specs=tuple([hbm] * n),
        scratch_shapes=[pltpu.SemaphoreType.DMA((7 * n,)), pltpu.SemaphoreType.DMA((7 * n,)),
                        pltpu.SemaphoreType.DMA((n,))],
    )(*arrays)


_HBM = pl.BlockSpec(memory_space=pltpu.HBM)
_SEM = pl.BlockSpec(memory_space=pltpu.SEMAPHORE)
_EFFECT = pltpu.SideEffectType.DATAFLOW_SIDE_EFFECTING
N_PEERS = N_DEV - 1


def _peer(k):
    x, y, c = _mesh_pos()
    return (1 - x if k & 4 else x, 1 - y if k & 2 else y, 1 - c if k & 1 else c)


def _split_start(srcs, scatter, after, name):
    n = len(srcs)
    zones = [jax.ShapeDtypeStruct(s.shape if scatter else ((N_DEV,) + s.shape), s.dtype) for s in srcs]

    def body(*refs):
        src, zone = refs[:n], refs[n:2 * n]
        outs = refs[2 * n + 1:]
        send, recv, token = outs[:n], outs[n:2 * n], outs[4 * n]
        x, y, c = _mesh_pos()
        my_idx = 4 * x + 2 * y + c
        for a in range(n):
            pltpu.make_async_copy(src[a].at[my_idx] if scatter else src[a],
                                  zone[a].at[N_PEERS] if scatter else zone[a].at[my_idx], recv[a]).start()
            for k in range(1, N_DEV):
                px, py, pc = _peer(k)
                pltpu.make_async_remote_copy(
                    src_ref=src[a].at[4 * px + 2 * py + pc] if scatter else src[a],
                    dst_ref=zone[a].at[k - 1] if scatter else zone[a].at[my_idx],
                    send_sem=send[a], recv_sem=recv[a], device_id=(px, py, pc), device_id_type=MESH_ID).start()
        token[...] = jnp.zeros_like(token)

    hbm = lambda a: pltpu.with_memory_space_constraint(a, pltpu.HBM)
    outs = pl.pallas_call(
        body, name=name,
        out_shape=tuple([pltpu.SemaphoreType.DMA(())] * (2 * n)
                        + [pltpu.HBM(s.shape, s.dtype) for s in srcs]
                        + [pltpu.HBM(z.shape, z.dtype) for z in zones]
                        + [jax.ShapeDtypeStruct((8, 128), F32)]),
        in_specs=[_HBM] * (2 * n) + [pl.BlockSpec(memory_space=pl.ANY)],
        out_specs=tuple([_SEM] * (2 * n) + [_HBM] * (2 * n) + [pl.BlockSpec(memory_space=pltpu.VMEM)]),
        input_output_aliases={**{a: 2 * n + a for a in range(n)}, **{n + a: 3 * n + a for a in range(n)}},
        compiler_params=pltpu.CompilerParams(has_side_effects=_EFFECT),
    )(*[hbm(s) for s in srcs], *[hbm(lax.empty(z.shape, z.dtype)) for z in zones], after)
    return outs[:n], outs[n:2 * n], outs[2 * n:3 * n], outs[3 * n:4 * n], outs[4 * n]


def _split_wait(send, recv, srcs, zones, after, name):
    n = len(srcs)

    def body(*refs):
        zone = refs[n:2 * n]
        send_sems, recv_sems = refs[2 * n:3 * n], refs[3 * n:4 * n]
        x, y, c = _mesh_pos()
        for a in range(n):
            seven = zone[a].at[pl.ds(0, N_PEERS)]
            pltpu.make_async_remote_copy(src_ref=seven, dst_ref=seven, send_sem=send_sems[a], recv_sem=recv_sems[a],
                                         device_id=(x, y, 1 - c), device_id_type=MESH_ID).wait_send()
            pltpu.make_async_remote_copy(src_ref=zone[a], dst_ref=zone[a], send_sem=send_sems[a],
                                         recv_sem=recv_sems[a], device_id=(x, y, 1 - c),
                                         device_id_type=MESH_ID).wait_recv()

    outs = pl.pallas_call(
        body, name=name,
        out_shape=tuple([pltpu.HBM(s.shape, s.dtype) for s in srcs] + [pltpu.HBM(z.shape, z.dtype) for z in zones]),
        in_specs=[_HBM] * (2 * n) + [_SEM] * (2 * n) + [pl.BlockSpec(memory_space=pl.ANY)],
        out_specs=tuple([_HBM] * (2 * n)),
        input_output_aliases={a: a for a in range(2 * n)},
        compiler_params=pltpu.CompilerParams(has_side_effects=_EFFECT),
    )(*srcs, *zones, *send, *recv, after)
    return outs[:n], outs[n:]


def _cat_blocks(g, axis):
    return jnp.concatenate([g[d] for d in range(N_DEV)], axis=axis)


N_LATENT = Q_LORA + KV_LORA + ROPE
W_SHARD = D_IN_PROJ // N_DEV


def _ref_cols(lo, hi):
    out = []
    if lo < N_LATENT:
        out.append((N_GATED + lo, N_GATED + min(hi, N_LATENT)))
    if hi > N_LATENT:
        out.append((max(lo, N_LATENT) - N_LATENT, hi - N_LATENT))
    return out


def _permute_w_in(blocks):
    pieces = []
    for lo, hi in ((N_LATENT, D_IN_PROJ), (0, N_LATENT)):
        for d in range(N_DEV):
            a, b = max(lo, d * W_SHARD), min(hi, (d + 1) * W_SHARD)
            if a < b:
                pieces.append(blocks[d][:, a - d * W_SHARD:b - d * W_SHARD])
    pieces.append(jnp.zeros((blocks.shape[1], NPP - D_IN_PROJ), blocks.dtype))
    return jnp.concatenate(pieces, axis=1)


def _split_w_in(w):
    slabs = []
    for d in range(N_DEV):
        parts = [w[:, a:b] for a, b in _ref_cols(d * W_SHARD, (d + 1) * W_SHARD)]
        slabs.append(parts[0] if len(parts) == 1 else jnp.concatenate(parts, axis=1))
    return jnp.stack(slabs)


def _permute_w_uq(w):
    w3 = w.reshape(w.shape[0], N_HEADS, NOPE + ROPE)
    return jnp.concatenate([w3[:, :, :NOPE].reshape(w.shape[0], -1), w3[:, :, NOPE:].reshape(w.shape[0], -1)], axis=1)


def _unpermute_w_uq(w):
    nope = w[:, :N_HEADS * NOPE].reshape(w.shape[0], N_HEADS, NOPE)
    rope = w[:, N_HEADS * NOPE:].reshape(w.shape[0], N_HEADS, ROPE)
    return jnp.concatenate([nope, rope], axis=2).reshape(w.shape[0], -1)


_SMALL = (("emb_ln_g", 16), ("emb_ln_b", 16), ("q_norm_g", 8), ("kv_norm_g", 8), ("w_pool", 1024),
          ("pool_scale", 8), ("b_out", 32), ("ln_g", 32), ("ln_b", 32))
SMALL_ROWS = sum(r for _, r in _SMALL)


def _pack_small(d):
    parts = []
    for name, rows in _SMALL:
        flat = d[name].reshape(-1)
        flat = jnp.pad(flat, (0, rows * 128 - flat.shape[0]))
        parts.append(flat.reshape(rows, 128))
    return jnp.concatenate(parts, axis=0)


def _unpack_small(packed, shapes):
    out, r0 = {}, 0
    for name, rows in _SMALL:
        size = 1
        for s in shapes[name]:
            size *= s
        out[name] = packed[r0:r0 + rows].reshape(-1)[:size].reshape(shapes[name])
        r0 += rows
    return out


def _rope_tables(positions):
    half = ROPE // 2
    inv_freq = ROPE_THETA ** (-jnp.arange(half, dtype=F32) / half)
    ang = positions.astype(F32)[:, None] * inv_freq
    cos, sin = jnp.cos(ang), jnp.sin(ang)
    return jnp.concatenate([cos, cos, cos, cos], axis=1), jnp.concatenate([-sin, sin, -sin, sin], axis=1)


def _local_step(x, positions, target, emb_g, emb_b, layer_weights, layer_weights_rest, on_sharded_grads,
                first_after=None):
    cos_t, sin_t = _rope_tables(positions)
    h, hb = _ln_fwd(x, emb_g, emb_b, "emb_ln_fwd", after=first_after)
    saved = []
    for l in range(DEPTH):
        W = layer_weights(l, h)
        proj = _mm(hb, W["w_in"], "nn", F32, "proj_fwd")
        qn, kvn, pooled, cv, mix = _mix_fwd(proj, W["q_norm_g"], W["kv_norm_g"], W["w_pool"], W["pool_scale"],
                                            W["conv_w"], "mix_fwd")
        rest, token = layer_weights_rest(l, proj)
        W = {**W, **rest}
        q = _mm(qn, W["w_uq"], "nn", F32, "q_up_fwd", after=token)
        kv = _mm(kvn, W["w_ukv"], "nn", BF16, "kv_up_fwd")
        qc, kc = _rope_fwd(q, kv, proj, cos_t, sin_t, "rope_fwd")
        o, mix, lse = _flash_fwd(qc, kc, kv, proj, mix, "flash_fwd")
        z = _mm(mix, W["w_out"], "nn", F32, "out_fwd", res=h, bias=W["b_out"], alpha=ALPHA)
        saved.append((W, hb, proj, qn, kvn, pooled, cv, kv, qc, kc, o, lse, mix, z))
        h, hb = _ln_fwd(z, W["ln_g"], W["ln_b"], "ln_fwd")
    sq, dh = _loss_head(h, target, "loss_head")

    grads = {k: [None] * DEPTH for k in ("q_norm_g", "kv_norm_g", "w_pool", "pool_scale", "conv_w", "b_out", "ln_g",
                                         "ln_b")}
    for l in reversed(range(DEPTH)):
        W, hb_in, proj, qn, kvn, pooled, cv, kv, qc, kc, o, lse, mix, z = saved[l]
        sharded = {}
        dz, dzb, grads["ln_g"][l], grads["ln_b"][l], grads["b_out"][l] = _ln_bwd(dh, z, W["ln_g"], "ln_bwd")
        dmix = _mm(dzb, W["w_out"], "nt", F32, "out_bwd_x")
        sharded["w_out"] = _mm(mix, dzb, "tn", GRAD_XFER, "out_bwd_w")
        do, delta, dproj, grads["w_pool"][l], grads["pool_scale"][l], grads["conv_w"][l] = _mix_bwd(
            dmix, proj, o, pooled, cv, W["w_pool"], W["pool_scale"], W["conv_w"], "mix_bwd")
        dqc, dkv, dkr = _flash_bwd(qc, kc, kv, do, lse, delta, "flash_bwd")
        dq, dkrope = _rope_bwd(dqc, dkr, cos_t, sin_t, "rope_bwd")
        dqn = _mm(dq, W["w_uq"], "nt", F32, "q_up_bwd_x")
        sharded["w_uq"] = _mm(qn, dq, "tn", GRAD_XFER, "q_up_bwd_w")
        dkvn = _mm(dkv, W["w_ukv"], "nt", F32, "kv_up_bwd_x")
        sharded["w_ukv"] = _mm(kvn, dkv, "tn", GRAD_XFER, "kv_up_bwd_w")
        token = on_sharded_grads(l, sharded)
        dproj, grads["q_norm_g"][l], grads["kv_norm_g"][l] = _rms_bwd(
            proj, dqn, dkvn, dkrope, dproj, W["q_norm_g"], W["kv_norm_g"], "rms_bwd")
        d_w_in = _mm(hb_in, dproj, "tn", GRAD_XFER, "proj_bwd_w", after=token)
        token = on_sharded_grads(l, {"w_in": d_w_in})
        dh = _mm(dproj, W["w_in"], "nt", F32, "proj_bwd_x", res=dz, alpha=ALPHA, after=token)
    grad_x, _, grads["emb_ln_g"], grads["emb_ln_b"], _ = _ln_bwd(dh, x, emb_g, "emb_ln_bwd")
    return sq, grad_x, grads


def kernel(x, positions, emb_ln_g, emb_ln_b, w_in, q_norm_g, kv_norm_g, w_uq, w_ukv, w_pool, pool_scale, conv_w, w_out, b_out, ln_g, ln_b, loss_target, m_emb_ln_g, m_emb_ln_b, m_w_in, m_q_norm_g, m_kv_norm_g, m_w_uq, m_w_ukv, m_w_pool, m_pool_scale, m_conv_w, m_w_out, m_b_out, m_ln_g, m_ln_b, v_emb_ln_g, v_emb_ln_b, v_w_in, v_q_norm_g, v_kv_norm_g, v_w_uq, v_w_ukv, v_w_pool, v_pool_scale, v_conv_w, v_w_out, v_b_out, v_ln_g, v_ln_b):
    weights = dict(emb_ln_g=emb_ln_g, emb_ln_b=emb_ln_b, w_in=w_in, q_norm_g=q_norm_g, kv_norm_g=kv_norm_g,
                   w_uq=w_uq, w_ukv=w_ukv, w_pool=w_pool, pool_scale=pool_scale, conv_w=conv_w, w_out=w_out,
                   b_out=b_out, ln_g=ln_g, ln_b=ln_b)
    mom1 = dict(emb_ln_g=m_emb_ln_g, emb_ln_b=m_emb_ln_b, w_in=m_w_in, q_norm_g=m_q_norm_g, kv_norm_g=m_kv_norm_g,
                w_uq=m_w_uq, w_ukv=m_w_ukv, w_pool=m_w_pool, pool_scale=m_pool_scale, conv_w=m_conv_w,
                w_out=m_w_out, b_out=m_b_out, ln_g=m_ln_g, ln_b=m_ln_b)
    mom2 = dict(emb_ln_g=v_emb_ln_g, emb_ln_b=v_emb_ln_b, w_in=v_w_in, q_norm_g=v_q_norm_g, kv_norm_g=v_kv_norm_g,
                w_uq=v_w_uq, w_ukv=v_w_ukv, w_pool=v_w_pool, pool_scale=v_pool_scale, conv_w=v_conv_w,
                w_out=v_w_out, b_out=v_b_out, ln_g=v_ln_g, ln_b=v_ln_b)

    big = ("w_in", "w_uq", "w_ukv", "w_out")

    conv_pad = jnp.zeros((8, 128), F32).at[0:DEPTH * CONV_WIDTH, 0:64].set(conv_w.reshape(DEPTH * CONV_WIDTH, 64))
    shard = lambda k, l: weights[k][l].astype(BF16)
    w_in0, conv_all = _all_gather([shard("w_in", 0), conv_pad], "w_in0_all_gather")
    rest0 = _split_start([shard(k, 0) for k in big[1:]], False, w_in0, "weights0_rest_start")
    conv_full = _cat_blocks(conv_all[:, 0:DEPTH * CONV_WIDTH, 0:64], 1).reshape(DEPTH, CONV_WIDTH, D_CONV)
    conv_full = jnp.pad(conv_full, ((0, 0), (0, 8 - CONV_WIDTH), (0, 0)))
    fetched = {}

    def layer_weights(l, ready):
        if l == 0:
            w_in_blocks = w_in0
        else:
            fetched[1] = _split_wait(*fetched["w1"][:4], ready, "weights1_wait")[1]
            w_in_blocks = fetched[1][0]
        return dict(
            w_in=_permute_w_in(w_in_blocks), conv_w=conv_full[l],
            q_norm_g=q_norm_g[l].reshape(1, -1), kv_norm_g=kv_norm_g[l].reshape(1, -1),
            w_pool=w_pool[l].astype(BF16), pool_scale=pool_scale[l].reshape(1, -1), b_out=b_out[l].reshape(1, -1),
            ln_g=ln_g[l].reshape(1, -1), ln_b=ln_b[l].reshape(1, -1))

    def layer_weights_rest(l, ready):
        token = None
        if l == 0:
            blocks = _split_wait(*rest0[:4], ready, "weights0_rest_wait")[1]
            fetched["w1"] = _split_start([shard(k, 1) for k in big], False, blocks[0], "weights1_start")
            token = fetched["w1"][4]
        else:
            blocks = fetched[1][1:]
        return dict(w_uq=_permute_w_uq(_cat_blocks(blocks[0], 1)), w_ukv=_cat_blocks(blocks[1], 1),
                    w_out=_cat_blocks(blocks[2], 0)), token

    by_dest = dict(
        w_in=_split_w_in,
        w_uq=lambda g: _unpermute_w_uq(g).reshape(Q_LORA, N_DEV, -1).transpose(1, 0, 2),
        w_ukv=lambda g: g.reshape(KV_LORA, N_DEV, -1).transpose(1, 0, 2),
        w_out=lambda g: g.reshape(N_DEV, -1, D_MODEL))
    in_flight = []

    def on_sharded_grads(l, g):
        names = [k for k in big if k in g]
        srcs = [by_dest[k](g[k]) for k in names]
        started = _split_start(srcs, True, srcs[0], "grads%d_%s_start" % (l, names[0]))
        in_flight.append((l, names, started[:4]))
        return started[4]

    sq, grad_x, G = _local_step(x[0], positions[0], loss_target[0], emb_ln_g.reshape(1, -1),
                                emb_ln_b.reshape(1, -1), layer_weights, layer_weights_rest, on_sharded_grads,
                                first_after=rest0[4])
    loss = lax.psum(sq[0, 0] * (0.5 / D_MODEL), ("x", "y", "c"))

    res = {}
    landed = {}
    for l, names, started in in_flight:
        zones = _split_wait(*started, grad_x, "grads%d_%s_wait" % (l, names[0]))[1]
        for k, zone in zip(names, zones):
            landed[k, l] = zone
    for name, rows in (("w_in", 256), ("w_uq", 256), ("w_ukv", 256), ("w_out", 128)):
        res[name] = _adamw([landed[name, l] for l in range(DEPTH)], weights[name], mom1[name], mom2[name],
                           "adamw_" + name, rows)

    d_conv = jnp.stack([G["conv_w"][l][0:CONV_WIDTH] for l in range(DEPTH)])
    d_conv = d_conv.reshape(DEPTH * CONV_WIDTH, N_DEV, 64).transpose(1, 0, 2)
    d_conv = jnp.zeros((N_DEV, 8, 128), F32).at[:, 0:DEPTH * CONV_WIDTH, 0:64].set(d_conv)
    small = dict(emb_ln_g=G["emb_ln_g"], emb_ln_b=G["emb_ln_b"])
    for k in ("q_norm_g", "kv_norm_g", "w_pool", "pool_scale", "b_out", "ln_g", "ln_b"):
        small[k] = jnp.stack(G[k])
    d_small = jnp.broadcast_to(_pack_small(small)[None], (N_DEV, SMALL_ROWS, 128))
    l_conv, l_small = _exchange([d_conv, d_small], "small_gradient_exchange")
    conv_shard = lambda a: jnp.zeros((8, 128), F32).at[0:DEPTH * CONV_WIDTH, 0:64].set(a.reshape(-1, 64))
    conv_res = _adamw([l_conv], conv_shard(conv_w)[None], conv_shard(m_conv_w)[None], conv_shard(v_conv_w)[None],
                      "adamw_conv_w", 8)
    res["conv_w"] = tuple(o[0, 0:DEPTH * CONV_WIDTH, 0:64].reshape(DEPTH, CONV_WIDTH, 64) for o in conv_res)
    small_res = _adamw([l_small], _pack_small(weights)[None], _pack_small(mom1)[None], _pack_small(mom2)[None],
                       "adamw_small", 392)
    shapes = {k: weights[k].shape for k, _ in _SMALL}
    unpacked = [_unpack_small(o[0], shapes) for o in small_res]
    for k, _ in _SMALL:
        res[k] = tuple(u[k] for u in unpacked)

    order = ("emb_ln_g", "emb_ln_b", "w_in", "q_norm_g", "kv_norm_g", "w_uq", "w_ukv", "w_pool", "pool_scale",
             "conv_w", "w_out", "b_out", "ln_g", "ln_b")
    return (loss, grad_x[None], *[res[k][0] for k in order], *[res[k][1] for k in order],
            *[res[k][2] for k in order], *[res[k][3] for k in order])
```

```python
import jax
import jax.numpy as jnp
from jax import lax
from jax.experimental import pallas as pl
from jax.experimental.pallas import tpu as pltpu

F32 = jnp.float32
BF16 = jnp.bfloat16

N_DEV = 8
D_MODEL = 2048
DEPTH = 2
N_HEADS = 8
NOPE = 128
ROPE = 64
V_DIM = 128
Q_LORA = 512
KV_LORA = 256
D_MLA = N_HEADS * V_DIM
D_POOL = 512
D_CONV = 512
POOL_WINDOWS = (2, 4, 8, 16)
POOL_GROUP = 128
CONV_WIDTH = 3
D_MIX = D_MLA + D_POOL + D_CONV
D_IN_PROJ = 4928
ROPE_THETA = 10000.0
LN_EPS = 1e-5
RMS_EPS = 1e-6
ALPHA = (2 * DEPTH) ** 0.25
ATTN_SCALE = (NOPE + ROPE) ** -0.5
ADAM_LR = 0.001
ADAM_B1 = 0.9
ADAM_B2 = 0.999
ADAM_EPS = 1e-08
ADAM_WD = 0.01
ADAM_STEP = 10

O_GMLA, O_PIN, O_GPOOL, O_CH, O_CB, O_CC, O_GCONV, O_QLAT, O_KVLAT, O_KROPE = (
    0, 1024, 1536, 2048, 2560, 3072, 3584, 4096, 4608, 4864)
NPP = 5120
N_GATED = O_QLAT
QC = NOPE + 2 * ROPE
HALO = 16
ATT_TILE = 512
ATT_CH = 256
LOG2E = 1.4426950408889634
EXP2_SCALE = ATTN_SCALE * LOG2E

GRAD_XFER = BF16
VMEM_LIMIT = 48 * 1024 * 1024
MESH_ID = pl.DeviceIdType.MESH


def _params(sem=None):
    return pltpu.CompilerParams(dimension_semantics=sem, vmem_limit_bytes=VMEM_LIMIT)


def _sigmoid(x):
    return 1.0 / (1.0 + jnp.exp(-x))


def _tile(dim, target):
    if dim <= target:
        return dim
    t = target - target % 128
    while dim % t:
        t -= 128
    return t


_DIMS = {"nn": (((1,), (0,)), ((), ())), "nt": (((1,), (1,)), ((), ())), "tn": (((0,), (0,)), ((), ()))}


def _mm(a, b, mode, out_dtype, name, res=None, bias=None, alpha=1.0, tm=1024, tn=1024, tk=2048, after=None):
    if mode == "nn":
        (M, K), (K2, N) = a.shape, b.shape
    elif mode == "nt":
        (M, K), (N, K2) = a.shape, b.shape
    else:
        (K, M), (K2, N) = a.shape, b.shape
    assert K == K2
    tm, tn, tk = _tile(M, tm), _tile(N, tn), _tile(K, tk)
    nk = K // tk
    has_res, has_bias = res is not None, bias is not None

    def body(*refs):
        a_ref, b_ref = refs[0], refs[1]
        pos = 2
        res_ref = bias_ref = None
        if has_res:
            res_ref = refs[pos]
            pos += 1
        if has_bias:
            bias_ref = refs[pos]
            pos += 1
        def finish(r, o_ref):
            if has_bias:
                r = r + bias_ref[...]
            if has_res:
                r = alpha * res_ref[...] + r
            o_ref[...] = r.astype(out_dtype)

        part = lax.dot_general(a_ref[...].astype(BF16), b_ref[...].astype(BF16), _DIMS[mode],
                               preferred_element_type=F32)
        if nk == 1:
            finish(part, refs[-1])
            return
        o_ref, acc_ref = refs[-2], refs[-1]
        k = pl.program_id(2)

        @pl.when(k == 0)
        def _():
            acc_ref[...] = part

        @pl.when(jnp.logical_and(k > 0, k < nk - 1))
        def _():
            acc_ref[...] += part

        @pl.when(k == nk - 1)
        def _():
            finish(acc_ref[...] + part, o_ref)

    if mode == "nn":
        in_specs = [pl.BlockSpec((tm, tk), lambda i, j, k: (i, k)), pl.BlockSpec((tk, tn), lambda i, j, k: (k, j))]
    elif mode == "nt":
        in_specs = [pl.BlockSpec((tm, tk), lambda i, j, k: (i, k)), pl.BlockSpec((tn, tk), lambda i, j, k: (j, k))]
    else:
        in_specs = [pl.BlockSpec((tk, tm), lambda i, j, k: (k, i)), pl.BlockSpec((tk, tn), lambda i, j, k: (k, j))]
    args = [a, b]
    if has_res:
        in_specs.append(pl.BlockSpec((tm, tn), lambda i, j, k: (i, j)))
        args.append(res)
    if has_bias:
        in_specs.append(pl.BlockSpec((1, tn), lambda i, j, k: (0, j)))
        args.append(bias)
    if after is not None:
        in_specs.append(pl.BlockSpec((8, 128), lambda i, j, k: (0, 0)))
        args.append(after)
    return pl.pallas_call(
        body, name=name,
        out_shape=jax.ShapeDtypeStruct((M, N), out_dtype),
        grid=(M // tm, N // tn, nk),
        in_specs=in_specs,
        out_specs=pl.BlockSpec((tm, tn), lambda i, j, k: (i, j)),
        scratch_shapes=[pltpu.VMEM((tm, tn), F32)] if nk > 1 else [],
        compiler_params=_params(("parallel", "parallel", "arbitrary")),
    )(*args)


def _ln_fwd(z, g, b, name, tq=256, after=None):
    T, D = z.shape

    def body(z_ref, g_ref, b_ref, *rest):
        y_ref, yb_ref = rest[-2:]
        zv = z_ref[...]
        mu = jnp.mean(zv, axis=1, keepdims=True)
        zc = zv - mu
        var = jnp.mean(zc * zc, axis=1, keepdims=True)
        y = zc * lax.rsqrt(var + LN_EPS) * g_ref[...] + b_ref[...]
        y_ref[...] = y
        yb_ref[...] = y.astype(BF16)

    row = pl.BlockSpec((tq, D), lambda i: (i, 0))
    vec = pl.BlockSpec((1, D), lambda i: (0, 0))
    return pl.pallas_call(
        body, name=name,
        out_shape=(jax.ShapeDtypeStruct((T, D), F32), jax.ShapeDtypeStruct((T, D), BF16)),
        grid=(T // tq,),
        in_specs=[row, vec, vec] + ([pl.BlockSpec((8, 128), lambda i: (0, 0))] if after is not None else []),
        out_specs=(row, row),
        compiler_params=_params(("parallel",)),
    )(z, g, b, *([after] if after is not None else []))


def _ln_bwd(dy, z, g, name, tq=256):
    T, D = z.shape

    def body(dy_ref, z_ref, g_ref, dz_ref, dzb_ref, dg_ref, db_ref, ds_ref):
        @pl.when(pl.program_id(0) == 0)
        def _():
            dg_ref[...] = jnp.zeros_like(dg_ref)
            db_ref[...] = jnp.zeros_like(db_ref)
            ds_ref[...] = jnp.zeros_like(ds_ref)

        zv, dyv = z_ref[...], dy_ref[...]
        mu = jnp.mean(zv, axis=1, keepdims=True)
        zc = zv - mu
        var = jnp.mean(zc * zc, axis=1, keepdims=True)
        rstd = lax.rsqrt(var + LN_EPS)
        xh = zc * rstd
        u = dyv * g_ref[...]
        dz = rstd * (u - jnp.mean(u, axis=1, keepdims=True) - xh * jnp.mean(u * xh, axis=1, keepdims=True))
        dz_ref[...] = dz
        dzb_ref[...] = dz.astype(BF16)
        dg_ref[...] += jnp.sum(dyv * xh, axis=0, keepdims=True)
        db_ref[...] += jnp.sum(dyv, axis=0, keepdims=True)
        ds_ref[...] += jnp.sum(dz, axis=0, keepdims=True)

    row = pl.BlockSpec((tq, D), lambda i: (i, 0))
    vec = pl.BlockSpec((1, D), lambda i: (0, 0))
    vshape = jax.ShapeDtypeStruct((1, D), F32)
    return pl.pallas_call(
        body, name=name,
        out_shape=(jax.ShapeDtypeStruct((T, D), F32), jax.ShapeDtypeStruct((T, D), BF16), vshape, vshape, vshape),
        grid=(T // tq,), in_specs=[row, row, vec], out_specs=(row, row, vec, vec, vec),
        compiler_params=_params(("arbitrary",)),
    )(dy, z, g)


def _loss_head(y, target, name, tq=256):
    T, D = y.shape

    def body(y_ref, t_ref, s_ref, dy_ref):
        @pl.when(pl.program_id(0) == 0)
        def _():
            s_ref[...] = jnp.zeros_like(s_ref)

        err = y_ref[...] - t_ref[...]
        s_ref[...] += jnp.sum(err * err)
        dy_ref[...] = err * (1.0 / D)

    row = pl.BlockSpec((tq, D), lambda i: (i, 0))
    acc = pl.BlockSpec((8, 128), lambda i: (0, 0))
    return pl.pallas_call(
        body, name=name,
        out_shape=(jax.ShapeDtypeStruct((8, 128), F32), jax.ShapeDtypeStruct((T, D), F32)),
        grid=(T // tq,), in_specs=[row, row], out_specs=(acc, row),
        compiler_params=_params(("arbitrary",)),
    )(y, target)


def _pblock(tq, width, offset):
    assert offset % width == 0
    blk = offset // width
    return pl.BlockSpec((tq, width), lambda i: (i, blk))


def _mix_fwd(proj, q_g, kv_g, w_pool, pool_scale, conv_w, name, tq=256):
    T = proj.shape[0]

    def body(ql_ref, kvl_ref, pin_ref, gp_ref, ch_ref, cb_ref, cc_ref, gc_ref, qg_ref, kvg_ref, wp_ref, ps_ref,
             cw_ref, qn_ref, kvn_ref, pooled_ref, cv_ref, ypc_ref, extp, extu):
        i = pl.program_id(0)
        for x_ref, g_ref, o_ref in ((ql_ref, qg_ref, qn_ref), (kvl_ref, kvg_ref, kvn_ref)):
            x = x_ref[...]
            r = lax.rsqrt(jnp.mean(x * x, axis=1, keepdims=True) + RMS_EPS)
            o_ref[...] = (x * r * g_ref[...]).astype(BF16)

        @pl.when(i == 0)
        def _():
            extp[0:HALO, :] = jnp.zeros((HALO, D_POOL), F32)
            extu[0:HALO, :] = jnp.zeros((HALO, D_CONV), F32)

        @pl.when(i > 0)
        def _():
            extp[0:HALO, :] = extp[tq:tq + HALO, :]
            extu[0:HALO, :] = extu[tq:tq + HALO, :]

        pin = pin_ref[...]
        extp[HALO:, :] = pin
        u = cc_ref[...] * ch_ref[...]
        extu[HALO:, :] = u
        t1 = (i * tq + lax.broadcasted_iota(jnp.int32, (tq, 1), 0) + 1).astype(F32)
        for g, w in enumerate(POOL_WINDOWS):
            cols = slice(g * POOL_GROUP, (g + 1) * POOL_GROUP)
            s = extp[:, cols]
            k = 1
            while k < w:
                s = s + pltpu.roll(s, k, 0)
                k *= 2
            mean = s[HALO:, :] / jnp.minimum(t1, float(w))
            pooled = (mean - pin[:, cols]).astype(BF16)
            pooled_ref[:, cols] = pooled
            r = jnp.dot(pooled, wp_ref[g], preferred_element_type=F32)
            gp = gp_ref[:, cols]
            ypc_ref[:, cols] = (r * ps_ref[:, cols] * (gp * _sigmoid(gp))).astype(BF16)
        eu = extu[...]
        u1 = pltpu.roll(eu, 1, 0)[HALO:, :]
        u2 = pltpu.roll(eu, 2, 0)[HALO:, :]
        cv = cw_ref[0:1, :] * u2 + cw_ref[1:2, :] * u1 + cw_ref[2:3, :] * u
        cv_ref[...] = cv
        gc = gc_ref[...]
        ypc_ref[:, D_POOL:] = (cb_ref[...] * cv * (gc * _sigmoid(gc))).astype(BF16)

    full = lambda shape: pl.BlockSpec(shape, lambda i: (0,) * len(shape))
    row = lambda w: pl.BlockSpec((tq, w), lambda i: (i, 0))
    return pl.pallas_call(
        body, name=name,
        out_shape=(jax.ShapeDtypeStruct((T, Q_LORA), BF16), jax.ShapeDtypeStruct((T, KV_LORA), BF16),
                   jax.ShapeDtypeStruct((T, D_POOL), BF16), jax.ShapeDtypeStruct((T, D_CONV), F32),
                   jax.ShapeDtypeStruct((T, D_MIX), BF16)),
        grid=(T // tq,),
        in_specs=[_pblock(tq, Q_LORA, O_QLAT), _pblock(tq, KV_LORA, O_KVLAT), _pblock(tq, 512, O_PIN),
                  _pblock(tq, 512, O_GPOOL), _pblock(tq, 512, O_CH), _pblock(tq, 512, O_CB), _pblock(tq, 512, O_CC),
                  _pblock(tq, 512, O_GCONV), full((1, Q_LORA)), full((1, KV_LORA)), full((4, 128, 128)),
                  full((1, D_POOL)), full((8, D_CONV))],
        out_specs=(row(Q_LORA), row(KV_LORA), row(D_POOL), row(D_CONV),
                   pl.BlockSpec((tq, D_POOL + D_CONV), lambda i: (i, D_MLA // (D_POOL + D_CONV)))),
        scratch_shapes=[pltpu.VMEM((tq + HALO, D_POOL), F32), pltpu.VMEM((tq + HALO, D_CONV), F32)],
        compiler_params=_params(("arbitrary",)),
    )(proj, proj, proj, proj, proj, proj, proj, proj, q_g, kv_g, w_pool, pool_scale, conv_w)


def _mix_bwd(dmix, proj, o, pooled, cv, w_pool, pool_scale, conv_w, name, tq=ATT_CH):
    T = proj.shape[0]
    nt = T // tq
    n_ext = tq + HALO

    def body(dym_ref, dyp_ref, dyc_ref, gm_ref, gp_ref, ch_ref, cb_ref, cc_ref, gc_ref, o_ref, pooled_ref, cv_ref,
             wp_ref, ps_ref, cw_ref, do_ref, delta_ref, dg_ref, dwp_ref, dps_ref, dcw_ref, exte, extd):
        i = pl.program_id(0)
        tile = nt - 1 - i

        @pl.when(i == 0)
        def _():
            dwp_ref[...] = jnp.zeros_like(dwp_ref)
            dps_ref[...] = jnp.zeros_like(dps_ref)
            dcw_ref[...] = jnp.zeros_like(dcw_ref)
            exte[tq:, :] = jnp.zeros((HALO, D_POOL), F32)
            extd[tq:, :] = jnp.zeros((HALO, D_CONV), F32)

        @pl.when(i > 0)
        def _():
            exte[tq:, :] = exte[0:HALO, :]
            extd[tq:, :] = extd[0:HALO, :]

        gm = gm_ref[...]
        sig = _sigmoid(gm)
        dym = dym_ref[...]
        ov = o_ref[...]
        do = dym * (gm * sig)
        do_ref[...] = do.astype(BF16)
        prod = do * ov
        ones = jnp.ones((8, V_DIM), F32)
        for h in range(N_HEADS):
            rows = lax.dot_general(ones, prod[:, h * V_DIM:(h + 1) * V_DIM], _DIMS["nt"],
                                   precision=lax.Precision.HIGHEST, preferred_element_type=F32)
            delta_ref[h, 0] = rows[0:1, :]
        dg_ref[:, O_GMLA:O_PIN] = (dym * ov * (sig * (1.0 + gm * (1.0 - sig)))).astype(BF16)

        t1 = (tile * tq + lax.broadcasted_iota(jnp.int32, (tq, 1), 0) + 1).astype(F32)
        for g, w in enumerate(POOL_WINDOWS):
            cols = slice(g * POOL_GROUP, (g + 1) * POOL_GROUP)
            pg = pooled_ref[:, cols]
            r = jnp.dot(pg, wp_ref[g], preferred_element_type=F32)
            gp = gp_ref[:, cols]
            sg = _sigmoid(gp)
            sl = gp * sg
            dyg = dyp_ref[:, cols]
            ps = ps_ref[:, cols]
            dg_ref[:, O_GPOOL + g * POOL_GROUP:O_GPOOL + (g + 1) * POOL_GROUP] = (
                dyg * (r * ps) * (sg * (1.0 + gp * (1.0 - sg)))).astype(BF16)
            dps_ref[:, cols] += jnp.sum(dyg * r * sl, axis=0, keepdims=True)
            dr = (dyg * ps * sl).astype(BF16)
            dwp_ref[g] += lax.dot_general(pg, dr, _DIMS["tn"], preferred_element_type=F32)
            dpooled = lax.dot_general(dr, wp_ref[g], _DIMS["nt"], preferred_element_type=F32)
            exte[0:tq, cols] = dpooled / jnp.minimum(t1, float(w))
            s = exte[:, cols]
            k = 1
            while k < w:
                s = s + pltpu.roll(s, n_ext - k, 0)
                k *= 2
            dg_ref[:, O_PIN + g * POOL_GROUP:O_PIN + (g + 1) * POOL_GROUP] = (s[0:tq, :] - dpooled).astype(BF16)

        gc = gc_ref[...]
        sg = _sigmoid(gc)
        sl = gc * sg
        dyc = dyc_ref[...]
        cb, cc, ch, cvv = cb_ref[...], cc_ref[...], ch_ref[...], cv_ref[...]
        dcv = dyc * cb * sl
        dg_ref[:, O_GCONV:O_GCONV + D_CONV] = (dyc * (cb * cvv) * (sg * (1.0 + gc * (1.0 - sg)))).astype(BF16)
        dg_ref[:, O_CB:O_CB + D_CONV] = (dyc * cvv * sl).astype(BF16)
        extd[0:tq, :] = dcv
        ed = extd[...]
        d1 = pltpu.roll(ed, n_ext - 1, 0)[0:tq, :]
        d2 = pltpu.roll(ed, n_ext - 2, 0)[0:tq, :]
        du = cw_ref[2:3, :] * dcv + cw_ref[1:2, :] * d1 + cw_ref[0:1, :] * d2
        u = cc * ch
        dcw_ref[0:1, :] += jnp.sum(u * d2, axis=0, keepdims=True)
        dcw_ref[1:2, :] += jnp.sum(u * d1, axis=0, keepdims=True)
        dcw_ref[2:3, :] += jnp.sum(u * dcv, axis=0, keepdims=True)
        dg_ref[:, O_CH:O_CH + D_CONV] = (du * cc).astype(BF16)
        dg_ref[:, O_CC:O_CC + D_CONV] = (du * ch).astype(BF16)

    def rblock(width, offset):
        assert offset % width == 0
        blk = offset // width
        return pl.BlockSpec((tq, width), lambda i: (nt - 1 - i, blk))

    full = lambda shape: pl.BlockSpec(shape, lambda i: (0,) * len(shape))
    return pl.pallas_call(
        body, name=name,
        out_shape=(jax.ShapeDtypeStruct((T, D_MLA), BF16), jax.ShapeDtypeStruct((N_HEADS, nt, 1, tq), F32),
                   jax.ShapeDtypeStruct((T, NPP), BF16),
                   jax.ShapeDtypeStruct((4, 128, 128), F32), jax.ShapeDtypeStruct((1, D_POOL), F32),
                   jax.ShapeDtypeStruct((8, D_CONV), F32)),
        grid=(nt,),
        in_specs=[rblock(1024, 0), rblock(512, 1024), rblock(512, 1536),
                  rblock(1024, O_GMLA), rblock(512, O_GPOOL), rblock(512, O_CH), rblock(512, O_CB),
                  rblock(512, O_CC), rblock(512, O_GCONV), rblock(1024, 0), rblock(512, 0), rblock(512, 0),
                  full((4, 128, 128)), full((1, D_POOL)), full((8, D_CONV))],
        out_specs=(rblock(1024, 0), pl.BlockSpec((N_HEADS, 1, 1, tq), lambda i: (0, nt - 1 - i, 0, 0)),
                   rblock(N_GATED, 0), full((4, 128, 128)), full((1, D_POOL)), full((8, D_CONV))),
        scratch_shapes=[pltpu.VMEM((n_ext, D_POOL), F32), pltpu.VMEM((n_ext, D_CONV), F32)],
        compiler_params=_params(("arbitrary",)),
    )(dmix, dmix, dmix, proj, proj, proj, proj, proj, proj, o, pooled, cv, w_pool, pool_scale, conv_w)


def _rms_bwd(proj, dqn, dkvn, dkrope, dproj, q_g, kv_g, name, tq=256):
    T = proj.shape[0]
    n_lat = NPP - N_GATED

    def body(ql_ref, kvl_ref, dqn_ref, dkvn_ref, dkr_ref, _, qg_ref, kvg_ref, dlat_ref, dqg_ref, dkvg_ref):
        @pl.when(pl.program_id(0) == 0)
        def _():
            dqg_ref[...] = jnp.zeros_like(dqg_ref)
            dkvg_ref[...] = jnp.zeros_like(dkvg_ref)

        for x_ref, dy_ref, g_ref, c0, dg_ref in ((ql_ref, dqn_ref, qg_ref, 0, dqg_ref),
                                                 (kvl_ref, dkvn_ref, kvg_ref, Q_LORA, dkvg_ref)):
            x, dy = x_ref[...], dy_ref[...]
            r = lax.rsqrt(jnp.mean(x * x, axis=1, keepdims=True) + RMS_EPS)
            xr = x * r
            u = dy * g_ref[...]
            dlat_ref[:, c0:c0 + x.shape[1]] = (r * (u - xr * jnp.mean(u * xr, axis=1, keepdims=True))).astype(BF16)
            dg_ref[...] += jnp.sum(dy * xr, axis=0, keepdims=True)
        dlat_ref[:, Q_LORA + KV_LORA:] = dkr_ref[...]

    row = lambda w: pl.BlockSpec((tq, w), lambda i: (i, 0))
    vec = lambda w: pl.BlockSpec((1, w), lambda i: (0, 0))
    assert N_GATED % n_lat == 0
    return pl.pallas_call(
        body, name=name,
        out_shape=(jax.ShapeDtypeStruct((T, NPP), BF16),
                   jax.ShapeDtypeStruct((1, Q_LORA), F32), jax.ShapeDtypeStruct((1, KV_LORA), F32)),
        grid=(T // tq,),
        in_specs=[_pblock(tq, Q_LORA, O_QLAT), _pblock(tq, KV_LORA, O_KVLAT), row(Q_LORA), row(KV_LORA),
                  row(n_lat - Q_LORA - KV_LORA), pl.BlockSpec(memory_space=pl.ANY), vec(Q_LORA), vec(KV_LORA)],
        out_specs=(pl.BlockSpec((tq, n_lat), lambda i: (i, N_GATED // n_lat)), vec(Q_LORA), vec(KV_LORA)),
        input_output_aliases={5: 0},
        compiler_params=_params(("arbitrary",)),
    )(proj, proj, dqn, dkvn, dkrope, dproj, q_g, kv_g)


def _swap_halves(x, lo):
    return jnp.where(lo, pltpu.roll(x, 96, 1), pltpu.roll(x, 32, 1))


def _rope_fwd(q, kv, proj, cos_t, sin_t, name, tq=256):
    T = q.shape[0]

    def body(qn_ref, qr_ref, kv_ref, kr_ref, c_ref, s_ref, qc_ref, kc_ref):
        C, S = c_ref[...], s_ref[...]
        lane = lax.broadcasted_iota(jnp.int32, (tq, 128), 1)
        lo = (lane % ROPE) < (ROPE // 2)
        first = lane < ROPE

        def rope(x):
            return x * C + _swap_halves(x, lo) * S

        kr = jnp.where(first, rope(kr_ref[...]), 0.0).astype(BF16)
        for j in range(N_HEADS // 2):
            r = rope(qr_ref[:, j * 128:(j + 1) * 128])
            pair = (jnp.where(first, r, 0.0), jnp.where(first, pltpu.roll(r, 64, 1), 0.0))
            for hh in range(2):
                h = 2 * j + hh
                qc_ref[h, :, 0:NOPE] = qn_ref[:, h * NOPE:(h + 1) * NOPE].astype(BF16)
                qc_ref[h, :, NOPE:QC] = pair[hh].astype(BF16)
        for h in range(N_HEADS):
            kc_ref[h, :, 0:NOPE] = kv_ref[:, h * 256:h * 256 + NOPE]
            kc_ref[h, :, NOPE:QC] = kr

    out = jax.ShapeDtypeStruct((N_HEADS, T, QC), BF16)
    hblock = pl.BlockSpec((N_HEADS, tq, QC), lambda i: (0, i, 0))
    return pl.pallas_call(
        body, name=name, out_shape=(out, out), grid=(T // tq,),
        in_specs=[pl.BlockSpec((tq, 1024), lambda i: (i, 0)), pl.BlockSpec((tq, 512), lambda i: (i, 2)),
                  pl.BlockSpec((tq, 2048), lambda i: (i, 0)), _pblock(tq, 128, O_KROPE),
                  pl.BlockSpec((tq, 128), lambda i: (i, 0)), pl.BlockSpec((tq, 128), lambda i: (i, 0))],
        out_specs=(hblock, hblock),
        compiler_params=_params(("parallel",)),
    )(q, q, kv, proj, cos_t, sin_t)


def _rope_bwd(dqc, dkr, cos_t, sin_t, name, tq=256):
    T = dqc.shape[1]

    def body(dqc_ref, dkr_ref, c_ref, s_ref, dq_ref, dk_ref):
        C, S = c_ref[...], s_ref[...]
        lane = lax.broadcasted_iota(jnp.int32, (tq, 128), 1)
        lo = (lane % ROPE) < (ROPE // 2)
        first = lane < ROPE

        def unrope(dy):
            return dy * C - _swap_halves(dy, lo) * S

        acc = dkr_ref[0]
        for h in range(1, N_HEADS):
            acc = acc + dkr_ref[h]
        dk_ref[:, 0:128] = jnp.where(first, unrope(acc), 0.0).astype(BF16)
        dk_ref[:, 128:256] = jnp.zeros((tq, 128), BF16)
        for j in range(N_HEADS // 2):
            d0 = dqc_ref[2 * j, :, NOPE:QC]
            d1 = dqc_ref[2 * j + 1, :, NOPE:QC]
            comb = jnp.where(first, d0, pltpu.roll(d1, 64, 1))
            dq_ref[:, 1024 + j * 128:1024 + (j + 1) * 128] = unrope(comb).astype(BF16)
        for h in range(N_HEADS):
            dq_ref[:, h * NOPE:(h + 1) * NOPE] = dqc_ref[h, :, 0:NOPE].astype(BF16)

    tab = pl.BlockSpec((tq, 128), lambda i: (i, 0))
    return pl.pallas_call(
        body, name=name,
        out_shape=(jax.ShapeDtypeStruct((T, 1536), BF16), jax.ShapeDtypeStruct((T, 256), BF16)),
        grid=(T // tq,),
        in_specs=[pl.BlockSpec((N_HEADS, tq, QC), lambda i: (0, i, 0)),
                  pl.BlockSpec((N_HEADS, tq, 128), lambda i: (0, i, 0)), tab, tab],
        out_specs=(pl.BlockSpec((tq, 1536), lambda i: (i, 0)), pl.BlockSpec((tq, 256), lambda i: (i, 0))),
        compiler_params=_params(("parallel",)),
    )(dqc, dkr, cos_t, sin_t)


def _flash_fwd(qc, kc, kv, proj, mix, name):
    H, T, _ = qc.shape
    tt = ATT_TILE
    nt = T // tt
    sp = tt // ATT_CH

    def body(q_ref, k_ref, v_ref, g_ref, _, o_ref, y_ref, lse_ref, vt_sc, s_sc, acc_sc, m_sc, l_sc):
        i = pl.program_id(1)

        @pl.when(i == 0)
        def _():
            for c in range(nt):
                vt_sc[c] = v_ref[c * tt:(c + 1) * tt, :].astype(F32).T.astype(BF16)

        q = q_ref[0]

        def issue(c, slot):
            s_sc[slot] = lax.dot_general(k_ref[0, pl.ds(pl.multiple_of(c * tt, tt), tt), :], q, _DIMS["nt"],
                                         preferred_element_type=F32)

        def softmax_pv(c, slot, masked):
            s = s_sc[slot]
            if masked:
                krow = c * tt + lax.broadcasted_iota(jnp.int32, s.shape, 0)
                qcol = i * tt + lax.broadcasted_iota(jnp.int32, s.shape, 1)
                s = jnp.where(krow <= qcol, s, -jnp.inf)
            m = m_sc[...]
            m_new = jnp.maximum(m, jnp.max(s, axis=0, keepdims=True))
            p = jnp.exp2((s - m_new) * EXP2_SCALE)
            a = jnp.exp2((m - m_new) * EXP2_SCALE)
            l_sc[...] = a * l_sc[...] + jnp.sum(p, axis=0, keepdims=True)
            acc_sc[...] = a * acc_sc[...] + jnp.dot(vt_sc[c], p.astype(BF16), preferred_element_type=F32)
            m_sc[...] = m_new

        m_sc[...] = jnp.full_like(m_sc, -jnp.inf)
        l_sc[...] = jnp.zeros_like(l_sc)
        acc_sc[...] = jnp.zeros_like(acc_sc)
        issue(0, 0)

        def pair(t, carry):
            issue(2 * t + 1, 1)
            softmax_pv(2 * t, 0, False)
            issue(2 * t + 2, 0)
            softmax_pv(2 * t + 1, 1, False)
            return carry

        lax.fori_loop(0, i // 2, pair, 0)

        @pl.when(i % 2 == 1)
        def _():
            issue(i, 1)
            softmax_pv(i - 1, 0, False)
            softmax_pv(i, 1, True)

        @pl.when(i % 2 == 0)
        def _():
            softmax_pv(i, 0, True)

        l = l_sc[...]
        o = (acc_sc[...] / l).T
        o_ref[...] = o
        lse = m_sc[...] * ATTN_SCALE + jnp.log(l)
        for r in range(sp):
            lse_ref[0, r] = lse[:, r * ATT_CH:(r + 1) * ATT_CH]
        g = g_ref[...]
        y_ref[...] = (o * (g * _sigmoid(g))).astype(BF16)

    return pl.pallas_call(
        body, name=name,
        out_shape=(jax.ShapeDtypeStruct((T, D_MLA), F32), jax.ShapeDtypeStruct((T, D_MIX), BF16),
                   jax.ShapeDtypeStruct((H, T // ATT_CH, 1, ATT_CH), F32)),
        grid=(H, nt),
        in_specs=[pl.BlockSpec((1, tt, QC), lambda h, i: (h, i, 0)),
                  pl.BlockSpec((1, T, QC), lambda h, i: (h, 0, 0)),
                  pl.BlockSpec((T, V_DIM), lambda h, i: (0, 2 * h + 1)),
                  pl.BlockSpec((tt, V_DIM), lambda h, i: (i, h)),
                  pl.BlockSpec(memory_space=pl.ANY)],
        input_output_aliases={4: 1},
        out_specs=(pl.BlockSpec((tt, V_DIM), lambda h, i: (i, h)),
                   pl.BlockSpec((tt, V_DIM), lambda h, i: (i, h)),
                   pl.BlockSpec((1, sp, 1, ATT_CH), lambda h, i: (h, i, 0, 0))),
        scratch_shapes=[pltpu.VMEM((nt, V_DIM, tt), BF16), pltpu.VMEM((2, tt, tt), F32),
                        pltpu.VMEM((V_DIM, tt), F32), pltpu.VMEM((1, tt), F32), pltpu.VMEM((1, tt), F32)],
        compiler_params=_params(("parallel", "arbitrary")),
    )(qc, kc, kv, proj, mix)


def _flash_bwd(qc, kc, kv, do, lse, delta, name):
    H, T, _ = qc.shape
    tt = ATT_TILE
    nt = T // tt
    sp = tt // ATT_CH

    def body(q_ref, k_ref, v_ref, do_ref, lse_ref, dl_ref, dq_ref, dkv_ref, dkr_ref, dqt_sc, dk_sc, dv_sc, s_sc,
             dp_sc):
        j = pl.program_id(1)

        @pl.when(j == 0)
        def _():
            dqt_sc[...] = jnp.zeros_like(dqt_sc)

        k = k_ref[0]
        v = v_ref[...]
        kt = k.astype(F32).T.astype(BF16)

        def operands(c):
            q0 = pl.multiple_of(c * tt, tt)
            return q_ref[0, pl.ds(q0, tt), :], do_ref[pl.ds(q0, tt), :]

        def stat_row(ref, c):
            return jnp.concatenate([ref[0, sp * c + r] for r in range(sp)], axis=1)

        def early(c, slot):
            q, dov = operands(c)
            s_sc[slot] = lax.dot_general(k, q, _DIMS["nt"], preferred_element_type=F32)
            dp_sc[slot] = lax.dot_general(v, dov, _DIMS["nt"], preferred_element_type=F32)

        def late(c, slot, masked):
            q, dov = operands(c)
            s, dp = s_sc[slot], dp_sc[slot]
            if masked:
                krow = j * tt + lax.broadcasted_iota(jnp.int32, s.shape, 0)
                qcol = c * tt + lax.broadcasted_iota(jnp.int32, s.shape, 1)
                s = jnp.where(krow <= qcol, s, -jnp.inf)
            p = jnp.exp2(s * EXP2_SCALE - stat_row(lse_ref, c) * LOG2E)
            ds = (p * (dp - stat_row(dl_ref, c)) * ATTN_SCALE).astype(BF16)
            dv = jnp.dot(p.astype(BF16), dov, preferred_element_type=F32)
            dk = jnp.dot(ds, q, preferred_element_type=F32)
            if masked:
                dv_sc[...] = dv
                dk_sc[...] = dk
            else:
                dv_sc[...] += dv
                dk_sc[...] += dk
            dqt_sc[c] += jnp.dot(kt, ds, preferred_element_type=F32)

        early(j, 0)

        @pl.when(j < nt - 1)
        def _():
            early(j + 1, 1)

        late(j, 0, True)
        n_rest = nt - 1 - j

        def pair(u, carry):
            a = j + 1 + 2 * u
            early(a + 1, 0)
            late(a, 1, False)

            @pl.when(a + 2 <= nt - 1)
            def _():
                early(a + 2, 1)

            late(a + 1, 0, False)
            return carry

        lax.fori_loop(0, n_rest // 2, pair, 0)

        @pl.when(n_rest % 2 == 1)
        def _():
            late(nt - 1, 1, False)

        dk = dk_sc[...]
        dkv_ref[:, 0:NOPE] = dk[:, 0:NOPE].astype(BF16)
        dkv_ref[:, NOPE:] = dv_sc[...].astype(BF16)
        dkr_ref[0] = dk[:, NOPE:]

        @pl.when(j == nt - 1)
        def _():
            for c in range(nt):
                dq_ref[0, c * tt:(c + 1) * tt, :] = dqt_sc[c].T

    head = lambda h, j: (h, 0, 0)
    stat = pl.BlockSpec((1, T // ATT_CH, 1, ATT_CH), lambda h, j: (h, 0, 0, 0))
    return pl.pallas_call(
        body, name=name,
        out_shape=(jax.ShapeDtypeStruct((H, T, QC), F32), jax.ShapeDtypeStruct((T, 2 * D_MLA), BF16),
                   jax.ShapeDtypeStruct((H, T, 128), F32)),
        grid=(H, nt),
        in_specs=[pl.BlockSpec((1, T, QC), head),
                  pl.BlockSpec((1, tt, QC), lambda h, j: (h, j, 0)),
                  pl.BlockSpec((tt, V_DIM), lambda h, j: (j, 2 * h + 1)),
                  pl.BlockSpec((T, V_DIM), lambda h, j: (0, h)),
                  stat, stat],
        out_specs=(pl.BlockSpec((1, T, QC), head),
                   pl.BlockSpec((tt, 256), lambda h, j: (j, h)),
                   pl.BlockSpec((1, tt, 128), lambda h, j: (h, j, 0))),
        scratch_shapes=[pltpu.VMEM((nt, QC, tt), F32), pltpu.VMEM((tt, QC), F32), pltpu.VMEM((tt, V_DIM), F32),
                        pltpu.VMEM((2, tt, tt), F32), pltpu.VMEM((2, tt, tt), F32)],
        compiler_params=_params(("parallel", "arbitrary")),
    )(qc, kc, kv, do, lse, delta)


def _adamw(lands, w, m, v, name, rows, cols=None):
    L, R, C = w.shape
    cols = C if cols is None else cols
    assert R % rows == 0 and C % cols == 0 and len(lands) == L
    nc = C // cols
    nb = (R // rows) * nc
    c1 = 1.0 - ADAM_B1 ** ADAM_STEP
    c2 = 1.0 - ADAM_B2 ** ADAM_STEP

    def body(*refs):
        land_refs = refs[:L]
        w_ref, m_ref, v_ref, g_ref, d_ref, nm_ref, nv_ref, g_sc = refs[L:]
        for ll in range(L):
            @pl.when(pl.program_id(0) == ll)
            def _(land_ref=land_refs[ll]):
                g = land_ref[0].astype(F32)
                for s in range(1, N_DEV):
                    g = g + land_ref[s].astype(F32)
                g_sc[...] = g

        g = g_sc[...]
        nm = ADAM_B1 * m_ref[0] + (1.0 - ADAM_B1) * g
        nv = ADAM_B2 * v_ref[0] + (1.0 - ADAM_B2) * (g * g)
        g_ref[0] = g
        nm_ref[0] = nm
        nv_ref[0] = nv
        d_ref[0] = -ADAM_LR * ((nm / c1) / (jnp.sqrt(nv / c2) + ADAM_EPS) + ADAM_WD * w_ref[0])

    def land_spec(ll):
        def index(l, i):
            i = jnp.where(l < ll, 0, jnp.where(l > ll, nb - 1, i))
            return (0, i // nc, i % nc)
        return pl.BlockSpec((N_DEV, rows, cols), index)

    blk = pl.BlockSpec((1, rows, cols), lambda l, i: (l, i // nc, i % nc))
    out = jax.ShapeDtypeStruct((L, R, C), F32)
    return pl.pallas_call(
        body, name=name, out_shape=(out, out, out, out), grid=(L, nb),
        in_specs=[land_spec(ll) for ll in range(L)] + [blk, blk, blk],
        out_specs=(blk, blk, blk, blk),
        scratch_shapes=[pltpu.VMEM((rows, cols), F32)],
        compiler_params=_params(("arbitrary", "arbitrary")),
    )(*lands, w, m, v)


def _mesh_pos():
    return lax.axis_index("x"), lax.axis_index("y"), lax.axis_index("c")


def _all_gather(arrays, name):
    n = len(arrays)

    def body(*refs):
        ins, outs = refs[:n], refs[n:2 * n]
        send_sems, recv_sems, local_sems = refs[2 * n:]
        x, y, c = _mesh_pos()
        me, sibling = (x, y, c), (x, y, 1 - c)
        chips = [(1 - x, y), (x, 1 - y), (1 - x, 1 - y)]

        def slot(a, pos):
            px, py, pc = pos
            return outs[a].at[4 * px + 2 * py + pc]

        def copy(a, k, block, to, src=None):
            return pltpu.make_async_remote_copy(
                src_ref=slot(a, block) if src is None else src, dst_ref=slot(a, block),
                send_sem=send_sems.at[a * 7 + k], recv_sem=recv_sems.at[a * 7 + k],
                device_id=to, device_id_type=MESH_ID)

        mine, first, passed = [], [], []
        for a in range(n):
            cp = pltpu.make_async_copy(ins[a], slot(a, me), local_sems.at[a])
            cp.start()
            mine.append(cp)
            cps = [copy(a, 0, me, sibling, src=ins[a])]
            cps += [copy(a, 1 + j, me, (*chip, c), src=ins[a]) for j, chip in enumerate(chips)]
            for cp in cps:
                cp.start()
            first += cps
        for j, chip in enumerate(chips):
            for a in range(n):
                copy(a, 1 + j, (*chip, c), me).wait_recv()
                cp = copy(a, 4 + j, (*chip, c), sibling)
                cp.start()
                passed.append(cp)
        for a in range(n):
            copy(a, 0, sibling, me).wait_recv()
            for j, chip in enumerate(chips):
                copy(a, 4 + j, (*chip, 1 - c), me).wait_recv()
        for cp in first + passed:
            cp.wait_send()
        for cp in mine:
            cp.wait()

    hbm = pl.BlockSpec(memory_space=pltpu.HBM)
    return pl.pallas_call(
        body, name=name,
        out_shape=tuple(jax.ShapeDtypeStruct((N_DEV,) + a.shape, a.dtype) for a in arrays),
        in_specs=[hbm] * n, out_specs=tuple([hbm] * n),
        scratch_shapes=[pltpu.SemaphoreType.DMA((7 * n,)), pltpu.SemaphoreType.DMA((7 * n,)),
                        pltpu.SemaphoreType.DMA((n,))],
    )(*arrays)


def _exchange(arrays, name):
    n = len(arrays)

    def body(*refs):
        ins, outs = refs[:n], refs[n:2 * n]
        send_sems, recv_sems, local_sems = refs[2 * n:]
        x, y, c = _mesh_pos()
        my_idx = 4 * x + 2 * y + c
        copies, local = [], []
        for a in range(n):
            cp = pltpu.make_async_copy(ins[a].at[my_idx], outs[a].at[my_idx], local_sems.at[a])
            cp.start()
            local.append(cp)
            for k in range(1, N_DEV):
                px = 1 - x if k & 4 else x
                py = 1 - y if k & 2 else y
                pc = 1 - c if k & 1 else c
                cp = pltpu.make_async_remote_copy(
                    src_ref=ins[a].at[4 * px + 2 * py + pc], dst_ref=outs[a].at[my_idx],
                    send_sem=send_sems.at[a * 7 + k - 1], recv_sem=recv_sems.at[a * 7 + k - 1],
                    device_id=(px, py, pc), device_id_type=MESH_ID)
                cp.start()
                copies.append(cp)
        for cp in copies:
            cp.wait()
        for cp in local:
            cp.wait()

    hbm = pl.BlockSpec(memory_space=pltpu.HBM)
    return pl.pallas_call(
        body, name=name,
        out_shape=tuple(jax.ShapeDtypeStruct(a.shape, a.dtype) for a in arrays),
        in_specs=[hbm] * n, out_specs=tuple([hbm] * n),
        scratch_shapes=[pltpu.SemaphoreType.DMA((7 * n,)), pltpu.SemaphoreType.DMA((7 * n,)),
                        pltpu.SemaphoreType.DMA((n,))],
    )(*arrays)


_HBM = pl.BlockSpec(memory_space=pltpu.HBM)
_SEM = pl.BlockSpec(memory_space=pltpu.SEMAPHORE)
_EFFECT = pltpu.SideEffectType.DATAFLOW_SIDE_EFFECTING
N_PEERS = N_DEV - 1


def _peer(k):
    x, y, c = _mesh_pos()
    return (1 - x if k & 4 else x, 1 - y if k & 2 else y, 1 - c if k & 1 else c)


def _split_start(srcs, scatter, after, name):
    n = len(srcs)
    zones = [jax.ShapeDtypeStruct(s.shape if scatter else ((N_DEV,) + s.shape), s.dtype) for s in srcs]

    def body(*refs):
        src, zone = refs[:n], refs[n:2 * n]
        outs = refs[2 * n + 1:]
        send, recv, token = outs[:n], outs[n:2 * n], outs[4 * n]
        x, y, c = _mesh_pos()
        my_idx = 4 * x + 2 * y + c
        for a in range(n):
            pltpu.make_async_copy(src[a].at[my_idx] if scatter else src[a],
                                  zone[a].at[N_PEERS] if scatter else zone[a].at[my_idx], recv[a]).start()
            for k in range(1, N_DEV):
                px, py, pc = _peer(k)
                pltpu.make_async_remote_copy(
                    src_ref=src[a].at[4 * px + 2 * py + pc] if scatter else src[a],
                    dst_ref=zone[a].at[k - 1] if scatter else zone[a].at[my_idx],
                    send_sem=send[a], recv_sem=recv[a], device_id=(px, py, pc), device_id_type=MESH_ID).start()
        token[...] = jnp.zeros_like(token)

    hbm = lambda a: pltpu.with_memory_space_constraint(a, pltpu.HBM)
    outs = pl.pallas_call(
        body, name=name,
        out_shape=tuple([pltpu.SemaphoreType.DMA(())] * (2 * n)
                        + [pltpu.HBM(s.shape, s.dtype) for s in srcs]
                        + [pltpu.HBM(z.shape, z.dtype) for z in zones]
                        + [jax.ShapeDtypeStruct((8, 128), F32)]),
        in_specs=[_HBM] * (2 * n) + [pl.BlockSpec(memory_space=pl.ANY)],
        out_specs=tuple([_SEM] * (2 * n) + [_HBM] * (2 * n) + [pl.BlockSpec(memory_space=pltpu.VMEM)]),
        input_output_aliases={**{a: 2 * n + a for a in range(n)}, **{n + a: 3 * n + a for a in range(n)}},
        compiler_params=pltpu.CompilerParams(has_side_effects=_EFFECT),
    )(*[hbm(s) for s in srcs], *[hbm(lax.empty(z.shape, z.dtype)) for z in zones], after)
    return outs[:n], outs[n:2 * n], outs[2 * n:3 * n], outs[3 * n:4 * n], outs[4 * n]


def _split_wait(send, recv, srcs, zones, after, name):
    n = len(srcs)

    def body(*refs):
        zone = refs[n:2 * n]
        send_sems, recv_sems = refs[2 * n:3 * n], refs[3 * n:4 * n]
        x, y, c = _mesh_pos()
        for a in range(n):
            seven = zone[a].at[pl.ds(0, N_PEERS)]
            pltpu.make_async_remote_copy(src_ref=seven, dst_ref=seven, send_sem=send_sems[a], recv_sem=recv_sems[a],
                                         device_id=(x, y, 1 - c), device_id_type=MESH_ID).wait_send()
            pltpu.make_async_remote_copy(src_ref=zone[a], dst_ref=zone[a], send_sem=send_sems[a],
                                         recv_sem=recv_sems[a], device_id=(x, y, 1 - c),
                                         device_id_type=MESH_ID).wait_recv()

    outs = pl.pallas_call(
        body, name=name,
        out_shape=tuple([pltpu.HBM(s.shape, s.dtype) for s in srcs] + [pltpu.HBM(z.shape, z.dtype) for z in zones]),
        in_specs=[_HBM] * (2 * n) + [_SEM] * (2 * n) + [pl.BlockSpec(memory_space=pl.ANY)],
        out_specs=tuple([_HBM] * (2 * n)),
        input_output_aliases={a: a for a in range(2 * n)},
        compiler_params=pltpu.CompilerParams(has_side_effects=_EFFECT),
    )(*srcs, *zones, *send, *recv, after)
    return outs[:n], outs[n:]


def _cat_blocks(g, axis):
    return jnp.concatenate([g[d] for d in range(N_DEV)], axis=axis)


N_LATENT = Q_LORA + KV_LORA + ROPE
W_SHARD = D_IN_PROJ // N_DEV


def _ref_cols(lo, hi):
    out = []
    if lo < N_LATENT:
        out.append((N_GATED + lo, N_GATED + min(hi, N_LATENT)))
    if hi > N_LATENT:
        out.append((max(lo, N_LATENT) - N_LATENT, hi - N_LATENT))
    return out


def _permute_w_in_t(blocks):
    pieces = []
    for lo, hi in ((N_LATENT, D_IN_PROJ), (0, N_LATENT)):
        for d in range(N_DEV):
            a, b = max(lo, d * W_SHARD), min(hi, (d + 1) * W_SHARD)
            if a < b:
                pieces.append(blocks[d][a - d * W_SHARD:b - d * W_SHARD])
    pieces.append(jnp.zeros((NPP - D_IN_PROJ, blocks.shape[2]), blocks.dtype))
    return jnp.concatenate(pieces, axis=0)


def _split_w_in_t(w):
    slabs = []
    for d in range(N_DEV):
        parts = [w[a:b] for a, b in _ref_cols(d * W_SHARD, (d + 1) * W_SHARD)]
        slabs.append(parts[0] if len(parts) == 1 else jnp.concatenate(parts, axis=0))
    return jnp.stack(slabs)


def _permute_w_uq(w):
    w3 = w.reshape(w.shape[0], N_HEADS, NOPE + ROPE)
    return jnp.concatenate([w3[:, :, :NOPE].reshape(w.shape[0], -1), w3[:, :, NOPE:].reshape(w.shape[0], -1)], axis=1)


def _unpermute_w_uq(w):
    nope = w[:, :N_HEADS * NOPE].reshape(w.shape[0], N_HEADS, NOPE)
    rope = w[:, N_HEADS * NOPE:].reshape(w.shape[0], N_HEADS, ROPE)
    return jnp.concatenate([nope, rope], axis=2).reshape(w.shape[0], -1)


_SMALL = (("emb_ln_g", 16), ("emb_ln_b", 16), ("q_norm_g", 8), ("kv_norm_g", 8), ("w_pool", 1024),
          ("pool_scale", 8), ("b_out", 32), ("ln_g", 32), ("ln_b", 32))
SMALL_ROWS = sum(r for _, r in _SMALL)


def _pack_small(d):
    parts = []
    for name, rows in _SMALL:
        flat = d[name].reshape(-1)
        flat = jnp.pad(flat, (0, rows * 128 - flat.shape[0]))
        parts.append(flat.reshape(rows, 128))
    return jnp.concatenate(parts, axis=0)


def _unpack_small(packed, shapes):
    out, r0 = {}, 0
    for name, rows in _SMALL:
        size = 1
        for s in shapes[name]:
            size *= s
        out[name] = packed[r0:r0 + rows].reshape(-1)[:size].reshape(shapes[name])
        r0 += rows
    return out


def _rope_tables(positions):
    half = ROPE // 2
    inv_freq = ROPE_THETA ** (-jnp.arange(half, dtype=F32) / half)
    ang = positions.astype(F32)[:, None] * inv_freq
    cos, sin = jnp.cos(ang), jnp.sin(ang)
    return jnp.concatenate([cos, cos, cos, cos], axis=1), jnp.concatenate([-sin, sin, -sin, sin], axis=1)


def _local_step(x, positions, target, emb_g, emb_b, layer_weights, layer_weights_rest, on_sharded_grads,
                first_after=None):
    cos_t, sin_t = _rope_tables(positions)
    h, hb = _ln_fwd(x, emb_g, emb_b, "emb_ln_fwd", after=first_after)
    saved = []
    for l in range(DEPTH):
        W = layer_weights(l, h)
        proj = _mm(hb, W["w_in_t"], "nt", F32, "proj_fwd")
        qn, kvn, pooled, cv, mix = _mix_fwd(proj, W["q_norm_g"], W["kv_norm_g"], W["w_pool"], W["pool_scale"],
                                            W["conv_w"], "mix_fwd")
        rest, token = layer_weights_rest(l, proj)
        W = {**W, **rest}
        q = _mm(qn, W["w_uq"], "nn", F32, "q_up_fwd", after=token)
        kv = _mm(kvn, W["w_ukv"], "nn", BF16, "kv_up_fwd")
        qc, kc = _rope_fwd(q, kv, proj, cos_t, sin_t, "rope_fwd")
        o, mix, lse = _flash_fwd(qc, kc, kv, proj, mix, "flash_fwd")
        z = _mm(mix, W["w_out"], "nn", F32, "out_fwd", res=h, bias=W["b_out"], alpha=ALPHA)
        saved.append((W, hb, proj, qn, kvn, pooled, cv, kv, qc, kc, o, lse, mix, z))
        h, hb = _ln_fwd(z, W["ln_g"], W["ln_b"], "ln_fwd")
    sq, dh = _loss_head(h, target, "loss_head")

    grads = {k: [None] * DEPTH for k in ("q_norm_g", "kv_norm_g", "w_pool", "pool_scale", "conv_w", "b_out", "ln_g",
                                         "ln_b")}
    for l in reversed(range(DEPTH)):
        W, hb_in, proj, qn, kvn, pooled, cv, kv, qc, kc, o, lse, mix, z = saved[l]
        sharded = {}
        dz, dzb, grads["ln_g"][l], grads["ln_b"][l], grads["b_out"][l] = _ln_bwd(dh, z, W["ln_g"], "ln_bwd")
        dmix = _mm(dzb, W["w_out"], "nt", F32, "out_bwd_x")
        sharded["w_out"] = _mm(mix, dzb, "tn", GRAD_XFER, "out_bwd_w", tk=4096)
        do, delta, dproj, grads["w_pool"][l], grads["pool_scale"][l], grads["conv_w"][l] = _mix_bwd(
            dmix, proj, o, pooled, cv, W["w_pool"], W["pool_scale"], W["conv_w"], "mix_bwd")
        dqc, dkv, dkr = _flash_bwd(qc, kc, kv, do, lse, delta, "flash_bwd")
        dq, dkrope = _rope_bwd(dqc, dkr, cos_t, sin_t, "rope_bwd")
        dqn = _mm(dq, W["w_uq"], "nt", F32, "q_up_bwd_x")
        sharded["w_uq"] = _mm(qn, dq, "tn", GRAD_XFER, "q_up_bwd_w")
        dkvn = _mm(dkv, W["w_ukv"], "nt", F32, "kv_up_bwd_x")
        sharded["w_ukv"] = _mm(kvn, dkv, "tn", GRAD_XFER, "kv_up_bwd_w")
        token = on_sharded_grads(l, sharded)
        dproj, grads["q_norm_g"][l], grads["kv_norm_g"][l] = _rms_bwd(
            proj, dqn, dkvn, dkrope, dproj, W["q_norm_g"], W["kv_norm_g"], "rms_bwd")
        d_w_in_t = _mm(dproj, hb_in, "tn", GRAD_XFER, "proj_bwd_w", tk=4096, after=token)
        token = on_sharded_grads(l, {"w_in": d_w_in_t})
        dh = _mm(dproj, W["w_in_t"], "nn", F32, "proj_bwd_x", res=dz, alpha=ALPHA, tk=2560, after=token)
    grad_x, _, grads["emb_ln_g"], grads["emb_ln_b"], _ = _ln_bwd(dh, x, emb_g, "emb_ln_bwd")
    return sq, grad_x, grads


def kernel(x, positions, emb_ln_g, emb_ln_b, w_in, q_norm_g, kv_norm_g, w_uq, w_ukv, w_pool, pool_scale, conv_w, w_out, b_out, ln_g, ln_b, loss_target, m_emb_ln_g, m_emb_ln_b, m_w_in, m_q_norm_g, m_kv_norm_g, m_w_uq, m_w_ukv, m_w_pool, m_pool_scale, m_conv_w, m_w_out, m_b_out, m_ln_g, m_ln_b, v_emb_ln_g, v_emb_ln_b, v_w_in, v_q_norm_g, v_kv_norm_g, v_w_uq, v_w_ukv, v_w_pool, v_pool_scale, v_conv_w, v_w_out, v_b_out, v_ln_g, v_ln_b):
    weights = dict(emb_ln_g=emb_ln_g, emb_ln_b=emb_ln_b, w_in=w_in, q_norm_g=q_norm_g, kv_norm_g=kv_norm_g,
                   w_uq=w_uq, w_ukv=w_ukv, w_pool=w_pool, pool_scale=pool_scale, conv_w=conv_w, w_out=w_out,
                   b_out=b_out, ln_g=ln_g, ln_b=ln_b)
    mom1 = dict(emb_ln_g=m_emb_ln_g, emb_ln_b=m_emb_ln_b, w_in=m_w_in, q_norm_g=m_q_norm_g, kv_norm_g=m_kv_norm_g,
                w_uq=m_w_uq, w_ukv=m_w_ukv, w_pool=m_w_pool, pool_scale=m_pool_scale, conv_w=m_conv_w,
                w_out=m_w_out, b_out=m_b_out, ln_g=m_ln_g, ln_b=m_ln_b)
    mom2 = dict(emb_ln_g=v_emb_ln_g, emb_ln_b=v_emb_ln_b, w_in=v_w_in, q_norm_g=v_q_norm_g, kv_norm_g=v_kv_norm_g,
                w_uq=v_w_uq, w_ukv=v_w_ukv, w_pool=v_w_pool, pool_scale=v_pool_scale, conv_w=v_conv_w,
                w_out=v_w_out, b_out=v_b_out, ln_g=v_ln_g, ln_b=v_ln_b)

    big = ("w_in", "w_uq", "w_ukv", "w_out")

    conv_pad = jnp.zeros((8, 128), F32).at[0:DEPTH * CONV_WIDTH, 0:64].set(conv_w.reshape(DEPTH * CONV_WIDTH, 64))
    t12 = lambda a: jnp.swapaxes(a, 1, 2)
    shard = lambda k, l: (t12(weights[k])[l] if k == "w_in" else weights[k][l]).astype(BF16)
    w_in0, conv_all = _all_gather([shard("w_in", 0), conv_pad], "w_in0_all_gather")
    rest0 = _split_start([shard(k, 0) for k in big[1:]], False, w_in0, "weights0_rest_start")
    conv_full = _cat_blocks(conv_all[:, 0:DEPTH * CONV_WIDTH, 0:64], 1).reshape(DEPTH, CONV_WIDTH, D_CONV)
    conv_full = jnp.pad(conv_full, ((0, 0), (0, 8 - CONV_WIDTH), (0, 0)))
    fetched = {}

    def layer_weights(l, ready):
        if l == 0:
            w_in_blocks = w_in0
        else:
            fetched[1] = _split_wait(*fetched["w1"][:4], ready, "weights1_wait")[1]
            w_in_blocks = fetched[1][0]
        return dict(
            w_in_t=_permute_w_in_t(w_in_blocks), conv_w=conv_full[l],
            q_norm_g=q_norm_g[l].reshape(1, -1), kv_norm_g=kv_norm_g[l].reshape(1, -1),
            w_pool=w_pool[l].astype(BF16), pool_scale=pool_scale[l].reshape(1, -1), b_out=b_out[l].reshape(1, -1),
            ln_g=ln_g[l].reshape(1, -1), ln_b=ln_b[l].reshape(1, -1))

    def layer_weights_rest(l, ready):
        token = None
        if l == 0:
            blocks = _split_wait(*rest0[:4], ready, "weights0_rest_wait")[1]
            fetched["w1"] = _split_start([shard(k, 1) for k in big], False, blocks[0], "weights1_start")
            token = fetched["w1"][4]
        else:
            blocks = fetched[1][1:]
        return dict(w_uq=_permute_w_uq(_cat_blocks(blocks[0], 1)), w_ukv=_cat_blocks(blocks[1], 1),
                    w_out=_cat_blocks(blocks[2], 0)), token

    by_dest = dict(
        w_in=_split_w_in_t,
        w_uq=lambda g: _unpermute_w_uq(g).reshape(Q_LORA, N_DEV, -1).transpose(1, 0, 2),
        w_ukv=lambda g: g.reshape(KV_LORA, N_DEV, -1).transpose(1, 0, 2),
        w_out=lambda g: g.reshape(N_DEV, -1, D_MODEL))
    in_flight = []

    def on_sharded_grads(l, g):
        names = [k for k in big if k in g]
        srcs = [by_dest[k](g[k]) for k in names]
        started = _split_start(srcs, True, srcs[0], "grads%d_%s_start" % (l, names[0]))
        in_flight.append((l, names, started[:4]))
        return started[4]

    sq, grad_x, G = _local_step(x[0], positions[0], loss_target[0], emb_ln_g.reshape(1, -1),
                                emb_ln_b.reshape(1, -1), layer_weights, layer_weights_rest, on_sharded_grads,
                                first_after=rest0[4])
    loss = lax.psum(sq[0, 0] * (0.5 / D_MODEL), ("x", "y", "c"))

    res = {}
    landed = {}
    for l, names, started in in_flight:
        zones = _split_wait(*started, grad_x, "grads%d_%s_wait" % (l, names[0]))[1]
        for k, zone in zip(names, zones):
            landed[k, l] = zone
    res["w_in"] = tuple(t12(o) for o in _adamw(
        [landed["w_in", l] for l in range(DEPTH)], t12(w_in), t12(m_w_in), t12(v_w_in), "adamw_w_in", W_SHARD, 512))
    for name, rows in (("w_uq", 256), ("w_ukv", 256), ("w_out", 128)):
        res[name] = _adamw([landed[name, l] for l in range(DEPTH)], weights[name], mom1[name], mom2[name],
                           "adamw_" + name, rows)

    d_conv = jnp.stack([G["conv_w"][l][0:CONV_WIDTH] for l in range(DEPTH)])
    d_conv = d_conv.reshape(DEPTH * CONV_WIDTH, N_DEV, 64).transpose(1, 0, 2)
    d_conv = jnp.zeros((N_DEV, 8, 128), F32).at[:, 0:DEPTH * CONV_WIDTH, 0:64].set(d_conv)
    small = dict(emb_ln_g=G["emb_ln_g"], emb_ln_b=G["emb_ln_b"])
    for k in ("q_norm_g", "kv_norm_g", "w_pool", "pool_scale", "b_out", "ln_g", "ln_b"):
        small[k] = jnp.stack(G[k])
    d_small = jnp.broadcast_to(_pack_small(small)[None], (N_DEV, SMALL_ROWS, 128))
    l_conv, l_small = _exchange([d_conv, d_small], "small_gradient_exchange")
    conv_shard = lambda a: jnp.zeros((8, 128), F32).at[0:DEPTH * CONV_WIDTH, 0:64].set(a.reshape(-1, 64))
    conv_res = _adamw([l_conv], conv_shard(conv_w)[None], conv_shard(m_conv_w)[None], conv_shard(v_conv_w)[None],
                      "adamw_conv_w", 8)
    res["conv_w"] = tuple(o[0, 0:DEPTH * CONV_WIDTH, 0:64].reshape(DEPTH, CONV_WIDTH, 64) for o in conv_res)
    small_res = _adamw([l_small], _pack_small(weights)[None], _pack_small(mom1)[None], _pack_small(mom2)[None],
                       "adamw_small", 392)
    shapes = {k: weights[k].shape for k, _ in _SMALL}
    unpacked = [_unpack_small(o[0], shapes) for o in small_res]
    for k, _ in _SMALL:
        res[k] = tuple(u[k] for u in unpacked)

    order = ("emb_ln_g", "emb_ln_b", "w_in", "q_norm_g", "kv_norm_g", "w_uq", "w_ukv", "w_pool", "pool_scale",
             "conv_w", "w_out", "b_out", "ln_g", "ln_b")
    return (loss, grad_x[None], *[res[k][0] for k in order], *[res[k][1] for k in order],
            *[res[k][2] for k in order], *[res[k][3] for k in order])
```

```python
import jax
import jax.numpy as jnp
from jax import lax
from jax.experimental import pallas as pl
from jax.experimental.pallas import tpu as pltpu

F32 = jnp.float32
BF16 = jnp.bfloat16

N_DEV = 8
D_MODEL = 2048
DEPTH = 2
N_HEADS = 8
NOPE = 128
ROPE = 64
V_DIM = 128
Q_LORA = 512
KV_LORA = 256
D_MLA = N_HEADS * V_DIM
D_POOL = 512
D_CONV = 512
POOL_WINDOWS = (2, 4, 8, 16)
POOL_GROUP = 128
CONV_WIDTH = 3
D_MIX = D_MLA + D_POOL + D_CONV
D_IN_PROJ = 4928
ROPE_THETA = 10000.0
LN_EPS = 1e-5
RMS_EPS = 1e-6
ALPHA = (2 * DEPTH) ** 0.25
ATTN_SCALE = (NOPE + ROPE) ** -0.5
ADAM_LR = 0.001
ADAM_B1 = 0.9
ADAM_B2 = 0.999
ADAM_EPS = 1e-08
ADAM_WD = 0.01
ADAM_STEP = 10

O_GMLA, O_PIN, O_GPOOL, O_CH, O_CB, O_CC, O_GCONV, O_QLAT, O_KVLAT, O_KROPE = (
    0, 1024, 1536, 2048, 2560, 3072, 3584, 4096, 4608, 4864)
NPP = 5120
N_GATED = O_QLAT
QC = NOPE + 2 * ROPE
HALO = 16
ATT_TILE = 512
ATT_CH = 256
LOG2E = 1.4426950408889634
EXP2_SCALE = ATTN_SCALE * LOG2E

GRAD_XFER = BF16
VMEM_LIMIT = 48 * 1024 * 1024
MESH_ID = pl.DeviceIdType.MESH


def _params(sem=None):
    return pltpu.CompilerParams(dimension_semantics=sem, vmem_limit_bytes=VMEM_LIMIT)


def _sigmoid(x):
    return 1.0 / (1.0 + jnp.exp(-x))


def _tile(dim, target):
    if dim <= target:
        return dim
    t = target - target % 128
    while dim % t:
        t -= 128
    return t


_DIMS = {"nn": (((1,), (0,)), ((), ())), "nt": (((1,), (1,)), ((), ())), "tn": (((0,), (0,)), ((), ()))}


def _mm(a, b, mode, out_dtype, name, res=None, bias=None, alpha=1.0, tm=1024, tn=1024, tk=2048, after=None):
    if mode == "nn":
        (M, K), (K2, N) = a.shape, b.shape
    elif mode == "nt":
        (M, K), (N, K2) = a.shape, b.shape
    else:
        (K, M), (K2, N) = a.shape, b.shape
    assert K == K2
    tm, tn, tk = _tile(M, tm), _tile(N, tn), _tile(K, tk)
    nk = K // tk
    has_res, has_bias = res is not None, bias is not None

    def body(*refs):
        a_ref, b_ref = refs[0], refs[1]
        pos = 2
        res_ref = bias_ref = None
        if has_res:
            res_ref = refs[pos]
            pos += 1
        if has_bias:
            bias_ref = refs[pos]
            pos += 1
        def finish(r, o_ref):
            if has_bias:
                r = r + bias_ref[...]
            if has_res:
                r = alpha * res_ref[...] + r
            o_ref[...] = r.astype(out_dtype)

        part = lax.dot_general(a_ref[...].astype(BF16), b_ref[...].astype(BF16), _DIMS[mode],
                               preferred_element_type=F32)
        if nk == 1:
            finish(part, refs[-1])
            return
        o_ref, acc_ref = refs[-2], refs[-1]
        k = pl.program_id(2)

        @pl.when(k == 0)
        def _():
            acc_ref[...] = part

        @pl.when(jnp.logical_and(k > 0, k < nk - 1))
        def _():
            acc_ref[...] += part

        @pl.when(k == nk - 1)
        def _():
            finish(acc_ref[...] + part, o_ref)

    if mode == "nn":
        in_specs = [pl.BlockSpec((tm, tk), lambda i, j, k: (i, k)), pl.BlockSpec((tk, tn), lambda i, j, k: (k, j))]
    elif mode == "nt":
        in_specs = [pl.BlockSpec((tm, tk), lambda i, j, k: (i, k)), pl.BlockSpec((tn, tk), lambda i, j, k: (j, k))]
    else:
        in_specs = [pl.BlockSpec((tk, tm), lambda i, j, k: (k, i)), pl.BlockSpec((tk, tn), lambda i, j, k: (k, j))]
    args = [a, b]
    if has_res:
        in_specs.append(pl.BlockSpec((tm, tn), lambda i, j, k: (i, j)))
        args.append(res)
    if has_bias:
        in_specs.append(pl.BlockSpec((1, tn), lambda i, j, k: (0, j)))
        args.append(bias)
    if after is not None:
        in_specs.append(pl.BlockSpec((8, 128), lambda i, j, k: (0, 0)))
        args.append(after)
    return pl.pallas_call(
        body, name=name,
        out_shape=jax.ShapeDtypeStruct((M, N), out_dtype),
        grid=(M // tm, N // tn, nk),
        in_specs=in_specs,
        out_specs=pl.BlockSpec((tm, tn), lambda i, j, k: (i, j)),
        scratch_shapes=[pltpu.VMEM((tm, tn), F32)] if nk > 1 else [],
        compiler_params=_params(("parallel", "parallel", "arbitrary")),
    )(*args)


def _ln_fwd(z, g, b, name, tq=256, after=None):
    T, D = z.shape

    def body(z_ref, g_ref, b_ref, *rest):
        y_ref, yb_ref = rest[-2:]
        zv = z_ref[...]
        mu = jnp.mean(zv, axis=1, keepdims=True)
        zc = zv - mu
        var = jnp.mean(zc * zc, axis=1, keepdims=True)
        y = zc * lax.rsqrt(var + LN_EPS) * g_ref[...] + b_ref[...]
        y_ref[...] = y
        yb_ref[...] = y.astype(BF16)

    row = pl.BlockSpec((tq, D), lambda i: (i, 0))
    vec = pl.BlockSpec((1, D), lambda i: (0, 0))
    return pl.pallas_call(
        body, name=name,
        out_shape=(jax.ShapeDtypeStruct((T, D), F32), jax.ShapeDtypeStruct((T, D), BF16)),
        grid=(T // tq,),
        in_specs=[row, vec, vec] + ([pl.BlockSpec((8, 128), lambda i: (0, 0))] if after is not None else []),
        out_specs=(row, row),
        compiler_params=_params(("parallel",)),
    )(z, g, b, *([after] if after is not None else []))


def _ln_bwd(dy, z, g, name, tq=256):
    T, D = z.shape

    def body(dy_ref, z_ref, g_ref, dz_ref, dzb_ref, dg_ref, db_ref, ds_ref):
        @pl.when(pl.program_id(0) == 0)
        def _():
            dg_ref[...] = jnp.zeros_like(dg_ref)
            db_ref[...] = jnp.zeros_like(db_ref)
            ds_ref[...] = jnp.zeros_like(ds_ref)

        zv, dyv = z_ref[...], dy_ref[...]
        mu = jnp.mean(zv, axis=1, keepdims=True)
        zc = zv - mu
        var = jnp.mean(zc * zc, axis=1, keepdims=True)
        rstd = lax.rsqrt(var + LN_EPS)
        xh = zc * rstd
        u = dyv * g_ref[...]
        dz = rstd * (u - jnp.mean(u, axis=1, keepdims=True) - xh * jnp.mean(u * xh, axis=1, keepdims=True))
        dz_ref[...] = dz
        dzb_ref[...] = dz.astype(BF16)
        dg_ref[...] += jnp.sum(dyv * xh, axis=0, keepdims=True)
        db_ref[...] += jnp.sum(dyv, axis=0, keepdims=True)
        ds_ref[...] += jnp.sum(dz, axis=0, keepdims=True)

    row = pl.BlockSpec((tq, D), lambda i: (i, 0))
    vec = pl.BlockSpec((1, D), lambda i: (0, 0))
    vshape = jax.ShapeDtypeStruct((1, D), F32)
    return pl.pallas_call(
        body, name=name,
        out_shape=(jax.ShapeDtypeStruct((T, D), F32), jax.ShapeDtypeStruct((T, D), BF16), vshape, vshape, vshape),
        grid=(T // tq,), in_specs=[row, row, vec], out_specs=(row, row, vec, vec, vec),
        compiler_params=_params(("arbitrary",)),
    )(dy, z, g)


def _loss_head(y, target, name, tq=256):
    T, D = y.shape

    def body(y_ref, t_ref, s_ref, dy_ref):
        @pl.when(pl.program_id(0) == 0)
        def _():
            s_ref[...] = jnp.zeros_like(s_ref)

        err = y_ref[...] - t_ref[...]
        s_ref[...] += jnp.sum(err * err)
        dy_ref[...] = err * (1.0 / D)

    row = pl.BlockSpec((tq, D), lambda i: (i, 0))
    acc = pl.BlockSpec((8, 128), lambda i: (0, 0))
    return pl.pallas_call(
        body, name=name,
        out_shape=(jax.ShapeDtypeStruct((8, 128), F32), jax.ShapeDtypeStruct((T, D), F32)),
        grid=(T // tq,), in_specs=[row, row], out_specs=(acc, row),
        compiler_params=_params(("arbitrary",)),
    )(y, target)


def _pblock(tq, width, offset):
    assert offset % width == 0
    blk = offset // width
    return pl.BlockSpec((tq, width), lambda i: (i, blk))


def _mix_fwd(proj, q_g, kv_g, w_pool, pool_scale, conv_w, name, tq=256):
    T = proj.shape[0]

    def body(ql_ref, kvl_ref, pin_ref, gp_ref, ch_ref, cb_ref, cc_ref, gc_ref, qg_ref, kvg_ref, wp_ref, ps_ref,
             cw_ref, qn_ref, kvn_ref, pooled_ref, cv_ref, ypc_ref, extp, extu):
        i = pl.program_id(0)
        for x_ref, g_ref, o_ref in ((ql_ref, qg_ref, qn_ref), (kvl_ref, kvg_ref, kvn_ref)):
            x = x_ref[...]
            r = lax.rsqrt(jnp.mean(x * x, axis=1, keepdims=True) + RMS_EPS)
            o_ref[...] = (x * r * g_ref[...]).astype(BF16)

        @pl.when(i == 0)
        def _():
            extp[0:HALO, :] = jnp.zeros((HALO, D_POOL), F32)
            extu[0:HALO, :] = jnp.zeros((HALO, D_CONV), F32)

        @pl.when(i > 0)
        def _():
            extp[0:HALO, :] = extp[tq:tq + HALO, :]
            extu[0:HALO, :] = extu[tq:tq + HALO, :]

        pin = pin_ref[...]
        extp[HALO:, :] = pin
        u = cc_ref[...] * ch_ref[...]
        extu[HALO:, :] = u
        t1 = (i * tq + lax.broadcasted_iota(jnp.int32, (tq, 1), 0) + 1).astype(F32)
        for g, w in enumerate(POOL_WINDOWS):
            cols = slice(g * POOL_GROUP, (g + 1) * POOL_GROUP)
            s = extp[:, cols]
            k = 1
            while k < w:
                s = s + pltpu.roll(s, k, 0)
                k *= 2
            mean = s[HALO:, :] / jnp.minimum(t1, float(w))
            pooled = (mean - pin[:, cols]).astype(BF16)
            pooled_ref[:, cols] = pooled
            r = jnp.dot(pooled, wp_ref[g], preferred_element_type=F32)
            gp = gp_ref[:, cols]
            ypc_ref[:, cols] = (r * ps_ref[:, cols] * (gp * _sigmoid(gp))).astype(BF16)
        eu = extu[...]
        u1 = pltpu.roll(eu, 1, 0)[HALO:, :]
        u2 = pltpu.roll(eu, 2, 0)[HALO:, :]
        cv = cw_ref[0:1, :] * u2 + cw_ref[1:2, :] * u1 + cw_ref[2:3, :] * u
        cv_ref[...] = cv
        gc = gc_ref[...]
        ypc_ref[:, D_POOL:] = (cb_ref[...] * cv * (gc * _sigmoid(gc))).astype(BF16)

    full = lambda shape: pl.BlockSpec(shape, lambda i: (0,) * len(shape))
    row = lambda w: pl.BlockSpec((tq, w), lambda i: (i, 0))
    return pl.pallas_call(
        body, name=name,
        out_shape=(jax.ShapeDtypeStruct((T, Q_LORA), BF16), jax.ShapeDtypeStruct((T, KV_LORA), BF16),
                   jax.ShapeDtypeStruct((T, D_POOL), BF16), jax.ShapeDtypeStruct((T, D_CONV), F32),
                   jax.ShapeDtypeStruct((T, D_MIX), BF16)),
        grid=(T // tq,),
        in_specs=[_pblock(tq, Q_LORA, O_QLAT), _pblock(tq, KV_LORA, O_KVLAT), _pblock(tq, 512, O_PIN),
                  _pblock(tq, 512, O_GPOOL), _pblock(tq, 512, O_CH), _pblock(tq, 512, O_CB), _pblock(tq, 512, O_CC),
                  _pblock(tq, 512, O_GCONV), full((1, Q_LORA)), full((1, KV_LORA)), full((4, 128, 128)),
                  full((1, D_POOL)), full((8, D_CONV))],
        out_specs=(row(Q_LORA), row(KV_LORA), row(D_POOL), row(D_CONV),
                   pl.BlockSpec((tq, D_POOL + D_CONV), lambda i: (i, D_MLA // (D_POOL + D_CONV)))),
        scratch_shapes=[pltpu.VMEM((tq + HALO, D_POOL), F32), pltpu.VMEM((tq + HALO, D_CONV), F32)],
        compiler_params=_params(("arbitrary",)),
    )(proj, proj, proj, proj, proj, proj, proj, proj, q_g, kv_g, w_pool, pool_scale, conv_w)


def _mix_bwd(dmix, proj, o, pooled, cv, w_pool, pool_scale, conv_w, name, tq=ATT_CH):
    T = proj.shape[0]
    nt = T // tq
    n_ext = tq + HALO

    def body(dym_ref, dyp_ref, dyc_ref, gm_ref, gp_ref, ch_ref, cb_ref, cc_ref, gc_ref, o_ref, pooled_ref, cv_ref,
             wp_ref, ps_ref, cw_ref, do_ref, delta_ref, dg_ref, dwp_ref, dps_ref, dcw_ref, exte, extd):
        i = pl.program_id(0)
        tile = nt - 1 - i

        @pl.when(i == 0)
        def _():
            dwp_ref[...] = jnp.zeros_like(dwp_ref)
            dps_ref[...] = jnp.zeros_like(dps_ref)
            dcw_ref[...] = jnp.zeros_like(dcw_ref)
            exte[tq:, :] = jnp.zeros((HALO, D_POOL), F32)
            extd[tq:, :] = jnp.zeros((HALO, D_CONV), F32)

        @pl.when(i > 0)
        def _():
            exte[tq:, :] = exte[0:HALO, :]
            extd[tq:, :] = extd[0:HALO, :]

        gm = gm_ref[...]
        sig = _sigmoid(gm)
        dym = dym_ref[...]
        ov = o_ref[...]
        do = dym * (gm * sig)
        do_ref[...] = do.astype(BF16)
        prod = do * ov
        ones = jnp.ones((8, V_DIM), F32)
        for h in range(N_HEADS):
            rows = lax.dot_general(ones, prod[:, h * V_DIM:(h + 1) * V_DIM], _DIMS["nt"],
                                   precision=lax.Precision.HIGHEST, preferred_element_type=F32)
            delta_ref[h, 0] = rows[0:1, :]
        dg_ref[:, O_GMLA:O_PIN] = (dym * ov * (sig * (1.0 + gm * (1.0 - sig)))).astype(BF16)

        t1 = (tile * tq + lax.broadcasted_iota(jnp.int32, (tq, 1), 0) + 1).astype(F32)
        for g, w in enumerate(POOL_WINDOWS):
            cols = slice(g * POOL_GROUP, (g + 1) * POOL_GROUP)
            pg = pooled_ref[:, cols]
            r = jnp.dot(pg, wp_ref[g], preferred_element_type=F32)
            gp = gp_ref[:, cols]
            sg = _sigmoid(gp)
            sl = gp * sg
            dyg = dyp_ref[:, cols]
            ps = ps_ref[:, cols]
            dg_ref[:, O_GPOOL + g * POOL_GROUP:O_GPOOL + (g + 1) * POOL_GROUP] = (
                dyg * (r * ps) * (sg * (1.0 + gp * (1.0 - sg)))).astype(BF16)
            dps_ref[:, cols] += jnp.sum(dyg * r * sl, axis=0, keepdims=True)
            dr = (dyg * ps * sl).astype(BF16)
            dwp_ref[g] += lax.dot_general(pg, dr, _DIMS["tn"], preferred_element_type=F32)
            dpooled = lax.dot_general(dr, wp_ref[g], _DIMS["nt"], preferred_element_type=F32)
            exte[0:tq, cols] = dpooled / jnp.minimum(t1, float(w))
            s = exte[:, cols]
            k = 1
            while k < w:
                s = s + pltpu.roll(s, n_ext - k, 0)
                k *= 2
            dg_ref[:, O_PIN + g * POOL_GROUP:O_PIN + (g + 1) * POOL_GROUP] = (s[0:tq, :] - dpooled).astype(BF16)

        gc = gc_ref[...]
        sg = _sigmoid(gc)
        sl = gc * sg
        dyc = dyc_ref[...]
        cb, cc, ch, cvv = cb_ref[...], cc_ref[...], ch_ref[...], cv_ref[...]
        dcv = dyc * cb * sl
        dg_ref[:, O_GCONV:O_GCONV + D_CONV] = (dyc * (cb * cvv) * (sg * (1.0 + gc * (1.0 - sg)))).astype(BF16)
        dg_ref[:, O_CB:O_CB + D_CONV] = (dyc * cvv * sl).astype(BF16)
        extd[0:tq, :] = dcv
        ed = extd[...]
        d1 = pltpu.roll(ed, n_ext - 1, 0)[0:tq, :]
        d2 = pltpu.roll(ed, n_ext - 2, 0)[0:tq, :]
        du = cw_ref[2:3, :] * dcv + cw_ref[1:2, :] * d1 + cw_ref[0:1, :] * d2
        u = cc * ch
        dcw_ref[0:1, :] += jnp.sum(u * d2, axis=0, keepdims=True)
        dcw_ref[1:2, :] += jnp.sum(u * d1, axis=0, keepdims=True)
        dcw_ref[2:3, :] += jnp.sum(u * dcv, axis=0, keepdims=True)
        dg_ref[:, O_CH:O_CH + D_CONV] = (du * cc).astype(BF16)
        dg_ref[:, O_CC:O_CC + D_CONV] = (du * ch).astype(BF16)

    def rblock(width, offset):
        assert offset % width == 0
        blk = offset // width
        return pl.BlockSpec((tq, width), lambda i: (nt - 1 - i, blk))

    full = lambda shape: pl.BlockSpec(shape, lambda i: (0,) * len(shape))
    return pl.pallas_call(
        body, name=name,
        out_shape=(jax.ShapeDtypeStruct((T, D_MLA), BF16), jax.ShapeDtypeStruct((N_HEADS, nt, 1, tq), F32),
                   jax.ShapeDtypeStruct((T, NPP), BF16),
                   jax.ShapeDtypeStruct((4, 128, 128), F32), jax.ShapeDtypeStruct((1, D_POOL), F32),
                   jax.ShapeDtypeStruct((8, D_CONV), F32)),
        grid=(nt,),
        in_specs=[rblock(1024, 0), rblock(512, 1024), rblock(512, 1536),
                  rblock(1024, O_GMLA), rblock(512, O_GPOOL), rblock(512, O_CH), rblock(512, O_CB),
                  rblock(512, O_CC), rblock(512, O_GCONV), rblock(1024, 0), rblock(512, 0), rblock(512, 0),
                  full((4, 128, 128)), full((1, D_POOL)), full((8, D_CONV))],
        out_specs=(rblock(1024, 0), pl.BlockSpec((N_HEADS, 1, 1, tq), lambda i: (0, nt - 1 - i, 0, 0)),
                   rblock(N_GATED, 0), full((4, 128, 128)), full((1, D_POOL)), full((8, D_CONV))),
        scratch_shapes=[pltpu.VMEM((n_ext, D_POOL), F32), pltpu.VMEM((n_ext, D_CONV), F32)],
        compiler_params=_params(("arbitrary",)),
    )(dmix, dmix, dmix, proj, proj, proj, proj, proj, proj, o, pooled, cv, w_pool, pool_scale, conv_w)


def _rms_bwd(proj, dqn, dkvn, dkrope, dproj, q_g, kv_g, name, tq=256):
    T = proj.shape[0]
    n_lat = NPP - N_GATED

    def body(ql_ref, kvl_ref, dqn_ref, dkvn_ref, dkr_ref, _, qg_ref, kvg_ref, dlat_ref, dqg_ref, dkvg_ref):
        @pl.when(pl.program_id(0) == 0)
        def _():
            dqg_ref[...] = jnp.zeros_like(dqg_ref)
            dkvg_ref[...] = jnp.zeros_like(dkvg_ref)

        for x_ref, dy_ref, g_ref, c0, dg_ref in ((ql_ref, dqn_ref, qg_ref, 0, dqg_ref),
                                                 (kvl_ref, dkvn_ref, kvg_ref, Q_LORA, dkvg_ref)):
            x, dy = x_ref[...], dy_ref[...]
            r = lax.rsqrt(jnp.mean(x * x, axis=1, keepdims=True) + RMS_EPS)
            xr = x * r
            u = dy * g_ref[...]
            dlat_ref[:, c0:c0 + x.shape[1]] = (r * (u - xr * jnp.mean(u * xr, axis=1, keepdims=True))).astype(BF16)
            dg_ref[...] += jnp.sum(dy * xr, axis=0, keepdims=True)
        dlat_ref[:, Q_LORA + KV_LORA:] = dkr_ref[...]

    row = lambda w: pl.BlockSpec((tq, w), lambda i: (i, 0))
    vec = lambda w: pl.BlockSpec((1, w), lambda i: (0, 0))
    assert N_GATED % n_lat == 0
    return pl.pallas_call(
        body, name=name,
        out_shape=(jax.ShapeDtypeStruct((T, NPP), BF16),
                   jax.ShapeDtypeStruct((1, Q_LORA), F32), jax.ShapeDtypeStruct((1, KV_LORA), F32)),
        grid=(T // tq,),
        in_specs=[_pblock(tq, Q_LORA, O_QLAT), _pblock(tq, KV_LORA, O_KVLAT), row(Q_LORA), row(KV_LORA),
                  row(n_lat - Q_LORA - KV_LORA), pl.BlockSpec(memory_space=pl.ANY), vec(Q_LORA), vec(KV_LORA)],
        out_specs=(pl.BlockSpec((tq, n_lat), lambda i: (i, N_GATED // n_lat)), vec(Q_LORA), vec(KV_LORA)),
        input_output_aliases={5: 0},
        compiler_params=_params(("arbitrary",)),
    )(proj, proj, dqn, dkvn, dkrope, dproj, q_g, kv_g)


def _swap_halves(x, lo):
    return jnp.where(lo, pltpu.roll(x, 96, 1), pltpu.roll(x, 32, 1))


def _rope_fwd(q, kv, proj, cos_t, sin_t, name, tq=256):
    T = q.shape[0]

    def body(qn_ref, qr_ref, kv_ref, kr_ref, c_ref, s_ref, qc_ref, kc_ref):
        C, S = c_ref[...], s_ref[...]
        lane = lax.broadcasted_iota(jnp.int32, (tq, 128), 1)
        lo = (lane % ROPE) < (ROPE // 2)
        first = lane < ROPE

        def rope(x):
            return x * C + _swap_halves(x, lo) * S

        kr = jnp.where(first, rope(kr_ref[...]), 0.0).astype(BF16)
        for j in range(N_HEADS // 2):
            r = rope(qr_ref[:, j * 128:(j + 1) * 128])
            pair = (jnp.where(first, r, 0.0), jnp.where(first, pltpu.roll(r, 64, 1), 0.0))
            for hh in range(2):
                h = 2 * j + hh
                qc_ref[h, :, 0:NOPE] = qn_ref[:, h * NOPE:(h + 1) * NOPE].astype(BF16)
                qc_ref[h, :, NOPE:QC] = pair[hh].astype(BF16)
        for h in range(N_HEADS):
            kc_ref[h, :, 0:NOPE] = kv_ref[:, h * 256:h * 256 + NOPE]
            kc_ref[h, :, NOPE:QC] = kr

    out = jax.ShapeDtypeStruct((N_HEADS, T, QC), BF16)
    hblock = pl.BlockSpec((N_HEADS, tq, QC), lambda i: (0, i, 0))
    return pl.pallas_call(
        body, name=name, out_shape=(out, out), grid=(T // tq,),
        in_specs=[pl.BlockSpec((tq, 1024), lambda i: (i, 0)), pl.BlockSpec((tq, 512), lambda i: (i, 2)),
                  pl.BlockSpec((tq, 2048), lambda i: (i, 0)), _pblock(tq, 128, O_KROPE),
                  pl.BlockSpec((tq, 128), lambda i: (i, 0)), pl.BlockSpec((tq, 128), lambda i: (i, 0))],
        out_specs=(hblock, hblock),
        compiler_params=_params(("parallel",)),
    )(q, q, kv, proj, cos_t, sin_t)


def _rope_bwd(dqc, dkr, cos_t, sin_t, name, tq=256):
    T = dqc.shape[1]

    def body(dqc_ref, dkr_ref, c_ref, s_ref, dq_ref, dk_ref):
        C, S = c_ref[...], s_ref[...]
        lane = lax.broadcasted_iota(jnp.int32, (tq, 128), 1)
        lo = (lane % ROPE) < (ROPE // 2)
        first = lane < ROPE

        def unrope(dy):
            return dy * C - _swap_halves(dy, lo) * S

        acc = dkr_ref[0]
        for h in range(1, N_HEADS):
            acc = acc + dkr_ref[h]
        dk_ref[:, 0:128] = jnp.where(first, unrope(acc), 0.0).astype(BF16)
        dk_ref[:, 128:256] = jnp.zeros((tq, 128), BF16)
        for j in range(N_HEADS // 2):
            d0 = dqc_ref[2 * j, :, NOPE:QC]
            d1 = dqc_ref[2 * j + 1, :, NOPE:QC]
            comb = jnp.where(first, d0, pltpu.roll(d1, 64, 1))
            dq_ref[:, 1024 + j * 128:1024 + (j + 1) * 128] = unrope(comb).astype(BF16)
        for h in range(N_HEADS):
            dq_ref[:, h * NOPE:(h + 1) * NOPE] = dqc_ref[h, :, 0:NOPE].astype(BF16)

    tab = pl.BlockSpec((tq, 128), lambda i: (i, 0))
    return pl.pallas_call(
        body, name=name,
        out_shape=(jax.ShapeDtypeStruct((T, 1536), BF16), jax.ShapeDtypeStruct((T, 256), BF16)),
        grid=(T // tq,),
        in_specs=[pl.BlockSpec((N_HEADS, tq, QC), lambda i: (0, i, 0)),
                  pl.BlockSpec((N_HEADS, tq, 128), lambda i: (0, i, 0)), tab, tab],
        out_specs=(pl.BlockSpec((tq, 1536), lambda i: (i, 0)), pl.BlockSpec((tq, 256), lambda i: (i, 0))),
        compiler_params=_params(("parallel",)),
    )(dqc, dkr, cos_t, sin_t)


def _flash_fwd(qc, kc, kv, proj, mix, name):
    H, T, _ = qc.shape
    tt = ATT_TILE
    nt = T // tt
    sp = tt // ATT_CH

    def body(q_ref, k_ref, v_ref, g_ref, _, o_ref, y_ref, lse_ref, vt_sc, s_sc, acc_sc, m_sc, l_sc):
        i = pl.program_id(1)

        @pl.when(i == 0)
        def _():
            for c in range(nt):
                vt_sc[c] = v_ref[c * tt:(c + 1) * tt, :].astype(F32).T.astype(BF16)

        q = q_ref[0]

        def issue(c, slot):
            s_sc[slot] = lax.dot_general(k_ref[0, pl.ds(pl.multiple_of(c * tt, tt), tt), :], q, _DIMS["nt"],
                                         preferred_element_type=F32)

        def softmax_pv(c, slot, masked):
            s = s_sc[slot]
            if masked:
                krow = c * tt + lax.broadcasted_iota(jnp.int32, s.shape, 0)
                qcol = i * tt + lax.broadcasted_iota(jnp.int32, s.shape, 1)
                s = jnp.where(krow <= qcol, s, -jnp.inf)
            m = m_sc[...]
            m_new = jnp.maximum(m, jnp.max(s, axis=0, keepdims=True))
            p = jnp.exp2((s - m_new) * EXP2_SCALE)
            a = jnp.exp2((m - m_new) * EXP2_SCALE)
            l_sc[...] = a * l_sc[...] + jnp.sum(p, axis=0, keepdims=True)
            acc_sc[...] = a * acc_sc[...] + jnp.dot(vt_sc[c], p.astype(BF16), preferred_element_type=F32)
            m_sc[...] = m_new

        m_sc[...] = jnp.full_like(m_sc, -jnp.inf)
        l_sc[...] = jnp.zeros_like(l_sc)
        acc_sc[...] = jnp.zeros_like(acc_sc)
        issue(0, 0)

        def pair(t, carry):
            issue(2 * t + 1, 1)
            softmax_pv(2 * t, 0, False)
            issue(2 * t + 2, 0)
            softmax_pv(2 * t + 1, 1, False)
            return carry

        lax.fori_loop(0, i // 2, pair, 0)

        @pl.when(i % 2 == 1)
        def _():
            issue(i, 1)
            softmax_pv(i - 1, 0, False)
            softmax_pv(i, 1, True)

        @pl.when(i % 2 == 0)
        def _():
            softmax_pv(i, 0, True)

        l = l_sc[...]
        o = (acc_sc[...] / l).T
        o_ref[...] = o
        lse = m_sc[...] * ATTN_SCALE + jnp.log(l)
        for r in range(sp):
            lse_ref[0, r] = lse[:, r * ATT_CH:(r + 1) * ATT_CH]
        g = g_ref[...]
        y_ref[...] = (o * (g * _sigmoid(g))).astype(BF16)

    return pl.pallas_call(
        body, name=name,
        out_shape=(jax.ShapeDtypeStruct((T, D_MLA), F32), jax.ShapeDtypeStruct((T, D_MIX), BF16),
                   jax.ShapeDtypeStruct((H, T // ATT_CH, 1, ATT_CH), F32)),
        grid=(H, nt),
        in_specs=[pl.BlockSpec((1, tt, QC), lambda h, i: (h, i, 0)),
                  pl.BlockSpec((1, T, QC), lambda h, i: (h, 0, 0)),
                  pl.BlockSpec((T, V_DIM), lambda h, i: (0, 2 * h + 1)),
                  pl.BlockSpec((tt, V_DIM), lambda h, i: (i, h)),
                  pl.BlockSpec(memory_space=pl.ANY)],
        input_output_aliases={4: 1},
        out_specs=(pl.BlockSpec((tt, V_DIM), lambda h, i: (i, h)),
                   pl.BlockSpec((tt, V_DIM), lambda h, i: (i, h)),
                   pl.BlockSpec((1, sp, 1, ATT_CH), lambda h, i: (h, i, 0, 0))),
        scratch_shapes=[pltpu.VMEM((nt, V_DIM, tt), BF16), pltpu.VMEM((2, tt, tt), F32),
                        pltpu.VMEM((V_DIM, tt), F32), pltpu.VMEM((1, tt), F32), pltpu.VMEM((1, tt), F32)],
        compiler_params=_params(("parallel", "arbitrary")),
    )(qc, kc, kv, proj, mix)


def _flash_bwd(qc, kc, kv, do, lse, delta, name):
    H, T, _ = qc.shape
    tt = ATT_TILE
    nt = T // tt
    sp = tt // ATT_CH

    def body(q_ref, k_ref, v_ref, do_ref, lse_ref, dl_ref, dq_ref, dkv_ref, dkr_ref, dqt_sc, dk_sc, dv_sc, s_sc,
             dp_sc):
        j = pl.program_id(1)

        @pl.when(j == 0)
        def _():
            dqt_sc[...] = jnp.zeros_like(dqt_sc)

        dk_sc[1] = jnp.zeros((tt, QC), F32)
        dv_sc[1] = jnp.zeros((tt, V_DIM), F32)
        k = k_ref[0]
        v = v_ref[...]
        kt = k.astype(F32).T.astype(BF16)

        def operands(c):
            q0 = pl.multiple_of(c * tt, tt)
            return q_ref[0, pl.ds(q0, tt), :], do_ref[pl.ds(q0, tt), :]

        def stat_row(ref, c):
            return jnp.concatenate([ref[0, sp * c + r] for r in range(sp)], axis=1)

        def early(c, slot):
            q, dov = operands(c)
            s_sc[slot] = lax.dot_general(k, q, _DIMS["nt"], preferred_element_type=F32)
            dp_sc[slot] = lax.dot_general(v, dov, _DIMS["nt"], preferred_element_type=F32)

        def late(c, slot, masked):
            q, dov = operands(c)
            s, dp = s_sc[slot], dp_sc[slot]
            if masked:
                krow = j * tt + lax.broadcasted_iota(jnp.int32, s.shape, 0)
                qcol = c * tt + lax.broadcasted_iota(jnp.int32, s.shape, 1)
                s = jnp.where(krow <= qcol, s, -jnp.inf)
            p = jnp.exp2(s * EXP2_SCALE - stat_row(lse_ref, c) * LOG2E)
            ds = (p * (dp - stat_row(dl_ref, c)) * ATTN_SCALE).astype(BF16)
            dv = jnp.dot(p.astype(BF16), dov, preferred_element_type=F32)
            dk = jnp.dot(ds, q, preferred_element_type=F32)
            if masked:
                dv_sc[slot] = dv
                dk_sc[slot] = dk
            else:
                dv_sc[slot] += dv
                dk_sc[slot] += dk
            dqt_sc[c] += jnp.dot(kt, ds, preferred_element_type=F32)

        early(j, 0)
        early(jnp.minimum(j + 1, nt - 1), 1)
        late(j, 0, True)
        n_rest = nt - 1 - j

        def pair(u, carry):
            a = j + 1 + 2 * u
            early(a + 1, 0)
            late(a, 1, False)
            early(jnp.minimum(a + 2, nt - 1), 1)
            late(a + 1, 0, False)
            return carry

        lax.fori_loop(0, n_rest // 2, pair, 0)

        @pl.when(n_rest % 2 == 1)
        def _():
            late(nt - 1, 1, False)

        dk = dk_sc[0] + dk_sc[1]
        dkv_ref[:, 0:NOPE] = dk[:, 0:NOPE].astype(BF16)
        dkv_ref[:, NOPE:] = (dv_sc[0] + dv_sc[1]).astype(BF16)
        dkr_ref[0] = dk[:, NOPE:]

        @pl.when(j == nt - 1)
        def _():
            for c in range(nt):
                dq_ref[0, c * tt:(c + 1) * tt, :] = dqt_sc[c].T

    head = lambda h, j: (h, 0, 0)
    stat = pl.BlockSpec((1, T // ATT_CH, 1, ATT_CH), lambda h, j: (h, 0, 0, 0))
    return pl.pallas_call(
        body, name=name,
        out_shape=(jax.ShapeDtypeStruct((H, T, QC), F32), jax.ShapeDtypeStruct((T, 2 * D_MLA), BF16),
                   jax.ShapeDtypeStruct((H, T, 128), F32)),
        grid=(H, nt),
        in_specs=[pl.BlockSpec((1, T, QC), head),
                  pl.BlockSpec((1, tt, QC), lambda h, j: (h, j, 0)),
                  pl.BlockSpec((tt, V_DIM), lambda h, j: (j, 2 * h + 1)),
                  pl.BlockSpec((T, V_DIM), lambda h, j: (0, h)),
                  stat, stat],
        out_specs=(pl.BlockSpec((1, T, QC), head),
                   pl.BlockSpec((tt, 256), lambda h, j: (j, h)),
                   pl.BlockSpec((1, tt, 128), lambda h, j: (h, j, 0))),
        scratch_shapes=[pltpu.VMEM((nt, QC, tt), F32), pltpu.VMEM((2, tt, QC), F32), pltpu.VMEM((2, tt, V_DIM), F32),
                        pltpu.VMEM((2, tt, tt), F32), pltpu.VMEM((2, tt, tt), F32)],
        compiler_params=_params(("parallel", "arbitrary")),
    )(qc, kc, kv, do, lse, delta)


def _adamw(lands, w, m, v, name, rows, cols=None):
    L, R, C = w.shape
    cols = C if cols is None else cols
    assert R % rows == 0 and C % cols == 0 and len(lands) == L
    nc = C // cols
    nb = (R // rows) * nc
    c1 = 1.0 - ADAM_B1 ** ADAM_STEP
    c2 = 1.0 - ADAM_B2 ** ADAM_STEP

    def body(*refs):
        land_refs = refs[:L]
        w_ref, m_ref, v_ref, g_ref, d_ref, nm_ref, nv_ref, g_sc = refs[L:]
        for ll in range(L):
            @pl.when(pl.program_id(0) == ll)
            def _(land_ref=land_refs[ll]):
                g = land_ref[0].astype(F32)
                for s in range(1, N_DEV):
                    g = g + land_ref[s].astype(F32)
                g_sc[...] = g

        g = g_sc[...]
        nm = ADAM_B1 * m_ref[0] + (1.0 - ADAM_B1) * g
        nv = ADAM_B2 * v_ref[0] + (1.0 - ADAM_B2) * (g * g)
        g_ref[0] = g
        nm_ref[0] = nm
        nv_ref[0] = nv
        d_ref[0] = -ADAM_LR * ((nm / c1) / (jnp.sqrt(nv / c2) + ADAM_EPS) + ADAM_WD * w_ref[0])

    def land_spec(ll):
        def index(l, i):
            i = jnp.where(l < ll, 0, jnp.where(l > ll, nb - 1, i))
            return (0, i // nc, i % nc)
        return pl.BlockSpec((N_DEV, rows, cols), index)

    blk = pl.BlockSpec((1, rows, cols), lambda l, i: (l, i // nc, i % nc))
    out = jax.ShapeDtypeStruct((L, R, C), F32)
    return pl.pallas_call(
        body, name=name, out_shape=(out, out, out, out), grid=(L, nb),
        in_specs=[land_spec(ll) for ll in range(L)] + [blk, blk, blk],
        out_specs=(blk, blk, blk, blk),
        scratch_shapes=[pltpu.VMEM((rows, cols), F32)],
        compiler_params=_params(("arbitrary", "arbitrary")),
    )(*lands, w, m, v)


def _mesh_pos():
    return lax.axis_index("x"), lax.axis_index("y"), lax.axis_index("c")


def _all_gather(arrays, name):
    n = len(arrays)

    def body(*refs):
        ins, outs = refs[:n], refs[n:2 * n]
        send_sems, recv_sems, local_sems = refs[2 * n:]
        x, y, c = _mesh_pos()
        me, sibling = (x, y, c), (x, y, 1 - c)
        chips = [(1 - x, y), (x, 1 - y), (1 - x, 1 - y)]

        def slot(a, pos):
            px, py, pc = pos
            return outs[a].at[4 * px + 2 * py + pc]

        def copy(a, k, block, to, src=None):
            return pltpu.make_async_remote_copy(
                src_ref=slot(a, block) if src is None else src, dst_ref=slot(a, block),
                send_sem=send_sems.at[a * 7 + k], recv_sem=recv_sems.at[a * 7 + k],
                device_id=to, device_id_type=MESH_ID)

        mine, first, passed = [], [], []
        for a in range(n):
            cp = pltpu.make_async_copy(ins[a], slot(a, me), local_sems.at[a])
            cp.start()
            mine.append(cp)
            cps = [copy(a, 0, me, sibling, src=ins[a])]
            cps += [copy(a, 1 + j, me, (*chip, c), src=ins[a]) for j, chip in enumerate(chips)]
            for cp in cps:
                cp.start()
            first += cps
        for j, chip in enumerate(chips):
            for a in range(n):
                copy(a, 1 + j, (*chip, c), me).wait_recv()
                cp = copy(a, 4 + j, (*chip, c), sibling)
                cp.start()
                passed.append(cp)
        for a in range(n):
            copy(a, 0, sibling, me).wait_recv()
            for j, chip in enumerate(chips):
                copy(a, 4 + j, (*chip, 1 - c), me).wait_recv()
        for cp in first + passed:
            cp.wait_send()
        for cp in mine:
            cp.wait()

    hbm = pl.BlockSpec(memory_space=pltpu.HBM)
    return pl.pallas_call(
        body, name=name,
        out_shape=tuple(jax.ShapeDtypeStruct((N_DEV,) + a.shape, a.dtype) for a in arrays),
        in_specs=[hbm] * n, out_specs=tuple([hbm] * n),
        scratch_shapes=[pltpu.SemaphoreType.DMA((7 * n,)), pltpu.SemaphoreType.DMA((7 * n,)),
                        pltpu.SemaphoreType.DMA((n,))],
    )(*arrays)


def _exchange(arrays, name):
    n = len(arrays)

    def body(*refs):
        ins, outs = refs[:n], refs[n:2 * n]
        send_sems, recv_sems, local_sems = refs[2 * n:]
        x, y, c = _mesh_pos()
        my_idx = 4 * x + 2 * y + c
        copies, local = [], []
        for a in range(n):
            cp = pltpu.make_async_copy(ins[a].at[my_idx], outs[a].at[my_idx], local_sems.at[a])
            cp.start()
            local.append(cp)
            for k in range(1, N_DEV):
                px = 1 - x if k & 4 else x
                py = 1 - y if k & 2 else y
                pc = 1 - c if k & 1 else c
                cp = pltpu.make_async_remote_copy(
                    src_ref=ins[a].at[4 * px + 2 * py + pc], dst_ref=outs[a].at[my_idx],
                    send_sem=send_sems.at[a * 7 + k - 1], recv_sem=recv_sems.at[a * 7 + k - 1],
                    device_id=(px, py, pc), device_id_type=MESH_ID)
                cp.start()
                copies.append(cp)
        for cp in copies:
            cp.wait()
        for cp in local:
            cp.wait()

    hbm = pl.BlockSpec(memory_space=pltpu.HBM)
    return pl.pallas_call(
        body, name=name,
        out_shape=tuple(jax.ShapeDtypeStruct(a.shape, a.dtype) for a in arrays),
        in_specs=[hbm] * n, out_specs=tuple([hbm] * n),
        scratch_shapes=[pltpu.SemaphoreType.DMA((7 * n,)), pltpu.SemaphoreType.DMA((7 * n,)),
                        pltpu.SemaphoreType.DMA((n,))],
    )(*arrays)


_HBM = pl.BlockSpec(memory_space=pltpu.HBM)
_SEM = pl.BlockSpec(memory_space=pltpu.SEMAPHORE)
_EFFECT = pltpu.SideEffectType.DATAFLOW_SIDE_EFFECTING
N_PEERS = N_DEV - 1


def _peer(k):
    x, y, c = _mesh_pos()
    return (1 - x if k & 4 else x, 1 - y if k & 2 else y, 1 - c if k & 1 else c)


def _split_start(srcs, scatter, after, name):
    n = len(srcs)
    zones = [jax.ShapeDtypeStruct(s.shape if scatter else ((N_DEV,) + s.shape), s.dtype) for s in srcs]

    def body(*refs):
        src, zone = refs[:n], refs[n:2 * n]
        outs = refs[2 * n + 1:]
        send, recv, token = outs[:n], outs[n:2 * n], outs[4 * n]
        x, y, c = _mesh_pos()
        my_idx = 4 * x + 2 * y + c
        for a in range(n):
            pltpu.make_async_copy(src[a].at[my_idx] if scatter else src[a],
                                  zone[a].at[N_PEERS] if scatter else zone[a].at[my_idx], recv[a]).start()
            for k in range(1, N_DEV):
                px, py, pc = _peer(k)
                pltpu.make_async_remote_copy(
                    src_ref=src[a].at[4 * px + 2 * py + pc] if scatter else src[a],
                    dst_ref=zone[a].at[k - 1] if scatter else zone[a].at[my_idx],
                    send_sem=send[a], recv_sem=recv[a], device_id=(px, py, pc), device_id_type=MESH_ID).start()
        token[...] = jnp.zeros_like(token)

    hbm = lambda a: pltpu.with_memory_space_constraint(a, pltpu.HBM)
    outs = pl.pallas_call(
        body, name=name,
        out_shape=tuple([pltpu.SemaphoreType.DMA(())] * (2 * n)
                        + [pltpu.HBM(s.shape, s.dtype) for s in srcs]
                        + [pltpu.HBM(z.shape, z.dtype) for z in zones]
                        + [jax.ShapeDtypeStruct((8, 128), F32)]),
        in_specs=[_HBM] * (2 * n) + [pl.BlockSpec(memory_space=pl.ANY)],
        out_specs=tuple([_SEM] * (2 * n) + [_HBM] * (2 * n) + [pl.BlockSpec(memory_space=pltpu.VMEM)]),
        input_output_aliases={**{a: 2 * n + a for a in range(n)}, **{n + a: 3 * n + a for a in range(n)}},
        compiler_params=pltpu.CompilerParams(has_side_effects=_EFFECT),
    )(*[hbm(s) for s in srcs], *[hbm(lax.empty(z.shape, z.dtype)) for z in zones], after)
    return outs[:n], outs[n:2 * n], outs[2 * n:3 * n], outs[3 * n:4 * n], outs[4 * n]


def _split_wait(send, recv, srcs, zones, after, name):
    n = len(srcs)

    def body(*refs):
        zone = refs[n:2 * n]
        send_sems, recv_sems = refs[2 * n:3 * n], refs[3 * n:4 * n]
        x, y, c = _mesh_pos()
        for a in range(n):
            seven = zone[a].at[pl.ds(0, N_PEERS)]
            pltpu.make_async_remote_copy(src_ref=seven, dst_ref=seven, send_sem=send_sems[a], recv_sem=recv_sems[a],
                                         device_id=(x, y, 1 - c), device_id_type=MESH_ID).wait_send()
            pltpu.make_async_remote_copy(src_ref=zone[a], dst_ref=zone[a], send_sem=send_sems[a],
                                         recv_sem=recv_sems[a], device_id=(x, y, 1 - c),
                                         device_id_type=MESH_ID).wait_recv()

    outs = pl.pallas_call(
        body, name=name,
        out_shape=tuple([pltpu.HBM(s.shape, s.dtype) for s in srcs] + [pltpu.HBM(z.shape, z.dtype) for z in zones]),
        in_specs=[_HBM] * (2 * n) + [_SEM] * (2 * n) + [pl.BlockSpec(memory_space=pl.ANY)],
        out_specs=tuple([_HBM] * (2 * n)),
        input_output_aliases={a: a for a in range(2 * n)},
        compiler_params=pltpu.CompilerParams(has_side_effects=_EFFECT),
    )(*srcs, *zones, *send, *recv, after)
    return outs[:n], outs[n:]


def _cat_blocks(g, axis):
    return jnp.concatenate([g[d] for d in range(N_DEV)], axis=axis)


N_LATENT = Q_LORA + KV_LORA + ROPE
W_SHARD = D_IN_PROJ // N_DEV


def _ref_cols(lo, hi):
    out = []
    if lo < N_LATENT:
        out.append((N_GATED + lo, N_GATED + min(hi, N_LATENT)))
    if hi > N_LATENT:
        out.append((max(lo, N_LATENT) - N_LATENT, hi - N_LATENT))
    return out


def _permute_w_in_t(blocks):
    pieces = []
    for lo, hi in ((N_LATENT, D_IN_PROJ), (0, N_LATENT)):
        for d in range(N_DEV):
            a, b = max(lo, d * W_SHARD), min(hi, (d + 1) * W_SHARD)
            if a < b:
                pieces.append(blocks[d][a - d * W_SHARD:b - d * W_SHARD])
    pieces.append(jnp.zeros((NPP - D_IN_PROJ, blocks.shape[2]), blocks.dtype))
    return jnp.concatenate(pieces, axis=0)


def _split_w_in_t(w):
    slabs = []
    for d in range(N_DEV):
        parts = [w[a:b] for a, b in _ref_cols(d * W_SHARD, (d + 1) * W_SHARD)]
        slabs.append(parts[0] if len(parts) == 1 else jnp.concatenate(parts, axis=0))
    return jnp.stack(slabs)


def _permute_w_uq(w):
    w3 = w.reshape(w.shape[0], N_HEADS, NOPE + ROPE)
    return jnp.concatenate([w3[:, :, :NOPE].reshape(w.shape[0], -1), w3[:, :, NOPE:].reshape(w.shape[0], -1)], axis=1)


def _unpermute_w_uq(w):
    nope = w[:, :N_HEADS * NOPE].reshape(w.shape[0], N_HEADS, NOPE)
    rope = w[:, N_HEADS * NOPE:].reshape(w.shape[0], N_HEADS, ROPE)
    return jnp.concatenate([nope, rope], axis=2).reshape(w.shape[0], -1)


_SMALL = (("emb_ln_g", 16), ("emb_ln_b", 16), ("q_norm_g", 8), ("kv_norm_g", 8), ("w_pool", 1024),
          ("pool_scale", 8), ("b_out", 32), ("ln_g", 32), ("ln_b", 32))
SMALL_ROWS = sum(r for _, r in _SMALL)


def _pack_small(d):
    parts = []
    for name, rows in _SMALL:
        flat = d[name].reshape(-1)
        flat = jnp.pad(flat, (0, rows * 128 - flat.shape[0]))
        parts.append(flat.reshape(rows, 128))
    return jnp.concatenate(parts, axis=0)


def _unpack_small(packed, shapes):
    out, r0 = {}, 0
    for name, rows in _SMALL:
        size = 1
        for s in shapes[name]:
            size *= s
        out[name] = packed[r0:r0 + rows].reshape(-1)[:size].reshape(shapes[name])
        r0 += rows
    return out


def _rope_tables(positions):
    half = ROPE // 2
    inv_freq = ROPE_THETA ** (-jnp.arange(half, dtype=F32) / half)
    ang = positions.astype(F32)[:, None] * inv_freq
    cos, sin = jnp.cos(ang), jnp.sin(ang)
    return jnp.concatenate([cos, cos, cos, cos], axis=1), jnp.concatenate([-sin, sin, -sin, sin], axis=1)


def _local_step(x, positions, target, emb_g, emb_b, layer_weights, layer_weights_rest, on_sharded_grads,
                first_after=None):
    cos_t, sin_t = _rope_tables(positions)
    h, hb = _ln_fwd(x, emb_g, emb_b, "emb_ln_fwd", after=first_after)
    saved = []
    for l in range(DEPTH):
        W = layer_weights(l, h)
        proj = _mm(hb, W["w_in_t"], "nt", F32, "proj_fwd")
        qn, kvn, pooled, cv, mix = _mix_fwd(proj, W["q_norm_g"], W["kv_norm_g"], W["w_pool"], W["pool_scale"],
                                            W["conv_w"], "mix_fwd")
        rest, token = layer_weights_rest(l, proj)
        W = {**W, **rest}
        q = _mm(qn, W["w_uq"], "nn", F32, "q_up_fwd", after=token)
        kv = _mm(kvn, W["w_ukv"], "nn", BF16, "kv_up_fwd")
        qc, kc = _rope_fwd(q, kv, proj, cos_t, sin_t, "rope_fwd")
        o, mix, lse = _flash_fwd(qc, kc, kv, proj, mix, "flash_fwd")
        z = _mm(mix, W["w_out"], "nn", F32, "out_fwd", res=h, bias=W["b_out"], alpha=ALPHA)
        saved.append((W, hb, proj, qn, kvn, pooled, cv, kv, qc, kc, o, lse, mix, z))
        h, hb = _ln_fwd(z, W["ln_g"], W["ln_b"], "ln_fwd")
    sq, dh = _loss_head(h, target, "loss_head")

    grads = {k: [None] * DEPTH for k in ("q_norm_g", "kv_norm_g", "w_pool", "pool_scale", "conv_w", "b_out", "ln_g",
                                         "ln_b")}
    for l in reversed(range(DEPTH)):
        W, hb_in, proj, qn, kvn, pooled, cv, kv, qc, kc, o, lse, mix, z = saved[l]
        sharded = {}
        dz, dzb, grads["ln_g"][l], grads["ln_b"][l], grads["b_out"][l] = _ln_bwd(dh, z, W["ln_g"], "ln_bwd")
        dmix = _mm(dzb, W["w_out"], "nt", F32, "out_bwd_x")
        sharded["w_out"] = _mm(mix, dzb, "tn", GRAD_XFER, "out_bwd_w", tk=4096)
        do, delta, dproj, grads["w_pool"][l], grads["pool_scale"][l], grads["conv_w"][l] = _mix_bwd(
            dmix, proj, o, pooled, cv, W["w_pool"], W["pool_scale"], W["conv_w"], "mix_bwd")
        dqc, dkv, dkr = _flash_bwd(qc, kc, kv, do, lse, delta, "flash_bwd")
        dq, dkrope = _rope_bwd(dqc, dkr, cos_t, sin_t, "rope_bwd")
        dqn = _mm(dq, W["w_uq"], "nt", F32, "q_up_bwd_x")
        sharded["w_uq"] = _mm(qn, dq, "tn", GRAD_XFER, "q_up_bwd_w")
        dkvn = _mm(dkv, W["w_ukv"], "nt", F32, "kv_up_bwd_x")
        sharded["w_ukv"] = _mm(kvn, dkv, "tn", GRAD_XFER, "kv_up_bwd_w")
        token = on_sharded_grads(l, sharded)
        dproj, grads["q_norm_g"][l], grads["kv_norm_g"][l] = _rms_bwd(
            proj, dqn, dkvn, dkrope, dproj, W["q_norm_g"], W["kv_norm_g"], "rms_bwd")
        d_w_in_t = _mm(dproj, hb_in, "tn", GRAD_XFER, "proj_bwd_w", tk=4096, after=token)
        token = on_sharded_grads(l, {"w_in": d_w_in_t})
        dh = _mm(dproj, W["w_in_t"], "nn", F32, "proj_bwd_x", res=dz, alpha=ALPHA, tk=2560, after=token)
    grad_x, _, grads["emb_ln_g"], grads["emb_ln_b"], _ = _ln_bwd(dh, x, emb_g, "emb_ln_bwd")
    return sq, grad_x, grads


def kernel(x, positions, emb_ln_g, emb_ln_b, w_in, q_norm_g, kv_norm_g, w_uq, w_ukv, w_pool, pool_scale, conv_w, w_out, b_out, ln_g, ln_b, loss_target, m_emb_ln_g, m_emb_ln_b, m_w_in, m_q_norm_g, m_kv_norm_g, m_w_uq, m_w_ukv, m_w_pool, m_pool_scale, m_conv_w, m_w_out, m_b_out, m_ln_g, m_ln_b, v_emb_ln_g, v_emb_ln_b, v_w_in, v_q_norm_g, v_kv_norm_g, v_w_uq, v_w_ukv, v_w_pool, v_pool_scale, v_conv_w, v_w_out, v_b_out, v_ln_g, v_ln_b):
    weights = dict(emb_ln_g=emb_ln_g, emb_ln_b=emb_ln_b, w_in=w_in, q_norm_g=q_norm_g, kv_norm_g=kv_norm_g,
                   w_uq=w_uq, w_ukv=w_ukv, w_pool=w_pool, pool_scale=pool_scale, conv_w=conv_w, w_out=w_out,
                   b_out=b_out, ln_g=ln_g, ln_b=ln_b)
    mom1 = dict(emb_ln_g=m_emb_ln_g, emb_ln_b=m_emb_ln_b, w_in=m_w_in, q_norm_g=m_q_norm_g, kv_norm_g=m_kv_norm_g,
                w_uq=m_w_uq, w_ukv=m_w_ukv, w_pool=m_w_pool, pool_scale=m_pool_scale, conv_w=m_conv_w,
                w_out=m_w_out, b_out=m_b_out, ln_g=m_ln_g, ln_b=m_ln_b)
    mom2 = dict(emb_ln_g=v_emb_ln_g, emb_ln_b=v_emb_ln_b, w_in=v_w_in, q_norm_g=v_q_norm_g, kv_norm_g=v_kv_norm_g,
                w_uq=v_w_uq, w_ukv=v_w_ukv, w_pool=v_w_pool, pool_scale=v_pool_scale, conv_w=v_conv_w,
                w_out=v_w_out, b_out=v_b_out, ln_g=v_ln_g, ln_b=v_ln_b)

    big = ("w_in", "w_uq", "w_ukv", "w_out")

    conv_pad = jnp.zeros((8, 128), F32).at[0:DEPTH * CONV_WIDTH, 0:64].set(conv_w.reshape(DEPTH * CONV_WIDTH, 64))
    t12 = lambda a: jnp.swapaxes(a, 1, 2)
    shard = lambda k, l: (t12(weights[k])[l] if k == "w_in" else weights[k][l]).astype(BF16)
    w_in0, conv_all = _all_gather([shard("w_in", 0), conv_pad], "w_in0_all_gather")
    rest0 = _split_start([shard(k, 0) for k in big[1:]], False, w_in0, "weights0_rest_start")
    conv_full = _cat_blocks(conv_all[:, 0:DEPTH * CONV_WIDTH, 0:64], 1).reshape(DEPTH, CONV_WIDTH, D_CONV)
    conv_full = jnp.pad(conv_full, ((0, 0), (0, 8 - CONV_WIDTH), (0, 0)))
    fetched = {}

    def layer_weights(l, ready):
        if l == 0:
            w_in_blocks = w_in0
        else:
            fetched[1] = _split_wait(*fetched["w1"][:4], ready, "weights1_wait")[1]
            w_in_blocks = fetched[1][0]
        return dict(
            w_in_t=_permute_w_in_t(w_in_blocks), conv_w=conv_full[l],
            q_norm_g=q_norm_g[l].reshape(1, -1), kv_norm_g=kv_norm_g[l].reshape(1, -1),
            w_pool=w_pool[l].astype(BF16), pool_scale=pool_scale[l].reshape(1, -1), b_out=b_out[l].reshape(1, -1),
            ln_g=ln_g[l].reshape(1, -1), ln_b=ln_b[l].reshape(1, -1))

    def layer_weights_rest(l, ready):
        token = None
        if l == 0:
            blocks = _split_wait(*rest0[:4], ready, "weights0_rest_wait")[1]
            fetched["w1"] = _split_start([shard(k, 1) for k in big], False, blocks[0], "weights1_start")
            token = fetched["w1"][4]
        else:
            blocks = fetched[1][1:]
        return dict(w_uq=_permute_w_uq(_cat_blocks(blocks[0], 1)), w_ukv=_cat_blocks(blocks[1], 1),
                    w_out=_cat_blocks(blocks[2], 0)), token

    by_dest = dict(
        w_in=_split_w_in_t,
        w_uq=lambda g: _unpermute_w_uq(g).reshape(Q_LORA, N_DEV, -1).transpose(1, 0, 2),
        w_ukv=lambda g: g.reshape(KV_LORA, N_DEV, -1).transpose(1, 0, 2),
        w_out=lambda g: g.reshape(N_DEV, -1, D_MODEL))
    in_flight = []

    def on_sharded_grads(l, g):
        names = [k for k in big if k in g]
        srcs = [by_dest[k](g[k]) for k in names]
        started = _split_start(srcs, True, srcs[0], "grads%d_%s_start" % (l, names[0]))
        in_flight.append((l, names, started[:4]))
        return started[4]

    sq, grad_x, G = _local_step(x[0], positions[0], loss_target[0], emb_ln_g.reshape(1, -1),
                                emb_ln_b.reshape(1, -1), layer_weights, layer_weights_rest, on_sharded_grads,
                                first_after=rest0[4])
    loss = lax.psum(sq[0, 0] * (0.5 / D_MODEL), ("x", "y", "c"))

    res = {}
    landed = {}
    for l, names, started in in_flight:
        zones = _split_wait(*started, grad_x, "grads%d_%s_wait" % (l, names[0]))[1]
        for k, zone in zip(names, zones):
            landed[k, l] = zone
    res["w_in"] = tuple(t12(o) for o in _adamw(
        [landed["w_in", l] for l in range(DEPTH)], t12(w_in), t12(m_w_in), t12(v_w_in), "adamw_w_in", W_SHARD, 512))
    for name, rows in (("w_uq", 256), ("w_ukv", 256), ("w_out", 128)):
        res[name] = _adamw([landed[name, l] for l in range(DEPTH)], weights[name], mom1[name], mom2[name],
                           "adamw_" + name, rows)

    d_conv = jnp.stack([G["conv_w"][l][0:CONV_WIDTH] for l in range(DEPTH)])
    d_conv = d_conv.reshape(DEPTH * CONV_WIDTH, N_DEV, 64).transpose(1, 0, 2)
    d_conv = jnp.zeros((N_DEV, 8, 128), F32).at[:, 0:DEPTH * CONV_WIDTH, 0:64].set(d_conv)
    small = dict(emb_ln_g=G["emb_ln_g"], emb_ln_b=G["emb_ln_b"])
    for k in ("q_norm_g", "kv_norm_g", "w_pool", "pool_scale", "b_out", "ln_g", "ln_b"):
        small[k] = jnp.stack(G[k])
    d_small = jnp.broadcast_to(_pack_small(small)[None], (N_DEV, SMALL_ROWS, 128))
    l_conv, l_small = _exchange([d_conv, d_small], "small_gradient_exchange")
    conv_shard = lambda a: jnp.zeros((8, 128), F32).at[0:DEPTH * CONV_WIDTH, 0:64].set(a.reshape(-1, 64))
    conv_res = _adamw([l_conv], conv_shard(conv_w)[None], conv_shard(m_conv_w)[None], conv_shard(v_conv_w)[None],
                      "adamw_conv_w", 8)
    res["conv_w"] = tuple(o[0, 0:DEPTH * CONV_WIDTH, 0:64].reshape(DEPTH, CONV_WIDTH, 64) for o in conv_res)
    small_res = _adamw([l_small], _pack_small(weights)[None], _pack_small(mom1)[None], _pack_small(mom2)[None],
                       "adamw_small", 392)
    shapes = {k: weights[k].shape for k, _ in _SMALL}
    unpacked = [_unpack_small(o[0], shapes) for o in small_res]
    for k, _ in _SMALL:
        res[k] = tuple(u[k] for u in unpacked)

    order = ("emb_ln_g", "emb_ln_b", "w_in", "q_norm_g", "kv_norm_g", "w_uq", "w_ukv", "w_pool", "pool_scale",
             "conv_w", "w_out", "b_out", "ln_g", "ln_b")
    return (loss, grad_x[None], *[res[k][0] for k in order], *[res[k][1] for k in order],
            *[res[k][2] for k in order], *[res[k][3] for k in order])
```

```python
import jax
import jax.numpy as jnp
from jax import lax
from jax.experimental import pallas as pl
from jax.experimental.pallas import tpu as pltpu

F32 = jnp.float32
BF16 = jnp.bfloat16

N_DEV = 8
D_MODEL = 2048
DEPTH = 2
N_HEADS = 8
NOPE = 128
ROPE = 64
V_DIM = 128
Q_LORA = 512
KV_LORA = 256
D_MLA = N_HEADS * V_DIM
D_POOL = 512
D_CONV = 512
POOL_WINDOWS = (2, 4, 8, 16)
POOL_GROUP = 128
CONV_WIDTH = 3
D_MIX = D_MLA + D_POOL + D_CONV
D_IN_PROJ = 4928
ROPE_THETA = 10000.0
LN_EPS = 1e-5
RMS_EPS = 1e-6
ALPHA = (2 * DEPTH) ** 0.25
ATTN_SCALE = (NOPE + ROPE) ** -0.5
ADAM_LR = 0.001
ADAM_B1 = 0.9
ADAM_B2 = 0.999
ADAM_EPS = 1e-08
ADAM_WD = 0.01
ADAM_STEP = 10

O_GMLA, O_PIN, O_GPOOL, O_CH, O_CB, O_CC, O_GCONV, O_QLAT, O_KVLAT, O_KROPE = (
    0, 1024, 1536, 2048, 2560, 3072, 3584, 4096, 4608, 4864)
NPP = 5120
N_GATED = O_QLAT
QC = NOPE + 2 * ROPE
HALO = 16
ATT_TILE = 512
ATT_CH = 256
LOG2E = 1.4426950408889634
EXP2_SCALE = ATTN_SCALE * LOG2E

GRAD_XFER = BF16
VMEM_LIMIT = 48 * 1024 * 1024
MESH_ID = pl.DeviceIdType.MESH


def _params(sem=None):
    return pltpu.CompilerParams(dimension_semantics=sem, vmem_limit_bytes=VMEM_LIMIT)


def _sigmoid(x):
    return 1.0 / (1.0 + jnp.exp(-x))


def _tile(dim, target):
    if dim <= target:
        return dim
    t = target - target % 128
    while dim % t:
        t -= 128
    return t


_DIMS = {"nn": (((1,), (0,)), ((), ())), "nt": (((1,), (1,)), ((), ())), "tn": (((0,), (0,)), ((), ()))}


def _mm(a, b, mode, out_dtype, name, res=None, bias=None, alpha=1.0, tm=1024, tn=1024, tk=2048, after=None):
    if mode == "nn":
        (M, K), (K2, N) = a.shape, b.shape
    elif mode == "nt":
        (M, K), (N, K2) = a.shape, b.shape
    else:
        (K, M), (K2, N) = a.shape, b.shape
    assert K == K2
    tm, tn, tk = _tile(M, tm), _tile(N, tn), _tile(K, tk)
    nk = K // tk
    has_res, has_bias = res is not None, bias is not None

    def body(*refs):
        a_ref, b_ref = refs[0], refs[1]
        pos = 2
        res_ref = bias_ref = None
        if has_res:
            res_ref = refs[pos]
            pos += 1
        if has_bias:
            bias_ref = refs[pos]
            pos += 1
        def finish(r, o_ref):
            if has_bias:
                r = r + bias_ref[...]
            if has_res:
                r = alpha * res_ref[...] + r
            o_ref[...] = r.astype(out_dtype)

        part = lax.dot_general(a_ref[...].astype(BF16), b_ref[...].astype(BF16), _DIMS[mode],
                               preferred_element_type=F32)
        if nk == 1:
            finish(part, refs[-1])
            return
        o_ref, acc_ref = refs[-2], refs[-1]
        k = pl.program_id(2)

        @pl.when(k == 0)
        def _():
            acc_ref[...] = part

        @pl.when(jnp.logical_and(k > 0, k < nk - 1))
        def _():
            acc_ref[...] += part

        @pl.when(k == nk - 1)
        def _():
            finish(acc_ref[...] + part, o_ref)

    if mode == "nn":
        in_specs = [pl.BlockSpec((tm, tk), lambda i, j, k: (i, k)), pl.BlockSpec((tk, tn), lambda i, j, k: (k, j))]
    elif mode == "nt":
        in_specs = [pl.BlockSpec((tm, tk), lambda i, j, k: (i, k)), pl.BlockSpec((tn, tk), lambda i, j, k: (j, k))]
    else:
        in_specs = [pl.BlockSpec((tk, tm), lambda i, j, k: (k, i)), pl.BlockSpec((tk, tn), lambda i, j, k: (k, j))]
    args = [a, b]
    if has_res:
        in_specs.append(pl.BlockSpec((tm, tn), lambda i, j, k: (i, j)))
        args.append(res)
    if has_bias:
        in_specs.append(pl.BlockSpec((1, tn), lambda i, j, k: (0, j)))
        args.append(bias)
    if after is not None:
        in_specs.append(pl.BlockSpec((8, 128), lambda i, j, k: (0, 0)))
        args.append(after)
    return pl.pallas_call(
        body, name=name,
        out_shape=jax.ShapeDtypeStruct((M, N), out_dtype),
        grid=(M // tm, N // tn, nk),
        in_specs=in_specs,
        out_specs=pl.BlockSpec((tm, tn), lambda i, j, k: (i, j)),
        scratch_shapes=[pltpu.VMEM((tm, tn), F32)] if nk > 1 else [],
        compiler_params=_params(("parallel", "parallel", "arbitrary")),
    )(*args)


def _ln_fwd(z, g, b, name, tq=256, after=None):
    T, D = z.shape

    def body(z_ref, g_ref, b_ref, *rest):
        y_ref, yb_ref = rest[-2:]
        zv = z_ref[...]
        mu = jnp.mean(zv, axis=1, keepdims=True)
        zc = zv - mu
        var = jnp.mean(zc * zc, axis=1, keepdims=True)
        y = zc * lax.rsqrt(var + LN_EPS) * g_ref[...] + b_ref[...]
        y_ref[...] = y
        yb_ref[...] = y.astype(BF16)

    row = pl.BlockSpec((tq, D), lambda i: (i, 0))
    vec = pl.BlockSpec((1, D), lambda i: (0, 0))
    return pl.pallas_call(
        body, name=name,
        out_shape=(jax.ShapeDtypeStruct((T, D), F32), jax.ShapeDtypeStruct((T, D), BF16)),
        grid=(T // tq,),
        in_specs=[row, vec, vec] + ([pl.BlockSpec((8, 128), lambda i: (0, 0))] if after is not None else []),
        out_specs=(row, row),
        compiler_params=_params(("parallel",)),
    )(z, g, b, *([after] if after is not None else []))


def _ln_bwd(dy, z, g, name, tq=256, target=None, for_matmul=True):
    T, D = z.shape
    with_loss = target is not None

    def body(*refs):
        dy_ref, z_ref, g_ref = refs[:3]
        outs = list(refs[4 if with_loss else 3:])
        dz_ref = outs.pop(0)
        dzb_ref, ds_ref = (outs.pop(0), outs.pop(0)) if for_matmul else (None, None)
        dg_ref, db_ref = outs.pop(0), outs.pop(0)
        sq_ref = outs.pop(0) if with_loss else None

        @pl.when(pl.program_id(0) == 0)
        def _():
            for ref in (dg_ref, db_ref, ds_ref, sq_ref):
                if ref is not None:
                    ref[...] = jnp.zeros_like(ref)

        zv, dyv = z_ref[...], dy_ref[...]
        if with_loss:
            err = dyv - refs[3][...]
            sq_ref[...] += jnp.sum(err * err)
            dyv = err * (1.0 / D)
        mu = jnp.mean(zv, axis=1, keepdims=True)
        zc = zv - mu
        var = jnp.mean(zc * zc, axis=1, keepdims=True)
        rstd = lax.rsqrt(var + LN_EPS)
        xh = zc * rstd
        u = dyv * g_ref[...]
        dz = rstd * (u - jnp.mean(u, axis=1, keepdims=True) - xh * jnp.mean(u * xh, axis=1, keepdims=True))
        dz_ref[...] = dz
        dg_ref[...] += jnp.sum(dyv * xh, axis=0, keepdims=True)
        db_ref[...] += jnp.sum(dyv, axis=0, keepdims=True)
        if for_matmul:
            dzb_ref[...] = dz.astype(BF16)
            ds_ref[...] += jnp.sum(dz, axis=0, keepdims=True)

    row = pl.BlockSpec((tq, D), lambda i: (i, 0))
    vec = pl.BlockSpec((1, D), lambda i: (0, 0))
    vshape = jax.ShapeDtypeStruct((1, D), F32)
    out_shape, out_specs = [jax.ShapeDtypeStruct((T, D), F32)], [row]
    if for_matmul:
        out_shape += [jax.ShapeDtypeStruct((T, D), BF16), vshape]
        out_specs += [row, vec]
    out_shape += [vshape, vshape]
    out_specs += [vec, vec]
    if with_loss:
        out_shape.append(jax.ShapeDtypeStruct((8, 128), F32))
        out_specs.append(pl.BlockSpec((8, 128), lambda i: (0, 0)))
    return pl.pallas_call(
        body, name=name, out_shape=tuple(out_shape), grid=(T // tq,),
        in_specs=[row, row, vec] + ([row] if with_loss else []), out_specs=tuple(out_specs),
        compiler_params=_params(("arbitrary",)),
    )(dy, z, g, *([target] if with_loss else []))


def _pblock(tq, width, offset):
    assert offset % width == 0
    blk = offset // width
    return pl.BlockSpec((tq, width), lambda i: (i, blk))


def _mix_fwd(proj, q_g, kv_g, w_pool, pool_scale, conv_w, name, tq=256):
    T = proj.shape[0]

    def body(ql_ref, kvl_ref, pin_ref, gp_ref, ch_ref, cb_ref, cc_ref, gc_ref, qg_ref, kvg_ref, wp_ref, ps_ref,
             cw_ref, qn_ref, kvn_ref, pooled_ref, cv_ref, ypc_ref, extp, extu):
        i = pl.program_id(0)
        for x_ref, g_ref, o_ref in ((ql_ref, qg_ref, qn_ref), (kvl_ref, kvg_ref, kvn_ref)):
            x = x_ref[...]
            r = lax.rsqrt(jnp.mean(x * x, axis=1, keepdims=True) + RMS_EPS)
            o_ref[...] = (x * r * g_ref[...]).astype(BF16)

        @pl.when(i == 0)
        def _():
            extp[0:HALO, :] = jnp.zeros((HALO, D_POOL), F32)
            extu[0:HALO, :] = jnp.zeros((HALO, D_CONV), F32)

        @pl.when(i > 0)
        def _():
            extp[0:HALO, :] = extp[tq:tq + HALO, :]
            extu[0:HALO, :] = extu[tq:tq + HALO, :]

        pin = pin_ref[...]
        extp[HALO:, :] = pin
        u = cc_ref[...] * ch_ref[...]
        extu[HALO:, :] = u
        t1 = (i * tq + lax.broadcasted_iota(jnp.int32, (tq, 1), 0) + 1).astype(F32)
        for g, w in enumerate(POOL_WINDOWS):
            cols = slice(g * POOL_GROUP, (g + 1) * POOL_GROUP)
            s = extp[:, cols]
            k = 1
            while k < w:
                s = s + pltpu.roll(s, k, 0)
                k *= 2
            mean = s[HALO:, :] / jnp.minimum(t1, float(w))
            pooled = (mean - pin[:, cols]).astype(BF16)
            pooled_ref[:, cols] = pooled
            r = jnp.dot(pooled, wp_ref[g], preferred_element_type=F32)
            gp = gp_ref[:, cols]
            ypc_ref[:, cols] = (r * ps_ref[:, cols] * (gp * _sigmoid(gp))).astype(BF16)
        eu = extu[...]
        u1 = pltpu.roll(eu, 1, 0)[HALO:, :]
        u2 = pltpu.roll(eu, 2, 0)[HALO:, :]
        cv = cw_ref[0:1, :] * u2 + cw_ref[1:2, :] * u1 + cw_ref[2:3, :] * u
        cv_ref[...] = cv
        gc = gc_ref[...]
        ypc_ref[:, D_POOL:] = (cb_ref[...] * cv * (gc * _sigmoid(gc))).astype(BF16)

    full = lambda shape: pl.BlockSpec(shape, lambda i: (0,) * len(shape))
    row = lambda w: pl.BlockSpec((tq, w), lambda i: (i, 0))
    return pl.pallas_call(
        body, name=name,
        out_shape=(jax.ShapeDtypeStruct((T, Q_LORA), BF16), jax.ShapeDtypeStruct((T, KV_LORA), BF16),
                   jax.ShapeDtypeStruct((T, D_POOL), BF16), jax.ShapeDtypeStruct((T, D_CONV), F32),
                   jax.ShapeDtypeStruct((T, D_MIX), BF16)),
        grid=(T // tq,),
        in_specs=[_pblock(tq, Q_LORA, O_QLAT), _pblock(tq, KV_LORA, O_KVLAT), _pblock(tq, 512, O_PIN),
                  _pblock(tq, 512, O_GPOOL), _pblock(tq, 512, O_CH), _pblock(tq, 512, O_CB), _pblock(tq, 512, O_CC),
                  _pblock(tq, 512, O_GCONV), full((1, Q_LORA)), full((1, KV_LORA)), full((4, 128, 128)),
                  full((1, D_POOL)), full((8, D_CONV))],
        out_specs=(row(Q_LORA), row(KV_LORA), row(D_POOL), row(D_CONV),
                   pl.BlockSpec((tq, D_POOL + D_CONV), lambda i: (i, D_MLA // (D_POOL + D_CONV)))),
        scratch_shapes=[pltpu.VMEM((tq + HALO, D_POOL), F32), pltpu.VMEM((tq + HALO, D_CONV), F32)],
        compiler_params=_params(("arbitrary",)),
    )(proj, proj, proj, proj, proj, proj, proj, proj, q_g, kv_g, w_pool, pool_scale, conv_w)


def _mix_bwd(dmix, proj, o, pooled, cv, w_pool, pool_scale, conv_w, name, tq=ATT_CH):
    T = proj.shape[0]
    nt = T // tq
    n_ext = tq + HALO

    def body(dym_ref, dyp_ref, dyc_ref, gm_ref, gp_ref, ch_ref, cb_ref, cc_ref, gc_ref, o_ref, pooled_ref, cv_ref,
             wp_ref, ps_ref, cw_ref, do_ref, delta_ref, dg_ref, dwp_ref, dps_ref, dcw_ref, exte, extd):
        i = pl.program_id(0)
        tile = nt - 1 - i

        @pl.when(i == 0)
        def _():
            dwp_ref[...] = jnp.zeros_like(dwp_ref)
            dps_ref[...] = jnp.zeros_like(dps_ref)
            dcw_ref[...] = jnp.zeros_like(dcw_ref)
            exte[tq:, :] = jnp.zeros((HALO, D_POOL), F32)
            extd[tq:, :] = jnp.zeros((HALO, D_CONV), F32)

        @pl.when(i > 0)
        def _():
            exte[tq:, :] = exte[0:HALO, :]
            extd[tq:, :] = extd[0:HALO, :]

        gm = gm_ref[...]
        sig = _sigmoid(gm)
        dym = dym_ref[...]
        ov = o_ref[...]
        do = dym * (gm * sig)
        do_ref[...] = do.astype(BF16)
        prod = do * ov
        ones = jnp.ones((8, V_DIM), F32)
        for h in range(N_HEADS):
            rows = lax.dot_general(ones, prod[:, h * V_DIM:(h + 1) * V_DIM], _DIMS["nt"],
                                   precision=lax.Precision.HIGHEST, preferred_element_type=F32)
            delta_ref[h, 0] = rows[0:1, :]
        dg_ref[:, O_GMLA:O_PIN] = (dym * ov * (sig * (1.0 + gm * (1.0 - sig)))).astype(BF16)

        t1 = (tile * tq + lax.broadcasted_iota(jnp.int32, (tq, 1), 0) + 1).astype(F32)
        for g, w in enumerate(POOL_WINDOWS):
            cols = slice(g * POOL_GROUP, (g + 1) * POOL_GROUP)
            pg = pooled_ref[:, cols]
            r = jnp.dot(pg, wp_ref[g], preferred_element_type=F32)
            gp = gp_ref[:, cols]
            sg = _sigmoid(gp)
            sl = gp * sg
            dyg = dyp_ref[:, cols]
            ps = ps_ref[:, cols]
            dg_ref[:, O_GPOOL + g * POOL_GROUP:O_GPOOL + (g + 1) * POOL_GROUP] = (
                dyg * (r * ps) * (sg * (1.0 + gp * (1.0 - sg)))).astype(BF16)
            dps_ref[:, cols] += jnp.sum(dyg * r * sl, axis=0, keepdims=True)
            dr = (dyg * ps * sl).astype(BF16)
            dwp_ref[g] += lax.dot_general(pg, dr, _DIMS["tn"], preferred_element_type=F32)
            dpooled = lax.dot_general(dr, wp_ref[g], _DIMS["nt"], preferred_element_type=F32)
            exte[0:tq, cols] = dpooled / jnp.minimum(t1, float(w))
            s = exte[:, cols]
            k = 1
            while k < w:
                s = s + pltpu.roll(s, n_ext - k, 0)
                k *= 2
            dg_ref[:, O_PIN + g * POOL_GROUP:O_PIN + (g + 1) * POOL_GROUP] = (s[0:tq, :] - dpooled).astype(BF16)

        gc = gc_ref[...]
        sg = _sigmoid(gc)
        sl = gc * sg
        dyc = dyc_ref[...]
        cb, cc, ch, cvv = cb_ref[...], cc_ref[...], ch_ref[...], cv_ref[...]
        dcv = dyc * cb * sl
        dg_ref[:, O_GCONV:O_GCONV + D_CONV] = (dyc * (cb * cvv) * (sg * (1.0 + gc * (1.0 - sg)))).astype(BF16)
        dg_ref[:, O_CB:O_CB + D_CONV] = (dyc * cvv * sl).astype(BF16)
        extd[0:tq, :] = dcv
        ed = extd[...]
        d1 = pltpu.roll(ed, n_ext - 1, 0)[0:tq, :]
        d2 = pltpu.roll(ed, n_ext - 2, 0)[0:tq, :]
        du = cw_ref[2:3, :] * dcv + cw_ref[1:2, :] * d1 + cw_ref[0:1, :] * d2
        u = cc * ch
        dcw_ref[0:1, :] += jnp.sum(u * d2, axis=0, keepdims=True)
        dcw_ref[1:2, :] += jnp.sum(u * d1, axis=0, keepdims=True)
        dcw_ref[2:3, :] += jnp.sum(u * dcv, axis=0, keepdims=True)
        dg_ref[:, O_CH:O_CH + D_CONV] = (du * cc).astype(BF16)
        dg_ref[:, O_CC:O_CC + D_CONV] = (du * ch).astype(BF16)

    def rblock(width, offset):
        assert offset % width == 0
        blk = offset // width
        return pl.BlockSpec((tq, width), lambda i: (nt - 1 - i, blk))

    full = lambda shape: pl.BlockSpec(shape, lambda i: (0,) * len(shape))
    return pl.pallas_call(
        body, name=name,
        out_shape=(jax.ShapeDtypeStruct((T, D_MLA), BF16), jax.ShapeDtypeStruct((N_HEADS, nt, 1, tq), F32),
                   jax.ShapeDtypeStruct((T, NPP), BF16),
                   jax.ShapeDtypeStruct((4, 128, 128), F32), jax.ShapeDtypeStruct((1, D_POOL), F32),
                   jax.ShapeDtypeStruct((8, D_CONV), F32)),
        grid=(nt,),
        in_specs=[rblock(1024, 0), rblock(512, 1024), rblock(512, 1536),
                  rblock(1024, O_GMLA), rblock(512, O_GPOOL), rblock(512, O_CH), rblock(512, O_CB),
                  rblock(512, O_CC), rblock(512, O_GCONV), rblock(1024, 0), rblock(512, 0), rblock(512, 0),
                  full((4, 128, 128)), full((1, D_POOL)), full((8, D_CONV))],
        out_specs=(rblock(1024, 0), pl.BlockSpec((N_HEADS, 1, 1, tq), lambda i: (0, nt - 1 - i, 0, 0)),
                   rblock(N_GATED, 0), full((4, 128, 128)), full((1, D_POOL)), full((8, D_CONV))),
        scratch_shapes=[pltpu.VMEM((n_ext, D_POOL), F32), pltpu.VMEM((n_ext, D_CONV), F32)],
        compiler_params=_params(("arbitrary",)),
    )(dmix, dmix, dmix, proj, proj, proj, proj, proj, proj, o, pooled, cv, w_pool, pool_scale, conv_w)


def _rms_bwd(proj, dqn, dkvn, dkrope, dproj, q_g, kv_g, name, tq=256):
    T = proj.shape[0]
    n_lat = NPP - N_GATED

    def body(ql_ref, kvl_ref, dqn_ref, dkvn_ref, dkr_ref, _, qg_ref, kvg_ref, dlat_ref, dqg_ref, dkvg_ref):
        @pl.when(pl.program_id(0) == 0)
        def _():
            dqg_ref[...] = jnp.zeros_like(dqg_ref)
            dkvg_ref[...] = jnp.zeros_like(dkvg_ref)

        for x_ref, dy_ref, g_ref, c0, dg_ref in ((ql_ref, dqn_ref, qg_ref, 0, dqg_ref),
                                                 (kvl_ref, dkvn_ref, kvg_ref, Q_LORA, dkvg_ref)):
            x, dy = x_ref[...], dy_ref[...]
            r = lax.rsqrt(jnp.mean(x * x, axis=1, keepdims=True) + RMS_EPS)
            xr = x * r
            u = dy * g_ref[...]
            dlat_ref[:, c0:c0 + x.shape[1]] = (r * (u - xr * jnp.mean(u * xr, axis=1, keepdims=True))).astype(BF16)
            dg_ref[...] += jnp.sum(dy * xr, axis=0, keepdims=True)
        dlat_ref[:, Q_LORA + KV_LORA:] = dkr_ref[...]

    row = lambda w: pl.BlockSpec((tq, w), lambda i: (i, 0))
    vec = lambda w: pl.BlockSpec((1, w), lambda i: (0, 0))
    assert N_GATED % n_lat == 0
    return pl.pallas_call(
        body, name=name,
        out_shape=(jax.ShapeDtypeStruct((T, NPP), BF16),
                   jax.ShapeDtypeStruct((1, Q_LORA), F32), jax.ShapeDtypeStruct((1, KV_LORA), F32)),
        grid=(T // tq,),
        in_specs=[_pblock(tq, Q_LORA, O_QLAT), _pblock(tq, KV_LORA, O_KVLAT), row(Q_LORA), row(KV_LORA),
                  row(n_lat - Q_LORA - KV_LORA), pl.BlockSpec(memory_space=pl.ANY), vec(Q_LORA), vec(KV_LORA)],
        out_specs=(pl.BlockSpec((tq, n_lat), lambda i: (i, N_GATED // n_lat)), vec(Q_LORA), vec(KV_LORA)),
        input_output_aliases={5: 0},
        compiler_params=_params(("arbitrary",)),
    )(proj, proj, dqn, dkvn, dkrope, dproj, q_g, kv_g)


def _swap_halves(x, lo):
    return jnp.where(lo, pltpu.roll(x, 96, 1), pltpu.roll(x, 32, 1))


def _rope_fwd(q, kv, proj, cos_t, sin_t, name, tq=256):
    T = q.shape[0]

    def body(qn_ref, qr_ref, kv_ref, kr_ref, c_ref, s_ref, qc_ref, kc_ref):
        C, S = c_ref[...], s_ref[...]
        lane = lax.broadcasted_iota(jnp.int32, (tq, 128), 1)
        lo = (lane % ROPE) < (ROPE // 2)
        first = lane < ROPE

        def rope(x):
            return x * C + _swap_halves(x, lo) * S

        kr = jnp.where(first, rope(kr_ref[...]), 0.0).astype(BF16)
        for j in range(N_HEADS // 2):
            r = rope(qr_ref[:, j * 128:(j + 1) * 128])
            pair = (jnp.where(first, r, 0.0), jnp.where(first, pltpu.roll(r, 64, 1), 0.0))
            for hh in range(2):
                h = 2 * j + hh
                qc_ref[h, :, 0:NOPE] = qn_ref[:, h * NOPE:(h + 1) * NOPE].astype(BF16)
                qc_ref[h, :, NOPE:QC] = pair[hh].astype(BF16)
        for h in range(N_HEADS):
            kc_ref[h, :, 0:NOPE] = kv_ref[:, h * 256:h * 256 + NOPE]
            kc_ref[h, :, NOPE:QC] = kr

    out = jax.ShapeDtypeStruct((N_HEADS, T, QC), BF16)
    hblock = pl.BlockSpec((N_HEADS, tq, QC), lambda i: (0, i, 0))
    return pl.pallas_call(
        body, name=name, out_shape=(out, out), grid=(T // tq,),
        in_specs=[pl.BlockSpec((tq, 1024), lambda i: (i, 0)), pl.BlockSpec((tq, 512), lambda i: (i, 2)),
                  pl.BlockSpec((tq, 2048), lambda i: (i, 0)), _pblock(tq, 128, O_KROPE),
                  pl.BlockSpec((tq, 128), lambda i: (i, 0)), pl.BlockSpec((tq, 128), lambda i: (i, 0))],
        out_specs=(hblock, hblock),
        compiler_params=_params(("parallel",)),
    )(q, q, kv, proj, cos_t, sin_t)


def _rope_bwd(dqc, dkr, cos_t, sin_t, name, tq=256):
    T = dqc.shape[1]

    def body(dqc_ref, dkr_ref, c_ref, s_ref, dq_ref, dk_ref):
        C, S = c_ref[...], s_ref[...]
        lane = lax.broadcasted_iota(jnp.int32, (tq, 128), 1)
        lo = (lane % ROPE) < (ROPE // 2)
        first = lane < ROPE

        def unrope(dy):
            return dy * C - _swap_halves(dy, lo) * S

        acc = dkr_ref[0]
        for h in range(1, N_HEADS):
            acc = acc + dkr_ref[h]
        dk_ref[:, 0:128] = jnp.where(first, unrope(acc), 0.0).astype(BF16)
        dk_ref[:, 128:256] = jnp.zeros((tq, 128), BF16)
        for j in range(N_HEADS // 2):
            d0 = dqc_ref[2 * j, :, NOPE:QC]
            d1 = dqc_ref[2 * j + 1, :, NOPE:QC]
            comb = jnp.where(first, d0, pltpu.roll(d1, 64, 1))
            dq_ref[:, 1024 + j * 128:1024 + (j + 1) * 128] = unrope(comb).astype(BF16)
        for h in range(N_HEADS):
            dq_ref[:, h * NOPE:(h + 1) * NOPE] = dqc_ref[h, :, 0:NOPE].astype(BF16)

    tab = pl.BlockSpec((tq, 128), lambda i: (i, 0))
    return pl.pallas_call(
        body, name=name,
        out_shape=(jax.ShapeDtypeStruct((T, 1536), BF16), jax.ShapeDtypeStruct((T, 256), BF16)),
        grid=(T // tq,),
        in_specs=[pl.BlockSpec((N_HEADS, tq, QC), lambda i: (0, i, 0)),
                  pl.BlockSpec((N_HEADS, tq, 128), lambda i: (0, i, 0)), tab, tab],
        out_specs=(pl.BlockSpec((tq, 1536), lambda i: (i, 0)), pl.BlockSpec((tq, 256), lambda i: (i, 0))),
        compiler_params=_params(("parallel",)),
    )(dqc, dkr, cos_t, sin_t)


def _flash_fwd(qc, kc, kv, proj, mix, name):
    H, T, _ = qc.shape
    tt = ATT_TILE
    nt = T // tt
    sp = tt // ATT_CH

    def body(q_ref, k_ref, v_ref, g_ref, _, o_ref, y_ref, lse_ref, vt_sc, s_sc, acc_sc, m_sc, l_sc):
        i = pl.program_id(1)

        @pl.when(i == 0)
        def _():
            for c in range(nt):
                vt_sc[c] = v_ref[c * tt:(c + 1) * tt, :].astype(F32).T.astype(BF16)

        q = q_ref[0]

        def issue(c, slot):
            s_sc[slot] = lax.dot_general(k_ref[0, pl.ds(pl.multiple_of(c * tt, tt), tt), :], q, _DIMS["nt"],
                                         preferred_element_type=F32)

        def softmax_pv(c, slot, masked):
            s = s_sc[slot]
            if masked:
                krow = c * tt + lax.broadcasted_iota(jnp.int32, s.shape, 0)
                qcol = i * tt + lax.broadcasted_iota(jnp.int32, s.shape, 1)
                s = jnp.where(krow <= qcol, s, -jnp.inf)
            m = m_sc[...]
            m_new = jnp.maximum(m, jnp.max(s, axis=0, keepdims=True))
            p = jnp.exp2((s - m_new) * EXP2_SCALE)
            a = jnp.exp2((m - m_new) * EXP2_SCALE)
            l_sc[...] = a * l_sc[...] + jnp.sum(p, axis=0, keepdims=True)
            acc_sc[...] = a * acc_sc[...] + jnp.dot(vt_sc[c], p.astype(BF16), preferred_element_type=F32)
            m_sc[...] = m_new

        issue(0, 0)
        m_sc[...] = jnp.full_like(m_sc, -jnp.inf)
        l_sc[...] = jnp.zeros_like(l_sc)
        acc_sc[...] = jnp.zeros_like(acc_sc)

        def pair(t, carry):
            issue(2 * t + 1, 1)
            softmax_pv(2 * t, 0, False)
            issue(2 * t + 2, 0)
            softmax_pv(2 * t + 1, 1, False)
            return carry

        lax.fori_loop(0, i // 2, pair, 0)

        @pl.when(i % 2 == 1)
        def _():
            issue(i, 1)
            softmax_pv(i - 1, 0, False)
            softmax_pv(i, 1, True)

        @pl.when(i % 2 == 0)
        def _():
            softmax_pv(i, 0, True)

        l = l_sc[...]
        o = (acc_sc[...] / l).T
        o_ref[...] = o
        lse = m_sc[...] * ATTN_SCALE + jnp.log(l)
        for r in range(sp):
            lse_ref[0, r] = lse[:, r * ATT_CH:(r + 1) * ATT_CH]
        g = g_ref[...]
        y_ref[...] = (o * (g * _sigmoid(g))).astype(BF16)

    return pl.pallas_call(
        body, name=name,
        out_shape=(jax.ShapeDtypeStruct((T, D_MLA), F32), jax.ShapeDtypeStruct((T, D_MIX), BF16),
                   jax.ShapeDtypeStruct((H, T // ATT_CH, 1, ATT_CH), F32)),
        grid=(H, nt),
        in_specs=[pl.BlockSpec((1, tt, QC), lambda h, i: (h, i, 0)),
                  pl.BlockSpec((1, T, QC), lambda h, i: (h, 0, 0)),
                  pl.BlockSpec((T, V_DIM), lambda h, i: (0, 2 * h + 1)),
                  pl.BlockSpec((tt, V_DIM), lambda h, i: (i, h)),
                  pl.BlockSpec(memory_space=pl.ANY)],
        input_output_aliases={4: 1},
        out_specs=(pl.BlockSpec((tt, V_DIM), lambda h, i: (i, h)),
                   pl.BlockSpec((tt, V_DIM), lambda h, i: (i, h)),
                   pl.BlockSpec((1, sp, 1, ATT_CH), lambda h, i: (h, i, 0, 0))),
        scratch_shapes=[pltpu.VMEM((nt, V_DIM, tt), BF16), pltpu.VMEM((2, tt, tt), F32),
                        pltpu.VMEM((V_DIM, tt), F32), pltpu.VMEM((1, tt), F32), pltpu.VMEM((1, tt), F32)],
        compiler_params=_params(("parallel", "arbitrary")),
    )(qc, kc, kv, proj, mix)


def _flash_bwd(qc, kc, kv, do, lse, delta, name):
    H, T, _ = qc.shape
    tt = ATT_TILE
    nt = T // tt
    sp = tt // ATT_CH

    def body(q_ref, k_ref, v_ref, do_ref, lse_ref, dl_ref, dq_ref, dkv_ref, dkr_ref, dqt_sc, dk_sc, dv_sc, s_sc,
             dp_sc, kt_sc):
        j = pl.program_id(1)

        @pl.when(j == 0)
        def _():
            dqt_sc[...] = jnp.zeros_like(dqt_sc)

        k = k_ref[0]
        v = v_ref[...]

        def operands(c):
            q0 = pl.multiple_of(c * tt, tt)
            return q_ref[0, pl.ds(q0, tt), :], do_ref[pl.ds(q0, tt), :]

        def stat_row(ref, c):
            return jnp.concatenate([ref[0, sp * c + r] for r in range(sp)], axis=1)

        def early(c, slot):
            q, dov = operands(c)
            s_sc[slot] = lax.dot_general(k, q, _DIMS["nt"], preferred_element_type=F32)
            dp_sc[slot] = lax.dot_general(v, dov, _DIMS["nt"], preferred_element_type=F32)

        def late(c, slot, masked, kt):
            q, dov = operands(c)
            s, dp = s_sc[slot], dp_sc[slot]
            if masked:
                krow = j * tt + lax.broadcasted_iota(jnp.int32, s.shape, 0)
                qcol = c * tt + lax.broadcasted_iota(jnp.int32, s.shape, 1)
                s = jnp.where(krow <= qcol, s, -jnp.inf)
            p = jnp.exp2(s * EXP2_SCALE - stat_row(lse_ref, c) * LOG2E)
            ds = (p * (dp - stat_row(dl_ref, c)) * ATTN_SCALE).astype(BF16)
            dv = jnp.dot(p.astype(BF16), dov, preferred_element_type=F32)
            dk = jnp.dot(ds, q, preferred_element_type=F32)
            if masked:
                dv_sc[slot] = dv
                dk_sc[slot] = dk
            else:
                dv_sc[slot] += dv
                dk_sc[slot] += dk
            dqt_sc[c] += jnp.dot(kt, ds, preferred_element_type=F32)

        early(j, 0)
        early(jnp.minimum(j + 1, nt - 1), 1)
        dk_sc[1] = jnp.zeros((tt, QC), F32)
        dv_sc[1] = jnp.zeros((tt, V_DIM), F32)
        kt_sc[...] = k.astype(F32).T.astype(BF16)
        late(j, 0, True, kt_sc[...])
        n_rest = nt - 1 - j

        def pair(u, carry):
            a = j + 1 + 2 * u
            kt = kt_sc[...]
            early(a + 1, 0)
            late(a, 1, False, kt)
            early(jnp.minimum(a + 2, nt - 1), 1)
            late(a + 1, 0, False, kt)
            return carry

        lax.fori_loop(0, n_rest // 2, pair, 0)

        @pl.when(n_rest % 2 == 1)
        def _():
            late(nt - 1, 1, False, kt_sc[...])

        dk = dk_sc[0] + dk_sc[1]
        dkv_ref[:, 0:NOPE] = dk[:, 0:NOPE].astype(BF16)
        dkv_ref[:, NOPE:] = (dv_sc[0] + dv_sc[1]).astype(BF16)
        dkr_ref[0] = dk[:, NOPE:]

        @pl.when(j == nt - 1)
        def _():
            for c in range(nt):
                dq_ref[0, c * tt:(c + 1) * tt, :] = dqt_sc[c].T

    head = lambda h, j: (h, 0, 0)
    stat = pl.BlockSpec((1, T // ATT_CH, 1, ATT_CH), lambda h, j: (h, 0, 0, 0))
    return pl.pallas_call(
        body, name=name,
        out_shape=(jax.ShapeDtypeStruct((H, T, QC), F32), jax.ShapeDtypeStruct((T, 2 * D_MLA), BF16),
                   jax.ShapeDtypeStruct((H, T, 128), F32)),
        grid=(H, nt),
        in_specs=[pl.BlockSpec((1, T, QC), head),
                  pl.BlockSpec((1, tt, QC), lambda h, j: (h, j, 0)),
                  pl.BlockSpec((tt, V_DIM), lambda h, j: (j, 2 * h + 1)),
                  pl.BlockSpec((T, V_DIM), lambda h, j: (0, h)),
                  stat, stat],
        out_specs=(pl.BlockSpec((1, T, QC), head),
                   pl.BlockSpec((tt, 256), lambda h, j: (j, h)),
                   pl.BlockSpec((1, tt, 128), lambda h, j: (h, j, 0))),
        scratch_shapes=[pltpu.VMEM((nt, QC, tt), F32), pltpu.VMEM((2, tt, QC), F32), pltpu.VMEM((2, tt, V_DIM), F32),
                        pltpu.VMEM((2, tt, tt), F32), pltpu.VMEM((2, tt, tt), F32), pltpu.VMEM((QC, tt), BF16)],
        compiler_params=_params(("parallel", "arbitrary")),
    )(qc, kc, kv, do, lse, delta)


def _adamw(lands, w, m, v, name, rows, cols=None):
    L, R, C = w.shape
    cols = C if cols is None else cols
    assert R % rows == 0 and C % cols == 0 and len(lands) == L
    nc = C // cols
    nb = (R // rows) * nc
    c1 = 1.0 - ADAM_B1 ** ADAM_STEP
    c2 = 1.0 - ADAM_B2 ** ADAM_STEP

    def body(*refs):
        land_refs = refs[:L]
        w_ref, m_ref, v_ref, g_ref, d_ref, nm_ref, nv_ref, g_sc = refs[L:]
        for ll in range(L):
            @pl.when(pl.program_id(0) == ll)
            def _(land_ref=land_refs[ll]):
                g = land_ref[0].astype(F32)
                for s in range(1, N_DEV):
                    g = g + land_ref[s].astype(F32)
                g_sc[...] = g

        g = g_sc[...]
        nm = ADAM_B1 * m_ref[0] + (1.0 - ADAM_B1) * g
        nv = ADAM_B2 * v_ref[0] + (1.0 - ADAM_B2) * (g * g)
        g_ref[0] = g
        nm_ref[0] = nm
        nv_ref[0] = nv
        d_ref[0] = -ADAM_LR * ((nm / c1) / (jnp.sqrt(nv / c2) + ADAM_EPS) + ADAM_WD * w_ref[0])

    def land_spec(ll):
        def index(l, i):
            i = jnp.where(l < ll, 0, jnp.where(l > ll, nb - 1, i))
            return (0, i // nc, i % nc)
        return pl.BlockSpec((N_DEV, rows, cols), index)

    blk = pl.BlockSpec((1, rows, cols), lambda l, i: (l, i // nc, i % nc))
    out = jax.ShapeDtypeStruct((L, R, C), F32)
    return pl.pallas_call(
        body, name=name, out_shape=(out, out, out, out), grid=(L, nb),
        in_specs=[land_spec(ll) for ll in range(L)] + [blk, blk, blk],
        out_specs=(blk, blk, blk, blk),
        scratch_shapes=[pltpu.VMEM((rows, cols), F32)],
        compiler_params=_params(("arbitrary", "arbitrary")),
    )(*lands, w, m, v)


def _mesh_pos():
    return lax.axis_index("x"), lax.axis_index("y"), lax.axis_index("c")


def _all_gather(arrays, name):
    n = len(arrays)

    def body(*refs):
        ins, outs = refs[:n], refs[n:2 * n]
        send_sems, recv_sems, local_sems = refs[2 * n:]
        x, y, c = _mesh_pos()
        me, sibling = (x, y, c), (x, y, 1 - c)
        chips = [(1 - x, y), (x, 1 - y), (1 - x, 1 - y)]

        def slot(a, pos):
            px, py, pc = pos
            return outs[a].at[4 * px + 2 * py + pc]

        def copy(a, k, block, to, src=None):
            return pltpu.make_async_remote_copy(
                src_ref=slot(a, block) if src is None else src, dst_ref=slot(a, block),
                send_sem=send_sems.at[a * 7 + k], recv_sem=recv_sems.at[a * 7 + k],
                device_id=to, device_id_type=MESH_ID)

        mine, first, passed = [], [], []
        for a in range(n):
            cp = pltpu.make_async_copy(ins[a], slot(a, me), local_sems.at[a])
            cp.start()
            mine.append(cp)
            cps = [copy(a, 0, me, sibling, src=ins[a])]
            cps += [copy(a, 1 + j, me, (*chip, c), src=ins[a]) for j, chip in enumerate(chips)]
            for cp in cps:
                cp.start()
            first += cps
        for j, chip in enumerate(chips):
            for a in range(n):
                copy(a, 1 + j, (*chip, c), me).wait_recv()
                cp = copy(a, 4 + j, (*chip, c), sibling)
                cp.start()
                passed.append(cp)
        for a in range(n):
            copy(a, 0, sibling, me).wait_recv()
            for j, chip in enumerate(chips):
                copy(a, 4 + j, (*chip, 1 - c), me).wait_recv()
        for cp in first + passed:
            cp.wait_send()
        for cp in mine:
            cp.wait()

    hbm = pl.BlockSpec(memory_space=pltpu.HBM)
    return pl.pallas_call(
        body, name=name,
        out_shape=tuple(jax.ShapeDtypeStruct((N_DEV,) + a.shape, a.dtype) for a in arrays),
        in_specs=[hbm] * n, out_specs=tuple([hbm] * n),
        scratch_shapes=[pltpu.SemaphoreType.DMA((7 * n,)), pltpu.SemaphoreType.DMA((7 * n,)),
                        pltpu.SemaphoreType.DMA((n,))],
    )(*arrays)


_HBM = pl.BlockSpec(memory_space=pltpu.HBM)
_SEM = pl.BlockSpec(memory_space=pltpu.SEMAPHORE)
_EFFECT = pltpu.SideEffectType.DATAFLOW_SIDE_EFFECTING
N_PEERS = N_DEV - 1


def _peer(k):
    x, y, c = _mesh_pos()
    return (1 - x if k & 4 else x, 1 - y if k & 2 else y, 1 - c if k & 1 else c)


def _split_start(srcs, scatter, after, name):
    n = len(srcs)
    zones = [jax.ShapeDtypeStruct(s.shape if scatter else ((N_DEV,) + s.shape), s.dtype) for s in srcs]

    def body(*refs):
        src, zone = refs[:n], refs[n:2 * n]
        outs = refs[2 * n + 1:]
        send, recv, token = outs[:n], outs[n:2 * n], outs[4 * n]
        x, y, c = _mesh_pos()
        my_idx = 4 * x + 2 * y + c
        for a in range(n):
            pltpu.make_async_copy(src[a].at[my_idx] if scatter else src[a],
                                  zone[a].at[N_PEERS] if scatter else zone[a].at[my_idx], recv[a]).start()
            for k in range(1, N_DEV):
                px, py, pc = _peer(k)
                pltpu.make_async_remote_copy(
                    src_ref=src[a].at[4 * px + 2 * py + pc] if scatter else src[a],
                    dst_ref=zone[a].at[k - 1] if scatter else zone[a].at[my_idx],
                    send_sem=send[a], recv_sem=recv[a], device_id=(px, py, pc), device_id_type=MESH_ID).start()
        token[...] = jnp.zeros_like(token)

    hbm = lambda a: pltpu.with_memory_space_constraint(a, pltpu.HBM)
    outs = pl.pallas_call(
        body, name=name,
        out_shape=tuple([pltpu.SemaphoreType.DMA(())] * (2 * n)
                        + [pltpu.HBM(s.shape, s.dtype) for s in srcs]
                        + [pltpu.HBM(z.shape, z.dtype) for z in zones]
                        + [jax.ShapeDtypeStruct((8, 128), F32)]),
        in_specs=[_HBM] * (2 * n) + [pl.BlockSpec(memory_space=pl.ANY)],
        out_specs=tuple([_SEM] * (2 * n) + [_HBM] * (2 * n) + [pl.BlockSpec(memory_space=pltpu.VMEM)]),
        input_output_aliases={**{a: 2 * n + a for a in range(n)}, **{n + a: 3 * n + a for a in range(n)}},
        compiler_params=pltpu.CompilerParams(has_side_effects=_EFFECT),
    )(*[hbm(s) for s in srcs], *[hbm(lax.empty(z.shape, z.dtype)) for z in zones], after)
    return outs[:n], outs[n:2 * n], outs[2 * n:3 * n], outs[3 * n:4 * n], outs[4 * n]


def _split_wait(send, recv, srcs, zones, after, name):
    n = len(srcs)

    def body(*refs):
        zone = refs[n:2 * n]
        send_sems, recv_sems = refs[2 * n:3 * n], refs[3 * n:4 * n]
        x, y, c = _mesh_pos()
        for a in range(n):
            seven = zone[a].at[pl.ds(0, N_PEERS)]
            pltpu.make_async_remote_copy(src_ref=seven, dst_ref=seven, send_sem=send_sems[a], recv_sem=recv_sems[a],
                                         device_id=(x, y, 1 - c), device_id_type=MESH_ID).wait_send()
            pltpu.make_async_remote_copy(src_ref=zone[a], dst_ref=zone[a], send_sem=send_sems[a],
                                         recv_sem=recv_sems[a], device_id=(x, y, 1 - c),
                                         device_id_type=MESH_ID).wait_recv()

    outs = pl.pallas_call(
        body, name=name,
        out_shape=tuple([pltpu.HBM(s.shape, s.dtype) for s in srcs] + [pltpu.HBM(z.shape, z.dtype) for z in zones]),
        in_specs=[_HBM] * (2 * n) + [_SEM] * (2 * n) + [pl.BlockSpec(memory_space=pl.ANY)],
        out_specs=tuple([_HBM] * (2 * n)),
        input_output_aliases={a: a for a in range(2 * n)},
        compiler_params=pltpu.CompilerParams(has_side_effects=_EFFECT),
    )(*srcs, *zones, *send, *recv, after)
    return outs[:n], outs[n:]


def _cat_blocks(g, axis):
    return jnp.concatenate([g[d] for d in range(N_DEV)], axis=axis)


N_LATENT = Q_LORA + KV_LORA + ROPE
W_SHARD = D_IN_PROJ // N_DEV


def _ref_cols(lo, hi):
    out = []
    if lo < N_LATENT:
        out.append((N_GATED + lo, N_GATED + min(hi, N_LATENT)))
    if hi > N_LATENT:
        out.append((max(lo, N_LATENT) - N_LATENT, hi - N_LATENT))
    return out


def _permute_w_in_t(blocks):
    pieces = []
    for lo, hi in ((N_LATENT, D_IN_PROJ), (0, N_LATENT)):
        for d in range(N_DEV):
            a, b = max(lo, d * W_SHARD), min(hi, (d + 1) * W_SHARD)
            if a < b:
                pieces.append(blocks[d][a - d * W_SHARD:b - d * W_SHARD])
    pieces.append(jnp.zeros((NPP - D_IN_PROJ, blocks.shape[2]), blocks.dtype))
    return jnp.concatenate(pieces, axis=0)


def _split_w_in_t(w):
    slabs = []
    for d in range(N_DEV):
        parts = [w[a:b] for a, b in _ref_cols(d * W_SHARD, (d + 1) * W_SHARD)]
        slabs.append(parts[0] if len(parts) == 1 else jnp.concatenate(parts, axis=0))
    return jnp.stack(slabs)


def _permute_w_uq(w):
    w3 = w.reshape(w.shape[0], N_HEADS, NOPE + ROPE)
    return jnp.concatenate([w3[:, :, :NOPE].reshape(w.shape[0], -1), w3[:, :, NOPE:].reshape(w.shape[0], -1)], axis=1)


def _unpermute_w_uq(w):
    nope = w[:, :N_HEADS * NOPE].reshape(w.shape[0], N_HEADS, NOPE)
    rope = w[:, N_HEADS * NOPE:].reshape(w.shape[0], N_HEADS, ROPE)
    return jnp.concatenate([nope, rope], axis=2).reshape(w.shape[0], -1)


_SMALL_EMB = (("emb_ln_g", 16), ("emb_ln_b", 16))
_SMALL_LAYER = (("q_norm_g", 8), ("kv_norm_g", 8), ("w_pool", 1024), ("pool_scale", 8), ("b_out", 32),
                ("ln_g", 32), ("ln_b", 32))
_SMALL = _SMALL_EMB + _SMALL_LAYER
CONV_ROWS = DEPTH * CONV_WIDTH * D_CONV // 128


def _pack_small(d, entries=_SMALL):
    parts = []
    for name, rows in entries:
        flat = d[name].reshape(-1)
        flat = jnp.pad(flat, (0, rows * 128 - flat.shape[0]))
        parts.append(flat.reshape(rows, 128))
    return jnp.concatenate(parts, axis=0)


def _unpack_small(packed, shapes):
    out, r0 = {}, 0
    for name, rows in _SMALL:
        size = 1
        for s in shapes[name]:
            size *= s
        out[name] = packed[r0:r0 + rows].reshape(-1)[:size].reshape(shapes[name])
        r0 += rows
    return out


def _rope_tables(positions):
    half = ROPE // 2
    inv_freq = ROPE_THETA ** (-jnp.arange(half, dtype=F32) / half)
    ang = positions.astype(F32)[:, None] * inv_freq
    cos, sin = jnp.cos(ang), jnp.sin(ang)
    return jnp.concatenate([cos, cos, cos, cos], axis=1), jnp.concatenate([-sin, sin, -sin, sin], axis=1)


def _local_step(x, positions, target, emb_g, emb_b, layer_weights, layer_weights_rest, on_sharded_grads,
                on_layer_grads=None, first_after=None):
    cos_t, sin_t = _rope_tables(positions)
    h, hb = _ln_fwd(x, emb_g, emb_b, "emb_ln_fwd", after=first_after)
    saved = []
    for l in range(DEPTH):
        W = layer_weights(l, h)
        proj = _mm(hb, W["w_in_t"], "nt", F32, "proj_fwd")
        qn, kvn, pooled, cv, mix = _mix_fwd(proj, W["q_norm_g"], W["kv_norm_g"], W["w_pool"], W["pool_scale"],
                                            W["conv_w"], "mix_fwd")
        rest, token = layer_weights_rest(l, proj)
        W = {**W, **rest}
        q = _mm(qn, W["w_uq"], "nn", F32, "q_up_fwd", after=token)
        kv = _mm(kvn, W["w_ukv"], "nn", BF16, "kv_up_fwd")
        qc, kc = _rope_fwd(q, kv, proj, cos_t, sin_t, "rope_fwd")
        o, mix, lse = _flash_fwd(qc, kc, kv, proj, mix, "flash_fwd")
        z = _mm(mix, W["w_out"], "nn", F32, "out_fwd", res=h, bias=W["b_out"], alpha=ALPHA)
        saved.append((W, hb, proj, qn, kvn, pooled, cv, kv, qc, kc, o, lse, mix, z))
        h, hb = _ln_fwd(z, W["ln_g"], W["ln_b"], "ln_fwd")
    dh, sq = h, None

    grads = {k: [None] * DEPTH for k in ("q_norm_g", "kv_norm_g", "w_pool", "pool_scale", "conv_w", "b_out", "ln_g",
                                         "ln_b")}
    for l in reversed(range(DEPTH)):
        W, hb_in, proj, qn, kvn, pooled, cv, kv, qc, kc, o, lse, mix, z = saved[l]
        sharded = {}
        if l == DEPTH - 1:
            dz, dzb, grads["b_out"][l], grads["ln_g"][l], grads["ln_b"][l], sq = _ln_bwd(
                dh, z, W["ln_g"], "ln_bwd_loss", target=target)
        else:
            dz, dzb, grads["b_out"][l], grads["ln_g"][l], grads["ln_b"][l] = _ln_bwd(dh, z, W["ln_g"], "ln_bwd")
        dmix = _mm(dzb, W["w_out"], "nt", F32, "out_bwd_x")
        sharded["w_out"] = _mm(mix, dzb, "tn", GRAD_XFER, "out_bwd_w", tk=4096)
        do, delta, dproj, grads["w_pool"][l], grads["pool_scale"][l], grads["conv_w"][l] = _mix_bwd(
            dmix, proj, o, pooled, cv, W["w_pool"], W["pool_scale"], W["conv_w"], "mix_bwd")
        dqc, dkv, dkr = _flash_bwd(qc, kc, kv, do, lse, delta, "flash_bwd")
        dq, dkrope = _rope_bwd(dqc, dkr, cos_t, sin_t, "rope_bwd")
        dqn = _mm(dq, W["w_uq"], "nt", F32, "q_up_bwd_x")
        sharded["w_uq"] = _mm(qn, dq, "tn", GRAD_XFER, "q_up_bwd_w")
        dkvn = _mm(dkv, W["w_ukv"], "nt", F32, "kv_up_bwd_x")
        sharded["w_ukv"] = _mm(kvn, dkv, "tn", GRAD_XFER, "kv_up_bwd_w")
        token = on_sharded_grads(l, sharded)
        dproj, grads["q_norm_g"][l], grads["kv_norm_g"][l] = _rms_bwd(
            proj, dqn, dkvn, dkrope, dproj, W["q_norm_g"], W["kv_norm_g"], "rms_bwd")
        if l == 0 and on_layer_grads is not None:
            token = on_layer_grads(grads, token)
        d_w_in_t = _mm(dproj, hb_in, "tn", GRAD_XFER, "proj_bwd_w", tk=4096, after=token)
        token = on_sharded_grads(l, {"w_in": d_w_in_t})
        dh = _mm(dproj, W["w_in_t"], "nn", F32, "proj_bwd_x", res=dz, alpha=ALPHA, tk=2560, after=token)
    grad_x, grads["emb_ln_g"], grads["emb_ln_b"] = _ln_bwd(dh, x, emb_g, "emb_ln_bwd", for_matmul=False)
    return sq, grad_x, grads


def kernel(x, positions, emb_ln_g, emb_ln_b, w_in, q_norm_g, kv_norm_g, w_uq, w_ukv, w_pool, pool_scale, conv_w, w_out, b_out, ln_g, ln_b, loss_target, m_emb_ln_g, m_emb_ln_b, m_w_in, m_q_norm_g, m_kv_norm_g, m_w_uq, m_w_ukv, m_w_pool, m_pool_scale, m_conv_w, m_w_out, m_b_out, m_ln_g, m_ln_b, v_emb_ln_g, v_emb_ln_b, v_w_in, v_q_norm_g, v_kv_norm_g, v_w_uq, v_w_ukv, v_w_pool, v_pool_scale, v_conv_w, v_w_out, v_b_out, v_ln_g, v_ln_b):
    weights = dict(emb_ln_g=emb_ln_g, emb_ln_b=emb_ln_b, w_in=w_in, q_norm_g=q_norm_g, kv_norm_g=kv_norm_g,
                   w_uq=w_uq, w_ukv=w_ukv, w_pool=w_pool, pool_scale=pool_scale, conv_w=conv_w, w_out=w_out,
                   b_out=b_out, ln_g=ln_g, ln_b=ln_b)
    mom1 = dict(emb_ln_g=m_emb_ln_g, emb_ln_b=m_emb_ln_b, w_in=m_w_in, q_norm_g=m_q_norm_g, kv_norm_g=m_kv_norm_g,
                w_uq=m_w_uq, w_ukv=m_w_ukv, w_pool=m_w_pool, pool_scale=m_pool_scale, conv_w=m_conv_w,
                w_out=m_w_out, b_out=m_b_out, ln_g=m_ln_g, ln_b=m_ln_b)
    mom2 = dict(emb_ln_g=v_emb_ln_g, emb_ln_b=v_emb_ln_b, w_in=v_w_in, q_norm_g=v_q_norm_g, kv_norm_g=v_kv_norm_g,
                w_uq=v_w_uq, w_ukv=v_w_ukv, w_pool=v_w_pool, pool_scale=v_pool_scale, conv_w=v_conv_w,
                w_out=v_w_out, b_out=v_b_out, ln_g=v_ln_g, ln_b=v_ln_b)

    big = ("w_in", "w_uq", "w_ukv", "w_out")

    conv_pad = jnp.zeros((8, 128), F32).at[0:DEPTH * CONV_WIDTH, 0:64].set(conv_w.reshape(DEPTH * CONV_WIDTH, 64))
    t12 = lambda a: jnp.swapaxes(a, 1, 2)
    shard = lambda k, l: (t12(weights[k])[l] if k == "w_in" else weights[k][l]).astype(BF16)
    w_in0, conv_all = _all_gather([shard("w_in", 0), conv_pad], "w_in0_all_gather")
    rest0 = _split_start([shard(k, 0) for k in big[1:]], False, w_in0, "weights0_rest_start")
    conv_full = _cat_blocks(conv_all[:, 0:DEPTH * CONV_WIDTH, 0:64], 1).reshape(DEPTH, CONV_WIDTH, D_CONV)
    conv_full = jnp.pad(conv_full, ((0, 0), (0, 8 - CONV_WIDTH), (0, 0)))
    fetched = {}

    def layer_weights(l, ready):
        if l == 0:
            w_in_blocks = w_in0
        else:
            fetched[1] = _split_wait(*fetched["w1"][:4], ready, "weights1_wait")[1]
            w_in_blocks = fetched[1][0]
        return dict(
            w_in_t=_permute_w_in_t(w_in_blocks), conv_w=conv_full[l],
            q_norm_g=q_norm_g[l].reshape(1, -1), kv_norm_g=kv_norm_g[l].reshape(1, -1),
            w_pool=w_pool[l].astype(BF16), pool_scale=pool_scale[l].reshape(1, -1), b_out=b_out[l].reshape(1, -1),
            ln_g=ln_g[l].reshape(1, -1), ln_b=ln_b[l].reshape(1, -1))

    def layer_weights_rest(l, ready):
        token = None
        if l == 0:
            blocks = _split_wait(*rest0[:4], ready, "weights0_rest_wait")[1]
            fetched["w1"] = _split_start([shard(k, 1) for k in big], False, blocks[0], "weights1_start")
            token = fetched["w1"][4]
        else:
            blocks = fetched[1][1:]
        return dict(w_uq=_permute_w_uq(_cat_blocks(blocks[0], 1)), w_ukv=_cat_blocks(blocks[1], 1),
                    w_out=_cat_blocks(blocks[2], 0)), token

    by_dest = dict(
        w_in=_split_w_in_t,
        w_uq=lambda g: _unpermute_w_uq(g).reshape(Q_LORA, N_DEV, -1).transpose(1, 0, 2),
        w_ukv=lambda g: g.reshape(KV_LORA, N_DEV, -1).transpose(1, 0, 2),
        w_out=lambda g: g.reshape(N_DEV, -1, D_MODEL))
    in_flight = []

    def on_sharded_grads(l, g):
        names = [k for k in big if k in g]
        srcs = [by_dest[k](g[k]) for k in names]
        started = _split_start(srcs, True, srcs[0], "grads%d_%s_start" % (l, names[0]))
        in_flight.append((l, names, started[:4]))
        return started[4]

    small_in_flight = []

    def on_layer_grads(g, token):
        stacked = {k: jnp.stack(g[k]) for k, _ in _SMALL_LAYER}
        conv = jnp.stack([g["conv_w"][l][0:CONV_WIDTH] for l in range(DEPTH)]).reshape(CONV_ROWS, 128)
        packed = jnp.concatenate([_pack_small(stacked, _SMALL_LAYER), conv], axis=0)
        started = _split_start([packed], False, token, "layer_grads_start")
        small_in_flight.append(started[:4])
        return started[4]

    sq, grad_x, G = _local_step(x[0], positions[0], loss_target[0], emb_ln_g.reshape(1, -1),
                                emb_ln_b.reshape(1, -1), layer_weights, layer_weights_rest, on_sharded_grads,
                                on_layer_grads, first_after=rest0[4])
    loss = lax.psum(sq[0, 0] * (0.5 / D_MODEL), ("x", "y", "c"))

    res = {}
    landed = {}
    for l, names, started in in_flight:
        zones = _split_wait(*started, grad_x, "grads%d_%s_wait" % (l, names[0]))[1]
        for k, zone in zip(names, zones):
            landed[k, l] = zone
    res["w_in"] = tuple(t12(o) for o in _adamw(
        [landed["w_in", l] for l in range(DEPTH)], t12(w_in), t12(m_w_in), t12(v_w_in), "adamw_w_in", W_SHARD, 512))
    for name, rows in (("w_uq", 256), ("w_ukv", 256), ("w_out", 128)):
        res[name] = _adamw([landed[name, l] for l in range(DEPTH)], weights[name], mom1[name], mom2[name],
                           "adamw_" + name, rows)

    layer_zone = _split_wait(*small_in_flight[0], grad_x, "layer_grads_wait")[1][0]
    emb_zone = _all_gather([_pack_small(G, _SMALL_EMB)], "emb_grads_all_gather")[0]
    n_layer_rows = sum(r for _, r in _SMALL_LAYER)
    l_small = jnp.concatenate([emb_zone, layer_zone[:, 0:n_layer_rows]], axis=1)
    my_idx = 4 * lax.axis_index("x") + 2 * lax.axis_index("y") + lax.axis_index("c")
    conv_all_grads = layer_zone[:, n_layer_rows:].reshape(N_DEV, DEPTH * CONV_WIDTH, D_CONV)
    l_conv = lax.dynamic_slice_in_dim(conv_all_grads, my_idx * 64, 64, axis=2)
    l_conv = jnp.zeros((N_DEV, 8, 128), F32).at[:, 0:DEPTH * CONV_WIDTH, 0:64].set(l_conv)
    conv_shard = lambda a: jnp.zeros((8, 128), F32).at[0:DEPTH * CONV_WIDTH, 0:64].set(a.reshape(-1, 64))
    conv_res = _adamw([l_conv], conv_shard(conv_w)[None], conv_shard(m_conv_w)[None], conv_shard(v_conv_w)[None],
                      "adamw_conv_w", 8)
    res["conv_w"] = tuple(o[0, 0:DEPTH * CONV_WIDTH, 0:64].reshape(DEPTH, CONV_WIDTH, 64) for o in conv_res)
    small_res = _adamw([l_small], _pack_small(weights)[None], _pack_small(mom1)[None], _pack_small(mom2)[None],
                       "adamw_small", 392)
    shapes = {k: weights[k].shape for k, _ in _SMALL}
    unpacked = [_unpack_small(o[0], shapes) for o in small_res]
    for k, _ in _SMALL:
        res[k] = tuple(u[k] for u in unpacked)

    order = ("emb_ln_g", "emb_ln_b", "w_in", "q_norm_g", "kv_norm_g", "w_uq", "w_ukv", "w_pool", "pool_scale",
             "conv_w", "w_out", "b_out", "ln_g", "ln_b")
    return (loss, grad_x[None], *[res[k][0] for k in order], *[res[k][1] for k in order],
            *[res[k][2] for k in order], *[res[k][3] for k in order])
```

```python
import jax
import jax.numpy as jnp
from jax import lax
from jax.experimental import pallas as pl
from jax.experimental.pallas import tpu as pltpu

F32 = jnp.float32
BF16 = jnp.bfloat16

N_DEV = 8
D_MODEL = 2048
DEPTH = 2
N_HEADS = 8
NOPE = 128
ROPE = 64
V_DIM = 128
Q_LORA = 512
KV_LORA = 256
D_MLA = N_HEADS * V_DIM
D_POOL = 512
D_CONV = 512
POOL_WINDOWS = (2, 4, 8, 16)
POOL_GROUP = 128
CONV_WIDTH = 3
D_MIX = D_MLA + D_POOL + D_CONV
D_IN_PROJ = 4928
ROPE_THETA = 10000.0
LN_EPS = 1e-5
RMS_EPS = 1e-6
ALPHA = (2 * DEPTH) ** 0.25
ATTN_SCALE = (NOPE + ROPE) ** -0.5
ADAM_LR = 0.001
ADAM_B1 = 0.9
ADAM_B2 = 0.999
ADAM_EPS = 1e-08
ADAM_WD = 0.01
ADAM_STEP = 10

O_GMLA, O_PIN, O_GPOOL, O_CH, O_CB, O_CC, O_GCONV, O_QLAT, O_KVLAT, O_KROPE = (
    0, 1024, 1536, 2048, 2560, 3072, 3584, 4096, 4608, 4864)
NPP = 5120
N_GATED = O_QLAT
QC = NOPE + 2 * ROPE
HALO = 16
ATT_TILE = 512
ATT_CH = 256
LOG2E = 1.4426950408889634
EXP2_SCALE = ATTN_SCALE * LOG2E

GRAD_XFER = BF16
VMEM_LIMIT = 48 * 1024 * 1024
MESH_ID = pl.DeviceIdType.MESH


def _params(sem=None):
    return pltpu.CompilerParams(dimension_semantics=sem, vmem_limit_bytes=VMEM_LIMIT)


def _sigmoid(x):
    return 1.0 / (1.0 + jnp.exp(-x))


def _tile(dim, target):
    if dim <= target:
        return dim
    t = target - target % 128
    while dim % t:
        t -= 128
    return t


_DIMS = {"nn": (((1,), (0,)), ((), ())), "nt": (((1,), (1,)), ((), ())), "tn": (((0,), (0,)), ((), ()))}


def _mm(a, b, mode, out_dtype, name, res=None, bias=None, alpha=1.0, tm=1024, tn=1024, tk=2048, after=None):
    if mode == "nn":
        (M, K), (K2, N) = a.shape, b.shape
    elif mode == "nt":
        (M, K), (N, K2) = a.shape, b.shape
    else:
        (K, M), (K2, N) = a.shape, b.shape
    assert K == K2
    tm, tn, tk = _tile(M, tm), _tile(N, tn), _tile(K, tk)
    nk = K // tk
    has_res, has_bias = res is not None, bias is not None

    def body(*refs):
        a_ref, b_ref = refs[0], refs[1]
        pos = 2
        res_ref = bias_ref = None
        if has_res:
            res_ref = refs[pos]
            pos += 1
        if has_bias:
            bias_ref = refs[pos]
            pos += 1
        def finish(r, o_ref):
            if has_bias:
                r = r + bias_ref[...]
            if has_res:
                r = alpha * res_ref[...] + r
            o_ref[...] = r.astype(out_dtype)

        part = lax.dot_general(a_ref[...].astype(BF16), b_ref[...].astype(BF16), _DIMS[mode],
                               preferred_element_type=F32)
        if nk == 1:
            finish(part, refs[-1])
            return
        o_ref, acc_ref = refs[-2], refs[-1]
        k = pl.program_id(2)

        @pl.when(k == 0)
        def _():
            acc_ref[...] = part

        @pl.when(jnp.logical_and(k > 0, k < nk - 1))
        def _():
            acc_ref[...] += part

        @pl.when(k == nk - 1)
        def _():
            finish(acc_ref[...] + part, o_ref)

    if mode == "nn":
        in_specs = [pl.BlockSpec((tm, tk), lambda i, j, k: (i, k)), pl.BlockSpec((tk, tn), lambda i, j, k: (k, j))]
    elif mode == "nt":
        in_specs = [pl.BlockSpec((tm, tk), lambda i, j, k: (i, k)), pl.BlockSpec((tn, tk), lambda i, j, k: (j, k))]
    else:
        in_specs = [pl.BlockSpec((tk, tm), lambda i, j, k: (k, i)), pl.BlockSpec((tk, tn), lambda i, j, k: (k, j))]
    args = [a, b]
    if has_res:
        in_specs.append(pl.BlockSpec((tm, tn), lambda i, j, k: (i, j)))
        args.append(res)
    if has_bias:
        in_specs.append(pl.BlockSpec((1, tn), lambda i, j, k: (0, j)))
        args.append(bias)
    if after is not None:
        in_specs.append(pl.BlockSpec((8, 128), lambda i, j, k: (0, 0)))
        args.append(after)
    return pl.pallas_call(
        body, name=name,
        out_shape=jax.ShapeDtypeStruct((M, N), out_dtype),
        grid=(M // tm, N // tn, nk),
        in_specs=in_specs,
        out_specs=pl.BlockSpec((tm, tn), lambda i, j, k: (i, j)),
        scratch_shapes=[pltpu.VMEM((tm, tn), F32)] if nk > 1 else [],
        compiler_params=_params(("parallel", "parallel", "arbitrary")),
    )(*args)


def _ln_fwd(z, g, b, name, tq=512, after=None):
    T, D = z.shape

    def body(z_ref, g_ref, b_ref, *rest):
        y_ref, yb_ref = rest[-2:]
        zv = z_ref[...]
        mu = jnp.mean(zv, axis=1, keepdims=True)
        zc = zv - mu
        var = jnp.mean(zc * zc, axis=1, keepdims=True)
        y = zc * lax.rsqrt(var + LN_EPS) * g_ref[...] + b_ref[...]
        y_ref[...] = y
        yb_ref[...] = y.astype(BF16)

    row = pl.BlockSpec((tq, D), lambda i: (i, 0))
    vec = pl.BlockSpec((1, D), lambda i: (0, 0))
    return pl.pallas_call(
        body, name=name,
        out_shape=(jax.ShapeDtypeStruct((T, D), F32), jax.ShapeDtypeStruct((T, D), BF16)),
        grid=(T // tq,),
        in_specs=[row, vec, vec] + ([pl.BlockSpec((8, 128), lambda i: (0, 0))] if after is not None else []),
        out_specs=(row, row),
        compiler_params=_params(("parallel",)),
    )(z, g, b, *([after] if after is not None else []))


def _ln_bwd(dy, z, g, name, tq=512, target=None, for_matmul=True):
    T, D = z.shape
    with_loss = target is not None

    def body(*refs):
        dy_ref, z_ref, g_ref = refs[:3]
        outs = list(refs[4 if with_loss else 3:])
        dz_ref = outs.pop(0)
        dzb_ref, ds_ref = (outs.pop(0), outs.pop(0)) if for_matmul else (None, None)
        dg_ref, db_ref = outs.pop(0), outs.pop(0)
        sq_ref = outs.pop(0) if with_loss else None

        @pl.when(pl.program_id(0) == 0)
        def _():
            for ref in (dg_ref, db_ref, ds_ref, sq_ref):
                if ref is not None:
                    ref[...] = jnp.zeros_like(ref)

        zv, dyv = z_ref[...], dy_ref[...]
        if with_loss:
            err = dyv - refs[3][...]
            sq_ref[...] += jnp.sum(err * err)
            dyv = err * (1.0 / D)
        mu = jnp.mean(zv, axis=1, keepdims=True)
        zc = zv - mu
        var = jnp.mean(zc * zc, axis=1, keepdims=True)
        rstd = lax.rsqrt(var + LN_EPS)
        xh = zc * rstd
        u = dyv * g_ref[...]
        dz = rstd * (u - jnp.mean(u, axis=1, keepdims=True) - xh * jnp.mean(u * xh, axis=1, keepdims=True))
        dz_ref[...] = dz
        dg_ref[...] += jnp.sum(dyv * xh, axis=0, keepdims=True)
        db_ref[...] += jnp.sum(dyv, axis=0, keepdims=True)
        if for_matmul:
            dzb_ref[...] = dz.astype(BF16)
            ds_ref[...] += jnp.sum(dz, axis=0, keepdims=True)

    row = pl.BlockSpec((tq, D), lambda i: (i, 0))
    vec = pl.BlockSpec((1, D), lambda i: (0, 0))
    vshape = jax.ShapeDtypeStruct((1, D), F32)
    out_shape, out_specs = [jax.ShapeDtypeStruct((T, D), F32)], [row]
    if for_matmul:
        out_shape += [jax.ShapeDtypeStruct((T, D), BF16), vshape]
        out_specs += [row, vec]
    out_shape += [vshape, vshape]
    out_specs += [vec, vec]
    if with_loss:
        out_shape.append(jax.ShapeDtypeStruct((8, 128), F32))
        out_specs.append(pl.BlockSpec((8, 128), lambda i: (0, 0)))
    return pl.pallas_call(
        body, name=name, out_shape=tuple(out_shape), grid=(T // tq,),
        in_specs=[row, row, vec] + ([row] if with_loss else []), out_specs=tuple(out_specs),
        compiler_params=_params(("arbitrary",)),
    )(dy, z, g, *([target] if with_loss else []))


def _pblock(tq, width, offset):
    assert offset % width == 0
    blk = offset // width
    return pl.BlockSpec((tq, width), lambda i: (i, blk))


def _mix_fwd(proj, q_g, kv_g, w_pool, pool_scale, conv_w, name, tq=256):
    T = proj.shape[0]

    def body(ql_ref, kvl_ref, pin_ref, gp_ref, ch_ref, cb_ref, cc_ref, gc_ref, qg_ref, kvg_ref, wp_ref, ps_ref,
             cw_ref, qn_ref, kvn_ref, pooled_ref, cv_ref, ypc_ref, extp, extu):
        i = pl.program_id(0)
        for x_ref, g_ref, o_ref in ((ql_ref, qg_ref, qn_ref), (kvl_ref, kvg_ref, kvn_ref)):
            x = x_ref[...]
            r = lax.rsqrt(jnp.mean(x * x, axis=1, keepdims=True) + RMS_EPS)
            o_ref[...] = (x * r * g_ref[...]).astype(BF16)

        @pl.when(i == 0)
        def _():
            extp[0:HALO, :] = jnp.zeros((HALO, D_POOL), F32)
            extu[0:HALO, :] = jnp.zeros((HALO, D_CONV), F32)

        @pl.when(i > 0)
        def _():
            extp[0:HALO, :] = extp[tq:tq + HALO, :]
            extu[0:HALO, :] = extu[tq:tq + HALO, :]

        pin = pin_ref[...]
        extp[HALO:, :] = pin
        u = cc_ref[...] * ch_ref[...]
        extu[HALO:, :] = u
        t1 = (i * tq + lax.broadcasted_iota(jnp.int32, (tq, 1), 0) + 1).astype(F32)
        for g, w in enumerate(POOL_WINDOWS):
            cols = slice(g * POOL_GROUP, (g + 1) * POOL_GROUP)
            s = extp[:, cols]
            k = 1
            while k < w:
                s = s + pltpu.roll(s, k, 0)
                k *= 2
            mean = s[HALO:, :] / jnp.minimum(t1, float(w))
            pooled = (mean - pin[:, cols]).astype(BF16)
            pooled_ref[:, cols] = pooled
            r = jnp.dot(pooled, wp_ref[g], preferred_element_type=F32)
            gp = gp_ref[:, cols]
            ypc_ref[:, cols] = (r * ps_ref[:, cols] * (gp * _sigmoid(gp))).astype(BF16)
        eu = extu[...]
        u1 = pltpu.roll(eu, 1, 0)[HALO:, :]
        u2 = pltpu.roll(eu, 2, 0)[HALO:, :]
        cv = cw_ref[0:1, :] * u2 + cw_ref[1:2, :] * u1 + cw_ref[2:3, :] * u
        cv_ref[...] = cv
        gc = gc_ref[...]
        ypc_ref[:, D_POOL:] = (cb_ref[...] * cv * (gc * _sigmoid(gc))).astype(BF16)

    full = lambda shape: pl.BlockSpec(shape, lambda i: (0,) * len(shape))
    row = lambda w: pl.BlockSpec((tq, w), lambda i: (i, 0))
    return pl.pallas_call(
        body, name=name,
        out_shape=(jax.ShapeDtypeStruct((T, Q_LORA), BF16), jax.ShapeDtypeStruct((T, KV_LORA), BF16),
                   jax.ShapeDtypeStruct((T, D_POOL), BF16), jax.ShapeDtypeStruct((T, D_CONV), F32),
                   jax.ShapeDtypeStruct((T, D_MIX), BF16)),
        grid=(T // tq,),
        in_specs=[_pblock(tq, Q_LORA, O_QLAT), _pblock(tq, KV_LORA, O_KVLAT), _pblock(tq, 512, O_PIN),
                  _pblock(tq, 512, O_GPOOL), _pblock(tq, 512, O_CH), _pblock(tq, 512, O_CB), _pblock(tq, 512, O_CC),
                  _pblock(tq, 512, O_GCONV), full((1, Q_LORA)), full((1, KV_LORA)), full((4, 128, 128)),
                  full((1, D_POOL)), full((8, D_CONV))],
        out_specs=(row(Q_LORA), row(KV_LORA), row(D_POOL), row(D_CONV),
                   pl.BlockSpec((tq, D_POOL + D_CONV), lambda i: (i, D_MLA // (D_POOL + D_CONV)))),
        scratch_shapes=[pltpu.VMEM((tq + HALO, D_POOL), F32), pltpu.VMEM((tq + HALO, D_CONV), F32)],
        compiler_params=_params(("arbitrary",)),
    )(proj, proj, proj, proj, proj, proj, proj, proj, q_g, kv_g, w_pool, pool_scale, conv_w)


def _mix_bwd(dmix, proj, o, pooled, cv, w_pool, pool_scale, conv_w, name, tq=ATT_CH):
    T = proj.shape[0]
    nt = T // tq
    n_ext = tq + HALO

    def body(dym_ref, dyp_ref, dyc_ref, gm_ref, gp_ref, ch_ref, cb_ref, cc_ref, gc_ref, o_ref, pooled_ref, cv_ref,
             wp_ref, ps_ref, cw_ref, do_ref, delta_ref, dg_ref, dwp_ref, dps_ref, dcw_ref, exte, extd):
        i = pl.program_id(0)
        tile = nt - 1 - i

        @pl.when(i == 0)
        def _():
            dwp_ref[...] = jnp.zeros_like(dwp_ref)
            dps_ref[...] = jnp.zeros_like(dps_ref)
            dcw_ref[...] = jnp.zeros_like(dcw_ref)
            exte[tq:, :] = jnp.zeros((HALO, D_POOL), F32)
            extd[tq:, :] = jnp.zeros((HALO, D_CONV), F32)

        @pl.when(i > 0)
        def _():
            exte[tq:, :] = exte[0:HALO, :]
            extd[tq:, :] = extd[0:HALO, :]

        gm = gm_ref[...]
        sig = _sigmoid(gm)
        dym = dym_ref[...]
        ov = o_ref[...]
        do = dym * (gm * sig)
        do_ref[...] = do.astype(BF16)
        prod = do * ov
        ones = jnp.ones((8, V_DIM), F32)
        for h in range(N_HEADS):
            rows = lax.dot_general(ones, prod[:, h * V_DIM:(h + 1) * V_DIM], _DIMS["nt"],
                                   precision=lax.Precision.HIGHEST, preferred_element_type=F32)
            delta_ref[h, 0] = rows[0:1, :]
        dg_ref[:, O_GMLA:O_PIN] = (dym * ov * (sig * (1.0 + gm * (1.0 - sig)))).astype(BF16)

        t1 = (tile * tq + lax.broadcasted_iota(jnp.int32, (tq, 1), 0) + 1).astype(F32)
        for g, w in enumerate(POOL_WINDOWS):
            cols = slice(g * POOL_GROUP, (g + 1) * POOL_GROUP)
            pg = pooled_ref[:, cols]
            r = jnp.dot(pg, wp_ref[g], preferred_element_type=F32)
            gp = gp_ref[:, cols]
            sg = _sigmoid(gp)
            sl = gp * sg
            dyg = dyp_ref[:, cols]
            ps = ps_ref[:, cols]
            dg_ref[:, O_GPOOL + g * POOL_GROUP:O_GPOOL + (g + 1) * POOL_GROUP] = (
                dyg * (r * ps) * (sg * (1.0 + gp * (1.0 - sg)))).astype(BF16)
            dps_ref[:, cols] += jnp.sum(dyg * r * sl, axis=0, keepdims=True)
            dr = (dyg * ps * sl).astype(BF16)
            dwp_ref[g] += lax.dot_general(pg, dr, _DIMS["tn"], preferred_element_type=F32)
            dpooled = lax.dot_general(dr, wp_ref[g], _DIMS["nt"], preferred_element_type=F32)
            exte[0:tq, cols] = dpooled / jnp.minimum(t1, float(w))
            s = exte[:, cols]
            k = 1
            while k < w:
                s = s + pltpu.roll(s, n_ext - k, 0)
                k *= 2
            dg_ref[:, O_PIN + g * POOL_GROUP:O_PIN + (g + 1) * POOL_GROUP] = (s[0:tq, :] - dpooled).astype(BF16)

        gc = gc_ref[...]
        sg = _sigmoid(gc)
        sl = gc * sg
        dyc = dyc_ref[...]
        cb, cc, ch, cvv = cb_ref[...], cc_ref[...], ch_ref[...], cv_ref[...]
        dcv = dyc * cb * sl
        dg_ref[:, O_GCONV:O_GCONV + D_CONV] = (dyc * (cb * cvv) * (sg * (1.0 + gc * (1.0 - sg)))).astype(BF16)
        dg_ref[:, O_CB:O_CB + D_CONV] = (dyc * cvv * sl).astype(BF16)
        extd[0:tq, :] = dcv
        ed = extd[...]
        d1 = pltpu.roll(ed, n_ext - 1, 0)[0:tq, :]
        d2 = pltpu.roll(ed, n_ext - 2, 0)[0:tq, :]
        du = cw_ref[2:3, :] * dcv + cw_ref[1:2, :] * d1 + cw_ref[0:1, :] * d2
        u = cc * ch
        dcw_ref[0:1, :] += jnp.sum(u * d2, axis=0, keepdims=True)
        dcw_ref[1:2, :] += jnp.sum(u * d1, axis=0, keepdims=True)
        dcw_ref[2:3, :] += jnp.sum(u * dcv, axis=0, keepdims=True)
        dg_ref[:, O_CH:O_CH + D_CONV] = (du * cc).astype(BF16)
        dg_ref[:, O_CC:O_CC + D_CONV] = (du * ch).astype(BF16)

    def rblock(width, offset):
        assert offset % width == 0
        blk = offset // width
        return pl.BlockSpec((tq, width), lambda i: (nt - 1 - i, blk))

    full = lambda shape: pl.BlockSpec(shape, lambda i: (0,) * len(shape))
    return pl.pallas_call(
        body, name=name,
        out_shape=(jax.ShapeDtypeStruct((T, D_MLA), BF16), jax.ShapeDtypeStruct((N_HEADS, nt, 1, tq), F32),
                   jax.ShapeDtypeStruct((T, NPP), BF16),
                   jax.ShapeDtypeStruct((4, 128, 128), F32), jax.ShapeDtypeStruct((1, D_POOL), F32),
                   jax.ShapeDtypeStruct((8, D_CONV), F32)),
        grid=(nt,),
        in_specs=[rblock(1024, 0), rblock(512, 1024), rblock(512, 1536),
                  rblock(1024, O_GMLA), rblock(512, O_GPOOL), rblock(512, O_CH), rblock(512, O_CB),
                  rblock(512, O_CC), rblock(512, O_GCONV), rblock(1024, 0), rblock(512, 0), rblock(512, 0),
                  full((4, 128, 128)), full((1, D_POOL)), full((8, D_CONV))],
        out_specs=(rblock(1024, 0), pl.BlockSpec((N_HEADS, 1, 1, tq), lambda i: (0, nt - 1 - i, 0, 0)),
                   rblock(N_GATED, 0), full((4, 128, 128)), full((1, D_POOL)), full((8, D_CONV))),
        scratch_shapes=[pltpu.VMEM((n_ext, D_POOL), F32), pltpu.VMEM((n_ext, D_CONV), F32)],
        compiler_params=_params(("arbitrary",)),
    )(dmix, dmix, dmix, proj, proj, proj, proj, proj, proj, o, pooled, cv, w_pool, pool_scale, conv_w)


def _rms_bwd(proj, dqn, dkvn, dkrope, dproj, q_g, kv_g, name, tq=256):
    T = proj.shape[0]
    n_lat = NPP - N_GATED

    def body(ql_ref, kvl_ref, dqn_ref, dkvn_ref, dkr_ref, _, qg_ref, kvg_ref, dlat_ref, dqg_ref, dkvg_ref):
        @pl.when(pl.program_id(0) == 0)
        def _():
            dqg_ref[...] = jnp.zeros_like(dqg_ref)
            dkvg_ref[...] = jnp.zeros_like(dkvg_ref)

        for x_ref, dy_ref, g_ref, c0, dg_ref in ((ql_ref, dqn_ref, qg_ref, 0, dqg_ref),
                                                 (kvl_ref, dkvn_ref, kvg_ref, Q_LORA, dkvg_ref)):
            x, dy = x_ref[...], dy_ref[...]
            r = lax.rsqrt(jnp.mean(x * x, axis=1, keepdims=True) + RMS_EPS)
            xr = x * r
            u = dy * g_ref[...]
            dlat_ref[:, c0:c0 + x.shape[1]] = (r * (u - xr * jnp.mean(u * xr, axis=1, keepdims=True))).astype(BF16)
            dg_ref[...] += jnp.sum(dy * xr, axis=0, keepdims=True)
        dlat_ref[:, Q_LORA + KV_LORA:] = dkr_ref[...]

    row = lambda w: pl.BlockSpec((tq, w), lambda i: (i, 0))
    vec = lambda w: pl.BlockSpec((1, w), lambda i: (0, 0))
    assert N_GATED % n_lat == 0
    return pl.pallas_call(
        body, name=name,
        out_shape=(jax.ShapeDtypeStruct((T, NPP), BF16),
                   jax.ShapeDtypeStruct((1, Q_LORA), F32), jax.ShapeDtypeStruct((1, KV_LORA), F32)),
        grid=(T // tq,),
        in_specs=[_pblock(tq, Q_LORA, O_QLAT), _pblock(tq, KV_LORA, O_KVLAT), row(Q_LORA), row(KV_LORA),
                  row(n_lat - Q_LORA - KV_LORA), pl.BlockSpec(memory_space=pl.ANY), vec(Q_LORA), vec(KV_LORA)],
        out_specs=(pl.BlockSpec((tq, n_lat), lambda i: (i, N_GATED // n_lat)), vec(Q_LORA), vec(KV_LORA)),
        input_output_aliases={5: 0},
        compiler_params=_params(("arbitrary",)),
    )(proj, proj, dqn, dkvn, dkrope, dproj, q_g, kv_g)


def _swap_halves(x, lo):
    return jnp.where(lo, pltpu.roll(x, 96, 1), pltpu.roll(x, 32, 1))


def _rope_fwd(q, kv, proj, cos_t, sin_t, name, tq=256):
    T = q.shape[0]

    def body(qn_ref, qr_ref, kv_ref, kr_ref, c_ref, s_ref, qc_ref, kc_ref):
        C, S = c_ref[...], s_ref[...]
        lane = lax.broadcasted_iota(jnp.int32, (tq, 128), 1)
        lo = (lane % ROPE) < (ROPE // 2)
        first = lane < ROPE

        def rope(x):
            return x * C + _swap_halves(x, lo) * S

        kr = jnp.where(first, rope(kr_ref[...]), 0.0).astype(BF16)
        for j in range(N_HEADS // 2):
            r = rope(qr_ref[:, j * 128:(j + 1) * 128])
            pair = (jnp.where(first, r, 0.0), jnp.where(first, pltpu.roll(r, 64, 1), 0.0))
            for hh in range(2):
                h = 2 * j + hh
                qc_ref[h, :, 0:NOPE] = qn_ref[:, h * NOPE:(h + 1) * NOPE].astype(BF16)
                qc_ref[h, :, NOPE:QC] = pair[hh].astype(BF16)
        for h in range(N_HEADS):
            kc_ref[h, :, 0:NOPE] = kv_ref[:, h * 256:h * 256 + NOPE]
            kc_ref[h, :, NOPE:QC] = kr

    out = jax.ShapeDtypeStruct((N_HEADS, T, QC), BF16)
    hblock = pl.BlockSpec((N_HEADS, tq, QC), lambda i: (0, i, 0))
    return pl.pallas_call(
        body, name=name, out_shape=(out, out), grid=(T // tq,),
        in_specs=[pl.BlockSpec((tq, 1024), lambda i: (i, 0)), pl.BlockSpec((tq, 512), lambda i: (i, 2)),
                  pl.BlockSpec((tq, 2048), lambda i: (i, 0)), _pblock(tq, 128, O_KROPE),
                  pl.BlockSpec((tq, 128), lambda i: (i, 0)), pl.BlockSpec((tq, 128), lambda i: (i, 0))],
        out_specs=(hblock, hblock),
        compiler_params=_params(("parallel",)),
    )(q, q, kv, proj, cos_t, sin_t)


def _rope_bwd(dqc, dkr, cos_t, sin_t, name, tq=256):
    T = dqc.shape[1]

    def body(dqc_ref, dkr_ref, c_ref, s_ref, dq_ref, dk_ref):
        C, S = c_ref[...], s_ref[...]
        lane = lax.broadcasted_iota(jnp.int32, (tq, 128), 1)
        lo = (lane % ROPE) < (ROPE // 2)
        first = lane < ROPE

        def unrope(dy):
            return dy * C - _swap_halves(dy, lo) * S

        acc = dkr_ref[0]
        for h in range(1, N_HEADS):
            acc = acc + dkr_ref[h]
        dk_ref[:, 0:128] = jnp.where(first, unrope(acc), 0.0).astype(BF16)
        dk_ref[:, 128:256] = jnp.zeros((tq, 128), BF16)
        for j in range(N_HEADS // 2):
            d0 = dqc_ref[2 * j, :, NOPE:QC]
            d1 = dqc_ref[2 * j + 1, :, NOPE:QC]
            comb = jnp.where(first, d0, pltpu.roll(d1, 64, 1))
            dq_ref[:, 1024 + j * 128:1024 + (j + 1) * 128] = unrope(comb).astype(BF16)
        for h in range(N_HEADS):
            dq_ref[:, h * NOPE:(h + 1) * NOPE] = dqc_ref[h, :, 0:NOPE].astype(BF16)

    tab = pl.BlockSpec((tq, 128), lambda i: (i, 0))
    return pl.pallas_call(
        body, name=name,
        out_shape=(jax.ShapeDtypeStruct((T, 1536), BF16), jax.ShapeDtypeStruct((T, 256), BF16)),
        grid=(T // tq,),
        in_specs=[pl.BlockSpec((N_HEADS, tq, QC), lambda i: (0, i, 0)),
                  pl.BlockSpec((N_HEADS, tq, 128), lambda i: (0, i, 0)), tab, tab],
        out_specs=(pl.BlockSpec((tq, 1536), lambda i: (i, 0)), pl.BlockSpec((tq, 256), lambda i: (i, 0))),
        compiler_params=_params(("parallel",)),
    )(dqc, dkr, cos_t, sin_t)


def _flash_fwd(qc, kc, kv, proj, mix, name):
    H, T, _ = qc.shape
    tt = ATT_TILE
    nt = T // tt
    sp = tt // ATT_CH

    def body(q_ref, k_ref, v_ref, g_ref, _, o_ref, y_ref, lse_ref, vt_sc, s_sc, acc_sc, m_sc, l_sc):
        i = pl.program_id(1)

        @pl.when(i == 0)
        def _():
            for c in range(nt):
                vt_sc[c] = v_ref[c * tt:(c + 1) * tt, :].astype(F32).T.astype(BF16)

        q = q_ref[0]

        def issue(c, slot):
            s_sc[slot] = lax.dot_general(k_ref[0, pl.ds(pl.multiple_of(c * tt, tt), tt), :], q, _DIMS["nt"],
                                         preferred_element_type=F32)

        def softmax_pv(c, slot, masked):
            s = s_sc[slot]
            if masked:
                krow = c * tt + lax.broadcasted_iota(jnp.int32, s.shape, 0)
                qcol = i * tt + lax.broadcasted_iota(jnp.int32, s.shape, 1)
                s = jnp.where(krow <= qcol, s, -jnp.inf)
            m = m_sc[...]
            m_new = jnp.maximum(m, jnp.max(s, axis=0, keepdims=True))
            p = jnp.exp2((s - m_new) * EXP2_SCALE)
            a = jnp.exp2((m - m_new) * EXP2_SCALE)
            l_sc[...] = a * l_sc[...] + jnp.sum(p, axis=0, keepdims=True)
            acc_sc[...] = a * acc_sc[...] + jnp.dot(vt_sc[c], p.astype(BF16), preferred_element_type=F32)
            m_sc[...] = m_new

        issue(0, 0)
        m_sc[...] = jnp.full_like(m_sc, -jnp.inf)
        l_sc[...] = jnp.zeros_like(l_sc)
        acc_sc[...] = jnp.zeros_like(acc_sc)

        def pair(t, carry):
            issue(2 * t + 1, 1)
            softmax_pv(2 * t, 0, False)
            issue(2 * t + 2, 0)
            softmax_pv(2 * t + 1, 1, False)
            return carry

        lax.fori_loop(0, i // 2, pair, 0)

        @pl.when(i % 2 == 1)
        def _():
            issue(i, 1)
            softmax_pv(i - 1, 0, False)
            softmax_pv(i, 1, True)

        @pl.when(i % 2 == 0)
        def _():
            softmax_pv(i, 0, True)

        l = l_sc[...]
        o = (acc_sc[...] / l).T
        o_ref[...] = o
        lse = m_sc[...] * ATTN_SCALE + jnp.log(l)
        for r in range(sp):
            lse_ref[0, r] = lse[:, r * ATT_CH:(r + 1) * ATT_CH]
        g = g_ref[...]
        y_ref[...] = (o * (g * _sigmoid(g))).astype(BF16)

    return pl.pallas_call(
        body, name=name,
        out_shape=(jax.ShapeDtypeStruct((T, D_MLA), F32), jax.ShapeDtypeStruct((T, D_MIX), BF16),
                   jax.ShapeDtypeStruct((H, T // ATT_CH, 1, ATT_CH), F32)),
        grid=(H, nt),
        in_specs=[pl.BlockSpec((1, tt, QC), lambda h, i: (h, i, 0)),
                  pl.BlockSpec((1, T, QC), lambda h, i: (h, 0, 0)),
                  pl.BlockSpec((T, V_DIM), lambda h, i: (0, 2 * h + 1)),
                  pl.BlockSpec((tt, V_DIM), lambda h, i: (i, h)),
                  pl.BlockSpec(memory_space=pl.ANY)],
        input_output_aliases={4: 1},
        out_specs=(pl.BlockSpec((tt, V_DIM), lambda h, i: (i, h)),
                   pl.BlockSpec((tt, V_DIM), lambda h, i: (i, h)),
                   pl.BlockSpec((1, sp, 1, ATT_CH), lambda h, i: (h, i, 0, 0))),
        scratch_shapes=[pltpu.VMEM((nt, V_DIM, tt), BF16), pltpu.VMEM((2, tt, tt), F32),
                        pltpu.VMEM((V_DIM, tt), F32), pltpu.VMEM((1, tt), F32), pltpu.VMEM((1, tt), F32)],
        compiler_params=_params(("parallel", "arbitrary")),
    )(qc, kc, kv, proj, mix)


def _flash_bwd(qc, kc, kv, do, lse, delta, name):
    H, T, _ = qc.shape
    tt = ATT_TILE
    nt = T // tt
    sp = tt // ATT_CH

    def body(q_ref, k_ref, v_ref, do_ref, lse_ref, dl_ref, dq_ref, dkv_ref, dkr_ref, dqt_sc, dk_sc, dv_sc, s_sc,
             dp_sc, kt_sc):
        j = pl.program_id(1)

        @pl.when(j == 0)
        def _():
            dqt_sc[...] = jnp.zeros_like(dqt_sc)

        k = k_ref[0]
        v = v_ref[...]

        def operands(c):
            q0 = pl.multiple_of(c * tt, tt)
            return q_ref[0, pl.ds(q0, tt), :], do_ref[pl.ds(q0, tt), :]

        def stat_row(ref, c):
            return jnp.concatenate([ref[0, sp * c + r] for r in range(sp)], axis=1)

        def early(c, slot):
            q, dov = operands(c)
            s_sc[slot] = lax.dot_general(k, q, _DIMS["nt"], preferred_element_type=F32)
            dp_sc[slot] = lax.dot_general(v, dov, _DIMS["nt"], preferred_element_type=F32)

        def late(c, slot, masked, kt):
            q, dov = operands(c)
            s, dp = s_sc[slot], dp_sc[slot]
            if masked:
                krow = j * tt + lax.broadcasted_iota(jnp.int32, s.shape, 0)
                qcol = c * tt + lax.broadcasted_iota(jnp.int32, s.shape, 1)
                s = jnp.where(krow <= qcol, s, -jnp.inf)
            p = jnp.exp2(s * EXP2_SCALE - stat_row(lse_ref, c) * LOG2E)
            ds = (p * (dp - stat_row(dl_ref, c)) * ATTN_SCALE).astype(BF16)
            dv = jnp.dot(p.astype(BF16), dov, preferred_element_type=F32)
            dk = jnp.dot(ds, q, preferred_element_type=F32)
            if masked:
                dv_sc[slot] = dv
                dk_sc[slot] = dk
            else:
                dv_sc[slot] += dv
                dk_sc[slot] += dk
            dqt_sc[c] += jnp.dot(kt, ds, preferred_element_type=F32)

        n_rest = nt - 1 - j
        odd = n_rest % 2

        def setup(other_slot):
            dk_sc[other_slot] = jnp.zeros((tt, QC), F32)
            dv_sc[other_slot] = jnp.zeros((tt, V_DIM), F32)
            kt_sc[...] = k.astype(F32).T.astype(BF16)

        @pl.when(odd == 0)
        def _():
            early(j, 0)
            early(jnp.minimum(j + 1, nt - 1), 1)
            setup(1)
            late(j, 0, True, kt_sc[...])

        @pl.when(odd == 1)
        def _():
            early(j, 1)
            early(j + 1, 0)
            setup(0)
            kt = kt_sc[...]
            late(j, 1, True, kt)
            early(jnp.minimum(j + 2, nt - 1), 1)
            late(j + 1, 0, False, kt)

        def pair(u, carry):
            a = j + 1 + odd + 2 * u
            kt = kt_sc[...]
            early(a + 1, 0)
            late(a, 1, False, kt)
            early(jnp.minimum(a + 2, nt - 1), 1)
            late(a + 1, 0, False, kt)
            return carry

        lax.fori_loop(0, n_rest // 2, pair, 0)
        dk = dk_sc[0] + dk_sc[1]
        dkv_ref[:, 0:NOPE] = dk[:, 0:NOPE].astype(BF16)
        dkv_ref[:, NOPE:] = (dv_sc[0] + dv_sc[1]).astype(BF16)
        dkr_ref[0] = dk[:, NOPE:]

        @pl.when(j == nt - 1)
        def _():
            for c in range(nt):
                dq_ref[0, c * tt:(c + 1) * tt, :] = dqt_sc[c].T

    head = lambda h, j: (h, 0, 0)
    stat = pl.BlockSpec((1, T // ATT_CH, 1, ATT_CH), lambda h, j: (h, 0, 0, 0))
    return pl.pallas_call(
        body, name=name,
        out_shape=(jax.ShapeDtypeStruct((H, T, QC), F32), jax.ShapeDtypeStruct((T, 2 * D_MLA), BF16),
                   jax.ShapeDtypeStruct((H, T, 128), F32)),
        grid=(H, nt),
        in_specs=[pl.BlockSpec((1, T, QC), head),
                  pl.BlockSpec((1, tt, QC), lambda h, j: (h, j, 0)),
                  pl.BlockSpec((tt, V_DIM), lambda h, j: (j, 2 * h + 1)),
                  pl.BlockSpec((T, V_DIM), lambda h, j: (0, h)),
                  stat, stat],
        out_specs=(pl.BlockSpec((1, T, QC), head),
                   pl.BlockSpec((tt, 256), lambda h, j: (j, h)),
                   pl.BlockSpec((1, tt, 128), lambda h, j: (h, j, 0))),
        scratch_shapes=[pltpu.VMEM((nt, QC, tt), F32), pltpu.VMEM((2, tt, QC), F32), pltpu.VMEM((2, tt, V_DIM), F32),
                        pltpu.VMEM((2, tt, tt), F32), pltpu.VMEM((2, tt, tt), F32), pltpu.VMEM((QC, tt), BF16)],
        compiler_params=_params(("parallel", "arbitrary")),
    )(qc, kc, kv, do, lse, delta)


def _adamw(lands, w, m, v, name, rows, cols=None, first_layer=0, into=None):
    layers, R, C = w.shape
    L = len(lands)
    cols = C if cols is None else cols
    assert R % rows == 0 and C % cols == 0 and first_layer + L <= layers
    nc = C // cols
    nb = (R // rows) * nc
    c1 = 1.0 - ADAM_B1 ** ADAM_STEP
    c2 = 1.0 - ADAM_B2 ** ADAM_STEP

    def body(*refs):
        land_refs = refs[:L]
        w_ref, m_ref, v_ref = refs[L:L + 3]
        g_ref, d_ref, nm_ref, nv_ref, g_sc = refs[-5:]
        for ll in range(L):
            @pl.when(pl.program_id(0) == ll)
            def _(land_ref=land_refs[ll]):
                g = land_ref[0].astype(F32)
                for s in range(1, N_DEV):
                    g = g + land_ref[s].astype(F32)
                g_sc[...] = g

        g = g_sc[...]
        nm = ADAM_B1 * m_ref[0] + (1.0 - ADAM_B1) * g
        nv = ADAM_B2 * v_ref[0] + (1.0 - ADAM_B2) * (g * g)
        g_ref[0] = g
        nm_ref[0] = nm
        nv_ref[0] = nv
        d_ref[0] = -ADAM_LR * ((nm / c1) / (jnp.sqrt(nv / c2) + ADAM_EPS) + ADAM_WD * w_ref[0])

    def land_spec(ll):
        def index(l, i):
            i = jnp.where(l < ll, 0, jnp.where(l > ll, nb - 1, i))
            return (0, i // nc, i % nc)
        return pl.BlockSpec((N_DEV, rows, cols), index)

    blk = pl.BlockSpec((1, rows, cols), lambda l, i: (first_layer + l, i // nc, i % nc))
    out = jax.ShapeDtypeStruct((layers, R, C), F32)
    extra = [] if into is None else list(into)
    return pl.pallas_call(
        body, name=name, out_shape=(out, out, out, out), grid=(L, nb),
        in_specs=[land_spec(ll) for ll in range(L)] + [blk, blk, blk] + [pl.BlockSpec(memory_space=pl.ANY)] * len(extra),
        out_specs=(blk, blk, blk, blk),
        input_output_aliases={L + 3 + i: i for i in range(len(extra))},
        scratch_shapes=[pltpu.VMEM((rows, cols), F32)],
        compiler_params=_params(("arbitrary", "arbitrary")),
    )(*lands, w, m, v, *extra)


def _mesh_pos():
    return lax.axis_index("x"), lax.axis_index("y"), lax.axis_index("c")


def _all_gather(arrays, name):
    n = len(arrays)

    def body(*refs):
        ins, outs = refs[:n], refs[n:2 * n]
        send_sems, recv_sems, local_sems = refs[2 * n:]
        x, y, c = _mesh_pos()
        me, sibling = (x, y, c), (x, y, 1 - c)
        chips = [(1 - x, y), (x, 1 - y), (1 - x, 1 - y)]

        def slot(a, pos):
            px, py, pc = pos
            return outs[a].at[4 * px + 2 * py + pc]

        def copy(a, k, block, to, src=None):
            return pltpu.make_async_remote_copy(
                src_ref=slot(a, block) if src is None else src, dst_ref=slot(a, block),
                send_sem=send_sems.at[a * 7 + k], recv_sem=recv_sems.at[a * 7 + k],
                device_id=to, device_id_type=MESH_ID)

        mine, first, passed = [], [], []
        for a in range(n):
            cp = pltpu.make_async_copy(ins[a], slot(a, me), local_sems.at[a])
            cp.start()
            mine.append(cp)
            cps = [copy(a, 0, me, sibling, src=ins[a])]
            cps += [copy(a, 1 + j, me, (*chip, c), src=ins[a]) for j, chip in enumerate(chips)]
            for cp in cps:
                cp.start()
            first += cps
        for j, chip in enumerate(chips):
            for a in range(n):
                copy(a, 1 + j, (*chip, c), me).wait_recv()
                cp = copy(a, 4 + j, (*chip, c), sibling)
                cp.start()
                passed.append(cp)
        for a in range(n):
            copy(a, 0, sibling, me).wait_recv()
            for j, chip in enumerate(chips):
                copy(a, 4 + j, (*chip, 1 - c), me).wait_recv()
        for cp in first + passed:
            cp.wait_send()
        for cp in mine:
            cp.wait()

    hbm = pl.BlockSpec(memory_space=pltpu.HBM)
    return pl.pallas_call(
        body, name=name,
        out_shape=tuple(jax.ShapeDtypeStruct((N_DEV,) + a.shape, a.dtype) for a in arrays),
        in_specs=[hbm] * n, out_specs=tuple([hbm] * n),
        scratch_shapes=[pltpu.SemaphoreType.DMA((7 * n,)), pltpu.SemaphoreType.DMA((7 * n,)),
                        pltpu.SemaphoreType.DMA((n,))],
    )(*arrays)


_HBM = pl.BlockSpec(memory_space=pltpu.HBM)
_SEM = pl.BlockSpec(memory_space=pltpu.SEMAPHORE)
_EFFECT = pltpu.SideEffectType.DATAFLOW_SIDE_EFFECTING
N_PEERS = N_DEV - 1


def _peer(k):
    x, y, c = _mesh_pos()
    return (1 - x if k & 4 else x, 1 - y if k & 2 else y, 1 - c if k & 1 else c)


def _split_start(srcs, scatter, after, name):
    n = len(srcs)
    zones = [jax.ShapeDtypeStruct(s.shape if scatter else ((N_DEV,) + s.shape), s.dtype) for s in srcs]

    def body(*refs):
        src, zone = refs[:n], refs[n:2 * n]
        outs = refs[2 * n + 1:]
        send, recv, token = outs[:n], outs[n:2 * n], outs[4 * n]
        x, y, c = _mesh_pos()
        my_idx = 4 * x + 2 * y + c
        for a in range(n):
            pltpu.make_async_copy(src[a].at[my_idx] if scatter else src[a],
                                  zone[a].at[N_PEERS] if scatter else zone[a].at[my_idx], recv[a]).start()
            for k in range(1, N_DEV):
                px, py, pc = _peer(k)
                pltpu.make_async_remote_copy(
                    src_ref=src[a].at[4 * px + 2 * py + pc] if scatter else src[a],
                    dst_ref=zone[a].at[k - 1] if scatter else zone[a].at[my_idx],
                    send_sem=send[a], recv_sem=recv[a], device_id=(px, py, pc), device_id_type=MESH_ID).start()
        token[...] = jnp.zeros_like(token)

    hbm = lambda a: pltpu.with_memory_space_constraint(a, pltpu.HBM)
    outs = pl.pallas_call(
        body, name=name,
        out_shape=tuple([pltpu.SemaphoreType.DMA(())] * (2 * n)
                        + [pltpu.HBM(s.shape, s.dtype) for s in srcs]
                        + [pltpu.HBM(z.shape, z.dtype) for z in zones]
                        + [jax.ShapeDtypeStruct((8, 128), F32)]),
        in_specs=[_HBM] * (2 * n) + [pl.BlockSpec(memory_space=pl.ANY)],
        out_specs=tuple([_SEM] * (2 * n) + [_HBM] * (2 * n) + [pl.BlockSpec(memory_space=pltpu.VMEM)]),
        input_output_aliases={**{a: 2 * n + a for a in range(n)}, **{n + a: 3 * n + a for a in range(n)}},
        compiler_params=pltpu.CompilerParams(has_side_effects=_EFFECT),
    )(*[hbm(s) for s in srcs], *[hbm(lax.empty(z.shape, z.dtype)) for z in zones], after)
    return outs[:n], outs[n:2 * n], outs[2 * n:3 * n], outs[3 * n:4 * n], outs[4 * n]


def _split_wait(send, recv, srcs, zones, after, name):
    n = len(srcs)

    def body(*refs):
        zone = refs[n:2 * n]
        send_sems, recv_sems = refs[2 * n:3 * n], refs[3 * n:4 * n]
        x, y, c = _mesh_pos()
        for a in range(n):
            seven = zone[a].at[pl.ds(0, N_PEERS)]
            pltpu.make_async_remote_copy(src_ref=seven, dst_ref=seven, send_sem=send_sems[a], recv_sem=recv_sems[a],
                                         device_id=(x, y, 1 - c), device_id_type=MESH_ID).wait_send()
            pltpu.make_async_remote_copy(src_ref=zone[a], dst_ref=zone[a], send_sem=send_sems[a],
                                         recv_sem=recv_sems[a], device_id=(x, y, 1 - c),
                                         device_id_type=MESH_ID).wait_recv()

    outs = pl.pallas_call(
        body, name=name,
        out_shape=tuple([pltpu.HBM(s.shape, s.dtype) for s in srcs] + [pltpu.HBM(z.shape, z.dtype) for z in zones]),
        in_specs=[_HBM] * (2 * n) + [_SEM] * (2 * n) + [pl.BlockSpec(memory_space=pl.ANY)],
        out_specs=tuple([_HBM] * (2 * n)),
        input_output_aliases={a: a for a in range(2 * n)},
        compiler_params=pltpu.CompilerParams(has_side_effects=_EFFECT),
    )(*srcs, *zones, *send, *recv, after)
    return outs[:n], outs[n:]


def _cat_blocks(g, axis):
    return jnp.concatenate([g[d] for d in range(N_DEV)], axis=axis)


N_LATENT = Q_LORA + KV_LORA + ROPE
W_SHARD = D_IN_PROJ // N_DEV


def _ref_cols(lo, hi):
    out = []
    if lo < N_LATENT:
        out.append((N_GATED + lo, N_GATED + min(hi, N_LATENT)))
    if hi > N_LATENT:
        out.append((max(lo, N_LATENT) - N_LATENT, hi - N_LATENT))
    return out


def _permute_w_in_t(blocks):
    pieces = []
    for lo, hi in ((N_LATENT, D_IN_PROJ), (0, N_LATENT)):
        for d in range(N_DEV):
            a, b = max(lo, d * W_SHARD), min(hi, (d + 1) * W_SHARD)
            if a < b:
                pieces.append(blocks[d][a - d * W_SHARD:b - d * W_SHARD])
    pieces.append(jnp.zeros((NPP - D_IN_PROJ, blocks.shape[2]), blocks.dtype))
    return jnp.concatenate(pieces, axis=0)


def _split_w_in_t(w):
    slabs = []
    for d in range(N_DEV):
        parts = [w[a:b] for a, b in _ref_cols(d * W_SHARD, (d + 1) * W_SHARD)]
        slabs.append(parts[0] if len(parts) == 1 else jnp.concatenate(parts, axis=0))
    return jnp.stack(slabs)


def _permute_w_uq(w):
    w3 = w.reshape(w.shape[0], N_HEADS, NOPE + ROPE)
    return jnp.concatenate([w3[:, :, :NOPE].reshape(w.shape[0], -1), w3[:, :, NOPE:].reshape(w.shape[0], -1)], axis=1)


def _unpermute_w_uq(w):
    nope = w[:, :N_HEADS * NOPE].reshape(w.shape[0], N_HEADS, NOPE)
    rope = w[:, N_HEADS * NOPE:].reshape(w.shape[0], N_HEADS, ROPE)
    return jnp.concatenate([nope, rope], axis=2).reshape(w.shape[0], -1)


_SMALL_EMB = (("emb_ln_g", 16), ("emb_ln_b", 16))
_SMALL_LAYER = (("q_norm_g", 8), ("kv_norm_g", 8), ("w_pool", 1024), ("pool_scale", 8), ("b_out", 32),
                ("ln_g", 32), ("ln_b", 32))
_SMALL = _SMALL_EMB + _SMALL_LAYER
CONV_ROWS = DEPTH * CONV_WIDTH * D_CONV // 128


def _pack_small(d, entries=_SMALL):
    parts = []
    for name, rows in entries:
        flat = d[name].reshape(-1)
        flat = jnp.pad(flat, (0, rows * 128 - flat.shape[0]))
        parts.append(flat.reshape(rows, 128))
    return jnp.concatenate(parts, axis=0)


def _unpack_small(packed, shapes):
    out, r0 = {}, 0
    for name, rows in _SMALL:
        size = 1
        for s in shapes[name]:
            size *= s
        out[name] = packed[r0:r0 + rows].reshape(-1)[:size].reshape(shapes[name])
        r0 += rows
    return out


def _rope_tables(positions):
    half = ROPE // 2
    inv_freq = ROPE_THETA ** (-jnp.arange(half, dtype=F32) / half)
    ang = positions.astype(F32)[:, None] * inv_freq
    cos, sin = jnp.cos(ang), jnp.sin(ang)
    return jnp.concatenate([cos, cos, cos, cos], axis=1), jnp.concatenate([-sin, sin, -sin, sin], axis=1)


def _local_step(x, positions, target, emb_g, emb_b, layer_weights, layer_weights_rest, on_sharded_grads,
                on_layer_grads=None, first_after=None):
    cos_t, sin_t = _rope_tables(positions)
    h, hb = _ln_fwd(x, emb_g, emb_b, "emb_ln_fwd", after=first_after)
    saved = []
    for l in range(DEPTH):
        W = layer_weights(l, h)
        proj = _mm(hb, W["w_in_t"], "nt", F32, "proj_fwd")
        qn, kvn, pooled, cv, mix = _mix_fwd(proj, W["q_norm_g"], W["kv_norm_g"], W["w_pool"], W["pool_scale"],
                                            W["conv_w"], "mix_fwd")
        rest, token = layer_weights_rest(l, proj)
        W = {**W, **rest}
        q = _mm(qn, W["w_uq"], "nn", F32, "q_up_fwd", after=token)
        kv = _mm(kvn, W["w_ukv"], "nn", BF16, "kv_up_fwd")
        qc, kc = _rope_fwd(q, kv, proj, cos_t, sin_t, "rope_fwd")
        o, mix, lse = _flash_fwd(qc, kc, kv, proj, mix, "flash_fwd")
        z = _mm(mix, W["w_out"], "nn", F32, "out_fwd", res=h, bias=W["b_out"], alpha=ALPHA)
        saved.append((W, hb, proj, qn, kvn, pooled, cv, kv, qc, kc, o, lse, mix, z))
        h, hb = _ln_fwd(z, W["ln_g"], W["ln_b"], "ln_fwd")
    dh, sq = h, None

    grads = {k: [None] * DEPTH for k in ("q_norm_g", "kv_norm_g", "w_pool", "pool_scale", "conv_w", "b_out", "ln_g",
                                         "ln_b")}
    for l in reversed(range(DEPTH)):
        W, hb_in, proj, qn, kvn, pooled, cv, kv, qc, kc, o, lse, mix, z = saved[l]
        sharded = {}
        if l == DEPTH - 1:
            dz, dzb, grads["b_out"][l], grads["ln_g"][l], grads["ln_b"][l], sq = _ln_bwd(
                dh, z, W["ln_g"], "ln_bwd_loss", target=target)
        else:
            dz, dzb, grads["b_out"][l], grads["ln_g"][l], grads["ln_b"][l] = _ln_bwd(dh, z, W["ln_g"], "ln_bwd")
        dmix = _mm(dzb, W["w_out"], "nt", F32, "out_bwd_x")
        sharded["w_out"] = _mm(mix, dzb, "tn", GRAD_XFER, "out_bwd_w", tk=4096)
        do, delta, dproj, grads["w_pool"][l], grads["pool_scale"][l], grads["conv_w"][l] = _mix_bwd(
            dmix, proj, o, pooled, cv, W["w_pool"], W["pool_scale"], W["conv_w"], "mix_bwd")
        dqc, dkv, dkr = _flash_bwd(qc, kc, kv, do, lse, delta, "flash_bwd")
        dq, dkrope = _rope_bwd(dqc, dkr, cos_t, sin_t, "rope_bwd")
        dqn = _mm(dq, W["w_uq"], "nt", F32, "q_up_bwd_x")
        sharded["w_uq"] = _mm(qn, dq, "tn", GRAD_XFER, "q_up_bwd_w")
        dkvn = _mm(dkv, W["w_ukv"], "nt", F32, "kv_up_bwd_x")
        sharded["w_ukv"] = _mm(kvn, dkv, "tn", GRAD_XFER, "kv_up_bwd_w")
        token = on_sharded_grads(l, sharded)
        dproj, grads["q_norm_g"][l], grads["kv_norm_g"][l] = _rms_bwd(
            proj, dqn, dkvn, dkrope, dproj, W["q_norm_g"], W["kv_norm_g"], "rms_bwd")
        if l == 0 and on_layer_grads is not None:
            token = on_layer_grads(grads, token)
        d_w_in_t = _mm(dproj, hb_in, "tn", GRAD_XFER, "proj_bwd_w", tk=4096, after=token)
        token = on_sharded_grads(l, {"w_in": d_w_in_t})
        dh = _mm(dproj, W["w_in_t"], "nn", F32, "proj_bwd_x", res=dz, alpha=ALPHA, tk=2560, after=token)
    grad_x, grads["emb_ln_g"], grads["emb_ln_b"] = _ln_bwd(dh, x, emb_g, "emb_ln_bwd", for_matmul=False)
    return sq, grad_x, grads


def kernel(x, positions, emb_ln_g, emb_ln_b, w_in, q_norm_g, kv_norm_g, w_uq, w_ukv, w_pool, pool_scale, conv_w, w_out, b_out, ln_g, ln_b, loss_target, m_emb_ln_g, m_emb_ln_b, m_w_in, m_q_norm_g, m_kv_norm_g, m_w_uq, m_w_ukv, m_w_pool, m_pool_scale, m_conv_w, m_w_out, m_b_out, m_ln_g, m_ln_b, v_emb_ln_g, v_emb_ln_b, v_w_in, v_q_norm_g, v_kv_norm_g, v_w_uq, v_w_ukv, v_w_pool, v_pool_scale, v_conv_w, v_w_out, v_b_out, v_ln_g, v_ln_b):
    weights = dict(emb_ln_g=emb_ln_g, emb_ln_b=emb_ln_b, w_in=w_in, q_norm_g=q_norm_g, kv_norm_g=kv_norm_g,
                   w_uq=w_uq, w_ukv=w_ukv, w_pool=w_pool, pool_scale=pool_scale, conv_w=conv_w, w_out=w_out,
                   b_out=b_out, ln_g=ln_g, ln_b=ln_b)
    mom1 = dict(emb_ln_g=m_emb_ln_g, emb_ln_b=m_emb_ln_b, w_in=m_w_in, q_norm_g=m_q_norm_g, kv_norm_g=m_kv_norm_g,
                w_uq=m_w_uq, w_ukv=m_w_ukv, w_pool=m_w_pool, pool_scale=m_pool_scale, conv_w=m_conv_w,
                w_out=m_w_out, b_out=m_b_out, ln_g=m_ln_g, ln_b=m_ln_b)
    mom2 = dict(emb_ln_g=v_emb_ln_g, emb_ln_b=v_emb_ln_b, w_in=v_w_in, q_norm_g=v_q_norm_g, kv_norm_g=v_kv_norm_g,
                w_uq=v_w_uq, w_ukv=v_w_ukv, w_pool=v_w_pool, pool_scale=v_pool_scale, conv_w=v_conv_w,
                w_out=v_w_out, b_out=v_b_out, ln_g=v_ln_g, ln_b=v_ln_b)

    big = ("w_in", "w_uq", "w_ukv", "w_out")

    conv_pad = jnp.zeros((8, 128), F32).at[0:DEPTH * CONV_WIDTH, 0:64].set(conv_w.reshape(DEPTH * CONV_WIDTH, 64))
    t12 = lambda a: jnp.swapaxes(a, 1, 2)
    shard = lambda k, l: (t12(weights[k])[l] if k == "w_in" else weights[k][l]).astype(BF16)
    w_in0, conv_all = _all_gather([shard("w_in", 0), conv_pad], "w_in0_all_gather")
    rest0 = _split_start([shard(k, 0) for k in big[1:]], False, w_in0, "weights0_rest_start")
    conv_full = _cat_blocks(conv_all[:, 0:DEPTH * CONV_WIDTH, 0:64], 1).reshape(DEPTH, CONV_WIDTH, D_CONV)
    conv_full = jnp.pad(conv_full, ((0, 0), (0, 8 - CONV_WIDTH), (0, 0)))
    fetched = {}

    def layer_weights(l, ready):
        if l == 0:
            w_in_blocks = w_in0
        else:
            fetched[1] = _split_wait(*fetched["w1"][:4], ready, "weights1_wait")[1]
            w_in_blocks = fetched[1][0]
        return dict(
            w_in_t=_permute_w_in_t(w_in_blocks), conv_w=conv_full[l],
            q_norm_g=q_norm_g[l].reshape(1, -1), kv_norm_g=kv_norm_g[l].reshape(1, -1),
            w_pool=w_pool[l].astype(BF16), pool_scale=pool_scale[l].reshape(1, -1), b_out=b_out[l].reshape(1, -1),
            ln_g=ln_g[l].reshape(1, -1), ln_b=ln_b[l].reshape(1, -1))

    def layer_weights_rest(l, ready):
        token = None
        if l == 0:
            blocks = _split_wait(*rest0[:4], ready, "weights0_rest_wait")[1]
            fetched["w1"] = _split_start([shard(k, 1) for k in big], False, blocks[0], "weights1_start")
            token = fetched["w1"][4]
        else:
            blocks = fetched[1][1:]
        return dict(w_uq=_permute_w_uq(_cat_blocks(blocks[0], 1)), w_ukv=_cat_blocks(blocks[1], 1),
                    w_out=_cat_blocks(blocks[2], 0)), token

    by_dest = dict(
        w_in=_split_w_in_t,
        w_uq=lambda g: _unpermute_w_uq(g).reshape(Q_LORA, N_DEV, -1).transpose(1, 0, 2),
        w_ukv=lambda g: g.reshape(KV_LORA, N_DEV, -1).transpose(1, 0, 2),
        w_out=lambda g: g.reshape(N_DEV, -1, D_MODEL))
    in_flight = []

    def on_sharded_grads(l, g):
        names = [k for k in big if k in g]
        srcs = [by_dest[k](g[k]) for k in names]
        started = _split_start(srcs, True, srcs[0], "grads%d_%s_start" % (l, names[0]))
        in_flight.append((l, names, started[:4]))
        return started[4]

    small_in_flight = []

    def on_layer_grads(g, token):
        stacked = {k: jnp.stack(g[k]) for k, _ in _SMALL_LAYER}
        conv = jnp.stack([g["conv_w"][l][0:CONV_WIDTH] for l in range(DEPTH)]).reshape(CONV_ROWS, 128)
        packed = jnp.concatenate([_pack_small(stacked, _SMALL_LAYER), conv], axis=0)
        started = _split_start([packed], False, token, "layer_grads_start")
        small_in_flight.append(started[:4])
        return started[4]

    sq, grad_x, G = _local_step(x[0], positions[0], loss_target[0], emb_ln_g.reshape(1, -1),
                                emb_ln_b.reshape(1, -1), layer_weights, layer_weights_rest, on_sharded_grads,
                                on_layer_grads, first_after=rest0[4])
    loss = lax.psum(sq[0, 0] * (0.5 / D_MODEL), ("x", "y", "c"))

    res = {}
    landed = {}
    for l, names, started in in_flight:
        zones = _split_wait(*started, grad_x, "grads%d_%s_wait" % (l, names[0]))[1]
        for k, zone in zip(names, zones):
            landed[k, l] = zone
    w_in_res = None
    for l in reversed(range(DEPTH)):
        w_in_res = _adamw([landed["w_in", l]], t12(w_in), t12(m_w_in), t12(v_w_in), "adamw_w_in_%d" % l, W_SHARD, 512,
                          first_layer=l, into=w_in_res)
    res["w_in"] = tuple(t12(o) for o in w_in_res)
    for name, rows in (("w_uq", 256), ("w_ukv", 256), ("w_out", 128)):
        res[name] = _adamw([landed[name, l] for l in range(DEPTH)], weights[name], mom1[name], mom2[name],
                           "adamw_" + name, rows)

    layer_zone = _split_wait(*small_in_flight[0], grad_x, "layer_grads_wait")[1][0]
    emb_zone = _all_gather([_pack_small(G, _SMALL_EMB)], "emb_grads_all_gather")[0]
    n_layer_rows = sum(r for _, r in _SMALL_LAYER)
    l_small = jnp.concatenate([emb_zone, layer_zone[:, 0:n_layer_rows]], axis=1)
    my_idx = 4 * lax.axis_index("x") + 2 * lax.axis_index("y") + lax.axis_index("c")
    conv_all_grads = layer_zone[:, n_layer_rows:].reshape(N_DEV, DEPTH * CONV_WIDTH, D_CONV)
    l_conv = lax.dynamic_slice_in_dim(conv_all_grads, my_idx * 64, 64, axis=2)
    l_conv = jnp.zeros((N_DEV, 8, 128), F32).at[:, 0:DEPTH * CONV_WIDTH, 0:64].set(l_conv)
    conv_shard = lambda a: jnp.zeros((8, 128), F32).at[0:DEPTH * CONV_WIDTH, 0:64].set(a.reshape(-1, 64))
    conv_res = _adamw([l_conv], conv_shard(conv_w)[None], conv_shard(m_conv_w)[None], conv_shard(v_conv_w)[None],
                      "adamw_conv_w", 8)
    res["conv_w"] = tuple(o[0, 0:DEPTH * CONV_WIDTH, 0:64].reshape(DEPTH, CONV_WIDTH, 64) for o in conv_res)
    small_res = _adamw([l_small], _pack_small(weights)[None], _pack_small(mom1)[None], _pack_small(mom2)[None],
                       "adamw_small", 392)
    shapes = {k: weights[k].shape for k, _ in _SMALL}
    unpacked = [_unpack_small(o[0], shapes) for o in small_res]
    for k, _ in _SMALL:
        res[k] = tuple(u[k] for u in unpacked)

    order = ("emb_ln_g", "emb_ln_b", "w_in", "q_norm_g", "kv_norm_g", "w_uq", "w_ukv", "w_pool", "pool_scale",
             "conv_w", "w_out", "b_out", "ln_g", "ln_b")
    return (loss, grad_x[None], *[res[k][0] for k in order], *[res[k][1] for k in order],
            *[res[k][2] for k in order], *[res[k][3] for k in order])
```

```python
import jax
import jax.numpy as jnp
from jax import lax
from jax.experimental import pallas as pl
from jax.experimental.pallas import tpu as pltpu

F32 = jnp.float32
BF16 = jnp.bfloat16

N_DEV = 8
D_MODEL = 2048
DEPTH = 2
N_HEADS = 8
NOPE = 128
ROPE = 64
V_DIM = 128
Q_LORA = 512
KV_LORA = 256
D_MLA = N_HEADS * V_DIM
D_POOL = 512
D_CONV = 512
POOL_WINDOWS = (2, 4, 8, 16)
POOL_GROUP = 128
CONV_WIDTH = 3
D_MIX = D_MLA + D_POOL + D_CONV
D_IN_PROJ = 4928
ROPE_THETA = 10000.0
LN_EPS = 1e-5
RMS_EPS = 1e-6
ALPHA = (2 * DEPTH) ** 0.25
ATTN_SCALE = (NOPE + ROPE) ** -0.5
ADAM_LR = 0.001
ADAM_B1 = 0.9
ADAM_B2 = 0.999
ADAM_EPS = 1e-08
ADAM_WD = 0.01
ADAM_STEP = 10

O_GMLA, O_PIN, O_GPOOL, O_CH, O_CB, O_CC, O_GCONV, O_QLAT, O_KVLAT, O_KROPE = (
    0, 1024, 1536, 2048, 2560, 3072, 3584, 4096, 4608, 4864)
NPP = 5120
N_GATED = O_QLAT
QC = NOPE + 2 * ROPE
HALO = 16
ATT_TILE = 512
ATT_CH = 256
LOG2E = 1.4426950408889634
EXP2_SCALE = ATTN_SCALE * LOG2E

GRAD_XFER = BF16
VMEM_LIMIT = 48 * 1024 * 1024
MESH_ID = pl.DeviceIdType.MESH


def _params(sem=None):
    return pltpu.CompilerParams(dimension_semantics=sem, vmem_limit_bytes=VMEM_LIMIT)


def _sigmoid(x):
    return 1.0 / (1.0 + jnp.exp(-x))


def _tile(dim, target):
    if dim <= target:
        return dim
    t = target - target % 128
    while dim % t:
        t -= 128
    return t


_DIMS = {"nn": (((1,), (0,)), ((), ())), "nt": (((1,), (1,)), ((), ())), "tn": (((0,), (0,)), ((), ()))}


def _mm(a, b, mode, out_dtype, name, res=None, bias=None, alpha=1.0, tm=1024, tn=1024, tk=2048, after=None):
    if mode == "nn":
        (M, K), (K2, N) = a.shape, b.shape
    elif mode == "nt":
        (M, K), (N, K2) = a.shape, b.shape
    else:
        (K, M), (K2, N) = a.shape, b.shape
    assert K == K2
    tm, tn, tk = _tile(M, tm), _tile(N, tn), _tile(K, tk)
    nk = K // tk
    has_res, has_bias = res is not None, bias is not None

    def body(*refs):
        a_ref, b_ref = refs[0], refs[1]
        pos = 2
        res_ref = bias_ref = None
        if has_res:
            res_ref = refs[pos]
            pos += 1
        if has_bias:
            bias_ref = refs[pos]
            pos += 1
        def finish(r, o_ref):
            if has_bias:
                r = r + bias_ref[...]
            if has_res:
                r = alpha * res_ref[...] + r
            o_ref[...] = r.astype(out_dtype)

        part = lax.dot_general(a_ref[...].astype(BF16), b_ref[...].astype(BF16), _DIMS[mode],
                               preferred_element_type=F32)
        if nk == 1:
            finish(part, refs[-1])
            return
        o_ref, acc_ref = refs[-2], refs[-1]
        k = pl.program_id(2)

        @pl.when(k == 0)
        def _():
            acc_ref[...] = part

        @pl.when(jnp.logical_and(k > 0, k < nk - 1))
        def _():
            acc_ref[...] += part

        @pl.when(k == nk - 1)
        def _():
            finish(acc_ref[...] + part, o_ref)

    if mode == "nn":
        in_specs = [pl.BlockSpec((tm, tk), lambda i, j, k: (i, k)), pl.BlockSpec((tk, tn), lambda i, j, k: (k, j))]
    elif mode == "nt":
        in_specs = [pl.BlockSpec((tm, tk), lambda i, j, k: (i, k)), pl.BlockSpec((tn, tk), lambda i, j, k: (j, k))]
    else:
        in_specs = [pl.BlockSpec((tk, tm), lambda i, j, k: (k, i)), pl.BlockSpec((tk, tn), lambda i, j, k: (k, j))]
    args = [a, b]
    if has_res:
        in_specs.append(pl.BlockSpec((tm, tn), lambda i, j, k: (i, j)))
        args.append(res)
    if has_bias:
        in_specs.append(pl.BlockSpec((1, tn), lambda i, j, k: (0, j)))
        args.append(bias)
    if after is not None:
        in_specs.append(pl.BlockSpec((8, 128), lambda i, j, k: (0, 0)))
        args.append(after)
    return pl.pallas_call(
        body, name=name,
        out_shape=jax.ShapeDtypeStruct((M, N), out_dtype),
        grid=(M // tm, N // tn, nk),
        in_specs=in_specs,
        out_specs=pl.BlockSpec((tm, tn), lambda i, j, k: (i, j)),
        scratch_shapes=[pltpu.VMEM((tm, tn), F32)] if nk > 1 else [],
        compiler_params=_params(("parallel", "parallel", "arbitrary")),
    )(*args)


def _ln_fwd(z, g, b, name, tq=512):
    T, D = z.shape

    def body(z_ref, g_ref, b_ref, y_ref, yb_ref):
        zv = z_ref[...]
        mu = jnp.mean(zv, axis=1, keepdims=True)
        zc = zv - mu
        var = jnp.mean(zc * zc, axis=1, keepdims=True)
        y = zc * lax.rsqrt(var + LN_EPS) * g_ref[...] + b_ref[...]
        y_ref[...] = y
        yb_ref[...] = y.astype(BF16)

    row = pl.BlockSpec((tq, D), lambda i: (i, 0))
    vec = pl.BlockSpec((1, D), lambda i: (0, 0))
    return pl.pallas_call(
        body, name=name,
        out_shape=(jax.ShapeDtypeStruct((T, D), F32), jax.ShapeDtypeStruct((T, D), BF16)),
        grid=(T // tq,), in_specs=[row, vec, vec], out_specs=(row, row),
        compiler_params=_params(("parallel",)),
    )(z, g, b)


def _ln_bwd(dy, z, g, name, tq=512, target=None, for_matmul=True):
    T, D = z.shape
    with_loss = target is not None

    def body(*refs):
        dy_ref, z_ref, g_ref = refs[:3]
        outs = list(refs[4 if with_loss else 3:])
        dz_ref = outs.pop(0)
        dzb_ref, ds_ref = (outs.pop(0), outs.pop(0)) if for_matmul else (None, None)
        dg_ref, db_ref = outs.pop(0), outs.pop(0)
        sq_ref = outs.pop(0) if with_loss else None

        @pl.when(pl.program_id(0) == 0)
        def _():
            for ref in (dg_ref, db_ref, ds_ref, sq_ref):
                if ref is not None:
                    ref[...] = jnp.zeros_like(ref)

        zv, dyv = z_ref[...], dy_ref[...]
        if with_loss:
            err = dyv - refs[3][...]
            sq_ref[...] += jnp.sum(err * err)
            dyv = err * (1.0 / D)
        mu = jnp.mean(zv, axis=1, keepdims=True)
        zc = zv - mu
        var = jnp.mean(zc * zc, axis=1, keepdims=True)
        rstd = lax.rsqrt(var + LN_EPS)
        xh = zc * rstd
        u = dyv * g_ref[...]
        dz = rstd * (u - jnp.mean(u, axis=1, keepdims=True) - xh * jnp.mean(u * xh, axis=1, keepdims=True))
        dz_ref[...] = dz
        dg_ref[...] += jnp.sum(dyv * xh, axis=0, keepdims=True)
        db_ref[...] += jnp.sum(dyv, axis=0, keepdims=True)
        if for_matmul:
            dzb_ref[...] = dz.astype(BF16)
            ds_ref[...] += jnp.sum(dz, axis=0, keepdims=True)

    row = pl.BlockSpec((tq, D), lambda i: (i, 0))
    vec = pl.BlockSpec((1, D), lambda i: (0, 0))
    vshape = jax.ShapeDtypeStruct((1, D), F32)
    out_shape, out_specs = [jax.ShapeDtypeStruct((T, D), F32)], [row]
    if for_matmul:
        out_shape += [jax.ShapeDtypeStruct((T, D), BF16), vshape]
        out_specs += [row, vec]
    out_shape += [vshape, vshape]
    out_specs += [vec, vec]
    if with_loss:
        out_shape.append(jax.ShapeDtypeStruct((8, 128), F32))
        out_specs.append(pl.BlockSpec((8, 128), lambda i: (0, 0)))
    return pl.pallas_call(
        body, name=name, out_shape=tuple(out_shape), grid=(T // tq,),
        in_specs=[row, row, vec] + ([row] if with_loss else []), out_specs=tuple(out_specs),
        compiler_params=_params(("arbitrary",)),
    )(dy, z, g, *([target] if with_loss else []))


def _pblock(tq, width, offset):
    assert offset % width == 0
    blk = offset // width
    return pl.BlockSpec((tq, width), lambda i: (i, blk))


def _mix_fwd(proj, q_g, kv_g, w_pool, pool_scale, conv_w, name, tq=256):
    T = proj.shape[0]

    def body(ql_ref, kvl_ref, pin_ref, gp_ref, ch_ref, cb_ref, cc_ref, gc_ref, qg_ref, kvg_ref, wp_ref, ps_ref,
             cw_ref, qn_ref, kvn_ref, pooled_ref, cv_ref, ypc_ref, extp, extu):
        i = pl.program_id(0)
        for x_ref, g_ref, o_ref in ((ql_ref, qg_ref, qn_ref), (kvl_ref, kvg_ref, kvn_ref)):
            x = x_ref[...]
            r = lax.rsqrt(jnp.mean(x * x, axis=1, keepdims=True) + RMS_EPS)
            o_ref[...] = (x * r * g_ref[...]).astype(BF16)

        @pl.when(i == 0)
        def _():
            extp[0:HALO, :] = jnp.zeros((HALO, D_POOL), F32)
            extu[0:HALO, :] = jnp.zeros((HALO, D_CONV), F32)

        @pl.when(i > 0)
        def _():
            extp[0:HALO, :] = extp[tq:tq + HALO, :]
            extu[0:HALO, :] = extu[tq:tq + HALO, :]

        pin = pin_ref[...]
        extp[HALO:, :] = pin
        u = cc_ref[...] * ch_ref[...]
        extu[HALO:, :] = u
        t1 = (i * tq + lax.broadcasted_iota(jnp.int32, (tq, 1), 0) + 1).astype(F32)
        for g, w in enumerate(POOL_WINDOWS):
            cols = slice(g * POOL_GROUP, (g + 1) * POOL_GROUP)
            s = extp[:, cols]
            k = 1
            while k < w:
                s = s + pltpu.roll(s, k, 0)
                k *= 2
            mean = s[HALO:, :] / jnp.minimum(t1, float(w))
            pooled = (mean - pin[:, cols]).astype(BF16)
            pooled_ref[:, cols] = pooled
            r = jnp.dot(pooled, wp_ref[g], preferred_element_type=F32)
            gp = gp_ref[:, cols]
            ypc_ref[:, cols] = (r * ps_ref[:, cols] * (gp * _sigmoid(gp))).astype(BF16)
        eu = extu[...]
        u1 = pltpu.roll(eu, 1, 0)[HALO:, :]
        u2 = pltpu.roll(eu, 2, 0)[HALO:, :]
        cv = cw_ref[0:1, :] * u2 + cw_ref[1:2, :] * u1 + cw_ref[2:3, :] * u
        cv_ref[...] = cv
        gc = gc_ref[...]
        ypc_ref[:, D_POOL:] = (cb_ref[...] * cv * (gc * _sigmoid(gc))).astype(BF16)

    full = lambda shape: pl.BlockSpec(shape, lambda i: (0,) * len(shape))
    row = lambda w: pl.BlockSpec((tq, w), lambda i: (i, 0))
    return pl.pallas_call(
        body, name=name,
        out_shape=(jax.ShapeDtypeStruct((T, Q_LORA), BF16), jax.ShapeDtypeStruct((T, KV_LORA), BF16),
                   jax.ShapeDtypeStruct((T, D_POOL), BF16), jax.ShapeDtypeStruct((T, D_CONV), F32),
                   jax.ShapeDtypeStruct((T, D_MIX), BF16)),
        grid=(T // tq,),
        in_specs=[_pblock(tq, Q_LORA, O_QLAT), _pblock(tq, KV_LORA, O_KVLAT), _pblock(tq, 512, O_PIN),
                  _pblock(tq, 512, O_GPOOL), _pblock(tq, 512, O_CH), _pblock(tq, 512, O_CB), _pblock(tq, 512, O_CC),
                  _pblock(tq, 512, O_GCONV), full((1, Q_LORA)), full((1, KV_LORA)), full((4, 128, 128)),
                  full((1, D_POOL)), full((8, D_CONV))],
        out_specs=(row(Q_LORA), row(KV_LORA), row(D_POOL), row(D_CONV),
                   pl.BlockSpec((tq, D_POOL + D_CONV), lambda i: (i, D_MLA // (D_POOL + D_CONV)))),
        scratch_shapes=[pltpu.VMEM((tq + HALO, D_POOL), F32), pltpu.VMEM((tq + HALO, D_CONV), F32)],
        compiler_params=_params(("arbitrary",)),
    )(proj, proj, proj, proj, proj, proj, proj, proj, q_g, kv_g, w_pool, pool_scale, conv_w)


def _mix_bwd(dmix, proj, o, pooled, cv, w_pool, pool_scale, conv_w, name, tq=ATT_CH):
    T = proj.shape[0]
    nt = T // tq
    n_ext = tq + HALO

    def body(dym_ref, dyp_ref, dyc_ref, gm_ref, gp_ref, ch_ref, cb_ref, cc_ref, gc_ref, o_ref, pooled_ref, cv_ref,
             wp_ref, ps_ref, cw_ref, do_ref, delta_ref, dg_ref, dwp_ref, dps_ref, dcw_ref, exte, extd):
        i = pl.program_id(0)
        tile = nt - 1 - i

        @pl.when(i == 0)
        def _():
            dwp_ref[...] = jnp.zeros_like(dwp_ref)
            dps_ref[...] = jnp.zeros_like(dps_ref)
            dcw_ref[...] = jnp.zeros_like(dcw_ref)
            exte[tq:, :] = jnp.zeros((HALO, D_POOL), F32)
            extd[tq:, :] = jnp.zeros((HALO, D_CONV), F32)

        @pl.when(i > 0)
        def _():
            exte[tq:, :] = exte[0:HALO, :]
            extd[tq:, :] = extd[0:HALO, :]

        gm = gm_ref[...]
        sig = _sigmoid(gm)
        dym = dym_ref[...]
        ov = o_ref[...]
        do = dym * (gm * sig)
        do_ref[...] = do.astype(BF16)
        prod = do * ov
        ones = jnp.ones((8, V_DIM), F32)
        for h in range(N_HEADS):
            rows = lax.dot_general(ones, prod[:, h * V_DIM:(h + 1) * V_DIM], _DIMS["nt"],
                                   precision=lax.Precision.HIGHEST, preferred_element_type=F32)
            delta_ref[h, 0] = rows[0:1, :]
        dg_ref[:, O_GMLA:O_PIN] = (dym * ov * (sig * (1.0 + gm * (1.0 - sig)))).astype(BF16)

        t1 = (tile * tq + lax.broadcasted_iota(jnp.int32, (tq, 1), 0) + 1).astype(F32)
        for g, w in enumerate(POOL_WINDOWS):
            cols = slice(g * POOL_GROUP, (g + 1) * POOL_GROUP)
            pg = pooled_ref[:, cols]
            r = jnp.dot(pg, wp_ref[g], preferred_element_type=F32)
            gp = gp_ref[:, cols]
            sg = _sigmoid(gp)
            sl = gp * sg
            dyg = dyp_ref[:, cols]
            ps = ps_ref[:, cols]
            dg_ref[:, O_GPOOL + g * POOL_GROUP:O_GPOOL + (g + 1) * POOL_GROUP] = (
                dyg * (r * ps) * (sg * (1.0 + gp * (1.0 - sg)))).astype(BF16)
            dps_ref[:, cols] += jnp.sum(dyg * r * sl, axis=0, keepdims=True)
            dr = (dyg * ps * sl).astype(BF16)
            dwp_ref[g] += lax.dot_general(pg, dr, _DIMS["tn"], preferred_element_type=F32)
            dpooled = lax.dot_general(dr, wp_ref[g], _DIMS["nt"], preferred_element_type=F32)
            exte[0:tq, cols] = dpooled / jnp.minimum(t1, float(w))
            s = exte[:, cols]
            k = 1
            while k < w:
                s = s + pltpu.roll(s, n_ext - k, 0)
                k *= 2
            dg_ref[:, O_PIN + g * POOL_GROUP:O_PIN + (g + 1) * POOL_GROUP] = (s[0:tq, :] - dpooled).astype(BF16)

        gc = gc_ref[...]
        sg = _sigmoid(gc)
        sl = gc * sg
        dyc = dyc_ref[...]
        cb, cc, ch, cvv = cb_ref[...], cc_ref[...], ch_ref[...], cv_ref[...]
        dcv = dyc * cb * sl
        dg_ref[:, O_GCONV:O_GCONV + D_CONV] = (dyc * (cb * cvv) * (sg * (1.0 + gc * (1.0 - sg)))).astype(BF16)
        dg_ref[:, O_CB:O_CB + D_CONV] = (dyc * cvv * sl).astype(BF16)
        extd[0:tq, :] = dcv
        ed = extd[...]
        d1 = pltpu.roll(ed, n_ext - 1, 0)[0:tq, :]
        d2 = pltpu.roll(ed, n_ext - 2, 0)[0:tq, :]
        du = cw_ref[2:3, :] * dcv + cw_ref[1:2, :] * d1 + cw_ref[0:1, :] * d2
        u = cc * ch
        dcw_ref[0:1, :] += jnp.sum(u * d2, axis=0, keepdims=True)
        dcw_ref[1:2, :] += jnp.sum(u * d1, axis=0, keepdims=True)
        dcw_ref[2:3, :] += jnp.sum(u * dcv, axis=0, keepdims=True)
        dg_ref[:, O_CH:O_CH + D_CONV] = (du * cc).astype(BF16)
        dg_ref[:, O_CC:O_CC + D_CONV] = (du * ch).astype(BF16)

    def rblock(width, offset):
        assert offset % width == 0
        blk = offset // width
        return pl.BlockSpec((tq, width), lambda i: (nt - 1 - i, blk))

    full = lambda shape: pl.BlockSpec(shape, lambda i: (0,) * len(shape))
    return pl.pallas_call(
        body, name=name,
        out_shape=(jax.ShapeDtypeStruct((T, D_MLA), BF16), jax.ShapeDtypeStruct((N_HEADS, nt, 1, tq), F32),
                   jax.ShapeDtypeStruct((T, NPP), BF16),
                   jax.ShapeDtypeStruct((4, 128, 128), F32), jax.ShapeDtypeStruct((1, D_POOL), F32),
                   jax.ShapeDtypeStruct((8, D_CONV), F32)),
        grid=(nt,),
        in_specs=[rblock(1024, 0), rblock(512, 1024), rblock(512, 1536),
                  rblock(1024, O_GMLA), rblock(512, O_GPOOL), rblock(512, O_CH), rblock(512, O_CB),
                  rblock(512, O_CC), rblock(512, O_GCONV), rblock(1024, 0), rblock(512, 0), rblock(512, 0),
                  full((4, 128, 128)), full((1, D_POOL)), full((8, D_CONV))],
        out_specs=(rblock(1024, 0), pl.BlockSpec((N_HEADS, 1, 1, tq), lambda i: (0, nt - 1 - i, 0, 0)),
                   rblock(N_GATED, 0), full((4, 128, 128)), full((1, D_POOL)), full((8, D_CONV))),
        scratch_shapes=[pltpu.VMEM((n_ext, D_POOL), F32), pltpu.VMEM((n_ext, D_CONV), F32)],
        compiler_params=_params(("arbitrary",)),
    )(dmix, dmix, dmix, proj, proj, proj, proj, proj, proj, o, pooled, cv, w_pool, pool_scale, conv_w)


def _rms_bwd(proj, dqn, dkvn, dkrope, dproj, q_g, kv_g, name, tq=256):
    T = proj.shape[0]
    n_lat = NPP - N_GATED

    def body(ql_ref, kvl_ref, dqn_ref, dkvn_ref, dkr_ref, _, qg_ref, kvg_ref, dlat_ref, dqg_ref, dkvg_ref):
        @pl.when(pl.program_id(0) == 0)
        def _():
            dqg_ref[...] = jnp.zeros_like(dqg_ref)
            dkvg_ref[...] = jnp.zeros_like(dkvg_ref)

        for x_ref, dy_ref, g_ref, c0, dg_ref in ((ql_ref, dqn_ref, qg_ref, 0, dqg_ref),
                                                 (kvl_ref, dkvn_ref, kvg_ref, Q_LORA, dkvg_ref)):
            x, dy = x_ref[...], dy_ref[...]
            r = lax.rsqrt(jnp.mean(x * x, axis=1, keepdims=True) + RMS_EPS)
            xr = x * r
            u = dy * g_ref[...]
            dlat_ref[:, c0:c0 + x.shape[1]] = (r * (u - xr * jnp.mean(u * xr, axis=1, keepdims=True))).astype(BF16)
            dg_ref[...] += jnp.sum(dy * xr, axis=0, keepdims=True)
        dlat_ref[:, Q_LORA + KV_LORA:] = dkr_ref[...]

    row = lambda w: pl.BlockSpec((tq, w), lambda i: (i, 0))
    vec = lambda w: pl.BlockSpec((1, w), lambda i: (0, 0))
    assert N_GATED % n_lat == 0
    return pl.pallas_call(
        body, name=name,
        out_shape=(jax.ShapeDtypeStruct((T, NPP), BF16),
                   jax.ShapeDtypeStruct((1, Q_LORA), F32), jax.ShapeDtypeStruct((1, KV_LORA), F32)),
        grid=(T // tq,),
        in_specs=[_pblock(tq, Q_LORA, O_QLAT), _pblock(tq, KV_LORA, O_KVLAT), row(Q_LORA), row(KV_LORA),
                  row(n_lat - Q_LORA - KV_LORA), pl.BlockSpec(memory_space=pl.ANY), vec(Q_LORA), vec(KV_LORA)],
        out_specs=(pl.BlockSpec((tq, n_lat), lambda i: (i, N_GATED // n_lat)), vec(Q_LORA), vec(KV_LORA)),
        input_output_aliases={5: 0},
        compiler_params=_params(("arbitrary",)),
    )(proj, proj, dqn, dkvn, dkrope, dproj, q_g, kv_g)


def _swap_halves(x, lo):
    return jnp.where(lo, pltpu.roll(x, 96, 1), pltpu.roll(x, 32, 1))


def _rope_fwd(q, kv, proj, cos_t, sin_t, name, tq=256):
    T = q.shape[0]

    def body(qn_ref, qr_ref, kv_ref, kr_ref, c_ref, s_ref, qc_ref, kc_ref):
        C, S = c_ref[...], s_ref[...]
        lane = lax.broadcasted_iota(jnp.int32, (tq, 128), 1)
        lo = (lane % ROPE) < (ROPE // 2)
        first = lane < ROPE

        def rope(x):
            return x * C + _swap_halves(x, lo) * S

        kr = jnp.where(first, rope(kr_ref[...]), 0.0).astype(BF16)
        for j in range(N_HEADS // 2):
            r = rope(qr_ref[:, j * 128:(j + 1) * 128])
            pair = (jnp.where(first, r, 0.0), jnp.where(first, pltpu.roll(r, 64, 1), 0.0))
            for hh in range(2):
                h = 2 * j + hh
                qc_ref[h, :, 0:NOPE] = qn_ref[:, h * NOPE:(h + 1) * NOPE].astype(BF16)
                qc_ref[h, :, NOPE:QC] = pair[hh].astype(BF16)
        for h in range(N_HEADS):
            kc_ref[h, :, 0:NOPE] = kv_ref[:, h * 256:h * 256 + NOPE]
            kc_ref[h, :, NOPE:QC] = kr

    out = jax.ShapeDtypeStruct((N_HEADS, T, QC), BF16)
    hblock = pl.BlockSpec((N_HEADS, tq, QC), lambda i: (0, i, 0))
    return pl.pallas_call(
        body, name=name, out_shape=(out, out), grid=(T // tq,),
        in_specs=[pl.BlockSpec((tq, 1024), lambda i: (i, 0)), pl.BlockSpec((tq, 512), lambda i: (i, 2)),
                  pl.BlockSpec((tq, 2048), lambda i: (i, 0)), _pblock(tq, 128, O_KROPE),
                  pl.BlockSpec((tq, 128), lambda i: (i, 0)), pl.BlockSpec((tq, 128), lambda i: (i, 0))],
        out_specs=(hblock, hblock),
        compiler_params=_params(("parallel",)),
    )(q, q, kv, proj, cos_t, sin_t)


def _rope_bwd(dqc, dkr, cos_t, sin_t, name, tq=256):
    T = dqc.shape[1]

    def body(dqc_ref, dkr_ref, c_ref, s_ref, dq_ref, dk_ref):
        C, S = c_ref[...], s_ref[...]
        lane = lax.broadcasted_iota(jnp.int32, (tq, 128), 1)
        lo = (lane % ROPE) < (ROPE // 2)
        first = lane < ROPE

        def unrope(dy):
            return dy * C - _swap_halves(dy, lo) * S

        acc = dkr_ref[0]
        for h in range(1, N_HEADS):
            acc = acc + dkr_ref[h]
        dk_ref[:, 0:128] = jnp.where(first, unrope(acc), 0.0).astype(BF16)
        dk_ref[:, 128:256] = jnp.zeros((tq, 128), BF16)
        for j in range(N_HEADS // 2):
            d0 = dqc_ref[2 * j, :, NOPE:QC]
            d1 = dqc_ref[2 * j + 1, :, NOPE:QC]
            comb = jnp.where(first, d0, pltpu.roll(d1, 64, 1))
            dq_ref[:, 1024 + j * 128:1024 + (j + 1) * 128] = unrope(comb).astype(BF16)
        for h in range(N_HEADS):
            dq_ref[:, h * NOPE:(h + 1) * NOPE] = dqc_ref[h, :, 0:NOPE].astype(BF16)

    tab = pl.BlockSpec((tq, 128), lambda i: (i, 0))
    return pl.pallas_call(
        body, name=name,
        out_shape=(jax.ShapeDtypeStruct((T, 1536), BF16), jax.ShapeDtypeStruct((T, 256), BF16)),
        grid=(T // tq,),
        in_specs=[pl.BlockSpec((N_HEADS, tq, QC), lambda i: (0, i, 0)),
                  pl.BlockSpec((N_HEADS, tq, 128), lambda i: (0, i, 0)), tab, tab],
        out_specs=(pl.BlockSpec((tq, 1536), lambda i: (i, 0)), pl.BlockSpec((tq, 256), lambda i: (i, 0))),
        compiler_params=_params(("parallel",)),
    )(dqc, dkr, cos_t, sin_t)


def _flash_fwd(qc, kc, kv, proj, mix, name):
    H, T, _ = qc.shape
    tt = ATT_TILE
    nt = T // tt
    sp = tt // ATT_CH

    def body(q_ref, k_ref, v_ref, g_ref, _, o_ref, y_ref, lse_ref, vt_sc, s_sc, acc_sc, m_sc, l_sc):
        i = pl.program_id(1)

        @pl.when(i == 0)
        def _():
            for c in range(nt):
                vt_sc[c] = v_ref[c * tt:(c + 1) * tt, :].astype(F32).T.astype(BF16)

        q = q_ref[0]

        def issue(c, slot):
            s_sc[slot] = lax.dot_general(k_ref[0, pl.ds(pl.multiple_of(c * tt, tt), tt), :], q, _DIMS["nt"],
                                         preferred_element_type=F32)

        def softmax_pv(c, slot, masked):
            s = s_sc[slot]
            if masked:
                krow = c * tt + lax.broadcasted_iota(jnp.int32, s.shape, 0)
                qcol = i * tt + lax.broadcasted_iota(jnp.int32, s.shape, 1)
                s = jnp.where(krow <= qcol, s, -jnp.inf)
            m = m_sc[...]
            m_new = jnp.maximum(m, jnp.max(s, axis=0, keepdims=True))
            p = jnp.exp2((s - m_new) * EXP2_SCALE)
            a = jnp.exp2((m - m_new) * EXP2_SCALE)
            l_sc[...] = a * l_sc[...] + jnp.sum(p, axis=0, keepdims=True)
            acc_sc[...] = a * acc_sc[...] + jnp.dot(vt_sc[c], p.astype(BF16), preferred_element_type=F32)
            m_sc[...] = m_new

        issue(0, 0)
        m_sc[...] = jnp.full_like(m_sc, -jnp.inf)
        l_sc[...] = jnp.zeros_like(l_sc)
        acc_sc[...] = jnp.zeros_like(acc_sc)

        def pair(t, carry):
            issue(2 * t + 1, 1)
            softmax_pv(2 * t, 0, False)
            issue(2 * t + 2, 0)
            softmax_pv(2 * t + 1, 1, False)
            return carry

        lax.fori_loop(0, i // 2, pair, 0)

        @pl.when(i % 2 == 1)
        def _():
            issue(i, 1)
            softmax_pv(i - 1, 0, False)
            softmax_pv(i, 1, True)

        @pl.when(i % 2 == 0)
        def _():
            softmax_pv(i, 0, True)

        l = l_sc[...]
        o = (acc_sc[...] / l).T
        o_ref[...] = o
        lse = m_sc[...] * ATTN_SCALE + jnp.log(l)
        for r in range(sp):
            lse_ref[0, r] = lse[:, r * ATT_CH:(r + 1) * ATT_CH]
        g = g_ref[...]
        y_ref[...] = (o * (g * _sigmoid(g))).astype(BF16)

    return pl.pallas_call(
        body, name=name,
        out_shape=(jax.ShapeDtypeStruct((T, D_MLA), F32), jax.ShapeDtypeStruct((T, D_MIX), BF16),
                   jax.ShapeDtypeStruct((H, T // ATT_CH, 1, ATT_CH), F32)),
        grid=(H, nt),
        in_specs=[pl.BlockSpec((1, tt, QC), lambda h, i: (h, i, 0)),
                  pl.BlockSpec((1, T, QC), lambda h, i: (h, 0, 0)),
                  pl.BlockSpec((T, V_DIM), lambda h, i: (0, 2 * h + 1)),
                  pl.BlockSpec((tt, V_DIM), lambda h, i: (i, h)),
                  pl.BlockSpec(memory_space=pl.ANY)],
        input_output_aliases={4: 1},
        out_specs=(pl.BlockSpec((tt, V_DIM), lambda h, i: (i, h)),
                   pl.BlockSpec((tt, V_DIM), lambda h, i: (i, h)),
                   pl.BlockSpec((1, sp, 1, ATT_CH), lambda h, i: (h, i, 0, 0))),
        scratch_shapes=[pltpu.VMEM((nt, V_DIM, tt), BF16), pltpu.VMEM((2, tt, tt), F32),
                        pltpu.VMEM((V_DIM, tt), F32), pltpu.VMEM((1, tt), F32), pltpu.VMEM((1, tt), F32)],
        compiler_params=_params(("parallel", "arbitrary")),
    )(qc, kc, kv, proj, mix)


def _flash_bwd(qc, kc, kv, do, lse, delta, name):
    H, T, _ = qc.shape
    tt = ATT_TILE
    nt = T // tt
    sp = tt // ATT_CH

    def body(q_ref, k_ref, v_ref, do_ref, lse_ref, dl_ref, dq_ref, dkv_ref, dkr_ref, dqt_sc, dk_sc, dv_sc, s_sc,
             dp_sc, kt_sc):
        j = pl.program_id(1)

        @pl.when(j == 0)
        def _():
            dqt_sc[...] = jnp.zeros_like(dqt_sc)

        k = k_ref[0]
        v = v_ref[...]

        def operands(c):
            q0 = pl.multiple_of(c * tt, tt)
            return q_ref[0, pl.ds(q0, tt), :], do_ref[pl.ds(q0, tt), :]

        def stat_row(ref, c):
            return jnp.concatenate([ref[0, sp * c + r] for r in range(sp)], axis=1)

        def early(c, slot):
            q, dov = operands(c)
            s_sc[slot] = lax.dot_general(k, q, _DIMS["nt"], preferred_element_type=F32)
            dp_sc[slot] = lax.dot_general(v, dov, _DIMS["nt"], preferred_element_type=F32)

        def late(c, slot, masked, kt):
            q, dov = operands(c)
            s, dp = s_sc[slot], dp_sc[slot]
            if masked:
                krow = j * tt + lax.broadcasted_iota(jnp.int32, s.shape, 0)
                qcol = c * tt + lax.broadcasted_iota(jnp.int32, s.shape, 1)
                s = jnp.where(krow <= qcol, s, -jnp.inf)
            p = jnp.exp2(s * EXP2_SCALE - stat_row(lse_ref, c) * LOG2E)
            ds = (p * (dp - stat_row(dl_ref, c)) * ATTN_SCALE).astype(BF16)
            dv = jnp.dot(p.astype(BF16), dov, preferred_element_type=F32)
            dk = jnp.dot(ds, q, preferred_element_type=F32)
            if masked:
                dv_sc[slot] = dv
                dk_sc[slot] = dk
            else:
                dv_sc[slot] += dv
                dk_sc[slot] += dk
            dqt_sc[c] += jnp.dot(kt, ds, preferred_element_type=F32)

        n_rest = nt - 1 - j
        odd = n_rest % 2

        def setup(other_slot):
            dk_sc[other_slot] = jnp.zeros((tt, QC), F32)
            dv_sc[other_slot] = jnp.zeros((tt, V_DIM), F32)
            kt_sc[...] = k.astype(F32).T.astype(BF16)

        @pl.when(odd == 0)
        def _():
            early(j, 0)
            early(jnp.minimum(j + 1, nt - 1), 1)
            setup(1)
            late(j, 0, True, kt_sc[...])

        @pl.when(odd == 1)
        def _():
            early(j, 1)
            early(j + 1, 0)
            setup(0)
            kt = kt_sc[...]
            late(j, 1, True, kt)
            early(jnp.minimum(j + 2, nt - 1), 1)
            late(j + 1, 0, False, kt)

        def pair(u, carry):
            a = j + 1 + odd + 2 * u
            kt = kt_sc[...]
            early(a + 1, 0)
            late(a, 1, False, kt)
            early(jnp.minimum(a + 2, nt - 1), 1)
            late(a + 1, 0, False, kt)
            return carry

        lax.fori_loop(0, n_rest // 2, pair, 0)
        dk = dk_sc[0] + dk_sc[1]
        dkv_ref[:, 0:NOPE] = dk[:, 0:NOPE].astype(BF16)
        dkv_ref[:, NOPE:] = (dv_sc[0] + dv_sc[1]).astype(BF16)
        dkr_ref[0] = dk[:, NOPE:]

        @pl.when(j == nt - 1)
        def _():
            for c in range(nt):
                dq_ref[0, c * tt:(c + 1) * tt, :] = dqt_sc[c].T

    head = lambda h, j: (h, 0, 0)
    stat = pl.BlockSpec((1, T // ATT_CH, 1, ATT_CH), lambda h, j: (h, 0, 0, 0))
    return pl.pallas_call(
        body, name=name,
        out_shape=(jax.ShapeDtypeStruct((H, T, QC), F32), jax.ShapeDtypeStruct((T, 2 * D_MLA), BF16),
                   jax.ShapeDtypeStruct((H, T, 128), F32)),
        grid=(H, nt),
        in_specs=[pl.BlockSpec((1, T, QC), head),
                  pl.BlockSpec((1, tt, QC), lambda h, j: (h, j, 0)),
                  pl.BlockSpec((tt, V_DIM), lambda h, j: (j, 2 * h + 1)),
                  pl.BlockSpec((T, V_DIM), lambda h, j: (0, h)),
                  stat, stat],
        out_specs=(pl.BlockSpec((1, T, QC), head),
                   pl.BlockSpec((tt, 256), lambda h, j: (j, h)),
                   pl.BlockSpec((1, tt, 128), lambda h, j: (h, j, 0))),
        scratch_shapes=[pltpu.VMEM((nt, QC, tt), F32), pltpu.VMEM((2, tt, QC), F32), pltpu.VMEM((2, tt, V_DIM), F32),
                        pltpu.VMEM((2, tt, tt), F32), pltpu.VMEM((2, tt, tt), F32), pltpu.VMEM((QC, tt), BF16)],
        compiler_params=_params(("parallel", "arbitrary")),
    )(qc, kc, kv, do, lse, delta)


def _adamw(lands, w, m, v, name, rows, cols=None, first_layer=0, into=None):
    layers, R, C = w.shape
    L = len(lands)
    cols = C if cols is None else cols
    assert R % rows == 0 and C % cols == 0 and first_layer + L <= layers
    nc = C // cols
    nb = (R // rows) * nc
    c1 = 1.0 - ADAM_B1 ** ADAM_STEP
    c2 = 1.0 - ADAM_B2 ** ADAM_STEP

    def body(*refs):
        land_refs = refs[:L]
        w_ref, m_ref, v_ref = refs[L:L + 3]
        g_ref, d_ref, nm_ref, nv_ref, g_sc = refs[-5:]
        for ll in range(L):
            @pl.when(pl.program_id(0) == ll)
            def _(land_ref=land_refs[ll]):
                g = land_ref[0].astype(F32)
                for s in range(1, N_DEV):
                    g = g + land_ref[s].astype(F32)
                g_sc[...] = g

        g = g_sc[...]
        nm = ADAM_B1 * m_ref[0] + (1.0 - ADAM_B1) * g
        nv = ADAM_B2 * v_ref[0] + (1.0 - ADAM_B2) * (g * g)
        g_ref[0] = g
        nm_ref[0] = nm
        nv_ref[0] = nv
        d_ref[0] = -ADAM_LR * ((nm / c1) / (jnp.sqrt(nv / c2) + ADAM_EPS) + ADAM_WD * w_ref[0])

    def land_spec(ll):
        def index(l, i):
            i = jnp.where(l < ll, 0, jnp.where(l > ll, nb - 1, i))
            return (0, i // nc, i % nc)
        return pl.BlockSpec((N_DEV, rows, cols), index)

    blk = pl.BlockSpec((1, rows, cols), lambda l, i: (first_layer + l, i // nc, i % nc))
    out = jax.ShapeDtypeStruct((layers, R, C), F32)
    extra = [] if into is None else list(into)
    return pl.pallas_call(
        body, name=name, out_shape=(out, out, out, out), grid=(L, nb),
        in_specs=[land_spec(ll) for ll in range(L)] + [blk, blk, blk] + [pl.BlockSpec(memory_space=pl.ANY)] * len(extra),
        out_specs=(blk, blk, blk, blk),
        input_output_aliases={L + 3 + i: i for i in range(len(extra))},
        scratch_shapes=[pltpu.VMEM((rows, cols), F32)],
        compiler_params=_params(("arbitrary", "arbitrary")),
    )(*lands, w, m, v, *extra)


def _mesh_pos():
    return lax.axis_index("x"), lax.axis_index("y"), lax.axis_index("c")


def _all_gather(arrays, name):
    n = len(arrays)

    def body(*refs):
        ins, outs = refs[:n], refs[n:2 * n]
        send_sems, recv_sems, local_sems = refs[2 * n:]
        x, y, c = _mesh_pos()
        me, sibling = (x, y, c), (x, y, 1 - c)
        chips = [(1 - x, y), (x, 1 - y), (1 - x, 1 - y)]

        def slot(a, pos):
            px, py, pc = pos
            return outs[a].at[4 * px + 2 * py + pc]

        def copy(a, k, block, to, src=None):
            return pltpu.make_async_remote_copy(
                src_ref=slot(a, block) if src is None else src, dst_ref=slot(a, block),
                send_sem=send_sems.at[a * 7 + k], recv_sem=recv_sems.at[a * 7 + k],
                device_id=to, device_id_type=MESH_ID)

        mine, first, passed = [], [], []
        for a in range(n):
            cp = pltpu.make_async_copy(ins[a], slot(a, me), local_sems.at[a])
            cp.start()
            mine.append(cp)
            cps = [copy(a, 0, me, sibling, src=ins[a])]
            cps += [copy(a, 1 + j, me, (*chip, c), src=ins[a]) for j, chip in enumerate(chips)]
            for cp in cps:
                cp.start()
            first += cps
        for j, chip in enumerate(chips):
            for a in range(n):
                copy(a, 1 + j, (*chip, c), me).wait_recv()
                cp = copy(a, 4 + j, (*chip, c), sibling)
                cp.start()
                passed.append(cp)
        for a in range(n):
            copy(a, 0, sibling, me).wait_recv()
            for j, chip in enumerate(chips):
                copy(a, 4 + j, (*chip, 1 - c), me).wait_recv()
        for cp in first + passed:
            cp.wait_send()
        for cp in mine:
            cp.wait()

    hbm = pl.BlockSpec(memory_space=pltpu.HBM)
    return pl.pallas_call(
        body, name=name,
        out_shape=tuple(jax.ShapeDtypeStruct((N_DEV,) + a.shape, a.dtype) for a in arrays),
        in_specs=[hbm] * n, out_specs=tuple([hbm] * n),
        scratch_shapes=[pltpu.SemaphoreType.DMA((7 * n,)), pltpu.SemaphoreType.DMA((7 * n,)),
                        pltpu.SemaphoreType.DMA((n,))],
    )(*arrays)


def _all_gather_under_ln(arrays, x, g, b, name, tq=512):
    n = len(arrays)
    T, D = x.shape
    nt = T // tq

    def body(*refs):
        x_ref, g_ref, b_ref = refs[:3]
        ins = refs[3:3 + n]
        y_ref, yb_ref = refs[3 + n:5 + n]
        outs = refs[5 + n:5 + 2 * n]
        send_sems, recv_sems, local_sems = refs[5 + 2 * n:]
        i = pl.program_id(0)
        mx, my, mc = _mesh_pos()
        me, sibling = (mx, my, mc), (mx, my, 1 - mc)
        chips = [(1 - mx, my), (mx, 1 - my), (1 - mx, 1 - my)]

        def slot(a, pos):
            px, py, pc = pos
            return outs[a].at[4 * px + 2 * py + pc]

        def copy(a, k, block, to, src=None):
            return pltpu.make_async_remote_copy(
                src_ref=slot(a, block) if src is None else src, dst_ref=slot(a, block),
                send_sem=send_sems.at[a * 7 + k], recv_sem=recv_sems.at[a * 7 + k],
                device_id=to, device_id_type=MESH_ID)

        def own(a):
            return pltpu.make_async_copy(ins[a], slot(a, me), local_sems.at[a])

        def first(a):
            return [copy(a, 0, me, sibling, src=ins[a])] + [
                copy(a, 1 + j, me, (*chip, mc), src=ins[a]) for j, chip in enumerate(chips)]

        @pl.when(i == 0)
        def _():
            for a in range(n):
                own(a).start()
                for cp in first(a):
                    cp.start()

        zv = x_ref[...]
        mu = jnp.mean(zv, axis=1, keepdims=True)
        zc = zv - mu
        var = jnp.mean(zc * zc, axis=1, keepdims=True)
        y = zc * lax.rsqrt(var + LN_EPS) * g_ref[...] + b_ref[...]
        y_ref[...] = y
        yb_ref[...] = y.astype(BF16)

        @pl.when(i == nt - 1)
        def _():
            passed = []
            for j, chip in enumerate(chips):
                for a in range(n):
                    copy(a, 1 + j, (*chip, mc), me).wait_recv()
                    cp = copy(a, 4 + j, (*chip, mc), sibling)
                    cp.start()
                    passed.append(cp)
            for a in range(n):
                copy(a, 0, sibling, me).wait_recv()
                for j, chip in enumerate(chips):
                    copy(a, 4 + j, (*chip, 1 - mc), me).wait_recv()
            for a in range(n):
                for cp in first(a):
                    cp.wait_send()
                own(a).wait()
            for cp in passed:
                cp.wait_send()

    row = pl.BlockSpec((tq, D), lambda i: (i, 0))
    vec = pl.BlockSpec((1, D), lambda i: (0, 0))
    hbm = pl.BlockSpec(memory_space=pltpu.HBM)
    outs = pl.pallas_call(
        body, name=name,
        out_shape=(jax.ShapeDtypeStruct((T, D), F32), jax.ShapeDtypeStruct((T, D), BF16))
        + tuple(jax.ShapeDtypeStruct((N_DEV,) + a.shape, a.dtype) for a in arrays),
        grid=(nt,), in_specs=[row, vec, vec] + [hbm] * n, out_specs=tuple([row, row] + [hbm] * n),
        scratch_shapes=[pltpu.SemaphoreType.DMA((7 * n,)), pltpu.SemaphoreType.DMA((7 * n,)),
                        pltpu.SemaphoreType.DMA((n,))],
        compiler_params=_params(("arbitrary",)),
    )(x, g, b, *arrays)
    return outs[0], outs[1], outs[2:]


_HBM = pl.BlockSpec(memory_space=pltpu.HBM)
_SEM = pl.BlockSpec(memory_space=pltpu.SEMAPHORE)
_EFFECT = pltpu.SideEffectType.DATAFLOW_SIDE_EFFECTING
N_PEERS = N_DEV - 1


def _peer(k):
    x, y, c = _mesh_pos()
    return (1 - x if k & 4 else x, 1 - y if k & 2 else y, 1 - c if k & 1 else c)


def _split_start(srcs, scatter, after, name):
    n = len(srcs)
    zones = [jax.ShapeDtypeStruct(s.shape if scatter else ((N_DEV,) + s.shape), s.dtype) for s in srcs]

    def body(*refs):
        src, zone = refs[:n], refs[n:2 * n]
        outs = refs[2 * n + 1:]
        send, recv, token = outs[:n], outs[n:2 * n], outs[4 * n]
        x, y, c = _mesh_pos()
        my_idx = 4 * x + 2 * y + c
        for a in range(n):
            pltpu.make_async_copy(src[a].at[my_idx] if scatter else src[a],
                                  zone[a].at[N_PEERS] if scatter else zone[a].at[my_idx], recv[a]).start()
            for k in range(1, N_DEV):
                px, py, pc = _peer(k)
                pltpu.make_async_remote_copy(
                    src_ref=src[a].at[4 * px + 2 * py + pc] if scatter else src[a],
                    dst_ref=zone[a].at[k - 1] if scatter else zone[a].at[my_idx],
                    send_sem=send[a], recv_sem=recv[a], device_id=(px, py, pc), device_id_type=MESH_ID).start()
        token[...] = jnp.zeros_like(token)

    hbm = lambda a: pltpu.with_memory_space_constraint(a, pltpu.HBM)
    outs = pl.pallas_call(
        body, name=name,
        out_shape=tuple([pltpu.SemaphoreType.DMA(())] * (2 * n)
                        + [pltpu.HBM(s.shape, s.dtype) for s in srcs]
                        + [pltpu.HBM(z.shape, z.dtype) for z in zones]
                        + [jax.ShapeDtypeStruct((8, 128), F32)]),
        in_specs=[_HBM] * (2 * n) + [pl.BlockSpec(memory_space=pl.ANY)],
        out_specs=tuple([_SEM] * (2 * n) + [_HBM] * (2 * n) + [pl.BlockSpec(memory_space=pltpu.VMEM)]),
        input_output_aliases={**{a: 2 * n + a for a in range(n)}, **{n + a: 3 * n + a for a in range(n)}},
        compiler_params=pltpu.CompilerParams(has_side_effects=_EFFECT),
    )(*[hbm(s) for s in srcs], *[hbm(lax.empty(z.shape, z.dtype)) for z in zones], after)
    return outs[:n], outs[n:2 * n], outs[2 * n:3 * n], outs[3 * n:4 * n], outs[4 * n]


def _split_wait(send, recv, srcs, zones, after, name):
    n = len(srcs)

    def body(*refs):
        zone = refs[n:2 * n]
        send_sems, recv_sems = refs[2 * n:3 * n], refs[3 * n:4 * n]
        x, y, c = _mesh_pos()
        for a in range(n):
            seven = zone[a].at[pl.ds(0, N_PEERS)]
            pltpu.make_async_remote_copy(src_ref=seven, dst_ref=seven, send_sem=send_sems[a], recv_sem=recv_sems[a],
                                         device_id=(x, y, 1 - c), device_id_type=MESH_ID).wait_send()
            pltpu.make_async_remote_copy(src_ref=zone[a], dst_ref=zone[a], send_sem=send_sems[a],
                                         recv_sem=recv_sems[a], device_id=(x, y, 1 - c),
                                         device_id_type=MESH_ID).wait_recv()

    outs = pl.pallas_call(
        body, name=name,
        out_shape=tuple([pltpu.HBM(s.shape, s.dtype) for s in srcs] + [pltpu.HBM(z.shape, z.dtype) for z in zones]),
        in_specs=[_HBM] * (2 * n) + [_SEM] * (2 * n) + [pl.BlockSpec(memory_space=pl.ANY)],
        out_specs=tuple([_HBM] * (2 * n)),
        input_output_aliases={a: a for a in range(2 * n)},
        compiler_params=pltpu.CompilerParams(has_side_effects=_EFFECT),
    )(*srcs, *zones, *send, *recv, after)
    return outs[:n], outs[n:]


def _cat_blocks(g, axis):
    return jnp.concatenate([g[d] for d in range(N_DEV)], axis=axis)


N_LATENT = Q_LORA + KV_LORA + ROPE
W_SHARD = D_IN_PROJ // N_DEV


def _ref_cols(lo, hi):
    out = []
    if lo < N_LATENT:
        out.append((N_GATED + lo, N_GATED + min(hi, N_LATENT)))
    if hi > N_LATENT:
        out.append((max(lo, N_LATENT) - N_LATENT, hi - N_LATENT))
    return out


def _permute_w_in_t(blocks):
    pieces = []
    for lo, hi in ((N_LATENT, D_IN_PROJ), (0, N_LATENT)):
        for d in range(N_DEV):
            a, b = max(lo, d * W_SHARD), min(hi, (d + 1) * W_SHARD)
            if a < b:
                pieces.append(blocks[d][a - d * W_SHARD:b - d * W_SHARD])
    pieces.append(jnp.zeros((NPP - D_IN_PROJ, blocks.shape[2]), blocks.dtype))
    return jnp.concatenate(pieces, axis=0)


def _split_w_in_t(w):
    slabs = []
    for d in range(N_DEV):
        parts = [w[a:b] for a, b in _ref_cols(d * W_SHARD, (d + 1) * W_SHARD)]
        slabs.append(parts[0] if len(parts) == 1 else jnp.concatenate(parts, axis=0))
    return jnp.stack(slabs)


def _permute_w_uq(w):
    w3 = w.reshape(w.shape[0], N_HEADS, NOPE + ROPE)
    return jnp.concatenate([w3[:, :, :NOPE].reshape(w.shape[0], -1), w3[:, :, NOPE:].reshape(w.shape[0], -1)], axis=1)


def _unpermute_w_uq(w):
    nope = w[:, :N_HEADS * NOPE].reshape(w.shape[0], N_HEADS, NOPE)
    rope = w[:, N_HEADS * NOPE:].reshape(w.shape[0], N_HEADS, ROPE)
    return jnp.concatenate([nope, rope], axis=2).reshape(w.shape[0], -1)


_SMALL_EMB = (("emb_ln_g", 16), ("emb_ln_b", 16))
_SMALL_LAYER = (("q_norm_g", 8), ("kv_norm_g", 8), ("w_pool", 1024), ("pool_scale", 8), ("b_out", 32),
                ("ln_g", 32), ("ln_b", 32))
_SMALL = _SMALL_EMB + _SMALL_LAYER
CONV_ROWS = DEPTH * CONV_WIDTH * D_CONV // 128


def _pack_small(d, entries=_SMALL):
    parts = []
    for name, rows in entries:
        flat = d[name].reshape(-1)
        flat = jnp.pad(flat, (0, rows * 128 - flat.shape[0]))
        parts.append(flat.reshape(rows, 128))
    return jnp.concatenate(parts, axis=0)


def _unpack_small(packed, shapes):
    out, r0 = {}, 0
    for name, rows in _SMALL:
        size = 1
        for s in shapes[name]:
            size *= s
        out[name] = packed[r0:r0 + rows].reshape(-1)[:size].reshape(shapes[name])
        r0 += rows
    return out


def _rope_tables(positions):
    half = ROPE // 2
    inv_freq = ROPE_THETA ** (-jnp.arange(half, dtype=F32) / half)
    ang = positions.astype(F32)[:, None] * inv_freq
    cos, sin = jnp.cos(ang), jnp.sin(ang)
    return jnp.concatenate([cos, cos, cos, cos], axis=1), jnp.concatenate([-sin, sin, -sin, sin], axis=1)


def _local_step(x, positions, target, emb_g, emb_b, layer_weights, layer_weights_rest, on_sharded_grads,
                on_layer_grads=None, first_after=None, embedded=None):
    cos_t, sin_t = _rope_tables(positions)
    h, hb = _ln_fwd(x, emb_g, emb_b, "emb_ln_fwd") if embedded is None else embedded
    saved = []
    for l in range(DEPTH):
        W = layer_weights(l, h)
        proj = _mm(hb, W["w_in_t"], "nt", F32, "proj_fwd", after=first_after if l == 0 else None)
        qn, kvn, pooled, cv, mix = _mix_fwd(proj, W["q_norm_g"], W["kv_norm_g"], W["w_pool"], W["pool_scale"],
                                            W["conv_w"], "mix_fwd")
        rest, token = layer_weights_rest(l, proj)
        W = {**W, **rest}
        q = _mm(qn, W["w_uq"], "nn", F32, "q_up_fwd", after=token)
        kv = _mm(kvn, W["w_ukv"], "nn", BF16, "kv_up_fwd")
        qc, kc = _rope_fwd(q, kv, proj, cos_t, sin_t, "rope_fwd")
        o, mix, lse = _flash_fwd(qc, kc, kv, proj, mix, "flash_fwd")
        z = _mm(mix, W["w_out"], "nn", F32, "out_fwd", res=h, bias=W["b_out"], alpha=ALPHA)
        saved.append((W, hb, proj, qn, kvn, pooled, cv, kv, qc, kc, o, lse, mix, z))
        h, hb = _ln_fwd(z, W["ln_g"], W["ln_b"], "ln_fwd")
    dh, sq = h, None

    grads = {k: [None] * DEPTH for k in ("q_norm_g", "kv_norm_g", "w_pool", "pool_scale", "conv_w", "b_out", "ln_g",
                                         "ln_b")}
    for l in reversed(range(DEPTH)):
        W, hb_in, proj, qn, kvn, pooled, cv, kv, qc, kc, o, lse, mix, z = saved[l]
        sharded = {}
        if l == DEPTH - 1:
            dz, dzb, grads["b_out"][l], grads["ln_g"][l], grads["ln_b"][l], sq = _ln_bwd(
                dh, z, W["ln_g"], "ln_bwd_loss", target=target)
        else:
            dz, dzb, grads["b_out"][l], grads["ln_g"][l], grads["ln_b"][l] = _ln_bwd(dh, z, W["ln_g"], "ln_bwd")
        dmix = _mm(dzb, W["w_out"], "nt", F32, "out_bwd_x")
        sharded["w_out"] = _mm(mix, dzb, "tn", GRAD_XFER, "out_bwd_w", tk=4096)
        do, delta, dproj, grads["w_pool"][l], grads["pool_scale"][l], grads["conv_w"][l] = _mix_bwd(
            dmix, proj, o, pooled, cv, W["w_pool"], W["pool_scale"], W["conv_w"], "mix_bwd")
        dqc, dkv, dkr = _flash_bwd(qc, kc, kv, do, lse, delta, "flash_bwd")
        dq, dkrope = _rope_bwd(dqc, dkr, cos_t, sin_t, "rope_bwd")
        dqn = _mm(dq, W["w_uq"], "nt", F32, "q_up_bwd_x")
        sharded["w_uq"] = _mm(qn, dq, "tn", GRAD_XFER, "q_up_bwd_w")
        dkvn = _mm(dkv, W["w_ukv"], "nt", F32, "kv_up_bwd_x")
        sharded["w_ukv"] = _mm(kvn, dkv, "tn", GRAD_XFER, "kv_up_bwd_w")
        token = on_sharded_grads(l, sharded)
        dproj, grads["q_norm_g"][l], grads["kv_norm_g"][l] = _rms_bwd(
            proj, dqn, dkvn, dkrope, dproj, W["q_norm_g"], W["kv_norm_g"], "rms_bwd")
        if l == 0 and on_layer_grads is not None:
            token = on_layer_grads(grads, token)
        d_w_in_t = _mm(dproj, hb_in, "tn", GRAD_XFER, "proj_bwd_w", tk=4096, after=token)
        token = on_sharded_grads(l, {"w_in": d_w_in_t})
        dh = _mm(dproj, W["w_in_t"], "nn", F32, "proj_bwd_x", res=dz, alpha=ALPHA, tk=2560, after=token)
    grad_x, grads["emb_ln_g"], grads["emb_ln_b"] = _ln_bwd(dh, x, emb_g, "emb_ln_bwd", for_matmul=False)
    return sq, grad_x, grads


def kernel(x, positions, emb_ln_g, emb_ln_b, w_in, q_norm_g, kv_norm_g, w_uq, w_ukv, w_pool, pool_scale, conv_w, w_out, b_out, ln_g, ln_b, loss_target, m_emb_ln_g, m_emb_ln_b, m_w_in, m_q_norm_g, m_kv_norm_g, m_w_uq, m_w_ukv, m_w_pool, m_pool_scale, m_conv_w, m_w_out, m_b_out, m_ln_g, m_ln_b, v_emb_ln_g, v_emb_ln_b, v_w_in, v_q_norm_g, v_kv_norm_g, v_w_uq, v_w_ukv, v_w_pool, v_pool_scale, v_conv_w, v_w_out, v_b_out, v_ln_g, v_ln_b):
    weights = dict(emb_ln_g=emb_ln_g, emb_ln_b=emb_ln_b, w_in=w_in, q_norm_g=q_norm_g, kv_norm_g=kv_norm_g,
                   w_uq=w_uq, w_ukv=w_ukv, w_pool=w_pool, pool_scale=pool_scale, conv_w=conv_w, w_out=w_out,
                   b_out=b_out, ln_g=ln_g, ln_b=ln_b)
    mom1 = dict(emb_ln_g=m_emb_ln_g, emb_ln_b=m_emb_ln_b, w_in=m_w_in, q_norm_g=m_q_norm_g, kv_norm_g=m_kv_norm_g,
                w_uq=m_w_uq, w_ukv=m_w_ukv, w_pool=m_w_pool, pool_scale=m_pool_scale, conv_w=m_conv_w,
                w_out=m_w_out, b_out=m_b_out, ln_g=m_ln_g, ln_b=m_ln_b)
    mom2 = dict(emb_ln_g=v_emb_ln_g, emb_ln_b=v_emb_ln_b, w_in=v_w_in, q_norm_g=v_q_norm_g, kv_norm_g=v_kv_norm_g,
                w_uq=v_w_uq, w_ukv=v_w_ukv, w_pool=v_w_pool, pool_scale=v_pool_scale, conv_w=v_conv_w,
                w_out=v_w_out, b_out=v_b_out, ln_g=v_ln_g, ln_b=v_ln_b)

    big = ("w_in", "w_uq", "w_ukv", "w_out")

    conv_pad = jnp.zeros((8, 128), F32).at[0:DEPTH * CONV_WIDTH, 0:64].set(conv_w.reshape(DEPTH * CONV_WIDTH, 64))
    t12 = lambda a: jnp.swapaxes(a, 1, 2)
    shard = lambda k, l: (t12(weights[k])[l] if k == "w_in" else weights[k][l]).astype(BF16)
    h0, h0b, (w_in0, conv_all) = _all_gather_under_ln(
        [shard("w_in", 0), conv_pad], x[0], emb_ln_g.reshape(1, -1), emb_ln_b.reshape(1, -1), "w_in0_all_gather_emb_ln")
    rest0 = _split_start([shard(k, 0) for k in big[1:]], False, w_in0, "weights0_rest_start")
    conv_full = _cat_blocks(conv_all[:, 0:DEPTH * CONV_WIDTH, 0:64], 1).reshape(DEPTH, CONV_WIDTH, D_CONV)
    conv_full = jnp.pad(conv_full, ((0, 0), (0, 8 - CONV_WIDTH), (0, 0)))
    fetched = {}

    def layer_weights(l, ready):
        if l == 0:
            w_in_blocks = w_in0
        else:
            fetched[1] = _split_wait(*fetched["w1"][:4], ready, "weights1_wait")[1]
            w_in_blocks = fetched[1][0]
        return dict(
            w_in_t=_permute_w_in_t(w_in_blocks), conv_w=conv_full[l],
            q_norm_g=q_norm_g[l].reshape(1, -1), kv_norm_g=kv_norm_g[l].reshape(1, -1),
            w_pool=w_pool[l].astype(BF16), pool_scale=pool_scale[l].reshape(1, -1), b_out=b_out[l].reshape(1, -1),
            ln_g=ln_g[l].reshape(1, -1), ln_b=ln_b[l].reshape(1, -1))

    def layer_weights_rest(l, ready):
        token = None
        if l == 0:
            blocks = _split_wait(*rest0[:4], ready, "weights0_rest_wait")[1]
            fetched["w1"] = _split_start([shard(k, 1) for k in big], False, blocks[0], "weights1_start")
            token = fetched["w1"][4]
        else:
            blocks = fetched[1][1:]
        return dict(w_uq=_permute_w_uq(_cat_blocks(blocks[0], 1)), w_ukv=_cat_blocks(blocks[1], 1),
                    w_out=_cat_blocks(blocks[2], 0)), token

    by_dest = dict(
        w_in=_split_w_in_t,
        w_uq=lambda g: _unpermute_w_uq(g).reshape(Q_LORA, N_DEV, -1).transpose(1, 0, 2),
        w_ukv=lambda g: g.reshape(KV_LORA, N_DEV, -1).transpose(1, 0, 2),
        w_out=lambda g: g.reshape(N_DEV, -1, D_MODEL))
    in_flight = []

    def on_sharded_grads(l, g):
        names = [k for k in big if k in g]
        srcs = [by_dest[k](g[k]) for k in names]
        started = _split_start(srcs, True, srcs[0], "grads%d_%s_start" % (l, names[0]))
        in_flight.append((l, names, started[:4]))
        return started[4]

    small_in_flight = []

    def on_layer_grads(g, token):
        stacked = {k: jnp.stack(g[k]) for k, _ in _SMALL_LAYER}
        conv = jnp.stack([g["conv_w"][l][0:CONV_WIDTH] for l in range(DEPTH)]).reshape(CONV_ROWS, 128)
        packed = jnp.concatenate([_pack_small(stacked, _SMALL_LAYER), conv], axis=0)
        started = _split_start([packed], False, token, "layer_grads_start")
        small_in_flight.append(started[:4])
        return started[4]

    sq, grad_x, G = _local_step(x[0], positions[0], loss_target[0], emb_ln_g.reshape(1, -1),
                                emb_ln_b.reshape(1, -1), layer_weights, layer_weights_rest, on_sharded_grads,
                                on_layer_grads, first_after=rest0[4], embedded=(h0, h0b))
    loss = lax.psum(sq[0, 0] * (0.5 / D_MODEL), ("x", "y", "c"))

    res = {}
    landed = {}
    for l, names, started in in_flight:
        zones = _split_wait(*started, grad_x, "grads%d_%s_wait" % (l, names[0]))[1]
        for k, zone in zip(names, zones):
            landed[k, l] = zone
    w_in_res = None
    for l in reversed(range(DEPTH)):
        w_in_res = _adamw([landed["w_in", l]], t12(w_in), t12(m_w_in), t12(v_w_in), "adamw_w_in_%d" % l, W_SHARD, 512,
                          first_layer=l, into=w_in_res)
    res["w_in"] = tuple(t12(o) for o in w_in_res)
    for name, rows in (("w_uq", 256), ("w_ukv", 256), ("w_out", 128)):
        res[name] = _adamw([landed[name, l] for l in range(DEPTH)], weights[name], mom1[name], mom2[name],
                           "adamw_" + name, rows)

    layer_zone = _split_wait(*small_in_flight[0], grad_x, "layer_grads_wait")[1][0]
    emb_zone = _all_gather([_pack_small(G, _SMALL_EMB)], "emb_grads_all_gather")[0]
    n_layer_rows = sum(r for _, r in _SMALL_LAYER)
    l_small = jnp.concatenate([emb_zone, layer_zone[:, 0:n_layer_rows]], axis=1)
    my_idx = 4 * lax.axis_index("x") + 2 * lax.axis_index("y") + lax.axis_index("c")
    conv_all_grads = layer_zone[:, n_layer_rows:].reshape(N_DEV, DEPTH * CONV_WIDTH, D_CONV)
    l_conv = lax.dynamic_slice_in_dim(conv_all_grads, my_idx * 64, 64, axis=2)
    l_conv = jnp.zeros((N_DEV, 8, 128), F32).at[:, 0:DEPTH * CONV_WIDTH, 0:64].set(l_conv)
    conv_shard = lambda a: jnp.zeros((8, 128), F32).at[0:DEPTH * CONV_WIDTH, 0:64].set(a.reshape(-1, 64))
    conv_res = _adamw([l_conv], conv_shard(conv_w)[None], conv_shard(m_conv_w)[None], conv_shard(v_conv_w)[None],
                      "adamw_conv_w", 8)
    res["conv_w"] = tuple(o[0, 0:DEPTH * CONV_WIDTH, 0:64].reshape(DEPTH, CONV_WIDTH, 64) for o in conv_res)
    small_res = _adamw([l_small], _pack_small(weights)[None], _pack_small(mom1)[None], _pack_small(mom2)[None],
                       "adamw_small", 392)
    shapes = {k: weights[k].shape for k, _ in _SMALL}
    unpacked = [_unpack_small(o[0], shapes) for o in small_res]
    for k, _ in _SMALL:
        res[k] = tuple(u[k] for u in unpacked)

    order = ("emb_ln_g", "emb_ln_b", "w_in", "q_norm_g", "kv_norm_g", "w_uq", "w_ukv", "w_pool", "pool_scale",
             "conv_w", "w_out", "b_out", "ln_g", "ln_b")
    return (loss, grad_x[None], *[res[k][0] for k in order], *[res[k][1] for k in order],
            *[res[k][2] for k in order], *[res[k][3] for k in order])
```

```python
import jax
import jax.numpy as jnp
from jax import lax
from jax.experimental import pallas as pl
from jax.experimental.pallas import tpu as pltpu

F32 = jnp.float32
BF16 = jnp.bfloat16

N_DEV = 8
D_MODEL = 2048
DEPTH = 2
N_HEADS = 8
NOPE = 128
ROPE = 64
V_DIM = 128
Q_LORA = 512
KV_LORA = 256
D_MLA = N_HEADS * V_DIM
D_POOL = 512
D_CONV = 512
POOL_WINDOWS = (2, 4, 8, 16)
POOL_GROUP = 128
CONV_WIDTH = 3
D_MIX = D_MLA + D_POOL + D_CONV
D_IN_PROJ = 4928
ROPE_THETA = 10000.0
LN_EPS = 1e-5
RMS_EPS = 1e-6
ALPHA = (2 * DEPTH) ** 0.25
ATTN_SCALE = (NOPE + ROPE) ** -0.5
ADAM_LR = 0.001
ADAM_B1 = 0.9
ADAM_B2 = 0.999
ADAM_EPS = 1e-08
ADAM_WD = 0.01
ADAM_STEP = 10

O_GMLA, O_PIN, O_GPOOL, O_CH, O_CB, O_CC, O_GCONV, O_QLAT, O_KVLAT, O_KROPE = (
    0, 1024, 1536, 2048, 2560, 3072, 3584, 4096, 4608, 4864)
NPP = 5120
N_GATED = O_QLAT
QC = NOPE + 2 * ROPE
HALO = 16
ATT_TILE = 512
ATT_CH = 256
LOG2E = 1.4426950408889634
EXP2_SCALE = ATTN_SCALE * LOG2E

GRAD_XFER = BF16
VMEM_LIMIT = 48 * 1024 * 1024
MESH_ID = pl.DeviceIdType.MESH


def _params(sem=None):
    return pltpu.CompilerParams(dimension_semantics=sem, vmem_limit_bytes=VMEM_LIMIT)


def _sigmoid(x):
    return 1.0 / (1.0 + jnp.exp(-x))


def _tile(dim, target):
    if dim <= target:
        return dim
    t = target - target % 128
    while dim % t:
        t -= 128
    return t


_DIMS = {"nn": (((1,), (0,)), ((), ())), "nt": (((1,), (1,)), ((), ())), "tn": (((0,), (0,)), ((), ()))}


def _mm(a, b, mode, out_dtype, name, res=None, bias=None, alpha=1.0, tm=1024, tn=1024, tk=2048, after=None):
    if mode == "nn":
        (M, K), (K2, N) = a.shape, b.shape
    elif mode == "nt":
        (M, K), (N, K2) = a.shape, b.shape
    else:
        (K, M), (K2, N) = a.shape, b.shape
    assert K == K2
    tm, tn, tk = _tile(M, tm), _tile(N, tn), _tile(K, tk)
    nk = K // tk
    has_res, has_bias = res is not None, bias is not None

    def body(*refs):
        a_ref, b_ref = refs[0], refs[1]
        pos = 2
        res_ref = bias_ref = None
        if has_res:
            res_ref = refs[pos]
            pos += 1
        if has_bias:
            bias_ref = refs[pos]
            pos += 1
        def finish(r, o_ref):
            if has_bias:
                r = r + bias_ref[...]
            if has_res:
                r = alpha * res_ref[...] + r
            o_ref[...] = r.astype(out_dtype)

        part = lax.dot_general(a_ref[...].astype(BF16), b_ref[...].astype(BF16), _DIMS[mode],
                               preferred_element_type=F32)
        if nk == 1:
            finish(part, refs[-1])
            return
        o_ref, acc_ref = refs[-2], refs[-1]
        k = pl.program_id(2)

        @pl.when(k == 0)
        def _():
            acc_ref[...] = part

        @pl.when(jnp.logical_and(k > 0, k < nk - 1))
        def _():
            acc_ref[...] += part

        @pl.when(k == nk - 1)
        def _():
            finish(acc_ref[...] + part, o_ref)

    if mode == "nn":
        in_specs = [pl.BlockSpec((tm, tk), lambda i, j, k: (i, k)), pl.BlockSpec((tk, tn), lambda i, j, k: (k, j))]
    elif mode == "nt":
        in_specs = [pl.BlockSpec((tm, tk), lambda i, j, k: (i, k)), pl.BlockSpec((tn, tk), lambda i, j, k: (j, k))]
    else:
        in_specs = [pl.BlockSpec((tk, tm), lambda i, j, k: (k, i)), pl.BlockSpec((tk, tn), lambda i, j, k: (k, j))]
    args = [a, b]
    if has_res:
        in_specs.append(pl.BlockSpec((tm, tn), lambda i, j, k: (i, j)))
        args.append(res)
    if has_bias:
        in_specs.append(pl.BlockSpec((1, tn), lambda i, j, k: (0, j)))
        args.append(bias)
    if after is not None:
        in_specs.append(pl.BlockSpec((8, 128), lambda i, j, k: (0, 0)))
        args.append(after)
    return pl.pallas_call(
        body, name=name,
        out_shape=jax.ShapeDtypeStruct((M, N), out_dtype),
        grid=(M // tm, N // tn, nk),
        in_specs=in_specs,
        out_specs=pl.BlockSpec((tm, tn), lambda i, j, k: (i, j)),
        scratch_shapes=[pltpu.VMEM((tm, tn), F32)] if nk > 1 else [],
        compiler_params=_params(("parallel", "parallel", "arbitrary")),
    )(*args)


def _ln_fwd(z, g, b, name, tq=512):
    T, D = z.shape

    def body(z_ref, g_ref, b_ref, y_ref, yb_ref):
        zv = z_ref[...]
        mu = jnp.mean(zv, axis=1, keepdims=True)
        zc = zv - mu
        var = jnp.mean(zc * zc, axis=1, keepdims=True)
        y = zc * lax.rsqrt(var + LN_EPS) * g_ref[...] + b_ref[...]
        y_ref[...] = y
        yb_ref[...] = y.astype(BF16)

    row = pl.BlockSpec((tq, D), lambda i: (i, 0))
    vec = pl.BlockSpec((1, D), lambda i: (0, 0))
    return pl.pallas_call(
        body, name=name,
        out_shape=(jax.ShapeDtypeStruct((T, D), F32), jax.ShapeDtypeStruct((T, D), BF16)),
        grid=(T // tq,), in_specs=[row, vec, vec], out_specs=(row, row),
        compiler_params=_params(("parallel",)),
    )(z, g, b)


def _ln_bwd(dy, z, g, name, tq=512, target=None, for_matmul=True):
    T, D = z.shape
    with_loss = target is not None

    def body(*refs):
        dy_ref, z_ref, g_ref = refs[:3]
        outs = list(refs[4 if with_loss else 3:])
        dz_ref = outs.pop(0)
        dzb_ref, ds_ref = (outs.pop(0), outs.pop(0)) if for_matmul else (None, None)
        dg_ref, db_ref = outs.pop(0), outs.pop(0)
        sq_ref = outs.pop(0) if with_loss else None

        @pl.when(pl.program_id(0) == 0)
        def _():
            for ref in (dg_ref, db_ref, ds_ref, sq_ref):
                if ref is not None:
                    ref[...] = jnp.zeros_like(ref)

        zv, dyv = z_ref[...], dy_ref[...]
        if with_loss:
            err = dyv - refs[3][...]
            sq_ref[...] += jnp.sum(err * err)
            dyv = err * (1.0 / D)
        mu = jnp.mean(zv, axis=1, keepdims=True)
        zc = zv - mu
        var = jnp.mean(zc * zc, axis=1, keepdims=True)
        rstd = lax.rsqrt(var + LN_EPS)
        xh = zc * rstd
        u = dyv * g_ref[...]
        dz = rstd * (u - jnp.mean(u, axis=1, keepdims=True) - xh * jnp.mean(u * xh, axis=1, keepdims=True))
        dz_ref[...] = dz
        dg_ref[...] += jnp.sum(dyv * xh, axis=0, keepdims=True)
        db_ref[...] += jnp.sum(dyv, axis=0, keepdims=True)
        if for_matmul:
            dzb_ref[...] = dz.astype(BF16)
            ds_ref[...] += jnp.sum(dz, axis=0, keepdims=True)

    row = pl.BlockSpec((tq, D), lambda i: (i, 0))
    vec = pl.BlockSpec((1, D), lambda i: (0, 0))
    vshape = jax.ShapeDtypeStruct((1, D), F32)
    out_shape, out_specs = [jax.ShapeDtypeStruct((T, D), F32)], [row]
    if for_matmul:
        out_shape += [jax.ShapeDtypeStruct((T, D), BF16), vshape]
        out_specs += [row, vec]
    out_shape += [vshape, vshape]
    out_specs += [vec, vec]
    if with_loss:
        out_shape.append(jax.ShapeDtypeStruct((8, 128), F32))
        out_specs.append(pl.BlockSpec((8, 128), lambda i: (0, 0)))
    return pl.pallas_call(
        body, name=name, out_shape=tuple(out_shape), grid=(T // tq,),
        in_specs=[row, row, vec] + ([row] if with_loss else []), out_specs=tuple(out_specs),
        compiler_params=_params(("arbitrary",)),
    )(dy, z, g, *([target] if with_loss else []))


def _pblock(tq, width, offset):
    assert offset % width == 0
    blk = offset // width
    return pl.BlockSpec((tq, width), lambda i: (i, blk))


def _mix_fwd(proj, q_g, kv_g, w_pool, pool_scale, conv_w, name, tq=256):
    T = proj.shape[0]

    def body(ql_ref, kvl_ref, pin_ref, gp_ref, ch_ref, cb_ref, cc_ref, gc_ref, qg_ref, kvg_ref, wp_ref, ps_ref,
             cw_ref, qn_ref, kvn_ref, pooled_ref, cv_ref, ypc_ref, extp, extu):
        i = pl.program_id(0)
        for x_ref, g_ref, o_ref in ((ql_ref, qg_ref, qn_ref), (kvl_ref, kvg_ref, kvn_ref)):
            x = x_ref[...]
            r = lax.rsqrt(jnp.mean(x * x, axis=1, keepdims=True) + RMS_EPS)
            o_ref[...] = (x * r * g_ref[...]).astype(BF16)

        @pl.when(i == 0)
        def _():
            extp[0:HALO, :] = jnp.zeros((HALO, D_POOL), F32)
            extu[0:HALO, :] = jnp.zeros((HALO, D_CONV), F32)

        @pl.when(i > 0)
        def _():
            extp[0:HALO, :] = extp[tq:tq + HALO, :]
            extu[0:HALO, :] = extu[tq:tq + HALO, :]

        t1 = (i * tq + lax.broadcasted_iota(jnp.int32, (tq, 1), 0) + 1).astype(F32)
        for g, w in enumerate(POOL_WINDOWS):
            cols = slice(g * POOL_GROUP, (g + 1) * POOL_GROUP)
            pin = pin_ref[:, cols]
            extp[HALO:, cols] = pin
            s = extp[:, cols]
            k = 1
            while k < w:
                s = s + pltpu.roll(s, k, 0)
                k *= 2
            mean = s[HALO:, :] / jnp.minimum(t1, float(w))
            pooled = (mean - pin).astype(BF16)
            pooled_ref[:, cols] = pooled
            r = jnp.dot(pooled, wp_ref[g], preferred_element_type=F32)
            gp = gp_ref[:, cols]
            ypc_ref[:, cols] = (r * ps_ref[:, cols] * (gp * _sigmoid(gp))).astype(BF16)
        for g in range(D_CONV // 128):
            cols = slice(g * 128, (g + 1) * 128)
            u = cc_ref[:, cols] * ch_ref[:, cols]
            extu[HALO:, cols] = u
            eu = extu[:, cols]
            u1 = pltpu.roll(eu, 1, 0)[HALO:, :]
            u2 = pltpu.roll(eu, 2, 0)[HALO:, :]
            cv = cw_ref[0:1, cols] * u2 + cw_ref[1:2, cols] * u1 + cw_ref[2:3, cols] * u
            cv_ref[:, cols] = cv
            gc = gc_ref[:, cols]
            ypc_ref[:, D_POOL + g * 128:D_POOL + (g + 1) * 128] = (
                cb_ref[:, cols] * cv * (gc * _sigmoid(gc))).astype(BF16)

    full = lambda shape: pl.BlockSpec(shape, lambda i: (0,) * len(shape))
    row = lambda w: pl.BlockSpec((tq, w), lambda i: (i, 0))
    return pl.pallas_call(
        body, name=name,
        out_shape=(jax.ShapeDtypeStruct((T, Q_LORA), BF16), jax.ShapeDtypeStruct((T, KV_LORA), BF16),
                   jax.ShapeDtypeStruct((T, D_POOL), BF16), jax.ShapeDtypeStruct((T, D_CONV), F32),
                   jax.ShapeDtypeStruct((T, D_MIX), BF16)),
        grid=(T // tq,),
        in_specs=[_pblock(tq, Q_LORA, O_QLAT), _pblock(tq, KV_LORA, O_KVLAT), _pblock(tq, 512, O_PIN),
                  _pblock(tq, 512, O_GPOOL), _pblock(tq, 512, O_CH), _pblock(tq, 512, O_CB), _pblock(tq, 512, O_CC),
                  _pblock(tq, 512, O_GCONV), full((1, Q_LORA)), full((1, KV_LORA)), full((4, 128, 128)),
                  full((1, D_POOL)), full((8, D_CONV))],
        out_specs=(row(Q_LORA), row(KV_LORA), row(D_POOL), row(D_CONV),
                   pl.BlockSpec((tq, D_POOL + D_CONV), lambda i: (i, D_MLA // (D_POOL + D_CONV)))),
        scratch_shapes=[pltpu.VMEM((tq + HALO, D_POOL), F32), pltpu.VMEM((tq + HALO, D_CONV), F32)],
        compiler_params=_params(("arbitrary",)),
    )(proj, proj, proj, proj, proj, proj, proj, proj, q_g, kv_g, w_pool, pool_scale, conv_w)


def _mix_bwd(dmix, proj, o, pooled, cv, w_pool, pool_scale, conv_w, name, tq=ATT_CH):
    T = proj.shape[0]
    nt = T // tq
    n_ext = tq + HALO

    def body(dym_ref, dyp_ref, dyc_ref, gm_ref, gp_ref, ch_ref, cb_ref, cc_ref, gc_ref, o_ref, pooled_ref, cv_ref,
             wp_ref, ps_ref, cw_ref, do_ref, delta_ref, dg_ref, dwp_ref, dps_ref, dcw_ref, exte, extd):
        i = pl.program_id(0)
        tile = nt - 1 - i

        @pl.when(i == 0)
        def _():
            dwp_ref[...] = jnp.zeros_like(dwp_ref)
            dps_ref[...] = jnp.zeros_like(dps_ref)
            dcw_ref[...] = jnp.zeros_like(dcw_ref)
            exte[tq:, :] = jnp.zeros((HALO, D_POOL), F32)
            extd[tq:, :] = jnp.zeros((HALO, D_CONV), F32)

        @pl.when(i > 0)
        def _():
            exte[tq:, :] = exte[0:HALO, :]
            extd[tq:, :] = extd[0:HALO, :]

        ones = jnp.ones((8, V_DIM), F32)
        for h in range(N_HEADS):
            cols = slice(h * V_DIM, (h + 1) * V_DIM)
            gm = gm_ref[:, cols]
            sig = _sigmoid(gm)
            dym = dym_ref[:, cols]
            ov = o_ref[:, cols]
            do = dym * (gm * sig)
            do_ref[:, cols] = do.astype(BF16)
            rows = lax.dot_general(ones, do * ov, _DIMS["nt"], precision=lax.Precision.HIGHEST,
                                   preferred_element_type=F32)
            delta_ref[h, 0] = rows[0:1, :]
            dg_ref[:, O_GMLA + h * V_DIM:O_GMLA + (h + 1) * V_DIM] = (
                dym * ov * (sig * (1.0 + gm * (1.0 - sig)))).astype(BF16)

        t1 = (tile * tq + lax.broadcasted_iota(jnp.int32, (tq, 1), 0) + 1).astype(F32)
        for g, w in enumerate(POOL_WINDOWS):
            cols = slice(g * POOL_GROUP, (g + 1) * POOL_GROUP)
            pg = pooled_ref[:, cols]
            r = jnp.dot(pg, wp_ref[g], preferred_element_type=F32)
            gp = gp_ref[:, cols]
            sg = _sigmoid(gp)
            sl = gp * sg
            dyg = dyp_ref[:, cols]
            ps = ps_ref[:, cols]
            dg_ref[:, O_GPOOL + g * POOL_GROUP:O_GPOOL + (g + 1) * POOL_GROUP] = (
                dyg * (r * ps) * (sg * (1.0 + gp * (1.0 - sg)))).astype(BF16)
            dps_ref[:, cols] += jnp.sum(dyg * r * sl, axis=0, keepdims=True)
            dr = (dyg * ps * sl).astype(BF16)
            dwp_ref[g] += lax.dot_general(pg, dr, _DIMS["tn"], preferred_element_type=F32)
            dpooled = lax.dot_general(dr, wp_ref[g], _DIMS["nt"], preferred_element_type=F32)
            exte[0:tq, cols] = dpooled / jnp.minimum(t1, float(w))
            s = exte[:, cols]
            k = 1
            while k < w:
                s = s + pltpu.roll(s, n_ext - k, 0)
                k *= 2
            dg_ref[:, O_PIN + g * POOL_GROUP:O_PIN + (g + 1) * POOL_GROUP] = (s[0:tq, :] - dpooled).astype(BF16)

        for g in range(D_CONV // 128):
            cols = slice(g * 128, (g + 1) * 128)
            out = lambda base: slice(base + g * 128, base + (g + 1) * 128)
            gc = gc_ref[:, cols]
            sg = _sigmoid(gc)
            sl = gc * sg
            dyc = dyc_ref[:, cols]
            cb, cc, ch, cvv = cb_ref[:, cols], cc_ref[:, cols], ch_ref[:, cols], cv_ref[:, cols]
            dcv = dyc * cb * sl
            dg_ref[:, out(O_GCONV)] = (dyc * (cb * cvv) * (sg * (1.0 + gc * (1.0 - sg)))).astype(BF16)
            dg_ref[:, out(O_CB)] = (dyc * cvv * sl).astype(BF16)
            extd[0:tq, cols] = dcv
            ed = extd[:, cols]
            d1 = pltpu.roll(ed, n_ext - 1, 0)[0:tq, :]
            d2 = pltpu.roll(ed, n_ext - 2, 0)[0:tq, :]
            du = cw_ref[2:3, cols] * dcv + cw_ref[1:2, cols] * d1 + cw_ref[0:1, cols] * d2
            u = cc * ch
            dcw_ref[0:1, cols] += jnp.sum(u * d2, axis=0, keepdims=True)
            dcw_ref[1:2, cols] += jnp.sum(u * d1, axis=0, keepdims=True)
            dcw_ref[2:3, cols] += jnp.sum(u * dcv, axis=0, keepdims=True)
            dg_ref[:, out(O_CH)] = (du * cc).astype(BF16)
            dg_ref[:, out(O_CC)] = (du * ch).astype(BF16)

    def rblock(width, offset):
        assert offset % width == 0
        blk = offset // width
        return pl.BlockSpec((tq, width), lambda i: (nt - 1 - i, blk))

    full = lambda shape: pl.BlockSpec(shape, lambda i: (0,) * len(shape))
    return pl.pallas_call(
        body, name=name,
        out_shape=(jax.ShapeDtypeStruct((T, D_MLA), BF16), jax.ShapeDtypeStruct((N_HEADS, nt, 1, tq), F32),
                   jax.ShapeDtypeStruct((T, NPP), BF16),
                   jax.ShapeDtypeStruct((4, 128, 128), F32), jax.ShapeDtypeStruct((1, D_POOL), F32),
                   jax.ShapeDtypeStruct((8, D_CONV), F32)),
        grid=(nt,),
        in_specs=[rblock(1024, 0), rblock(512, 1024), rblock(512, 1536),
                  rblock(1024, O_GMLA), rblock(512, O_GPOOL), rblock(512, O_CH), rblock(512, O_CB),
                  rblock(512, O_CC), rblock(512, O_GCONV), rblock(1024, 0), rblock(512, 0), rblock(512, 0),
                  full((4, 128, 128)), full((1, D_POOL)), full((8, D_CONV))],
        out_specs=(rblock(1024, 0), pl.BlockSpec((N_HEADS, 1, 1, tq), lambda i: (0, nt - 1 - i, 0, 0)),
                   rblock(N_GATED, 0), full((4, 128, 128)), full((1, D_POOL)), full((8, D_CONV))),
        scratch_shapes=[pltpu.VMEM((n_ext, D_POOL), F32), pltpu.VMEM((n_ext, D_CONV), F32)],
        compiler_params=_params(("arbitrary",)),
    )(dmix, dmix, dmix, proj, proj, proj, proj, proj, proj, o, pooled, cv, w_pool, pool_scale, conv_w)


def _rms_bwd(proj, dqn, dkvn, dkrope, dproj, q_g, kv_g, name, tq=256):
    T = proj.shape[0]
    n_lat = NPP - N_GATED

    def body(ql_ref, kvl_ref, dqn_ref, dkvn_ref, dkr_ref, _, qg_ref, kvg_ref, dlat_ref, dqg_ref, dkvg_ref):
        @pl.when(pl.program_id(0) == 0)
        def _():
            dqg_ref[...] = jnp.zeros_like(dqg_ref)
            dkvg_ref[...] = jnp.zeros_like(dkvg_ref)

        for x_ref, dy_ref, g_ref, c0, dg_ref in ((ql_ref, dqn_ref, qg_ref, 0, dqg_ref),
                                                 (kvl_ref, dkvn_ref, kvg_ref, Q_LORA, dkvg_ref)):
            x, dy = x_ref[...], dy_ref[...]
            r = lax.rsqrt(jnp.mean(x * x, axis=1, keepdims=True) + RMS_EPS)
            xr = x * r
            u = dy * g_ref[...]
            dlat_ref[:, c0:c0 + x.shape[1]] = (r * (u - xr * jnp.mean(u * xr, axis=1, keepdims=True))).astype(BF16)
            dg_ref[...] += jnp.sum(dy * xr, axis=0, keepdims=True)
        dlat_ref[:, Q_LORA + KV_LORA:] = dkr_ref[...]

    row = lambda w: pl.BlockSpec((tq, w), lambda i: (i, 0))
    vec = lambda w: pl.BlockSpec((1, w), lambda i: (0, 0))
    assert N_GATED % n_lat == 0
    return pl.pallas_call(
        body, name=name,
        out_shape=(jax.ShapeDtypeStruct((T, NPP), BF16),
                   jax.ShapeDtypeStruct((1, Q_LORA), F32), jax.ShapeDtypeStruct((1, KV_LORA), F32)),
        grid=(T // tq,),
        in_specs=[_pblock(tq, Q_LORA, O_QLAT), _pblock(tq, KV_LORA, O_KVLAT), row(Q_LORA), row(KV_LORA),
                  row(n_lat - Q_LORA - KV_LORA), pl.BlockSpec(memory_space=pl.ANY), vec(Q_LORA), vec(KV_LORA)],
        out_specs=(pl.BlockSpec((tq, n_lat), lambda i: (i, N_GATED // n_lat)), vec(Q_LORA), vec(KV_LORA)),
        input_output_aliases={5: 0},
        compiler_params=_params(("arbitrary",)),
    )(proj, proj, dqn, dkvn, dkrope, dproj, q_g, kv_g)


def _swap_halves(x, lo):
    return jnp.where(lo, pltpu.roll(x, 96, 1), pltpu.roll(x, 32, 1))


def _rope_fwd(q, kv, proj, cos_t, sin_t, name, tq=256):
    T = q.shape[0]

    def body(qn_ref, qr_ref, kv_ref, kr_ref, c_ref, s_ref, qc_ref, kc_ref):
        C, S = c_ref[...], s_ref[...]
        lane = lax.broadcasted_iota(jnp.int32, (tq, 128), 1)
        lo = (lane % ROPE) < (ROPE // 2)
        first = lane < ROPE

        def rope(x):
            return x * C + _swap_halves(x, lo) * S

        kr = jnp.where(first, rope(kr_ref[...]), 0.0).astype(BF16)
        for j in range(N_HEADS // 2):
            r = rope(qr_ref[:, j * 128:(j + 1) * 128])
            pair = (jnp.where(first, r, 0.0), jnp.where(first, pltpu.roll(r, 64, 1), 0.0))
            for hh in range(2):
                h = 2 * j + hh
                qc_ref[h, :, 0:NOPE] = qn_ref[:, h * NOPE:(h + 1) * NOPE].astype(BF16)
                qc_ref[h, :, NOPE:QC] = pair[hh].astype(BF16)
        for h in range(N_HEADS):
            kc_ref[h, :, 0:NOPE] = kv_ref[:, h * 256:h * 256 + NOPE]
            kc_ref[h, :, NOPE:QC] = kr

    out = jax.ShapeDtypeStruct((N_HEADS, T, QC), BF16)
    hblock = pl.BlockSpec((N_HEADS, tq, QC), lambda i: (0, i, 0))
    return pl.pallas_call(
        body, name=name, out_shape=(out, out), grid=(T // tq,),
        in_specs=[pl.BlockSpec((tq, 1024), lambda i: (i, 0)), pl.BlockSpec((tq, 512), lambda i: (i, 2)),
                  pl.BlockSpec((tq, 2048), lambda i: (i, 0)), _pblock(tq, 128, O_KROPE),
                  pl.BlockSpec((tq, 128), lambda i: (i, 0)), pl.BlockSpec((tq, 128), lambda i: (i, 0))],
        out_specs=(hblock, hblock),
        compiler_params=_params(("parallel",)),
    )(q, q, kv, proj, cos_t, sin_t)


def _rope_bwd(dqc, dkr, cos_t, sin_t, name, tq=256):
    T = dqc.shape[1]

    def body(dqc_ref, dkr_ref, c_ref, s_ref, dq_ref, dk_ref):
        C, S = c_ref[...], s_ref[...]
        lane = lax.broadcasted_iota(jnp.int32, (tq, 128), 1)
        lo = (lane % ROPE) < (ROPE // 2)
        first = lane < ROPE

        def unrope(dy):
            return dy * C - _swap_halves(dy, lo) * S

        acc = dkr_ref[0]
        for h in range(1, N_HEADS):
            acc = acc + dkr_ref[h]
        dk_ref[:, 0:128] = jnp.where(first, unrope(acc), 0.0).astype(BF16)
        dk_ref[:, 128:256] = jnp.zeros((tq, 128), BF16)
        for j in range(N_HEADS // 2):
            d0 = dqc_ref[2 * j, :, NOPE:QC]
            d1 = dqc_ref[2 * j + 1, :, NOPE:QC]
            comb = jnp.where(first, d0, pltpu.roll(d1, 64, 1))
            dq_ref[:, 1024 + j * 128:1024 + (j + 1) * 128] = unrope(comb).astype(BF16)
        for h in range(N_HEADS):
            dq_ref[:, h * NOPE:(h + 1) * NOPE] = dqc_ref[h, :, 0:NOPE].astype(BF16)

    tab = pl.BlockSpec((tq, 128), lambda i: (i, 0))
    return pl.pallas_call(
        body, name=name,
        out_shape=(jax.ShapeDtypeStruct((T, 1536), BF16), jax.ShapeDtypeStruct((T, 256), BF16)),
        grid=(T // tq,),
        in_specs=[pl.BlockSpec((N_HEADS, tq, QC), lambda i: (0, i, 0)),
                  pl.BlockSpec((N_HEADS, tq, 128), lambda i: (0, i, 0)), tab, tab],
        out_specs=(pl.BlockSpec((tq, 1536), lambda i: (i, 0)), pl.BlockSpec((tq, 256), lambda i: (i, 0))),
        compiler_params=_params(("parallel",)),
    )(dqc, dkr, cos_t, sin_t)


def _flash_fwd(qc, kc, kv, proj, mix, name):
    H, T, _ = qc.shape
    tt = ATT_TILE
    nt = T // tt
    sp = tt // ATT_CH

    def body(q_ref, k_ref, v_ref, g_ref, _, o_ref, y_ref, lse_ref, vt_sc, s_sc, acc_sc, m_sc, l_sc, bias_sc):
        i = pl.program_id(1)

        @pl.when(i == 0)
        def _():
            for c in range(nt):
                vt_sc[c] = v_ref[c * tt:(c + 1) * tt, :].astype(F32).T.astype(BF16)
            krow = lax.broadcasted_iota(jnp.int32, (tt, tt), 0)
            qcol = lax.broadcasted_iota(jnp.int32, (tt, tt), 1)
            bias_sc[...] = jnp.where(krow <= qcol, 0.0, -jnp.inf)

        q = q_ref[0]

        def issue(c, slot):
            s_sc[slot] = lax.dot_general(k_ref[0, pl.ds(pl.multiple_of(c * tt, tt), tt), :], q, _DIMS["nt"],
                                         preferred_element_type=F32)

        def softmax_pv(c, slot, masked):
            s = s_sc[slot]
            if masked:
                s = s + bias_sc[...]
            m = m_sc[...]
            m_new = jnp.maximum(m, jnp.max(s, axis=0, keepdims=True))
            p = jnp.exp2((s - m_new) * EXP2_SCALE)
            a = jnp.exp2((m - m_new) * EXP2_SCALE)
            l_sc[...] = a * l_sc[...] + jnp.sum(p, axis=0, keepdims=True)
            acc_sc[...] = a * acc_sc[...] + jnp.dot(vt_sc[c], p.astype(BF16), preferred_element_type=F32)
            m_sc[...] = m_new

        issue(0, 0)
        m_sc[...] = jnp.full_like(m_sc, -jnp.inf)
        l_sc[...] = jnp.zeros_like(l_sc)
        acc_sc[...] = jnp.zeros_like(acc_sc)

        def pair(t, carry):
            issue(2 * t + 1, 1)
            softmax_pv(2 * t, 0, False)
            issue(2 * t + 2, 0)
            softmax_pv(2 * t + 1, 1, False)
            return carry

        lax.fori_loop(0, i // 2, pair, 0)

        @pl.when(i % 2 == 1)
        def _():
            issue(i, 1)
            softmax_pv(i - 1, 0, False)
            softmax_pv(i, 1, True)

        @pl.when(i % 2 == 0)
        def _():
            softmax_pv(i, 0, True)

        l = l_sc[...]
        o = (acc_sc[...] / l).T
        o_ref[...] = o
        lse = m_sc[...] * ATTN_SCALE + jnp.log(l)
        for r in range(sp):
            lse_ref[0, r] = lse[:, r * ATT_CH:(r + 1) * ATT_CH]
        g = g_ref[...]
        y_ref[...] = (o * (g * _sigmoid(g))).astype(BF16)

    return pl.pallas_call(
        body, name=name,
        out_shape=(jax.ShapeDtypeStruct((T, D_MLA), F32), jax.ShapeDtypeStruct((T, D_MIX), BF16),
                   jax.ShapeDtypeStruct((H, T // ATT_CH, 1, ATT_CH), F32)),
        grid=(H, nt),
        in_specs=[pl.BlockSpec((1, tt, QC), lambda h, i: (h, i, 0)),
                  pl.BlockSpec((1, T, QC), lambda h, i: (h, 0, 0)),
                  pl.BlockSpec((T, V_DIM), lambda h, i: (0, 2 * h + 1)),
                  pl.BlockSpec((tt, V_DIM), lambda h, i: (i, h)),
                  pl.BlockSpec(memory_space=pl.ANY)],
        input_output_aliases={4: 1},
        out_specs=(pl.BlockSpec((tt, V_DIM), lambda h, i: (i, h)),
                   pl.BlockSpec((tt, V_DIM), lambda h, i: (i, h)),
                   pl.BlockSpec((1, sp, 1, ATT_CH), lambda h, i: (h, i, 0, 0))),
        scratch_shapes=[pltpu.VMEM((nt, V_DIM, tt), BF16), pltpu.VMEM((2, tt, tt), F32),
                        pltpu.VMEM((V_DIM, tt), F32), pltpu.VMEM((1, tt), F32), pltpu.VMEM((1, tt), F32),
                        pltpu.VMEM((tt, tt), F32)],
        compiler_params=_params(("parallel", "arbitrary")),
    )(qc, kc, kv, proj, mix)


def _flash_bwd(qc, kc, kv, do, lse, delta, name):
    H, T, _ = qc.shape
    tt = ATT_TILE
    nt = T // tt
    sp = tt // ATT_CH

    def body(q_ref, k_ref, v_ref, do_ref, lse_ref, dl_ref, dq_ref, dkv_ref, dkr_ref, dqt_sc, dk_sc, dv_sc, s_sc,
             dp_sc, kt_sc):
        j = pl.program_id(1)

        @pl.when(j == 0)
        def _():
            dqt_sc[...] = jnp.zeros_like(dqt_sc)

        k = k_ref[0]
        v = v_ref[...]

        def operands(c):
            q0 = pl.multiple_of(c * tt, tt)
            return q_ref[0, pl.ds(q0, tt), :], do_ref[pl.ds(q0, tt), :]

        def stat_row(ref, c):
            return jnp.concatenate([ref[0, sp * c + r] for r in range(sp)], axis=1)

        def early(c, slot):
            q, dov = operands(c)
            s_sc[slot] = lax.dot_general(k, q, _DIMS["nt"], preferred_element_type=F32)
            dp_sc[slot] = lax.dot_general(v, dov, _DIMS["nt"], preferred_element_type=F32)

        def late(c, slot, masked, kt):
            q, dov = operands(c)
            s, dp = s_sc[slot], dp_sc[slot]
            if masked:
                krow = j * tt + lax.broadcasted_iota(jnp.int32, s.shape, 0)
                qcol = c * tt + lax.broadcasted_iota(jnp.int32, s.shape, 1)
                s = jnp.where(krow <= qcol, s, -jnp.inf)
            p = jnp.exp2(s * EXP2_SCALE - stat_row(lse_ref, c) * LOG2E)
            ds = (p * (dp - stat_row(dl_ref, c)) * ATTN_SCALE).astype(BF16)
            dv = jnp.dot(p.astype(BF16), dov, preferred_element_type=F32)
            dk = jnp.dot(ds, q, preferred_element_type=F32)
            if masked:
                dv_sc[slot] = dv
                dk_sc[slot] = dk
            else:
                dv_sc[slot] += dv
                dk_sc[slot] += dk
            dqt_sc[c] += jnp.dot(kt, ds, preferred_element_type=F32)

        n_rest = nt - 1 - j
        odd = n_rest % 2

        def setup(other_slot):
            dk_sc[other_slot] = jnp.zeros((tt, QC), F32)
            dv_sc[other_slot] = jnp.zeros((tt, V_DIM), F32)
            kt_sc[...] = k.astype(F32).T.astype(BF16)

        @pl.when(odd == 0)
        def _():
            early(j, 0)
            early(jnp.minimum(j + 1, nt - 1), 1)
            setup(1)
            late(j, 0, True, kt_sc[...])

        @pl.when(odd == 1)
        def _():
            early(j, 1)
            early(j + 1, 0)
            setup(0)
            kt = kt_sc[...]
            late(j, 1, True, kt)
            early(jnp.minimum(j + 2, nt - 1), 1)
            late(j + 1, 0, False, kt)

        def pair(u, carry):
            a = j + 1 + odd + 2 * u
            kt = kt_sc[...]
            early(a + 1, 0)
            late(a, 1, False, kt)
            early(jnp.minimum(a + 2, nt - 1), 1)
            late(a + 1, 0, False, kt)
            return carry

        lax.fori_loop(0, n_rest // 2, pair, 0)
        dk = dk_sc[0] + dk_sc[1]
        dkv_ref[:, 0:NOPE] = dk[:, 0:NOPE].astype(BF16)
        dkv_ref[:, NOPE:] = (dv_sc[0] + dv_sc[1]).astype(BF16)
        dkr_ref[0] = dk[:, NOPE:]

        @pl.when(j == nt - 1)
        def _():
            for c in range(nt):
                dq_ref[0, c * tt:(c + 1) * tt, :] = dqt_sc[c].T

    head = lambda h, j: (h, 0, 0)
    stat = pl.BlockSpec((1, T // ATT_CH, 1, ATT_CH), lambda h, j: (h, 0, 0, 0))
    return pl.pallas_call(
        body, name=name,
        out_shape=(jax.ShapeDtypeStruct((H, T, QC), F32), jax.ShapeDtypeStruct((T, 2 * D_MLA), BF16),
                   jax.ShapeDtypeStruct((H, T, 128), F32)),
        grid=(H, nt),
        in_specs=[pl.BlockSpec((1, T, QC), head),
                  pl.BlockSpec((1, tt, QC), lambda h, j: (h, j, 0)),
                  pl.BlockSpec((tt, V_DIM), lambda h, j: (j, 2 * h + 1)),
                  pl.BlockSpec((T, V_DIM), lambda h, j: (0, h)),
                  stat, stat],
        out_specs=(pl.BlockSpec((1, T, QC), head),
                   pl.BlockSpec((tt, 256), lambda h, j: (j, h)),
                   pl.BlockSpec((1, tt, 128), lambda h, j: (h, j, 0))),
        scratch_shapes=[pltpu.VMEM((nt, QC, tt), F32), pltpu.VMEM((2, tt, QC), F32), pltpu.VMEM((2, tt, V_DIM), F32),
                        pltpu.VMEM((2, tt, tt), F32), pltpu.VMEM((2, tt, tt), F32), pltpu.VMEM((QC, tt), BF16)],
        compiler_params=_params(("parallel", "arbitrary")),
    )(qc, kc, kv, do, lse, delta)


def _adamw(lands, w, m, v, name, rows, cols=None, first_layer=0, into=None):
    layers, R, C = w.shape
    L = len(lands)
    cols = C if cols is None else cols
    assert R % rows == 0 and C % cols == 0 and first_layer + L <= layers
    nc = C // cols
    nb = (R // rows) * nc
    c1 = 1.0 - ADAM_B1 ** ADAM_STEP
    c2 = 1.0 - ADAM_B2 ** ADAM_STEP

    def body(*refs):
        land_refs = refs[:L]
        w_ref, m_ref, v_ref = refs[L:L + 3]
        g_ref, d_ref, nm_ref, nv_ref, g_sc = refs[-5:]
        for ll in range(L):
            @pl.when(pl.program_id(0) == ll)
            def _(land_ref=land_refs[ll]):
                g = land_ref[0].astype(F32)
                for s in range(1, N_DEV):
                    g = g + land_ref[s].astype(F32)
                g_sc[...] = g

        g = g_sc[...]
        nm = ADAM_B1 * m_ref[0] + (1.0 - ADAM_B1) * g
        nv = ADAM_B2 * v_ref[0] + (1.0 - ADAM_B2) * (g * g)
        g_ref[0] = g
        nm_ref[0] = nm
        nv_ref[0] = nv
        d_ref[0] = -ADAM_LR * ((nm / c1) / (jnp.sqrt(nv / c2) + ADAM_EPS) + ADAM_WD * w_ref[0])

    def land_spec(ll):
        def index(l, i):
            i = jnp.where(l < ll, 0, jnp.where(l > ll, nb - 1, i))
            return (0, i // nc, i % nc)
        return pl.BlockSpec((N_DEV, rows, cols), index)

    blk = pl.BlockSpec((1, rows, cols), lambda l, i: (first_layer + l, i // nc, i % nc))
    out = jax.ShapeDtypeStruct((layers, R, C), F32)
    extra = [] if into is None else list(into)
    return pl.pallas_call(
        body, name=name, out_shape=(out, out, out, out), grid=(L, nb),
        in_specs=[land_spec(ll) for ll in range(L)] + [blk, blk, blk] + [pl.BlockSpec(memory_space=pl.ANY)] * len(extra),
        out_specs=(blk, blk, blk, blk),
        input_output_aliases={L + 3 + i: i for i in range(len(extra))},
        scratch_shapes=[pltpu.VMEM((rows, cols), F32)],
        compiler_params=_params(("arbitrary", "arbitrary")),
    )(*lands, w, m, v, *extra)


def _mesh_pos():
    return lax.axis_index("x"), lax.axis_index("y"), lax.axis_index("c")


def _all_gather(arrays, name):
    n = len(arrays)

    def body(*refs):
        ins, outs = refs[:n], refs[n:2 * n]
        send_sems, recv_sems, local_sems = refs[2 * n:]
        x, y, c = _mesh_pos()
        me, sibling = (x, y, c), (x, y, 1 - c)
        chips = [(1 - x, y), (x, 1 - y), (1 - x, 1 - y)]

        def slot(a, pos):
            px, py, pc = pos
            return outs[a].at[4 * px + 2 * py + pc]

        def copy(a, k, block, to, src=None):
            return pltpu.make_async_remote_copy(
                src_ref=slot(a, block) if src is None else src, dst_ref=slot(a, block),
                send_sem=send_sems.at[a * 7 + k], recv_sem=recv_sems.at[a * 7 + k],
                device_id=to, device_id_type=MESH_ID)

        mine, first, passed = [], [], []
        for a in range(n):
            cp = pltpu.make_async_copy(ins[a], slot(a, me), local_sems.at[a])
            cp.start()
            mine.append(cp)
            cps = [copy(a, 0, me, sibling, src=ins[a])]
            cps += [copy(a, 1 + j, me, (*chip, c), src=ins[a]) for j, chip in enumerate(chips)]
            for cp in cps:
                cp.start()
            first += cps
        for j, chip in enumerate(chips):
            for a in range(n):
                copy(a, 1 + j, (*chip, c), me).wait_recv()
                cp = copy(a, 4 + j, (*chip, c), sibling)
                cp.start()
                passed.append(cp)
        for a in range(n):
            copy(a, 0, sibling, me).wait_recv()
            for j, chip in enumerate(chips):
                copy(a, 4 + j, (*chip, 1 - c), me).wait_recv()
        for cp in first + passed:
            cp.wait_send()
        for cp in mine:
            cp.wait()

    hbm = pl.BlockSpec(memory_space=pltpu.HBM)
    return pl.pallas_call(
        body, name=name,
        out_shape=tuple(jax.ShapeDtypeStruct((N_DEV,) + a.shape, a.dtype) for a in arrays),
        in_specs=[hbm] * n, out_specs=tuple([hbm] * n),
        scratch_shapes=[pltpu.SemaphoreType.DMA((7 * n,)), pltpu.SemaphoreType.DMA((7 * n,)),
                        pltpu.SemaphoreType.DMA((n,))],
    )(*arrays)


def _all_gather_under_ln(arrays, x, g, b, name, tq=512):
    n = len(arrays)
    T, D = x.shape
    nt = T // tq

    def body(*refs):
        x_ref, g_ref, b_ref = refs[:3]
        ins = refs[3:3 + n]
        y_ref, yb_ref = refs[3 + n:5 + n]
        outs = refs[5 + n:5 + 2 * n]
        send_sems, recv_sems, local_sems = refs[5 + 2 * n:]
        i = pl.program_id(0)
        mx, my, mc = _mesh_pos()
        me, sibling = (mx, my, mc), (mx, my, 1 - mc)
        chips = [(1 - mx, my), (mx, 1 - my), (1 - mx, 1 - my)]

        def slot(a, pos):
            px, py, pc = pos
            return outs[a].at[4 * px + 2 * py + pc]

        def copy(a, k, block, to, src=None):
            return pltpu.make_async_remote_copy(
                src_ref=slot(a, block) if src is None else src, dst_ref=slot(a, block),
                send_sem=send_sems.at[a * 7 + k], recv_sem=recv_sems.at[a * 7 + k],
                device_id=to, device_id_type=MESH_ID)

        def own(a):
            return pltpu.make_async_copy(ins[a], slot(a, me), local_sems.at[a])

        def first(a):
            return [copy(a, 0, me, sibling, src=ins[a])] + [
                copy(a, 1 + j, me, (*chip, mc), src=ins[a]) for j, chip in enumerate(chips)]

        @pl.when(i == 0)
        def _():
            for a in range(n):
                own(a).start()
                for cp in first(a):
                    cp.start()

        zv = x_ref[...]
        mu = jnp.mean(zv, axis=1, keepdims=True)
        zc = zv - mu
        var = jnp.mean(zc * zc, axis=1, keepdims=True)
        y = zc * lax.rsqrt(var + LN_EPS) * g_ref[...] + b_ref[...]
        y_ref[...] = y
        yb_ref[...] = y.astype(BF16)

        @pl.when(i == nt - 1)
        def _():
            passed = []
            for j, chip in enumerate(chips):
                for a in range(n):
                    copy(a, 1 + j, (*chip, mc), me).wait_recv()
                    cp = copy(a, 4 + j, (*chip, mc), sibling)
                    cp.start()
                    passed.append(cp)
            for a in range(n):
                copy(a, 0, sibling, me).wait_recv()
                for j, chip in enumerate(chips):
                    copy(a, 4 + j, (*chip, 1 - mc), me).wait_recv()
            for a in range(n):
                for cp in first(a):
                    cp.wait_send()
                own(a).wait()
            for cp in passed:
                cp.wait_send()

    row = pl.BlockSpec((tq, D), lambda i: (i, 0))
    vec = pl.BlockSpec((1, D), lambda i: (0, 0))
    hbm = pl.BlockSpec(memory_space=pltpu.HBM)
    outs = pl.pallas_call(
        body, name=name,
        out_shape=(jax.ShapeDtypeStruct((T, D), F32), jax.ShapeDtypeStruct((T, D), BF16))
        + tuple(jax.ShapeDtypeStruct((N_DEV,) + a.shape, a.dtype) for a in arrays),
        grid=(nt,), in_specs=[row, vec, vec] + [hbm] * n, out_specs=tuple([row, row] + [hbm] * n),
        scratch_shapes=[pltpu.SemaphoreType.DMA((7 * n,)), pltpu.SemaphoreType.DMA((7 * n,)),
                        pltpu.SemaphoreType.DMA((n,))],
        compiler_params=_params(("arbitrary",)),
    )(x, g, b, *arrays)
    return outs[0], outs[1], outs[2:]


_HBM = pl.BlockSpec(memory_space=pltpu.HBM)
_SEM = pl.BlockSpec(memory_space=pltpu.SEMAPHORE)
_EFFECT = pltpu.SideEffectType.DATAFLOW_SIDE_EFFECTING
N_PEERS = N_DEV - 1


def _peer(k):
    x, y, c = _mesh_pos()
    return (1 - x if k & 4 else x, 1 - y if k & 2 else y, 1 - c if k & 1 else c)


def _split_start(srcs, scatter, after, name):
    n = len(srcs)
    zones = [jax.ShapeDtypeStruct(s.shape if scatter else ((N_DEV,) + s.shape), s.dtype) for s in srcs]

    def body(*refs):
        src, zone = refs[:n], refs[n:2 * n]
        outs = refs[2 * n + 1:]
        send, recv, token = outs[:n], outs[n:2 * n], outs[4 * n]
        x, y, c = _mesh_pos()
        my_idx = 4 * x + 2 * y + c
        for a in range(n):
            pltpu.make_async_copy(src[a].at[my_idx] if scatter else src[a],
                                  zone[a].at[N_PEERS] if scatter else zone[a].at[my_idx], recv[a]).start()
            for k in range(1, N_DEV):
                px, py, pc = _peer(k)
                pltpu.make_async_remote_copy(
                    src_ref=src[a].at[4 * px + 2 * py + pc] if scatter else src[a],
                    dst_ref=zone[a].at[k - 1] if scatter else zone[a].at[my_idx],
                    send_sem=send[a], recv_sem=recv[a], device_id=(px, py, pc), device_id_type=MESH_ID).start()
        token[...] = jnp.zeros_like(token)

    hbm = lambda a: pltpu.with_memory_space_constraint(a, pltpu.HBM)
    outs = pl.pallas_call(
        body, name=name,
        out_shape=tuple([pltpu.SemaphoreType.DMA(())] * (2 * n)
                        + [pltpu.HBM(s.shape, s.dtype) for s in srcs]
                        + [pltpu.HBM(z.shape, z.dtype) for z in zones]
                        + [jax.ShapeDtypeStruct((8, 128), F32)]),
        in_specs=[_HBM] * (2 * n) + [pl.BlockSpec(memory_space=pl.ANY)],
        out_specs=tuple([_SEM] * (2 * n) + [_HBM] * (2 * n) + [pl.BlockSpec(memory_space=pltpu.VMEM)]),
        input_output_aliases={**{a: 2 * n + a for a in range(n)}, **{n + a: 3 * n + a for a in range(n)}},
        compiler_params=pltpu.CompilerParams(has_side_effects=_EFFECT),
    )(*[hbm(s) for s in srcs], *[hbm(lax.empty(z.shape, z.dtype)) for z in zones], after)
    return outs[:n], outs[n:2 * n], outs[2 * n:3 * n], outs[3 * n:4 * n], outs[4 * n]


def _split_wait(send, recv, srcs, zones, after, name):
    n = len(srcs)

    def body(*refs):
        zone = refs[n:2 * n]
        send_sems, recv_sems = refs[2 * n:3 * n], refs[3 * n:4 * n]
        x, y, c = _mesh_pos()
        for a in range(n):
            seven = zone[a].at[pl.ds(0, N_PEERS)]
            pltpu.make_async_remote_copy(src_ref=seven, dst_ref=seven, send_sem=send_sems[a], recv_sem=recv_sems[a],
                                         device_id=(x, y, 1 - c), device_id_type=MESH_ID).wait_send()
            pltpu.make_async_remote_copy(src_ref=zone[a], dst_ref=zone[a], send_sem=send_sems[a],
                                         recv_sem=recv_sems[a], device_id=(x, y, 1 - c),
                                         device_id_type=MESH_ID).wait_recv()

    outs = pl.pallas_call(
        body, name=name,
        out_shape=tuple([pltpu.HBM(s.shape, s.dtype) for s in srcs] + [pltpu.HBM(z.shape, z.dtype) for z in zones]),
        in_specs=[_HBM] * (2 * n) + [_SEM] * (2 * n) + [pl.BlockSpec(memory_space=pl.ANY)],
        out_specs=tuple([_HBM] * (2 * n)),
        input_output_aliases={a: a for a in range(2 * n)},
        compiler_params=pltpu.CompilerParams(has_side_effects=_EFFECT),
    )(*srcs, *zones, *send, *recv, after)
    return outs[:n], outs[n:]


def _cat_blocks(g, axis):
    return jnp.concatenate([g[d] for d in range(N_DEV)], axis=axis)


N_LATENT = Q_LORA + KV_LORA + ROPE
W_SHARD = D_IN_PROJ // N_DEV


def _ref_cols(lo, hi):
    out = []
    if lo < N_LATENT:
        out.append((N_GATED + lo, N_GATED + min(hi, N_LATENT)))
    if hi > N_LATENT:
        out.append((max(lo, N_LATENT) - N_LATENT, hi - N_LATENT))
    return out


def _permute_w_in_t(blocks):
    pieces = []
    for lo, hi in ((N_LATENT, D_IN_PROJ), (0, N_LATENT)):
        for d in range(N_DEV):
            a, b = max(lo, d * W_SHARD), min(hi, (d + 1) * W_SHARD)
            if a < b:
                pieces.append(blocks[d][a - d * W_SHARD:b - d * W_SHARD])
    pieces.append(jnp.zeros((NPP - D_IN_PROJ, blocks.shape[2]), blocks.dtype))
    return jnp.concatenate(pieces, axis=0)


def _split_w_in_t(w):
    slabs = []
    for d in range(N_DEV):
        parts = [w[a:b] for a, b in _ref_cols(d * W_SHARD, (d + 1) * W_SHARD)]
        slabs.append(parts[0] if len(parts) == 1 else jnp.concatenate(parts, axis=0))
    return jnp.stack(slabs)


def _permute_w_uq(w):
    w3 = w.reshape(w.shape[0], N_HEADS, NOPE + ROPE)
    return jnp.concatenate([w3[:, :, :NOPE].reshape(w.shape[0], -1), w3[:, :, NOPE:].reshape(w.shape[0], -1)], axis=1)


def _unpermute_w_uq(w):
    nope = w[:, :N_HEADS * NOPE].reshape(w.shape[0], N_HEADS, NOPE)
    rope = w[:, N_HEADS * NOPE:].reshape(w.shape[0], N_HEADS, ROPE)
    return jnp.concatenate([nope, rope], axis=2).reshape(w.shape[0], -1)


_SMALL_EMB = (("emb_ln_g", 16), ("emb_ln_b", 16))
_SMALL_LAYER = (("q_norm_g", 8), ("kv_norm_g", 8), ("w_pool", 1024), ("pool_scale", 8), ("b_out", 32),
                ("ln_g", 32), ("ln_b", 32))
_SMALL = _SMALL_EMB + _SMALL_LAYER
CONV_ROWS = DEPTH * CONV_WIDTH * D_CONV // 128


def _pack_small(d, entries=_SMALL):
    parts = []
    for name, rows in entries:
        flat = d[name].reshape(-1)
        flat = jnp.pad(flat, (0, rows * 128 - flat.shape[0]))
        parts.append(flat.reshape(rows, 128))
    return jnp.concatenate(parts, axis=0)


def _unpack_small(packed, shapes):
    out, r0 = {}, 0
    for name, rows in _SMALL:
        size = 1
        for s in shapes[name]:
            size *= s
        out[name] = packed[r0:r0 + rows].reshape(-1)[:size].reshape(shapes[name])
        r0 += rows
    return out


def _rope_tables(positions):
    half = ROPE // 2
    inv_freq = ROPE_THETA ** (-jnp.arange(half, dtype=F32) / half)
    ang = positions.astype(F32)[:, None] * inv_freq
    cos, sin = jnp.cos(ang), jnp.sin(ang)
    return jnp.concatenate([cos, cos, cos, cos], axis=1), jnp.concatenate([-sin, sin, -sin, sin], axis=1)


def _local_step(x, positions, target, emb_g, emb_b, layer_weights, layer_weights_rest, on_sharded_grads,
                on_layer_grads=None, first_after=None, embedded=None):
    cos_t, sin_t = _rope_tables(positions)
    h, hb = _ln_fwd(x, emb_g, emb_b, "emb_ln_fwd") if embedded is None else embedded
    saved = []
    for l in range(DEPTH):
        W = layer_weights(l, h)
        proj = _mm(hb, W["w_in_t"], "nt", F32, "proj_fwd", after=first_after if l == 0 else None)
        qn, kvn, pooled, cv, mix = _mix_fwd(proj, W["q_norm_g"], W["kv_norm_g"], W["w_pool"], W["pool_scale"],
                                            W["conv_w"], "mix_fwd")
        rest, token = layer_weights_rest(l, proj)
        W = {**W, **rest}
        q = _mm(qn, W["w_uq"], "nn", F32, "q_up_fwd", after=token)
        kv = _mm(kvn, W["w_ukv"], "nn", BF16, "kv_up_fwd")
        qc, kc = _rope_fwd(q, kv, proj, cos_t, sin_t, "rope_fwd")
        o, mix, lse = _flash_fwd(qc, kc, kv, proj, mix, "flash_fwd")
        z = _mm(mix, W["w_out"], "nn", F32, "out_fwd", res=h, bias=W["b_out"], alpha=ALPHA)
        saved.append((W, hb, proj, qn, kvn, pooled, cv, kv, qc, kc, o, lse, mix, z))
        h, hb = _ln_fwd(z, W["ln_g"], W["ln_b"], "ln_fwd")
    dh, sq = h, None

    grads = {k: [None] * DEPTH for k in ("q_norm_g", "kv_norm_g", "w_pool", "pool_scale", "conv_w", "b_out", "ln_g",
                                         "ln_b")}
    for l in reversed(range(DEPTH)):
        W, hb_in, proj, qn, kvn, pooled, cv, kv, qc, kc, o, lse, mix, z = saved[l]
        sharded = {}
        if l == DEPTH - 1:
            dz, dzb, grads["b_out"][l], grads["ln_g"][l], grads["ln_b"][l], sq = _ln_bwd(
                dh, z, W["ln_g"], "ln_bwd_loss", target=target)
        else:
            dz, dzb, grads["b_out"][l], grads["ln_g"][l], grads["ln_b"][l] = _ln_bwd(dh, z, W["ln_g"], "ln_bwd")
        dmix = _mm(dzb, W["w_out"], "nt", F32, "out_bwd_x")
        sharded["w_out"] = _mm(mix, dzb, "tn", GRAD_XFER, "out_bwd_w", tk=4096)
        do, delta, dproj, grads["w_pool"][l], grads["pool_scale"][l], grads["conv_w"][l] = _mix_bwd(
            dmix, proj, o, pooled, cv, W["w_pool"], W["pool_scale"], W["conv_w"], "mix_bwd")
        dqc, dkv, dkr = _flash_bwd(qc, kc, kv, do, lse, delta, "flash_bwd")
        dq, dkrope = _rope_bwd(dqc, dkr, cos_t, sin_t, "rope_bwd")
        dqn = _mm(dq, W["w_uq"], "nt", F32, "q_up_bwd_x")
        sharded["w_uq"] = _mm(qn, dq, "tn", GRAD_XFER, "q_up_bwd_w")
        dkvn = _mm(dkv, W["w_ukv"], "nt", F32, "kv_up_bwd_x")
        sharded["w_ukv"] = _mm(kvn, dkv, "tn", GRAD_XFER, "kv_up_bwd_w")
        token = on_sharded_grads(l, sharded)
        dproj, grads["q_norm_g"][l], grads["kv_norm_g"][l] = _rms_bwd(
            proj, dqn, dkvn, dkrope, dproj, W["q_norm_g"], W["kv_norm_g"], "rms_bwd")
        if l == 0 and on_layer_grads is not None:
            token = on_layer_grads(grads, token)
        d_w_in_t = _mm(dproj, hb_in, "tn", GRAD_XFER, "proj_bwd_w", tk=4096, after=token)
        token = on_sharded_grads(l, {"w_in": d_w_in_t})
        dh = _mm(dproj, W["w_in_t"], "nn", F32, "proj_bwd_x", res=dz, alpha=ALPHA, tm=512, tk=NPP, after=token)
    grad_x, grads["emb_ln_g"], grads["emb_ln_b"] = _ln_bwd(dh, x, emb_g, "emb_ln_bwd", for_matmul=False)
    return sq, grad_x, grads


def kernel(x, positions, emb_ln_g, emb_ln_b, w_in, q_norm_g, kv_norm_g, w_uq, w_ukv, w_pool, pool_scale, conv_w, w_out, b_out, ln_g, ln_b, loss_target, m_emb_ln_g, m_emb_ln_b, m_w_in, m_q_norm_g, m_kv_norm_g, m_w_uq, m_w_ukv, m_w_pool, m_pool_scale, m_conv_w, m_w_out, m_b_out, m_ln_g, m_ln_b, v_emb_ln_g, v_emb_ln_b, v_w_in, v_q_norm_g, v_kv_norm_g, v_w_uq, v_w_ukv, v_w_pool, v_pool_scale, v_conv_w, v_w_out, v_b_out, v_ln_g, v_ln_b):
    weights = dict(emb_ln_g=emb_ln_g, emb_ln_b=emb_ln_b, w_in=w_in, q_norm_g=q_norm_g, kv_norm_g=kv_norm_g,
                   w_uq=w_uq, w_ukv=w_ukv, w_pool=w_pool, pool_scale=pool_scale, conv_w=conv_w, w_out=w_out,
                   b_out=b_out, ln_g=ln_g, ln_b=ln_b)
    mom1 = dict(emb_ln_g=m_emb_ln_g, emb_ln_b=m_emb_ln_b, w_in=m_w_in, q_norm_g=m_q_norm_g, kv_norm_g=m_kv_norm_g,
                w_uq=m_w_uq, w_ukv=m_w_ukv, w_pool=m_w_pool, pool_scale=m_pool_scale, conv_w=m_conv_w,
                w_out=m_w_out, b_out=m_b_out, ln_g=m_ln_g, ln_b=m_ln_b)
    mom2 = dict(emb_ln_g=v_emb_ln_g, emb_ln_b=v_emb_ln_b, w_in=v_w_in, q_norm_g=v_q_norm_g, kv_norm_g=v_kv_norm_g,
                w_uq=v_w_uq, w_ukv=v_w_ukv, w_pool=v_w_pool, pool_scale=v_pool_scale, conv_w=v_conv_w,
                w_out=v_w_out, b_out=v_b_out, ln_g=v_ln_g, ln_b=v_ln_b)

    big = ("w_in", "w_uq", "w_ukv", "w_out")

    conv_pad = jnp.zeros((8, 128), F32).at[0:DEPTH * CONV_WIDTH, 0:64].set(conv_w.reshape(DEPTH * CONV_WIDTH, 64))
    t12 = lambda a: jnp.swapaxes(a, 1, 2)
    shard = lambda k, l: (t12(weights[k])[l] if k == "w_in" else weights[k][l]).astype(BF16)
    h0, h0b, (w_in0, conv_all) = _all_gather_under_ln(
        [shard("w_in", 0), conv_pad], x[0], emb_ln_g.reshape(1, -1), emb_ln_b.reshape(1, -1), "w_in0_all_gather_emb_ln")
    rest0 = _split_start([shard(k, 0) for k in big[1:]], False, w_in0, "weights0_rest_start")
    conv_full = _cat_blocks(conv_all[:, 0:DEPTH * CONV_WIDTH, 0:64], 1).reshape(DEPTH, CONV_WIDTH, D_CONV)
    conv_full = jnp.pad(conv_full, ((0, 0), (0, 8 - CONV_WIDTH), (0, 0)))
    fetched = {}

    def layer_weights(l, ready):
        if l == 0:
            w_in_blocks = w_in0
        else:
            fetched[1] = _split_wait(*fetched["w1"][:4], ready, "weights1_wait")[1]
            w_in_blocks = fetched[1][0]
        return dict(
            w_in_t=_permute_w_in_t(w_in_blocks), conv_w=conv_full[l],
            q_norm_g=q_norm_g[l].reshape(1, -1), kv_norm_g=kv_norm_g[l].reshape(1, -1),
            w_pool=w_pool[l].astype(BF16), pool_scale=pool_scale[l].reshape(1, -1), b_out=b_out[l].reshape(1, -1),
            ln_g=ln_g[l].reshape(1, -1), ln_b=ln_b[l].reshape(1, -1))

    def layer_weights_rest(l, ready):
        token = None
        if l == 0:
            blocks = _split_wait(*rest0[:4], ready, "weights0_rest_wait")[1]
            fetched["w1"] = _split_start([shard(k, 1) for k in big], False, blocks[0], "weights1_start")
            token = fetched["w1"][4]
        else:
            blocks = fetched[1][1:]
        return dict(w_uq=_permute_w_uq(_cat_blocks(blocks[0], 1)), w_ukv=_cat_blocks(blocks[1], 1),
                    w_out=blocks[2].reshape(D_MIX, D_MODEL)), token

    by_dest = dict(
        w_in=_split_w_in_t,
        w_uq=lambda g: _unpermute_w_uq(g).reshape(Q_LORA, N_DEV, -1).transpose(1, 0, 2),
        w_ukv=lambda g: g.reshape(KV_LORA, N_DEV, -1).transpose(1, 0, 2),
        w_out=lambda g: g.reshape(N_DEV, -1, D_MODEL))
    in_flight = []

    def on_sharded_grads(l, g):
        names = [k for k in big if k in g]
        srcs = [by_dest[k](g[k]) for k in names]
        started = _split_start(srcs, True, srcs[0], "grads%d_%s_start" % (l, names[0]))
        in_flight.append((l, names, started[:4]))
        return started[4]

    small_in_flight = []

    def on_layer_grads(g, token):
        stacked = {k: jnp.stack(g[k]) for k, _ in _SMALL_LAYER}
        conv = jnp.stack([g["conv_w"][l][0:CONV_WIDTH] for l in range(DEPTH)]).reshape(CONV_ROWS, 128)
        packed = jnp.concatenate([_pack_small(stacked, _SMALL_LAYER), conv], axis=0)
        started = _split_start([packed], False, token, "layer_grads_start")
        small_in_flight.append(started[:4])
        return started[4]

    sq, grad_x, G = _local_step(x[0], positions[0], loss_target[0], emb_ln_g.reshape(1, -1),
                                emb_ln_b.reshape(1, -1), layer_weights, layer_weights_rest, on_sharded_grads,
                                on_layer_grads, first_after=rest0[4], embedded=(h0, h0b))
    loss = lax.psum(sq[0, 0] * (0.5 / D_MODEL), ("x", "y", "c"))

    res = {}
    landed = {}
    for l, names, started in in_flight:
        zones = _split_wait(*started, grad_x, "grads%d_%s_wait" % (l, names[0]))[1]
        for k, zone in zip(names, zones):
            landed[k, l] = zone
    w_in_res = None
    for l in reversed(range(DEPTH)):
        w_in_res = _adamw([landed["w_in", l]], t12(w_in), t12(m_w_in), t12(v_w_in), "adamw_w_in_%d" % l, W_SHARD, 512,
                          first_layer=l, into=w_in_res)
    res["w_in"] = tuple(t12(o) for o in w_in_res)
    for name, rows in (("w_uq", 256), ("w_ukv", 256), ("w_out", 128)):
        res[name] = _adamw([landed[name, l] for l in range(DEPTH)], weights[name], mom1[name], mom2[name],
                           "adamw_" + name, rows)

    layer_zone = _split_wait(*small_in_flight[0], grad_x, "layer_grads_wait")[1][0]
    emb_zone = _all_gather([_pack_small(G, _SMALL_EMB)], "emb_grads_all_gather")[0]
    n_layer_rows = sum(r for _, r in _SMALL_LAYER)
    l_small = jnp.concatenate([emb_zone, layer_zone[:, 0:n_layer_rows]], axis=1)
    my_idx = 4 * lax.axis_index("x") + 2 * lax.axis_index("y") + lax.axis_index("c")
    conv_all_grads = layer_zone[:, n_layer_rows:].reshape(N_DEV, DEPTH * CONV_WIDTH, D_CONV)
    l_conv = lax.dynamic_slice_in_dim(conv_all_grads, my_idx * 64, 64, axis=2)
    l_conv = jnp.zeros((N_DEV, 8, 128), F32).at[:, 0:DEPTH * CONV_WIDTH, 0:64].set(l_conv)
    conv_shard = lambda a: jnp.zeros((8, 128), F32).at[0:DEPTH * CONV_WIDTH, 0:64].set(a.reshape(-1, 64))
    conv_res = _adamw([l_conv], conv_shard(conv_w)[None], conv_shard(m_conv_w)[None], conv_shard(v_conv_w)[None],
                      "adamw_conv_w", 8)
    res["conv_w"] = tuple(o[0, 0:DEPTH * CONV_WIDTH, 0:64].reshape(DEPTH, CONV_WIDTH, 64) for o in conv_res)
    small_res = _adamw([l_small], _pack_small(weights)[None], _pack_small(mom1)[None], _pack_small(mom2)[None],
                       "adamw_small", 392)
    shapes = {k: weights[k].shape for k, _ in _SMALL}
    unpacked = [_unpack_small(o[0], shapes) for o in small_res]
    for k, _ in _SMALL:
        res[k] = tuple(u[k] for u in unpacked)

    order = ("emb_ln_g", "emb_ln_b", "w_in", "q_norm_g", "kv_norm_g", "w_uq", "w_ukv", "w_pool", "pool_scale",
             "conv_w", "w_out", "b_out", "ln_g", "ln_b")
    return (loss, grad_x[None], *[res[k][0] for k in order], *[res[k][1] for k in order],
            *[res[k][2] for k in order], *[res[k][3] for k in order])
```

```python
import jax
import jax.numpy as jnp
from jax import lax
from jax.experimental import pallas as pl
from jax.experimental.pallas import tpu as pltpu

F32 = jnp.float32
BF16 = jnp.bfloat16

N_DEV = 8
D_MODEL = 2048
DEPTH = 2
N_HEADS = 8
NOPE = 128
ROPE = 64
V_DIM = 128
Q_LORA = 512
KV_LORA = 256
D_MLA = N_HEADS * V_DIM
D_POOL = 512
D_CONV = 512
POOL_WINDOWS = (2, 4, 8, 16)
POOL_GROUP = 128
CONV_WIDTH = 3
D_MIX = D_MLA + D_POOL + D_CONV
D_IN_PROJ = 4928
ROPE_THETA = 10000.0
LN_EPS = 1e-5
RMS_EPS = 1e-6
ALPHA = (2 * DEPTH) ** 0.25
ATTN_SCALE = (NOPE + ROPE) ** -0.5
ADAM_LR = 0.001
ADAM_B1 = 0.9
ADAM_B2 = 0.999
ADAM_EPS = 1e-08
ADAM_WD = 0.01
ADAM_STEP = 10

O_GMLA, O_PIN, O_GPOOL, O_CH, O_CB, O_CC, O_GCONV, O_QLAT, O_KVLAT, O_KROPE = (
    0, 1024, 1536, 2048, 2560, 3072, 3584, 4096, 4608, 4864)
NPP = 5120
N_GATED = O_QLAT
QC = NOPE + 2 * ROPE
HALO = 16
ATT_TILE = 512
ATT_CH = 256
LOG2E = 1.4426950408889634
EXP2_SCALE = ATTN_SCALE * LOG2E

GRAD_XFER = BF16
VMEM_LIMIT = 48 * 1024 * 1024
MESH_ID = pl.DeviceIdType.MESH


def _params(sem=None):
    return pltpu.CompilerParams(dimension_semantics=sem, vmem_limit_bytes=VMEM_LIMIT)


def _sigmoid(x):
    return 1.0 / (1.0 + jnp.exp(-x))


def _tile(dim, target):
    if dim <= target:
        return dim
    t = target - target % 128
    while dim % t:
        t -= 128
    return t


_DIMS = {"nn": (((1,), (0,)), ((), ())), "nt": (((1,), (1,)), ((), ())), "tn": (((0,), (0,)), ((), ()))}


def _mm(a, b, mode, out_dtype, name, res=None, bias=None, alpha=1.0, tm=1024, tn=1024, tk=2048, after=None):
    if mode == "nn":
        (M, K), (K2, N) = a.shape, b.shape
    elif mode == "nt":
        (M, K), (N, K2) = a.shape, b.shape
    else:
        (K, M), (K2, N) = a.shape, b.shape
    assert K == K2
    tm, tn, tk = _tile(M, tm), _tile(N, tn), _tile(K, tk)
    nk = K // tk
    has_res, has_bias = res is not None, bias is not None

    def body(*refs):
        a_ref, b_ref = refs[0], refs[1]
        pos = 2
        res_ref = bias_ref = None
        if has_res:
            res_ref = refs[pos]
            pos += 1
        if has_bias:
            bias_ref = refs[pos]
            pos += 1
        def finish(r, o_ref):
            if has_bias:
                r = r + bias_ref[...]
            if has_res:
                r = alpha * res_ref[...] + r
            o_ref[...] = r.astype(out_dtype)

        part = lax.dot_general(a_ref[...].astype(BF16), b_ref[...].astype(BF16), _DIMS[mode],
                               preferred_element_type=F32)
        if nk == 1:
            finish(part, refs[-1])
            return
        o_ref, acc_ref = refs[-2], refs[-1]
        k = pl.program_id(2)

        @pl.when(k == 0)
        def _():
            acc_ref[...] = part

        @pl.when(jnp.logical_and(k > 0, k < nk - 1))
        def _():
            acc_ref[...] += part

        @pl.when(k == nk - 1)
        def _():
            finish(acc_ref[...] + part, o_ref)

    if mode == "nn":
        in_specs = [pl.BlockSpec((tm, tk), lambda i, j, k: (i, k)), pl.BlockSpec((tk, tn), lambda i, j, k: (k, j))]
    elif mode == "nt":
        in_specs = [pl.BlockSpec((tm, tk), lambda i, j, k: (i, k)), pl.BlockSpec((tn, tk), lambda i, j, k: (j, k))]
    else:
        in_specs = [pl.BlockSpec((tk, tm), lambda i, j, k: (k, i)), pl.BlockSpec((tk, tn), lambda i, j, k: (k, j))]
    args = [a, b]
    if has_res:
        in_specs.append(pl.BlockSpec((tm, tn), lambda i, j, k: (i, j)))
        args.append(res)
    if has_bias:
        in_specs.append(pl.BlockSpec((1, tn), lambda i, j, k: (0, j)))
        args.append(bias)
    if after is not None:
        in_specs.append(pl.BlockSpec((8, 128), lambda i, j, k: (0, 0)))
        args.append(after)
    return pl.pallas_call(
        body, name=name,
        out_shape=jax.ShapeDtypeStruct((M, N), out_dtype),
        grid=(M // tm, N // tn, nk),
        in_specs=in_specs,
        out_specs=pl.BlockSpec((tm, tn), lambda i, j, k: (i, j)),
        scratch_shapes=[pltpu.VMEM((tm, tn), F32)] if nk > 1 else [],
        compiler_params=_params(("parallel", "parallel", "arbitrary")),
    )(*args)


def _ln_fwd(z, g, b, name, tq=512):
    T, D = z.shape

    def body(z_ref, g_ref, b_ref, y_ref, yb_ref):
        zv = z_ref[...]
        mu = jnp.mean(zv, axis=1, keepdims=True)
        zc = zv - mu
        var = jnp.mean(zc * zc, axis=1, keepdims=True)
        y = zc * lax.rsqrt(var + LN_EPS) * g_ref[...] + b_ref[...]
        y_ref[...] = y
        yb_ref[...] = y.astype(BF16)

    row = pl.BlockSpec((tq, D), lambda i: (i, 0))
    vec = pl.BlockSpec((1, D), lambda i: (0, 0))
    return pl.pallas_call(
        body, name=name,
        out_shape=(jax.ShapeDtypeStruct((T, D), F32), jax.ShapeDtypeStruct((T, D), BF16)),
        grid=(T // tq,), in_specs=[row, vec, vec], out_specs=(row, row),
        compiler_params=_params(("parallel",)),
    )(z, g, b)


def _ln_bwd(dy, z, g, name, tq=512, target=None, for_matmul=True):
    T, D = z.shape
    with_loss = target is not None

    def body(*refs):
        dy_ref, z_ref, g_ref = refs[:3]
        outs = list(refs[4 if with_loss else 3:])
        dz_ref = outs.pop(0)
        dzb_ref, ds_ref = (outs.pop(0), outs.pop(0)) if for_matmul else (None, None)
        dg_ref, db_ref = outs.pop(0), outs.pop(0)
        sq_ref = outs.pop(0) if with_loss else None

        @pl.when(pl.program_id(0) == 0)
        def _():
            for ref in (dg_ref, db_ref, ds_ref, sq_ref):
                if ref is not None:
                    ref[...] = jnp.zeros_like(ref)

        zv, dyv = z_ref[...], dy_ref[...]
        if with_loss:
            err = dyv - refs[3][...]
            sq_ref[...] += jnp.sum(err * err)
            dyv = err * (1.0 / D)
        mu = jnp.mean(zv, axis=1, keepdims=True)
        zc = zv - mu
        var = jnp.mean(zc * zc, axis=1, keepdims=True)
        rstd = lax.rsqrt(var + LN_EPS)
        xh = zc * rstd
        u = dyv * g_ref[...]
        dz = rstd * (u - jnp.mean(u, axis=1, keepdims=True) - xh * jnp.mean(u * xh, axis=1, keepdims=True))
        dz_ref[...] = dz
        dg_ref[...] += jnp.sum(dyv * xh, axis=0, keepdims=True)
        db_ref[...] += jnp.sum(dyv, axis=0, keepdims=True)
        if for_matmul:
            dzb_ref[...] = dz.astype(BF16)
            ds_ref[...] += jnp.sum(dz, axis=0, keepdims=True)

    row = pl.BlockSpec((tq, D), lambda i: (i, 0))
    vec = pl.BlockSpec((1, D), lambda i: (0, 0))
    vshape = jax.ShapeDtypeStruct((1, D), F32)
    out_shape, out_specs = [jax.ShapeDtypeStruct((T, D), F32)], [row]
    if for_matmul:
        out_shape += [jax.ShapeDtypeStruct((T, D), BF16), vshape]
        out_specs += [row, vec]
    out_shape += [vshape, vshape]
    out_specs += [vec, vec]
    if with_loss:
        out_shape.append(jax.ShapeDtypeStruct((8, 128), F32))
        out_specs.append(pl.BlockSpec((8, 128), lambda i: (0, 0)))
    return pl.pallas_call(
        body, name=name, out_shape=tuple(out_shape), grid=(T // tq,),
        in_specs=[row, row, vec] + ([row] if with_loss else []), out_specs=tuple(out_specs),
        compiler_params=_params(("arbitrary",)),
    )(dy, z, g, *([target] if with_loss else []))


def _pblock(tq, width, offset):
    assert offset % width == 0
    blk = offset // width
    return pl.BlockSpec((tq, width), lambda i: (i, blk))


def _mix_fwd(proj, q_g, kv_g, w_pool, pool_scale, conv_w, name, tq=256):
    T = proj.shape[0]

    def body(ql_ref, kvl_ref, pin_ref, gp_ref, ch_ref, cb_ref, cc_ref, gc_ref, qg_ref, kvg_ref, wp_ref, ps_ref,
             cw_ref, qn_ref, kvn_ref, pooled_ref, cv_ref, ypc_ref, extp, extu):
        i = pl.program_id(0)
        for x_ref, g_ref, o_ref in ((ql_ref, qg_ref, qn_ref), (kvl_ref, kvg_ref, kvn_ref)):
            x = x_ref[...]
            r = lax.rsqrt(jnp.mean(x * x, axis=1, keepdims=True) + RMS_EPS)
            o_ref[...] = (x * r * g_ref[...]).astype(BF16)

        @pl.when(i == 0)
        def _():
            extp[0:HALO, :] = jnp.zeros((HALO, D_POOL), F32)
            extu[0:HALO, :] = jnp.zeros((HALO, D_CONV), F32)

        @pl.when(i > 0)
        def _():
            extp[0:HALO, :] = extp[tq:tq + HALO, :]
            extu[0:HALO, :] = extu[tq:tq + HALO, :]

        t1 = (i * tq + lax.broadcasted_iota(jnp.int32, (tq, 1), 0) + 1).astype(F32)
        for g, w in enumerate(POOL_WINDOWS):
            cols = slice(g * POOL_GROUP, (g + 1) * POOL_GROUP)
            pin = pin_ref[:, cols]
            extp[HALO:, cols] = pin
            s = extp[:, cols]
            k = 1
            while k < w:
                s = s + pltpu.roll(s, k, 0)
                k *= 2
            mean = s[HALO:, :] / jnp.minimum(t1, float(w))
            pooled = (mean - pin).astype(BF16)
            pooled_ref[:, cols] = pooled
            r = jnp.dot(pooled, wp_ref[g], preferred_element_type=F32)
            gp = gp_ref[:, cols]
            ypc_ref[:, cols] = (r * ps_ref[:, cols] * (gp * _sigmoid(gp))).astype(BF16)
        for g in range(D_CONV // 128):
            cols = slice(g * 128, (g + 1) * 128)
            u = cc_ref[:, cols] * ch_ref[:, cols]
            extu[HALO:, cols] = u
            eu = extu[:, cols]
            u1 = pltpu.roll(eu, 1, 0)[HALO:, :]
            u2 = pltpu.roll(eu, 2, 0)[HALO:, :]
            cv = cw_ref[0:1, cols] * u2 + cw_ref[1:2, cols] * u1 + cw_ref[2:3, cols] * u
            cv_ref[:, cols] = cv
            gc = gc_ref[:, cols]
            ypc_ref[:, D_POOL + g * 128:D_POOL + (g + 1) * 128] = (
                cb_ref[:, cols] * cv * (gc * _sigmoid(gc))).astype(BF16)

    full = lambda shape: pl.BlockSpec(shape, lambda i: (0,) * len(shape))
    row = lambda w: pl.BlockSpec((tq, w), lambda i: (i, 0))
    return pl.pallas_call(
        body, name=name,
        out_shape=(jax.ShapeDtypeStruct((T, Q_LORA), BF16), jax.ShapeDtypeStruct((T, KV_LORA), BF16),
                   jax.ShapeDtypeStruct((T, D_POOL), BF16), jax.ShapeDtypeStruct((T, D_CONV), F32),
                   jax.ShapeDtypeStruct((T, D_MIX), BF16)),
        grid=(T // tq,),
        in_specs=[_pblock(tq, Q_LORA, O_QLAT), _pblock(tq, KV_LORA, O_KVLAT), _pblock(tq, 512, O_PIN),
                  _pblock(tq, 512, O_GPOOL), _pblock(tq, 512, O_CH), _pblock(tq, 512, O_CB), _pblock(tq, 512, O_CC),
                  _pblock(tq, 512, O_GCONV), full((1, Q_LORA)), full((1, KV_LORA)), full((4, 128, 128)),
                  full((1, D_POOL)), full((8, D_CONV))],
        out_specs=(row(Q_LORA), row(KV_LORA), row(D_POOL), row(D_CONV),
                   pl.BlockSpec((tq, D_POOL + D_CONV), lambda i: (i, D_MLA // (D_POOL + D_CONV)))),
        scratch_shapes=[pltpu.VMEM((tq + HALO, D_POOL), F32), pltpu.VMEM((tq + HALO, D_CONV), F32)],
        compiler_params=_params(("arbitrary",)),
    )(proj, proj, proj, proj, proj, proj, proj, proj, q_g, kv_g, w_pool, pool_scale, conv_w)


def _mix_bwd(dmix, proj, o, pooled, cv, w_pool, pool_scale, conv_w, name, tq=ATT_CH):
    T = proj.shape[0]
    nt = T // tq
    n_ext = tq + HALO

    def body(dym_ref, dyp_ref, dyc_ref, gm_ref, gp_ref, ch_ref, cb_ref, cc_ref, gc_ref, o_ref, pooled_ref, cv_ref,
             wp_ref, ps_ref, cw_ref, do_ref, delta_ref, dg_ref, dwp_ref, dps_ref, dcw_ref, exte, extd):
        i = pl.program_id(0)
        tile = nt - 1 - i

        @pl.when(i == 0)
        def _():
            dwp_ref[...] = jnp.zeros_like(dwp_ref)
            dps_ref[...] = jnp.zeros_like(dps_ref)
            dcw_ref[...] = jnp.zeros_like(dcw_ref)
            exte[tq:, :] = jnp.zeros((HALO, D_POOL), F32)
            extd[tq:, :] = jnp.zeros((HALO, D_CONV), F32)

        @pl.when(i > 0)
        def _():
            exte[tq:, :] = exte[0:HALO, :]
            extd[tq:, :] = extd[0:HALO, :]

        ones = jnp.ones((8, V_DIM), F32)
        for h in range(N_HEADS):
            cols = slice(h * V_DIM, (h + 1) * V_DIM)
            gm = gm_ref[:, cols]
            sig = _sigmoid(gm)
            dym = dym_ref[:, cols]
            ov = o_ref[:, cols]
            do = dym * (gm * sig)
            do_ref[:, cols] = do.astype(BF16)
            rows = lax.dot_general(ones, do * ov, _DIMS["nt"], precision=lax.Precision.HIGHEST,
                                   preferred_element_type=F32)
            delta_ref[h, 0] = rows[0:1, :]
            dg_ref[:, O_GMLA + h * V_DIM:O_GMLA + (h + 1) * V_DIM] = (
                dym * ov * (sig * (1.0 + gm * (1.0 - sig)))).astype(BF16)

        t1 = (tile * tq + lax.broadcasted_iota(jnp.int32, (tq, 1), 0) + 1).astype(F32)
        for g, w in enumerate(POOL_WINDOWS):
            cols = slice(g * POOL_GROUP, (g + 1) * POOL_GROUP)
            pg = pooled_ref[:, cols]
            r = jnp.dot(pg, wp_ref[g], preferred_element_type=F32)
            gp = gp_ref[:, cols]
            sg = _sigmoid(gp)
            sl = gp * sg
            dyg = dyp_ref[:, cols]
            ps = ps_ref[:, cols]
            dg_ref[:, O_GPOOL + g * POOL_GROUP:O_GPOOL + (g + 1) * POOL_GROUP] = (
                dyg * (r * ps) * (sg * (1.0 + gp * (1.0 - sg)))).astype(BF16)
            dps_ref[:, cols] += jnp.sum(dyg * r * sl, axis=0, keepdims=True)
            dr = (dyg * ps * sl).astype(BF16)
            dwp_ref[g] += lax.dot_general(pg, dr, _DIMS["tn"], preferred_element_type=F32)
            dpooled = lax.dot_general(dr, wp_ref[g], _DIMS["nt"], preferred_element_type=F32)
            exte[0:tq, cols] = dpooled / jnp.minimum(t1, float(w))
            s = exte[:, cols]
            k = 1
            while k < w:
                s = s + pltpu.roll(s, n_ext - k, 0)
                k *= 2
            dg_ref[:, O_PIN + g * POOL_GROUP:O_PIN + (g + 1) * POOL_GROUP] = (s[0:tq, :] - dpooled).astype(BF16)

        for g in range(D_CONV // 128):
            cols = slice(g * 128, (g + 1) * 128)
            out = lambda base: slice(base + g * 128, base + (g + 1) * 128)
            gc = gc_ref[:, cols]
            sg = _sigmoid(gc)
            sl = gc * sg
            dyc = dyc_ref[:, cols]
            cb, cc, ch, cvv = cb_ref[:, cols], cc_ref[:, cols], ch_ref[:, cols], cv_ref[:, cols]
            dcv = dyc * cb * sl
            dg_ref[:, out(O_GCONV)] = (dyc * (cb * cvv) * (sg * (1.0 + gc * (1.0 - sg)))).astype(BF16)
            dg_ref[:, out(O_CB)] = (dyc * cvv * sl).astype(BF16)
            extd[0:tq, cols] = dcv
            ed = extd[:, cols]
            d1 = pltpu.roll(ed, n_ext - 1, 0)[0:tq, :]
            d2 = pltpu.roll(ed, n_ext - 2, 0)[0:tq, :]
            du = cw_ref[2:3, cols] * dcv + cw_ref[1:2, cols] * d1 + cw_ref[0:1, cols] * d2
            u = cc * ch
            dcw_ref[0:1, cols] += jnp.sum(u * d2, axis=0, keepdims=True)
            dcw_ref[1:2, cols] += jnp.sum(u * d1, axis=0, keepdims=True)
            dcw_ref[2:3, cols] += jnp.sum(u * dcv, axis=0, keepdims=True)
            dg_ref[:, out(O_CH)] = (du * cc).astype(BF16)
            dg_ref[:, out(O_CC)] = (du * ch).astype(BF16)

    def rblock(width, offset):
        assert offset % width == 0
        blk = offset // width
        return pl.BlockSpec((tq, width), lambda i: (nt - 1 - i, blk))

    full = lambda shape: pl.BlockSpec(shape, lambda i: (0,) * len(shape))
    return pl.pallas_call(
        body, name=name,
        out_shape=(jax.ShapeDtypeStruct((T, D_MLA), BF16), jax.ShapeDtypeStruct((N_HEADS, nt, 1, tq), F32),
                   jax.ShapeDtypeStruct((T, NPP), BF16),
                   jax.ShapeDtypeStruct((4, 128, 128), F32), jax.ShapeDtypeStruct((1, D_POOL), F32),
                   jax.ShapeDtypeStruct((8, D_CONV), F32)),
        grid=(nt,),
        in_specs=[rblock(1024, 0), rblock(512, 1024), rblock(512, 1536),
                  rblock(1024, O_GMLA), rblock(512, O_GPOOL), rblock(512, O_CH), rblock(512, O_CB),
                  rblock(512, O_CC), rblock(512, O_GCONV), rblock(1024, 0), rblock(512, 0), rblock(512, 0),
                  full((4, 128, 128)), full((1, D_POOL)), full((8, D_CONV))],
        out_specs=(rblock(1024, 0), pl.BlockSpec((N_HEADS, 1, 1, tq), lambda i: (0, nt - 1 - i, 0, 0)),
                   rblock(N_GATED, 0), full((4, 128, 128)), full((1, D_POOL)), full((8, D_CONV))),
        scratch_shapes=[pltpu.VMEM((n_ext, D_POOL), F32), pltpu.VMEM((n_ext, D_CONV), F32)],
        compiler_params=_params(("arbitrary",)),
    )(dmix, dmix, dmix, proj, proj, proj, proj, proj, proj, o, pooled, cv, w_pool, pool_scale, conv_w)


def _up_rms_bwd(proj, dq, dkv, w_uq, w_ukv, dkrope, dproj, q_g, kv_g, name, tq=256):
    T = proj.shape[0]
    n_lat = NPP - N_GATED

    def body(ql_ref, kvl_ref, dq_ref, dkv_ref, wq_ref, wkv_ref, dkr_ref, _, qg_ref, kvg_ref, dlat_ref, dqg_ref,
             dkvg_ref):
        @pl.when(pl.program_id(0) == 0)
        def _():
            dqg_ref[...] = jnp.zeros_like(dqg_ref)
            dkvg_ref[...] = jnp.zeros_like(dkvg_ref)

        dqn = lax.dot_general(dq_ref[...], wq_ref[...], _DIMS["nt"], preferred_element_type=F32)
        dkvn = lax.dot_general(dkv_ref[...], wkv_ref[...], _DIMS["nt"], preferred_element_type=F32)
        for x_ref, dy, g_ref, c0, dg_ref in ((ql_ref, dqn, qg_ref, 0, dqg_ref),
                                             (kvl_ref, dkvn, kvg_ref, Q_LORA, dkvg_ref)):
            x = x_ref[...]
            r = lax.rsqrt(jnp.mean(x * x, axis=1, keepdims=True) + RMS_EPS)
            xr = x * r
            u = dy * g_ref[...]
            dlat_ref[:, c0:c0 + x.shape[1]] = (r * (u - xr * jnp.mean(u * xr, axis=1, keepdims=True))).astype(BF16)
            dg_ref[...] += jnp.sum(dy * xr, axis=0, keepdims=True)
        dlat_ref[:, Q_LORA + KV_LORA:] = dkr_ref[...]

    row = lambda w: pl.BlockSpec((tq, w), lambda i: (i, 0))
    vec = lambda w: pl.BlockSpec((1, w), lambda i: (0, 0))
    assert N_GATED % n_lat == 0
    return pl.pallas_call(
        body, name=name,
        out_shape=(jax.ShapeDtypeStruct((T, NPP), BF16),
                   jax.ShapeDtypeStruct((1, Q_LORA), F32), jax.ShapeDtypeStruct((1, KV_LORA), F32)),
        grid=(T // tq,),
        in_specs=[_pblock(tq, Q_LORA, O_QLAT), _pblock(tq, KV_LORA, O_KVLAT), row(dq.shape[1]), row(dkv.shape[1]),
                  pl.BlockSpec(w_uq.shape, lambda i: (0, 0)), pl.BlockSpec(w_ukv.shape, lambda i: (0, 0)),
                  row(n_lat - Q_LORA - KV_LORA), pl.BlockSpec(memory_space=pl.ANY), vec(Q_LORA), vec(KV_LORA)],
        out_specs=(pl.BlockSpec((tq, n_lat), lambda i: (i, N_GATED // n_lat)), vec(Q_LORA), vec(KV_LORA)),
        input_output_aliases={7: 0},
        compiler_params=_params(("arbitrary",)),
    )(proj, proj, dq, dkv, w_uq, w_ukv, dkrope, dproj, q_g, kv_g)


def _swap_halves(x, lo):
    return jnp.where(lo, pltpu.roll(x, 96, 1), pltpu.roll(x, 32, 1))


def _up_rope_fwd(qn, kvn, w_uq, w_ukv, proj, cos_t, sin_t, name, tq=256, after=None):
    T = qn.shape[0]

    def body(qn_ref, kvn_ref, wq_ref, wkv_ref, kr_ref, c_ref, s_ref, *rest):
        kv_ref, qc_ref, kc_ref = rest[-3:]
        q = jnp.dot(qn_ref[...], wq_ref[...], preferred_element_type=F32)
        kv_ref[...] = jnp.dot(kvn_ref[...], wkv_ref[...], preferred_element_type=F32).astype(BF16)
        C, S = c_ref[...], s_ref[...]
        lane = lax.broadcasted_iota(jnp.int32, (tq, 128), 1)
        lo = (lane % ROPE) < (ROPE // 2)
        first = lane < ROPE

        def rope(x):
            return x * C + _swap_halves(x, lo) * S

        kr = jnp.where(first, rope(kr_ref[...]), 0.0).astype(BF16)
        n_nope = N_HEADS * NOPE
        for j in range(N_HEADS // 2):
            r = rope(q[:, n_nope + j * 128:n_nope + (j + 1) * 128])
            pair = (jnp.where(first, r, 0.0), jnp.where(first, pltpu.roll(r, 64, 1), 0.0))
            for hh in range(2):
                h = 2 * j + hh
                qc_ref[h, :, 0:NOPE] = q[:, h * NOPE:(h + 1) * NOPE].astype(BF16)
                qc_ref[h, :, NOPE:QC] = pair[hh].astype(BF16)
        for h in range(N_HEADS):
            kc_ref[h, :, 0:NOPE] = kv_ref[:, h * 256:h * 256 + NOPE]
            kc_ref[h, :, NOPE:QC] = kr

    out = jax.ShapeDtypeStruct((N_HEADS, T, QC), BF16)
    hblock = pl.BlockSpec((N_HEADS, tq, QC), lambda i: (0, i, 0))
    row = lambda w: pl.BlockSpec((tq, w), lambda i: (i, 0))
    full = lambda a: pl.BlockSpec(a.shape, lambda i: (0, 0))
    return pl.pallas_call(
        body, name=name, out_shape=(jax.ShapeDtypeStruct((T, 2 * D_MLA), BF16), out, out), grid=(T // tq,),
        in_specs=[row(Q_LORA), row(KV_LORA), full(w_uq), full(w_ukv), _pblock(tq, 128, O_KROPE), row(128), row(128)]
        + ([pl.BlockSpec((8, 128), lambda i: (0, 0))] if after is not None else []),
        out_specs=(row(2 * D_MLA), hblock, hblock),
        compiler_params=_params(("parallel",)),
    )(qn, kvn, w_uq, w_ukv, proj, cos_t, sin_t, *([after] if after is not None else []))


def _rope_bwd(dqc, dkr, cos_t, sin_t, name, tq=256):
    T = dqc.shape[1]

    def body(dqc_ref, dkr_ref, c_ref, s_ref, dq_ref, dk_ref):
        C, S = c_ref[...], s_ref[...]
        lane = lax.broadcasted_iota(jnp.int32, (tq, 128), 1)
        lo = (lane % ROPE) < (ROPE // 2)
        first = lane < ROPE

        def unrope(dy):
            return dy * C - _swap_halves(dy, lo) * S

        acc = dkr_ref[0]
        for h in range(1, N_HEADS):
            acc = acc + dkr_ref[h]
        dk_ref[:, 0:128] = jnp.where(first, unrope(acc), 0.0).astype(BF16)
        dk_ref[:, 128:256] = jnp.zeros((tq, 128), BF16)
        for j in range(N_HEADS // 2):
            d0 = dqc_ref[2 * j, :, NOPE:QC]
            d1 = dqc_ref[2 * j + 1, :, NOPE:QC]
            comb = jnp.where(first, d0, pltpu.roll(d1, 64, 1))
            dq_ref[:, 1024 + j * 128:1024 + (j + 1) * 128] = unrope(comb).astype(BF16)
        for h in range(N_HEADS):
            dq_ref[:, h * NOPE:(h + 1) * NOPE] = dqc_ref[h, :, 0:NOPE].astype(BF16)

    tab = pl.BlockSpec((tq, 128), lambda i: (i, 0))
    return pl.pallas_call(
        body, name=name,
        out_shape=(jax.ShapeDtypeStruct((T, 1536), BF16), jax.ShapeDtypeStruct((T, 256), BF16)),
        grid=(T // tq,),
        in_specs=[pl.BlockSpec((N_HEADS, tq, QC), lambda i: (0, i, 0)),
                  pl.BlockSpec((N_HEADS, tq, 128), lambda i: (0, i, 0)), tab, tab],
        out_specs=(pl.BlockSpec((tq, 1536), lambda i: (i, 0)), pl.BlockSpec((tq, 256), lambda i: (i, 0))),
        compiler_params=_params(("parallel",)),
    )(dqc, dkr, cos_t, sin_t)


def _flash_fwd(qc, kc, kv, proj, mix, name):
    H, T, _ = qc.shape
    tt = ATT_TILE
    nt = T // tt
    sp = tt // ATT_CH

    def body(q_ref, k_ref, v_ref, g_ref, _, o_ref, y_ref, lse_ref, vt_sc, s_sc, acc_sc, m_sc, l_sc, bias_sc):
        i = pl.program_id(1)

        @pl.when(i == 0)
        def _():
            for c in range(nt):
                vt_sc[c] = v_ref[c * tt:(c + 1) * tt, :].astype(F32).T.astype(BF16)
            krow = lax.broadcasted_iota(jnp.int32, (tt, tt), 0)
            qcol = lax.broadcasted_iota(jnp.int32, (tt, tt), 1)
            bias_sc[...] = jnp.where(krow <= qcol, 0.0, -jnp.inf)

        q = q_ref[0]

        def issue(c, slot):
            s_sc[slot] = lax.dot_general(k_ref[0, pl.ds(pl.multiple_of(c * tt, tt), tt), :], q, _DIMS["nt"],
                                         preferred_element_type=F32)

        def softmax_pv(c, slot, masked):
            s = s_sc[slot]
            if masked:
                s = s + bias_sc[...]
            m = m_sc[...]
            m_new = jnp.maximum(m, jnp.max(s, axis=0, keepdims=True))
            p = jnp.exp2((s - m_new) * EXP2_SCALE)
            a = jnp.exp2((m - m_new) * EXP2_SCALE)
            l_sc[...] = a * l_sc[...] + jnp.sum(p, axis=0, keepdims=True)
            acc_sc[...] = a * acc_sc[...] + jnp.dot(vt_sc[c], p.astype(BF16), preferred_element_type=F32)
            m_sc[...] = m_new

        issue(0, 0)
        m_sc[...] = jnp.full_like(m_sc, -jnp.inf)
        l_sc[...] = jnp.zeros_like(l_sc)
        acc_sc[...] = jnp.zeros_like(acc_sc)

        def pair(t, carry):
            issue(2 * t + 1, 1)
            softmax_pv(2 * t, 0, False)
            issue(2 * t + 2, 0)
            softmax_pv(2 * t + 1, 1, False)
            return carry

        lax.fori_loop(0, i // 2, pair, 0)

        @pl.when(i % 2 == 1)
        def _():
            issue(i, 1)
            softmax_pv(i - 1, 0, False)
            softmax_pv(i, 1, True)

        @pl.when(i % 2 == 0)
        def _():
            softmax_pv(i, 0, True)

        l = l_sc[...]
        o = (acc_sc[...] / l).T
        o_ref[...] = o
        lse = m_sc[...] * ATTN_SCALE + jnp.log(l)
        for r in range(sp):
            lse_ref[0, r] = lse[:, r * ATT_CH:(r + 1) * ATT_CH]
        g = g_ref[...]
        y_ref[...] = (o * (g * _sigmoid(g))).astype(BF16)

    return pl.pallas_call(
        body, name=name,
        out_shape=(jax.ShapeDtypeStruct((T, D_MLA), F32), jax.ShapeDtypeStruct((T, D_MIX), BF16),
                   jax.ShapeDtypeStruct((H, T // ATT_CH, 1, ATT_CH), F32)),
        grid=(H, nt),
        in_specs=[pl.BlockSpec((1, tt, QC), lambda h, i: (h, i, 0)),
                  pl.BlockSpec((1, T, QC), lambda h, i: (h, 0, 0)),
                  pl.BlockSpec((T, V_DIM), lambda h, i: (0, 2 * h + 1)),
                  pl.BlockSpec((tt, V_DIM), lambda h, i: (i, h)),
                  pl.BlockSpec(memory_space=pl.ANY)],
        input_output_aliases={4: 1},
        out_specs=(pl.BlockSpec((tt, V_DIM), lambda h, i: (i, h)),
                   pl.BlockSpec((tt, V_DIM), lambda h, i: (i, h)),
                   pl.BlockSpec((1, sp, 1, ATT_CH), lambda h, i: (h, i, 0, 0))),
        scratch_shapes=[pltpu.VMEM((nt, V_DIM, tt), BF16), pltpu.VMEM((2, tt, tt), F32),
                        pltpu.VMEM((V_DIM, tt), F32), pltpu.VMEM((1, tt), F32), pltpu.VMEM((1, tt), F32),
                        pltpu.VMEM((tt, tt), F32)],
        compiler_params=_params(("parallel", "arbitrary")),
    )(qc, kc, kv, proj, mix)


def _flash_bwd(qc, kc, kv, do, lse, delta, name):
    H, T, _ = qc.shape
    tt = ATT_TILE
    nt = T // tt
    sp = tt // ATT_CH

    def body(q_ref, k_ref, v_ref, do_ref, lse_ref, dl_ref, dq_ref, dkv_ref, dkr_ref, dqt_sc, dk_sc, dv_sc, s_sc,
             dp_sc, kt_sc):
        j = pl.program_id(1)

        @pl.when(j == 0)
        def _():
            dqt_sc[...] = jnp.zeros_like(dqt_sc)

        k = k_ref[0]
        v = v_ref[...]

        def operands(c):
            q0 = pl.multiple_of(c * tt, tt)
            return q_ref[0, pl.ds(q0, tt), :], do_ref[pl.ds(q0, tt), :]

        def stat_row(ref, c):
            return jnp.concatenate([ref[0, sp * c + r] for r in range(sp)], axis=1)

        def early(c, slot):
            q, dov = operands(c)
            s_sc[slot] = lax.dot_general(k, q, _DIMS["nt"], preferred_element_type=F32)
            dp_sc[slot] = lax.dot_general(v, dov, _DIMS["nt"], preferred_element_type=F32)

        def late(c, slot, masked, kt):
            q, dov = operands(c)
            s, dp = s_sc[slot], dp_sc[slot]
            if masked:
                krow = j * tt + lax.broadcasted_iota(jnp.int32, s.shape, 0)
                qcol = c * tt + lax.broadcasted_iota(jnp.int32, s.shape, 1)
                s = jnp.where(krow <= qcol, s, -jnp.inf)
            p = jnp.exp2(s * EXP2_SCALE - stat_row(lse_ref, c) * LOG2E)
            ds = (p * (dp - stat_row(dl_ref, c)) * ATTN_SCALE).astype(BF16)
            dv = jnp.dot(p.astype(BF16), dov, preferred_element_type=F32)
            dk = jnp.dot(ds, q, preferred_element_type=F32)
            if masked:
                dv_sc[slot] = dv
                dk_sc[slot] = dk
            else:
                dv_sc[slot] += dv
                dk_sc[slot] += dk
            dqt_sc[c] += jnp.dot(kt, ds, preferred_element_type=F32)

        n_rest = nt - 1 - j
        odd = n_rest % 2

        def setup(other_slot):
            dk_sc[other_slot] = jnp.zeros((tt, QC), F32)
            dv_sc[other_slot] = jnp.zeros((tt, V_DIM), F32)
            kt_sc[...] = k.astype(F32).T.astype(BF16)

        @pl.when(odd == 0)
        def _():
            early(j, 0)
            early(jnp.minimum(j + 1, nt - 1), 1)
            setup(1)
            late(j, 0, True, kt_sc[...])

        @pl.when(odd == 1)
        def _():
            early(j, 1)
            early(j + 1, 0)
            setup(0)
            kt = kt_sc[...]
            late(j, 1, True, kt)
            early(jnp.minimum(j + 2, nt - 1), 1)
            late(j + 1, 0, False, kt)

        def pair(u, carry):
            a = j + 1 + odd + 2 * u
            kt = kt_sc[...]
            early(a + 1, 0)
            late(a, 1, False, kt)
            early(jnp.minimum(a + 2, nt - 1), 1)
            late(a + 1, 0, False, kt)
            return carry

        lax.fori_loop(0, n_rest // 2, pair, 0)
        dk = dk_sc[0] + dk_sc[1]
        dkv_ref[:, 0:NOPE] = dk[:, 0:NOPE].astype(BF16)
        dkv_ref[:, NOPE:] = (dv_sc[0] + dv_sc[1]).astype(BF16)
        dkr_ref[0] = dk[:, NOPE:]

        @pl.when(j == nt - 1)
        def _():
            for c in range(nt):
                dq_ref[0, c * tt:(c + 1) * tt, :] = dqt_sc[c].T

    head = lambda h, j: (h, 0, 0)
    stat = pl.BlockSpec((1, T // ATT_CH, 1, ATT_CH), lambda h, j: (h, 0, 0, 0))
    return pl.pallas_call(
        body, name=name,
        out_shape=(jax.ShapeDtypeStruct((H, T, QC), F32), jax.ShapeDtypeStruct((T, 2 * D_MLA), BF16),
                   jax.ShapeDtypeStruct((H, T, 128), F32)),
        grid=(H, nt),
        in_specs=[pl.BlockSpec((1, T, QC), head),
                  pl.BlockSpec((1, tt, QC), lambda h, j: (h, j, 0)),
                  pl.BlockSpec((tt, V_DIM), lambda h, j: (j, 2 * h + 1)),
                  pl.BlockSpec((T, V_DIM), lambda h, j: (0, h)),
                  stat, stat],
        out_specs=(pl.BlockSpec((1, T, QC), head),
                   pl.BlockSpec((tt, 256), lambda h, j: (j, h)),
                   pl.BlockSpec((1, tt, 128), lambda h, j: (h, j, 0))),
        scratch_shapes=[pltpu.VMEM((nt, QC, tt), F32), pltpu.VMEM((2, tt, QC), F32), pltpu.VMEM((2, tt, V_DIM), F32),
                        pltpu.VMEM((2, tt, tt), F32), pltpu.VMEM((2, tt, tt), F32), pltpu.VMEM((QC, tt), BF16)],
        compiler_params=_params(("parallel", "arbitrary")),
    )(qc, kc, kv, do, lse, delta)


def _adamw(lands, w, m, v, name, rows, cols=None, first_layer=0, into=None):
    layers, R, C = w.shape
    L = len(lands)
    cols = C if cols is None else cols
    assert R % rows == 0 and C % cols == 0 and first_layer + L <= layers
    nc = C // cols
    nb = (R // rows) * nc
    c1 = 1.0 - ADAM_B1 ** ADAM_STEP
    c2 = 1.0 - ADAM_B2 ** ADAM_STEP

    def body(*refs):
        land_refs = refs[:L]
        w_ref, m_ref, v_ref = refs[L:L + 3]
        g_ref, d_ref, nm_ref, nv_ref, g_sc = refs[-5:]
        for ll in range(L):
            @pl.when(pl.program_id(0) == ll)
            def _(land_ref=land_refs[ll]):
                g = land_ref[0].astype(F32)
                for s in range(1, N_DEV):
                    g = g + land_ref[s].astype(F32)
                g_sc[...] = g

        g = g_sc[...]
        nm = ADAM_B1 * m_ref[0] + (1.0 - ADAM_B1) * g
        nv = ADAM_B2 * v_ref[0] + (1.0 - ADAM_B2) * (g * g)
        g_ref[0] = g
        nm_ref[0] = nm
        nv_ref[0] = nv
        d_ref[0] = -ADAM_LR * ((nm / c1) / (jnp.sqrt(nv / c2) + ADAM_EPS) + ADAM_WD * w_ref[0])

    def land_spec(ll):
        def index(l, i):
            i = jnp.where(l < ll, 0, jnp.where(l > ll, nb - 1, i))
            return (0, i // nc, i % nc)
        return pl.BlockSpec((N_DEV, rows, cols), index)

    blk = pl.BlockSpec((1, rows, cols), lambda l, i: (first_layer + l, i // nc, i % nc))
    out = jax.ShapeDtypeStruct((layers, R, C), F32)
    extra = [] if into is None else list(into)
    return pl.pallas_call(
        body, name=name, out_shape=(out, out, out, out), grid=(L, nb),
        in_specs=[land_spec(ll) for ll in range(L)] + [blk, blk, blk] + [pl.BlockSpec(memory_space=pl.ANY)] * len(extra),
        out_specs=(blk, blk, blk, blk),
        input_output_aliases={L + 3 + i: i for i in range(len(extra))},
        scratch_shapes=[pltpu.VMEM((rows, cols), F32)],
        compiler_params=_params(("arbitrary", "arbitrary")),
    )(*lands, w, m, v, *extra)


def _mesh_pos():
    return lax.axis_index("x"), lax.axis_index("y"), lax.axis_index("c")


def _all_gather(arrays, name):
    n = len(arrays)

    def body(*refs):
        ins, outs = refs[:n], refs[n:2 * n]
        send_sems, recv_sems, local_sems = refs[2 * n:]
        x, y, c = _mesh_pos()
        me, sibling = (x, y, c), (x, y, 1 - c)
        chips = [(1 - x, y), (x, 1 - y), (1 - x, 1 - y)]

        def slot(a, pos):
            px, py, pc = pos
            return outs[a].at[4 * px + 2 * py + pc]

        def copy(a, k, block, to, src=None):
            return pltpu.make_async_remote_copy(
                src_ref=slot(a, block) if src is None else src, dst_ref=slot(a, block),
                send_sem=send_sems.at[a * 7 + k], recv_sem=recv_sems.at[a * 7 + k],
                device_id=to, device_id_type=MESH_ID)

        mine, first, passed = [], [], []
        for a in range(n):
            cp = pltpu.make_async_copy(ins[a], slot(a, me), local_sems.at[a])
            cp.start()
            mine.append(cp)
            cps = [copy(a, 0, me, sibling, src=ins[a])]
            cps += [copy(a, 1 + j, me, (*chip, c), src=ins[a]) for j, chip in enumerate(chips)]
            for cp in cps:
                cp.start()
            first += cps
        for j, chip in enumerate(chips):
            for a in range(n):
                copy(a, 1 + j, (*chip, c), me).wait_recv()
                cp = copy(a, 4 + j, (*chip, c), sibling)
                cp.start()
                passed.append(cp)
        for a in range(n):
            copy(a, 0, sibling, me).wait_recv()
            for j, chip in enumerate(chips):
                copy(a, 4 + j, (*chip, 1 - c), me).wait_recv()
        for cp in first + passed:
            cp.wait_send()
        for cp in mine:
            cp.wait()

    hbm = pl.BlockSpec(memory_space=pltpu.HBM)
    return pl.pallas_call(
        body, name=name,
        out_shape=tuple(jax.ShapeDtypeStruct((N_DEV,) + a.shape, a.dtype) for a in arrays),
        in_specs=[hbm] * n, out_specs=tuple([hbm] * n),
        scratch_shapes=[pltpu.SemaphoreType.DMA((7 * n,)), pltpu.SemaphoreType.DMA((7 * n,)),
                        pltpu.SemaphoreType.DMA((n,))],
    )(*arrays)


def _all_gather_under_ln(arrays, x, g, b, name, tq=512):
    n = len(arrays)
    T, D = x.shape
    nt = T // tq

    def body(*refs):
        x_ref, g_ref, b_ref = refs[:3]
        ins = refs[3:3 + n]
        y_ref, yb_ref = refs[3 + n:5 + n]
        outs = refs[5 + n:5 + 2 * n]
        send_sems, recv_sems, local_sems = refs[5 + 2 * n:]
        i = pl.program_id(0)
        mx, my, mc = _mesh_pos()
        me, sibling = (mx, my, mc), (mx, my, 1 - mc)
        chips = [(1 - mx, my), (mx, 1 - my), (1 - mx, 1 - my)]

        def slot(a, pos):
            px, py, pc = pos
            return outs[a].at[4 * px + 2 * py + pc]

        def copy(a, k, block, to, src=None):
            return pltpu.make_async_remote_copy(
                src_ref=slot(a, block) if src is None else src, dst_ref=slot(a, block),
                send_sem=send_sems.at[a * 7 + k], recv_sem=recv_sems.at[a * 7 + k],
                device_id=to, device_id_type=MESH_ID)

        def own(a):
            return pltpu.make_async_copy(ins[a], slot(a, me), local_sems.at[a])

        def first(a):
            return [copy(a, 0, me, sibling, src=ins[a])] + [
                copy(a, 1 + j, me, (*chip, mc), src=ins[a]) for j, chip in enumerate(chips)]

        @pl.when(i == 0)
        def _():
            for a in range(n):
                own(a).start()
                for cp in first(a):
                    cp.start()

        zv = x_ref[...]
        mu = jnp.mean(zv, axis=1, keepdims=True)
        zc = zv - mu
        var = jnp.mean(zc * zc, axis=1, keepdims=True)
        y = zc * lax.rsqrt(var + LN_EPS) * g_ref[...] + b_ref[...]
        y_ref[...] = y
        yb_ref[...] = y.astype(BF16)

        @pl.when(i == nt - 1)
        def _():
            passed = []
            for j, chip in enumerate(chips):
                for a in range(n):
                    copy(a, 1 + j, (*chip, mc), me).wait_recv()
                    cp = copy(a, 4 + j, (*chip, mc), sibling)
                    cp.start()
                    passed.append(cp)
            for a in range(n):
                copy(a, 0, sibling, me).wait_recv()
                for j, chip in enumerate(chips):
                    copy(a, 4 + j, (*chip, 1 - mc), me).wait_recv()
            for a in range(n):
                for cp in first(a):
                    cp.wait_send()
                own(a).wait()
            for cp in passed:
                cp.wait_send()

    row = pl.BlockSpec((tq, D), lambda i: (i, 0))
    vec = pl.BlockSpec((1, D), lambda i: (0, 0))
    hbm = pl.BlockSpec(memory_space=pltpu.HBM)
    outs = pl.pallas_call(
        body, name=name,
        out_shape=(jax.ShapeDtypeStruct((T, D), F32), jax.ShapeDtypeStruct((T, D), BF16))
        + tuple(jax.ShapeDtypeStruct((N_DEV,) + a.shape, a.dtype) for a in arrays),
        grid=(nt,), in_specs=[row, vec, vec] + [hbm] * n, out_specs=tuple([row, row] + [hbm] * n),
        scratch_shapes=[pltpu.SemaphoreType.DMA((7 * n,)), pltpu.SemaphoreType.DMA((7 * n,)),
                        pltpu.SemaphoreType.DMA((n,))],
        compiler_params=_params(("arbitrary",)),
    )(x, g, b, *arrays)
    return outs[0], outs[1], outs[2:]


_HBM = pl.BlockSpec(memory_space=pltpu.HBM)
_SEM = pl.BlockSpec(memory_space=pltpu.SEMAPHORE)
_EFFECT = pltpu.SideEffectType.DATAFLOW_SIDE_EFFECTING
N_PEERS = N_DEV - 1


def _peer(k):
    x, y, c = _mesh_pos()
    return (1 - x if k & 4 else x, 1 - y if k & 2 else y, 1 - c if k & 1 else c)


def _split_start(srcs, scatter, after, name):
    n = len(srcs)
    zones = [jax.ShapeDtypeStruct(s.shape if scatter else ((N_DEV,) + s.shape), s.dtype) for s in srcs]

    def body(*refs):
        src, zone = refs[:n], refs[n:2 * n]
        outs = refs[2 * n + 1:]
        send, recv, token = outs[:n], outs[n:2 * n], outs[4 * n]
        x, y, c = _mesh_pos()
        my_idx = 4 * x + 2 * y + c
        for a in range(n):
            pltpu.make_async_copy(src[a].at[my_idx] if scatter else src[a],
                                  zone[a].at[N_PEERS] if scatter else zone[a].at[my_idx], recv[a]).start()
            for k in range(1, N_DEV):
                px, py, pc = _peer(k)
                pltpu.make_async_remote_copy(
                    src_ref=src[a].at[4 * px + 2 * py + pc] if scatter else src[a],
                    dst_ref=zone[a].at[k - 1] if scatter else zone[a].at[my_idx],
                    send_sem=send[a], recv_sem=recv[a], device_id=(px, py, pc), device_id_type=MESH_ID).start()
        token[...] = jnp.zeros_like(token)

    hbm = lambda a: pltpu.with_memory_space_constraint(a, pltpu.HBM)
    outs = pl.pallas_call(
        body, name=name,
        out_shape=tuple([pltpu.SemaphoreType.DMA(())] * (2 * n)
                        + [pltpu.HBM(s.shape, s.dtype) for s in srcs]
                        + [pltpu.HBM(z.shape, z.dtype) for z in zones]
                        + [jax.ShapeDtypeStruct((8, 128), F32)]),
        in_specs=[_HBM] * (2 * n) + [pl.BlockSpec(memory_space=pl.ANY)],
        out_specs=tuple([_SEM] * (2 * n) + [_HBM] * (2 * n) + [pl.BlockSpec(memory_space=pltpu.VMEM)]),
        input_output_aliases={**{a: 2 * n + a for a in range(n)}, **{n + a: 3 * n + a for a in range(n)}},
        compiler_params=pltpu.CompilerParams(has_side_effects=_EFFECT),
    )(*[hbm(s) for s in srcs], *[hbm(lax.empty(z.shape, z.dtype)) for z in zones], after)
    return outs[:n], outs[n:2 * n], outs[2 * n:3 * n], outs[3 * n:4 * n], outs[4 * n]


def _split_wait(send, recv, srcs, zones, after, name):
    n = len(srcs)

    def body(*refs):
        zone = refs[n:2 * n]
        send_sems, recv_sems = refs[2 * n:3 * n], refs[3 * n:4 * n]
        x, y, c = _mesh_pos()
        for a in range(n):
            seven = zone[a].at[pl.ds(0, N_PEERS)]
            pltpu.make_async_remote_copy(src_ref=seven, dst_ref=seven, send_sem=send_sems[a], recv_sem=recv_sems[a],
                                         device_id=(x, y, 1 - c), device_id_type=MESH_ID).wait_send()
            pltpu.make_async_remote_copy(src_ref=zone[a], dst_ref=zone[a], send_sem=send_sems[a],
                                         recv_sem=recv_sems[a], device_id=(x, y, 1 - c),
                                         device_id_type=MESH_ID).wait_recv()

    outs = pl.pallas_call(
        body, name=name,
        out_shape=tuple([pltpu.HBM(s.shape, s.dtype) for s in srcs] + [pltpu.HBM(z.shape, z.dtype) for z in zones]),
        in_specs=[_HBM] * (2 * n) + [_SEM] * (2 * n) + [pl.BlockSpec(memory_space=pl.ANY)],
        out_specs=tuple([_HBM] * (2 * n)),
        input_output_aliases={a: a for a in range(2 * n)},
        compiler_params=pltpu.CompilerParams(has_side_effects=_EFFECT),
    )(*srcs, *zones, *send, *recv, after)
    return outs[:n], outs[n:]


def _cat_blocks(g, axis):
    return jnp.concatenate([g[d] for d in range(N_DEV)], axis=axis)


N_LATENT = Q_LORA + KV_LORA + ROPE
W_SHARD = D_IN_PROJ // N_DEV


def _ref_cols(lo, hi):
    out = []
    if lo < N_LATENT:
        out.append((N_GATED + lo, N_GATED + min(hi, N_LATENT)))
    if hi > N_LATENT:
        out.append((max(lo, N_LATENT) - N_LATENT, hi - N_LATENT))
    return out


def _permute_w_in_t(blocks):
    pieces = []
    for lo, hi in ((N_LATENT, D_IN_PROJ), (0, N_LATENT)):
        for d in range(N_DEV):
            a, b = max(lo, d * W_SHARD), min(hi, (d + 1) * W_SHARD)
            if a < b:
                pieces.append(blocks[d][a - d * W_SHARD:b - d * W_SHARD])
    pieces.append(jnp.zeros((NPP - D_IN_PROJ, blocks.shape[2]), blocks.dtype))
    return jnp.concatenate(pieces, axis=0)


def _split_w_in_t(w):
    slabs = []
    for d in range(N_DEV):
        parts = [w[a:b] for a, b in _ref_cols(d * W_SHARD, (d + 1) * W_SHARD)]
        slabs.append(parts[0] if len(parts) == 1 else jnp.concatenate(parts, axis=0))
    return jnp.stack(slabs)


def _permute_w_uq(w):
    w3 = w.reshape(w.shape[0], N_HEADS, NOPE + ROPE)
    return jnp.concatenate([w3[:, :, :NOPE].reshape(w.shape[0], -1), w3[:, :, NOPE:].reshape(w.shape[0], -1)], axis=1)


def _unpermute_w_uq(w):
    nope = w[:, :N_HEADS * NOPE].reshape(w.shape[0], N_HEADS, NOPE)
    rope = w[:, N_HEADS * NOPE:].reshape(w.shape[0], N_HEADS, ROPE)
    return jnp.concatenate([nope, rope], axis=2).reshape(w.shape[0], -1)


_SMALL_EMB = (("emb_ln_g", 16), ("emb_ln_b", 16))
_SMALL_LAYER = (("q_norm_g", 8), ("kv_norm_g", 8), ("w_pool", 1024), ("pool_scale", 8), ("b_out", 32),
                ("ln_g", 32), ("ln_b", 32))
_SMALL = _SMALL_EMB + _SMALL_LAYER
CONV_ROWS = DEPTH * CONV_WIDTH * D_CONV // 128


def _pack_small(d, entries=_SMALL):
    parts = []
    for name, rows in entries:
        flat = d[name].reshape(-1)
        flat = jnp.pad(flat, (0, rows * 128 - flat.shape[0]))
        parts.append(flat.reshape(rows, 128))
    return jnp.concatenate(parts, axis=0)


def _unpack_small(packed, shapes):
    out, r0 = {}, 0
    for name, rows in _SMALL:
        size = 1
        for s in shapes[name]:
            size *= s
        out[name] = packed[r0:r0 + rows].reshape(-1)[:size].reshape(shapes[name])
        r0 += rows
    return out


def _rope_tables(positions):
    half = ROPE // 2
    inv_freq = ROPE_THETA ** (-jnp.arange(half, dtype=F32) / half)
    ang = positions.astype(F32)[:, None] * inv_freq
    cos, sin = jnp.cos(ang), jnp.sin(ang)
    return jnp.concatenate([cos, cos, cos, cos], axis=1), jnp.concatenate([-sin, sin, -sin, sin], axis=1)


def _local_step(x, positions, target, emb_g, emb_b, layer_weights, layer_weights_rest, on_sharded_grads,
                on_layer_grads=None, first_after=None, embedded=None):
    cos_t, sin_t = _rope_tables(positions)
    h, hb = _ln_fwd(x, emb_g, emb_b, "emb_ln_fwd") if embedded is None else embedded
    saved = []
    for l in range(DEPTH):
        W = layer_weights(l, h)
        proj = _mm(hb, W["w_in_t"], "nt", F32, "proj_fwd", after=first_after if l == 0 else None)
        qn, kvn, pooled, cv, mix = _mix_fwd(proj, W["q_norm_g"], W["kv_norm_g"], W["w_pool"], W["pool_scale"],
                                            W["conv_w"], "mix_fwd")
        rest, token = layer_weights_rest(l, proj)
        W = {**W, **rest}
        kv, qc, kc = _up_rope_fwd(qn, kvn, W["w_uq"], W["w_ukv"], proj, cos_t, sin_t, "up_rope_fwd", after=token)
        o, mix, lse = _flash_fwd(qc, kc, kv, proj, mix, "flash_fwd")
        z = _mm(mix, W["w_out"], "nn", F32, "out_fwd", res=h, bias=W["b_out"], alpha=ALPHA)
        saved.append((W, hb, proj, qn, kvn, pooled, cv, kv, qc, kc, o, lse, mix, z))
        h, hb = _ln_fwd(z, W["ln_g"], W["ln_b"], "ln_fwd")
    dh, sq = h, None

    grads = {k: [None] * DEPTH for k in ("q_norm_g", "kv_norm_g", "w_pool", "pool_scale", "conv_w", "b_out", "ln_g",
                                         "ln_b")}
    for l in reversed(range(DEPTH)):
        W, hb_in, proj, qn, kvn, pooled, cv, kv, qc, kc, o, lse, mix, z = saved[l]
        sharded = {}
        if l == DEPTH - 1:
            dz, dzb, grads["b_out"][l], grads["ln_g"][l], grads["ln_b"][l], sq = _ln_bwd(
                dh, z, W["ln_g"], "ln_bwd_loss", target=target)
        else:
            dz, dzb, grads["b_out"][l], grads["ln_g"][l], grads["ln_b"][l] = _ln_bwd(dh, z, W["ln_g"], "ln_bwd")
        dmix = _mm(dzb, W["w_out"], "nt", F32, "out_bwd_x")
        sharded["w_out"] = _mm(mix, dzb, "tn", GRAD_XFER, "out_bwd_w", tk=4096)
        do, delta, dproj, grads["w_pool"][l], grads["pool_scale"][l], grads["conv_w"][l] = _mix_bwd(
            dmix, proj, o, pooled, cv, W["w_pool"], W["pool_scale"], W["conv_w"], "mix_bwd")
        dqc, dkv, dkr = _flash_bwd(qc, kc, kv, do, lse, delta, "flash_bwd")
        dq, dkrope = _rope_bwd(dqc, dkr, cos_t, sin_t, "rope_bwd")
        sharded["w_uq"] = _mm(qn, dq, "tn", GRAD_XFER, "q_up_bwd_w")
        sharded["w_ukv"] = _mm(kvn, dkv, "tn", GRAD_XFER, "kv_up_bwd_w")
        token = on_sharded_grads(l, sharded)
        dproj, grads["q_norm_g"][l], grads["kv_norm_g"][l] = _up_rms_bwd(
            proj, dq, dkv, W["w_uq"], W["w_ukv"], dkrope, dproj, W["q_norm_g"], W["kv_norm_g"], "up_rms_bwd")
        if l == 0 and on_layer_grads is not None:
            token = on_layer_grads(grads, token)
        d_w_in_t = _mm(dproj, hb_in, "tn", GRAD_XFER, "proj_bwd_w", tk=4096, after=token)
        token = on_sharded_grads(l, {"w_in": d_w_in_t})
        dh = _mm(dproj, W["w_in_t"], "nn", F32, "proj_bwd_x", res=dz, alpha=ALPHA, tk=2560, after=token)
    grad_x, grads["emb_ln_g"], grads["emb_ln_b"] = _ln_bwd(dh, x, emb_g, "emb_ln_bwd", for_matmul=False)
    return sq, grad_x, grads


def kernel(x, positions, emb_ln_g, emb_ln_b, w_in, q_norm_g, kv_norm_g, w_uq, w_ukv, w_pool, pool_scale, conv_w, w_out, b_out, ln_g, ln_b, loss_target, m_emb_ln_g, m_emb_ln_b, m_w_in, m_q_norm_g, m_kv_norm_g, m_w_uq, m_w_ukv, m_w_pool, m_pool_scale, m_conv_w, m_w_out, m_b_out, m_ln_g, m_ln_b, v_emb_ln_g, v_emb_ln_b, v_w_in, v_q_norm_g, v_kv_norm_g, v_w_uq, v_w_ukv, v_w_pool, v_pool_scale, v_conv_w, v_w_out, v_b_out, v_ln_g, v_ln_b):
    weights = dict(emb_ln_g=emb_ln_g, emb_ln_b=emb_ln_b, w_in=w_in, q_norm_g=q_norm_g, kv_norm_g=kv_norm_g,
                   w_uq=w_uq, w_ukv=w_ukv, w_pool=w_pool, pool_scale=pool_scale, conv_w=conv_w, w_out=w_out,
                   b_out=b_out, ln_g=ln_g, ln_b=ln_b)
    mom1 = dict(emb_ln_g=m_emb_ln_g, emb_ln_b=m_emb_ln_b, w_in=m_w_in, q_norm_g=m_q_norm_g, kv_norm_g=m_kv_norm_g,
                w_uq=m_w_uq, w_ukv=m_w_ukv, w_pool=m_w_pool, pool_scale=m_pool_scale, conv_w=m_conv_w,
                w_out=m_w_out, b_out=m_b_out, ln_g=m_ln_g, ln_b=m_ln_b)
    mom2 = dict(emb_ln_g=v_emb_ln_g, emb_ln_b=v_emb_ln_b, w_in=v_w_in, q_norm_g=v_q_norm_g, kv_norm_g=v_kv_norm_g,
                w_uq=v_w_uq, w_ukv=v_w_ukv, w_pool=v_w_pool, pool_scale=v_pool_scale, conv_w=v_conv_w,
                w_out=v_w_out, b_out=v_b_out, ln_g=v_ln_g, ln_b=v_ln_b)

    big = ("w_in", "w_uq", "w_ukv", "w_out")

    conv_pad = jnp.zeros((8, 128), F32).at[0:DEPTH * CONV_WIDTH, 0:64].set(conv_w.reshape(DEPTH * CONV_WIDTH, 64))
    t12 = lambda a: jnp.swapaxes(a, 1, 2)
    shard = lambda k, l: (t12(weights[k])[l] if k == "w_in" else weights[k][l]).astype(BF16)
    h0, h0b, (w_in0, conv_all) = _all_gather_under_ln(
        [shard("w_in", 0), conv_pad], x[0], emb_ln_g.reshape(1, -1), emb_ln_b.reshape(1, -1), "w_in0_all_gather_emb_ln")
    rest0 = _split_start([shard(k, 0) for k in big[1:]], False, w_in0, "weights0_rest_start")
    conv_full = _cat_blocks(conv_all[:, 0:DEPTH * CONV_WIDTH, 0:64], 1).reshape(DEPTH, CONV_WIDTH, D_CONV)
    conv_full = jnp.pad(conv_full, ((0, 0), (0, 8 - CONV_WIDTH), (0, 0)))
    fetched = {}

    def layer_weights(l, ready):
        if l == 0:
            w_in_blocks = w_in0
        else:
            fetched[1] = _split_wait(*fetched["w1"][:4], ready, "weights1_wait")[1]
            w_in_blocks = fetched[1][0]
        return dict(
            w_in_t=_permute_w_in_t(w_in_blocks), conv_w=conv_full[l],
            q_norm_g=q_norm_g[l].reshape(1, -1), kv_norm_g=kv_norm_g[l].reshape(1, -1),
            w_pool=w_pool[l].astype(BF16), pool_scale=pool_scale[l].reshape(1, -1), b_out=b_out[l].reshape(1, -1),
            ln_g=ln_g[l].reshape(1, -1), ln_b=ln_b[l].reshape(1, -1))

    def layer_weights_rest(l, ready):
        token = None
        if l == 0:
            blocks = _split_wait(*rest0[:4], ready, "weights0_rest_wait")[1]
            fetched["w1"] = _split_start([shard(k, 1) for k in big], False, blocks[0], "weights1_start")
            token = fetched["w1"][4]
        else:
            blocks = fetched[1][1:]
        return dict(w_uq=_permute_w_uq(_cat_blocks(blocks[0], 1)), w_ukv=_cat_blocks(blocks[1], 1),
                    w_out=blocks[2].reshape(D_MIX, D_MODEL)), token

    by_dest = dict(
        w_in=_split_w_in_t,
        w_uq=lambda g: _unpermute_w_uq(g).reshape(Q_LORA, N_DEV, -1).transpose(1, 0, 2),
        w_ukv=lambda g: g.reshape(KV_LORA, N_DEV, -1).transpose(1, 0, 2),
        w_out=lambda g: g.reshape(N_DEV, -1, D_MODEL))
    in_flight = []

    def on_sharded_grads(l, g):
        names = [k for k in big if k in g]
        srcs = [by_dest[k](g[k]) for k in names]
        started = _split_start(srcs, True, srcs[0], "grads%d_%s_start" % (l, names[0]))
        in_flight.append((l, names, started[:4]))
        return started[4]

    small_in_flight = []

    def on_layer_grads(g, token):
        stacked = {k: jnp.stack(g[k]) for k, _ in _SMALL_LAYER}
        conv = jnp.stack([g["conv_w"][l][0:CONV_WIDTH] for l in range(DEPTH)]).reshape(CONV_ROWS, 128)
        packed = jnp.concatenate([_pack_small(stacked, _SMALL_LAYER), conv], axis=0)
        started = _split_start([packed], False, token, "layer_grads_start")
        small_in_flight.append(started[:4])
        return started[4]

    sq, grad_x, G = _local_step(x[0], positions[0], loss_target[0], emb_ln_g.reshape(1, -1),
                                emb_ln_b.reshape(1, -1), layer_weights, layer_weights_rest, on_sharded_grads,
                                on_layer_grads, first_after=rest0[4], embedded=(h0, h0b))
    loss = lax.psum(sq[0, 0] * (0.5 / D_MODEL), ("x", "y", "c"))

    res = {}
    landed = {}
    for l, names, started in in_flight:
        zones = _split_wait(*started, grad_x, "grads%d_%s_wait" % (l, names[0]))[1]
        for k, zone in zip(names, zones):
            landed[k, l] = zone
    w_in_res = None
    for l in reversed(range(DEPTH)):
        w_in_res = _adamw([landed["w_in", l]], t12(w_in), t12(m_w_in), t12(v_w_in), "adamw_w_in_%d" % l, W_SHARD, 512,
                          first_layer=l, into=w_in_res)
    res["w_in"] = tuple(t12(o) for o in w_in_res)
    for name, rows in (("w_uq", 256), ("w_ukv", 256), ("w_out", 128)):
        res[name] = _adamw([landed[name, l] for l in range(DEPTH)], weights[name], mom1[name], mom2[name],
                           "adamw_" + name, rows)

    layer_zone = _split_wait(*small_in_flight[0], grad_x, "layer_grads_wait")[1][0]
    emb_zone = _all_gather([_pack_small(G, _SMALL_EMB)], "emb_grads_all_gather")[0]
    n_layer_rows = sum(r for _, r in _SMALL_LAYER)
    l_small = jnp.concatenate([emb_zone, layer_zone[:, 0:n_layer_rows]], axis=1)
    my_idx = 4 * lax.axis_index("x") + 2 * lax.axis_index("y") + lax.axis_index("c")
    conv_all_grads = layer_zone[:, n_layer_rows:].reshape(N_DEV, DEPTH * CONV_WIDTH, D_CONV)
    l_conv = lax.dynamic_slice_in_dim(conv_all_grads, my_idx * 64, 64, axis=2)
    l_conv = jnp.zeros((N_DEV, 8, 128), F32).at[:, 0:DEPTH * CONV_WIDTH, 0:64].set(l_conv)
    conv_shard = lambda a: jnp.zeros((8, 128), F32).at[0:DEPTH * CONV_WIDTH, 0:64].set(a.reshape(-1, 64))
    conv_res = _adamw([l_conv], conv_shard(conv_w)[None], conv_shard(m_conv_w)[None], conv_shard(v_conv_w)[None],
                      "adamw_conv_w", 8)
    res["conv_w"] = tuple(o[0, 0:DEPTH * CONV_WIDTH, 0:64].reshape(DEPTH, CONV_WIDTH, 64) for o in conv_res)
    small_res = _adamw([l_small], _pack_small(weights)[None], _pack_small(mom1)[None], _pack_small(mom2)[None],
                       "adamw_small", 392)
    shapes = {k: weights[k].shape for k, _ in _SMALL}
    unpacked = [_unpack_small(o[0], shapes) for o in small_res]
    for k, _ in _SMALL:
        res[k] = tuple(u[k] for u in unpacked)

    order = ("emb_ln_g", "emb_ln_b", "w_in", "q_norm_g", "kv_norm_g", "w_uq", "w_ukv", "w_pool", "pool_scale",
             "conv_w", "w_out", "b_out", "ln_g", "ln_b")
    return (loss, grad_x[None], *[res[k][0] for k in order], *[res[k][1] for k in order],
            *[res[k][2] for k in order], *[res[k][3] for k in order])
```

```python
import jax
import jax.numpy as jnp
from jax import lax
from jax.experimental import pallas as pl
from jax.experimental.pallas import tpu as pltpu

F32 = jnp.float32
BF16 = jnp.bfloat16

N_DEV = 8
D_MODEL = 2048
DEPTH = 2
N_HEADS = 8
NOPE = 128
ROPE = 64
V_DIM = 128
Q_LORA = 512
KV_LORA = 256
D_MLA = N_HEADS * V_DIM
D_POOL = 512
D_CONV = 512
POOL_WINDOWS = (2, 4, 8, 16)
POOL_GROUP = 128
CONV_WIDTH = 3
D_MIX = D_MLA + D_POOL + D_CONV
D_IN_PROJ = 4928
ROPE_THETA = 10000.0
LN_EPS = 1e-5
RMS_EPS = 1e-6
ALPHA = (2 * DEPTH) ** 0.25
ATTN_SCALE = (NOPE + ROPE) ** -0.5
ADAM_LR = 0.001
ADAM_B1 = 0.9
ADAM_B2 = 0.999
ADAM_EPS = 1e-08
ADAM_WD = 0.01
ADAM_STEP = 10

O_GMLA, O_PIN, O_GPOOL, O_CH, O_CB, O_CC, O_GCONV, O_QLAT, O_KVLAT, O_KROPE = (
    0, 1024, 1536, 2048, 2560, 3072, 3584, 4096, 4608, 4864)
NPP = 5120
N_GATED = O_QLAT
QC = NOPE + 2 * ROPE
HALO = 16
ATT_TILE = 512
ATT_CH = 256
LOG2E = 1.4426950408889634
EXP2_SCALE = ATTN_SCALE * LOG2E

GRAD_XFER = BF16
VMEM_LIMIT = 48 * 1024 * 1024
MESH_ID = pl.DeviceIdType.MESH


def _params(sem=None):
    return pltpu.CompilerParams(dimension_semantics=sem, vmem_limit_bytes=VMEM_LIMIT)


def _sigmoid(x):
    return 1.0 / (1.0 + jnp.exp(-x))


def _tile(dim, target):
    if dim <= target:
        return dim
    t = target - target % 128
    while dim % t:
        t -= 128
    return t


_DIMS = {"nn": (((1,), (0,)), ((), ())), "nt": (((1,), (1,)), ((), ())), "tn": (((0,), (0,)), ((), ()))}


def _mm(a, b, mode, out_dtype, name, res=None, bias=None, alpha=1.0, tm=1024, tn=1024, tk=2048, after=None):
    if mode == "nn":
        (M, K), (K2, N) = a.shape, b.shape
    elif mode == "nt":
        (M, K), (N, K2) = a.shape, b.shape
    else:
        (K, M), (K2, N) = a.shape, b.shape
    assert K == K2
    tm, tn, tk = _tile(M, tm), _tile(N, tn), _tile(K, tk)
    nk = K // tk
    has_res, has_bias = res is not None, bias is not None

    def body(*refs):
        a_ref, b_ref = refs[0], refs[1]
        pos = 2
        res_ref = bias_ref = None
        if has_res:
            res_ref = refs[pos]
            pos += 1
        if has_bias:
            bias_ref = refs[pos]
            pos += 1
        def finish(r, o_ref):
            if has_bias:
                r = r + bias_ref[...]
            if has_res:
                r = alpha * res_ref[...] + r
            o_ref[...] = r.astype(out_dtype)

        part = lax.dot_general(a_ref[...].astype(BF16), b_ref[...].astype(BF16), _DIMS[mode],
                               preferred_element_type=F32)
        if nk == 1:
            finish(part, refs[-1])
            return
        o_ref, acc_ref = refs[-2], refs[-1]
        k = pl.program_id(2)

        @pl.when(k == 0)
        def _():
            acc_ref[...] = part

        @pl.when(jnp.logical_and(k > 0, k < nk - 1))
        def _():
            acc_ref[...] += part

        @pl.when(k == nk - 1)
        def _():
            finish(acc_ref[...] + part, o_ref)

    if mode == "nn":
        in_specs = [pl.BlockSpec((tm, tk), lambda i, j, k: (i, k)), pl.BlockSpec((tk, tn), lambda i, j, k: (k, j))]
    elif mode == "nt":
        in_specs = [pl.BlockSpec((tm, tk), lambda i, j, k: (i, k)), pl.BlockSpec((tn, tk), lambda i, j, k: (j, k))]
    else:
        in_specs = [pl.BlockSpec((tk, tm), lambda i, j, k: (k, i)), pl.BlockSpec((tk, tn), lambda i, j, k: (k, j))]
    args = [a, b]
    if has_res:
        in_specs.append(pl.BlockSpec((tm, tn), lambda i, j, k: (i, j)))
        args.append(res)
    if has_bias:
        in_specs.append(pl.BlockSpec((1, tn), lambda i, j, k: (0, j)))
        args.append(bias)
    if after is not None:
        in_specs.append(pl.BlockSpec((8, 128), lambda i, j, k: (0, 0)))
        args.append(after)
    return pl.pallas_call(
        body, name=name,
        out_shape=jax.ShapeDtypeStruct((M, N), out_dtype),
        grid=(M // tm, N // tn, nk),
        in_specs=in_specs,
        out_specs=pl.BlockSpec((tm, tn), lambda i, j, k: (i, j)),
        scratch_shapes=[pltpu.VMEM((tm, tn), F32)] if nk > 1 else [],
        compiler_params=_params(("parallel", "parallel", "arbitrary")),
    )(*args)


def _ln_fwd(z, g, b, name, tq=512):
    T, D = z.shape

    def body(z_ref, g_ref, b_ref, y_ref, yb_ref):
        zv = z_ref[...]
        mu = jnp.mean(zv, axis=1, keepdims=True)
        zc = zv - mu
        var = jnp.mean(zc * zc, axis=1, keepdims=True)
        y = zc * lax.rsqrt(var + LN_EPS) * g_ref[...] + b_ref[...]
        y_ref[...] = y
        yb_ref[...] = y.astype(BF16)

    row = pl.BlockSpec((tq, D), lambda i: (i, 0))
    vec = pl.BlockSpec((1, D), lambda i: (0, 0))
    return pl.pallas_call(
        body, name=name,
        out_shape=(jax.ShapeDtypeStruct((T, D), F32), jax.ShapeDtypeStruct((T, D), BF16)),
        grid=(T // tq,), in_specs=[row, vec, vec], out_specs=(row, row),
        compiler_params=_params(("parallel",)),
    )(z, g, b)


def _ln_bwd(dy, z, g, name, tq=512, target=None, for_matmul=True):
    T, D = z.shape
    with_loss = target is not None

    def body(*refs):
        dy_ref, z_ref, g_ref = refs[:3]
        outs = list(refs[4 if with_loss else 3:])
        dz_ref = outs.pop(0)
        dzb_ref, ds_ref = (outs.pop(0), outs.pop(0)) if for_matmul else (None, None)
        dg_ref, db_ref = outs.pop(0), outs.pop(0)
        sq_ref = outs.pop(0) if with_loss else None

        @pl.when(pl.program_id(0) == 0)
        def _():
            for ref in (dg_ref, db_ref, ds_ref, sq_ref):
                if ref is not None:
                    ref[...] = jnp.zeros_like(ref)

        zv, dyv = z_ref[...], dy_ref[...]
        if with_loss:
            err = dyv - refs[3][...]
            sq_ref[...] += jnp.sum(err * err)
            dyv = err * (1.0 / D)
        mu = jnp.mean(zv, axis=1, keepdims=True)
        zc = zv - mu
        var = jnp.mean(zc * zc, axis=1, keepdims=True)
        rstd = lax.rsqrt(var + LN_EPS)
        xh = zc * rstd
        u = dyv * g_ref[...]
        dz = rstd * (u - jnp.mean(u, axis=1, keepdims=True) - xh * jnp.mean(u * xh, axis=1, keepdims=True))
        dz_ref[...] = dz
        dg_ref[...] += jnp.sum(dyv * xh, axis=0, keepdims=True)
        db_ref[...] += jnp.sum(dyv, axis=0, keepdims=True)
        if for_matmul:
            dzb_ref[...] = dz.astype(BF16)
            ds_ref[...] += jnp.sum(dz, axis=0, keepdims=True)

    row = pl.BlockSpec((tq, D), lambda i: (i, 0))
    vec = pl.BlockSpec((1, D), lambda i: (0, 0))
    vshape = jax.ShapeDtypeStruct((1, D), F32)
    out_shape, out_specs = [jax.ShapeDtypeStruct((T, D), F32)], [row]
    if for_matmul:
        out_shape += [jax.ShapeDtypeStruct((T, D), BF16), vshape]
        out_specs += [row, vec]
    out_shape += [vshape, vshape]
    out_specs += [vec, vec]
    if with_loss:
        out_shape.append(jax.ShapeDtypeStruct((8, 128), F32))
        out_specs.append(pl.BlockSpec((8, 128), lambda i: (0, 0)))
    return pl.pallas_call(
        body, name=name, out_shape=tuple(out_shape), grid=(T // tq,),
        in_specs=[row, row, vec] + ([row] if with_loss else []), out_specs=tuple(out_specs),
        compiler_params=_params(("arbitrary",)),
    )(dy, z, g, *([target] if with_loss else []))


def _pblock(tq, width, offset):
    assert offset % width == 0
    blk = offset // width
    return pl.BlockSpec((tq, width), lambda i: (i, blk))


def _mix_fwd(proj, q_g, kv_g, w_pool, pool_scale, conv_w, name, tq=256):
    T = proj.shape[0]

    def body(ql_ref, kvl_ref, pin_ref, gp_ref, ch_ref, cb_ref, cc_ref, gc_ref, qg_ref, kvg_ref, wp_ref, ps_ref,
             cw_ref, qn_ref, kvn_ref, pooled_ref, cv_ref, ypc_ref, extp, extu):
        i = pl.program_id(0)
        for x_ref, g_ref, o_ref in ((ql_ref, qg_ref, qn_ref), (kvl_ref, kvg_ref, kvn_ref)):
            x = x_ref[...]
            r = lax.rsqrt(jnp.mean(x * x, axis=1, keepdims=True) + RMS_EPS)
            o_ref[...] = (x * r * g_ref[...]).astype(BF16)

        @pl.when(i == 0)
        def _():
            extp[0:HALO, :] = jnp.zeros((HALO, D_POOL), F32)
            extu[0:HALO, :] = jnp.zeros((HALO, D_CONV), F32)

        @pl.when(i > 0)
        def _():
            extp[0:HALO, :] = extp[tq:tq + HALO, :]
            extu[0:HALO, :] = extu[tq:tq + HALO, :]

        t1 = (i * tq + lax.broadcasted_iota(jnp.int32, (tq, 1), 0) + 1).astype(F32)
        for g, w in enumerate(POOL_WINDOWS):
            cols = slice(g * POOL_GROUP, (g + 1) * POOL_GROUP)
            pin = pin_ref[:, cols]
            extp[HALO:, cols] = pin
            s = extp[:, cols]
            k = 1
            while k < w:
                s = s + pltpu.roll(s, k, 0)
                k *= 2
            mean = s[HALO:, :] / jnp.minimum(t1, float(w))
            pooled = (mean - pin).astype(BF16)
            pooled_ref[:, cols] = pooled
            r = jnp.dot(pooled, wp_ref[g], preferred_element_type=F32)
            gp = gp_ref[:, cols]
            ypc_ref[:, cols] = (r * ps_ref[:, cols] * (gp * _sigmoid(gp))).astype(BF16)
        for g in range(D_CONV // 128):
            cols = slice(g * 128, (g + 1) * 128)
            u = cc_ref[:, cols] * ch_ref[:, cols]
            extu[HALO:, cols] = u
            eu = extu[:, cols]
            u1 = pltpu.roll(eu, 1, 0)[HALO:, :]
            u2 = pltpu.roll(eu, 2, 0)[HALO:, :]
            cv = cw_ref[0:1, cols] * u2 + cw_ref[1:2, cols] * u1 + cw_ref[2:3, cols] * u
            cv_ref[:, cols] = cv
            gc = gc_ref[:, cols]
            ypc_ref[:, D_POOL + g * 128:D_POOL + (g + 1) * 128] = (
                cb_ref[:, cols] * cv * (gc * _sigmoid(gc))).astype(BF16)

    full = lambda shape: pl.BlockSpec(shape, lambda i: (0,) * len(shape))
    row = lambda w: pl.BlockSpec((tq, w), lambda i: (i, 0))
    return pl.pallas_call(
        body, name=name,
        out_shape=(jax.ShapeDtypeStruct((T, Q_LORA), BF16), jax.ShapeDtypeStruct((T, KV_LORA), BF16),
                   jax.ShapeDtypeStruct((T, D_POOL), BF16), jax.ShapeDtypeStruct((T, D_CONV), F32),
                   jax.ShapeDtypeStruct((T, D_MIX), BF16)),
        grid=(T // tq,),
        in_specs=[_pblock(tq, Q_LORA, O_QLAT), _pblock(tq, KV_LORA, O_KVLAT), _pblock(tq, 512, O_PIN),
                  _pblock(tq, 512, O_GPOOL), _pblock(tq, 512, O_CH), _pblock(tq, 512, O_CB), _pblock(tq, 512, O_CC),
                  _pblock(tq, 512, O_GCONV), full((1, Q_LORA)), full((1, KV_LORA)), full((4, 128, 128)),
                  full((1, D_POOL)), full((8, D_CONV))],
        out_specs=(row(Q_LORA), row(KV_LORA), row(D_POOL), row(D_CONV),
                   pl.BlockSpec((tq, D_POOL + D_CONV), lambda i: (i, D_MLA // (D_POOL + D_CONV)))),
        scratch_shapes=[pltpu.VMEM((tq + HALO, D_POOL), F32), pltpu.VMEM((tq + HALO, D_CONV), F32)],
        compiler_params=_params(("arbitrary",)),
    )(proj, proj, proj, proj, proj, proj, proj, proj, q_g, kv_g, w_pool, pool_scale, conv_w)


def _mix_bwd(dmix, proj, o, pooled, cv, w_pool, pool_scale, conv_w, name, tq=ATT_CH):
    T = proj.shape[0]
    nt = T // tq
    n_ext = tq + HALO

    def body(dym_ref, dyp_ref, dyc_ref, gm_ref, gp_ref, ch_ref, cb_ref, cc_ref, gc_ref, o_ref, pooled_ref, cv_ref,
             wp_ref, ps_ref, cw_ref, do_ref, delta_ref, dg_ref, dwp_ref, dps_ref, dcw_ref, exte, extd):
        i = pl.program_id(0)
        tile = nt - 1 - i

        @pl.when(i == 0)
        def _():
            dwp_ref[...] = jnp.zeros_like(dwp_ref)
            dps_ref[...] = jnp.zeros_like(dps_ref)
            dcw_ref[...] = jnp.zeros_like(dcw_ref)
            exte[tq:, :] = jnp.zeros((HALO, D_POOL), F32)
            extd[tq:, :] = jnp.zeros((HALO, D_CONV), F32)

        @pl.when(i > 0)
        def _():
            exte[tq:, :] = exte[0:HALO, :]
            extd[tq:, :] = extd[0:HALO, :]

        ones = jnp.ones((8, V_DIM), F32)
        for h in range(N_HEADS):
            cols = slice(h * V_DIM, (h + 1) * V_DIM)
            gm = gm_ref[:, cols]
            sig = _sigmoid(gm)
            dym = dym_ref[:, cols]
            ov = o_ref[:, cols]
            do = dym * (gm * sig)
            do_ref[:, cols] = do.astype(BF16)
            rows = lax.dot_general(ones, do * ov, _DIMS["nt"], precision=lax.Precision.HIGHEST,
                                   preferred_element_type=F32)
            delta_ref[h, 0] = rows[0:1, :]
            dg_ref[:, O_GMLA + h * V_DIM:O_GMLA + (h + 1) * V_DIM] = (
                dym * ov * (sig * (1.0 + gm * (1.0 - sig)))).astype(BF16)

        t1 = (tile * tq + lax.broadcasted_iota(jnp.int32, (tq, 1), 0) + 1).astype(F32)
        for g, w in enumerate(POOL_WINDOWS):
            cols = slice(g * POOL_GROUP, (g + 1) * POOL_GROUP)
            pg = pooled_ref[:, cols]
            r = jnp.dot(pg, wp_ref[g], preferred_element_type=F32)
            gp = gp_ref[:, cols]
            sg = _sigmoid(gp)
            sl = gp * sg
            dyg = dyp_ref[:, cols]
            ps = ps_ref[:, cols]
            dg_ref[:, O_GPOOL + g * POOL_GROUP:O_GPOOL + (g + 1) * POOL_GROUP] = (
                dyg * (r * ps) * (sg * (1.0 + gp * (1.0 - sg)))).astype(BF16)
            dps_ref[:, cols] += jnp.sum(dyg * r * sl, axis=0, keepdims=True)
            dr = (dyg * ps * sl).astype(BF16)
            dwp_ref[g] += lax.dot_general(pg, dr, _DIMS["tn"], preferred_element_type=F32)
            dpooled = lax.dot_general(dr, wp_ref[g], _DIMS["nt"], preferred_element_type=F32)
            exte[0:tq, cols] = dpooled / jnp.minimum(t1, float(w))
            s = exte[:, cols]
            k = 1
            while k < w:
                s = s + pltpu.roll(s, n_ext - k, 0)
                k *= 2
            dg_ref[:, O_PIN + g * POOL_GROUP:O_PIN + (g + 1) * POOL_GROUP] = (s[0:tq, :] - dpooled).astype(BF16)

        for g in range(D_CONV // 128):
            cols = slice(g * 128, (g + 1) * 128)
            out = lambda base: slice(base + g * 128, base + (g + 1) * 128)
            gc = gc_ref[:, cols]
            sg = _sigmoid(gc)
            sl = gc * sg
            dyc = dyc_ref[:, cols]
            cb, cc, ch, cvv = cb_ref[:, cols], cc_ref[:, cols], ch_ref[:, cols], cv_ref[:, cols]
            dcv = dyc * cb * sl
            dg_ref[:, out(O_GCONV)] = (dyc * (cb * cvv) * (sg * (1.0 + gc * (1.0 - sg)))).astype(BF16)
            dg_ref[:, out(O_CB)] = (dyc * cvv * sl).astype(BF16)
            extd[0:tq, cols] = dcv
            ed = extd[:, cols]
            d1 = pltpu.roll(ed, n_ext - 1, 0)[0:tq, :]
            d2 = pltpu.roll(ed, n_ext - 2, 0)[0:tq, :]
            du = cw_ref[2:3, cols] * dcv + cw_ref[1:2, cols] * d1 + cw_ref[0:1, cols] * d2
            u = cc * ch
            dcw_ref[0:1, cols] += jnp.sum(u * d2, axis=0, keepdims=True)
            dcw_ref[1:2, cols] += jnp.sum(u * d1, axis=0, keepdims=True)
            dcw_ref[2:3, cols] += jnp.sum(u * dcv, axis=0, keepdims=True)
            dg_ref[:, out(O_CH)] = (du * cc).astype(BF16)
            dg_ref[:, out(O_CC)] = (du * ch).astype(BF16)

    def rblock(width, offset):
        assert offset % width == 0
        blk = offset // width
        return pl.BlockSpec((tq, width), lambda i: (nt - 1 - i, blk))

    full = lambda shape: pl.BlockSpec(shape, lambda i: (0,) * len(shape))
    return pl.pallas_call(
        body, name=name,
        out_shape=(jax.ShapeDtypeStruct((T, D_MLA), BF16), jax.ShapeDtypeStruct((N_HEADS, nt, 1, tq), F32),
                   jax.ShapeDtypeStruct((T, NPP), BF16),
                   jax.ShapeDtypeStruct((4, 128, 128), F32), jax.ShapeDtypeStruct((1, D_POOL), F32),
                   jax.ShapeDtypeStruct((8, D_CONV), F32)),
        grid=(nt,),
        in_specs=[rblock(1024, 0), rblock(512, 1024), rblock(512, 1536),
                  rblock(1024, O_GMLA), rblock(512, O_GPOOL), rblock(512, O_CH), rblock(512, O_CB),
                  rblock(512, O_CC), rblock(512, O_GCONV), rblock(1024, 0), rblock(512, 0), rblock(512, 0),
                  full((4, 128, 128)), full((1, D_POOL)), full((8, D_CONV))],
        out_specs=(rblock(1024, 0), pl.BlockSpec((N_HEADS, 1, 1, tq), lambda i: (0, nt - 1 - i, 0, 0)),
                   rblock(N_GATED, 0), full((4, 128, 128)), full((1, D_POOL)), full((8, D_CONV))),
        scratch_shapes=[pltpu.VMEM((n_ext, D_POOL), F32), pltpu.VMEM((n_ext, D_CONV), F32)],
        compiler_params=_params(("arbitrary",)),
    )(dmix, dmix, dmix, proj, proj, proj, proj, proj, proj, o, pooled, cv, w_pool, pool_scale, conv_w)


def _up_rms_bwd(proj, qn, kvn, dq, dkv, w_uq, w_ukv, dkrope, dproj, q_g, kv_g, name, tq=256):
    T = proj.shape[0]
    n_lat = NPP - N_GATED
    nt = T // tq

    def body(ql_ref, kvl_ref, qn_ref, kvn_ref, dq_ref, dkv_ref, wq_ref, wkv_ref, dkr_ref, _, qg_ref, kvg_ref,
             dlat_ref, dqg_ref, dkvg_ref, dwq_ref, dwkv_ref, dwq_sc, dwkv_sc):
        i = pl.program_id(0)

        @pl.when(i == 0)
        def _():
            dqg_ref[...] = jnp.zeros_like(dqg_ref)
            dkvg_ref[...] = jnp.zeros_like(dkvg_ref)
            dwq_sc[...] = jnp.zeros_like(dwq_sc)
            dwkv_sc[...] = jnp.zeros_like(dwkv_sc)

        dq, dkv = dq_ref[...], dkv_ref[...]
        dwq_sc[...] += lax.dot_general(qn_ref[...], dq, _DIMS["tn"], preferred_element_type=F32)
        dwkv_sc[...] += lax.dot_general(kvn_ref[...], dkv, _DIMS["tn"], preferred_element_type=F32)

        @pl.when(i == nt - 1)
        def _():
            dwq_ref[...] = dwq_sc[...].astype(GRAD_XFER)
            dwkv_ref[...] = dwkv_sc[...].astype(GRAD_XFER)

        dqn = lax.dot_general(dq, wq_ref[...], _DIMS["nt"], preferred_element_type=F32)
        dkvn = lax.dot_general(dkv, wkv_ref[...], _DIMS["nt"], preferred_element_type=F32)
        for x_ref, dy, g_ref, c0, dg_ref in ((ql_ref, dqn, qg_ref, 0, dqg_ref),
                                             (kvl_ref, dkvn, kvg_ref, Q_LORA, dkvg_ref)):
            x = x_ref[...]
            r = lax.rsqrt(jnp.mean(x * x, axis=1, keepdims=True) + RMS_EPS)
            xr = x * r
            u = dy * g_ref[...]
            dlat_ref[:, c0:c0 + x.shape[1]] = (r * (u - xr * jnp.mean(u * xr, axis=1, keepdims=True))).astype(BF16)
            dg_ref[...] += jnp.sum(dy * xr, axis=0, keepdims=True)
        dlat_ref[:, Q_LORA + KV_LORA:] = dkr_ref[...]

    row = lambda w: pl.BlockSpec((tq, w), lambda i: (i, 0))
    vec = lambda w: pl.BlockSpec((1, w), lambda i: (0, 0))
    assert N_GATED % n_lat == 0
    whole = lambda a: pl.BlockSpec(a.shape, lambda i: (0, 0))
    return pl.pallas_call(
        body, name=name,
        out_shape=(jax.ShapeDtypeStruct((T, NPP), BF16),
                   jax.ShapeDtypeStruct((1, Q_LORA), F32), jax.ShapeDtypeStruct((1, KV_LORA), F32),
                   jax.ShapeDtypeStruct(w_uq.shape, GRAD_XFER), jax.ShapeDtypeStruct(w_ukv.shape, GRAD_XFER)),
        grid=(nt,),
        in_specs=[_pblock(tq, Q_LORA, O_QLAT), _pblock(tq, KV_LORA, O_KVLAT), row(Q_LORA), row(KV_LORA),
                  row(dq.shape[1]), row(dkv.shape[1]), whole(w_uq), whole(w_ukv),
                  row(n_lat - Q_LORA - KV_LORA), pl.BlockSpec(memory_space=pl.ANY), vec(Q_LORA), vec(KV_LORA)],
        out_specs=(pl.BlockSpec((tq, n_lat), lambda i: (i, N_GATED // n_lat)), vec(Q_LORA), vec(KV_LORA),
                   whole(w_uq), whole(w_ukv)),
        input_output_aliases={9: 0},
        scratch_shapes=[pltpu.VMEM(w_uq.shape, F32), pltpu.VMEM(w_ukv.shape, F32)],
        compiler_params=_params(("arbitrary",)),
    )(proj, proj, qn, kvn, dq, dkv, w_uq, w_ukv, dkrope, dproj, q_g, kv_g)


def _swap_halves(x, lo):
    return jnp.where(lo, pltpu.roll(x, 96, 1), pltpu.roll(x, 32, 1))


def _up_rope_fwd(qn, kvn, w_uq, w_ukv, proj, cos_t, sin_t, name, tq=256, after=None):
    T = qn.shape[0]

    def body(qn_ref, kvn_ref, wq_ref, wkv_ref, kr_ref, c_ref, s_ref, *rest):
        kv_ref, qc_ref, kc_ref = rest[-3:]
        q = jnp.dot(qn_ref[...], wq_ref[...], preferred_element_type=F32)
        kv_ref[...] = jnp.dot(kvn_ref[...], wkv_ref[...], preferred_element_type=F32).astype(BF16)
        C, S = c_ref[...], s_ref[...]
        lane = lax.broadcasted_iota(jnp.int32, (tq, 128), 1)
        lo = (lane % ROPE) < (ROPE // 2)
        first = lane < ROPE

        def rope(x):
            return x * C + _swap_halves(x, lo) * S

        kr = jnp.where(first, rope(kr_ref[...]), 0.0).astype(BF16)
        n_nope = N_HEADS * NOPE
        for j in range(N_HEADS // 2):
            r = rope(q[:, n_nope + j * 128:n_nope + (j + 1) * 128])
            pair = (jnp.where(first, r, 0.0), jnp.where(first, pltpu.roll(r, 64, 1), 0.0))
            for hh in range(2):
                h = 2 * j + hh
                qc_ref[h, :, 0:NOPE] = q[:, h * NOPE:(h + 1) * NOPE].astype(BF16)
                qc_ref[h, :, NOPE:QC] = pair[hh].astype(BF16)
        for h in range(N_HEADS):
            kc_ref[h, :, 0:NOPE] = kv_ref[:, h * 256:h * 256 + NOPE]
            kc_ref[h, :, NOPE:QC] = kr

    out = jax.ShapeDtypeStruct((N_HEADS, T, QC), BF16)
    hblock = pl.BlockSpec((N_HEADS, tq, QC), lambda i: (0, i, 0))
    row = lambda w: pl.BlockSpec((tq, w), lambda i: (i, 0))
    full = lambda a: pl.BlockSpec(a.shape, lambda i: (0, 0))
    return pl.pallas_call(
        body, name=name, out_shape=(jax.ShapeDtypeStruct((T, 2 * D_MLA), BF16), out, out), grid=(T // tq,),
        in_specs=[row(Q_LORA), row(KV_LORA), full(w_uq), full(w_ukv), _pblock(tq, 128, O_KROPE), row(128), row(128)]
        + ([pl.BlockSpec((8, 128), lambda i: (0, 0))] if after is not None else []),
        out_specs=(row(2 * D_MLA), hblock, hblock),
        compiler_params=_params(("parallel",)),
    )(qn, kvn, w_uq, w_ukv, proj, cos_t, sin_t, *([after] if after is not None else []))


def _rope_bwd(dqc, dkr, cos_t, sin_t, name, tq=256):
    T = dqc.shape[1]

    def body(dqc_ref, dkr_ref, c_ref, s_ref, dq_ref, dk_ref):
        C, S = c_ref[...], s_ref[...]
        lane = lax.broadcasted_iota(jnp.int32, (tq, 128), 1)
        lo = (lane % ROPE) < (ROPE // 2)
        first = lane < ROPE

        def unrope(dy):
            return dy * C - _swap_halves(dy, lo) * S

        acc = dkr_ref[0]
        for h in range(1, N_HEADS):
            acc = acc + dkr_ref[h]
        dk_ref[:, 0:128] = jnp.where(first, unrope(acc), 0.0).astype(BF16)
        dk_ref[:, 128:256] = jnp.zeros((tq, 128), BF16)
        for j in range(N_HEADS // 2):
            d0 = dqc_ref[2 * j, :, NOPE:QC]
            d1 = dqc_ref[2 * j + 1, :, NOPE:QC]
            comb = jnp.where(first, d0, pltpu.roll(d1, 64, 1))
            dq_ref[:, 1024 + j * 128:1024 + (j + 1) * 128] = unrope(comb).astype(BF16)
        for h in range(N_HEADS):
            dq_ref[:, h * NOPE:(h + 1) * NOPE] = dqc_ref[h, :, 0:NOPE].astype(BF16)

    tab = pl.BlockSpec((tq, 128), lambda i: (i, 0))
    return pl.pallas_call(
        body, name=name,
        out_shape=(jax.ShapeDtypeStruct((T, 1536), BF16), jax.ShapeDtypeStruct((T, 256), BF16)),
        grid=(T // tq,),
        in_specs=[pl.BlockSpec((N_HEADS, tq, QC), lambda i: (0, i, 0)),
                  pl.BlockSpec((N_HEADS, tq, 128), lambda i: (0, i, 0)), tab, tab],
        out_specs=(pl.BlockSpec((tq, 1536), lambda i: (i, 0)), pl.BlockSpec((tq, 256), lambda i: (i, 0))),
        compiler_params=_params(("parallel",)),
    )(dqc, dkr, cos_t, sin_t)


def _flash_fwd(qc, kc, kv, proj, mix, name):
    H, T, _ = qc.shape
    tt = ATT_TILE
    nt = T // tt
    sp = tt // ATT_CH

    def body(q_ref, k_ref, v_ref, g_ref, _, o_ref, y_ref, lse_ref, vt_sc, s_sc, acc_sc, m_sc, l_sc, bias_sc):
        i = pl.program_id(1)

        @pl.when(i == 0)
        def _():
            for c in range(nt):
                vt_sc[c] = v_ref[c * tt:(c + 1) * tt, :].astype(F32).T.astype(BF16)
            krow = lax.broadcasted_iota(jnp.int32, (tt, tt), 0)
            qcol = lax.broadcasted_iota(jnp.int32, (tt, tt), 1)
            bias_sc[...] = jnp.where(krow <= qcol, 0.0, -jnp.inf)

        q = q_ref[0]

        def issue(c, slot):
            s_sc[slot] = lax.dot_general(k_ref[0, pl.ds(pl.multiple_of(c * tt, tt), tt), :], q, _DIMS["nt"],
                                         preferred_element_type=F32)

        def softmax_pv(c, slot, masked):
            s = s_sc[slot]
            if masked:
                s = s + bias_sc[...]
            m = m_sc[...]
            m_new = jnp.maximum(m, jnp.max(s, axis=0, keepdims=True))
            p = jnp.exp2((s - m_new) * EXP2_SCALE)
            a = jnp.exp2((m - m_new) * EXP2_SCALE)
            l_sc[...] = a * l_sc[...] + jnp.sum(p, axis=0, keepdims=True)
            acc_sc[...] = a * acc_sc[...] + jnp.dot(vt_sc[c], p.astype(BF16), preferred_element_type=F32)
            m_sc[...] = m_new

        issue(0, 0)
        m_sc[...] = jnp.full_like(m_sc, -jnp.inf)
        l_sc[...] = jnp.zeros_like(l_sc)
        acc_sc[...] = jnp.zeros_like(acc_sc)

        def pair(t, carry):
            issue(2 * t + 1, 1)
            softmax_pv(2 * t, 0, False)
            issue(2 * t + 2, 0)
            softmax_pv(2 * t + 1, 1, False)
            return carry

        lax.fori_loop(0, i // 2, pair, 0)

        @pl.when(i % 2 == 1)
        def _():
            issue(i, 1)
            softmax_pv(i - 1, 0, False)
            softmax_pv(i, 1, True)

        @pl.when(i % 2 == 0)
        def _():
            softmax_pv(i, 0, True)

        l = l_sc[...]
        o = (acc_sc[...] / l).T
        o_ref[...] = o
        lse = m_sc[...] * ATTN_SCALE + jnp.log(l)
        for r in range(sp):
            lse_ref[0, r] = lse[:, r * ATT_CH:(r + 1) * ATT_CH]
        g = g_ref[...]
        y_ref[...] = (o * (g * _sigmoid(g))).astype(BF16)

    return pl.pallas_call(
        body, name=name,
        out_shape=(jax.ShapeDtypeStruct((T, D_MLA), F32), jax.ShapeDtypeStruct((T, D_MIX), BF16),
                   jax.ShapeDtypeStruct((H, T // ATT_CH, 1, ATT_CH), F32)),
        grid=(H, nt),
        in_specs=[pl.BlockSpec((1, tt, QC), lambda h, i: (h, i, 0)),
                  pl.BlockSpec((1, T, QC), lambda h, i: (h, 0, 0)),
                  pl.BlockSpec((T, V_DIM), lambda h, i: (0, 2 * h + 1)),
                  pl.BlockSpec((tt, V_DIM), lambda h, i: (i, h)),
                  pl.BlockSpec(memory_space=pl.ANY)],
        input_output_aliases={4: 1},
        out_specs=(pl.BlockSpec((tt, V_DIM), lambda h, i: (i, h)),
                   pl.BlockSpec((tt, V_DIM), lambda h, i: (i, h)),
                   pl.BlockSpec((1, sp, 1, ATT_CH), lambda h, i: (h, i, 0, 0))),
        scratch_shapes=[pltpu.VMEM((nt, V_DIM, tt), BF16), pltpu.VMEM((2, tt, tt), F32),
                        pltpu.VMEM((V_DIM, tt), F32), pltpu.VMEM((1, tt), F32), pltpu.VMEM((1, tt), F32),
                        pltpu.VMEM((tt, tt), F32)],
        compiler_params=_params(("parallel", "arbitrary")),
    )(qc, kc, kv, proj, mix)


def _flash_bwd(qc, kc, kv, do, lse, delta, name):
    H, T, _ = qc.shape
    tt = ATT_TILE
    nt = T // tt
    sp = tt // ATT_CH

    def body(q_ref, k_ref, v_ref, do_ref, lse_ref, dl_ref, dq_ref, dkv_ref, dkr_ref, dqt_sc, dk_sc, dv_sc, s_sc,
             dp_sc, kt_sc):
        j = pl.program_id(1)

        @pl.when(j == 0)
        def _():
            dqt_sc[...] = jnp.zeros_like(dqt_sc)

        k = k_ref[0]
        v = v_ref[...]

        def operands(c):
            q0 = pl.multiple_of(c * tt, tt)
            return q_ref[0, pl.ds(q0, tt), :], do_ref[pl.ds(q0, tt), :]

        def stat_row(ref, c):
            return jnp.concatenate([ref[0, sp * c + r] for r in range(sp)], axis=1)

        def early(c, slot):
            q, dov = operands(c)
            s_sc[slot] = lax.dot_general(k, q, _DIMS["nt"], preferred_element_type=F32)
            dp_sc[slot] = lax.dot_general(v, dov, _DIMS["nt"], preferred_element_type=F32)

        def late(c, slot, masked, kt):
            q, dov = operands(c)
            s, dp = s_sc[slot], dp_sc[slot]
            if masked:
                krow = j * tt + lax.broadcasted_iota(jnp.int32, s.shape, 0)
                qcol = c * tt + lax.broadcasted_iota(jnp.int32, s.shape, 1)
                s = jnp.where(krow <= qcol, s, -jnp.inf)
            p = jnp.exp2(s * EXP2_SCALE - stat_row(lse_ref, c) * LOG2E)
            ds = (p * (dp - stat_row(dl_ref, c)) * ATTN_SCALE).astype(BF16)
            dv = jnp.dot(p.astype(BF16), dov, preferred_element_type=F32)
            dk = jnp.dot(ds, q, preferred_element_type=F32)
            if masked:
                dv_sc[slot] = dv
                dk_sc[slot] = dk
            else:
                dv_sc[slot] += dv
                dk_sc[slot] += dk
            dqt_sc[c] += jnp.dot(kt, ds, preferred_element_type=F32)

        n_rest = nt - 1 - j
        odd = n_rest % 2

        def setup(other_slot):
            dk_sc[other_slot] = jnp.zeros((tt, QC), F32)
            dv_sc[other_slot] = jnp.zeros((tt, V_DIM), F32)
            kt_sc[...] = k.astype(F32).T.astype(BF16)

        @pl.when(odd == 0)
        def _():
            early(j, 0)
            early(jnp.minimum(j + 1, nt - 1), 1)
            setup(1)
            late(j, 0, True, kt_sc[...])

        @pl.when(odd == 1)
        def _():
            early(j, 1)
            early(j + 1, 0)
            setup(0)
            kt = kt_sc[...]
            late(j, 1, True, kt)
            early(jnp.minimum(j + 2, nt - 1), 1)
            late(j + 1, 0, False, kt)

        def pair(u, carry):
            a = j + 1 + odd + 2 * u
            kt = kt_sc[...]
            early(a + 1, 0)
            late(a, 1, False, kt)
            early(jnp.minimum(a + 2, nt - 1), 1)
            late(a + 1, 0, False, kt)
            return carry

        lax.fori_loop(0, n_rest // 2, pair, 0)
        dk = dk_sc[0] + dk_sc[1]
        dkv_ref[:, 0:NOPE] = dk[:, 0:NOPE].astype(BF16)
        dkv_ref[:, NOPE:] = (dv_sc[0] + dv_sc[1]).astype(BF16)
        dkr_ref[0] = dk[:, NOPE:]

        @pl.when(j == nt - 1)
        def _():
            for c in range(nt):
                dq_ref[0, c * tt:(c + 1) * tt, :] = dqt_sc[c].T

    head = lambda h, j: (h, 0, 0)
    stat = pl.BlockSpec((1, T // ATT_CH, 1, ATT_CH), lambda h, j: (h, 0, 0, 0))
    return pl.pallas_call(
        body, name=name,
        out_shape=(jax.ShapeDtypeStruct((H, T, QC), F32), jax.ShapeDtypeStruct((T, 2 * D_MLA), BF16),
                   jax.ShapeDtypeStruct((H, T, 128), F32)),
        grid=(H, nt),
        in_specs=[pl.BlockSpec((1, T, QC), head),
                  pl.BlockSpec((1, tt, QC), lambda h, j: (h, j, 0)),
                  pl.BlockSpec((tt, V_DIM), lambda h, j: (j, 2 * h + 1)),
                  pl.BlockSpec((T, V_DIM), lambda h, j: (0, h)),
                  stat, stat],
        out_specs=(pl.BlockSpec((1, T, QC), head),
                   pl.BlockSpec((tt, 256), lambda h, j: (j, h)),
                   pl.BlockSpec((1, tt, 128), lambda h, j: (h, j, 0))),
        scratch_shapes=[pltpu.VMEM((nt, QC, tt), F32), pltpu.VMEM((2, tt, QC), F32), pltpu.VMEM((2, tt, V_DIM), F32),
                        pltpu.VMEM((2, tt, tt), F32), pltpu.VMEM((2, tt, tt), F32), pltpu.VMEM((QC, tt), BF16)],
        compiler_params=_params(("parallel", "arbitrary")),
    )(qc, kc, kv, do, lse, delta)


def _adamw(lands, w, m, v, name, rows, cols=None, first_layer=0, into=None):
    layers, R, C = w.shape
    L = len(lands)
    cols = C if cols is None else cols
    assert R % rows == 0 and C % cols == 0 and first_layer + L <= layers
    nc = C // cols
    nb = (R // rows) * nc
    c1 = 1.0 - ADAM_B1 ** ADAM_STEP
    c2 = 1.0 - ADAM_B2 ** ADAM_STEP

    def body(*refs):
        land_refs = refs[:L]
        w_ref, m_ref, v_ref = refs[L:L + 3]
        g_ref, d_ref, nm_ref, nv_ref, g_sc = refs[-5:]
        for ll in range(L):
            @pl.when(pl.program_id(0) == ll)
            def _(land_ref=land_refs[ll]):
                g = land_ref[0].astype(F32)
                for s in range(1, N_DEV):
                    g = g + land_ref[s].astype(F32)
                g_sc[...] = g

        g = g_sc[...]
        nm = ADAM_B1 * m_ref[0] + (1.0 - ADAM_B1) * g
        nv = ADAM_B2 * v_ref[0] + (1.0 - ADAM_B2) * (g * g)
        g_ref[0] = g
        nm_ref[0] = nm
        nv_ref[0] = nv
        d_ref[0] = -ADAM_LR * ((nm / c1) / (jnp.sqrt(nv / c2) + ADAM_EPS) + ADAM_WD * w_ref[0])

    def land_spec(ll):
        def index(l, i):
            i = jnp.where(l < ll, 0, jnp.where(l > ll, nb - 1, i))
            return (0, i // nc, i % nc)
        return pl.BlockSpec((N_DEV, rows, cols), index)

    blk = pl.BlockSpec((1, rows, cols), lambda l, i: (first_layer + l, i // nc, i % nc))
    out = jax.ShapeDtypeStruct((layers, R, C), F32)
    extra = [] if into is None else list(into)
    return pl.pallas_call(
        body, name=name, out_shape=(out, out, out, out), grid=(L, nb),
        in_specs=[land_spec(ll) for ll in range(L)] + [blk, blk, blk] + [pl.BlockSpec(memory_space=pl.ANY)] * len(extra),
        out_specs=(blk, blk, blk, blk),
        input_output_aliases={L + 3 + i: i for i in range(len(extra))},
        scratch_shapes=[pltpu.VMEM((rows, cols), F32)],
        compiler_params=_params(("arbitrary", "arbitrary")),
    )(*lands, w, m, v, *extra)


def _mesh_pos():
    return lax.axis_index("x"), lax.axis_index("y"), lax.axis_index("c")


def _all_gather(arrays, name):
    n = len(arrays)

    def body(*refs):
        ins, outs = refs[:n], refs[n:2 * n]
        send_sems, recv_sems, local_sems = refs[2 * n:]
        x, y, c = _mesh_pos()
        me, sibling = (x, y, c), (x, y, 1 - c)
        chips = [(1 - x, y), (x, 1 - y), (1 - x, 1 - y)]

        def slot(a, pos):
            px, py, pc = pos
            return outs[a].at[4 * px + 2 * py + pc]

        def copy(a, k, block, to, src=None):
            return pltpu.make_async_remote_copy(
                src_ref=slot(a, block) if src is None else src, dst_ref=slot(a, block),
                send_sem=send_sems.at[a * 7 + k], recv_sem=recv_sems.at[a * 7 + k],
                device_id=to, device_id_type=MESH_ID)

        mine, first, passed = [], [], []
        for a in range(n):
            cp = pltpu.make_async_copy(ins[a], slot(a, me), local_sems.at[a])
            cp.start()
            mine.append(cp)
            cps = [copy(a, 0, me, sibling, src=ins[a])]
            cps += [copy(a, 1 + j, me, (*chip, c), src=ins[a]) for j, chip in enumerate(chips)]
            for cp in cps:
                cp.start()
            first += cps
        for j, chip in enumerate(chips):
            for a in range(n):
                copy(a, 1 + j, (*chip, c), me).wait_recv()
                cp = copy(a, 4 + j, (*chip, c), sibling)
                cp.start()
                passed.append(cp)
        for a in range(n):
            copy(a, 0, sibling, me).wait_recv()
            for j, chip in enumerate(chips):
                copy(a, 4 + j, (*chip, 1 - c), me).wait_recv()
        for cp in first + passed:
            cp.wait_send()
        for cp in mine:
            cp.wait()

    hbm = pl.BlockSpec(memory_space=pltpu.HBM)
    return pl.pallas_call(
        body, name=name,
        out_shape=tuple(jax.ShapeDtypeStruct((N_DEV,) + a.shape, a.dtype) for a in arrays),
        in_specs=[hbm] * n, out_specs=tuple([hbm] * n),
        scratch_shapes=[pltpu.SemaphoreType.DMA((7 * n,)), pltpu.SemaphoreType.DMA((7 * n,)),
                        pltpu.SemaphoreType.DMA((n,))],
    )(*arrays)


def _all_gather_under_ln(arrays, x, g, b, name, tq=512):
    n = len(arrays)
    T, D = x.shape
    nt = T // tq

    def body(*refs):
        x_ref, g_ref, b_ref = refs[:3]
        ins = refs[3:3 + n]
        y_ref, yb_ref = refs[3 + n:5 + n]
        outs = refs[5 + n:5 + 2 * n]
        send_sems, recv_sems, local_sems = refs[5 + 2 * n:]
        i = pl.program_id(0)
        mx, my, mc = _mesh_pos()
        me, sibling = (mx, my, mc), (mx, my, 1 - mc)
        chips = [(1 - mx, my), (mx, 1 - my), (1 - mx, 1 - my)]

        def slot(a, pos):
            px, py, pc = pos
            return outs[a].at[4 * px + 2 * py + pc]

        def copy(a, k, block, to, src=None):
            return pltpu.make_async_remote_copy(
                src_ref=slot(a, block) if src is None else src, dst_ref=slot(a, block),
                send_sem=send_sems.at[a * 7 + k], recv_sem=recv_sems.at[a * 7 + k],
                device_id=to, device_id_type=MESH_ID)

        def own(a):
            return pltpu.make_async_copy(ins[a], slot(a, me), local_sems.at[a])

        def first(a):
            return [copy(a, 0, me, sibling, src=ins[a])] + [
                copy(a, 1 + j, me, (*chip, mc), src=ins[a]) for j, chip in enumerate(chips)]

        @pl.when(i == 0)
        def _():
            for a in range(n):
                own(a).start()
                for cp in first(a):
                    cp.start()

        zv = x_ref[...]
        mu = jnp.mean(zv, axis=1, keepdims=True)
        zc = zv - mu
        var = jnp.mean(zc * zc, axis=1, keepdims=True)
        y = zc * lax.rsqrt(var + LN_EPS) * g_ref[...] + b_ref[...]
        y_ref[...] = y
        yb_ref[...] = y.astype(BF16)

        @pl.when(i == nt - 1)
        def _():
            passed = []
            for j, chip in enumerate(chips):
                for a in range(n):
                    copy(a, 1 + j, (*chip, mc), me).wait_recv()
                    cp = copy(a, 4 + j, (*chip, mc), sibling)
                    cp.start()
                    passed.append(cp)
            for a in range(n):
                copy(a, 0, sibling, me).wait_recv()
                for j, chip in enumerate(chips):
                    copy(a, 4 + j, (*chip, 1 - mc), me).wait_recv()
            for a in range(n):
                for cp in first(a):
                    cp.wait_send()
                own(a).wait()
            for cp in passed:
                cp.wait_send()

    row = pl.BlockSpec((tq, D), lambda i: (i, 0))
    vec = pl.BlockSpec((1, D), lambda i: (0, 0))
    hbm = pl.BlockSpec(memory_space=pltpu.HBM)
    outs = pl.pallas_call(
        body, name=name,
        out_shape=(jax.ShapeDtypeStruct((T, D), F32), jax.ShapeDtypeStruct((T, D), BF16))
        + tuple(jax.ShapeDtypeStruct((N_DEV,) + a.shape, a.dtype) for a in arrays),
        grid=(nt,), in_specs=[row, vec, vec] + [hbm] * n, out_specs=tuple([row, row] + [hbm] * n),
        scratch_shapes=[pltpu.SemaphoreType.DMA((7 * n,)), pltpu.SemaphoreType.DMA((7 * n,)),
                        pltpu.SemaphoreType.DMA((n,))],
        compiler_params=_params(("arbitrary",)),
    )(x, g, b, *arrays)
    return outs[0], outs[1], outs[2:]


_HBM = pl.BlockSpec(memory_space=pltpu.HBM)
_SEM = pl.BlockSpec(memory_space=pltpu.SEMAPHORE)
_EFFECT = pltpu.SideEffectType.DATAFLOW_SIDE_EFFECTING
N_PEERS = N_DEV - 1


def _peer(k):
    x, y, c = _mesh_pos()
    return (1 - x if k & 4 else x, 1 - y if k & 2 else y, 1 - c if k & 1 else c)


def _split_start(srcs, scatter, after, name):
    n = len(srcs)
    zones = [jax.ShapeDtypeStruct(s.shape if scatter else ((N_DEV,) + s.shape), s.dtype) for s in srcs]

    def body(*refs):
        src, zone = refs[:n], refs[n:2 * n]
        outs = refs[2 * n + 1:]
        send, recv, token = outs[:n], outs[n:2 * n], outs[4 * n]
        x, y, c = _mesh_pos()
        my_idx = 4 * x + 2 * y + c
        for a in range(n):
            pltpu.make_async_copy(src[a].at[my_idx] if scatter else src[a],
                                  zone[a].at[N_PEERS] if scatter else zone[a].at[my_idx], recv[a]).start()
            for k in range(1, N_DEV):
                px, py, pc = _peer(k)
                pltpu.make_async_remote_copy(
                    src_ref=src[a].at[4 * px + 2 * py + pc] if scatter else src[a],
                    dst_ref=zone[a].at[k - 1] if scatter else zone[a].at[my_idx],
                    send_sem=send[a], recv_sem=recv[a], device_id=(px, py, pc), device_id_type=MESH_ID).start()
        token[...] = jnp.zeros_like(token)

    hbm = lambda a: pltpu.with_memory_space_constraint(a, pltpu.HBM)
    outs = pl.pallas_call(
        body, name=name,
        out_shape=tuple([pltpu.SemaphoreType.DMA(())] * (2 * n)
                        + [pltpu.HBM(s.shape, s.dtype) for s in srcs]
                        + [pltpu.HBM(z.shape, z.dtype) for z in zones]
                        + [jax.ShapeDtypeStruct((8, 128), F32)]),
        in_specs=[_HBM] * (2 * n) + [pl.BlockSpec(memory_space=pl.ANY)],
        out_specs=tuple([_SEM] * (2 * n) + [_HBM] * (2 * n) + [pl.BlockSpec(memory_space=pltpu.VMEM)]),
        input_output_aliases={**{a: 2 * n + a for a in range(n)}, **{n + a: 3 * n + a for a in range(n)}},
        compiler_params=pltpu.CompilerParams(has_side_effects=_EFFECT),
    )(*[hbm(s) for s in srcs], *[hbm(lax.empty(z.shape, z.dtype)) for z in zones], after)
    return outs[:n], outs[n:2 * n], outs[2 * n:3 * n], outs[3 * n:4 * n], outs[4 * n]


def _split_wait(send, recv, srcs, zones, after, name):
    n = len(srcs)

    def body(*refs):
        zone = refs[n:2 * n]
        send_sems, recv_sems = refs[2 * n:3 * n], refs[3 * n:4 * n]
        x, y, c = _mesh_pos()
        for a in range(n):
            seven = zone[a].at[pl.ds(0, N_PEERS)]
            pltpu.make_async_remote_copy(src_ref=seven, dst_ref=seven, send_sem=send_sems[a], recv_sem=recv_sems[a],
                                         device_id=(x, y, 1 - c), device_id_type=MESH_ID).wait_send()
            pltpu.make_async_remote_copy(src_ref=zone[a], dst_ref=zone[a], send_sem=send_sems[a],
                                         recv_sem=recv_sems[a], device_id=(x, y, 1 - c),
                                         device_id_type=MESH_ID).wait_recv()

    outs = pl.pallas_call(
        body, name=name,
        out_shape=tuple([pltpu.HBM(s.shape, s.dtype) for s in srcs] + [pltpu.HBM(z.shape, z.dtype) for z in zones]),
        in_specs=[_HBM] * (2 * n) + [_SEM] * (2 * n) + [pl.BlockSpec(memory_space=pl.ANY)],
        out_specs=tuple([_HBM] * (2 * n)),
        input_output_aliases={a: a for a in range(2 * n)},
        compiler_params=pltpu.CompilerParams(has_side_effects=_EFFECT),
    )(*srcs, *zones, *send, *recv, after)
    return outs[:n], outs[n:]


def _cat_blocks(g, axis):
    return jnp.concatenate([g[d] for d in range(N_DEV)], axis=axis)


N_LATENT = Q_LORA + KV_LORA + ROPE
W_SHARD = D_IN_PROJ // N_DEV


def _ref_cols(lo, hi):
    out = []
    if lo < N_LATENT:
        out.append((N_GATED + lo, N_GATED + min(hi, N_LATENT)))
    if hi > N_LATENT:
        out.append((max(lo, N_LATENT) - N_LATENT, hi - N_LATENT))
    return out


def _permute_w_in_t(blocks):
    pieces = []
    for lo, hi in ((N_LATENT, D_IN_PROJ), (0, N_LATENT)):
        for d in range(N_DEV):
            a, b = max(lo, d * W_SHARD), min(hi, (d + 1) * W_SHARD)
            if a < b:
                pieces.append(blocks[d][a - d * W_SHARD:b - d * W_SHARD])
    pieces.append(jnp.zeros((NPP - D_IN_PROJ, blocks.shape[2]), blocks.dtype))
    return jnp.concatenate(pieces, axis=0)


def _split_w_in_t(w):
    slabs = []
    for d in range(N_DEV):
        parts = [w[a:b] for a, b in _ref_cols(d * W_SHARD, (d + 1) * W_SHARD)]
        slabs.append(parts[0] if len(parts) == 1 else jnp.concatenate(parts, axis=0))
    return jnp.stack(slabs)


def _permute_w_uq(w):
    w3 = w.reshape(w.shape[0], N_HEADS, NOPE + ROPE)
    return jnp.concatenate([w3[:, :, :NOPE].reshape(w.shape[0], -1), w3[:, :, NOPE:].reshape(w.shape[0], -1)], axis=1)


def _unpermute_w_uq(w):
    nope = w[:, :N_HEADS * NOPE].reshape(w.shape[0], N_HEADS, NOPE)
    rope = w[:, N_HEADS * NOPE:].reshape(w.shape[0], N_HEADS, ROPE)
    return jnp.concatenate([nope, rope], axis=2).reshape(w.shape[0], -1)


_SMALL_EMB = (("emb_ln_g", 16), ("emb_ln_b", 16))
_SMALL_LAYER = (("q_norm_g", 8), ("kv_norm_g", 8), ("w_pool", 1024), ("pool_scale", 8), ("b_out", 32),
                ("ln_g", 32), ("ln_b", 32))
_SMALL = _SMALL_EMB + _SMALL_LAYER
CONV_ROWS = DEPTH * CONV_WIDTH * D_CONV // 128


def _pack_small(d, entries=_SMALL):
    parts = []
    for name, rows in entries:
        flat = d[name].reshape(-1)
        flat = jnp.pad(flat, (0, rows * 128 - flat.shape[0]))
        parts.append(flat.reshape(rows, 128))
    return jnp.concatenate(parts, axis=0)


def _unpack_small(packed, shapes):
    out, r0 = {}, 0
    for name, rows in _SMALL:
        size = 1
        for s in shapes[name]:
            size *= s
        out[name] = packed[r0:r0 + rows].reshape(-1)[:size].reshape(shapes[name])
        r0 += rows
    return out


def _rope_tables(positions):
    half = ROPE // 2
    inv_freq = ROPE_THETA ** (-jnp.arange(half, dtype=F32) / half)
    ang = positions.astype(F32)[:, None] * inv_freq
    cos, sin = jnp.cos(ang), jnp.sin(ang)
    return jnp.concatenate([cos, cos, cos, cos], axis=1), jnp.concatenate([-sin, sin, -sin, sin], axis=1)


def _local_step(x, positions, target, emb_g, emb_b, layer_weights, layer_weights_rest, on_sharded_grads,
                on_layer_grads=None, first_after=None, embedded=None):
    cos_t, sin_t = _rope_tables(positions)
    h, hb = _ln_fwd(x, emb_g, emb_b, "emb_ln_fwd") if embedded is None else embedded
    saved = []
    for l in range(DEPTH):
        W = layer_weights(l, h)
        proj = _mm(hb, W["w_in_t"], "nt", F32, "proj_fwd", after=first_after if l == 0 else None)
        qn, kvn, pooled, cv, mix = _mix_fwd(proj, W["q_norm_g"], W["kv_norm_g"], W["w_pool"], W["pool_scale"],
                                            W["conv_w"], "mix_fwd")
        rest, token = layer_weights_rest(l, proj)
        W = {**W, **rest}
        kv, qc, kc = _up_rope_fwd(qn, kvn, W["w_uq"], W["w_ukv"], proj, cos_t, sin_t, "up_rope_fwd", after=token)
        o, mix, lse = _flash_fwd(qc, kc, kv, proj, mix, "flash_fwd")
        z = _mm(mix, W["w_out"], "nn", F32, "out_fwd", res=h, bias=W["b_out"], alpha=ALPHA)
        saved.append((W, hb, proj, qn, kvn, pooled, cv, kv, qc, kc, o, lse, mix, z))
        h, hb = _ln_fwd(z, W["ln_g"], W["ln_b"], "ln_fwd")
    dh, sq = h, None

    grads = {k: [None] * DEPTH for k in ("q_norm_g", "kv_norm_g", "w_pool", "pool_scale", "conv_w", "b_out", "ln_g",
                                         "ln_b")}
    for l in reversed(range(DEPTH)):
        W, hb_in, proj, qn, kvn, pooled, cv, kv, qc, kc, o, lse, mix, z = saved[l]
        sharded = {}
        if l == DEPTH - 1:
            dz, dzb, grads["b_out"][l], grads["ln_g"][l], grads["ln_b"][l], sq = _ln_bwd(
                dh, z, W["ln_g"], "ln_bwd_loss", target=target)
        else:
            dz, dzb, grads["b_out"][l], grads["ln_g"][l], grads["ln_b"][l] = _ln_bwd(dh, z, W["ln_g"], "ln_bwd")
        dmix = _mm(dzb, W["w_out"], "nt", F32, "out_bwd_x")
        sharded["w_out"] = _mm(mix, dzb, "tn", GRAD_XFER, "out_bwd_w", tk=4096)
        do, delta, dproj, grads["w_pool"][l], grads["pool_scale"][l], grads["conv_w"][l] = _mix_bwd(
            dmix, proj, o, pooled, cv, W["w_pool"], W["pool_scale"], W["conv_w"], "mix_bwd")
        dqc, dkv, dkr = _flash_bwd(qc, kc, kv, do, lse, delta, "flash_bwd")
        dq, dkrope = _rope_bwd(dqc, dkr, cos_t, sin_t, "rope_bwd")
        dproj, grads["q_norm_g"][l], grads["kv_norm_g"][l], sharded["w_uq"], sharded["w_ukv"] = _up_rms_bwd(
            proj, qn, kvn, dq, dkv, W["w_uq"], W["w_ukv"], dkrope, dproj, W["q_norm_g"], W["kv_norm_g"], "up_rms_bwd")
        token = on_sharded_grads(l, sharded)
        if l == 0 and on_layer_grads is not None:
            token = on_layer_grads(grads, token)
        d_w_in_t = _mm(dproj, hb_in, "tn", GRAD_XFER, "proj_bwd_w", tk=4096, after=token)
        token = on_sharded_grads(l, {"w_in": d_w_in_t})
        dh = _mm(dproj, W["w_in_t"], "nn", F32, "proj_bwd_x", res=dz, alpha=ALPHA, tk=2560, after=token)
    grad_x, grads["emb_ln_g"], grads["emb_ln_b"] = _ln_bwd(dh, x, emb_g, "emb_ln_bwd", for_matmul=False)
    return sq, grad_x, grads


def kernel(x, positions, emb_ln_g, emb_ln_b, w_in, q_norm_g, kv_norm_g, w_uq, w_ukv, w_pool, pool_scale, conv_w, w_out, b_out, ln_g, ln_b, loss_target, m_emb_ln_g, m_emb_ln_b, m_w_in, m_q_norm_g, m_kv_norm_g, m_w_uq, m_w_ukv, m_w_pool, m_pool_scale, m_conv_w, m_w_out, m_b_out, m_ln_g, m_ln_b, v_emb_ln_g, v_emb_ln_b, v_w_in, v_q_norm_g, v_kv_norm_g, v_w_uq, v_w_ukv, v_w_pool, v_pool_scale, v_conv_w, v_w_out, v_b_out, v_ln_g, v_ln_b):
    weights = dict(emb_ln_g=emb_ln_g, emb_ln_b=emb_ln_b, w_in=w_in, q_norm_g=q_norm_g, kv_norm_g=kv_norm_g,
                   w_uq=w_uq, w_ukv=w_ukv, w_pool=w_pool, pool_scale=pool_scale, conv_w=conv_w, w_out=w_out,
                   b_out=b_out, ln_g=ln_g, ln_b=ln_b)
    mom1 = dict(emb_ln_g=m_emb_ln_g, emb_ln_b=m_emb_ln_b, w_in=m_w_in, q_norm_g=m_q_norm_g, kv_norm_g=m_kv_norm_g,
                w_uq=m_w_uq, w_ukv=m_w_ukv, w_pool=m_w_pool, pool_scale=m_pool_scale, conv_w=m_conv_w,
                w_out=m_w_out, b_out=m_b_out, ln_g=m_ln_g, ln_b=m_ln_b)
    mom2 = dict(emb_ln_g=v_emb_ln_g, emb_ln_b=v_emb_ln_b, w_in=v_w_in, q_norm_g=v_q_norm_g, kv_norm_g=v_kv_norm_g,
                w_uq=v_w_uq, w_ukv=v_w_ukv, w_pool=v_w_pool, pool_scale=v_pool_scale, conv_w=v_conv_w,
                w_out=v_w_out, b_out=v_b_out, ln_g=v_ln_g, ln_b=v_ln_b)

    big = ("w_in", "w_uq", "w_ukv", "w_out")

    conv_pad = jnp.zeros((8, 128), F32).at[0:DEPTH * CONV_WIDTH, 0:64].set(conv_w.reshape(DEPTH * CONV_WIDTH, 64))
    t12 = lambda a: jnp.swapaxes(a, 1, 2)
    shard = lambda k, l: (t12(weights[k])[l] if k == "w_in" else weights[k][l]).astype(BF16)
    h0, h0b, (w_in0, conv_all) = _all_gather_under_ln(
        [shard("w_in", 0), conv_pad], x[0], emb_ln_g.reshape(1, -1), emb_ln_b.reshape(1, -1), "w_in0_all_gather_emb_ln")
    rest0 = _split_start([shard(k, 0) for k in big[1:]], False, w_in0, "weights0_rest_start")
    conv_full = _cat_blocks(conv_all[:, 0:DEPTH * CONV_WIDTH, 0:64], 1).reshape(DEPTH, CONV_WIDTH, D_CONV)
    conv_full = jnp.pad(conv_full, ((0, 0), (0, 8 - CONV_WIDTH), (0, 0)))
    fetched = {}

    def layer_weights(l, ready):
        if l == 0:
            w_in_blocks = w_in0
        else:
            fetched[1] = _split_wait(*fetched["w1"][:4], ready, "weights1_wait")[1]
            w_in_blocks = fetched[1][0]
        return dict(
            w_in_t=_permute_w_in_t(w_in_blocks), conv_w=conv_full[l],
            q_norm_g=q_norm_g[l].reshape(1, -1), kv_norm_g=kv_norm_g[l].reshape(1, -1),
            w_pool=w_pool[l].astype(BF16), pool_scale=pool_scale[l].reshape(1, -1), b_out=b_out[l].reshape(1, -1),
            ln_g=ln_g[l].reshape(1, -1), ln_b=ln_b[l].reshape(1, -1))

    def layer_weights_rest(l, ready):
        token = None
        if l == 0:
            blocks = _split_wait(*rest0[:4], ready, "weights0_rest_wait")[1]
            fetched["w1"] = _split_start([shard(k, 1) for k in big], False, blocks[0], "weights1_start")
            token = fetched["w1"][4]
        else:
            blocks = fetched[1][1:]
        return dict(w_uq=_permute_w_uq(_cat_blocks(blocks[0], 1)), w_ukv=_cat_blocks(blocks[1], 1),
                    w_out=blocks[2].reshape(D_MIX, D_MODEL)), token

    by_dest = dict(
        w_in=_split_w_in_t,
        w_uq=lambda g: _unpermute_w_uq(g).reshape(Q_LORA, N_DEV, -1).transpose(1, 0, 2),
        w_ukv=lambda g: g.reshape(KV_LORA, N_DEV, -1).transpose(1, 0, 2),
        w_out=lambda g: g.reshape(N_DEV, -1, D_MODEL))
    in_flight = []

    def on_sharded_grads(l, g):
        names = [k for k in big if k in g]
        srcs = [by_dest[k](g[k]) for k in names]
        started = _split_start(srcs, True, srcs[0], "grads%d_%s_start" % (l, names[0]))
        in_flight.append((l, names, started[:4]))
        return started[4]

    small_in_flight = []

    def on_layer_grads(g, token):
        stacked = {k: jnp.stack(g[k]) for k, _ in _SMALL_LAYER}
        conv = jnp.stack([g["conv_w"][l][0:CONV_WIDTH] for l in range(DEPTH)]).reshape(CONV_ROWS, 128)
        packed = jnp.concatenate([_pack_small(stacked, _SMALL_LAYER), conv], axis=0)
        started = _split_start([packed], False, token, "layer_grads_start")
        small_in_flight.append(started[:4])
        return started[4]

    sq, grad_x, G = _local_step(x[0], positions[0], loss_target[0], emb_ln_g.reshape(1, -1),
                                emb_ln_b.reshape(1, -1), layer_weights, layer_weights_rest, on_sharded_grads,
                                on_layer_grads, first_after=rest0[4], embedded=(h0, h0b))
    loss = lax.psum(sq[0, 0] * (0.5 / D_MODEL), ("x", "y", "c"))

    res = {}
    landed = {}
    for l, names, started in in_flight:
        zones = _split_wait(*started, grad_x, "grads%d_%s_wait" % (l, names[0]))[1]
        for k, zone in zip(names, zones):
            landed[k, l] = zone
    w_in_res = None
    for l in reversed(range(DEPTH)):
        w_in_res = _adamw([landed["w_in", l]], t12(w_in), t12(m_w_in), t12(v_w_in), "adamw_w_in_%d" % l, W_SHARD, 512,
                          first_layer=l, into=w_in_res)
    res["w_in"] = tuple(t12(o) for o in w_in_res)
    for name, rows in (("w_uq", 256), ("w_ukv", 256), ("w_out", 128)):
        res[name] = _adamw([landed[name, l] for l in range(DEPTH)], weights[name], mom1[name], mom2[name],
                           "adamw_" + name, rows)

    layer_zone = _split_wait(*small_in_flight[0], grad_x, "layer_grads_wait")[1][0]
    emb_zone = _all_gather([_pack_small(G, _SMALL_EMB)], "emb_grads_all_gather")[0]
    n_layer_rows = sum(r for _, r in _SMALL_LAYER)
    l_small = jnp.concatenate([emb_zone, layer_zone[:, 0:n_layer_rows]], axis=1)
    my_idx = 4 * lax.axis_index("x") + 2 * lax.axis_index("y") + lax.axis_index("c")
    conv_all_grads = layer_zone[:, n_layer_rows:].reshape(N_DEV, DEPTH * CONV_WIDTH, D_CONV)
    l_conv = lax.dynamic_slice_in_dim(conv_all_grads, my_idx * 64, 64, axis=2)
    l_conv = jnp.zeros((N_DEV, 8, 128), F32).at[:, 0:DEPTH * CONV_WIDTH, 0:64].set(l_conv)
    conv_shard = lambda a: jnp.zeros((8, 128), F32).at[0:DEPTH * CONV_WIDTH, 0:64].set(a.reshape(-1, 64))
    conv_res = _adamw([l_conv], conv_shard(conv_w)[None], conv_shard(m_conv_w)[None], conv_shard(v_conv_w)[None],
                      "adamw_conv_w", 8)
    res["conv_w"] = tuple(o[0, 0:DEPTH * CONV_WIDTH, 0:64].reshape(DEPTH, CONV_WIDTH, 64) for o in conv_res)
    small_res = _adamw([l_small], _pack_small(weights)[None], _pack_small(mom1)[None], _pack_small(mom2)[None],
                       "adamw_small", 392)
    shapes = {k: weights[k].shape for k, _ in _SMALL}
    unpacked = [_unpack_small(o[0], shapes) for o in small_res]
    for k, _ in _SMALL:
        res[k] = tuple(u[k] for u in unpacked)

    order = ("emb_ln_g", "emb_ln_b", "w_in", "q_norm_g", "kv_norm_g", "w_uq", "w_ukv", "w_pool", "pool_scale",
             "conv_w", "w_out", "b_out", "ln_g", "ln_b")
    return (loss, grad_x[None], *[res[k][0] for k in order], *[res[k][1] for k in order],
            *[res[k][2] for k in order], *[res[k][3] for k in order])
```

```python
import jax
import jax.numpy as jnp
from jax import lax
from jax.experimental import pallas as pl
from jax.experimental.pallas import tpu as pltpu

F32 = jnp.float32
BF16 = jnp.bfloat16

N_DEV = 8
D_MODEL = 2048
DEPTH = 2
N_HEADS = 8
NOPE = 128
ROPE = 64
V_DIM = 128
Q_LORA = 512
KV_LORA = 256
D_MLA = N_HEADS * V_DIM
D_POOL = 512
D_CONV = 512
POOL_WINDOWS = (2, 4, 8, 16)
POOL_GROUP = 128
CONV_WIDTH = 3
D_MIX = D_MLA + D_POOL + D_CONV
D_IN_PROJ = 4928
ROPE_THETA = 10000.0
LN_EPS = 1e-5
RMS_EPS = 1e-6
ALPHA = (2 * DEPTH) ** 0.25
ATTN_SCALE = (NOPE + ROPE) ** -0.5
ADAM_LR = 0.001
ADAM_B1 = 0.9
ADAM_B2 = 0.999
ADAM_EPS = 1e-08
ADAM_WD = 0.01
ADAM_STEP = 10

O_GMLA, O_PIN, O_GPOOL, O_CH, O_CB, O_CC, O_GCONV, O_QLAT, O_KVLAT, O_KROPE = (
    0, 1024, 1536, 2048, 2560, 3072, 3584, 4096, 4608, 4864)
NPP = 5120
N_GATED = O_QLAT
QC = NOPE + 2 * ROPE
HALO = 16
ATT_TILE = 512
ATT_CH = 256
LOG2E = 1.4426950408889634
EXP2_SCALE = ATTN_SCALE * LOG2E

GRAD_XFER = BF16
VMEM_LIMIT = 48 * 1024 * 1024
ATT_BWD_VMEM_LIMIT = 58 * 1024 * 1024
MESH_ID = pl.DeviceIdType.MESH


def _params(sem=None):
    return pltpu.CompilerParams(dimension_semantics=sem, vmem_limit_bytes=VMEM_LIMIT)


def _sigmoid(x):
    return 1.0 / (1.0 + jnp.exp(-x))


def _tile(dim, target):
    if dim <= target:
        return dim
    t = target - target % 128
    while dim % t:
        t -= 128
    return t


_DIMS = {"nn": (((1,), (0,)), ((), ())), "nt": (((1,), (1,)), ((), ())), "tn": (((0,), (0,)), ((), ()))}


def _mm(a, b, mode, out_dtype, name, res=None, bias=None, alpha=1.0, tm=1024, tn=1024, tk=2048, after=None):
    if mode == "nn":
        (M, K), (K2, N) = a.shape, b.shape
    elif mode == "nt":
        (M, K), (N, K2) = a.shape, b.shape
    else:
        (K, M), (K2, N) = a.shape, b.shape
    assert K == K2
    tm, tn, tk = _tile(M, tm), _tile(N, tn), _tile(K, tk)
    nk = K // tk
    has_res, has_bias = res is not None, bias is not None

    def body(*refs):
        a_ref, b_ref = refs[0], refs[1]
        pos = 2
        res_ref = bias_ref = None
        if has_res:
            res_ref = refs[pos]
            pos += 1
        if has_bias:
            bias_ref = refs[pos]
            pos += 1
        def finish(r, o_ref):
            if has_bias:
                r = r + bias_ref[...]
            if has_res:
                r = alpha * res_ref[...] + r
            o_ref[...] = r.astype(out_dtype)

        part = lax.dot_general(a_ref[...].astype(BF16), b_ref[...].astype(BF16), _DIMS[mode],
                               preferred_element_type=F32)
        if nk == 1:
            finish(part, refs[-1])
            return
        o_ref, acc_ref = refs[-2], refs[-1]
        k = pl.program_id(2)

        @pl.when(k == 0)
        def _():
            acc_ref[...] = part

        @pl.when(jnp.logical_and(k > 0, k < nk - 1))
        def _():
            acc_ref[...] += part

        @pl.when(k == nk - 1)
        def _():
            finish(acc_ref[...] + part, o_ref)

    if mode == "nn":
        in_specs = [pl.BlockSpec((tm, tk), lambda i, j, k: (i, k)), pl.BlockSpec((tk, tn), lambda i, j, k: (k, j))]
    elif mode == "nt":
        in_specs = [pl.BlockSpec((tm, tk), lambda i, j, k: (i, k)), pl.BlockSpec((tn, tk), lambda i, j, k: (j, k))]
    else:
        in_specs = [pl.BlockSpec((tk, tm), lambda i, j, k: (k, i)), pl.BlockSpec((tk, tn), lambda i, j, k: (k, j))]
    args = [a, b]
    if has_res:
        in_specs.append(pl.BlockSpec((tm, tn), lambda i, j, k: (i, j)))
        args.append(res)
    if has_bias:
        in_specs.append(pl.BlockSpec((1, tn), lambda i, j, k: (0, j)))
        args.append(bias)
    if after is not None:
        in_specs.append(pl.BlockSpec((8, 128), lambda i, j, k: (0, 0)))
        args.append(after)
    return pl.pallas_call(
        body, name=name,
        out_shape=jax.ShapeDtypeStruct((M, N), out_dtype),
        grid=(M // tm, N // tn, nk),
        in_specs=in_specs,
        out_specs=pl.BlockSpec((tm, tn), lambda i, j, k: (i, j)),
        scratch_shapes=[pltpu.VMEM((tm, tn), F32)] if nk > 1 else [],
        compiler_params=_params(("parallel", "parallel", "arbitrary")),
    )(*args)


def _ln_fwd(z, g, b, name, tq=512):
    T, D = z.shape

    def body(z_ref, g_ref, b_ref, y_ref, yb_ref):
        zv = z_ref[...]
        mu = jnp.mean(zv, axis=1, keepdims=True)
        zc = zv - mu
        var = jnp.mean(zc * zc, axis=1, keepdims=True)
        y = zc * lax.rsqrt(var + LN_EPS) * g_ref[...] + b_ref[...]
        y_ref[...] = y
        yb_ref[...] = y.astype(BF16)

    row = pl.BlockSpec((tq, D), lambda i: (i, 0))
    vec = pl.BlockSpec((1, D), lambda i: (0, 0))
    return pl.pallas_call(
        body, name=name,
        out_shape=(jax.ShapeDtypeStruct((T, D), F32), jax.ShapeDtypeStruct((T, D), BF16)),
        grid=(T // tq,), in_specs=[row, vec, vec], out_specs=(row, row),
        compiler_params=_params(("parallel",)),
    )(z, g, b)


def _ln_bwd(dy, z, g, name, tq=512, target=None, for_matmul=True):
    T, D = z.shape
    with_loss = target is not None

    def body(*refs):
        dy_ref, z_ref, g_ref = refs[:3]
        outs = list(refs[4 if with_loss else 3:])
        dz_ref = outs.pop(0)
        dzb_ref, ds_ref = (outs.pop(0), outs.pop(0)) if for_matmul else (None, None)
        dg_ref, db_ref = outs.pop(0), outs.pop(0)
        sq_ref = outs.pop(0) if with_loss else None

        @pl.when(pl.program_id(0) == 0)
        def _():
            for ref in (dg_ref, db_ref, ds_ref, sq_ref):
                if ref is not None:
                    ref[...] = jnp.zeros_like(ref)

        zv, dyv = z_ref[...], dy_ref[...]
        if with_loss:
            err = dyv - refs[3][...]
            sq_ref[...] += jnp.sum(err * err)
            dyv = err * (1.0 / D)
        mu = jnp.mean(zv, axis=1, keepdims=True)
        zc = zv - mu
        var = jnp.mean(zc * zc, axis=1, keepdims=True)
        rstd = lax.rsqrt(var + LN_EPS)
        xh = zc * rstd
        u = dyv * g_ref[...]
        dz = rstd * (u - jnp.mean(u, axis=1, keepdims=True) - xh * jnp.mean(u * xh, axis=1, keepdims=True))
        dz_ref[...] = dz
        dg_ref[...] += jnp.sum(dyv * xh, axis=0, keepdims=True)
        db_ref[...] += jnp.sum(dyv, axis=0, keepdims=True)
        if for_matmul:
            dzb_ref[...] = dz.astype(BF16)
            ds_ref[...] += jnp.sum(dz, axis=0, keepdims=True)

    row = pl.BlockSpec((tq, D), lambda i: (i, 0))
    vec = pl.BlockSpec((1, D), lambda i: (0, 0))
    vshape = jax.ShapeDtypeStruct((1, D), F32)
    out_shape, out_specs = [jax.ShapeDtypeStruct((T, D), F32)], [row]
    if for_matmul:
        out_shape += [jax.ShapeDtypeStruct((T, D), BF16), vshape]
        out_specs += [row, vec]
    out_shape += [vshape, vshape]
    out_specs += [vec, vec]
    if with_loss:
        out_shape.append(jax.ShapeDtypeStruct((8, 128), F32))
        out_specs.append(pl.BlockSpec((8, 128), lambda i: (0, 0)))
    return pl.pallas_call(
        body, name=name, out_shape=tuple(out_shape), grid=(T // tq,),
        in_specs=[row, row, vec] + ([row] if with_loss else []), out_specs=tuple(out_specs),
        compiler_params=_params(("arbitrary",)),
    )(dy, z, g, *([target] if with_loss else []))


def _pblock(tq, width, offset):
    assert offset % width == 0
    blk = offset // width
    return pl.BlockSpec((tq, width), lambda i: (i, blk))


def _mix_fwd(proj, q_g, kv_g, w_pool, pool_scale, conv_w, name, tq=256):
    T = proj.shape[0]

    def body(ql_ref, kvl_ref, pin_ref, gp_ref, ch_ref, cb_ref, cc_ref, gc_ref, qg_ref, kvg_ref, wp_ref, ps_ref,
             cw_ref, qn_ref, kvn_ref, pooled_ref, cv_ref, ypc_ref, extp, extu):
        i = pl.program_id(0)
        for x_ref, g_ref, o_ref in ((ql_ref, qg_ref, qn_ref), (kvl_ref, kvg_ref, kvn_ref)):
            x = x_ref[...]
            r = lax.rsqrt(jnp.mean(x * x, axis=1, keepdims=True) + RMS_EPS)
            o_ref[...] = (x * r * g_ref[...]).astype(BF16)

        @pl.when(i == 0)
        def _():
            extp[0:HALO, :] = jnp.zeros((HALO, D_POOL), F32)
            extu[0:HALO, :] = jnp.zeros((HALO, D_CONV), F32)

        @pl.when(i > 0)
        def _():
            extp[0:HALO, :] = extp[tq:tq + HALO, :]
            extu[0:HALO, :] = extu[tq:tq + HALO, :]

        t1 = (i * tq + lax.broadcasted_iota(jnp.int32, (tq, 1), 0) + 1).astype(F32)
        for g, w in enumerate(POOL_WINDOWS):
            cols = slice(g * POOL_GROUP, (g + 1) * POOL_GROUP)
            pin = pin_ref[:, cols]
            extp[HALO:, cols] = pin
            s = extp[:, cols]
            k = 1
            while k < w:
                s = s + pltpu.roll(s, k, 0)
                k *= 2
            mean = s[HALO:, :] / jnp.minimum(t1, float(w))
            pooled = (mean - pin).astype(BF16)
            pooled_ref[:, cols] = pooled
            r = jnp.dot(pooled, wp_ref[g], preferred_element_type=F32)
            gp = gp_ref[:, cols]
            ypc_ref[:, cols] = (r * ps_ref[:, cols] * (gp * _sigmoid(gp))).astype(BF16)
        for g in range(D_CONV // 128):
            cols = slice(g * 128, (g + 1) * 128)
            u = cc_ref[:, cols] * ch_ref[:, cols]
            extu[HALO:, cols] = u
            eu = extu[:, cols]
            u1 = pltpu.roll(eu, 1, 0)[HALO:, :]
            u2 = pltpu.roll(eu, 2, 0)[HALO:, :]
            cv = cw_ref[0:1, cols] * u2 + cw_ref[1:2, cols] * u1 + cw_ref[2:3, cols] * u
            cv_ref[:, cols] = cv
            gc = gc_ref[:, cols]
            ypc_ref[:, D_POOL + g * 128:D_POOL + (g + 1) * 128] = (
                cb_ref[:, cols] * cv * (gc * _sigmoid(gc))).astype(BF16)

    full = lambda shape: pl.BlockSpec(shape, lambda i: (0,) * len(shape))
    row = lambda w: pl.BlockSpec((tq, w), lambda i: (i, 0))
    return pl.pallas_call(
        body, name=name,
        out_shape=(jax.ShapeDtypeStruct((T, Q_LORA), BF16), jax.ShapeDtypeStruct((T, KV_LORA), BF16),
                   jax.ShapeDtypeStruct((T, D_POOL), BF16), jax.ShapeDtypeStruct((T, D_CONV), F32),
                   jax.ShapeDtypeStruct((T, D_MIX), BF16)),
        grid=(T // tq,),
        in_specs=[_pblock(tq, Q_LORA, O_QLAT), _pblock(tq, KV_LORA, O_KVLAT), _pblock(tq, 512, O_PIN),
                  _pblock(tq, 512, O_GPOOL), _pblock(tq, 512, O_CH), _pblock(tq, 512, O_CB), _pblock(tq, 512, O_CC),
                  _pblock(tq, 512, O_GCONV), full((1, Q_LORA)), full((1, KV_LORA)), full((4, 128, 128)),
                  full((1, D_POOL)), full((8, D_CONV))],
        out_specs=(row(Q_LORA), row(KV_LORA), row(D_POOL), row(D_CONV),
                   pl.BlockSpec((tq, D_POOL + D_CONV), lambda i: (i, D_MLA // (D_POOL + D_CONV)))),
        scratch_shapes=[pltpu.VMEM((tq + HALO, D_POOL), F32), pltpu.VMEM((tq + HALO, D_CONV), F32)],
        compiler_params=_params(("arbitrary",)),
    )(proj, proj, proj, proj, proj, proj, proj, proj, q_g, kv_g, w_pool, pool_scale, conv_w)


def _mix_bwd(dmix, proj, o, pooled, cv, w_pool, pool_scale, conv_w, name, tq=ATT_CH):
    T = proj.shape[0]
    nt = T // tq
    n_ext = tq + HALO

    def body(dym_ref, dyp_ref, dyc_ref, gm_ref, gp_ref, ch_ref, cb_ref, cc_ref, gc_ref, o_ref, pooled_ref, cv_ref,
             wp_ref, ps_ref, cw_ref, do_ref, delta_ref, dg_ref, dwp_ref, dps_ref, dcw_ref, exte, extd):
        i = pl.program_id(0)
        tile = nt - 1 - i

        @pl.when(i == 0)
        def _():
            dwp_ref[...] = jnp.zeros_like(dwp_ref)
            dps_ref[...] = jnp.zeros_like(dps_ref)
            dcw_ref[...] = jnp.zeros_like(dcw_ref)
            exte[tq:, :] = jnp.zeros((HALO, D_POOL), F32)
            extd[tq:, :] = jnp.zeros((HALO, D_CONV), F32)

        @pl.when(i > 0)
        def _():
            exte[tq:, :] = exte[0:HALO, :]
            extd[tq:, :] = extd[0:HALO, :]

        ones = jnp.ones((8, V_DIM), F32)
        for h in range(N_HEADS):
            cols = slice(h * V_DIM, (h + 1) * V_DIM)
            gm = gm_ref[:, cols]
            sig = _sigmoid(gm)
            dym = dym_ref[:, cols]
            ov = o_ref[:, cols]
            do = dym * (gm * sig)
            do_ref[:, cols] = do.astype(BF16)
            rows = lax.dot_general(ones, do * ov, _DIMS["nt"], precision=lax.Precision.HIGHEST,
                                   preferred_element_type=F32)
            delta_ref[h, 0] = rows[0:1, :]
            dg_ref[:, O_GMLA + h * V_DIM:O_GMLA + (h + 1) * V_DIM] = (
                dym * ov * (sig * (1.0 + gm * (1.0 - sig)))).astype(BF16)

        t1 = (tile * tq + lax.broadcasted_iota(jnp.int32, (tq, 1), 0) + 1).astype(F32)
        for g, w in enumerate(POOL_WINDOWS):
            cols = slice(g * POOL_GROUP, (g + 1) * POOL_GROUP)
            pg = pooled_ref[:, cols]
            r = jnp.dot(pg, wp_ref[g], preferred_element_type=F32)
            gp = gp_ref[:, cols]
            sg = _sigmoid(gp)
            sl = gp * sg
            dyg = dyp_ref[:, cols]
            ps = ps_ref[:, cols]
            dg_ref[:, O_GPOOL + g * POOL_GROUP:O_GPOOL + (g + 1) * POOL_GROUP] = (
                dyg * (r * ps) * (sg * (1.0 + gp * (1.0 - sg)))).astype(BF16)
            dps_ref[:, cols] += jnp.sum(dyg * r * sl, axis=0, keepdims=True)
            dr = (dyg * ps * sl).astype(BF16)
            dwp_ref[g] += lax.dot_general(pg, dr, _DIMS["tn"], preferred_element_type=F32)
            dpooled = lax.dot_general(dr, wp_ref[g], _DIMS["nt"], preferred_element_type=F32)
            exte[0:tq, cols] = dpooled / jnp.minimum(t1, float(w))
            s = exte[:, cols]
            k = 1
            while k < w:
                s = s + pltpu.roll(s, n_ext - k, 0)
                k *= 2
            dg_ref[:, O_PIN + g * POOL_GROUP:O_PIN + (g + 1) * POOL_GROUP] = (s[0:tq, :] - dpooled).astype(BF16)

        for g in range(D_CONV // 128):
            cols = slice(g * 128, (g + 1) * 128)
            out = lambda base: slice(base + g * 128, base + (g + 1) * 128)
            gc = gc_ref[:, cols]
            sg = _sigmoid(gc)
            sl = gc * sg
            dyc = dyc_ref[:, cols]
            cb, cc, ch, cvv = cb_ref[:, cols], cc_ref[:, cols], ch_ref[:, cols], cv_ref[:, cols]
            dcv = dyc * cb * sl
            dg_ref[:, out(O_GCONV)] = (dyc * (cb * cvv) * (sg * (1.0 + gc * (1.0 - sg)))).astype(BF16)
            dg_ref[:, out(O_CB)] = (dyc * cvv * sl).astype(BF16)
            extd[0:tq, cols] = dcv
            ed = extd[:, cols]
            d1 = pltpu.roll(ed, n_ext - 1, 0)[0:tq, :]
            d2 = pltpu.roll(ed, n_ext - 2, 0)[0:tq, :]
            du = cw_ref[2:3, cols] * dcv + cw_ref[1:2, cols] * d1 + cw_ref[0:1, cols] * d2
            u = cc * ch
            dcw_ref[0:1, cols] += jnp.sum(u * d2, axis=0, keepdims=True)
            dcw_ref[1:2, cols] += jnp.sum(u * d1, axis=0, keepdims=True)
            dcw_ref[2:3, cols] += jnp.sum(u * dcv, axis=0, keepdims=True)
            dg_ref[:, out(O_CH)] = (du * cc).astype(BF16)
            dg_ref[:, out(O_CC)] = (du * ch).astype(BF16)

    def rblock(width, offset):
        assert offset % width == 0
        blk = offset // width
        return pl.BlockSpec((tq, width), lambda i: (nt - 1 - i, blk))

    full = lambda shape: pl.BlockSpec(shape, lambda i: (0,) * len(shape))
    return pl.pallas_call(
        body, name=name,
        out_shape=(jax.ShapeDtypeStruct((T, D_MLA), BF16), jax.ShapeDtypeStruct((N_HEADS, nt, 1, tq), F32),
                   jax.ShapeDtypeStruct((T, NPP), BF16),
                   jax.ShapeDtypeStruct((4, 128, 128), F32), jax.ShapeDtypeStruct((1, D_POOL), F32),
                   jax.ShapeDtypeStruct((8, D_CONV), F32)),
        grid=(nt,),
        in_specs=[rblock(1024, 0), rblock(512, 1024), rblock(512, 1536),
                  rblock(1024, O_GMLA), rblock(512, O_GPOOL), rblock(512, O_CH), rblock(512, O_CB),
                  rblock(512, O_CC), rblock(512, O_GCONV), rblock(1024, 0), rblock(512, 0), rblock(512, 0),
                  full((4, 128, 128)), full((1, D_POOL)), full((8, D_CONV))],
        out_specs=(rblock(1024, 0), pl.BlockSpec((N_HEADS, 1, 1, tq), lambda i: (0, nt - 1 - i, 0, 0)),
                   rblock(N_GATED, 0), full((4, 128, 128)), full((1, D_POOL)), full((8, D_CONV))),
        scratch_shapes=[pltpu.VMEM((n_ext, D_POOL), F32), pltpu.VMEM((n_ext, D_CONV), F32)],
        compiler_params=_params(("arbitrary",)),
    )(dmix, dmix, dmix, proj, proj, proj, proj, proj, proj, o, pooled, cv, w_pool, pool_scale, conv_w)


def _up_rms_bwd(proj, dq, dkv, w_uq, w_ukv, dkrope, dproj, q_g, kv_g, name, tq=256):
    T = proj.shape[0]
    n_lat = NPP - N_GATED

    def body(ql_ref, kvl_ref, dq_ref, dkv_ref, wq_ref, wkv_ref, dkr_ref, _, qg_ref, kvg_ref, dlat_ref, dqg_ref,
             dkvg_ref):
        @pl.when(pl.program_id(0) == 0)
        def _():
            dqg_ref[...] = jnp.zeros_like(dqg_ref)
            dkvg_ref[...] = jnp.zeros_like(dkvg_ref)

        dqn = lax.dot_general(dq_ref[...], wq_ref[...], _DIMS["nt"], preferred_element_type=F32)
        dkvn = lax.dot_general(dkv_ref[...], wkv_ref[...], _DIMS["nt"], preferred_element_type=F32)
        for x_ref, dy, g_ref, c0, dg_ref in ((ql_ref, dqn, qg_ref, 0, dqg_ref),
                                             (kvl_ref, dkvn, kvg_ref, Q_LORA, dkvg_ref)):
            x = x_ref[...]
            r = lax.rsqrt(jnp.mean(x * x, axis=1, keepdims=True) + RMS_EPS)
            xr = x * r
            u = dy * g_ref[...]
            dlat_ref[:, c0:c0 + x.shape[1]] = (r * (u - xr * jnp.mean(u * xr, axis=1, keepdims=True))).astype(BF16)
            dg_ref[...] += jnp.sum(dy * xr, axis=0, keepdims=True)
        dlat_ref[:, Q_LORA + KV_LORA:] = dkr_ref[...]

    row = lambda w: pl.BlockSpec((tq, w), lambda i: (i, 0))
    vec = lambda w: pl.BlockSpec((1, w), lambda i: (0, 0))
    assert N_GATED % n_lat == 0
    return pl.pallas_call(
        body, name=name,
        out_shape=(jax.ShapeDtypeStruct((T, NPP), BF16),
                   jax.ShapeDtypeStruct((1, Q_LORA), F32), jax.ShapeDtypeStruct((1, KV_LORA), F32)),
        grid=(T // tq,),
        in_specs=[_pblock(tq, Q_LORA, O_QLAT), _pblock(tq, KV_LORA, O_KVLAT), row(dq.shape[1]), row(dkv.shape[1]),
                  pl.BlockSpec(w_uq.shape, lambda i: (0, 0)), pl.BlockSpec(w_ukv.shape, lambda i: (0, 0)),
                  row(n_lat - Q_LORA - KV_LORA), pl.BlockSpec(memory_space=pl.ANY), vec(Q_LORA), vec(KV_LORA)],
        out_specs=(pl.BlockSpec((tq, n_lat), lambda i: (i, N_GATED // n_lat)), vec(Q_LORA), vec(KV_LORA)),
        input_output_aliases={7: 0},
        compiler_params=_params(("arbitrary",)),
    )(proj, proj, dq, dkv, w_uq, w_ukv, dkrope, dproj, q_g, kv_g)


def _swap_halves(x, lo):
    return jnp.where(lo, pltpu.roll(x, 96, 1), pltpu.roll(x, 32, 1))


def _up_rope_fwd(qn, kvn, w_uq, w_ukv, proj, cos_t, sin_t, name, tq=256, after=None):
    T = qn.shape[0]

    def body(qn_ref, kvn_ref, wq_ref, wkv_ref, kr_ref, c_ref, s_ref, *rest):
        kv_ref, qc_ref, kc_ref = rest[-3:]
        q = jnp.dot(qn_ref[...], wq_ref[...], preferred_element_type=F32)
        kv_ref[...] = jnp.dot(kvn_ref[...], wkv_ref[...], preferred_element_type=F32).astype(BF16)
        C, S = c_ref[...], s_ref[...]
        lane = lax.broadcasted_iota(jnp.int32, (tq, 128), 1)
        lo = (lane % ROPE) < (ROPE // 2)
        first = lane < ROPE

        def rope(x):
            return x * C + _swap_halves(x, lo) * S

        kr = jnp.where(first, rope(kr_ref[...]), 0.0).astype(BF16)
        n_nope = N_HEADS * NOPE
        for j in range(N_HEADS // 2):
            r = rope(q[:, n_nope + j * 128:n_nope + (j + 1) * 128])
            pair = (jnp.where(first, r, 0.0), jnp.where(first, pltpu.roll(r, 64, 1), 0.0))
            for hh in range(2):
                h = 2 * j + hh
                qc_ref[h, :, 0:NOPE] = q[:, h * NOPE:(h + 1) * NOPE].astype(BF16)
                qc_ref[h, :, NOPE:QC] = pair[hh].astype(BF16)
        for h in range(N_HEADS):
            kc_ref[h, :, 0:NOPE] = kv_ref[:, h * 256:h * 256 + NOPE]
            kc_ref[h, :, NOPE:QC] = kr

    out = jax.ShapeDtypeStruct((N_HEADS, T, QC), BF16)
    hblock = pl.BlockSpec((N_HEADS, tq, QC), lambda i: (0, i, 0))
    row = lambda w: pl.BlockSpec((tq, w), lambda i: (i, 0))
    full = lambda a: pl.BlockSpec(a.shape, lambda i: (0, 0))
    return pl.pallas_call(
        body, name=name, out_shape=(jax.ShapeDtypeStruct((T, 2 * D_MLA), BF16), out, out), grid=(T // tq,),
        in_specs=[row(Q_LORA), row(KV_LORA), full(w_uq), full(w_ukv), _pblock(tq, 128, O_KROPE), row(128), row(128)]
        + ([pl.BlockSpec((8, 128), lambda i: (0, 0))] if after is not None else []),
        out_specs=(row(2 * D_MLA), hblock, hblock),
        compiler_params=_params(("parallel",)),
    )(qn, kvn, w_uq, w_ukv, proj, cos_t, sin_t, *([after] if after is not None else []))


def _rope_bwd(dqc, dkr, cos_t, sin_t, name, tq=256):
    T = dqc.shape[1]

    def body(dqc_ref, dkr_ref, c_ref, s_ref, dq_ref, dk_ref):
        C, S = c_ref[...], s_ref[...]
        lane = lax.broadcasted_iota(jnp.int32, (tq, 128), 1)
        lo = (lane % ROPE) < (ROPE // 2)
        first = lane < ROPE

        def unrope(dy):
            return dy * C - _swap_halves(dy, lo) * S

        acc = dkr_ref[0]
        for h in range(1, N_HEADS):
            acc = acc + dkr_ref[h]
        dk_ref[:, 0:128] = jnp.where(first, unrope(acc), 0.0).astype(BF16)
        dk_ref[:, 128:256] = jnp.zeros((tq, 128), BF16)
        for j in range(N_HEADS // 2):
            d0 = dqc_ref[2 * j, :, NOPE:QC]
            d1 = dqc_ref[2 * j + 1, :, NOPE:QC]
            comb = jnp.where(first, d0, pltpu.roll(d1, 64, 1))
            dq_ref[:, 1024 + j * 128:1024 + (j + 1) * 128] = unrope(comb).astype(BF16)
        for h in range(N_HEADS):
            dq_ref[:, h * NOPE:(h + 1) * NOPE] = dqc_ref[h, :, 0:NOPE].astype(BF16)

    tab = pl.BlockSpec((tq, 128), lambda i: (i, 0))
    return pl.pallas_call(
        body, name=name,
        out_shape=(jax.ShapeDtypeStruct((T, 1536), BF16), jax.ShapeDtypeStruct((T, 256), BF16)),
        grid=(T // tq,),
        in_specs=[pl.BlockSpec((N_HEADS, tq, QC), lambda i: (0, i, 0)),
                  pl.BlockSpec((N_HEADS, tq, 128), lambda i: (0, i, 0)), tab, tab],
        out_specs=(pl.BlockSpec((tq, 1536), lambda i: (i, 0)), pl.BlockSpec((tq, 256), lambda i: (i, 0))),
        compiler_params=_params(("parallel",)),
    )(dqc, dkr, cos_t, sin_t)


def _flash_fwd(qc, kc, kv, proj, mix, name):
    H, T, _ = qc.shape
    tt = ATT_TILE
    nt = T // tt
    sp = tt // ATT_CH

    def body(q_ref, k_ref, v_ref, g_ref, _, o_ref, y_ref, lse_ref, vt_sc, s_sc, acc_sc, m_sc, l_sc, bias_sc):
        i = pl.program_id(1)

        @pl.when(i == 0)
        def _():
            for c in range(nt):
                vt_sc[c] = v_ref[c * tt:(c + 1) * tt, :].astype(F32).T.astype(BF16)
            krow = lax.broadcasted_iota(jnp.int32, (tt, tt), 0)
            qcol = lax.broadcasted_iota(jnp.int32, (tt, tt), 1)
            bias_sc[...] = jnp.where(krow <= qcol, 0.0, -jnp.inf)

        q = q_ref[0]

        def issue(c, slot):
            s_sc[slot] = lax.dot_general(k_ref[0, pl.ds(pl.multiple_of(c * tt, tt), tt), :], q, _DIMS["nt"],
                                         preferred_element_type=F32)

        def softmax_pv(c, slot, masked):
            s = s_sc[slot]
            if masked:
                s = s + bias_sc[...]
            m = m_sc[...]
            m_new = jnp.maximum(m, jnp.max(s, axis=0, keepdims=True))
            p = jnp.exp2((s - m_new) * EXP2_SCALE)
            a = jnp.exp2((m - m_new) * EXP2_SCALE)
            l_sc[...] = a * l_sc[...] + jnp.sum(p, axis=0, keepdims=True)
            acc_sc[...] = a * acc_sc[...] + jnp.dot(vt_sc[c], p.astype(BF16), preferred_element_type=F32)
            m_sc[...] = m_new

        issue(0, 0)
        m_sc[...] = jnp.full_like(m_sc, -jnp.inf)
        l_sc[...] = jnp.zeros_like(l_sc)
        acc_sc[...] = jnp.zeros_like(acc_sc)

        def pair(t, carry):
            issue(2 * t + 1, 1)
            softmax_pv(2 * t, 0, False)
            issue(2 * t + 2, 0)
            softmax_pv(2 * t + 1, 1, False)
            return carry

        lax.fori_loop(0, i // 2, pair, 0)

        @pl.when(i % 2 == 1)
        def _():
            issue(i, 1)
            softmax_pv(i - 1, 0, False)
            softmax_pv(i, 1, True)

        @pl.when(i % 2 == 0)
        def _():
            softmax_pv(i, 0, True)

        l = l_sc[...]
        o = (acc_sc[...] / l).T
        o_ref[...] = o
        lse = m_sc[...] * ATTN_SCALE + jnp.log(l)
        for r in range(sp):
            lse_ref[0, r] = lse[:, r * ATT_CH:(r + 1) * ATT_CH]
        g = g_ref[...]
        y_ref[...] = (o * (g * _sigmoid(g))).astype(BF16)

    return pl.pallas_call(
        body, name=name,
        out_shape=(jax.ShapeDtypeStruct((T, D_MLA), F32), jax.ShapeDtypeStruct((T, D_MIX), BF16),
                   jax.ShapeDtypeStruct((H, T // ATT_CH, 1, ATT_CH), F32)),
        grid=(H, nt),
        in_specs=[pl.BlockSpec((1, tt, QC), lambda h, i: (h, i, 0)),
                  pl.BlockSpec((1, T, QC), lambda h, i: (h, 0, 0)),
                  pl.BlockSpec((T, V_DIM), lambda h, i: (0, 2 * h + 1)),
                  pl.BlockSpec((tt, V_DIM), lambda h, i: (i, h)),
                  pl.BlockSpec(memory_space=pl.ANY)],
        input_output_aliases={4: 1},
        out_specs=(pl.BlockSpec((tt, V_DIM), lambda h, i: (i, h)),
                   pl.BlockSpec((tt, V_DIM), lambda h, i: (i, h)),
                   pl.BlockSpec((1, sp, 1, ATT_CH), lambda h, i: (h, i, 0, 0))),
        scratch_shapes=[pltpu.VMEM((nt, V_DIM, tt), BF16), pltpu.VMEM((2, tt, tt), F32),
                        pltpu.VMEM((V_DIM, tt), F32), pltpu.VMEM((1, tt), F32), pltpu.VMEM((1, tt), F32),
                        pltpu.VMEM((tt, tt), F32)],
        compiler_params=_params(("parallel", "arbitrary")),
    )(qc, kc, kv, proj, mix)


def _flash_bwd(qc, kc, kv, do, lse, delta, name):
    H, T, _ = qc.shape
    tt = ATT_TILE
    nt = T // tt
    sp = tt // ATT_CH
    pairs = [(j, c) for j in range(nt) for c in range(j, nt)]
    assert len(pairs) % 2 == 0
    table = jnp.asarray(pairs + [pairs[-1]], jnp.int32)

    def body(tab_ref, q_ref, k_ref, v_ref, do_ref, lse_ref, dl_ref, dq_ref, dkv_ref, dkr_ref, dqt_sc, dk_sc, dv_sc,
             s_sc, dp_sc, kt_sc, bias_sc):
        def operands(j, c):
            k0, q0 = pl.multiple_of(j * tt, tt), pl.multiple_of(c * tt, tt)
            return (k_ref[0, pl.ds(k0, tt), :], v_ref[pl.ds(k0, tt), :], q_ref[0, pl.ds(q0, tt), :],
                    do_ref[pl.ds(q0, tt), :])

        def stat_row(ref, c):
            return jnp.concatenate([ref[0, sp * c + r] for r in range(sp)], axis=1)

        def early(p, slot):
            k, v, q, dov = operands(tab_ref[p, 0], tab_ref[p, 1])
            s_sc[slot] = lax.dot_general(k, q, _DIMS["nt"], preferred_element_type=F32)
            dp_sc[slot] = lax.dot_general(v, dov, _DIMS["nt"], preferred_element_type=F32)

        def late(p, slot):
            j, c = tab_ref[p, 0], tab_ref[p, 1]
            _, _, q, dov = operands(j, c)
            s = s_sc[slot] + jnp.where(j == c, bias_sc[...], 0.0)
            pr = jnp.exp2(s * EXP2_SCALE - stat_row(lse_ref, c) * LOG2E)
            ds = (pr * (dp_sc[slot] - stat_row(dl_ref, c)) * ATTN_SCALE).astype(BF16)
            dv_sc[j] += jnp.dot(pr.astype(BF16), dov, preferred_element_type=F32)
            dk_sc[j] += jnp.dot(ds, q, preferred_element_type=F32)
            dqt_sc[c] += jnp.dot(kt_sc[j], ds, preferred_element_type=F32)

        early(0, 0)
        dqt_sc[...] = jnp.zeros_like(dqt_sc)
        dk_sc[...] = jnp.zeros_like(dk_sc)
        dv_sc[...] = jnp.zeros_like(dv_sc)
        krow = lax.broadcasted_iota(jnp.int32, (tt, tt), 0)
        qcol = lax.broadcasted_iota(jnp.int32, (tt, tt), 1)
        bias_sc[...] = jnp.where(krow <= qcol, 0.0, -jnp.inf)
        for j in range(nt):
            kt_sc[j] = k_ref[0, j * tt:(j + 1) * tt, :].astype(F32).T.astype(BF16)

        def two(u, carry):
            p = 2 * u
            early(p + 1, 1)
            late(p, 0)
            early(p + 2, 0)
            late(p + 1, 1)
            return carry

        lax.fori_loop(0, len(pairs) // 2, two, 0)
        for j in range(nt):
            rows = slice(j * tt, (j + 1) * tt)
            dk = dk_sc[j]
            dkv_ref[rows, 0:NOPE] = dk[:, 0:NOPE].astype(BF16)
            dkv_ref[rows, NOPE:] = dv_sc[j].astype(BF16)
            dkr_ref[0, rows, :] = dk[:, NOPE:]
            dq_ref[0, rows, :] = dqt_sc[j].T

    head = lambda h, tab: (h, 0, 0)
    stat = pl.BlockSpec((1, T // ATT_CH, 1, ATT_CH), lambda h, tab: (h, 0, 0, 0))
    return pl.pallas_call(
        body, name=name,
        out_shape=(jax.ShapeDtypeStruct((H, T, QC), F32), jax.ShapeDtypeStruct((T, 2 * D_MLA), BF16),
                   jax.ShapeDtypeStruct((H, T, 128), F32)),
        grid_spec=pltpu.PrefetchScalarGridSpec(
            num_scalar_prefetch=1, grid=(H,),
            in_specs=[pl.BlockSpec((1, T, QC), head), pl.BlockSpec((1, T, QC), head),
                      pl.BlockSpec((T, V_DIM), lambda h, tab: (0, 2 * h + 1)),
                      pl.BlockSpec((T, V_DIM), lambda h, tab: (0, h)), stat, stat],
            out_specs=(pl.BlockSpec((1, T, QC), head), pl.BlockSpec((T, 256), lambda h, tab: (0, h)),
                       pl.BlockSpec((1, T, 128), head)),
            scratch_shapes=[pltpu.VMEM((nt, QC, tt), F32), pltpu.VMEM((nt, tt, QC), F32),
                            pltpu.VMEM((nt, tt, V_DIM), F32), pltpu.VMEM((2, tt, tt), F32),
                            pltpu.VMEM((2, tt, tt), F32), pltpu.VMEM((nt, QC, tt), BF16), pltpu.VMEM((tt, tt), F32)]),
        compiler_params=pltpu.CompilerParams(dimension_semantics=("arbitrary",),
                                             vmem_limit_bytes=ATT_BWD_VMEM_LIMIT),
    )(table, qc, kc, kv, do, lse, delta)


def _adamw(lands, w, m, v, name, rows, cols=None, first_layer=0, into=None):
    layers, R, C = w.shape
    L = len(lands)
    cols = C if cols is None else cols
    assert R % rows == 0 and C % cols == 0 and first_layer + L <= layers
    nc = C // cols
    nb = (R // rows) * nc
    c1 = 1.0 - ADAM_B1 ** ADAM_STEP
    c2 = 1.0 - ADAM_B2 ** ADAM_STEP

    def body(*refs):
        land_refs = refs[:L]
        w_ref, m_ref, v_ref = refs[L:L + 3]
        g_ref, d_ref, nm_ref, nv_ref, g_sc = refs[-5:]
        for ll in range(L):
            @pl.when(pl.program_id(0) == ll)
            def _(land_ref=land_refs[ll]):
                g = land_ref[0].astype(F32)
                for s in range(1, N_DEV):
                    g = g + land_ref[s].astype(F32)
                g_sc[...] = g

        g = g_sc[...]
        nm = ADAM_B1 * m_ref[0] + (1.0 - ADAM_B1) * g
        nv = ADAM_B2 * v_ref[0] + (1.0 - ADAM_B2) * (g * g)
        g_ref[0] = g
        nm_ref[0] = nm
        nv_ref[0] = nv
        d_ref[0] = -ADAM_LR * ((nm / c1) / (jnp.sqrt(nv / c2) + ADAM_EPS) + ADAM_WD * w_ref[0])

    def land_spec(ll):
        def index(l, i):
            i = jnp.where(l < ll, 0, jnp.where(l > ll, nb - 1, i))
            return (0, i // nc, i % nc)
        return pl.BlockSpec((N_DEV, rows, cols), index)

    blk = pl.BlockSpec((1, rows, cols), lambda l, i: (first_layer + l, i // nc, i % nc))
    out = jax.ShapeDtypeStruct((layers, R, C), F32)
    extra = [] if into is None else list(into)
    return pl.pallas_call(
        body, name=name, out_shape=(out, out, out, out), grid=(L, nb),
        in_specs=[land_spec(ll) for ll in range(L)] + [blk, blk, blk] + [pl.BlockSpec(memory_space=pl.ANY)] * len(extra),
        out_specs=(blk, blk, blk, blk),
        input_output_aliases={L + 3 + i: i for i in range(len(extra))},
        scratch_shapes=[pltpu.VMEM((rows, cols), F32)],
        compiler_params=_params(("arbitrary", "arbitrary")),
    )(*lands, w, m, v, *extra)


def _mesh_pos():
    return lax.axis_index("x"), lax.axis_index("y"), lax.axis_index("c")


def _all_gather(arrays, name):
    n = len(arrays)

    def body(*refs):
        ins, outs = refs[:n], refs[n:2 * n]
        send_sems, recv_sems, local_sems = refs[2 * n:]
        x, y, c = _mesh_pos()
        me, sibling = (x, y, c), (x, y, 1 - c)
        chips = [(1 - x, y), (x, 1 - y), (1 - x, 1 - y)]

        def slot(a, pos):
            px, py, pc = pos
            return outs[a].at[4 * px + 2 * py + pc]

        def copy(a, k, block, to, src=None):
            return pltpu.make_async_remote_copy(
                src_ref=slot(a, block) if src is None else src, dst_ref=slot(a, block),
                send_sem=send_sems.at[a * 7 + k], recv_sem=recv_sems.at[a * 7 + k],
                device_id=to, device_id_type=MESH_ID)

        mine, first, passed = [], [], []
        for a in range(n):
            cp = pltpu.make_async_copy(ins[a], slot(a, me), local_sems.at[a])
            cp.start()
            mine.append(cp)
            cps = [copy(a, 0, me, sibling, src=ins[a])]
            cps += [copy(a, 1 + j, me, (*chip, c), src=ins[a]) for j, chip in enumerate(chips)]
            for cp in cps:
                cp.start()
            first += cps
        for j, chip in enumerate(chips):
            for a in range(n):
                copy(a, 1 + j, (*chip, c), me).wait_recv()
                cp = copy(a, 4 + j, (*chip, c), sibling)
                cp.start()
                passed.append(cp)
        for a in range(n):
            copy(a, 0, sibling, me).wait_recv()
            for j, chip in enumerate(chips):
                copy(a, 4 + j, (*chip, 1 - c), me).wait_recv()
        for cp in first + passed:
            cp.wait_send()
        for cp in mine:
            cp.wait()

    hbm = pl.BlockSpec(memory_space=pltpu.HBM)
    return pl.pallas_call(
        body, name=name,
        out_shape=tuple(jax.ShapeDtypeStruct((N_DEV,) + a.shape, a.dtype) for a in arrays),
        in_specs=[hbm] * n, out_specs=tuple([hbm] * n),
        scratch_shapes=[pltpu.SemaphoreType.DMA((7 * n,)), pltpu.SemaphoreType.DMA((7 * n,)),
                        pltpu.SemaphoreType.DMA((n,))],
    )(*arrays)


def _all_gather_under_ln(arrays, x, g, b, name, tq=512):
    n = len(arrays)
    T, D = x.shape
    nt = T // tq

    def body(*refs):
        x_ref, g_ref, b_ref = refs[:3]
        ins = refs[3:3 + n]
        y_ref, yb_ref = refs[3 + n:5 + n]
        outs = refs[5 + n:5 + 2 * n]
        send_sems, recv_sems, local_sems = refs[5 + 2 * n:]
        i = pl.program_id(0)
        mx, my, mc = _mesh_pos()
        me, sibling = (mx, my, mc), (mx, my, 1 - mc)
        chips = [(1 - mx, my), (mx, 1 - my), (1 - mx, 1 - my)]

        def slot(a, pos):
            px, py, pc = pos
            return outs[a].at[4 * px + 2 * py + pc]

        def copy(a, k, block, to, src=None):
            return pltpu.make_async_remote_copy(
                src_ref=slot(a, block) if src is None else src, dst_ref=slot(a, block),
                send_sem=send_sems.at[a * 7 + k], recv_sem=recv_sems.at[a * 7 + k],
                device_id=to, device_id_type=MESH_ID)

        def own(a):
            return pltpu.make_async_copy(ins[a], slot(a, me), local_sems.at[a])

        def first(a):
            return [copy(a, 0, me, sibling, src=ins[a])] + [
                copy(a, 1 + j, me, (*chip, mc), src=ins[a]) for j, chip in enumerate(chips)]

        @pl.when(i == 0)
        def _():
            for a in range(n):
                own(a).start()
                for cp in first(a):
                    cp.start()

        zv = x_ref[...]
        mu = jnp.mean(zv, axis=1, keepdims=True)
        zc = zv - mu
        var = jnp.mean(zc * zc, axis=1, keepdims=True)
        y = zc * lax.rsqrt(var + LN_EPS) * g_ref[...] + b_ref[...]
        y_ref[...] = y
        yb_ref[...] = y.astype(BF16)

        @pl.when(i == nt - 1)
        def _():
            passed = []
            for j, chip in enumerate(chips):
                for a in range(n):
                    copy(a, 1 + j, (*chip, mc), me).wait_recv()
                    cp = copy(a, 4 + j, (*chip, mc), sibling)
                    cp.start()
                    passed.append(cp)
            for a in range(n):
                copy(a, 0, sibling, me).wait_recv()
                for j, chip in enumerate(chips):
                    copy(a, 4 + j, (*chip, 1 - mc), me).wait_recv()
            for a in range(n):
                for cp in first(a):
                    cp.wait_send()
                own(a).wait()
            for cp in passed:
                cp.wait_send()

    row = pl.BlockSpec((tq, D), lambda i: (i, 0))
    vec = pl.BlockSpec((1, D), lambda i: (0, 0))
    hbm = pl.BlockSpec(memory_space=pltpu.HBM)
    outs = pl.pallas_call(
        body, name=name,
        out_shape=(jax.ShapeDtypeStruct((T, D), F32), jax.ShapeDtypeStruct((T, D), BF16))
        + tuple(jax.ShapeDtypeStruct((N_DEV,) + a.shape, a.dtype) for a in arrays),
        grid=(nt,), in_specs=[row, vec, vec] + [hbm] * n, out_specs=tuple([row, row] + [hbm] * n),
        scratch_shapes=[pltpu.SemaphoreType.DMA((7 * n,)), pltpu.SemaphoreType.DMA((7 * n,)),
                        pltpu.SemaphoreType.DMA((n,))],
        compiler_params=_params(("arbitrary",)),
    )(x, g, b, *arrays)
    return outs[0], outs[1], outs[2:]


_HBM = pl.BlockSpec(memory_space=pltpu.HBM)
_SEM = pl.BlockSpec(memory_space=pltpu.SEMAPHORE)
_EFFECT = pltpu.SideEffectType.DATAFLOW_SIDE_EFFECTING
N_PEERS = N_DEV - 1


def _peer(k):
    x, y, c = _mesh_pos()
    return (1 - x if k & 4 else x, 1 - y if k & 2 else y, 1 - c if k & 1 else c)


def _split_start(srcs, scatter, after, name):
    n = len(srcs)
    zones = [jax.ShapeDtypeStruct(s.shape if scatter else ((N_DEV,) + s.shape), s.dtype) for s in srcs]

    def body(*refs):
        src, zone = refs[:n], refs[n:2 * n]
        outs = refs[2 * n + 1:]
        send, recv, token = outs[:n], outs[n:2 * n], outs[4 * n]
        x, y, c = _mesh_pos()
        my_idx = 4 * x + 2 * y + c
        for a in range(n):
            pltpu.make_async_copy(src[a].at[my_idx] if scatter else src[a],
                                  zone[a].at[N_PEERS] if scatter else zone[a].at[my_idx], recv[a]).start()
            for k in range(1, N_DEV):
                px, py, pc = _peer(k)
                pltpu.make_async_remote_copy(
                    src_ref=src[a].at[4 * px + 2 * py + pc] if scatter else src[a],
                    dst_ref=zone[a].at[k - 1] if scatter else zone[a].at[my_idx],
                    send_sem=send[a], recv_sem=recv[a], device_id=(px, py, pc), device_id_type=MESH_ID).start()
        token[...] = jnp.zeros_like(token)

    hbm = lambda a: pltpu.with_memory_space_constraint(a, pltpu.HBM)
    outs = pl.pallas_call(
        body, name=name,
        out_shape=tuple([pltpu.SemaphoreType.DMA(())] * (2 * n)
                        + [pltpu.HBM(s.shape, s.dtype) for s in srcs]
                        + [pltpu.HBM(z.shape, z.dtype) for z in zones]
                        + [jax.ShapeDtypeStruct((8, 128), F32)]),
        in_specs=[_HBM] * (2 * n) + [pl.BlockSpec(memory_space=pl.ANY)],
        out_specs=tuple([_SEM] * (2 * n) + [_HBM] * (2 * n) + [pl.BlockSpec(memory_space=pltpu.VMEM)]),
        input_output_aliases={**{a: 2 * n + a for a in range(n)}, **{n + a: 3 * n + a for a in range(n)}},
        compiler_params=pltpu.CompilerParams(has_side_effects=_EFFECT),
    )(*[hbm(s) for s in srcs], *[hbm(lax.empty(z.shape, z.dtype)) for z in zones], after)
    return outs[:n], outs[n:2 * n], outs[2 * n:3 * n], outs[3 * n:4 * n], outs[4 * n]


def _split_wait(send, recv, srcs, zones, after, name):
    n = len(srcs)

    def body(*refs):
        zone = refs[n:2 * n]
        send_sems, recv_sems = refs[2 * n:3 * n], refs[3 * n:4 * n]
        x, y, c = _mesh_pos()
        for a in range(n):
            seven = zone[a].at[pl.ds(0, N_PEERS)]
            pltpu.make_async_remote_copy(src_ref=seven, dst_ref=seven, send_sem=send_sems[a], recv_sem=recv_sems[a],
                                         device_id=(x, y, 1 - c), device_id_type=MESH_ID).wait_send()
            pltpu.make_async_remote_copy(src_ref=zone[a], dst_ref=zone[a], send_sem=send_sems[a],
                                         recv_sem=recv_sems[a], device_id=(x, y, 1 - c),
                                         device_id_type=MESH_ID).wait_recv()

    outs = pl.pallas_call(
        body, name=name,
        out_shape=tuple([pltpu.HBM(s.shape, s.dtype) for s in srcs] + [pltpu.HBM(z.shape, z.dtype) for z in zones]),
        in_specs=[_HBM] * (2 * n) + [_SEM] * (2 * n) + [pl.BlockSpec(memory_space=pl.ANY)],
        out_specs=tuple([_HBM] * (2 * n)),
        input_output_aliases={a: a for a in range(2 * n)},
        compiler_params=pltpu.CompilerParams(has_side_effects=_EFFECT),
    )(*srcs, *zones, *send, *recv, after)
    return outs[:n], outs[n:]


def _cat_blocks(g, axis):
    return jnp.concatenate([g[d] for d in range(N_DEV)], axis=axis)


N_LATENT = Q_LORA + KV_LORA + ROPE
W_SHARD = D_IN_PROJ // N_DEV


def _ref_cols(lo, hi):
    out = []
    if lo < N_LATENT:
        out.append((N_GATED + lo, N_GATED + min(hi, N_LATENT)))
    if hi > N_LATENT:
        out.append((max(lo, N_LATENT) - N_LATENT, hi - N_LATENT))
    return out


def _permute_w_in_t(blocks):
    pieces = []
    for lo, hi in ((N_LATENT, D_IN_PROJ), (0, N_LATENT)):
        for d in range(N_DEV):
            a, b = max(lo, d * W_SHARD), min(hi, (d + 1) * W_SHARD)
            if a < b:
                pieces.append(blocks[d][a - d * W_SHARD:b - d * W_SHARD])
    pieces.append(jnp.zeros((NPP - D_IN_PROJ, blocks.shape[2]), blocks.dtype))
    return jnp.concatenate(pieces, axis=0)


def _split_w_in_t(w):
    slabs = []
    for d in range(N_DEV):
        parts = [w[a:b] for a, b in _ref_cols(d * W_SHARD, (d + 1) * W_SHARD)]
        slabs.append(parts[0] if len(parts) == 1 else jnp.concatenate(parts, axis=0))
    return jnp.stack(slabs)


def _permute_w_uq(w):
    w3 = w.reshape(w.shape[0], N_HEADS, NOPE + ROPE)
    return jnp.concatenate([w3[:, :, :NOPE].reshape(w.shape[0], -1), w3[:, :, NOPE:].reshape(w.shape[0], -1)], axis=1)


def _unpermute_w_uq(w):
    nope = w[:, :N_HEADS * NOPE].reshape(w.shape[0], N_HEADS, NOPE)
    rope = w[:, N_HEADS * NOPE:].reshape(w.shape[0], N_HEADS, ROPE)
    return jnp.concatenate([nope, rope], axis=2).reshape(w.shape[0], -1)


_SMALL_EMB = (("emb_ln_g", 16), ("emb_ln_b", 16))
_SMALL_LAYER = (("q_norm_g", 8), ("kv_norm_g", 8), ("w_pool", 1024), ("pool_scale", 8), ("b_out", 32),
                ("ln_g", 32), ("ln_b", 32))
_SMALL = _SMALL_EMB + _SMALL_LAYER
CONV_ROWS = DEPTH * CONV_WIDTH * D_CONV // 128


def _pack_small(d, entries=_SMALL):
    parts = []
    for name, rows in entries:
        flat = d[name].reshape(-1)
        flat = jnp.pad(flat, (0, rows * 128 - flat.shape[0]))
        parts.append(flat.reshape(rows, 128))
    return jnp.concatenate(parts, axis=0)


def _unpack_small(packed, shapes):
    out, r0 = {}, 0
    for name, rows in _SMALL:
        size = 1
        for s in shapes[name]:
            size *= s
        out[name] = packed[r0:r0 + rows].reshape(-1)[:size].reshape(shapes[name])
        r0 += rows
    return out


def _rope_tables(positions):
    half = ROPE // 2
    inv_freq = ROPE_THETA ** (-jnp.arange(half, dtype=F32) / half)
    ang = positions.astype(F32)[:, None] * inv_freq
    cos, sin = jnp.cos(ang), jnp.sin(ang)
    return jnp.concatenate([cos, cos, cos, cos], axis=1), jnp.concatenate([-sin, sin, -sin, sin], axis=1)


def _local_step(x, positions, target, emb_g, emb_b, layer_weights, layer_weights_rest, on_sharded_grads,
                on_layer_grads=None, first_after=None, embedded=None):
    cos_t, sin_t = _rope_tables(positions)
    h, hb = _ln_fwd(x, emb_g, emb_b, "emb_ln_fwd") if embedded is None else embedded
    saved = []
    for l in range(DEPTH):
        W = layer_weights(l, h)
        proj = _mm(hb, W["w_in_t"], "nt", F32, "proj_fwd", after=first_after if l == 0 else None)
        qn, kvn, pooled, cv, mix = _mix_fwd(proj, W["q_norm_g"], W["kv_norm_g"], W["w_pool"], W["pool_scale"],
                                            W["conv_w"], "mix_fwd")
        rest, token = layer_weights_rest(l, proj)
        W = {**W, **rest}
        kv, qc, kc = _up_rope_fwd(qn, kvn, W["w_uq"], W["w_ukv"], proj, cos_t, sin_t, "up_rope_fwd", after=token)
        o, mix, lse = _flash_fwd(qc, kc, kv, proj, mix, "flash_fwd")
        z = _mm(mix, W["w_out"], "nn", F32, "out_fwd", res=h, bias=W["b_out"], alpha=ALPHA)
        saved.append((W, hb, proj, qn, kvn, pooled, cv, kv, qc, kc, o, lse, mix, z))
        h, hb = _ln_fwd(z, W["ln_g"], W["ln_b"], "ln_fwd")
    dh, sq = h, None

    grads = {k: [None] * DEPTH for k in ("q_norm_g", "kv_norm_g", "w_pool", "pool_scale", "conv_w", "b_out", "ln_g",
                                         "ln_b")}
    for l in reversed(range(DEPTH)):
        W, hb_in, proj, qn, kvn, pooled, cv, kv, qc, kc, o, lse, mix, z = saved[l]
        sharded = {}
        if l == DEPTH - 1:
            dz, dzb, grads["b_out"][l], grads["ln_g"][l], grads["ln_b"][l], sq = _ln_bwd(
                dh, z, W["ln_g"], "ln_bwd_loss", target=target)
        else:
            dz, dzb, grads["b_out"][l], grads["ln_g"][l], grads["ln_b"][l] = _ln_bwd(dh, z, W["ln_g"], "ln_bwd")
        dmix = _mm(dzb, W["w_out"], "nt", F32, "out_bwd_x")
        sharded["w_out"] = _mm(mix, dzb, "tn", GRAD_XFER, "out_bwd_w", tk=4096)
        do, delta, dproj, grads["w_pool"][l], grads["pool_scale"][l], grads["conv_w"][l] = _mix_bwd(
            dmix, proj, o, pooled, cv, W["w_pool"], W["pool_scale"], W["conv_w"], "mix_bwd")
        dqc, dkv, dkr = _flash_bwd(qc, kc, kv, do, lse, delta, "flash_bwd")
        dq, dkrope = _rope_bwd(dqc, dkr, cos_t, sin_t, "rope_bwd")
        sharded["w_uq"] = _mm(qn, dq, "tn", GRAD_XFER, "q_up_bwd_w")
        sharded["w_ukv"] = _mm(kvn, dkv, "tn", GRAD_XFER, "kv_up_bwd_w")
        token = on_sharded_grads(l, sharded)
        dproj, grads["q_norm_g"][l], grads["kv_norm_g"][l] = _up_rms_bwd(
            proj, dq, dkv, W["w_uq"], W["w_ukv"], dkrope, dproj, W["q_norm_g"], W["kv_norm_g"], "up_rms_bwd")
        if l == 0 and on_layer_grads is not None:
            token = on_layer_grads(grads, token)
        d_w_in_t = _mm(dproj, hb_in, "tn", GRAD_XFER, "proj_bwd_w", tk=4096, after=token)
        token = on_sharded_grads(l, {"w_in": d_w_in_t})
        dh = _mm(dproj, W["w_in_t"], "nn", F32, "proj_bwd_x", res=dz, alpha=ALPHA, tk=2560, after=token)
    grad_x, grads["emb_ln_g"], grads["emb_ln_b"] = _ln_bwd(dh, x, emb_g, "emb_ln_bwd", for_matmul=False)
    return sq, grad_x, grads


def kernel(x, positions, emb_ln_g, emb_ln_b, w_in, q_norm_g, kv_norm_g, w_uq, w_ukv, w_pool, pool_scale, conv_w, w_out, b_out, ln_g, ln_b, loss_target, m_emb_ln_g, m_emb_ln_b, m_w_in, m_q_norm_g, m_kv_norm_g, m_w_uq, m_w_ukv, m_w_pool, m_pool_scale, m_conv_w, m_w_out, m_b_out, m_ln_g, m_ln_b, v_emb_ln_g, v_emb_ln_b, v_w_in, v_q_norm_g, v_kv_norm_g, v_w_uq, v_w_ukv, v_w_pool, v_pool_scale, v_conv_w, v_w_out, v_b_out, v_ln_g, v_ln_b):
    weights = dict(emb_ln_g=emb_ln_g, emb_ln_b=emb_ln_b, w_in=w_in, q_norm_g=q_norm_g, kv_norm_g=kv_norm_g,
                   w_uq=w_uq, w_ukv=w_ukv, w_pool=w_pool, pool_scale=pool_scale, conv_w=conv_w, w_out=w_out,
                   b_out=b_out, ln_g=ln_g, ln_b=ln_b)
    mom1 = dict(emb_ln_g=m_emb_ln_g, emb_ln_b=m_emb_ln_b, w_in=m_w_in, q_norm_g=m_q_norm_g, kv_norm_g=m_kv_norm_g,
                w_uq=m_w_uq, w_ukv=m_w_ukv, w_pool=m_w_pool, pool_scale=m_pool_scale, conv_w=m_conv_w,
                w_out=m_w_out, b_out=m_b_out, ln_g=m_ln_g, ln_b=m_ln_b)
    mom2 = dict(emb_ln_g=v_emb_ln_g, emb_ln_b=v_emb_ln_b, w_in=v_w_in, q_norm_g=v_q_norm_g, kv_norm_g=v_kv_norm_g,
                w_uq=v_w_uq, w_ukv=v_w_ukv, w_pool=v_w_pool, pool_scale=v_pool_scale, conv_w=v_conv_w,
                w_out=v_w_out, b_out=v_b_out, ln_g=v_ln_g, ln_b=v_ln_b)

    big = ("w_in", "w_uq", "w_ukv", "w_out")

    conv_pad = jnp.zeros((8, 128), F32).at[0:DEPTH * CONV_WIDTH, 0:64].set(conv_w.reshape(DEPTH * CONV_WIDTH, 64))
    t12 = lambda a: jnp.swapaxes(a, 1, 2)
    shard = lambda k, l: (t12(weights[k])[l] if k == "w_in" else weights[k][l]).astype(BF16)
    h0, h0b, (w_in0, conv_all) = _all_gather_under_ln(
        [shard("w_in", 0), conv_pad], x[0], emb_ln_g.reshape(1, -1), emb_ln_b.reshape(1, -1), "w_in0_all_gather_emb_ln")
    rest0 = _split_start([shard(k, 0) for k in big[1:]], False, w_in0, "weights0_rest_start")
    conv_full = _cat_blocks(conv_all[:, 0:DEPTH * CONV_WIDTH, 0:64], 1).reshape(DEPTH, CONV_WIDTH, D_CONV)
    conv_full = jnp.pad(conv_full, ((0, 0), (0, 8 - CONV_WIDTH), (0, 0)))
    fetched = {}

    def layer_weights(l, ready):
        if l == 0:
            w_in_blocks = w_in0
        else:
            fetched[1] = _split_wait(*fetched["w1"][:4], ready, "weights1_wait")[1]
            w_in_blocks = fetched[1][0]
        return dict(
            w_in_t=_permute_w_in_t(w_in_blocks), conv_w=conv_full[l],
            q_norm_g=q_norm_g[l].reshape(1, -1), kv_norm_g=kv_norm_g[l].reshape(1, -1),
            w_pool=w_pool[l].astype(BF16), pool_scale=pool_scale[l].reshape(1, -1), b_out=b_out[l].reshape(1, -1),
            ln_g=ln_g[l].reshape(1, -1), ln_b=ln_b[l].reshape(1, -1))

    def layer_weights_rest(l, ready):
        token = None
        if l == 0:
            blocks = _split_wait(*rest0[:4], ready, "weights0_rest_wait")[1]
            fetched["w1"] = _split_start([shard(k, 1) for k in big], False, blocks[0], "weights1_start")
            token = fetched["w1"][4]
        else:
            blocks = fetched[1][1:]
        return dict(w_uq=_permute_w_uq(_cat_blocks(blocks[0], 1)), w_ukv=_cat_blocks(blocks[1], 1),
                    w_out=blocks[2].reshape(D_MIX, D_MODEL)), token

    by_dest = dict(
        w_in=_split_w_in_t,
        w_uq=lambda g: _unpermute_w_uq(g).reshape(Q_LORA, N_DEV, -1).transpose(1, 0, 2),
        w_ukv=lambda g: g.reshape(KV_LORA, N_DEV, -1).transpose(1, 0, 2),
        w_out=lambda g: g.reshape(N_DEV, -1, D_MODEL))
    in_flight = []

    def on_sharded_grads(l, g):
        names = [k for k in big if k in g]
        srcs = [by_dest[k](g[k]) for k in names]
        started = _split_start(srcs, True, srcs[0], "grads%d_%s_start" % (l, names[0]))
        in_flight.append((l, names, started[:4]))
        return started[4]

    small_in_flight = []

    def on_layer_grads(g, token):
        stacked = {k: jnp.stack(g[k]) for k, _ in _SMALL_LAYER}
        conv = jnp.stack([g["conv_w"][l][0:CONV_WIDTH] for l in range(DEPTH)]).reshape(CONV_ROWS, 128)
        packed = jnp.concatenate([_pack_small(stacked, _SMALL_LAYER), conv], axis=0)
        started = _split_start([packed], False, token, "layer_grads_start")
        small_in_flight.append(started[:4])
        return started[4]

    sq, grad_x, G = _local_step(x[0], positions[0], loss_target[0], emb_ln_g.reshape(1, -1),
                                emb_ln_b.reshape(1, -1), layer_weights, layer_weights_rest, on_sharded_grads,
                                on_layer_grads, first_after=rest0[4], embedded=(h0, h0b))
    loss = lax.psum(sq[0, 0] * (0.5 / D_MODEL), ("x", "y", "c"))

    res = {}
    landed = {}
    for l, names, started in in_flight:
        zones = _split_wait(*started, grad_x, "grads%d_%s_wait" % (l, names[0]))[1]
        for k, zone in zip(names, zones):
            landed[k, l] = zone
    w_in_res = None
    for l in reversed(range(DEPTH)):
        w_in_res = _adamw([landed["w_in", l]], t12(w_in), t12(m_w_in), t12(v_w_in), "adamw_w_in_%d" % l, W_SHARD, 512,
                          first_layer=l, into=w_in_res)
    res["w_in"] = tuple(t12(o) for o in w_in_res)
    for name, rows in (("w_uq", 256), ("w_ukv", 256), ("w_out", 128)):
        res[name] = _adamw([landed[name, l] for l in range(DEPTH)], weights[name], mom1[name], mom2[name],
                           "adamw_" + name, rows)

    layer_zone = _split_wait(*small_in_flight[0], grad_x, "layer_grads_wait")[1][0]
    emb_zone = _all_gather([_pack_small(G, _SMALL_EMB)], "emb_grads_all_gather")[0]
    n_layer_rows = sum(r for _, r in _SMALL_LAYER)
    l_small = jnp.concatenate([emb_zone, layer_zone[:, 0:n_layer_rows]], axis=1)
    my_idx = 4 * lax.axis_index("x") + 2 * lax.axis_index("y") + lax.axis_index("c")
    conv_all_grads = layer_zone[:, n_layer_rows:].reshape(N_DEV, DEPTH * CONV_WIDTH, D_CONV)
    l_conv = lax.dynamic_slice_in_dim(conv_all_grads, my_idx * 64, 64, axis=2)
    l_conv = jnp.zeros((N_DEV, 8, 128), F32).at[:, 0:DEPTH * CONV_WIDTH, 0:64].set(l_conv)
    conv_shard = lambda a: jnp.zeros((8, 128), F32).at[0:DEPTH * CONV_WIDTH, 0:64].set(a.reshape(-1, 64))
    conv_res = _adamw([l_conv], conv_shard(conv_w)[None], conv_shard(m_conv_w)[None], conv_shard(v_conv_w)[None],
                      "adamw_conv_w", 8)
    res["conv_w"] = tuple(o[0, 0:DEPTH * CONV_WIDTH, 0:64].reshape(DEPTH, CONV_WIDTH, 64) for o in conv_res)
    small_res = _adamw([l_small], _pack_small(weights)[None], _pack_small(mom1)[None], _pack_small(mom2)[None],
                       "adamw_small", 392)
    shapes = {k: weights[k].shape for k, _ in _SMALL}
    unpacked = [_unpack_small(o[0], shapes) for o in small_res]
    for k, _ in _SMALL:
        res[k] = tuple(u[k] for u in unpacked)

    order = ("emb_ln_g", "emb_ln_b", "w_in", "q_norm_g", "kv_norm_g", "w_uq", "w_ukv", "w_pool", "pool_scale",
             "conv_w", "w_out", "b_out", "ln_g", "ln_b")
    return (loss, grad_x[None], *[res[k][0] for k in order], *[res[k][1] for k in order],
            *[res[k][2] for k in order], *[res[k][3] for k in order])
```

```python
import jax
import jax.numpy as jnp
from jax import lax
from jax.experimental import pallas as pl
from jax.experimental.pallas import tpu as pltpu

F32 = jnp.float32
BF16 = jnp.bfloat16

N_DEV = 8
D_MODEL = 2048
DEPTH = 2
N_HEADS = 8
NOPE = 128
ROPE = 64
V_DIM = 128
Q_LORA = 512
KV_LORA = 256
D_MLA = N_HEADS * V_DIM
D_POOL = 512
D_CONV = 512
POOL_WINDOWS = (2, 4, 8, 16)
POOL_GROUP = 128
CONV_WIDTH = 3
D_MIX = D_MLA + D_POOL + D_CONV
D_IN_PROJ = 4928
ROPE_THETA = 10000.0
LN_EPS = 1e-5
RMS_EPS = 1e-6
ALPHA = (2 * DEPTH) ** 0.25
ATTN_SCALE = (NOPE + ROPE) ** -0.5
ADAM_LR = 0.001
ADAM_B1 = 0.9
ADAM_B2 = 0.999
ADAM_EPS = 1e-08
ADAM_WD = 0.01
ADAM_STEP = 10

O_GMLA, O_PIN, O_GPOOL, O_CH, O_CB, O_CC, O_GCONV, O_QLAT, O_KVLAT, O_KROPE = (
    0, 1024, 1536, 2048, 2560, 3072, 3584, 4096, 4608, 4864)
NPP = 5120
N_GATED = O_QLAT
QC = NOPE + 2 * ROPE
HALO = 16
ATT_TILE = 512
ATT_CH = 256
LOG2E = 1.4426950408889634
EXP2_SCALE = ATTN_SCALE * LOG2E

GRAD_XFER = BF16
VMEM_LIMIT = 48 * 1024 * 1024
ATT_BWD_VMEM_LIMIT = 58 * 1024 * 1024
MESH_ID = pl.DeviceIdType.MESH


def _params(sem=None):
    return pltpu.CompilerParams(dimension_semantics=sem, vmem_limit_bytes=VMEM_LIMIT)


def _sigmoid(x):
    return 1.0 / (1.0 + jnp.exp(-x))


def _tile(dim, target):
    if dim <= target:
        return dim
    t = target - target % 128
    while dim % t:
        t -= 128
    return t


_DIMS = {"nn": (((1,), (0,)), ((), ())), "nt": (((1,), (1,)), ((), ())), "tn": (((0,), (0,)), ((), ()))}


def _mm(a, b, mode, out_dtype, name, res=None, bias=None, alpha=1.0, tm=1024, tn=1024, tk=2048, after=None):
    if mode == "nn":
        (M, K), (K2, N) = a.shape, b.shape
    elif mode == "nt":
        (M, K), (N, K2) = a.shape, b.shape
    else:
        (K, M), (K2, N) = a.shape, b.shape
    assert K == K2
    tm, tn, tk = _tile(M, tm), _tile(N, tn), _tile(K, tk)
    nk = K // tk
    has_res, has_bias = res is not None, bias is not None

    def body(*refs):
        a_ref, b_ref = refs[0], refs[1]
        pos = 2
        res_ref = bias_ref = None
        if has_res:
            res_ref = refs[pos]
            pos += 1
        if has_bias:
            bias_ref = refs[pos]
            pos += 1
        def finish(r, o_ref):
            if has_bias:
                r = r + bias_ref[...]
            if has_res:
                r = alpha * res_ref[...] + r
            o_ref[...] = r.astype(out_dtype)

        part = lax.dot_general(a_ref[...].astype(BF16), b_ref[...].astype(BF16), _DIMS[mode],
                               preferred_element_type=F32)
        if nk == 1:
            finish(part, refs[-1])
            return
        o_ref, acc_ref = refs[-2], refs[-1]
        k = pl.program_id(2)

        @pl.when(k == 0)
        def _():
            acc_ref[...] = part

        @pl.when(jnp.logical_and(k > 0, k < nk - 1))
        def _():
            acc_ref[...] += part

        @pl.when(k == nk - 1)
        def _():
            finish(acc_ref[...] + part, o_ref)

    if mode == "nn":
        in_specs = [pl.BlockSpec((tm, tk), lambda i, j, k: (i, k)), pl.BlockSpec((tk, tn), lambda i, j, k: (k, j))]
    elif mode == "nt":
        in_specs = [pl.BlockSpec((tm, tk), lambda i, j, k: (i, k)), pl.BlockSpec((tn, tk), lambda i, j, k: (j, k))]
    else:
        in_specs = [pl.BlockSpec((tk, tm), lambda i, j, k: (k, i)), pl.BlockSpec((tk, tn), lambda i, j, k: (k, j))]
    args = [a, b]
    if has_res:
        in_specs.append(pl.BlockSpec((tm, tn), lambda i, j, k: (i, j)))
        args.append(res)
    if has_bias:
        in_specs.append(pl.BlockSpec((1, tn), lambda i, j, k: (0, j)))
        args.append(bias)
    if after is not None:
        in_specs.append(pl.BlockSpec((8, 128), lambda i, j, k: (0, 0)))
        args.append(after)
    return pl.pallas_call(
        body, name=name,
        out_shape=jax.ShapeDtypeStruct((M, N), out_dtype),
        grid=(M // tm, N // tn, nk),
        in_specs=in_specs,
        out_specs=pl.BlockSpec((tm, tn), lambda i, j, k: (i, j)),
        scratch_shapes=[pltpu.VMEM((tm, tn), F32)] if nk > 1 else [],
        compiler_params=_params(("parallel", "parallel", "arbitrary")),
    )(*args)


def _ln_fwd(z, g, b, name, tq=512):
    T, D = z.shape

    def body(z_ref, g_ref, b_ref, y_ref, yb_ref):
        zv = z_ref[...]
        mu = jnp.mean(zv, axis=1, keepdims=True)
        zc = zv - mu
        var = jnp.mean(zc * zc, axis=1, keepdims=True)
        y = zc * lax.rsqrt(var + LN_EPS) * g_ref[...] + b_ref[...]
        y_ref[...] = y
        yb_ref[...] = y.astype(BF16)

    row = pl.BlockSpec((tq, D), lambda i: (i, 0))
    vec = pl.BlockSpec((1, D), lambda i: (0, 0))
    return pl.pallas_call(
        body, name=name,
        out_shape=(jax.ShapeDtypeStruct((T, D), F32), jax.ShapeDtypeStruct((T, D), BF16)),
        grid=(T // tq,), in_specs=[row, vec, vec], out_specs=(row, row),
        compiler_params=_params(("parallel",)),
    )(z, g, b)


def _ln_bwd(dy, z, g, name, tq=512, target=None, for_matmul=True):
    T, D = z.shape
    with_loss = target is not None

    def body(*refs):
        dy_ref, z_ref, g_ref = refs[:3]
        outs = list(refs[4 if with_loss else 3:])
        dz_ref = outs.pop(0)
        dzb_ref, ds_ref = (outs.pop(0), outs.pop(0)) if for_matmul else (None, None)
        dg_ref, db_ref = outs.pop(0), outs.pop(0)
        sq_ref = outs.pop(0) if with_loss else None

        @pl.when(pl.program_id(0) == 0)
        def _():
            for ref in (dg_ref, db_ref, ds_ref, sq_ref):
                if ref is not None:
                    ref[...] = jnp.zeros_like(ref)

        zv, dyv = z_ref[...], dy_ref[...]
        if with_loss:
            err = dyv - refs[3][...]
            sq_ref[...] += jnp.sum(err * err)
            dyv = err * (1.0 / D)
        mu = jnp.mean(zv, axis=1, keepdims=True)
        zc = zv - mu
        var = jnp.mean(zc * zc, axis=1, keepdims=True)
        rstd = lax.rsqrt(var + LN_EPS)
        xh = zc * rstd
        u = dyv * g_ref[...]
        dz = rstd * (u - jnp.mean(u, axis=1, keepdims=True) - xh * jnp.mean(u * xh, axis=1, keepdims=True))
        dz_ref[...] = dz
        dg_ref[...] += jnp.sum(dyv * xh, axis=0, keepdims=True)
        db_ref[...] += jnp.sum(dyv, axis=0, keepdims=True)
        if for_matmul:
            dzb_ref[...] = dz.astype(BF16)
            ds_ref[...] += jnp.sum(dz, axis=0, keepdims=True)

    row = pl.BlockSpec((tq, D), lambda i: (i, 0))
    vec = pl.BlockSpec((1, D), lambda i: (0, 0))
    vshape = jax.ShapeDtypeStruct((1, D), F32)
    out_shape, out_specs = [jax.ShapeDtypeStruct((T, D), F32)], [row]
    if for_matmul:
        out_shape += [jax.ShapeDtypeStruct((T, D), BF16), vshape]
        out_specs += [row, vec]
    out_shape += [vshape, vshape]
    out_specs += [vec, vec]
    if with_loss:
        out_shape.append(jax.ShapeDtypeStruct((8, 128), F32))
        out_specs.append(pl.BlockSpec((8, 128), lambda i: (0, 0)))
    return pl.pallas_call(
        body, name=name, out_shape=tuple(out_shape), grid=(T // tq,),
        in_specs=[row, row, vec] + ([row] if with_loss else []), out_specs=tuple(out_specs),
        compiler_params=_params(("arbitrary",)),
    )(dy, z, g, *([target] if with_loss else []))


def _pblock(tq, width, offset):
    assert offset % width == 0
    blk = offset // width
    return pl.BlockSpec((tq, width), lambda i: (i, blk))


def _mix_fwd(proj, q_g, kv_g, w_pool, pool_scale, conv_w, name, tq=256):
    T = proj.shape[0]

    def body(ql_ref, kvl_ref, pin_ref, gp_ref, ch_ref, cb_ref, cc_ref, gc_ref, qg_ref, kvg_ref, wp_ref, ps_ref,
             cw_ref, qn_ref, kvn_ref, pooled_ref, cv_ref, ypc_ref, extp, extu):
        i = pl.program_id(0)
        for x_ref, g_ref, o_ref in ((ql_ref, qg_ref, qn_ref), (kvl_ref, kvg_ref, kvn_ref)):
            x = x_ref[...]
            r = lax.rsqrt(jnp.mean(x * x, axis=1, keepdims=True) + RMS_EPS)
            o_ref[...] = (x * r * g_ref[...]).astype(BF16)

        @pl.when(i == 0)
        def _():
            extp[0:HALO, :] = jnp.zeros((HALO, D_POOL), F32)
            extu[0:HALO, :] = jnp.zeros((HALO, D_CONV), F32)

        @pl.when(i > 0)
        def _():
            extp[0:HALO, :] = extp[tq:tq + HALO, :]
            extu[0:HALO, :] = extu[tq:tq + HALO, :]

        t1 = (i * tq + lax.broadcasted_iota(jnp.int32, (tq, 1), 0) + 1).astype(F32)
        for g, w in enumerate(POOL_WINDOWS):
            cols = slice(g * POOL_GROUP, (g + 1) * POOL_GROUP)
            pin = pin_ref[:, cols]
            extp[HALO:, cols] = pin
            s = extp[:, cols]
            k = 1
            while k < w:
                s = s + pltpu.roll(s, k, 0)
                k *= 2
            mean = s[HALO:, :] / jnp.minimum(t1, float(w))
            pooled = (mean - pin).astype(BF16)
            pooled_ref[:, cols] = pooled
            r = jnp.dot(pooled, wp_ref[g], preferred_element_type=F32)
            gp = gp_ref[:, cols]
            ypc_ref[:, cols] = (r * ps_ref[:, cols] * (gp * _sigmoid(gp))).astype(BF16)
        for g in range(D_CONV // 128):
            cols = slice(g * 128, (g + 1) * 128)
            u = cc_ref[:, cols] * ch_ref[:, cols]
            extu[HALO:, cols] = u
            eu = extu[:, cols]
            u1 = pltpu.roll(eu, 1, 0)[HALO:, :]
            u2 = pltpu.roll(eu, 2, 0)[HALO:, :]
            cv = cw_ref[0:1, cols] * u2 + cw_ref[1:2, cols] * u1 + cw_ref[2:3, cols] * u
            cv_ref[:, cols] = cv
            gc = gc_ref[:, cols]
            ypc_ref[:, D_POOL + g * 128:D_POOL + (g + 1) * 128] = (
                cb_ref[:, cols] * cv * (gc * _sigmoid(gc))).astype(BF16)

    full = lambda shape: pl.BlockSpec(shape, lambda i: (0,) * len(shape))
    row = lambda w: pl.BlockSpec((tq, w), lambda i: (i, 0))
    return pl.pallas_call(
        body, name=name,
        out_shape=(jax.ShapeDtypeStruct((T, Q_LORA), BF16), jax.ShapeDtypeStruct((T, KV_LORA), BF16),
                   jax.ShapeDtypeStruct((T, D_POOL), BF16), jax.ShapeDtypeStruct((T, D_CONV), F32),
                   jax.ShapeDtypeStruct((T, D_MIX), BF16)),
        grid=(T // tq,),
        in_specs=[_pblock(tq, Q_LORA, O_QLAT), _pblock(tq, KV_LORA, O_KVLAT), _pblock(tq, 512, O_PIN),
                  _pblock(tq, 512, O_GPOOL), _pblock(tq, 512, O_CH), _pblock(tq, 512, O_CB), _pblock(tq, 512, O_CC),
                  _pblock(tq, 512, O_GCONV), full((1, Q_LORA)), full((1, KV_LORA)), full((4, 128, 128)),
                  full((1, D_POOL)), full((8, D_CONV))],
        out_specs=(row(Q_LORA), row(KV_LORA), row(D_POOL), row(D_CONV),
                   pl.BlockSpec((tq, D_POOL + D_CONV), lambda i: (i, D_MLA // (D_POOL + D_CONV)))),
        scratch_shapes=[pltpu.VMEM((tq + HALO, D_POOL), F32), pltpu.VMEM((tq + HALO, D_CONV), F32)],
        compiler_params=_params(("arbitrary",)),
    )(proj, proj, proj, proj, proj, proj, proj, proj, q_g, kv_g, w_pool, pool_scale, conv_w)


def _mix_bwd(dmix, proj, o, pooled, cv, w_pool, pool_scale, conv_w, name, tq=ATT_CH):
    T = proj.shape[0]
    nt = T // tq
    n_ext = tq + HALO

    def body(dym_ref, dyp_ref, dyc_ref, gm_ref, gp_ref, ch_ref, cb_ref, cc_ref, gc_ref, o_ref, pooled_ref, cv_ref,
             wp_ref, ps_ref, cw_ref, do_ref, delta_ref, dg_ref, dwp_ref, dps_ref, dcw_ref, exte, extd):
        i = pl.program_id(0)
        tile = nt - 1 - i

        @pl.when(i == 0)
        def _():
            dwp_ref[...] = jnp.zeros_like(dwp_ref)
            dps_ref[...] = jnp.zeros_like(dps_ref)
            dcw_ref[...] = jnp.zeros_like(dcw_ref)
            exte[tq:, :] = jnp.zeros((HALO, D_POOL), F32)
            extd[tq:, :] = jnp.zeros((HALO, D_CONV), F32)

        @pl.when(i > 0)
        def _():
            exte[tq:, :] = exte[0:HALO, :]
            extd[tq:, :] = extd[0:HALO, :]

        ones = jnp.ones((8, V_DIM), F32)
        for h in range(N_HEADS):
            cols = slice(h * V_DIM, (h + 1) * V_DIM)
            gm = gm_ref[:, cols]
            sig = _sigmoid(gm)
            dym = dym_ref[:, cols]
            ov = o_ref[:, cols]
            do = dym * (gm * sig)
            do_ref[:, cols] = do.astype(BF16)
            rows = lax.dot_general(ones, do * ov, _DIMS["nt"], precision=lax.Precision.HIGHEST,
                                   preferred_element_type=F32)
            delta_ref[h, 0] = rows[0:1, :]
            dg_ref[:, O_GMLA + h * V_DIM:O_GMLA + (h + 1) * V_DIM] = (
                dym * ov * (sig * (1.0 + gm * (1.0 - sig)))).astype(BF16)

        t1 = (tile * tq + lax.broadcasted_iota(jnp.int32, (tq, 1), 0) + 1).astype(F32)
        for g, w in enumerate(POOL_WINDOWS):
            cols = slice(g * POOL_GROUP, (g + 1) * POOL_GROUP)
            pg = pooled_ref[:, cols]
            r = jnp.dot(pg, wp_ref[g], preferred_element_type=F32)
            gp = gp_ref[:, cols]
            sg = _sigmoid(gp)
            sl = gp * sg
            dyg = dyp_ref[:, cols]
            ps = ps_ref[:, cols]
            dg_ref[:, O_GPOOL + g * POOL_GROUP:O_GPOOL + (g + 1) * POOL_GROUP] = (
                dyg * (r * ps) * (sg * (1.0 + gp * (1.0 - sg)))).astype(BF16)
            dps_ref[:, cols] += jnp.sum(dyg * r * sl, axis=0, keepdims=True)
            dr = (dyg * ps * sl).astype(BF16)
            dwp_ref[g] += lax.dot_general(pg, dr, _DIMS["tn"], preferred_element_type=F32)
            dpooled = lax.dot_general(dr, wp_ref[g], _DIMS["nt"], preferred_element_type=F32)
            exte[0:tq, cols] = dpooled / jnp.minimum(t1, float(w))
            s = exte[:, cols]
            k = 1
            while k < w:
                s = s + pltpu.roll(s, n_ext - k, 0)
                k *= 2
            dg_ref[:, O_PIN + g * POOL_GROUP:O_PIN + (g + 1) * POOL_GROUP] = (s[0:tq, :] - dpooled).astype(BF16)

        for g in range(D_CONV // 128):
            cols = slice(g * 128, (g + 1) * 128)
            out = lambda base: slice(base + g * 128, base + (g + 1) * 128)
            gc = gc_ref[:, cols]
            sg = _sigmoid(gc)
            sl = gc * sg
            dyc = dyc_ref[:, cols]
            cb, cc, ch, cvv = cb_ref[:, cols], cc_ref[:, cols], ch_ref[:, cols], cv_ref[:, cols]
            dcv = dyc * cb * sl
            dg_ref[:, out(O_GCONV)] = (dyc * (cb * cvv) * (sg * (1.0 + gc * (1.0 - sg)))).astype(BF16)
            dg_ref[:, out(O_CB)] = (dyc * cvv * sl).astype(BF16)
            extd[0:tq, cols] = dcv
            ed = extd[:, cols]
            d1 = pltpu.roll(ed, n_ext - 1, 0)[0:tq, :]
            d2 = pltpu.roll(ed, n_ext - 2, 0)[0:tq, :]
            du = cw_ref[2:3, cols] * dcv + cw_ref[1:2, cols] * d1 + cw_ref[0:1, cols] * d2
            u = cc * ch
            dcw_ref[0:1, cols] += jnp.sum(u * d2, axis=0, keepdims=True)
            dcw_ref[1:2, cols] += jnp.sum(u * d1, axis=0, keepdims=True)
            dcw_ref[2:3, cols] += jnp.sum(u * dcv, axis=0, keepdims=True)
            dg_ref[:, out(O_CH)] = (du * cc).astype(BF16)
            dg_ref[:, out(O_CC)] = (du * ch).astype(BF16)

    def rblock(width, offset):
        assert offset % width == 0
        blk = offset // width
        return pl.BlockSpec((tq, width), lambda i: (nt - 1 - i, blk))

    full = lambda shape: pl.BlockSpec(shape, lambda i: (0,) * len(shape))
    return pl.pallas_call(
        body, name=name,
        out_shape=(jax.ShapeDtypeStruct((T, D_MLA), BF16), jax.ShapeDtypeStruct((N_HEADS, nt, 1, tq), F32),
                   jax.ShapeDtypeStruct((T, NPP), BF16),
                   jax.ShapeDtypeStruct((4, 128, 128), F32), jax.ShapeDtypeStruct((1, D_POOL), F32),
                   jax.ShapeDtypeStruct((8, D_CONV), F32)),
        grid=(nt,),
        in_specs=[rblock(1024, 0), rblock(512, 1024), rblock(512, 1536),
                  rblock(1024, O_GMLA), rblock(512, O_GPOOL), rblock(512, O_CH), rblock(512, O_CB),
                  rblock(512, O_CC), rblock(512, O_GCONV), rblock(1024, 0), rblock(512, 0), rblock(512, 0),
                  full((4, 128, 128)), full((1, D_POOL)), full((8, D_CONV))],
        out_specs=(rblock(1024, 0), pl.BlockSpec((N_HEADS, 1, 1, tq), lambda i: (0, nt - 1 - i, 0, 0)),
                   rblock(N_GATED, 0), full((4, 128, 128)), full((1, D_POOL)), full((8, D_CONV))),
        scratch_shapes=[pltpu.VMEM((n_ext, D_POOL), F32), pltpu.VMEM((n_ext, D_CONV), F32)],
        compiler_params=_params(("arbitrary",)),
    )(dmix, dmix, dmix, proj, proj, proj, proj, proj, proj, o, pooled, cv, w_pool, pool_scale, conv_w)


def _up_rms_bwd(proj, dq, dkv, w_uq, w_ukv, dkrope, dproj, q_g, kv_g, name, tq=256):
    T = proj.shape[0]
    n_lat = NPP - N_GATED

    def body(ql_ref, kvl_ref, dq_ref, dkv_ref, wq_ref, wkv_ref, dkr_ref, _, qg_ref, kvg_ref, dlat_ref, dqg_ref,
             dkvg_ref):
        @pl.when(pl.program_id(0) == 0)
        def _():
            dqg_ref[...] = jnp.zeros_like(dqg_ref)
            dkvg_ref[...] = jnp.zeros_like(dkvg_ref)

        dqn = lax.dot_general(dq_ref[...], wq_ref[...], _DIMS["nt"], preferred_element_type=F32)
        dkvn = lax.dot_general(dkv_ref[...], wkv_ref[...], _DIMS["nt"], preferred_element_type=F32)
        for x_ref, dy, g_ref, c0, dg_ref in ((ql_ref, dqn, qg_ref, 0, dqg_ref),
                                             (kvl_ref, dkvn, kvg_ref, Q_LORA, dkvg_ref)):
            x = x_ref[...]
            r = lax.rsqrt(jnp.mean(x * x, axis=1, keepdims=True) + RMS_EPS)
            xr = x * r
            u = dy * g_ref[...]
            dlat_ref[:, c0:c0 + x.shape[1]] = (r * (u - xr * jnp.mean(u * xr, axis=1, keepdims=True))).astype(BF16)
            dg_ref[...] += jnp.sum(dy * xr, axis=0, keepdims=True)
        dlat_ref[:, Q_LORA + KV_LORA:] = dkr_ref[...]

    row = lambda w: pl.BlockSpec((tq, w), lambda i: (i, 0))
    vec = lambda w: pl.BlockSpec((1, w), lambda i: (0, 0))
    assert N_GATED % n_lat == 0
    return pl.pallas_call(
        body, name=name,
        out_shape=(jax.ShapeDtypeStruct((T, NPP), BF16),
                   jax.ShapeDtypeStruct((1, Q_LORA), F32), jax.ShapeDtypeStruct((1, KV_LORA), F32)),
        grid=(T // tq,),
        in_specs=[_pblock(tq, Q_LORA, O_QLAT), _pblock(tq, KV_LORA, O_KVLAT), row(dq.shape[1]), row(dkv.shape[1]),
                  pl.BlockSpec(w_uq.shape, lambda i: (0, 0)), pl.BlockSpec(w_ukv.shape, lambda i: (0, 0)),
                  row(n_lat - Q_LORA - KV_LORA), pl.BlockSpec(memory_space=pl.ANY), vec(Q_LORA), vec(KV_LORA)],
        out_specs=(pl.BlockSpec((tq, n_lat), lambda i: (i, N_GATED // n_lat)), vec(Q_LORA), vec(KV_LORA)),
        input_output_aliases={7: 0},
        compiler_params=_params(("arbitrary",)),
    )(proj, proj, dq, dkv, w_uq, w_ukv, dkrope, dproj, q_g, kv_g)


def _swap_halves(x, lo):
    return jnp.where(lo, pltpu.roll(x, 96, 1), pltpu.roll(x, 32, 1))


def _up_rope_fwd(qn, kvn, w_uq, w_ukv, proj, cos_t, sin_t, name, tq=256, after=None):
    T = qn.shape[0]

    def body(qn_ref, kvn_ref, wq_ref, wkv_ref, kr_ref, c_ref, s_ref, *rest):
        kv_ref, qc_ref, kc_ref = rest[-3:]
        q = jnp.dot(qn_ref[...], wq_ref[...], preferred_element_type=F32)
        kv_ref[...] = jnp.dot(kvn_ref[...], wkv_ref[...], preferred_element_type=F32).astype(BF16)
        C, S = c_ref[...], s_ref[...]
        lane = lax.broadcasted_iota(jnp.int32, (tq, 128), 1)
        lo = (lane % ROPE) < (ROPE // 2)
        first = lane < ROPE

        def rope(x):
            return x * C + _swap_halves(x, lo) * S

        kr = jnp.where(first, rope(kr_ref[...]), 0.0).astype(BF16)
        n_nope = N_HEADS * NOPE
        for j in range(N_HEADS // 2):
            r = rope(q[:, n_nope + j * 128:n_nope + (j + 1) * 128])
            pair = (jnp.where(first, r, 0.0), jnp.where(first, pltpu.roll(r, 64, 1), 0.0))
            for hh in range(2):
                h = 2 * j + hh
                qc_ref[h, :, 0:NOPE] = q[:, h * NOPE:(h + 1) * NOPE].astype(BF16)
                qc_ref[h, :, NOPE:QC] = pair[hh].astype(BF16)
        for h in range(N_HEADS):
            kc_ref[h, :, 0:NOPE] = kv_ref[:, h * 256:h * 256 + NOPE]
            kc_ref[h, :, NOPE:QC] = kr

    out = jax.ShapeDtypeStruct((N_HEADS, T, QC), BF16)
    hblock = pl.BlockSpec((N_HEADS, tq, QC), lambda i: (0, i, 0))
    row = lambda w: pl.BlockSpec((tq, w), lambda i: (i, 0))
    full = lambda a: pl.BlockSpec(a.shape, lambda i: (0, 0))
    return pl.pallas_call(
        body, name=name, out_shape=(jax.ShapeDtypeStruct((T, 2 * D_MLA), BF16), out, out), grid=(T // tq,),
        in_specs=[row(Q_LORA), row(KV_LORA), full(w_uq), full(w_ukv), _pblock(tq, 128, O_KROPE), row(128), row(128)]
        + ([pl.BlockSpec((8, 128), lambda i: (0, 0))] if after is not None else []),
        out_specs=(row(2 * D_MLA), hblock, hblock),
        compiler_params=_params(("parallel",)),
    )(qn, kvn, w_uq, w_ukv, proj, cos_t, sin_t, *([after] if after is not None else []))


def _rope_bwd(dqc, dkr, cos_t, sin_t, name, tq=256):
    T = dqc.shape[1]

    def body(dqc_ref, dkr_ref, c_ref, s_ref, dq_ref, dk_ref):
        C, S = c_ref[...], s_ref[...]
        lane = lax.broadcasted_iota(jnp.int32, (tq, 128), 1)
        lo = (lane % ROPE) < (ROPE // 2)
        first = lane < ROPE

        def unrope(dy):
            return dy * C - _swap_halves(dy, lo) * S

        acc = dkr_ref[0]
        for h in range(1, N_HEADS):
            acc = acc + dkr_ref[h]
        dk_ref[:, 0:128] = jnp.where(first, unrope(acc), 0.0).astype(BF16)
        dk_ref[:, 128:256] = jnp.zeros((tq, 128), BF16)
        for j in range(N_HEADS // 2):
            d0 = dqc_ref[2 * j, :, NOPE:QC]
            d1 = dqc_ref[2 * j + 1, :, NOPE:QC]
            comb = jnp.where(first, d0, pltpu.roll(d1, 64, 1))
            dq_ref[:, 1024 + j * 128:1024 + (j + 1) * 128] = unrope(comb).astype(BF16)
        for h in range(N_HEADS):
            dq_ref[:, h * NOPE:(h + 1) * NOPE] = dqc_ref[h, :, 0:NOPE].astype(BF16)

    tab = pl.BlockSpec((tq, 128), lambda i: (i, 0))
    return pl.pallas_call(
        body, name=name,
        out_shape=(jax.ShapeDtypeStruct((T, 1536), BF16), jax.ShapeDtypeStruct((T, 256), BF16)),
        grid=(T // tq,),
        in_specs=[pl.BlockSpec((N_HEADS, tq, QC), lambda i: (0, i, 0)),
                  pl.BlockSpec((N_HEADS, tq, 128), lambda i: (0, i, 0)), tab, tab],
        out_specs=(pl.BlockSpec((tq, 1536), lambda i: (i, 0)), pl.BlockSpec((tq, 256), lambda i: (i, 0))),
        compiler_params=_params(("parallel",)),
    )(dqc, dkr, cos_t, sin_t)


def _flash_fwd(qc, kc, kv, proj, mix, name):
    H, T, _ = qc.shape
    tt = ATT_TILE
    nt = T // tt
    sp = tt // ATT_CH
    pairs = [(i, c, int(c == i)) for i in range(nt) for c in range(i + 1)]
    assert len(pairs) % 2 == 0
    table = jnp.asarray(pairs + [pairs[-1]], jnp.int32)

    def body(tab_ref, q_ref, k_ref, v_ref, g_ref, _, o_ref, y_ref, lse_ref, vt_sc, s_sc, acc_sc, m_sc, l_sc, bias_sc):
        def issue(p, slot):
            i, c = tab_ref[p, 0], tab_ref[p, 1]
            s_sc[slot] = lax.dot_general(k_ref[0, pl.ds(pl.multiple_of(c * tt, tt), tt), :],
                                         q_ref[0, pl.ds(pl.multiple_of(i * tt, tt), tt), :], _DIMS["nt"],
                                         preferred_element_type=F32)

        def softmax_pv(p, slot):
            i, c, diag = tab_ref[p, 0], tab_ref[p, 1], tab_ref[p, 2]
            s = s_sc[slot] + bias_sc[diag]
            m = m_sc[i]
            m_new = jnp.maximum(m, jnp.max(s, axis=0, keepdims=True))
            pr = jnp.exp2((s - m_new) * EXP2_SCALE)
            a = jnp.exp2((m - m_new) * EXP2_SCALE)
            l_sc[i] = a * l_sc[i] + jnp.sum(pr, axis=0, keepdims=True)
            acc_sc[i] = a * acc_sc[i] + jnp.dot(vt_sc[c], pr.astype(BF16), preferred_element_type=F32)
            m_sc[i] = m_new

        issue(0, 0)
        m_sc[...] = jnp.full_like(m_sc, -jnp.inf)
        l_sc[...] = jnp.zeros_like(l_sc)
        acc_sc[...] = jnp.zeros_like(acc_sc)
        krow = lax.broadcasted_iota(jnp.int32, (tt, tt), 0)
        qcol = lax.broadcasted_iota(jnp.int32, (tt, tt), 1)
        bias_sc[0] = jnp.zeros((tt, tt), F32)
        bias_sc[1] = jnp.where(krow <= qcol, 0.0, -jnp.inf)
        for c in range(nt):
            vt_sc[c] = v_ref[c * tt:(c + 1) * tt, :].astype(F32).T.astype(BF16)

        def two(u, carry):
            p = 2 * u
            issue(p + 1, 1)
            softmax_pv(p, 0)
            issue(p + 2, 0)
            softmax_pv(p + 1, 1)
            return carry

        lax.fori_loop(0, len(pairs) // 2, two, 0)
        for i in range(nt):
            rows = slice(i * tt, (i + 1) * tt)
            l = l_sc[i]
            o = (acc_sc[i] / l).T
            o_ref[rows, :] = o
            lse = m_sc[i] * ATTN_SCALE + jnp.log(l)
            for r in range(sp):
                lse_ref[0, sp * i + r] = lse[:, r * ATT_CH:(r + 1) * ATT_CH]
            g = g_ref[rows, :]
            y_ref[rows, :] = (o * (g * _sigmoid(g))).astype(BF16)

    head = lambda h, tab: (h, 0, 0)
    col = lambda h, tab: (0, h)
    return pl.pallas_call(
        body, name=name,
        out_shape=(jax.ShapeDtypeStruct((T, D_MLA), F32), jax.ShapeDtypeStruct((T, D_MIX), BF16),
                   jax.ShapeDtypeStruct((H, T // ATT_CH, 1, ATT_CH), F32)),
        grid_spec=pltpu.PrefetchScalarGridSpec(
            num_scalar_prefetch=1, grid=(H,),
            in_specs=[pl.BlockSpec((1, T, QC), head), pl.BlockSpec((1, T, QC), head),
                      pl.BlockSpec((T, V_DIM), lambda h, tab: (0, 2 * h + 1)), pl.BlockSpec((T, V_DIM), col),
                      pl.BlockSpec(memory_space=pl.ANY)],
            out_specs=(pl.BlockSpec((T, V_DIM), col), pl.BlockSpec((T, V_DIM), col),
                       pl.BlockSpec((1, T // ATT_CH, 1, ATT_CH), lambda h, tab: (h, 0, 0, 0))),
            scratch_shapes=[pltpu.VMEM((nt, V_DIM, tt), BF16), pltpu.VMEM((2, tt, tt), F32),
                            pltpu.VMEM((nt, V_DIM, tt), F32), pltpu.VMEM((nt, 1, tt), F32),
                            pltpu.VMEM((nt, 1, tt), F32), pltpu.VMEM((2, tt, tt), F32)]),
        input_output_aliases={5: 1},
        compiler_params=_params(("arbitrary",)),
    )(table, qc, kc, kv, proj, mix)


def _flash_bwd(qc, kc, kv, do, lse, delta, name):
    H, T, _ = qc.shape
    tt = ATT_TILE
    nt = T // tt
    sp = tt // ATT_CH
    pairs = [(j, c) for j in range(nt) for c in range(j, nt)]
    assert len(pairs) % 2 == 0
    table = jnp.asarray(pairs + [pairs[-1]], jnp.int32)

    def body(tab_ref, q_ref, k_ref, v_ref, do_ref, lse_ref, dl_ref, dq_ref, dkv_ref, dkr_ref, dqt_sc, dk_sc, dv_sc,
             s_sc, dp_sc, kt_sc, bias_sc):
        def operands(j, c):
            k0, q0 = pl.multiple_of(j * tt, tt), pl.multiple_of(c * tt, tt)
            return (k_ref[0, pl.ds(k0, tt), :], v_ref[pl.ds(k0, tt), :], q_ref[0, pl.ds(q0, tt), :],
                    do_ref[pl.ds(q0, tt), :])

        def stat_row(ref, c):
            return jnp.concatenate([ref[0, sp * c + r] for r in range(sp)], axis=1)

        def early(p, slot):
            k, v, q, dov = operands(tab_ref[p, 0], tab_ref[p, 1])
            s_sc[slot] = lax.dot_general(k, q, _DIMS["nt"], preferred_element_type=F32)
            dp_sc[slot] = lax.dot_general(v, dov, _DIMS["nt"], preferred_element_type=F32)

        def late(p, slot):
            j, c = tab_ref[p, 0], tab_ref[p, 1]
            _, _, q, dov = operands(j, c)
            s = s_sc[slot] + jnp.where(j == c, bias_sc[...], 0.0)
            pr = jnp.exp2(s * EXP2_SCALE - stat_row(lse_ref, c) * LOG2E)
            ds = (pr * (dp_sc[slot] - stat_row(dl_ref, c)) * ATTN_SCALE).astype(BF16)
            dv_sc[j] += jnp.dot(pr.astype(BF16), dov, preferred_element_type=F32)
            dk_sc[j] += jnp.dot(ds, q, preferred_element_type=F32)
            dqt_sc[c] += jnp.dot(kt_sc[j], ds, preferred_element_type=F32)

        early(0, 0)
        dqt_sc[...] = jnp.zeros_like(dqt_sc)
        dk_sc[...] = jnp.zeros_like(dk_sc)
        dv_sc[...] = jnp.zeros_like(dv_sc)
        krow = lax.broadcasted_iota(jnp.int32, (tt, tt), 0)
        qcol = lax.broadcasted_iota(jnp.int32, (tt, tt), 1)
        bias_sc[...] = jnp.where(krow <= qcol, 0.0, -jnp.inf)
        for j in range(nt):
            kt_sc[j] = k_ref[0, j * tt:(j + 1) * tt, :].astype(F32).T.astype(BF16)

        def two(u, carry):
            p = 2 * u
            early(p + 1, 1)
            late(p, 0)
            early(p + 2, 0)
            late(p + 1, 1)
            return carry

        lax.fori_loop(0, len(pairs) // 2, two, 0)
        for j in range(nt):
            rows = slice(j * tt, (j + 1) * tt)
            dk = dk_sc[j]
            dkv_ref[rows, 0:NOPE] = dk[:, 0:NOPE].astype(BF16)
            dkv_ref[rows, NOPE:] = dv_sc[j].astype(BF16)
            dkr_ref[0, rows, :] = dk[:, NOPE:]
            dq_ref[0, rows, :] = dqt_sc[j].T

    head = lambda h, tab: (h, 0, 0)
    stat = pl.BlockSpec((1, T // ATT_CH, 1, ATT_CH), lambda h, tab: (h, 0, 0, 0))
    return pl.pallas_call(
        body, name=name,
        out_shape=(jax.ShapeDtypeStruct((H, T, QC), F32), jax.ShapeDtypeStruct((T, 2 * D_MLA), BF16),
                   jax.ShapeDtypeStruct((H, T, 128), F32)),
        grid_spec=pltpu.PrefetchScalarGridSpec(
            num_scalar_prefetch=1, grid=(H,),
            in_specs=[pl.BlockSpec((1, T, QC), head), pl.BlockSpec((1, T, QC), head),
                      pl.BlockSpec((T, V_DIM), lambda h, tab: (0, 2 * h + 1)),
                      pl.BlockSpec((T, V_DIM), lambda h, tab: (0, h)), stat, stat],
            out_specs=(pl.BlockSpec((1, T, QC), head), pl.BlockSpec((T, 256), lambda h, tab: (0, h)),
                       pl.BlockSpec((1, T, 128), head)),
            scratch_shapes=[pltpu.VMEM((nt, QC, tt), F32), pltpu.VMEM((nt, tt, QC), F32),
                            pltpu.VMEM((nt, tt, V_DIM), F32), pltpu.VMEM((2, tt, tt), F32),
                            pltpu.VMEM((2, tt, tt), F32), pltpu.VMEM((nt, QC, tt), BF16), pltpu.VMEM((tt, tt), F32)]),
        compiler_params=pltpu.CompilerParams(dimension_semantics=("arbitrary",),
                                             vmem_limit_bytes=ATT_BWD_VMEM_LIMIT),
    )(table, qc, kc, kv, do, lse, delta)


def _adamw(lands, w, m, v, name, rows, cols=None, first_layer=0, into=None):
    layers, R, C = w.shape
    L = len(lands)
    cols = C if cols is None else cols
    assert R % rows == 0 and C % cols == 0 and first_layer + L <= layers
    nc = C // cols
    nb = (R // rows) * nc
    c1 = 1.0 - ADAM_B1 ** ADAM_STEP
    c2 = 1.0 - ADAM_B2 ** ADAM_STEP

    def body(*refs):
        land_refs = refs[:L]
        w_ref, m_ref, v_ref = refs[L:L + 3]
        g_ref, d_ref, nm_ref, nv_ref, g_sc = refs[-5:]
        for ll in range(L):
            @pl.when(pl.program_id(0) == ll)
            def _(land_ref=land_refs[ll]):
                g = land_ref[0].astype(F32)
                for s in range(1, N_DEV):
                    g = g + land_ref[s].astype(F32)
                g_sc[...] = g

        g = g_sc[...]
        nm = ADAM_B1 * m_ref[0] + (1.0 - ADAM_B1) * g
        nv = ADAM_B2 * v_ref[0] + (1.0 - ADAM_B2) * (g * g)
        g_ref[0] = g
        nm_ref[0] = nm
        nv_ref[0] = nv
        d_ref[0] = -ADAM_LR * ((nm / c1) / (jnp.sqrt(nv / c2) + ADAM_EPS) + ADAM_WD * w_ref[0])

    def land_spec(ll):
        def index(l, i):
            i = jnp.where(l < ll, 0, jnp.where(l > ll, nb - 1, i))
            return (0, i // nc, i % nc)
        return pl.BlockSpec((N_DEV, rows, cols), index)

    blk = pl.BlockSpec((1, rows, cols), lambda l, i: (first_layer + l, i // nc, i % nc))
    out = jax.ShapeDtypeStruct((layers, R, C), F32)
    extra = [] if into is None else list(into)
    return pl.pallas_call(
        body, name=name, out_shape=(out, out, out, out), grid=(L, nb),
        in_specs=[land_spec(ll) for ll in range(L)] + [blk, blk, blk] + [pl.BlockSpec(memory_space=pl.ANY)] * len(extra),
        out_specs=(blk, blk, blk, blk),
        input_output_aliases={L + 3 + i: i for i in range(len(extra))},
        scratch_shapes=[pltpu.VMEM((rows, cols), F32)],
        compiler_params=_params(("arbitrary", "arbitrary")),
    )(*lands, w, m, v, *extra)


def _mesh_pos():
    return lax.axis_index("x"), lax.axis_index("y"), lax.axis_index("c")


def _all_gather(arrays, name):
    n = len(arrays)

    def body(*refs):
        ins, outs = refs[:n], refs[n:2 * n]
        send_sems, recv_sems, local_sems = refs[2 * n:]
        x, y, c = _mesh_pos()
        me, sibling = (x, y, c), (x, y, 1 - c)
        chips = [(1 - x, y), (x, 1 - y), (1 - x, 1 - y)]

        def slot(a, pos):
            px, py, pc = pos
            return outs[a].at[4 * px + 2 * py + pc]

        def copy(a, k, block, to, src=None):
            return pltpu.make_async_remote_copy(
                src_ref=slot(a, block) if src is None else src, dst_ref=slot(a, block),
                send_sem=send_sems.at[a * 7 + k], recv_sem=recv_sems.at[a * 7 + k],
                device_id=to, device_id_type=MESH_ID)

        mine, first, passed = [], [], []
        for a in range(n):
            cp = pltpu.make_async_copy(ins[a], slot(a, me), local_sems.at[a])
            cp.start()
            mine.append(cp)
            cps = [copy(a, 0, me, sibling, src=ins[a])]
            cps += [copy(a, 1 + j, me, (*chip, c), src=ins[a]) for j, chip in enumerate(chips)]
            for cp in cps:
                cp.start()
            first += cps
        for j, chip in enumerate(chips):
            for a in range(n):
                copy(a, 1 + j, (*chip, c), me).wait_recv()
                cp = copy(a, 4 + j, (*chip, c), sibling)
                cp.start()
                passed.append(cp)
        for a in range(n):
            copy(a, 0, sibling, me).wait_recv()
            for j, chip in enumerate(chips):
                copy(a, 4 + j, (*chip, 1 - c), me).wait_recv()
        for cp in first + passed:
            cp.wait_send()
        for cp in mine:
            cp.wait()

    hbm = pl.BlockSpec(memory_space=pltpu.HBM)
    return pl.pallas_call(
        body, name=name,
        out_shape=tuple(jax.ShapeDtypeStruct((N_DEV,) + a.shape, a.dtype) for a in arrays),
        in_specs=[hbm] * n, out_specs=tuple([hbm] * n),
        scratch_shapes=[pltpu.SemaphoreType.DMA((7 * n,)), pltpu.SemaphoreType.DMA((7 * n,)),
                        pltpu.SemaphoreType.DMA((n,))],
    )(*arrays)


def _all_gather_under_ln(arrays, x, g, b, name, tq=512):
    n = len(arrays)
    T, D = x.shape
    nt = T // tq

    def body(*refs):
        x_ref, g_ref, b_ref = refs[:3]
        ins = refs[3:3 + n]
        y_ref, yb_ref = refs[3 + n:5 + n]
        outs = refs[5 + n:5 + 2 * n]
        send_sems, recv_sems, local_sems = refs[5 + 2 * n:]
        i = pl.program_id(0)
        mx, my, mc = _mesh_pos()
        me, sibling = (mx, my, mc), (mx, my, 1 - mc)
        chips = [(1 - mx, my), (mx, 1 - my), (1 - mx, 1 - my)]

        def slot(a, pos):
            px, py, pc = pos
            return outs[a].at[4 * px + 2 * py + pc]

        def copy(a, k, block, to, src=None):
            return pltpu.make_async_remote_copy(
                src_ref=slot(a, block) if src is None else src, dst_ref=slot(a, block),
                send_sem=send_sems.at[a * 7 + k], recv_sem=recv_sems.at[a * 7 + k],
                device_id=to, device_id_type=MESH_ID)

        def own(a):
            return pltpu.make_async_copy(ins[a], slot(a, me), local_sems.at[a])

        def first(a):
            return [copy(a, 0, me, sibling, src=ins[a])] + [
                copy(a, 1 + j, me, (*chip, mc), src=ins[a]) for j, chip in enumerate(chips)]

        @pl.when(i == 0)
        def _():
            for a in range(n):
                own(a).start()
                for cp in first(a):
                    cp.start()

        zv = x_ref[...]
        mu = jnp.mean(zv, axis=1, keepdims=True)
        zc = zv - mu
        var = jnp.mean(zc * zc, axis=1, keepdims=True)
        y = zc * lax.rsqrt(var + LN_EPS) * g_ref[...] + b_ref[...]
        y_ref[...] = y
        yb_ref[...] = y.astype(BF16)

        @pl.when(i == nt - 1)
        def _():
            passed = []
            for j, chip in enumerate(chips):
                for a in range(n):
                    copy(a, 1 + j, (*chip, mc), me).wait_recv()
                    cp = copy(a, 4 + j, (*chip, mc), sibling)
                    cp.start()
                    passed.append(cp)
            for a in range(n):
                copy(a, 0, sibling, me).wait_recv()
                for j, chip in enumerate(chips):
                    copy(a, 4 + j, (*chip, 1 - mc), me).wait_recv()
            for a in range(n):
                for cp in first(a):
                    cp.wait_send()
                own(a).wait()
            for cp in passed:
                cp.wait_send()

    row = pl.BlockSpec((tq, D), lambda i: (i, 0))
    vec = pl.BlockSpec((1, D), lambda i: (0, 0))
    hbm = pl.BlockSpec(memory_space=pltpu.HBM)
    outs = pl.pallas_call(
        body, name=name,
        out_shape=(jax.ShapeDtypeStruct((T, D), F32), jax.ShapeDtypeStruct((T, D), BF16))
        + tuple(jax.ShapeDtypeStruct((N_DEV,) + a.shape, a.dtype) for a in arrays),
        grid=(nt,), in_specs=[row, vec, vec] + [hbm] * n, out_specs=tuple([row, row] + [hbm] * n),
        scratch_shapes=[pltpu.SemaphoreType.DMA((7 * n,)), pltpu.SemaphoreType.DMA((7 * n,)),
                        pltpu.SemaphoreType.DMA((n,))],
        compiler_params=_params(("arbitrary",)),
    )(x, g, b, *arrays)
    return outs[0], outs[1], outs[2:]


_HBM = pl.BlockSpec(memory_space=pltpu.HBM)
_SEM = pl.BlockSpec(memory_space=pltpu.SEMAPHORE)
_EFFECT = pltpu.SideEffectType.DATAFLOW_SIDE_EFFECTING
N_PEERS = N_DEV - 1


def _peer(k):
    x, y, c = _mesh_pos()
    return (1 - x if k & 4 else x, 1 - y if k & 2 else y, 1 - c if k & 1 else c)


def _split_start(srcs, scatter, after, name):
    n = len(srcs)
    zones = [jax.ShapeDtypeStruct(s.shape if scatter else ((N_DEV,) + s.shape), s.dtype) for s in srcs]

    def body(*refs):
        src, zone = refs[:n], refs[n:2 * n]
        outs = refs[2 * n + 1:]
        send, recv, token = outs[:n], outs[n:2 * n], outs[4 * n]
        x, y, c = _mesh_pos()
        my_idx = 4 * x + 2 * y + c
        for a in range(n):
            pltpu.make_async_copy(src[a].at[my_idx] if scatter else src[a],
                                  zone[a].at[N_PEERS] if scatter else zone[a].at[my_idx], recv[a]).start()
            for k in range(1, N_DEV):
                px, py, pc = _peer(k)
                pltpu.make_async_remote_copy(
                    src_ref=src[a].at[4 * px + 2 * py + pc] if scatter else src[a],
                    dst_ref=zone[a].at[k - 1] if scatter else zone[a].at[my_idx],
                    send_sem=send[a], recv_sem=recv[a], device_id=(px, py, pc), device_id_type=MESH_ID).start()
        token[...] = jnp.zeros_like(token)

    hbm = lambda a: pltpu.with_memory_space_constraint(a, pltpu.HBM)
    outs = pl.pallas_call(
        body, name=name,
        out_shape=tuple([pltpu.SemaphoreType.DMA(())] * (2 * n)
                        + [pltpu.HBM(s.shape, s.dtype) for s in srcs]
                        + [pltpu.HBM(z.shape, z.dtype) for z in zones]
                        + [jax.ShapeDtypeStruct((8, 128), F32)]),
        in_specs=[_HBM] * (2 * n) + [pl.BlockSpec(memory_space=pl.ANY)],
        out_specs=tuple([_SEM] * (2 * n) + [_HBM] * (2 * n) + [pl.BlockSpec(memory_space=pltpu.VMEM)]),
        input_output_aliases={**{a: 2 * n + a for a in range(n)}, **{n + a: 3 * n + a for a in range(n)}},
        compiler_params=pltpu.CompilerParams(has_side_effects=_EFFECT),
    )(*[hbm(s) for s in srcs], *[hbm(lax.empty(z.shape, z.dtype)) for z in zones], after)
    return outs[:n], outs[n:2 * n], outs[2 * n:3 * n], outs[3 * n:4 * n], outs[4 * n]


def _split_wait(send, recv, srcs, zones, after, name):
    n = len(srcs)

    def body(*refs):
        zone = refs[n:2 * n]
        send_sems, recv_sems = refs[2 * n:3 * n], refs[3 * n:4 * n]
        x, y, c = _mesh_pos()
        for a in range(n):
            seven = zone[a].at[pl.ds(0, N_PEERS)]
            pltpu.make_async_remote_copy(src_ref=seven, dst_ref=seven, send_sem=send_sems[a], recv_sem=recv_sems[a],
                                         device_id=(x, y, 1 - c), device_id_type=MESH_ID).wait_send()
            pltpu.make_async_remote_copy(src_ref=zone[a], dst_ref=zone[a], send_sem=send_sems[a],
                                         recv_sem=recv_sems[a], device_id=(x, y, 1 - c),
                                         device_id_type=MESH_ID).wait_recv()

    outs = pl.pallas_call(
        body, name=name,
        out_shape=tuple([pltpu.HBM(s.shape, s.dtype) for s in srcs] + [pltpu.HBM(z.shape, z.dtype) for z in zones]),
        in_specs=[_HBM] * (2 * n) + [_SEM] * (2 * n) + [pl.BlockSpec(memory_space=pl.ANY)],
        out_specs=tuple([_HBM] * (2 * n)),
        input_output_aliases={a: a for a in range(2 * n)},
        compiler_params=pltpu.CompilerParams(has_side_effects=_EFFECT),
    )(*srcs, *zones, *send, *recv, after)
    return outs[:n], outs[n:]


def _cat_blocks(g, axis):
    return jnp.concatenate([g[d] for d in range(N_DEV)], axis=axis)


N_LATENT = Q_LORA + KV_LORA + ROPE
W_SHARD = D_IN_PROJ // N_DEV


def _ref_cols(lo, hi):
    out = []
    if lo < N_LATENT:
        out.append((N_GATED + lo, N_GATED + min(hi, N_LATENT)))
    if hi > N_LATENT:
        out.append((max(lo, N_LATENT) - N_LATENT, hi - N_LATENT))
    return out


def _permute_w_in_t(blocks):
    pieces = []
    for lo, hi in ((N_LATENT, D_IN_PROJ), (0, N_LATENT)):
        for d in range(N_DEV):
            a, b = max(lo, d * W_SHARD), min(hi, (d + 1) * W_SHARD)
            if a < b:
                pieces.append(blocks[d][a - d * W_SHARD:b - d * W_SHARD])
    pieces.append(jnp.zeros((NPP - D_IN_PROJ, blocks.shape[2]), blocks.dtype))
    return jnp.concatenate(pieces, axis=0)


def _split_w_in_t(w):
    slabs = []
    for d in range(N_DEV):
        parts = [w[a:b] for a, b in _ref_cols(d * W_SHARD, (d + 1) * W_SHARD)]
        slabs.append(parts[0] if len(parts) == 1 else jnp.concatenate(parts, axis=0))
    return jnp.stack(slabs)


def _permute_w_uq(w):
    w3 = w.reshape(w.shape[0], N_HEADS, NOPE + ROPE)
    return jnp.concatenate([w3[:, :, :NOPE].reshape(w.shape[0], -1), w3[:, :, NOPE:].reshape(w.shape[0], -1)], axis=1)


def _unpermute_w_uq(w):
    nope = w[:, :N_HEADS * NOPE].reshape(w.shape[0], N_HEADS, NOPE)
    rope = w[:, N_HEADS * NOPE:].reshape(w.shape[0], N_HEADS, ROPE)
    return jnp.concatenate([nope, rope], axis=2).reshape(w.shape[0], -1)


_SMALL_EMB = (("emb_ln_g", 16), ("emb_ln_b", 16))
_SMALL_LAYER = (("q_norm_g", 8), ("kv_norm_g", 8), ("w_pool", 1024), ("pool_scale", 8), ("b_out", 32),
                ("ln_g", 32), ("ln_b", 32))
_SMALL = _SMALL_EMB + _SMALL_LAYER
CONV_ROWS = DEPTH * CONV_WIDTH * D_CONV // 128


def _pack_small(d, entries=_SMALL):
    parts = []
    for name, rows in entries:
        flat = d[name].reshape(-1)
        flat = jnp.pad(flat, (0, rows * 128 - flat.shape[0]))
        parts.append(flat.reshape(rows, 128))
    return jnp.concatenate(parts, axis=0)


def _unpack_small(packed, shapes):
    out, r0 = {}, 0
    for name, rows in _SMALL:
        size = 1
        for s in shapes[name]:
            size *= s
        out[name] = packed[r0:r0 + rows].reshape(-1)[:size].reshape(shapes[name])
        r0 += rows
    return out


def _rope_tables(positions):
    half = ROPE // 2
    inv_freq = ROPE_THETA ** (-jnp.arange(half, dtype=F32) / half)
    ang = positions.astype(F32)[:, None] * inv_freq
    cos, sin = jnp.cos(ang), jnp.sin(ang)
    return jnp.concatenate([cos, cos, cos, cos], axis=1), jnp.concatenate([-sin, sin, -sin, sin], axis=1)


def _local_step(x, positions, target, emb_g, emb_b, layer_weights, layer_weights_rest, on_sharded_grads,
                on_layer_grads=None, first_after=None, embedded=None):
    cos_t, sin_t = _rope_tables(positions)
    h, hb = _ln_fwd(x, emb_g, emb_b, "emb_ln_fwd") if embedded is None else embedded
    saved = []
    for l in range(DEPTH):
        W = layer_weights(l, h)
        proj = _mm(hb, W["w_in_t"], "nt", F32, "proj_fwd", after=first_after if l == 0 else None)
        qn, kvn, pooled, cv, mix = _mix_fwd(proj, W["q_norm_g"], W["kv_norm_g"], W["w_pool"], W["pool_scale"],
                                            W["conv_w"], "mix_fwd")
        rest, token = layer_weights_rest(l, proj)
        W = {**W, **rest}
        kv, qc, kc = _up_rope_fwd(qn, kvn, W["w_uq"], W["w_ukv"], proj, cos_t, sin_t, "up_rope_fwd", after=token)
        o, mix, lse = _flash_fwd(qc, kc, kv, proj, mix, "flash_fwd")
        z = _mm(mix, W["w_out"], "nn", F32, "out_fwd", res=h, bias=W["b_out"], alpha=ALPHA)
        saved.append((W, hb, proj, qn, kvn, pooled, cv, kv, qc, kc, o, lse, mix, z))
        h, hb = _ln_fwd(z, W["ln_g"], W["ln_b"], "ln_fwd")
    dh, sq = h, None

    grads = {k: [None] * DEPTH for k in ("q_norm_g", "kv_norm_g", "w_pool", "pool_scale", "conv_w", "b_out", "ln_g",
                                         "ln_b")}
    for l in reversed(range(DEPTH)):
        W, hb_in, proj, qn, kvn, pooled, cv, kv, qc, kc, o, lse, mix, z = saved[l]
        sharded = {}
        if l == DEPTH - 1:
            dz, dzb, grads["b_out"][l], grads["ln_g"][l], grads["ln_b"][l], sq = _ln_bwd(
                dh, z, W["ln_g"], "ln_bwd_loss", target=target)
        else:
            dz, dzb, grads["b_out"][l], grads["ln_g"][l], grads["ln_b"][l] = _ln_bwd(dh, z, W["ln_g"], "ln_bwd")
        dmix = _mm(dzb, W["w_out"], "nt", F32, "out_bwd_x")
        sharded["w_out"] = _mm(mix, dzb, "tn", GRAD_XFER, "out_bwd_w", tk=4096)
        do, delta, dproj, grads["w_pool"][l], grads["pool_scale"][l], grads["conv_w"][l] = _mix_bwd(
            dmix, proj, o, pooled, cv, W["w_pool"], W["pool_scale"], W["conv_w"], "mix_bwd")
        dqc, dkv, dkr = _flash_bwd(qc, kc, kv, do, lse, delta, "flash_bwd")
        dq, dkrope = _rope_bwd(dqc, dkr, cos_t, sin_t, "rope_bwd")
        sharded["w_uq"] = _mm(qn, dq, "tn", GRAD_XFER, "q_up_bwd_w")
        sharded["w_ukv"] = _mm(kvn, dkv, "tn", GRAD_XFER, "kv_up_bwd_w")
        token = on_sharded_grads(l, sharded)
        dproj, grads["q_norm_g"][l], grads["kv_norm_g"][l] = _up_rms_bwd(
            proj, dq, dkv, W["w_uq"], W["w_ukv"], dkrope, dproj, W["q_norm_g"], W["kv_norm_g"], "up_rms_bwd")
        if l == 0 and on_layer_grads is not None:
            token = on_layer_grads(grads, token)
        d_w_in_t = _mm(dproj, hb_in, "tn", GRAD_XFER, "proj_bwd_w", tk=4096, after=token)
        token = on_sharded_grads(l, {"w_in": d_w_in_t})
        dh = _mm(dproj, W["w_in_t"], "nn", F32, "proj_bwd_x", res=dz, alpha=ALPHA, tk=2560, after=token)
    grad_x, grads["emb_ln_g"], grads["emb_ln_b"] = _ln_bwd(dh, x, emb_g, "emb_ln_bwd", for_matmul=False)
    return sq, grad_x, grads


def kernel(x, positions, emb_ln_g, emb_ln_b, w_in, q_norm_g, kv_norm_g, w_uq, w_ukv, w_pool, pool_scale, conv_w, w_out, b_out, ln_g, ln_b, loss_target, m_emb_ln_g, m_emb_ln_b, m_w_in, m_q_norm_g, m_kv_norm_g, m_w_uq, m_w_ukv, m_w_pool, m_pool_scale, m_conv_w, m_w_out, m_b_out, m_ln_g, m_ln_b, v_emb_ln_g, v_emb_ln_b, v_w_in, v_q_norm_g, v_kv_norm_g, v_w_uq, v_w_ukv, v_w_pool, v_pool_scale, v_conv_w, v_w_out, v_b_out, v_ln_g, v_ln_b):
    weights = dict(emb_ln_g=emb_ln_g, emb_ln_b=emb_ln_b, w_in=w_in, q_norm_g=q_norm_g, kv_norm_g=kv_norm_g,
                   w_uq=w_uq, w_ukv=w_ukv, w_pool=w_pool, pool_scale=pool_scale, conv_w=conv_w, w_out=w_out,
                   b_out=b_out, ln_g=ln_g, ln_b=ln_b)
    mom1 = dict(emb_ln_g=m_emb_ln_g, emb_ln_b=m_emb_ln_b, w_in=m_w_in, q_norm_g=m_q_norm_g, kv_norm_g=m_kv_norm_g,
                w_uq=m_w_uq, w_ukv=m_w_ukv, w_pool=m_w_pool, pool_scale=m_pool_scale, conv_w=m_conv_w,
                w_out=m_w_out, b_out=m_b_out, ln_g=m_ln_g, ln_b=m_ln_b)
    mom2 = dict(emb_ln_g=v_emb_ln_g, emb_ln_b=v_emb_ln_b, w_in=v_w_in, q_norm_g=v_q_norm_g, kv_norm_g=v_kv_norm_g,
                w_uq=v_w_uq, w_ukv=v_w_ukv, w_pool=v_w_pool, pool_scale=v_pool_scale, conv_w=v_conv_w,
                w_out=v_w_out, b_out=v_b_out, ln_g=v_ln_g, ln_b=v_ln_b)

    big = ("w_in", "w_uq", "w_ukv", "w_out")

    conv_pad = jnp.zeros((8, 128), F32).at[0:DEPTH * CONV_WIDTH, 0:64].set(conv_w.reshape(DEPTH * CONV_WIDTH, 64))
    t12 = lambda a: jnp.swapaxes(a, 1, 2)
    shard = lambda k, l: (t12(weights[k])[l] if k == "w_in" else weights[k][l]).astype(BF16)
    h0, h0b, (w_in0, conv_all) = _all_gather_under_ln(
        [shard("w_in", 0), conv_pad], x[0], emb_ln_g.reshape(1, -1), emb_ln_b.reshape(1, -1), "w_in0_all_gather_emb_ln")
    rest0 = _split_start([shard(k, 0) for k in big[1:]], False, w_in0, "weights0_rest_start")
    conv_full = _cat_blocks(conv_all[:, 0:DEPTH * CONV_WIDTH, 0:64], 1).reshape(DEPTH, CONV_WIDTH, D_CONV)
    conv_full = jnp.pad(conv_full, ((0, 0), (0, 8 - CONV_WIDTH), (0, 0)))
    fetched = {}

    def layer_weights(l, ready):
        if l == 0:
            w_in_blocks = w_in0
        else:
            fetched[1] = _split_wait(*fetched["w1"][:4], ready, "weights1_wait")[1]
            w_in_blocks = fetched[1][0]
        return dict(
            w_in_t=_permute_w_in_t(w_in_blocks), conv_w=conv_full[l],
            q_norm_g=q_norm_g[l].reshape(1, -1), kv_norm_g=kv_norm_g[l].reshape(1, -1),
            w_pool=w_pool[l].astype(BF16), pool_scale=pool_scale[l].reshape(1, -1), b_out=b_out[l].reshape(1, -1),
            ln_g=ln_g[l].reshape(1, -1), ln_b=ln_b[l].reshape(1, -1))

    def layer_weights_rest(l, ready):
        token = None
        if l == 0:
            blocks = _split_wait(*rest0[:4], ready, "weights0_rest_wait")[1]
            fetched["w1"] = _split_start([shard(k, 1) for k in big], False, blocks[0], "weights1_start")
            token = fetched["w1"][4]
        else:
            blocks = fetched[1][1:]
        return dict(w_uq=_permute_w_uq(_cat_blocks(blocks[0], 1)), w_ukv=_cat_blocks(blocks[1], 1),
                    w_out=blocks[2].reshape(D_MIX, D_MODEL)), token

    by_dest = dict(
        w_in=_split_w_in_t,
        w_uq=lambda g: _unpermute_w_uq(g).reshape(Q_LORA, N_DEV, -1).transpose(1, 0, 2),
        w_ukv=lambda g: g.reshape(KV_LORA, N_DEV, -1).transpose(1, 0, 2),
        w_out=lambda g: g.reshape(N_DEV, -1, D_MODEL))
    in_flight = []

    def on_sharded_grads(l, g):
        names = [k for k in big if k in g]
        srcs = [by_dest[k](g[k]) for k in names]
        started = _split_start(srcs, True, srcs[0], "grads%d_%s_start" % (l, names[0]))
        in_flight.append((l, names, started[:4]))
        return started[4]

    small_in_flight = []

    def on_layer_grads(g, token):
        stacked = {k: jnp.stack(g[k]) for k, _ in _SMALL_LAYER}
        conv = jnp.stack([g["conv_w"][l][0:CONV_WIDTH] for l in range(DEPTH)]).reshape(CONV_ROWS, 128)
        packed = jnp.concatenate([_pack_small(stacked, _SMALL_LAYER), conv], axis=0)
        started = _split_start([packed], False, token, "layer_grads_start")
        small_in_flight.append(started[:4])
        return started[4]

    sq, grad_x, G = _local_step(x[0], positions[0], loss_target[0], emb_ln_g.reshape(1, -1),
                                emb_ln_b.reshape(1, -1), layer_weights, layer_weights_rest, on_sharded_grads,
                                on_layer_grads, first_after=rest0[4], embedded=(h0, h0b))
    loss = lax.psum(sq[0, 0] * (0.5 / D_MODEL), ("x", "y", "c"))

    res = {}
    landed = {}
    for l, names, started in in_flight:
        zones = _split_wait(*started, grad_x, "grads%d_%s_wait" % (l, names[0]))[1]
        for k, zone in zip(names, zones):
            landed[k, l] = zone
    w_in_res = None
    for l in reversed(range(DEPTH)):
        w_in_res = _adamw([landed["w_in", l]], t12(w_in), t12(m_w_in), t12(v_w_in), "adamw_w_in_%d" % l, W_SHARD, 512,
                          first_layer=l, into=w_in_res)
    res["w_in"] = tuple(t12(o) for o in w_in_res)
    for name, rows in (("w_uq", 256), ("w_ukv", 256), ("w_out", 128)):
        res[name] = _adamw([landed[name, l] for l in range(DEPTH)], weights[name], mom1[name], mom2[name],
                           "adamw_" + name, rows)

    layer_zone = _split_wait(*small_in_flight[0], grad_x, "layer_grads_wait")[1][0]
    emb_zone = _all_gather([_pack_small(G, _SMALL_EMB)], "emb_grads_all_gather")[0]
    n_layer_rows = sum(r for _, r in _SMALL_LAYER)
    l_small = jnp.concatenate([emb_zone, layer_zone[:, 0:n_layer_rows]], axis=1)
    my_idx = 4 * lax.axis_index("x") + 2 * lax.axis_index("y") + lax.axis_index("c")
    conv_all_grads = layer_zone[:, n_layer_rows:].reshape(N_DEV, DEPTH * CONV_WIDTH, D_CONV)
    l_conv = lax.dynamic_slice_in_dim(conv_all_grads, my_idx * 64, 64, axis=2)
    l_conv = jnp.zeros((N_DEV, 8, 128), F32).at[:, 0:DEPTH * CONV_WIDTH, 0:64].set(l_conv)
    conv_shard = lambda a: jnp.zeros((8, 128), F32).at[0:DEPTH * CONV_WIDTH, 0:64].set(a.reshape(-1, 64))
    conv_res = _adamw([l_conv], conv_shard(conv_w)[None], conv_shard(m_conv_w)[None], conv_shard(v_conv_w)[None],
                      "adamw_conv_w", 8)
    res["conv_w"] = tuple(o[0, 0:DEPTH * CONV_WIDTH, 0:64].reshape(DEPTH, CONV_WIDTH, 64) for o in conv_res)
    small_res = _adamw([l_small], _pack_small(weights)[None], _pack_small(mom1)[None], _pack_small(mom2)[None],
                       "adamw_small", 392)
    shapes = {k: weights[k].shape for k, _ in _SMALL}
    unpacked = [_unpack_small(o[0], shapes) for o in small_res]
    for k, _ in _SMALL:
        res[k] = tuple(u[k] for u in unpacked)

    order = ("emb_ln_g", "emb_ln_b", "w_in", "q_norm_g", "kv_norm_g", "w_uq", "w_ukv", "w_pool", "pool_scale",
             "conv_w", "w_out", "b_out", "ln_g", "ln_b")
    return (loss, grad_x[None], *[res[k][0] for k in order], *[res[k][1] for k in order],
            *[res[k][2] for k in order], *[res[k][3] for k in order])
```

```python
import jax
import jax.numpy as jnp
from jax import lax
from jax.experimental import pallas as pl
from jax.experimental.pallas import tpu as pltpu

F32 = jnp.float32
BF16 = jnp.bfloat16

N_DEV = 8
D_MODEL = 2048
DEPTH = 2
N_HEADS = 8
NOPE = 128
ROPE = 64
V_DIM = 128
Q_LORA = 512
KV_LORA = 256
D_MLA = N_HEADS * V_DIM
D_POOL = 512
D_CONV = 512
POOL_WINDOWS = (2, 4, 8, 16)
POOL_GROUP = 128
CONV_WIDTH = 3
D_MIX = D_MLA + D_POOL + D_CONV
D_IN_PROJ = 4928
ROPE_THETA = 10000.0
LN_EPS = 1e-5
RMS_EPS = 1e-6
ALPHA = (2 * DEPTH) ** 0.25
ATTN_SCALE = (NOPE + ROPE) ** -0.5
ADAM_LR = 0.001
ADAM_B1 = 0.9
ADAM_B2 = 0.999
ADAM_EPS = 1e-08
ADAM_WD = 0.01
ADAM_STEP = 10

O_GMLA, O_PIN, O_GPOOL, O_CH, O_CB, O_CC, O_GCONV, O_QLAT, O_KVLAT, O_KROPE = (
    0, 1024, 1536, 2048, 2560, 3072, 3584, 4096, 4608, 4864)
NPP = 5120
N_GATED = O_QLAT
QC = NOPE + 2 * ROPE
HALO = 16
ATT_TILE = 512
ATT_CH = 256
LOG2E = 1.4426950408889634
EXP2_SCALE = ATTN_SCALE * LOG2E

GRAD_XFER = BF16
VMEM_LIMIT = 48 * 1024 * 1024
ATT_BWD_VMEM_LIMIT = 58 * 1024 * 1024
MESH_ID = pl.DeviceIdType.MESH


def _params(sem=None):
    return pltpu.CompilerParams(dimension_semantics=sem, vmem_limit_bytes=VMEM_LIMIT)


def _sigmoid(x):
    return 1.0 / (1.0 + jnp.exp(-x))


def _tile(dim, target):
    if dim <= target:
        return dim
    t = target - target % 128
    while dim % t:
        t -= 128
    return t


_DIMS = {"nn": (((1,), (0,)), ((), ())), "nt": (((1,), (1,)), ((), ())), "tn": (((0,), (0,)), ((), ()))}


def _mm(a, b, mode, out_dtype, name, res=None, bias=None, alpha=1.0, tm=1024, tn=1024, tk=2048, after=None):
    if mode == "nn":
        (M, K), (K2, N) = a.shape, b.shape
    elif mode == "nt":
        (M, K), (N, K2) = a.shape, b.shape
    else:
        (K, M), (K2, N) = a.shape, b.shape
    assert K == K2
    tm, tn, tk = _tile(M, tm), _tile(N, tn), _tile(K, tk)
    nk = K // tk
    has_res, has_bias = res is not None, bias is not None

    def body(*refs):
        a_ref, b_ref = refs[0], refs[1]
        pos = 2
        res_ref = bias_ref = None
        if has_res:
            res_ref = refs[pos]
            pos += 1
        if has_bias:
            bias_ref = refs[pos]
            pos += 1
        def finish(r, o_ref):
            if has_bias:
                r = r + bias_ref[...]
            if has_res:
                r = alpha * res_ref[...] + r
            o_ref[...] = r.astype(out_dtype)

        part = lax.dot_general(a_ref[...].astype(BF16), b_ref[...].astype(BF16), _DIMS[mode],
                               preferred_element_type=F32)
        if nk == 1:
            finish(part, refs[-1])
            return
        o_ref, acc_ref = refs[-2], refs[-1]
        k = pl.program_id(2)

        @pl.when(k == 0)
        def _():
            acc_ref[...] = part

        @pl.when(jnp.logical_and(k > 0, k < nk - 1))
        def _():
            acc_ref[...] += part

        @pl.when(k == nk - 1)
        def _():
            finish(acc_ref[...] + part, o_ref)

    if mode == "nn":
        in_specs = [pl.BlockSpec((tm, tk), lambda i, j, k: (i, k)), pl.BlockSpec((tk, tn), lambda i, j, k: (k, j))]
    elif mode == "nt":
        in_specs = [pl.BlockSpec((tm, tk), lambda i, j, k: (i, k)), pl.BlockSpec((tn, tk), lambda i, j, k: (j, k))]
    else:
        in_specs = [pl.BlockSpec((tk, tm), lambda i, j, k: (k, i)), pl.BlockSpec((tk, tn), lambda i, j, k: (k, j))]
    args = [a, b]
    if has_res:
        in_specs.append(pl.BlockSpec((tm, tn), lambda i, j, k: (i, j)))
        args.append(res)
    if has_bias:
        in_specs.append(pl.BlockSpec((1, tn), lambda i, j, k: (0, j)))
        args.append(bias)
    if after is not None:
        in_specs.append(pl.BlockSpec((8, 128), lambda i, j, k: (0, 0)))
        args.append(after)
    return pl.pallas_call(
        body, name=name,
        out_shape=jax.ShapeDtypeStruct((M, N), out_dtype),
        grid=(M // tm, N // tn, nk),
        in_specs=in_specs,
        out_specs=pl.BlockSpec((tm, tn), lambda i, j, k: (i, j)),
        scratch_shapes=[pltpu.VMEM((tm, tn), F32)] if nk > 1 else [],
        compiler_params=_params(("parallel", "parallel", "arbitrary")),
    )(*args)


def _ln_fwd(z, g, b, name, tq=512, for_matmul=True):
    T, D = z.shape

    def body(z_ref, g_ref, b_ref, y_ref, *yb_ref):
        zv = z_ref[...]
        mu = jnp.mean(zv, axis=1, keepdims=True)
        zc = zv - mu
        var = jnp.mean(zc * zc, axis=1, keepdims=True)
        y = zc * lax.rsqrt(var + LN_EPS) * g_ref[...] + b_ref[...]
        y_ref[...] = y
        if for_matmul:
            yb_ref[0][...] = y.astype(BF16)

    row = pl.BlockSpec((tq, D), lambda i: (i, 0))
    vec = pl.BlockSpec((1, D), lambda i: (0, 0))
    outs = pl.pallas_call(
        body, name=name,
        out_shape=(jax.ShapeDtypeStruct((T, D), F32),) + ((jax.ShapeDtypeStruct((T, D), BF16),) if for_matmul else ()),
        grid=(T // tq,), in_specs=[row, vec, vec], out_specs=(row,) * (2 if for_matmul else 1),
        compiler_params=_params(("parallel",)),
    )(z, g, b)
    return (outs[0], outs[1]) if for_matmul else (outs[0], None)


def _ln_bwd(dy, z, g, name, tq=512, target=None, for_matmul=True):
    T, D = z.shape
    with_loss = target is not None

    def body(*refs):
        dy_ref, z_ref, g_ref = refs[:3]
        outs = list(refs[4 if with_loss else 3:])
        dz_ref = outs.pop(0)
        dzb_ref, ds_ref = (outs.pop(0), outs.pop(0)) if for_matmul else (None, None)
        dg_ref, db_ref = outs.pop(0), outs.pop(0)
        sq_ref = outs.pop(0) if with_loss else None

        @pl.when(pl.program_id(0) == 0)
        def _():
            for ref in (dg_ref, db_ref, ds_ref, sq_ref):
                if ref is not None:
                    ref[...] = jnp.zeros_like(ref)

        zv, dyv = z_ref[...], dy_ref[...]
        if with_loss:
            err = dyv - refs[3][...]
            sq_ref[...] += jnp.sum(err * err)
            dyv = err * (1.0 / D)
        mu = jnp.mean(zv, axis=1, keepdims=True)
        zc = zv - mu
        var = jnp.mean(zc * zc, axis=1, keepdims=True)
        rstd = lax.rsqrt(var + LN_EPS)
        xh = zc * rstd
        u = dyv * g_ref[...]
        dz = rstd * (u - jnp.mean(u, axis=1, keepdims=True) - xh * jnp.mean(u * xh, axis=1, keepdims=True))
        dz_ref[...] = dz
        dg_ref[...] += jnp.sum(dyv * xh, axis=0, keepdims=True)
        db_ref[...] += jnp.sum(dyv, axis=0, keepdims=True)
        if for_matmul:
            dzb_ref[...] = dz.astype(BF16)
            ds_ref[...] += jnp.sum(dz, axis=0, keepdims=True)

    row = pl.BlockSpec((tq, D), lambda i: (i, 0))
    vec = pl.BlockSpec((1, D), lambda i: (0, 0))
    vshape = jax.ShapeDtypeStruct((1, D), F32)
    out_shape, out_specs = [jax.ShapeDtypeStruct((T, D), F32)], [row]
    if for_matmul:
        out_shape += [jax.ShapeDtypeStruct((T, D), BF16), vshape]
        out_specs += [row, vec]
    out_shape += [vshape, vshape]
    out_specs += [vec, vec]
    if with_loss:
        out_shape.append(jax.ShapeDtypeStruct((8, 128), F32))
        out_specs.append(pl.BlockSpec((8, 128), lambda i: (0, 0)))
    return pl.pallas_call(
        body, name=name, out_shape=tuple(out_shape), grid=(T // tq,),
        in_specs=[row, row, vec] + ([row] if with_loss else []), out_specs=tuple(out_specs),
        compiler_params=_params(("arbitrary",)),
    )(dy, z, g, *([target] if with_loss else []))


def _pblock(tq, width, offset):
    assert offset % width == 0
    blk = offset // width
    return pl.BlockSpec((tq, width), lambda i: (i, blk))


def _mix_fwd(proj, q_g, kv_g, w_pool, pool_scale, conv_w, name, tq=256):
    T = proj.shape[0]

    def body(ql_ref, kvl_ref, pin_ref, gp_ref, ch_ref, cb_ref, cc_ref, gc_ref, qg_ref, kvg_ref, wp_ref, ps_ref,
             cw_ref, qn_ref, kvn_ref, pooled_ref, cv_ref, ypc_ref, extp, extu):
        i = pl.program_id(0)
        for x_ref, g_ref, o_ref in ((ql_ref, qg_ref, qn_ref), (kvl_ref, kvg_ref, kvn_ref)):
            x = x_ref[...]
            r = lax.rsqrt(jnp.mean(x * x, axis=1, keepdims=True) + RMS_EPS)
            o_ref[...] = (x * r * g_ref[...]).astype(BF16)

        @pl.when(i == 0)
        def _():
            extp[0:HALO, :] = jnp.zeros((HALO, D_POOL), F32)
            extu[0:HALO, :] = jnp.zeros((HALO, D_CONV), F32)

        @pl.when(i > 0)
        def _():
            extp[0:HALO, :] = extp[tq:tq + HALO, :]
            extu[0:HALO, :] = extu[tq:tq + HALO, :]

        t1 = (i * tq + lax.broadcasted_iota(jnp.int32, (tq, 1), 0) + 1).astype(F32)
        for g, w in enumerate(POOL_WINDOWS):
            cols = slice(g * POOL_GROUP, (g + 1) * POOL_GROUP)
            pin = pin_ref[:, cols]
            extp[HALO:, cols] = pin
            s = extp[:, cols]
            k = 1
            while k < w:
                s = s + pltpu.roll(s, k, 0)
                k *= 2
            mean = s[HALO:, :] / jnp.minimum(t1, float(w))
            pooled = (mean - pin).astype(BF16)
            pooled_ref[:, cols] = pooled
            r = jnp.dot(pooled, wp_ref[g], preferred_element_type=F32)
            gp = gp_ref[:, cols]
            ypc_ref[:, cols] = (r * ps_ref[:, cols] * (gp * _sigmoid(gp))).astype(BF16)
        for g in range(D_CONV // 128):
            cols = slice(g * 128, (g + 1) * 128)
            u = cc_ref[:, cols] * ch_ref[:, cols]
            extu[HALO:, cols] = u
            eu = extu[:, cols]
            u1 = pltpu.roll(eu, 1, 0)[HALO:, :]
            u2 = pltpu.roll(eu, 2, 0)[HALO:, :]
            cv = cw_ref[0:1, cols] * u2 + cw_ref[1:2, cols] * u1 + cw_ref[2:3, cols] * u
            cv_ref[:, cols] = cv
            gc = gc_ref[:, cols]
            ypc_ref[:, D_POOL + g * 128:D_POOL + (g + 1) * 128] = (
                cb_ref[:, cols] * cv * (gc * _sigmoid(gc))).astype(BF16)

    full = lambda shape: pl.BlockSpec(shape, lambda i: (0,) * len(shape))
    row = lambda w: pl.BlockSpec((tq, w), lambda i: (i, 0))
    return pl.pallas_call(
        body, name=name,
        out_shape=(jax.ShapeDtypeStruct((T, Q_LORA), BF16), jax.ShapeDtypeStruct((T, KV_LORA), BF16),
                   jax.ShapeDtypeStruct((T, D_POOL), BF16), jax.ShapeDtypeStruct((T, D_CONV), F32),
                   jax.ShapeDtypeStruct((T, D_MIX), BF16)),
        grid=(T // tq,),
        in_specs=[_pblock(tq, Q_LORA, O_QLAT), _pblock(tq, KV_LORA, O_KVLAT), _pblock(tq, 512, O_PIN),
                  _pblock(tq, 512, O_GPOOL), _pblock(tq, 512, O_CH), _pblock(tq, 512, O_CB), _pblock(tq, 512, O_CC),
                  _pblock(tq, 512, O_GCONV), full((1, Q_LORA)), full((1, KV_LORA)), full((4, 128, 128)),
                  full((1, D_POOL)), full((8, D_CONV))],
        out_specs=(row(Q_LORA), row(KV_LORA), row(D_POOL), row(D_CONV),
                   pl.BlockSpec((tq, D_POOL + D_CONV), lambda i: (i, D_MLA // (D_POOL + D_CONV)))),
        scratch_shapes=[pltpu.VMEM((tq + HALO, D_POOL), F32), pltpu.VMEM((tq + HALO, D_CONV), F32)],
        compiler_params=_params(("arbitrary",)),
    )(proj, proj, proj, proj, proj, proj, proj, proj, q_g, kv_g, w_pool, pool_scale, conv_w)


def _mix_bwd(dmix, proj, o, pooled, cv, w_pool, pool_scale, conv_w, name, tq=ATT_CH):
    T = proj.shape[0]
    nt = T // tq
    n_ext = tq + HALO

    def body(dym_ref, dyp_ref, dyc_ref, gm_ref, gp_ref, ch_ref, cb_ref, cc_ref, gc_ref, o_ref, pooled_ref, cv_ref,
             wp_ref, ps_ref, cw_ref, do_ref, delta_ref, dg_ref, dwp_ref, dps_ref, dcw_ref, exte, extd):
        i = pl.program_id(0)
        tile = nt - 1 - i

        @pl.when(i == 0)
        def _():
            dwp_ref[...] = jnp.zeros_like(dwp_ref)
            dps_ref[...] = jnp.zeros_like(dps_ref)
            dcw_ref[...] = jnp.zeros_like(dcw_ref)
            exte[tq:, :] = jnp.zeros((HALO, D_POOL), F32)
            extd[tq:, :] = jnp.zeros((HALO, D_CONV), F32)

        @pl.when(i > 0)
        def _():
            exte[tq:, :] = exte[0:HALO, :]
            extd[tq:, :] = extd[0:HALO, :]

        ones = jnp.ones((8, V_DIM), F32)
        for h in range(N_HEADS):
            cols = slice(h * V_DIM, (h + 1) * V_DIM)
            gm = gm_ref[:, cols]
            sig = _sigmoid(gm)
            dym = dym_ref[:, cols]
            ov = o_ref[:, cols]
            do = dym * (gm * sig)
            do_ref[:, cols] = do.astype(BF16)
            rows = lax.dot_general(ones, do * ov, _DIMS["nt"], precision=lax.Precision.HIGHEST,
                                   preferred_element_type=F32)
            delta_ref[h, 0] = rows[0:1, :]
            dg_ref[:, O_GMLA + h * V_DIM:O_GMLA + (h + 1) * V_DIM] = (
                dym * ov * (sig * (1.0 + gm * (1.0 - sig)))).astype(BF16)

        t1 = (tile * tq + lax.broadcasted_iota(jnp.int32, (tq, 1), 0) + 1).astype(F32)
        for g, w in enumerate(POOL_WINDOWS):
            cols = slice(g * POOL_GROUP, (g + 1) * POOL_GROUP)
            pg = pooled_ref[:, cols]
            r = jnp.dot(pg, wp_ref[g], preferred_element_type=F32)
            gp = gp_ref[:, cols]
            sg = _sigmoid(gp)
            sl = gp * sg
            dyg = dyp_ref[:, cols]
            ps = ps_ref[:, cols]
            dg_ref[:, O_GPOOL + g * POOL_GROUP:O_GPOOL + (g + 1) * POOL_GROUP] = (
                dyg * (r * ps) * (sg * (1.0 + gp * (1.0 - sg)))).astype(BF16)
            dps_ref[:, cols] += jnp.sum(dyg * r * sl, axis=0, keepdims=True)
            dr = (dyg * ps * sl).astype(BF16)
            dwp_ref[g] += lax.dot_general(pg, dr, _DIMS["tn"], preferred_element_type=F32)
            dpooled = lax.dot_general(dr, wp_ref[g], _DIMS["nt"], preferred_element_type=F32)
            exte[0:tq, cols] = dpooled / jnp.minimum(t1, float(w))
            s = exte[:, cols]
            k = 1
            while k < w:
                s = s + pltpu.roll(s, n_ext - k, 0)
                k *= 2
            dg_ref[:, O_PIN + g * POOL_GROUP:O_PIN + (g + 1) * POOL_GROUP] = (s[0:tq, :] - dpooled).astype(BF16)

        for g in range(D_CONV // 128):
            cols = slice(g * 128, (g + 1) * 128)
            out = lambda base: slice(base + g * 128, base + (g + 1) * 128)
            gc = gc_ref[:, cols]
            sg = _sigmoid(gc)
            sl = gc * sg
            dyc = dyc_ref[:, cols]
            cb, cc, ch, cvv = cb_ref[:, cols], cc_ref[:, cols], ch_ref[:, cols], cv_ref[:, cols]
            dcv = dyc * cb * sl
            dg_ref[:, out(O_GCONV)] = (dyc * (cb * cvv) * (sg * (1.0 + gc * (1.0 - sg)))).astype(BF16)
            dg_ref[:, out(O_CB)] = (dyc * cvv * sl).astype(BF16)
            extd[0:tq, cols] = dcv
            ed = extd[:, cols]
            d1 = pltpu.roll(ed, n_ext - 1, 0)[0:tq, :]
            d2 = pltpu.roll(ed, n_ext - 2, 0)[0:tq, :]
            du = cw_ref[2:3, cols] * dcv + cw_ref[1:2, cols] * d1 + cw_ref[0:1, cols] * d2
            u = cc * ch
            dcw_ref[0:1, cols] += jnp.sum(u * d2, axis=0, keepdims=True)
            dcw_ref[1:2, cols] += jnp.sum(u * d1, axis=0, keepdims=True)
            dcw_ref[2:3, cols] += jnp.sum(u * dcv, axis=0, keepdims=True)
            dg_ref[:, out(O_CH)] = (du * cc).astype(BF16)
            dg_ref[:, out(O_CC)] = (du * ch).astype(BF16)

    def rblock(width, offset):
        assert offset % width == 0
        blk = offset // width
        return pl.BlockSpec((tq, width), lambda i: (nt - 1 - i, blk))

    full = lambda shape: pl.BlockSpec(shape, lambda i: (0,) * len(shape))
    return pl.pallas_call(
        body, name=name,
        out_shape=(jax.ShapeDtypeStruct((T, D_MLA), BF16), jax.ShapeDtypeStruct((N_HEADS, nt, 1, tq), F32),
                   jax.ShapeDtypeStruct((T, NPP), BF16),
                   jax.ShapeDtypeStruct((4, 128, 128), F32), jax.ShapeDtypeStruct((1, D_POOL), F32),
                   jax.ShapeDtypeStruct((8, D_CONV), F32)),
        grid=(nt,),
        in_specs=[rblock(1024, 0), rblock(512, 1024), rblock(512, 1536),
                  rblock(1024, O_GMLA), rblock(512, O_GPOOL), rblock(512, O_CH), rblock(512, O_CB),
                  rblock(512, O_CC), rblock(512, O_GCONV), rblock(1024, 0), rblock(512, 0), rblock(512, 0),
                  full((4, 128, 128)), full((1, D_POOL)), full((8, D_CONV))],
        out_specs=(rblock(1024, 0), pl.BlockSpec((N_HEADS, 1, 1, tq), lambda i: (0, nt - 1 - i, 0, 0)),
                   rblock(N_GATED, 0), full((4, 128, 128)), full((1, D_POOL)), full((8, D_CONV))),
        scratch_shapes=[pltpu.VMEM((n_ext, D_POOL), F32), pltpu.VMEM((n_ext, D_CONV), F32)],
        compiler_params=_params(("arbitrary",)),
    )(dmix, dmix, dmix, proj, proj, proj, proj, proj, proj, o, pooled, cv, w_pool, pool_scale, conv_w)


def _up_rms_bwd(proj, dq, dkv, w_uq, w_ukv, dkrope, dproj, q_g, kv_g, name, tq=256):
    T = proj.shape[0]
    n_lat = NPP - N_GATED

    def body(ql_ref, kvl_ref, dq_ref, dkv_ref, wq_ref, wkv_ref, dkr_ref, _, qg_ref, kvg_ref, dlat_ref, dqg_ref,
             dkvg_ref):
        @pl.when(pl.program_id(0) == 0)
        def _():
            dqg_ref[...] = jnp.zeros_like(dqg_ref)
            dkvg_ref[...] = jnp.zeros_like(dkvg_ref)

        dqn = lax.dot_general(dq_ref[...], wq_ref[...], _DIMS["nt"], preferred_element_type=F32)
        dkvn = lax.dot_general(dkv_ref[...], wkv_ref[...], _DIMS["nt"], preferred_element_type=F32)
        for x_ref, dy, g_ref, c0, dg_ref in ((ql_ref, dqn, qg_ref, 0, dqg_ref),
                                             (kvl_ref, dkvn, kvg_ref, Q_LORA, dkvg_ref)):
            x = x_ref[...]
            r = lax.rsqrt(jnp.mean(x * x, axis=1, keepdims=True) + RMS_EPS)
            xr = x * r
            u = dy * g_ref[...]
            dlat_ref[:, c0:c0 + x.shape[1]] = (r * (u - xr * jnp.mean(u * xr, axis=1, keepdims=True))).astype(BF16)
            dg_ref[...] += jnp.sum(dy * xr, axis=0, keepdims=True)
        dlat_ref[:, Q_LORA + KV_LORA:] = dkr_ref[...]

    row = lambda w: pl.BlockSpec((tq, w), lambda i: (i, 0))
    vec = lambda w: pl.BlockSpec((1, w), lambda i: (0, 0))
    assert N_GATED % n_lat == 0
    return pl.pallas_call(
        body, name=name,
        out_shape=(jax.ShapeDtypeStruct((T, NPP), BF16),
                   jax.ShapeDtypeStruct((1, Q_LORA), F32), jax.ShapeDtypeStruct((1, KV_LORA), F32)),
        grid=(T // tq,),
        in_specs=[_pblock(tq, Q_LORA, O_QLAT), _pblock(tq, KV_LORA, O_KVLAT), row(dq.shape[1]), row(dkv.shape[1]),
                  pl.BlockSpec(w_uq.shape, lambda i: (0, 0)), pl.BlockSpec(w_ukv.shape, lambda i: (0, 0)),
                  row(n_lat - Q_LORA - KV_LORA), pl.BlockSpec(memory_space=pl.ANY), vec(Q_LORA), vec(KV_LORA)],
        out_specs=(pl.BlockSpec((tq, n_lat), lambda i: (i, N_GATED // n_lat)), vec(Q_LORA), vec(KV_LORA)),
        input_output_aliases={7: 0},
        compiler_params=_params(("arbitrary",)),
    )(proj, proj, dq, dkv, w_uq, w_ukv, dkrope, dproj, q_g, kv_g)


def _swap_halves(x, lo):
    return jnp.where(lo, pltpu.roll(x, 96, 1), pltpu.roll(x, 32, 1))


def _up_rope_fwd(qn, kvn, w_uq, w_ukv, proj, cos_t, sin_t, name, tq=256, after=None):
    T = qn.shape[0]

    def body(qn_ref, kvn_ref, wq_ref, wkv_ref, kr_ref, c_ref, s_ref, *rest):
        kv_ref, qc_ref, kc_ref = rest[-3:]
        q = jnp.dot(qn_ref[...], wq_ref[...], preferred_element_type=F32)
        kv_ref[...] = jnp.dot(kvn_ref[...], wkv_ref[...], preferred_element_type=F32).astype(BF16)
        C, S = c_ref[...], s_ref[...]
        lane = lax.broadcasted_iota(jnp.int32, (tq, 128), 1)
        lo = (lane % ROPE) < (ROPE // 2)
        first = lane < ROPE

        def rope(x):
            return x * C + _swap_halves(x, lo) * S

        kr = jnp.where(first, rope(kr_ref[...]), 0.0).astype(BF16)
        n_nope = N_HEADS * NOPE
        for j in range(N_HEADS // 2):
            r = rope(q[:, n_nope + j * 128:n_nope + (j + 1) * 128])
            pair = (jnp.where(first, r, 0.0), jnp.where(first, pltpu.roll(r, 64, 1), 0.0))
            for hh in range(2):
                h = 2 * j + hh
                qc_ref[h, :, 0:NOPE] = q[:, h * NOPE:(h + 1) * NOPE].astype(BF16)
                qc_ref[h, :, NOPE:QC] = pair[hh].astype(BF16)
        for h in range(N_HEADS):
            kc_ref[h, :, 0:NOPE] = kv_ref[:, h * 256:h * 256 + NOPE]
            kc_ref[h, :, NOPE:QC] = kr

    out = jax.ShapeDtypeStruct((N_HEADS, T, QC), BF16)
    hblock = pl.BlockSpec((N_HEADS, tq, QC), lambda i: (0, i, 0))
    row = lambda w: pl.BlockSpec((tq, w), lambda i: (i, 0))
    full = lambda a: pl.BlockSpec(a.shape, lambda i: (0, 0))
    return pl.pallas_call(
        body, name=name, out_shape=(jax.ShapeDtypeStruct((T, 2 * D_MLA), BF16), out, out), grid=(T // tq,),
        in_specs=[row(Q_LORA), row(KV_LORA), full(w_uq), full(w_ukv), _pblock(tq, 128, O_KROPE), row(128), row(128)]
        + ([pl.BlockSpec((8, 128), lambda i: (0, 0))] if after is not None else []),
        out_specs=(row(2 * D_MLA), hblock, hblock),
        compiler_params=_params(("parallel",)),
    )(qn, kvn, w_uq, w_ukv, proj, cos_t, sin_t, *([after] if after is not None else []))


def _rope_bwd(dqc, dkr, cos_t, sin_t, name, tq=256):
    T = dqc.shape[1]

    def body(dqc_ref, dkr_ref, c_ref, s_ref, dq_ref, dk_ref):
        C, S = c_ref[...], s_ref[...]
        lane = lax.broadcasted_iota(jnp.int32, (tq, 128), 1)
        lo = (lane % ROPE) < (ROPE // 2)
        first = lane < ROPE

        def unrope(dy):
            return dy * C - _swap_halves(dy, lo) * S

        acc = dkr_ref[0]
        for h in range(1, N_HEADS):
            acc = acc + dkr_ref[h]
        dk_ref[:, 0:128] = jnp.where(first, unrope(acc), 0.0).astype(BF16)
        dk_ref[:, 128:256] = jnp.zeros((tq, 128), BF16)
        for j in range(N_HEADS // 2):
            d0 = dqc_ref[2 * j, :, NOPE:QC]
            d1 = dqc_ref[2 * j + 1, :, NOPE:QC]
            comb = jnp.where(first, d0, pltpu.roll(d1, 64, 1))
            dq_ref[:, 1024 + j * 128:1024 + (j + 1) * 128] = unrope(comb).astype(BF16)
        for h in range(N_HEADS):
            dq_ref[:, h * NOPE:(h + 1) * NOPE] = dqc_ref[h, :, 0:NOPE].astype(BF16)

    tab = pl.BlockSpec((tq, 128), lambda i: (i, 0))
    return pl.pallas_call(
        body, name=name,
        out_shape=(jax.ShapeDtypeStruct((T, 1536), BF16), jax.ShapeDtypeStruct((T, 256), BF16)),
        grid=(T // tq,),
        in_specs=[pl.BlockSpec((N_HEADS, tq, QC), lambda i: (0, i, 0)),
                  pl.BlockSpec((N_HEADS, tq, 128), lambda i: (0, i, 0)), tab, tab],
        out_specs=(pl.BlockSpec((tq, 1536), lambda i: (i, 0)), pl.BlockSpec((tq, 256), lambda i: (i, 0))),
        compiler_params=_params(("parallel",)),
    )(dqc, dkr, cos_t, sin_t)


def _flash_fwd(qc, kc, kv, proj, mix, name):
    H, T, _ = qc.shape
    tt = ATT_TILE
    nt = T // tt
    sp = tt // ATT_CH
    pairs = [(i, c, int(c == i)) for i in range(nt) for c in range(i + 1)]
    assert len(pairs) % 2 == 0
    table = jnp.asarray(pairs + [pairs[-1]], jnp.int32)

    def body(tab_ref, q_ref, k_ref, v_ref, g_ref, _, o_ref, y_ref, lse_ref, vt_sc, s_sc, acc_sc, m_sc, l_sc, bias_sc):
        def issue(p, slot):
            i, c = tab_ref[p, 0], tab_ref[p, 1]
            s_sc[slot] = lax.dot_general(k_ref[0, pl.ds(pl.multiple_of(c * tt, tt), tt), :],
                                         q_ref[0, pl.ds(pl.multiple_of(i * tt, tt), tt), :], _DIMS["nt"],
                                         preferred_element_type=F32)

        def softmax_pv(p, slot):
            i, c, diag = tab_ref[p, 0], tab_ref[p, 1], tab_ref[p, 2]
            s = s_sc[slot] + bias_sc[diag]
            m = m_sc[i]
            m_new = jnp.maximum(m, jnp.max(s, axis=0, keepdims=True))
            pr = jnp.exp2((s - m_new) * EXP2_SCALE)
            a = jnp.exp2((m - m_new) * EXP2_SCALE)
            l_sc[i] = a * l_sc[i] + jnp.sum(pr, axis=0, keepdims=True)
            acc_sc[i] = a * acc_sc[i] + jnp.dot(vt_sc[c], pr.astype(BF16), preferred_element_type=F32)
            m_sc[i] = m_new

        issue(0, 0)
        m_sc[...] = jnp.full_like(m_sc, -jnp.inf)
        l_sc[...] = jnp.zeros_like(l_sc)
        acc_sc[...] = jnp.zeros_like(acc_sc)
        krow = lax.broadcasted_iota(jnp.int32, (tt, tt), 0)
        qcol = lax.broadcasted_iota(jnp.int32, (tt, tt), 1)
        bias_sc[0] = jnp.zeros((tt, tt), F32)
        bias_sc[1] = jnp.where(krow <= qcol, 0.0, -jnp.inf)
        for c in range(nt):
            vt_sc[c] = v_ref[c * tt:(c + 1) * tt, :].astype(F32).T.astype(BF16)

        def two(u, carry):
            p = 2 * u
            issue(p + 1, 1)
            softmax_pv(p, 0)
            issue(p + 2, 0)
            softmax_pv(p + 1, 1)
            return carry

        lax.fori_loop(0, len(pairs) // 2, two, 0)
        for i in range(nt):
            rows = slice(i * tt, (i + 1) * tt)
            l = l_sc[i]
            o = (acc_sc[i] / l).T
            o_ref[rows, :] = o
            lse = m_sc[i] * ATTN_SCALE + jnp.log(l)
            for r in range(sp):
                lse_ref[0, sp * i + r] = lse[:, r * ATT_CH:(r + 1) * ATT_CH]
            g = g_ref[rows, :]
            y_ref[rows, :] = (o * (g * _sigmoid(g))).astype(BF16)

    head = lambda h, tab: (h, 0, 0)
    col = lambda h, tab: (0, h)
    return pl.pallas_call(
        body, name=name,
        out_shape=(jax.ShapeDtypeStruct((T, D_MLA), F32), jax.ShapeDtypeStruct((T, D_MIX), BF16),
                   jax.ShapeDtypeStruct((H, T // ATT_CH, 1, ATT_CH), F32)),
        grid_spec=pltpu.PrefetchScalarGridSpec(
            num_scalar_prefetch=1, grid=(H,),
            in_specs=[pl.BlockSpec((1, T, QC), head), pl.BlockSpec((1, T, QC), head),
                      pl.BlockSpec((T, V_DIM), lambda h, tab: (0, 2 * h + 1)), pl.BlockSpec((T, V_DIM), col),
                      pl.BlockSpec(memory_space=pl.ANY)],
            out_specs=(pl.BlockSpec((T, V_DIM), col), pl.BlockSpec((T, V_DIM), col),
                       pl.BlockSpec((1, T // ATT_CH, 1, ATT_CH), lambda h, tab: (h, 0, 0, 0))),
            scratch_shapes=[pltpu.VMEM((nt, V_DIM, tt), BF16), pltpu.VMEM((2, tt, tt), F32),
                            pltpu.VMEM((nt, V_DIM, tt), F32), pltpu.VMEM((nt, 1, tt), F32),
                            pltpu.VMEM((nt, 1, tt), F32), pltpu.VMEM((2, tt, tt), F32)]),
        input_output_aliases={5: 1},
        compiler_params=_params(("arbitrary",)),
    )(table, qc, kc, kv, proj, mix)


def _flash_bwd(qc, kc, kv, do, lse, delta, name):
    H, T, _ = qc.shape
    tt = ATT_TILE
    nt = T // tt
    sp = tt // ATT_CH
    pairs = [(j, c) for j in range(nt) for c in range(j, nt)]
    assert len(pairs) % 2 == 0
    table = jnp.asarray(pairs + [pairs[-1]], jnp.int32)

    def body(tab_ref, q_ref, k_ref, v_ref, do_ref, lse_ref, dl_ref, dq_ref, dkv_ref, dkr_ref, dqt_sc, dk_sc, dv_sc,
             s_sc, dp_sc, kt_sc, bias_sc):
        def operands(j, c):
            k0, q0 = pl.multiple_of(j * tt, tt), pl.multiple_of(c * tt, tt)
            return (k_ref[0, pl.ds(k0, tt), :], v_ref[pl.ds(k0, tt), :], q_ref[0, pl.ds(q0, tt), :],
                    do_ref[pl.ds(q0, tt), :])

        def stat_row(ref, c):
            return jnp.concatenate([ref[0, sp * c + r] for r in range(sp)], axis=1)

        def early(p, slot):
            k, v, q, dov = operands(tab_ref[p, 0], tab_ref[p, 1])
            s_sc[slot] = lax.dot_general(k, q, _DIMS["nt"], preferred_element_type=F32)
            dp_sc[slot] = lax.dot_general(v, dov, _DIMS["nt"], preferred_element_type=F32)

        def late(p, slot):
            j, c = tab_ref[p, 0], tab_ref[p, 1]
            _, _, q, dov = operands(j, c)
            s = s_sc[slot] + jnp.where(j == c, bias_sc[...], 0.0)
            pr = jnp.exp2(s * EXP2_SCALE - stat_row(lse_ref, c) * LOG2E)
            ds = (pr * (dp_sc[slot] - stat_row(dl_ref, c)) * ATTN_SCALE).astype(BF16)
            dv_sc[j] += jnp.dot(pr.astype(BF16), dov, preferred_element_type=F32)
            dk_sc[j] += jnp.dot(ds, q, preferred_element_type=F32)
            dqt_sc[c] += jnp.dot(kt_sc[j], ds, preferred_element_type=F32)

        early(0, 0)
        dqt_sc[...] = jnp.zeros_like(dqt_sc)
        dk_sc[...] = jnp.zeros_like(dk_sc)
        dv_sc[...] = jnp.zeros_like(dv_sc)
        krow = lax.broadcasted_iota(jnp.int32, (tt, tt), 0)
        qcol = lax.broadcasted_iota(jnp.int32, (tt, tt), 1)
        bias_sc[...] = jnp.where(krow <= qcol, 0.0, -jnp.inf)
        for j in range(nt):
            kt_sc[j] = k_ref[0, j * tt:(j + 1) * tt, :].astype(F32).T.astype(BF16)

        def two(u, carry):
            p = 2 * u
            early(p + 1, 1)
            late(p, 0)
            early(p + 2, 0)
            late(p + 1, 1)
            return carry

        lax.fori_loop(0, len(pairs) // 2, two, 0)
        for j in range(nt):
            rows = slice(j * tt, (j + 1) * tt)
            dk = dk_sc[j]
            dkv_ref[rows, 0:NOPE] = dk[:, 0:NOPE].astype(BF16)
            dkv_ref[rows, NOPE:] = dv_sc[j].astype(BF16)
            dkr_ref[0, rows, :] = dk[:, NOPE:]
            dq_ref[0, rows, :] = dqt_sc[j].T

    head = lambda h, tab: (h, 0, 0)
    stat = pl.BlockSpec((1, T // ATT_CH, 1, ATT_CH), lambda h, tab: (h, 0, 0, 0))
    return pl.pallas_call(
        body, name=name,
        out_shape=(jax.ShapeDtypeStruct((H, T, QC), F32), jax.ShapeDtypeStruct((T, 2 * D_MLA), BF16),
                   jax.ShapeDtypeStruct((H, T, 128), F32)),
        grid_spec=pltpu.PrefetchScalarGridSpec(
            num_scalar_prefetch=1, grid=(H,),
            in_specs=[pl.BlockSpec((1, T, QC), head), pl.BlockSpec((1, T, QC), head),
                      pl.BlockSpec((T, V_DIM), lambda h, tab: (0, 2 * h + 1)),
                      pl.BlockSpec((T, V_DIM), lambda h, tab: (0, h)), stat, stat],
            out_specs=(pl.BlockSpec((1, T, QC), head), pl.BlockSpec((T, 256), lambda h, tab: (0, h)),
                       pl.BlockSpec((1, T, 128), head)),
            scratch_shapes=[pltpu.VMEM((nt, QC, tt), F32), pltpu.VMEM((nt, tt, QC), F32),
                            pltpu.VMEM((nt, tt, V_DIM), F32), pltpu.VMEM((2, tt, tt), F32),
                            pltpu.VMEM((2, tt, tt), F32), pltpu.VMEM((nt, QC, tt), BF16), pltpu.VMEM((tt, tt), F32)]),
        compiler_params=pltpu.CompilerParams(dimension_semantics=("arbitrary",),
                                             vmem_limit_bytes=ATT_BWD_VMEM_LIMIT),
    )(table, qc, kc, kv, do, lse, delta)


def _adamw(lands, w, m, v, name, rows, cols=None, first_layer=0, into=None):
    layers, R, C = w.shape
    L = len(lands)
    cols = C if cols is None else cols
    assert R % rows == 0 and C % cols == 0 and first_layer + L <= layers
    nc = C // cols
    nb = (R // rows) * nc
    c1 = 1.0 - ADAM_B1 ** ADAM_STEP
    c2 = 1.0 - ADAM_B2 ** ADAM_STEP

    def body(*refs):
        land_refs = refs[:L]
        w_ref, m_ref, v_ref = refs[L:L + 3]
        g_ref, d_ref, nm_ref, nv_ref, g_sc = refs[-5:]
        for ll in range(L):
            @pl.when(pl.program_id(0) == ll)
            def _(land_ref=land_refs[ll]):
                g = land_ref[0].astype(F32)
                for s in range(1, N_DEV):
                    g = g + land_ref[s].astype(F32)
                g_sc[...] = g

        g = g_sc[...]
        nm = ADAM_B1 * m_ref[0] + (1.0 - ADAM_B1) * g
        nv = ADAM_B2 * v_ref[0] + (1.0 - ADAM_B2) * (g * g)
        g_ref[0] = g
        nm_ref[0] = nm
        nv_ref[0] = nv
        d_ref[0] = -ADAM_LR * ((nm / c1) / (jnp.sqrt(nv / c2) + ADAM_EPS) + ADAM_WD * w_ref[0])

    def land_spec(ll):
        def index(l, i):
            i = jnp.where(l < ll, 0, jnp.where(l > ll, nb - 1, i))
            return (0, i // nc, i % nc)
        return pl.BlockSpec((N_DEV, rows, cols), index)

    blk = pl.BlockSpec((1, rows, cols), lambda l, i: (first_layer + l, i // nc, i % nc))
    out = jax.ShapeDtypeStruct((layers, R, C), F32)
    extra = [] if into is None else list(into)
    return pl.pallas_call(
        body, name=name, out_shape=(out, out, out, out), grid=(L, nb),
        in_specs=[land_spec(ll) for ll in range(L)] + [blk, blk, blk] + [pl.BlockSpec(memory_space=pl.ANY)] * len(extra),
        out_specs=(blk, blk, blk, blk),
        input_output_aliases={L + 3 + i: i for i in range(len(extra))},
        scratch_shapes=[pltpu.VMEM((rows, cols), F32)],
        compiler_params=_params(("arbitrary", "arbitrary")),
    )(*lands, w, m, v, *extra)


def _mesh_pos():
    return lax.axis_index("x"), lax.axis_index("y"), lax.axis_index("c")


def _all_gather(arrays, name):
    n = len(arrays)

    def body(*refs):
        ins, outs = refs[:n], refs[n:2 * n]
        send_sems, recv_sems, local_sems = refs[2 * n:]
        x, y, c = _mesh_pos()
        me, sibling = (x, y, c), (x, y, 1 - c)
        chips = [(1 - x, y), (x, 1 - y), (1 - x, 1 - y)]

        def slot(a, pos):
            px, py, pc = pos
            return outs[a].at[4 * px + 2 * py + pc]

        def copy(a, k, block, to, src=None):
            return pltpu.make_async_remote_copy(
                src_ref=slot(a, block) if src is None else src, dst_ref=slot(a, block),
                send_sem=send_sems.at[a * 7 + k], recv_sem=recv_sems.at[a * 7 + k],
                device_id=to, device_id_type=MESH_ID)

        mine, first, passed = [], [], []
        for a in range(n):
            cp = pltpu.make_async_copy(ins[a], slot(a, me), local_sems.at[a])
            cp.start()
            mine.append(cp)
            cps = [copy(a, 0, me, sibling, src=ins[a])]
            cps += [copy(a, 1 + j, me, (*chip, c), src=ins[a]) for j, chip in enumerate(chips)]
            for cp in cps:
                cp.start()
            first += cps
        for j, chip in enumerate(chips):
            for a in range(n):
                copy(a, 1 + j, (*chip, c), me).wait_recv()
                cp = copy(a, 4 + j, (*chip, c), sibling)
                cp.start()
                passed.append(cp)
        for a in range(n):
            copy(a, 0, sibling, me).wait_recv()
            for j, chip in enumerate(chips):
                copy(a, 4 + j, (*chip, 1 - c), me).wait_recv()
        for cp in first + passed:
            cp.wait_send()
        for cp in mine:
            cp.wait()

    hbm = pl.BlockSpec(memory_space=pltpu.HBM)
    return pl.pallas_call(
        body, name=name,
        out_shape=tuple(jax.ShapeDtypeStruct((N_DEV,) + a.shape, a.dtype) for a in arrays),
        in_specs=[hbm] * n, out_specs=tuple([hbm] * n),
        scratch_shapes=[pltpu.SemaphoreType.DMA((7 * n,)), pltpu.SemaphoreType.DMA((7 * n,)),
                        pltpu.SemaphoreType.DMA((n,))],
    )(*arrays)


def _all_gather_under_ln(arrays, x, g, b, name, tq=512):
    n = len(arrays)
    T, D = x.shape
    nt = T // tq

    def body(*refs):
        x_ref, g_ref, b_ref = refs[:3]
        ins = refs[3:3 + n]
        y_ref, yb_ref = refs[3 + n:5 + n]
        outs = refs[5 + n:5 + 2 * n]
        send_sems, recv_sems, local_sems = refs[5 + 2 * n:]
        i = pl.program_id(0)
        mx, my, mc = _mesh_pos()
        me, sibling = (mx, my, mc), (mx, my, 1 - mc)
        chips = [(1 - mx, my), (mx, 1 - my), (1 - mx, 1 - my)]

        def slot(a, pos):
            px, py, pc = pos
            return outs[a].at[4 * px + 2 * py + pc]

        def copy(a, k, block, to, src=None):
            return pltpu.make_async_remote_copy(
                src_ref=slot(a, block) if src is None else src, dst_ref=slot(a, block),
                send_sem=send_sems.at[a * 7 + k], recv_sem=recv_sems.at[a * 7 + k],
                device_id=to, device_id_type=MESH_ID)

        def own(a):
            return pltpu.make_async_copy(ins[a], slot(a, me), local_sems.at[a])

        def first(a):
            return [copy(a, 0, me, sibling, src=ins[a])] + [
                copy(a, 1 + j, me, (*chip, mc), src=ins[a]) for j, chip in enumerate(chips)]

        @pl.when(i == 0)
        def _():
            for a in range(n):
                own(a).start()
                for cp in first(a):
                    cp.start()

        zv = x_ref[...]
        mu = jnp.mean(zv, axis=1, keepdims=True)
        zc = zv - mu
        var = jnp.mean(zc * zc, axis=1, keepdims=True)
        y = zc * lax.rsqrt(var + LN_EPS) * g_ref[...] + b_ref[...]
        y_ref[...] = y
        yb_ref[...] = y.astype(BF16)

        @pl.when(i == nt - 1)
        def _():
            passed = []
            for j, chip in enumerate(chips):
                for a in range(n):
                    copy(a, 1 + j, (*chip, mc), me).wait_recv()
                    cp = copy(a, 4 + j, (*chip, mc), sibling)
                    cp.start()
                    passed.append(cp)
            for a in range(n):
                copy(a, 0, sibling, me).wait_recv()
                for j, chip in enumerate(chips):
                    copy(a, 4 + j, (*chip, 1 - mc), me).wait_recv()
            for a in range(n):
                for cp in first(a):
                    cp.wait_send()
                own(a).wait()
            for cp in passed:
                cp.wait_send()

    row = pl.BlockSpec((tq, D), lambda i: (i, 0))
    vec = pl.BlockSpec((1, D), lambda i: (0, 0))
    hbm = pl.BlockSpec(memory_space=pltpu.HBM)
    outs = pl.pallas_call(
        body, name=name,
        out_shape=(jax.ShapeDtypeStruct((T, D), F32), jax.ShapeDtypeStruct((T, D), BF16))
        + tuple(jax.ShapeDtypeStruct((N_DEV,) + a.shape, a.dtype) for a in arrays),
        grid=(nt,), in_specs=[row, vec, vec] + [hbm] * n, out_specs=tuple([row, row] + [hbm] * n),
        scratch_shapes=[pltpu.SemaphoreType.DMA((7 * n,)), pltpu.SemaphoreType.DMA((7 * n,)),
                        pltpu.SemaphoreType.DMA((n,))],
        compiler_params=_params(("arbitrary",)),
    )(x, g, b, *arrays)
    return outs[0], outs[1], outs[2:]


_HBM = pl.BlockSpec(memory_space=pltpu.HBM)
_SEM = pl.BlockSpec(memory_space=pltpu.SEMAPHORE)
_EFFECT = pltpu.SideEffectType.DATAFLOW_SIDE_EFFECTING
N_PEERS = N_DEV - 1


def _peer(k):
    x, y, c = _mesh_pos()
    return (1 - x if k & 4 else x, 1 - y if k & 2 else y, 1 - c if k & 1 else c)


def _split_start(srcs, scatter, after, name):
    n = len(srcs)
    zones = [jax.ShapeDtypeStruct(s.shape if scatter else ((N_DEV,) + s.shape), s.dtype) for s in srcs]

    def body(*refs):
        src, zone = refs[:n], refs[n:2 * n]
        outs = refs[2 * n + 1:]
        send, recv, token = outs[:n], outs[n:2 * n], outs[4 * n]
        x, y, c = _mesh_pos()
        my_idx = 4 * x + 2 * y + c
        for a in range(n):
            pltpu.make_async_copy(src[a].at[my_idx] if scatter else src[a],
                                  zone[a].at[N_PEERS] if scatter else zone[a].at[my_idx], recv[a]).start()
            for k in range(1, N_DEV):
                px, py, pc = _peer(k)
                pltpu.make_async_remote_copy(
                    src_ref=src[a].at[4 * px + 2 * py + pc] if scatter else src[a],
                    dst_ref=zone[a].at[k - 1] if scatter else zone[a].at[my_idx],
                    send_sem=send[a], recv_sem=recv[a], device_id=(px, py, pc), device_id_type=MESH_ID).start()
        token[...] = jnp.zeros_like(token)

    hbm = lambda a: pltpu.with_memory_space_constraint(a, pltpu.HBM)
    outs = pl.pallas_call(
        body, name=name,
        out_shape=tuple([pltpu.SemaphoreType.DMA(())] * (2 * n)
                        + [pltpu.HBM(s.shape, s.dtype) for s in srcs]
                        + [pltpu.HBM(z.shape, z.dtype) for z in zones]
                        + [jax.ShapeDtypeStruct((8, 128), F32)]),
        in_specs=[_HBM] * (2 * n) + [pl.BlockSpec(memory_space=pl.ANY)],
        out_specs=tuple([_SEM] * (2 * n) + [_HBM] * (2 * n) + [pl.BlockSpec(memory_space=pltpu.VMEM)]),
        input_output_aliases={**{a: 2 * n + a for a in range(n)}, **{n + a: 3 * n + a for a in range(n)}},
        compiler_params=pltpu.CompilerParams(has_side_effects=_EFFECT),
    )(*[hbm(s) for s in srcs], *[hbm(lax.empty(z.shape, z.dtype)) for z in zones], after)
    return outs[:n], outs[n:2 * n], outs[2 * n:3 * n], outs[3 * n:4 * n], outs[4 * n]


def _split_wait(send, recv, srcs, zones, after, name):
    n = len(srcs)

    def body(*refs):
        zone = refs[n:2 * n]
        send_sems, recv_sems = refs[2 * n:3 * n], refs[3 * n:4 * n]
        x, y, c = _mesh_pos()
        for a in range(n):
            seven = zone[a].at[pl.ds(0, N_PEERS)]
            pltpu.make_async_remote_copy(src_ref=seven, dst_ref=seven, send_sem=send_sems[a], recv_sem=recv_sems[a],
                                         device_id=(x, y, 1 - c), device_id_type=MESH_ID).wait_send()
            pltpu.make_async_remote_copy(src_ref=zone[a], dst_ref=zone[a], send_sem=send_sems[a],
                                         recv_sem=recv_sems[a], device_id=(x, y, 1 - c),
                                         device_id_type=MESH_ID).wait_recv()

    outs = pl.pallas_call(
        body, name=name,
        out_shape=tuple([pltpu.HBM(s.shape, s.dtype) for s in srcs] + [pltpu.HBM(z.shape, z.dtype) for z in zones]),
        in_specs=[_HBM] * (2 * n) + [_SEM] * (2 * n) + [pl.BlockSpec(memory_space=pl.ANY)],
        out_specs=tuple([_HBM] * (2 * n)),
        input_output_aliases={a: a for a in range(2 * n)},
        compiler_params=pltpu.CompilerParams(has_side_effects=_EFFECT),
    )(*srcs, *zones, *send, *recv, after)
    return outs[:n], outs[n:]


def _cat_blocks(g, axis):
    return jnp.concatenate([g[d] for d in range(N_DEV)], axis=axis)


N_LATENT = Q_LORA + KV_LORA + ROPE
W_SHARD = D_IN_PROJ // N_DEV


def _ref_cols(lo, hi):
    out = []
    if lo < N_LATENT:
        out.append((N_GATED + lo, N_GATED + min(hi, N_LATENT)))
    if hi > N_LATENT:
        out.append((max(lo, N_LATENT) - N_LATENT, hi - N_LATENT))
    return out


def _permute_w_in_t(blocks):
    pieces = []
    for lo, hi in ((N_LATENT, D_IN_PROJ), (0, N_LATENT)):
        for d in range(N_DEV):
            a, b = max(lo, d * W_SHARD), min(hi, (d + 1) * W_SHARD)
            if a < b:
                pieces.append(blocks[d][a - d * W_SHARD:b - d * W_SHARD])
    pieces.append(jnp.zeros((NPP - D_IN_PROJ, blocks.shape[2]), blocks.dtype))
    return jnp.concatenate(pieces, axis=0)


def _split_w_in_t(w):
    slabs = []
    for d in range(N_DEV):
        parts = [w[a:b] for a, b in _ref_cols(d * W_SHARD, (d + 1) * W_SHARD)]
        slabs.append(parts[0] if len(parts) == 1 else jnp.concatenate(parts, axis=0))
    return jnp.stack(slabs)


def _permute_w_uq(w):
    w3 = w.reshape(w.shape[0], N_HEADS, NOPE + ROPE)
    return jnp.concatenate([w3[:, :, :NOPE].reshape(w.shape[0], -1), w3[:, :, NOPE:].reshape(w.shape[0], -1)], axis=1)


def _unpermute_w_uq(w):
    nope = w[:, :N_HEADS * NOPE].reshape(w.shape[0], N_HEADS, NOPE)
    rope = w[:, N_HEADS * NOPE:].reshape(w.shape[0], N_HEADS, ROPE)
    return jnp.concatenate([nope, rope], axis=2).reshape(w.shape[0], -1)


_SMALL_EMB = (("emb_ln_g", 16), ("emb_ln_b", 16))
_SMALL_LAYER = (("q_norm_g", 8), ("kv_norm_g", 8), ("w_pool", 1024), ("pool_scale", 8), ("b_out", 32),
                ("ln_g", 32), ("ln_b", 32))
_SMALL = _SMALL_EMB + _SMALL_LAYER
CONV_ROWS = DEPTH * CONV_WIDTH * D_CONV // 128


def _pack_small(d, entries=_SMALL):
    parts = []
    for name, rows in entries:
        flat = d[name].reshape(-1)
        flat = jnp.pad(flat, (0, rows * 128 - flat.shape[0]))
        parts.append(flat.reshape(rows, 128))
    return jnp.concatenate(parts, axis=0)


def _unpack_small(packed, shapes):
    out, r0 = {}, 0
    for name, rows in _SMALL:
        size = 1
        for s in shapes[name]:
            size *= s
        out[name] = packed[r0:r0 + rows].reshape(-1)[:size].reshape(shapes[name])
        r0 += rows
    return out


def _rope_tables(positions):
    half = ROPE // 2
    inv_freq = ROPE_THETA ** (-jnp.arange(half, dtype=F32) / half)
    ang = positions.astype(F32)[:, None] * inv_freq
    cos, sin = jnp.cos(ang), jnp.sin(ang)
    return jnp.concatenate([cos, cos, cos, cos], axis=1), jnp.concatenate([-sin, sin, -sin, sin], axis=1)


def _local_step(x, positions, target, emb_g, emb_b, layer_weights, layer_weights_rest, on_sharded_grads,
                on_layer_grads=None, first_after=None, embedded=None):
    cos_t, sin_t = _rope_tables(positions)
    h, hb = _ln_fwd(x, emb_g, emb_b, "emb_ln_fwd") if embedded is None else embedded
    saved = []
    for l in range(DEPTH):
        W = layer_weights(l, h)
        proj = _mm(hb, W["w_in_t"], "nt", F32, "proj_fwd", after=first_after if l == 0 else None)
        qn, kvn, pooled, cv, mix = _mix_fwd(proj, W["q_norm_g"], W["kv_norm_g"], W["w_pool"], W["pool_scale"],
                                            W["conv_w"], "mix_fwd")
        rest, token = layer_weights_rest(l, proj)
        W = {**W, **rest}
        kv, qc, kc = _up_rope_fwd(qn, kvn, W["w_uq"], W["w_ukv"], proj, cos_t, sin_t, "up_rope_fwd", after=token)
        o, mix, lse = _flash_fwd(qc, kc, kv, proj, mix, "flash_fwd")
        z = _mm(mix, W["w_out"], "nn", F32, "out_fwd", res=h, bias=W["b_out"], alpha=ALPHA)
        saved.append((W, hb, proj, qn, kvn, pooled, cv, kv, qc, kc, o, lse, mix, z))
        last = l == DEPTH - 1
        h, hb = _ln_fwd(z, W["ln_g"], W["ln_b"], "ln_fwd_out" if last else "ln_fwd", for_matmul=not last)
    dh, sq = h, None

    grads = {k: [None] * DEPTH for k in ("q_norm_g", "kv_norm_g", "w_pool", "pool_scale", "conv_w", "b_out", "ln_g",
                                         "ln_b")}
    for l in reversed(range(DEPTH)):
        W, hb_in, proj, qn, kvn, pooled, cv, kv, qc, kc, o, lse, mix, z = saved[l]
        sharded = {}
        if l == DEPTH - 1:
            dz, dzb, grads["b_out"][l], grads["ln_g"][l], grads["ln_b"][l], sq = _ln_bwd(
                dh, z, W["ln_g"], "ln_bwd_loss", target=target)
        else:
            dz, dzb, grads["b_out"][l], grads["ln_g"][l], grads["ln_b"][l] = _ln_bwd(dh, z, W["ln_g"], "ln_bwd")
        dmix = _mm(dzb, W["w_out"], "nt", F32, "out_bwd_x")
        sharded["w_out"] = _mm(mix, dzb, "tn", GRAD_XFER, "out_bwd_w", tk=4096)
        do, delta, dproj, grads["w_pool"][l], grads["pool_scale"][l], grads["conv_w"][l] = _mix_bwd(
            dmix, proj, o, pooled, cv, W["w_pool"], W["pool_scale"], W["conv_w"], "mix_bwd")
        dqc, dkv, dkr = _flash_bwd(qc, kc, kv, do, lse, delta, "flash_bwd")
        dq, dkrope = _rope_bwd(dqc, dkr, cos_t, sin_t, "rope_bwd")
        sharded["w_uq"] = _mm(qn, dq, "tn", GRAD_XFER, "q_up_bwd_w")
        sharded["w_ukv"] = _mm(kvn, dkv, "tn", GRAD_XFER, "kv_up_bwd_w")
        token = on_sharded_grads(l, sharded)
        dproj, grads["q_norm_g"][l], grads["kv_norm_g"][l] = _up_rms_bwd(
            proj, dq, dkv, W["w_uq"], W["w_ukv"], dkrope, dproj, W["q_norm_g"], W["kv_norm_g"], "up_rms_bwd")
        if l == 0 and on_layer_grads is not None:
            token = on_layer_grads(grads, token)
        d_w_in_t = _mm(dproj, hb_in, "tn", GRAD_XFER, "proj_bwd_w", tk=4096, after=token)
        token = on_sharded_grads(l, {"w_in": d_w_in_t})
        dh = _mm(dproj, W["w_in_t"], "nn", F32, "proj_bwd_x", res=dz, alpha=ALPHA, tk=2560, after=token)
    grad_x, grads["emb_ln_g"], grads["emb_ln_b"] = _ln_bwd(dh, x, emb_g, "emb_ln_bwd", for_matmul=False)
    return sq, grad_x, grads


def kernel(x, positions, emb_ln_g, emb_ln_b, w_in, q_norm_g, kv_norm_g, w_uq, w_ukv, w_pool, pool_scale, conv_w, w_out, b_out, ln_g, ln_b, loss_target, m_emb_ln_g, m_emb_ln_b, m_w_in, m_q_norm_g, m_kv_norm_g, m_w_uq, m_w_ukv, m_w_pool, m_pool_scale, m_conv_w, m_w_out, m_b_out, m_ln_g, m_ln_b, v_emb_ln_g, v_emb_ln_b, v_w_in, v_q_norm_g, v_kv_norm_g, v_w_uq, v_w_ukv, v_w_pool, v_pool_scale, v_conv_w, v_w_out, v_b_out, v_ln_g, v_ln_b):
    weights = dict(emb_ln_g=emb_ln_g, emb_ln_b=emb_ln_b, w_in=w_in, q_norm_g=q_norm_g, kv_norm_g=kv_norm_g,
                   w_uq=w_uq, w_ukv=w_ukv, w_pool=w_pool, pool_scale=pool_scale, conv_w=conv_w, w_out=w_out,
                   b_out=b_out, ln_g=ln_g, ln_b=ln_b)
    mom1 = dict(emb_ln_g=m_emb_ln_g, emb_ln_b=m_emb_ln_b, w_in=m_w_in, q_norm_g=m_q_norm_g, kv_norm_g=m_kv_norm_g,
                w_uq=m_w_uq, w_ukv=m_w_ukv, w_pool=m_w_pool, pool_scale=m_pool_scale, conv_w=m_conv_w,
                w_out=m_w_out, b_out=m_b_out, ln_g=m_ln_g, ln_b=m_ln_b)
    mom2 = dict(emb_ln_g=v_emb_ln_g, emb_ln_b=v_emb_ln_b, w_in=v_w_in, q_norm_g=v_q_norm_g, kv_norm_g=v_kv_norm_g,
                w_uq=v_w_uq, w_ukv=v_w_ukv, w_pool=v_w_pool, pool_scale=v_pool_scale, conv_w=v_conv_w,
                w_out=v_w_out, b_out=v_b_out, ln_g=v_ln_g, ln_b=v_ln_b)

    big = ("w_in", "w_uq", "w_ukv", "w_out")

    conv_pad = jnp.zeros((8, 128), F32).at[0:DEPTH * CONV_WIDTH, 0:64].set(conv_w.reshape(DEPTH * CONV_WIDTH, 64))
    t12 = lambda a: jnp.swapaxes(a, 1, 2)
    shard = lambda k, l: (t12(weights[k])[l] if k == "w_in" else weights[k][l]).astype(BF16)
    h0, h0b, (w_in0, conv_all) = _all_gather_under_ln(
        [shard("w_in", 0), conv_pad], x[0], emb_ln_g.reshape(1, -1), emb_ln_b.reshape(1, -1), "w_in0_all_gather_emb_ln")
    rest0 = _split_start([shard(k, 0) for k in big[1:]], False, w_in0, "weights0_rest_start")
    conv_full = _cat_blocks(conv_all[:, 0:DEPTH * CONV_WIDTH, 0:64], 1).reshape(DEPTH, CONV_WIDTH, D_CONV)
    conv_full = jnp.pad(conv_full, ((0, 0), (0, 8 - CONV_WIDTH), (0, 0)))
    fetched = {}

    def layer_weights(l, ready):
        if l == 0:
            w_in_blocks = w_in0
        else:
            fetched[1] = _split_wait(*fetched["w1"][:4], ready, "weights1_wait")[1]
            w_in_blocks = fetched[1][0]
        return dict(
            w_in_t=_permute_w_in_t(w_in_blocks), conv_w=conv_full[l],
            q_norm_g=q_norm_g[l].reshape(1, -1), kv_norm_g=kv_norm_g[l].reshape(1, -1),
            w_pool=w_pool[l].astype(BF16), pool_scale=pool_scale[l].reshape(1, -1), b_out=b_out[l].reshape(1, -1),
            ln_g=ln_g[l].reshape(1, -1), ln_b=ln_b[l].reshape(1, -1))

    def layer_weights_rest(l, ready):
        token = None
        if l == 0:
            blocks = _split_wait(*rest0[:4], ready, "weights0_rest_wait")[1]
            fetched["w1"] = _split_start([shard(k, 1) for k in big], False, blocks[0], "weights1_start")
            token = fetched["w1"][4]
        else:
            blocks = fetched[1][1:]
        return dict(w_uq=_permute_w_uq(_cat_blocks(blocks[0], 1)), w_ukv=_cat_blocks(blocks[1], 1),
                    w_out=blocks[2].reshape(D_MIX, D_MODEL)), token

    by_dest = dict(
        w_in=_split_w_in_t,
        w_uq=lambda g: _unpermute_w_uq(g).reshape(Q_LORA, N_DEV, -1).transpose(1, 0, 2),
        w_ukv=lambda g: g.reshape(KV_LORA, N_DEV, -1).transpose(1, 0, 2),
        w_out=lambda g: g.reshape(N_DEV, -1, D_MODEL))
    in_flight = []

    def on_sharded_grads(l, g):
        names = [k for k in big if k in g]
        srcs = [by_dest[k](g[k]) for k in names]
        started = _split_start(srcs, True, srcs[0], "grads%d_%s_start" % (l, names[0]))
        in_flight.append((l, names, started[:4]))
        return started[4]

    small_in_flight = []

    def on_layer_grads(g, token):
        stacked = {k: jnp.stack(g[k]) for k, _ in _SMALL_LAYER}
        conv = jnp.stack([g["conv_w"][l][0:CONV_WIDTH] for l in range(DEPTH)]).reshape(CONV_ROWS, 128)
        packed = jnp.concatenate([_pack_small(stacked, _SMALL_LAYER), conv], axis=0)
        started = _split_start([packed], False, token, "layer_grads_start")
        small_in_flight.append(started[:4])
        return started[4]

    sq, grad_x, G = _local_step(x[0], positions[0], loss_target[0], emb_ln_g.reshape(1, -1),
                                emb_ln_b.reshape(1, -1), layer_weights, layer_weights_rest, on_sharded_grads,
                                on_layer_grads, first_after=rest0[4], embedded=(h0, h0b))

    res = {}
    landed = {}
    for l, names, started in in_flight:
        zones = _split_wait(*started, grad_x, "grads%d_%s_wait" % (l, names[0]))[1]
        for k, zone in zip(names, zones):
            landed[k, l] = zone
    w_in_res = None
    for l in reversed(range(DEPTH)):
        w_in_res = _adamw([landed["w_in", l]], t12(w_in), t12(m_w_in), t12(v_w_in), "adamw_w_in_%d" % l, W_SHARD, 512,
                          first_layer=l, into=w_in_res)
    res["w_in"] = tuple(t12(o) for o in w_in_res)
    for name, rows in (("w_uq", 256), ("w_ukv", 256), ("w_out", 128)):
        res[name] = _adamw([landed[name, l] for l in range(DEPTH)], weights[name], mom1[name], mom2[name],
                           "adamw_" + name, rows)

    layer_zone = _split_wait(*small_in_flight[0], grad_x, "layer_grads_wait")[1][0]
    n_emb_rows = sum(r for _, r in _SMALL_EMB)
    emb_zone = _all_gather([jnp.concatenate([_pack_small(G, _SMALL_EMB), sq], axis=0)], "emb_grads_all_gather")[0]
    loss = jnp.sum(emb_zone[:, n_emb_rows, 0]) * (0.5 / D_MODEL)
    n_layer_rows = sum(r for _, r in _SMALL_LAYER)
    l_small = jnp.concatenate([emb_zone[:, 0:n_emb_rows], layer_zone[:, 0:n_layer_rows]], axis=1)
    my_idx = 4 * lax.axis_index("x") + 2 * lax.axis_index("y") + lax.axis_index("c")
    conv_all_grads = layer_zone[:, n_layer_rows:].reshape(N_DEV, DEPTH * CONV_WIDTH, D_CONV)
    l_conv = lax.dynamic_slice_in_dim(conv_all_grads, my_idx * 64, 64, axis=2)
    l_conv = jnp.zeros((N_DEV, 8, 128), F32).at[:, 0:DEPTH * CONV_WIDTH, 0:64].set(l_conv)
    conv_shard = lambda a: jnp.zeros((8, 128), F32).at[0:DEPTH * CONV_WIDTH, 0:64].set(a.reshape(-1, 64))
    conv_res = _adamw([l_conv], conv_shard(conv_w)[None], conv_shard(m_conv_w)[None], conv_shard(v_conv_w)[None],
                      "adamw_conv_w", 8)
    res["conv_w"] = tuple(o[0, 0:DEPTH * CONV_WIDTH, 0:64].reshape(DEPTH, CONV_WIDTH, 64) for o in conv_res)
    small_res = _adamw([l_small], _pack_small(weights)[None], _pack_small(mom1)[None], _pack_small(mom2)[None],
                       "adamw_small", 392)
    shapes = {k: weights[k].shape for k, _ in _SMALL}
    unpacked = [_unpack_small(o[0], shapes) for o in small_res]
    for k, _ in _SMALL:
        res[k] = tuple(u[k] for u in unpacked)

    order = ("emb_ln_g", "emb_ln_b", "w_in", "q_norm_g", "kv_norm_g", "w_uq", "w_ukv", "w_pool", "pool_scale",
             "conv_w", "w_out", "b_out", "ln_g", "ln_b")
    return (loss, grad_x[None], *[res[k][0] for k in order], *[res[k][1] for k in order],
            *[res[k][2] for k in order], *[res[k][3] for k in order])
```

```python
import jax
import jax.numpy as jnp
from jax import lax
from jax.experimental import pallas as pl
from jax.experimental.pallas import tpu as pltpu

F32 = jnp.float32
BF16 = jnp.bfloat16

N_DEV = 8
D_MODEL = 2048
DEPTH = 2
N_HEADS = 8
NOPE = 128
ROPE = 64
V_DIM = 128
Q_LORA = 512
KV_LORA = 256
D_MLA = N_HEADS * V_DIM
D_POOL = 512
D_CONV = 512
POOL_WINDOWS = (2, 4, 8, 16)
POOL_GROUP = 128
CONV_WIDTH = 3
D_MIX = D_MLA + D_POOL + D_CONV
D_IN_PROJ = 4928
ROPE_THETA = 10000.0
LN_EPS = 1e-5
RMS_EPS = 1e-6
ALPHA = (2 * DEPTH) ** 0.25
ATTN_SCALE = (NOPE + ROPE) ** -0.5
ADAM_LR = 0.001
ADAM_B1 = 0.9
ADAM_B2 = 0.999
ADAM_EPS = 1e-08
ADAM_WD = 0.01
ADAM_STEP = 10

O_GMLA, O_PIN, O_GPOOL, O_CH, O_CB, O_CC, O_GCONV, O_QLAT, O_KVLAT, O_KROPE = (
    0, 1024, 1536, 2048, 2560, 3072, 3584, 4096, 4608, 4864)
NPP = 5120
N_GATED = O_QLAT
QC = NOPE + 2 * ROPE
HALO = 16
ATT_TILE = 512
ATT_CH = 256
LOG2E = 1.4426950408889634
EXP2_SCALE = ATTN_SCALE * LOG2E

GRAD_XFER = BF16
VMEM_LIMIT = 48 * 1024 * 1024
ATT_BWD_VMEM_LIMIT = 58 * 1024 * 1024
MESH_ID = pl.DeviceIdType.MESH


def _params(sem=None):
    return pltpu.CompilerParams(dimension_semantics=sem, vmem_limit_bytes=VMEM_LIMIT)


def _sigmoid(x):
    return 1.0 / (1.0 + jnp.exp(-x))


def _tile(dim, target):
    if dim <= target:
        return dim
    t = target - target % 128
    while dim % t:
        t -= 128
    return t


_DIMS = {"nn": (((1,), (0,)), ((), ())), "nt": (((1,), (1,)), ((), ())), "tn": (((0,), (0,)), ((), ()))}


def _mm(a, b, mode, out_dtype, name, res=None, bias=None, alpha=1.0, tm=1024, tn=1024, tk=2048, after=None):
    if mode == "nn":
        (M, K), (K2, N) = a.shape, b.shape
    elif mode == "nt":
        (M, K), (N, K2) = a.shape, b.shape
    else:
        (K, M), (K2, N) = a.shape, b.shape
    assert K == K2
    tm, tn, tk = _tile(M, tm), _tile(N, tn), _tile(K, tk)
    nk = K // tk
    has_res, has_bias = res is not None, bias is not None

    def body(*refs):
        a_ref, b_ref = refs[0], refs[1]
        pos = 2
        res_ref = bias_ref = None
        if has_res:
            res_ref = refs[pos]
            pos += 1
        if has_bias:
            bias_ref = refs[pos]
            pos += 1
        def finish(r, o_ref):
            if has_bias:
                r = r + bias_ref[...]
            if has_res:
                r = alpha * res_ref[...] + r
            o_ref[...] = r.astype(out_dtype)

        part = lax.dot_general(a_ref[...].astype(BF16), b_ref[...].astype(BF16), _DIMS[mode],
                               preferred_element_type=F32)
        if nk == 1:
            finish(part, refs[-1])
            return
        o_ref, acc_ref = refs[-2], refs[-1]
        k = pl.program_id(2)

        @pl.when(k == 0)
        def _():
            acc_ref[...] = part

        @pl.when(jnp.logical_and(k > 0, k < nk - 1))
        def _():
            acc_ref[...] += part

        @pl.when(k == nk - 1)
        def _():
            finish(acc_ref[...] + part, o_ref)

    if mode == "nn":
        in_specs = [pl.BlockSpec((tm, tk), lambda i, j, k: (i, k)), pl.BlockSpec((tk, tn), lambda i, j, k: (k, j))]
    elif mode == "nt":
        in_specs = [pl.BlockSpec((tm, tk), lambda i, j, k: (i, k)), pl.BlockSpec((tn, tk), lambda i, j, k: (j, k))]
    else:
        in_specs = [pl.BlockSpec((tk, tm), lambda i, j, k: (k, i)), pl.BlockSpec((tk, tn), lambda i, j, k: (k, j))]
    args = [a, b]
    if has_res:
        in_specs.append(pl.BlockSpec((tm, tn), lambda i, j, k: (i, j)))
        args.append(res)
    if has_bias:
        in_specs.append(pl.BlockSpec((1, tn), lambda i, j, k: (0, j)))
        args.append(bias)
    if after is not None:
        in_specs.append(pl.BlockSpec((8, 128), lambda i, j, k: (0, 0)))
        args.append(after)
    return pl.pallas_call(
        body, name=name,
        out_shape=jax.ShapeDtypeStruct((M, N), out_dtype),
        grid=(M // tm, N // tn, nk),
        in_specs=in_specs,
        out_specs=pl.BlockSpec((tm, tn), lambda i, j, k: (i, j)),
        scratch_shapes=[pltpu.VMEM((tm, tn), F32)] if nk > 1 else [],
        compiler_params=_params(("parallel", "parallel", "arbitrary")),
    )(*args)


def _ln_fwd(z, g, b, name, tq=512, for_matmul=True):
    T, D = z.shape

    def body(z_ref, g_ref, b_ref, y_ref, *yb_ref):
        zv = z_ref[...]
        mu = jnp.mean(zv, axis=1, keepdims=True)
        zc = zv - mu
        var = jnp.mean(zc * zc, axis=1, keepdims=True)
        y = zc * lax.rsqrt(var + LN_EPS) * g_ref[...] + b_ref[...]
        y_ref[...] = y
        if for_matmul:
            yb_ref[0][...] = y.astype(BF16)

    row = pl.BlockSpec((tq, D), lambda i: (i, 0))
    vec = pl.BlockSpec((1, D), lambda i: (0, 0))
    outs = pl.pallas_call(
        body, name=name,
        out_shape=(jax.ShapeDtypeStruct((T, D), F32),) + ((jax.ShapeDtypeStruct((T, D), BF16),) if for_matmul else ()),
        grid=(T // tq,), in_specs=[row, vec, vec], out_specs=(row,) * (2 if for_matmul else 1),
        compiler_params=_params(("parallel",)),
    )(z, g, b)
    return (outs[0], outs[1]) if for_matmul else (outs[0], None)


def _ln_bwd(dy, z, g, name, tq=512, target=None, for_matmul=True):
    T, D = z.shape
    with_loss = target is not None

    def body(*refs):
        dy_ref, z_ref, g_ref = refs[:3]
        outs = list(refs[4 if with_loss else 3:])
        dz_ref = outs.pop(0)
        dzb_ref, ds_ref = (outs.pop(0), outs.pop(0)) if for_matmul else (None, None)
        dg_ref, db_ref = outs.pop(0), outs.pop(0)
        sq_ref = outs.pop(0) if with_loss else None

        @pl.when(pl.program_id(0) == 0)
        def _():
            for ref in (dg_ref, db_ref, ds_ref, sq_ref):
                if ref is not None:
                    ref[...] = jnp.zeros_like(ref)

        zv, dyv = z_ref[...], dy_ref[...]
        if with_loss:
            err = dyv - refs[3][...]
            sq_ref[...] += jnp.sum(err * err)
            dyv = err * (1.0 / D)
        mu = jnp.mean(zv, axis=1, keepdims=True)
        zc = zv - mu
        var = jnp.mean(zc * zc, axis=1, keepdims=True)
        rstd = lax.rsqrt(var + LN_EPS)
        xh = zc * rstd
        u = dyv * g_ref[...]
        dz = rstd * (u - jnp.mean(u, axis=1, keepdims=True) - xh * jnp.mean(u * xh, axis=1, keepdims=True))
        dz_ref[...] = dz
        dg_ref[...] += jnp.sum(dyv * xh, axis=0, keepdims=True)
        db_ref[...] += jnp.sum(dyv, axis=0, keepdims=True)
        if for_matmul:
            dzb_ref[...] = dz.astype(BF16)
            ds_ref[...] += jnp.sum(dz, axis=0, keepdims=True)

    row = pl.BlockSpec((tq, D), lambda i: (i, 0))
    vec = pl.BlockSpec((1, D), lambda i: (0, 0))
    vshape = jax.ShapeDtypeStruct((1, D), F32)
    out_shape, out_specs = [jax.ShapeDtypeStruct((T, D), F32)], [row]
    if for_matmul:
        out_shape += [jax.ShapeDtypeStruct((T, D), BF16), vshape]
        out_specs += [row, vec]
    out_shape += [vshape, vshape]
    out_specs += [vec, vec]
    if with_loss:
        out_shape.append(jax.ShapeDtypeStruct((8, 128), F32))
        out_specs.append(pl.BlockSpec((8, 128), lambda i: (0, 0)))
    return pl.pallas_call(
        body, name=name, out_shape=tuple(out_shape), grid=(T // tq,),
        in_specs=[row, row, vec] + ([row] if with_loss else []), out_specs=tuple(out_specs),
        compiler_params=_params(("arbitrary",)),
    )(dy, z, g, *([target] if with_loss else []))


def _pblock(tq, width, offset):
    assert offset % width == 0
    blk = offset // width
    return pl.BlockSpec((tq, width), lambda i: (i, blk))


def _mix_fwd(proj, q_g, kv_g, w_pool, pool_scale, conv_w, name, tq=256):
    T = proj.shape[0]

    def body(ql_ref, kvl_ref, pin_ref, gp_ref, ch_ref, cb_ref, cc_ref, gc_ref, qg_ref, kvg_ref, wp_ref, ps_ref,
             cw_ref, qn_ref, kvn_ref, pooled_ref, cv_ref, ypc_ref, extp, extu):
        i = pl.program_id(0)
        for x_ref, g_ref, o_ref in ((ql_ref, qg_ref, qn_ref), (kvl_ref, kvg_ref, kvn_ref)):
            x = x_ref[...]
            r = lax.rsqrt(jnp.mean(x * x, axis=1, keepdims=True) + RMS_EPS)
            o_ref[...] = (x * r * g_ref[...]).astype(BF16)

        @pl.when(i == 0)
        def _():
            extp[0:HALO, :] = jnp.zeros((HALO, D_POOL), F32)
            extu[0:HALO, :] = jnp.zeros((HALO, D_CONV), F32)

        @pl.when(i > 0)
        def _():
            extp[0:HALO, :] = extp[tq:tq + HALO, :]
            extu[0:HALO, :] = extu[tq:tq + HALO, :]

        t1 = (i * tq + lax.broadcasted_iota(jnp.int32, (tq, 1), 0) + 1).astype(F32)
        for g, w in enumerate(POOL_WINDOWS):
            cols = slice(g * POOL_GROUP, (g + 1) * POOL_GROUP)
            pin = pin_ref[:, cols]
            extp[HALO:, cols] = pin
            s = extp[:, cols]
            k = 1
            while k < w:
                s = s + pltpu.roll(s, k, 0)
                k *= 2
            mean = s[HALO:, :] / jnp.minimum(t1, float(w))
            pooled = (mean - pin).astype(BF16)
            pooled_ref[:, cols] = pooled
            r = jnp.dot(pooled, wp_ref[g], preferred_element_type=F32)
            gp = gp_ref[:, cols]
            ypc_ref[:, cols] = (r * ps_ref[:, cols] * (gp * _sigmoid(gp))).astype(BF16)
        for g in range(D_CONV // 128):
            cols = slice(g * 128, (g + 1) * 128)
            u = cc_ref[:, cols] * ch_ref[:, cols]
            extu[HALO:, cols] = u
            eu = extu[:, cols]
            u1 = pltpu.roll(eu, 1, 0)[HALO:, :]
            u2 = pltpu.roll(eu, 2, 0)[HALO:, :]
            cv = cw_ref[0:1, cols] * u2 + cw_ref[1:2, cols] * u1 + cw_ref[2:3, cols] * u
            cv_ref[:, cols] = cv
            gc = gc_ref[:, cols]
            ypc_ref[:, D_POOL + g * 128:D_POOL + (g + 1) * 128] = (
                cb_ref[:, cols] * cv * (gc * _sigmoid(gc))).astype(BF16)

    full = lambda shape: pl.BlockSpec(shape, lambda i: (0,) * len(shape))
    row = lambda w: pl.BlockSpec((tq, w), lambda i: (i, 0))
    return pl.pallas_call(
        body, name=name,
        out_shape=(jax.ShapeDtypeStruct((T, Q_LORA), BF16), jax.ShapeDtypeStruct((T, KV_LORA), BF16),
                   jax.ShapeDtypeStruct((T, D_POOL), BF16), jax.ShapeDtypeStruct((T, D_CONV), F32),
                   jax.ShapeDtypeStruct((T, D_MIX), BF16)),
        grid=(T // tq,),
        in_specs=[_pblock(tq, Q_LORA, O_QLAT), _pblock(tq, KV_LORA, O_KVLAT), _pblock(tq, 512, O_PIN),
                  _pblock(tq, 512, O_GPOOL), _pblock(tq, 512, O_CH), _pblock(tq, 512, O_CB), _pblock(tq, 512, O_CC),
                  _pblock(tq, 512, O_GCONV), full((1, Q_LORA)), full((1, KV_LORA)), full((4, 128, 128)),
                  full((1, D_POOL)), full((8, D_CONV))],
        out_specs=(row(Q_LORA), row(KV_LORA), row(D_POOL), row(D_CONV),
                   pl.BlockSpec((tq, D_POOL + D_CONV), lambda i: (i, D_MLA // (D_POOL + D_CONV)))),
        scratch_shapes=[pltpu.VMEM((tq + HALO, D_POOL), F32), pltpu.VMEM((tq + HALO, D_CONV), F32)],
        compiler_params=_params(("arbitrary",)),
    )(proj, proj, proj, proj, proj, proj, proj, proj, q_g, kv_g, w_pool, pool_scale, conv_w)


def _mix_bwd(dmix, proj, o, pooled, cv, w_pool, pool_scale, conv_w, name, tq=ATT_CH):
    T = proj.shape[0]
    nt = T // tq
    n_ext = tq + HALO

    def body(dym_ref, dyp_ref, dyc_ref, gm_ref, gp_ref, ch_ref, cb_ref, cc_ref, gc_ref, o_ref, pooled_ref, cv_ref,
             wp_ref, ps_ref, cw_ref, do_ref, delta_ref, dg_ref, dwp_ref, dps_ref, dcw_ref, exte, extd):
        i = pl.program_id(0)
        tile = nt - 1 - i

        @pl.when(i == 0)
        def _():
            dwp_ref[...] = jnp.zeros_like(dwp_ref)
            dps_ref[...] = jnp.zeros_like(dps_ref)
            dcw_ref[...] = jnp.zeros_like(dcw_ref)
            exte[tq:, :] = jnp.zeros((HALO, D_POOL), F32)
            extd[tq:, :] = jnp.zeros((HALO, D_CONV), F32)

        @pl.when(i > 0)
        def _():
            exte[tq:, :] = exte[0:HALO, :]
            extd[tq:, :] = extd[0:HALO, :]

        ones = jnp.ones((8, V_DIM), F32)
        for h in range(N_HEADS):
            cols = slice(h * V_DIM, (h + 1) * V_DIM)
            gm = gm_ref[:, cols]
            sig = _sigmoid(gm)
            dym = dym_ref[:, cols]
            ov = o_ref[:, cols]
            do = dym * (gm * sig)
            do_ref[:, cols] = do.astype(BF16)
            rows = lax.dot_general(ones, do * ov, _DIMS["nt"], precision=lax.Precision.HIGHEST,
                                   preferred_element_type=F32)
            delta_ref[h, 0] = rows[0:1, :]
            dg_ref[:, O_GMLA + h * V_DIM:O_GMLA + (h + 1) * V_DIM] = (
                dym * ov * (sig * (1.0 + gm * (1.0 - sig)))).astype(BF16)

        t1 = (tile * tq + lax.broadcasted_iota(jnp.int32, (tq, 1), 0) + 1).astype(F32)
        for g, w in enumerate(POOL_WINDOWS):
            cols = slice(g * POOL_GROUP, (g + 1) * POOL_GROUP)
            pg = pooled_ref[:, cols]
            r = jnp.dot(pg, wp_ref[g], preferred_element_type=F32)
            gp = gp_ref[:, cols]
            sg = _sigmoid(gp)
            sl = gp * sg
            dyg = dyp_ref[:, cols]
            ps = ps_ref[:, cols]
            dg_ref[:, O_GPOOL + g * POOL_GROUP:O_GPOOL + (g + 1) * POOL_GROUP] = (
                dyg * (r * ps) * (sg * (1.0 + gp * (1.0 - sg)))).astype(BF16)
            dps_ref[:, cols] += jnp.sum(dyg * r * sl, axis=0, keepdims=True)
            dr = (dyg * ps * sl).astype(BF16)
            dwp_ref[g] += lax.dot_general(pg, dr, _DIMS["tn"], preferred_element_type=F32)
            dpooled = lax.dot_general(dr, wp_ref[g], _DIMS["nt"], preferred_element_type=F32)
            exte[0:tq, cols] = dpooled / jnp.minimum(t1, float(w))
            s = exte[:, cols]
            k = 1
            while k < w:
                s = s + pltpu.roll(s, n_ext - k, 0)
                k *= 2
            dg_ref[:, O_PIN + g * POOL_GROUP:O_PIN + (g + 1) * POOL_GROUP] = (s[0:tq, :] - dpooled).astype(BF16)

        for g in range(D_CONV // 128):
            cols = slice(g * 128, (g + 1) * 128)
            out = lambda base: slice(base + g * 128, base + (g + 1) * 128)
            gc = gc_ref[:, cols]
            sg = _sigmoid(gc)
            sl = gc * sg
            dyc = dyc_ref[:, cols]
            cb, cc, ch, cvv = cb_ref[:, cols], cc_ref[:, cols], ch_ref[:, cols], cv_ref[:, cols]
            dcv = dyc * cb * sl
            dg_ref[:, out(O_GCONV)] = (dyc * (cb * cvv) * (sg * (1.0 + gc * (1.0 - sg)))).astype(BF16)
            dg_ref[:, out(O_CB)] = (dyc * cvv * sl).astype(BF16)
            extd[0:tq, cols] = dcv
            ed = extd[:, cols]
            d1 = pltpu.roll(ed, n_ext - 1, 0)[0:tq, :]
            d2 = pltpu.roll(ed, n_ext - 2, 0)[0:tq, :]
            du = cw_ref[2:3, cols] * dcv + cw_ref[1:2, cols] * d1 + cw_ref[0:1, cols] * d2
            u = cc * ch
            dcw_ref[0:1, cols] += jnp.sum(u * d2, axis=0, keepdims=True)
            dcw_ref[1:2, cols] += jnp.sum(u * d1, axis=0, keepdims=True)
            dcw_ref[2:3, cols] += jnp.sum(u * dcv, axis=0, keepdims=True)
            dg_ref[:, out(O_CH)] = (du * cc).astype(BF16)
            dg_ref[:, out(O_CC)] = (du * ch).astype(BF16)

    def rblock(width, offset):
        assert offset % width == 0
        blk = offset // width
        return pl.BlockSpec((tq, width), lambda i: (nt - 1 - i, blk))

    full = lambda shape: pl.BlockSpec(shape, lambda i: (0,) * len(shape))
    return pl.pallas_call(
        body, name=name,
        out_shape=(jax.ShapeDtypeStruct((T, D_MLA), BF16), jax.ShapeDtypeStruct((N_HEADS, nt, 1, tq), F32),
                   jax.ShapeDtypeStruct((T, NPP), BF16),
                   jax.ShapeDtypeStruct((4, 128, 128), F32), jax.ShapeDtypeStruct((1, D_POOL), F32),
                   jax.ShapeDtypeStruct((8, D_CONV), F32)),
        grid=(nt,),
        in_specs=[rblock(1024, 0), rblock(512, 1024), rblock(512, 1536),
                  rblock(1024, O_GMLA), rblock(512, O_GPOOL), rblock(512, O_CH), rblock(512, O_CB),
                  rblock(512, O_CC), rblock(512, O_GCONV), rblock(1024, 0), rblock(512, 0), rblock(512, 0),
                  full((4, 128, 128)), full((1, D_POOL)), full((8, D_CONV))],
        out_specs=(rblock(1024, 0), pl.BlockSpec((N_HEADS, 1, 1, tq), lambda i: (0, nt - 1 - i, 0, 0)),
                   rblock(N_GATED, 0), full((4, 128, 128)), full((1, D_POOL)), full((8, D_CONV))),
        scratch_shapes=[pltpu.VMEM((n_ext, D_POOL), F32), pltpu.VMEM((n_ext, D_CONV), F32)],
        compiler_params=_params(("arbitrary",)),
    )(dmix, dmix, dmix, proj, proj, proj, proj, proj, proj, o, pooled, cv, w_pool, pool_scale, conv_w)


def _up_rms_bwd(proj, dq, dkv, w_uq, w_ukv, dkrope, dproj, q_g, kv_g, name, tq=256):
    T = proj.shape[0]
    n_lat = NPP - N_GATED

    def body(ql_ref, kvl_ref, dq_ref, dkv_ref, wq_ref, wkv_ref, dkr_ref, _, qg_ref, kvg_ref, dlat_ref, dqg_ref,
             dkvg_ref):
        @pl.when(pl.program_id(0) == 0)
        def _():
            dqg_ref[...] = jnp.zeros_like(dqg_ref)
            dkvg_ref[...] = jnp.zeros_like(dkvg_ref)

        dqn = lax.dot_general(dq_ref[...], wq_ref[...], _DIMS["nt"], preferred_element_type=F32)
        dkvn = lax.dot_general(dkv_ref[...], wkv_ref[...], _DIMS["nt"], preferred_element_type=F32)
        for x_ref, dy, g_ref, c0, dg_ref in ((ql_ref, dqn, qg_ref, 0, dqg_ref),
                                             (kvl_ref, dkvn, kvg_ref, Q_LORA, dkvg_ref)):
            x = x_ref[...]
            r = lax.rsqrt(jnp.mean(x * x, axis=1, keepdims=True) + RMS_EPS)
            xr = x * r
            u = dy * g_ref[...]
            dlat_ref[:, c0:c0 + x.shape[1]] = (r * (u - xr * jnp.mean(u * xr, axis=1, keepdims=True))).astype(BF16)
            dg_ref[...] += jnp.sum(dy * xr, axis=0, keepdims=True)
        dlat_ref[:, Q_LORA + KV_LORA:] = dkr_ref[...]

    row = lambda w: pl.BlockSpec((tq, w), lambda i: (i, 0))
    vec = lambda w: pl.BlockSpec((1, w), lambda i: (0, 0))
    assert N_GATED % n_lat == 0
    return pl.pallas_call(
        body, name=name,
        out_shape=(jax.ShapeDtypeStruct((T, NPP), BF16),
                   jax.ShapeDtypeStruct((1, Q_LORA), F32), jax.ShapeDtypeStruct((1, KV_LORA), F32)),
        grid=(T // tq,),
        in_specs=[_pblock(tq, Q_LORA, O_QLAT), _pblock(tq, KV_LORA, O_KVLAT), row(dq.shape[1]), row(dkv.shape[1]),
                  pl.BlockSpec(w_uq.shape, lambda i: (0, 0)), pl.BlockSpec(w_ukv.shape, lambda i: (0, 0)),
                  row(n_lat - Q_LORA - KV_LORA), pl.BlockSpec(memory_space=pl.ANY), vec(Q_LORA), vec(KV_LORA)],
        out_specs=(pl.BlockSpec((tq, n_lat), lambda i: (i, N_GATED // n_lat)), vec(Q_LORA), vec(KV_LORA)),
        input_output_aliases={7: 0},
        compiler_params=_params(("arbitrary",)),
    )(proj, proj, dq, dkv, w_uq, w_ukv, dkrope, dproj, q_g, kv_g)


def _swap_halves(x, lo):
    return jnp.where(lo, pltpu.roll(x, 96, 1), pltpu.roll(x, 32, 1))


def _up_rope_fwd(qn, kvn, w_uq, w_ukv, proj, cos_t, sin_t, name, tq=256, after=None):
    T = qn.shape[0]

    def body(qn_ref, kvn_ref, wq_ref, wkv_ref, kr_ref, c_ref, s_ref, *rest):
        kv_ref, qc_ref, kc_ref = rest[-3:]
        q = jnp.dot(qn_ref[...], wq_ref[...], preferred_element_type=F32)
        kv_ref[...] = jnp.dot(kvn_ref[...], wkv_ref[...], preferred_element_type=F32).astype(BF16)
        C, S = c_ref[...], s_ref[...]
        lane = lax.broadcasted_iota(jnp.int32, (tq, 128), 1)
        lo = (lane % ROPE) < (ROPE // 2)
        first = lane < ROPE

        def rope(x):
            return x * C + _swap_halves(x, lo) * S

        kr = jnp.where(first, rope(kr_ref[...]), 0.0).astype(BF16)
        n_nope = N_HEADS * NOPE
        for j in range(N_HEADS // 2):
            r = rope(q[:, n_nope + j * 128:n_nope + (j + 1) * 128])
            pair = (jnp.where(first, r, 0.0), jnp.where(first, pltpu.roll(r, 64, 1), 0.0))
            for hh in range(2):
                h = 2 * j + hh
                qc_ref[h, :, 0:NOPE] = q[:, h * NOPE:(h + 1) * NOPE].astype(BF16)
                qc_ref[h, :, NOPE:QC] = pair[hh].astype(BF16)
        for h in range(N_HEADS):
            kc_ref[h, :, 0:NOPE] = kv_ref[:, h * 256:h * 256 + NOPE]
            kc_ref[h, :, NOPE:QC] = kr

    out = jax.ShapeDtypeStruct((N_HEADS, T, QC), BF16)
    hblock = pl.BlockSpec((N_HEADS, tq, QC), lambda i: (0, i, 0))
    row = lambda w: pl.BlockSpec((tq, w), lambda i: (i, 0))
    full = lambda a: pl.BlockSpec(a.shape, lambda i: (0, 0))
    return pl.pallas_call(
        body, name=name, out_shape=(jax.ShapeDtypeStruct((T, 2 * D_MLA), BF16), out, out), grid=(T // tq,),
        in_specs=[row(Q_LORA), row(KV_LORA), full(w_uq), full(w_ukv), _pblock(tq, 128, O_KROPE), row(128), row(128)]
        + ([pl.BlockSpec((8, 128), lambda i: (0, 0))] if after is not None else []),
        out_specs=(row(2 * D_MLA), hblock, hblock),
        compiler_params=_params(("parallel",)),
    )(qn, kvn, w_uq, w_ukv, proj, cos_t, sin_t, *([after] if after is not None else []))


def _rope_bwd(dqc, dkr, cos_t, sin_t, name, tq=256):
    T = dqc.shape[1]

    def body(dqc_ref, dkr_ref, c_ref, s_ref, dq_ref, dk_ref):
        C, S = c_ref[...], s_ref[...]
        lane = lax.broadcasted_iota(jnp.int32, (tq, 128), 1)
        lo = (lane % ROPE) < (ROPE // 2)
        first = lane < ROPE

        def unrope(dy):
            return dy * C - _swap_halves(dy, lo) * S

        acc = dkr_ref[0]
        for h in range(1, N_HEADS):
            acc = acc + dkr_ref[h]
        dk_ref[:, 0:128] = jnp.where(first, unrope(acc), 0.0).astype(BF16)
        dk_ref[:, 128:256] = jnp.zeros((tq, 128), BF16)
        for j in range(N_HEADS // 2):
            d0 = dqc_ref[2 * j, :, NOPE:QC]
            d1 = dqc_ref[2 * j + 1, :, NOPE:QC]
            comb = jnp.where(first, d0, pltpu.roll(d1, 64, 1))
            dq_ref[:, 1024 + j * 128:1024 + (j + 1) * 128] = unrope(comb).astype(BF16)
        for h in range(N_HEADS):
            dq_ref[:, h * NOPE:(h + 1) * NOPE] = dqc_ref[h, :, 0:NOPE].astype(BF16)

    tab = pl.BlockSpec((tq, 128), lambda i: (i, 0))
    return pl.pallas_call(
        body, name=name,
        out_shape=(jax.ShapeDtypeStruct((T, 1536), BF16), jax.ShapeDtypeStruct((T, 256), BF16)),
        grid=(T // tq,),
        in_specs=[pl.BlockSpec((N_HEADS, tq, QC), lambda i: (0, i, 0)),
                  pl.BlockSpec((N_HEADS, tq, 128), lambda i: (0, i, 0)), tab, tab],
        out_specs=(pl.BlockSpec((tq, 1536), lambda i: (i, 0)), pl.BlockSpec((tq, 256), lambda i: (i, 0))),
        compiler_params=_params(("parallel",)),
    )(dqc, dkr, cos_t, sin_t)


def _flash_fwd(qc, kc, kv, proj, mix, name):
    H, T, _ = qc.shape
    tt = ATT_TILE
    nt = T // tt
    sp = tt // ATT_CH
    pairs = [(i, c, int(c == i)) for i in range(nt) for c in range(i + 1)]
    assert len(pairs) % 2 == 0
    table = jnp.asarray(pairs + [pairs[-1]], jnp.int32)

    def body(tab_ref, q_ref, k_ref, v_ref, g_ref, _, o_ref, y_ref, lse_ref, vt_sc, s_sc, acc_sc, m_sc, l_sc, bias_sc):
        def issue(p, slot):
            i, c = tab_ref[p, 0], tab_ref[p, 1]
            s_sc[slot] = lax.dot_general(k_ref[0, pl.ds(pl.multiple_of(c * tt, tt), tt), :],
                                         q_ref[0, pl.ds(pl.multiple_of(i * tt, tt), tt), :], _DIMS["nt"],
                                         preferred_element_type=F32)

        def softmax_pv(p, slot):
            i, c, diag = tab_ref[p, 0], tab_ref[p, 1], tab_ref[p, 2]
            s = s_sc[slot] + bias_sc[diag]
            m = m_sc[i]
            m_new = jnp.maximum(m, jnp.max(s, axis=0, keepdims=True))
            pr = jnp.exp2((s - m_new) * EXP2_SCALE)
            a = jnp.exp2((m - m_new) * EXP2_SCALE)
            l_sc[i] = a * l_sc[i] + jnp.sum(pr, axis=0, keepdims=True)
            acc_sc[i] = a * acc_sc[i] + jnp.dot(vt_sc[c], pr.astype(BF16), preferred_element_type=F32)
            m_sc[i] = m_new

        issue(0, 0)
        m_sc[...] = jnp.full_like(m_sc, -jnp.inf)
        l_sc[...] = jnp.zeros_like(l_sc)
        acc_sc[...] = jnp.zeros_like(acc_sc)
        krow = lax.broadcasted_iota(jnp.int32, (tt, tt), 0)
        qcol = lax.broadcasted_iota(jnp.int32, (tt, tt), 1)
        bias_sc[0] = jnp.zeros((tt, tt), F32)
        bias_sc[1] = jnp.where(krow <= qcol, 0.0, -jnp.inf)
        for c in range(nt):
            vt_sc[c] = v_ref[c * tt:(c + 1) * tt, :].astype(F32).T.astype(BF16)

        def two(u, carry):
            p = 2 * u
            issue(p + 1, 1)
            softmax_pv(p, 0)
            issue(p + 2, 0)
            softmax_pv(p + 1, 1)
            return carry

        lax.fori_loop(0, len(pairs) // 2, two, 0)
        for i in range(nt):
            rows = slice(i * tt, (i + 1) * tt)
            l = l_sc[i]
            o = (acc_sc[i] / l).T
            o_ref[rows, :] = o
            lse = m_sc[i] * ATTN_SCALE + jnp.log(l)
            for r in range(sp):
                lse_ref[0, sp * i + r] = lse[:, r * ATT_CH:(r + 1) * ATT_CH]
            g = g_ref[rows, :]
            y_ref[rows, :] = (o * (g * _sigmoid(g))).astype(BF16)

    head = lambda h, tab: (h, 0, 0)
    col = lambda h, tab: (0, h)
    return pl.pallas_call(
        body, name=name,
        out_shape=(jax.ShapeDtypeStruct((T, D_MLA), F32), jax.ShapeDtypeStruct((T, D_MIX), BF16),
                   jax.ShapeDtypeStruct((H, T // ATT_CH, 1, ATT_CH), F32)),
        grid_spec=pltpu.PrefetchScalarGridSpec(
            num_scalar_prefetch=1, grid=(H,),
            in_specs=[pl.BlockSpec((1, T, QC), head), pl.BlockSpec((1, T, QC), head),
                      pl.BlockSpec((T, V_DIM), lambda h, tab: (0, 2 * h + 1)), pl.BlockSpec((T, V_DIM), col),
                      pl.BlockSpec(memory_space=pl.ANY)],
            out_specs=(pl.BlockSpec((T, V_DIM), col), pl.BlockSpec((T, V_DIM), col),
                       pl.BlockSpec((1, T // ATT_CH, 1, ATT_CH), lambda h, tab: (h, 0, 0, 0))),
            scratch_shapes=[pltpu.VMEM((nt, V_DIM, tt), BF16), pltpu.VMEM((2, tt, tt), F32),
                            pltpu.VMEM((nt, V_DIM, tt), F32), pltpu.VMEM((nt, 1, tt), F32),
                            pltpu.VMEM((nt, 1, tt), F32), pltpu.VMEM((2, tt, tt), F32)]),
        input_output_aliases={5: 1},
        compiler_params=_params(("arbitrary",)),
    )(table, qc, kc, kv, proj, mix)


def _flash_bwd(qc, kc, kv, do, lse, delta, name):
    H, T, _ = qc.shape
    tt = ATT_TILE
    nt = T // tt
    sp = tt // ATT_CH
    pairs = [(j, c) for j in range(nt) for c in range(j, nt)]
    assert len(pairs) % 2 == 0
    table = jnp.asarray(pairs + [pairs[-1]], jnp.int32)

    def body(tab_ref, q_ref, k_ref, v_ref, do_ref, lse_ref, dl_ref, dq_ref, dkv_ref, dkr_ref, dqt_sc, dk_sc, dv_sc,
             s_sc, dp_sc, kt_sc, bias_sc):
        def operands(j, c):
            k0, q0 = pl.multiple_of(j * tt, tt), pl.multiple_of(c * tt, tt)
            return (k_ref[0, pl.ds(k0, tt), :], v_ref[pl.ds(k0, tt), :], q_ref[0, pl.ds(q0, tt), :],
                    do_ref[pl.ds(q0, tt), :])

        def stat_row(ref, c):
            return jnp.concatenate([ref[0, sp * c + r] for r in range(sp)], axis=1)

        def early(p, slot):
            k, v, q, dov = operands(tab_ref[p, 0], tab_ref[p, 1])
            s_sc[slot] = lax.dot_general(k, q, _DIMS["nt"], preferred_element_type=F32)
            dp_sc[slot] = lax.dot_general(v, dov, _DIMS["nt"], preferred_element_type=F32)

        def late(p, slot):
            j, c = tab_ref[p, 0], tab_ref[p, 1]
            _, _, q, dov = operands(j, c)
            s = s_sc[slot] + jnp.where(j == c, bias_sc[...], 0.0)
            pr = jnp.exp2(s * EXP2_SCALE - stat_row(lse_ref, c) * LOG2E)
            ds = (pr * (dp_sc[slot] - stat_row(dl_ref, c)) * ATTN_SCALE).astype(BF16)
            dv_sc[j] += jnp.dot(pr.astype(BF16), dov, preferred_element_type=F32)
            dk_sc[j] += jnp.dot(ds, q, preferred_element_type=F32)
            dqt_sc[c] += jnp.dot(kt_sc[j], ds, preferred_element_type=F32)

        early(0, 0)
        dqt_sc[...] = jnp.zeros_like(dqt_sc)
        dk_sc[...] = jnp.zeros_like(dk_sc)
        dv_sc[...] = jnp.zeros_like(dv_sc)
        krow = lax.broadcasted_iota(jnp.int32, (tt, tt), 0)
        qcol = lax.broadcasted_iota(jnp.int32, (tt, tt), 1)
        bias_sc[...] = jnp.where(krow <= qcol, 0.0, -jnp.inf)
        for j in range(nt):
            kt_sc[j] = k_ref[0, j * tt:(j + 1) * tt, :].astype(F32).T.astype(BF16)

        def two(u, carry):
            p = 2 * u
            early(p + 1, 1)
            late(p, 0)
            early(p + 2, 0)
            late(p + 1, 1)
            return carry

        lax.fori_loop(0, len(pairs) // 2, two, 0)
        for j in range(nt):
            rows = slice(j * tt, (j + 1) * tt)
            dk = dk_sc[j]
            dkv_ref[rows, 0:NOPE] = dk[:, 0:NOPE].astype(BF16)
            dkv_ref[rows, NOPE:] = dv_sc[j].astype(BF16)
            dkr_ref[0, rows, :] = dk[:, NOPE:]
            dq_ref[0, rows, :] = dqt_sc[j].T

    head = lambda h, tab: (h, 0, 0)
    stat = pl.BlockSpec((1, T // ATT_CH, 1, ATT_CH), lambda h, tab: (h, 0, 0, 0))
    return pl.pallas_call(
        body, name=name,
        out_shape=(jax.ShapeDtypeStruct((H, T, QC), F32), jax.ShapeDtypeStruct((T, 2 * D_MLA), BF16),
                   jax.ShapeDtypeStruct((H, T, 128), F32)),
        grid_spec=pltpu.PrefetchScalarGridSpec(
            num_scalar_prefetch=1, grid=(H,),
            in_specs=[pl.BlockSpec((1, T, QC), head), pl.BlockSpec((1, T, QC), head),
                      pl.BlockSpec((T, V_DIM), lambda h, tab: (0, 2 * h + 1)),
                      pl.BlockSpec((T, V_DIM), lambda h, tab: (0, h)), stat, stat],
            out_specs=(pl.BlockSpec((1, T, QC), head), pl.BlockSpec((T, 256), lambda h, tab: (0, h)),
                       pl.BlockSpec((1, T, 128), head)),
            scratch_shapes=[pltpu.VMEM((nt, QC, tt), F32), pltpu.VMEM((nt, tt, QC), F32),
                            pltpu.VMEM((nt, tt, V_DIM), F32), pltpu.VMEM((2, tt, tt), F32),
                            pltpu.VMEM((2, tt, tt), F32), pltpu.VMEM((nt, QC, tt), BF16), pltpu.VMEM((tt, tt), F32)]),
        compiler_params=pltpu.CompilerParams(dimension_semantics=("arbitrary",),
                                             vmem_limit_bytes=ATT_BWD_VMEM_LIMIT),
    )(table, qc, kc, kv, do, lse, delta)


def _adamw(lands, w, m, v, name, rows, cols=None, first_layer=0, into=None):
    layers, R, C = w.shape
    L = len(lands)
    cols = C if cols is None else cols
    assert R % rows == 0 and C % cols == 0 and first_layer + L <= layers
    nc = C // cols
    nb = (R // rows) * nc
    c1 = 1.0 - ADAM_B1 ** ADAM_STEP
    c2 = 1.0 - ADAM_B2 ** ADAM_STEP

    def body(*refs):
        land_refs = refs[:L]
        w_ref, m_ref, v_ref = refs[L:L + 3]
        g_ref, d_ref, nm_ref, nv_ref, g_sc = refs[-5:]
        for ll in range(L):
            @pl.when(pl.program_id(0) == ll)
            def _(land_ref=land_refs[ll]):
                g = land_ref[0].astype(F32)
                for s in range(1, N_DEV):
                    g = g + land_ref[s].astype(F32)
                g_sc[...] = g

        g = g_sc[...]
        nm = ADAM_B1 * m_ref[0] + (1.0 - ADAM_B1) * g
        nv = ADAM_B2 * v_ref[0] + (1.0 - ADAM_B2) * (g * g)
        g_ref[0] = g
        nm_ref[0] = nm
        nv_ref[0] = nv
        d_ref[0] = -ADAM_LR * ((nm / c1) / (jnp.sqrt(nv / c2) + ADAM_EPS) + ADAM_WD * w_ref[0])

    def land_spec(ll):
        def index(l, i):
            i = jnp.where(l < ll, 0, jnp.where(l > ll, nb - 1, i))
            return (0, i // nc, i % nc)
        return pl.BlockSpec((N_DEV, rows, cols), index)

    blk = pl.BlockSpec((1, rows, cols), lambda l, i: (first_layer + l, i // nc, i % nc))
    out = jax.ShapeDtypeStruct((layers, R, C), F32)
    extra = [] if into is None else list(into)
    return pl.pallas_call(
        body, name=name, out_shape=(out, out, out, out), grid=(L, nb),
        in_specs=[land_spec(ll) for ll in range(L)] + [blk, blk, blk] + [pl.BlockSpec(memory_space=pl.ANY)] * len(extra),
        out_specs=(blk, blk, blk, blk),
        input_output_aliases={L + 3 + i: i for i in range(len(extra))},
        scratch_shapes=[pltpu.VMEM((rows, cols), F32)],
        compiler_params=_params(("arbitrary", "arbitrary")),
    )(*lands, w, m, v, *extra)


def _mesh_pos():
    return lax.axis_index("x"), lax.axis_index("y"), lax.axis_index("c")


def _all_gather(arrays, name):
    n = len(arrays)

    def body(*refs):
        ins, outs = refs[:n], refs[n:2 * n]
        send_sems, recv_sems, local_sems = refs[2 * n:]
        x, y, c = _mesh_pos()
        me, sibling = (x, y, c), (x, y, 1 - c)
        chips = [(1 - x, y), (x, 1 - y), (1 - x, 1 - y)]

        def slot(a, pos):
            px, py, pc = pos
            return outs[a].at[4 * px + 2 * py + pc]

        def copy(a, k, block, to, src=None):
            return pltpu.make_async_remote_copy(
                src_ref=slot(a, block) if src is None else src, dst_ref=slot(a, block),
                send_sem=send_sems.at[a * 7 + k], recv_sem=recv_sems.at[a * 7 + k],
                device_id=to, device_id_type=MESH_ID)

        mine, first, passed = [], [], []
        for a in range(n):
            cp = pltpu.make_async_copy(ins[a], slot(a, me), local_sems.at[a])
            cp.start()
            mine.append(cp)
            cps = [copy(a, 0, me, sibling, src=ins[a])]
            cps += [copy(a, 1 + j, me, (*chip, c), src=ins[a]) for j, chip in enumerate(chips)]
            for cp in cps:
                cp.start()
            first += cps
        for j, chip in enumerate(chips):
            for a in range(n):
                copy(a, 1 + j, (*chip, c), me).wait_recv()
                cp = copy(a, 4 + j, (*chip, c), sibling)
                cp.start()
                passed.append(cp)
        for a in range(n):
            copy(a, 0, sibling, me).wait_recv()
            for j, chip in enumerate(chips):
                copy(a, 4 + j, (*chip, 1 - c), me).wait_recv()
        for cp in first + passed:
            cp.wait_send()
        for cp in mine:
            cp.wait()

    hbm = pl.BlockSpec(memory_space=pltpu.HBM)
    return pl.pallas_call(
        body, name=name,
        out_shape=tuple(jax.ShapeDtypeStruct((N_DEV,) + a.shape, a.dtype) for a in arrays),
        in_specs=[hbm] * n, out_specs=tuple([hbm] * n),
        scratch_shapes=[pltpu.SemaphoreType.DMA((7 * n,)), pltpu.SemaphoreType.DMA((7 * n,)),
                        pltpu.SemaphoreType.DMA((n,))],
    )(*arrays)


def _all_gather_under_ln(arrays, x, g, b, name, tq=512):
    n = len(arrays)
    T, D = x.shape
    nt = T // tq

    def body(*refs):
        x_ref, g_ref, b_ref = refs[:3]
        ins = refs[3:3 + n]
        y_ref, yb_ref = refs[3 + n:5 + n]
        outs = refs[5 + n:5 + 2 * n]
        send_sems, recv_sems, local_sems = refs[5 + 2 * n:]
        i = pl.program_id(0)
        mx, my, mc = _mesh_pos()
        me, sibling = (mx, my, mc), (mx, my, 1 - mc)
        chips = [(1 - mx, my), (mx, 1 - my), (1 - mx, 1 - my)]

        def slot(a, pos):
            px, py, pc = pos
            return outs[a].at[4 * px + 2 * py + pc]

        def copy(a, k, block, to, src=None):
            return pltpu.make_async_remote_copy(
                src_ref=slot(a, block) if src is None else src, dst_ref=slot(a, block),
                send_sem=send_sems.at[a * 7 + k], recv_sem=recv_sems.at[a * 7 + k],
                device_id=to, device_id_type=MESH_ID)

        def own(a):
            return pltpu.make_async_copy(ins[a], slot(a, me), local_sems.at[a])

        def first(a):
            return [copy(a, 0, me, sibling, src=ins[a])] + [
                copy(a, 1 + j, me, (*chip, mc), src=ins[a]) for j, chip in enumerate(chips)]

        @pl.when(i == 0)
        def _():
            for a in range(n):
                own(a).start()
                for cp in first(a):
                    cp.start()

        zv = x_ref[...]
        mu = jnp.mean(zv, axis=1, keepdims=True)
        zc = zv - mu
        var = jnp.mean(zc * zc, axis=1, keepdims=True)
        y = zc * lax.rsqrt(var + LN_EPS) * g_ref[...] + b_ref[...]
        y_ref[...] = y
        yb_ref[...] = y.astype(BF16)

        @pl.when(i == nt - 1)
        def _():
            passed = []
            for j, chip in enumerate(chips):
                for a in range(n):
                    copy(a, 1 + j, (*chip, mc), me).wait_recv()
                    cp = copy(a, 4 + j, (*chip, mc), sibling)
                    cp.start()
                    passed.append(cp)
            for a in range(n):
                copy(a, 0, sibling, me).wait_recv()
                for j, chip in enumerate(chips):
                    copy(a, 4 + j, (*chip, 1 - mc), me).wait_recv()
            for a in range(n):
                for cp in first(a):
                    cp.wait_send()
                own(a).wait()
            for cp in passed:
                cp.wait_send()

    row = pl.BlockSpec((tq, D), lambda i: (i, 0))
    vec = pl.BlockSpec((1, D), lambda i: (0, 0))
    hbm = pl.BlockSpec(memory_space=pltpu.HBM)
    outs = pl.pallas_call(
        body, name=name,
        out_shape=(jax.ShapeDtypeStruct((T, D), F32), jax.ShapeDtypeStruct((T, D), BF16))
        + tuple(jax.ShapeDtypeStruct((N_DEV,) + a.shape, a.dtype) for a in arrays),
        grid=(nt,), in_specs=[row, vec, vec] + [hbm] * n, out_specs=tuple([row, row] + [hbm] * n),
        scratch_shapes=[pltpu.SemaphoreType.DMA((7 * n,)), pltpu.SemaphoreType.DMA((7 * n,)),
                        pltpu.SemaphoreType.DMA((n,))],
        compiler_params=_params(("arbitrary",)),
    )(x, g, b, *arrays)
    return outs[0], outs[1], outs[2:]


_HBM = pl.BlockSpec(memory_space=pltpu.HBM)
_SEM = pl.BlockSpec(memory_space=pltpu.SEMAPHORE)
_EFFECT = pltpu.SideEffectType.DATAFLOW_SIDE_EFFECTING
N_PEERS = N_DEV - 1


def _peer(k):
    x, y, c = _mesh_pos()
    return (1 - x if k & 4 else x, 1 - y if k & 2 else y, 1 - c if k & 1 else c)


def _split_start(srcs, scatter, after, name):
    n = len(srcs)
    zones = [jax.ShapeDtypeStruct(s.shape if scatter else ((N_DEV,) + s.shape), s.dtype) for s in srcs]

    def body(*refs):
        src, zone = refs[:n], refs[n:2 * n]
        outs = refs[2 * n + 1:]
        send, recv, token = outs[:n], outs[n:2 * n], outs[4 * n]
        x, y, c = _mesh_pos()
        my_idx = 4 * x + 2 * y + c
        for a in range(n):
            pltpu.make_async_copy(src[a].at[my_idx] if scatter else src[a],
                                  zone[a].at[N_PEERS] if scatter else zone[a].at[my_idx], recv[a]).start()
            for k in range(1, N_DEV):
                px, py, pc = _peer(k)
                pltpu.make_async_remote_copy(
                    src_ref=src[a].at[4 * px + 2 * py + pc] if scatter else src[a],
                    dst_ref=zone[a].at[k - 1] if scatter else zone[a].at[my_idx],
                    send_sem=send[a], recv_sem=recv[a], device_id=(px, py, pc), device_id_type=MESH_ID).start()
        token[...] = jnp.zeros_like(token)

    hbm = lambda a: pltpu.with_memory_space_constraint(a, pltpu.HBM)
    outs = pl.pallas_call(
        body, name=name,
        out_shape=tuple([pltpu.SemaphoreType.DMA(())] * (2 * n)
                        + [pltpu.HBM(s.shape, s.dtype) for s in srcs]
                        + [pltpu.HBM(z.shape, z.dtype) for z in zones]
                        + [jax.ShapeDtypeStruct((8, 128), F32)]),
        in_specs=[_HBM] * (2 * n) + [pl.BlockSpec(memory_space=pl.ANY)],
        out_specs=tuple([_SEM] * (2 * n) + [_HBM] * (2 * n) + [pl.BlockSpec(memory_space=pltpu.VMEM)]),
        input_output_aliases={**{a: 2 * n + a for a in range(n)}, **{n + a: 3 * n + a for a in range(n)}},
        compiler_params=pltpu.CompilerParams(has_side_effects=_EFFECT),
    )(*[hbm(s) for s in srcs], *[hbm(lax.empty(z.shape, z.dtype)) for z in zones], after)
    return outs[:n], outs[n:2 * n], outs[2 * n:3 * n], outs[3 * n:4 * n], outs[4 * n]


def _split_wait(send, recv, srcs, zones, after, name):
    n = len(srcs)

    def body(*refs):
        zone = refs[n:2 * n]
        send_sems, recv_sems = refs[2 * n:3 * n], refs[3 * n:4 * n]
        x, y, c = _mesh_pos()
        for a in range(n):
            seven = zone[a].at[pl.ds(0, N_PEERS)]
            pltpu.make_async_remote_copy(src_ref=seven, dst_ref=seven, send_sem=send_sems[a], recv_sem=recv_sems[a],
                                         device_id=(x, y, 1 - c), device_id_type=MESH_ID).wait_send()
            pltpu.make_async_remote_copy(src_ref=zone[a], dst_ref=zone[a], send_sem=send_sems[a],
                                         recv_sem=recv_sems[a], device_id=(x, y, 1 - c),
                                         device_id_type=MESH_ID).wait_recv()

    outs = pl.pallas_call(
        body, name=name,
        out_shape=tuple([pltpu.HBM(s.shape, s.dtype) for s in srcs] + [pltpu.HBM(z.shape, z.dtype) for z in zones]),
        in_specs=[_HBM] * (2 * n) + [_SEM] * (2 * n) + [pl.BlockSpec(memory_space=pl.ANY)],
        out_specs=tuple([_HBM] * (2 * n)),
        input_output_aliases={a: a for a in range(2 * n)},
        compiler_params=pltpu.CompilerParams(has_side_effects=_EFFECT),
    )(*srcs, *zones, *send, *recv, after)
    return outs[:n], outs[n:]


def _cat_blocks(g, axis):
    return jnp.concatenate([g[d] for d in range(N_DEV)], axis=axis)


N_LATENT = Q_LORA + KV_LORA + ROPE
W_SHARD = D_IN_PROJ // N_DEV


def _ref_cols(lo, hi):
    out = []
    if lo < N_LATENT:
        out.append((N_GATED + lo, N_GATED + min(hi, N_LATENT)))
    if hi > N_LATENT:
        out.append((max(lo, N_LATENT) - N_LATENT, hi - N_LATENT))
    return out


def _permute_w_in_t(blocks):
    pieces = []
    for lo, hi in ((N_LATENT, D_IN_PROJ), (0, N_LATENT)):
        for d in range(N_DEV):
            a, b = max(lo, d * W_SHARD), min(hi, (d + 1) * W_SHARD)
            if a < b:
                pieces.append(blocks[d][a - d * W_SHARD:b - d * W_SHARD])
    pieces.append(jnp.zeros((NPP - D_IN_PROJ, blocks.shape[2]), blocks.dtype))
    return jnp.concatenate(pieces, axis=0)


def _split_w_in_t(w):
    slabs = []
    for d in range(N_DEV):
        parts = [w[a:b] for a, b in _ref_cols(d * W_SHARD, (d + 1) * W_SHARD)]
        slabs.append(parts[0] if len(parts) == 1 else jnp.concatenate(parts, axis=0))
    return jnp.stack(slabs)


def _permute_w_uq(w):
    w3 = w.reshape(w.shape[0], N_HEADS, NOPE + ROPE)
    return jnp.concatenate([w3[:, :, :NOPE].reshape(w.shape[0], -1), w3[:, :, NOPE:].reshape(w.shape[0], -1)], axis=1)


def _unpermute_w_uq(w):
    nope = w[:, :N_HEADS * NOPE].reshape(w.shape[0], N_HEADS, NOPE)
    rope = w[:, N_HEADS * NOPE:].reshape(w.shape[0], N_HEADS, ROPE)
    return jnp.concatenate([nope, rope], axis=2).reshape(w.shape[0], -1)


_SMALL_EMB = (("emb_ln_g", 16), ("emb_ln_b", 16))
_SMALL_LAYER = (("q_norm_g", 8), ("kv_norm_g", 8), ("w_pool", 1024), ("pool_scale", 8), ("b_out", 32),
                ("ln_g", 32), ("ln_b", 32))
_SMALL = _SMALL_EMB + _SMALL_LAYER
CONV_ROWS = DEPTH * CONV_WIDTH * D_CONV // 128


def _pack_small(d, entries=_SMALL):
    parts = []
    for name, rows in entries:
        flat = d[name].reshape(-1)
        flat = jnp.pad(flat, (0, rows * 128 - flat.shape[0]))
        parts.append(flat.reshape(rows, 128))
    return jnp.concatenate(parts, axis=0)


def _unpack_small(packed, shapes):
    out, r0 = {}, 0
    for name, rows in _SMALL:
        size = 1
        for s in shapes[name]:
            size *= s
        out[name] = packed[r0:r0 + rows].reshape(-1)[:size].reshape(shapes[name])
        r0 += rows
    return out


def _rope_tables(positions):
    half = ROPE // 2
    inv_freq = ROPE_THETA ** (-jnp.arange(half, dtype=F32) / half)
    ang = positions.astype(F32)[:, None] * inv_freq
    cos, sin = jnp.cos(ang), jnp.sin(ang)
    return jnp.concatenate([cos, cos, cos, cos], axis=1), jnp.concatenate([-sin, sin, -sin, sin], axis=1)


def _local_step(x, positions, target, emb_g, emb_b, layer_weights, layer_weights_rest, on_sharded_grads,
                on_layer_grads=None, first_after=None, embedded=None):
    cos_t, sin_t = _rope_tables(positions)
    h, hb = _ln_fwd(x, emb_g, emb_b, "emb_ln_fwd") if embedded is None else embedded
    saved = []
    for l in range(DEPTH):
        W = layer_weights(l, h)
        proj = _mm(hb, W["w_in_t"], "nt", F32, "proj_fwd", after=first_after if l == 0 else None)
        qn, kvn, pooled, cv, mix = _mix_fwd(proj, W["q_norm_g"], W["kv_norm_g"], W["w_pool"], W["pool_scale"],
                                            W["conv_w"], "mix_fwd")
        rest, token = layer_weights_rest(l, proj)
        W = {**W, **rest}
        kv, qc, kc = _up_rope_fwd(qn, kvn, W["w_uq"], W["w_ukv"], proj, cos_t, sin_t, "up_rope_fwd", after=token)
        o, mix, lse = _flash_fwd(qc, kc, kv, proj, mix, "flash_fwd")
        z = _mm(mix, W["w_out"], "nn", F32, "out_fwd", res=h, bias=W["b_out"], alpha=ALPHA)
        saved.append((W, hb, proj, qn, kvn, pooled, cv, kv, qc, kc, o, lse, mix, z))
        last = l == DEPTH - 1
        h, hb = _ln_fwd(z, W["ln_g"], W["ln_b"], "ln_fwd_out" if last else "ln_fwd", for_matmul=not last)
    dh, sq = h, None

    grads = {k: [None] * DEPTH for k in ("q_norm_g", "kv_norm_g", "w_pool", "pool_scale", "conv_w", "b_out", "ln_g",
                                         "ln_b")}
    for l in reversed(range(DEPTH)):
        W, hb_in, proj, qn, kvn, pooled, cv, kv, qc, kc, o, lse, mix, z = saved[l]
        sharded = {}
        if l == DEPTH - 1:
            dz, dzb, grads["b_out"][l], grads["ln_g"][l], grads["ln_b"][l], sq = _ln_bwd(
                dh, z, W["ln_g"], "ln_bwd_loss", target=target)
        else:
            dz, dzb, grads["b_out"][l], grads["ln_g"][l], grads["ln_b"][l] = _ln_bwd(dh, z, W["ln_g"], "ln_bwd")
        dmix = _mm(dzb, W["w_out"], "nt", F32, "out_bwd_x")
        sharded["w_out"] = _mm(mix, dzb, "tn", GRAD_XFER, "out_bwd_w", tk=4096)
        do, delta, dproj, grads["w_pool"][l], grads["pool_scale"][l], grads["conv_w"][l] = _mix_bwd(
            dmix, proj, o, pooled, cv, W["w_pool"], W["pool_scale"], W["conv_w"], "mix_bwd")
        dqc, dkv, dkr = _flash_bwd(qc, kc, kv, do, lse, delta, "flash_bwd")
        dq, dkrope = _rope_bwd(dqc, dkr, cos_t, sin_t, "rope_bwd")
        sharded["w_uq"] = _mm(qn, dq, "tn", GRAD_XFER, "q_up_bwd_w")
        sharded["w_ukv"] = _mm(kvn, dkv, "tn", GRAD_XFER, "kv_up_bwd_w")
        token = on_sharded_grads(l, sharded)
        dproj, grads["q_norm_g"][l], grads["kv_norm_g"][l] = _up_rms_bwd(
            proj, dq, dkv, W["w_uq"], W["w_ukv"], dkrope, dproj, W["q_norm_g"], W["kv_norm_g"], "up_rms_bwd")
        if l == 0 and on_layer_grads is not None:
            token = on_layer_grads(grads, token)
        d_w_in_t = _mm(dproj, hb_in, "tn", GRAD_XFER, "proj_bwd_w", tk=4096, after=token)
        token = on_sharded_grads(l, {"w_in": d_w_in_t})
        dh = _mm(dproj, W["w_in_t"], "nn", F32, "proj_bwd_x", res=dz, alpha=ALPHA, tn=512, tk=NPP, after=token)
    grad_x, grads["emb_ln_g"], grads["emb_ln_b"] = _ln_bwd(dh, x, emb_g, "emb_ln_bwd", for_matmul=False)
    return sq, grad_x, grads


def kernel(x, positions, emb_ln_g, emb_ln_b, w_in, q_norm_g, kv_norm_g, w_uq, w_ukv, w_pool, pool_scale, conv_w, w_out, b_out, ln_g, ln_b, loss_target, m_emb_ln_g, m_emb_ln_b, m_w_in, m_q_norm_g, m_kv_norm_g, m_w_uq, m_w_ukv, m_w_pool, m_pool_scale, m_conv_w, m_w_out, m_b_out, m_ln_g, m_ln_b, v_emb_ln_g, v_emb_ln_b, v_w_in, v_q_norm_g, v_kv_norm_g, v_w_uq, v_w_ukv, v_w_pool, v_pool_scale, v_conv_w, v_w_out, v_b_out, v_ln_g, v_ln_b):
    weights = dict(emb_ln_g=emb_ln_g, emb_ln_b=emb_ln_b, w_in=w_in, q_norm_g=q_norm_g, kv_norm_g=kv_norm_g,
                   w_uq=w_uq, w_ukv=w_ukv, w_pool=w_pool, pool_scale=pool_scale, conv_w=conv_w, w_out=w_out,
                   b_out=b_out, ln_g=ln_g, ln_b=ln_b)
    mom1 = dict(emb_ln_g=m_emb_ln_g, emb_ln_b=m_emb_ln_b, w_in=m_w_in, q_norm_g=m_q_norm_g, kv_norm_g=m_kv_norm_g,
                w_uq=m_w_uq, w_ukv=m_w_ukv, w_pool=m_w_pool, pool_scale=m_pool_scale, conv_w=m_conv_w,
                w_out=m_w_out, b_out=m_b_out, ln_g=m_ln_g, ln_b=m_ln_b)
    mom2 = dict(emb_ln_g=v_emb_ln_g, emb_ln_b=v_emb_ln_b, w_in=v_w_in, q_norm_g=v_q_norm_g, kv_norm_g=v_kv_norm_g,
                w_uq=v_w_uq, w_ukv=v_w_ukv, w_pool=v_w_pool, pool_scale=v_pool_scale, conv_w=v_conv_w,
                w_out=v_w_out, b_out=v_b_out, ln_g=v_ln_g, ln_b=v_ln_b)

    big = ("w_in", "w_uq", "w_ukv", "w_out")

    conv_pad = jnp.zeros((8, 128), F32).at[0:DEPTH * CONV_WIDTH, 0:64].set(conv_w.reshape(DEPTH * CONV_WIDTH, 64))
    t12 = lambda a: jnp.swapaxes(a, 1, 2)
    shard = lambda k, l: (t12(weights[k])[l] if k == "w_in" else weights[k][l]).astype(BF16)
    h0, h0b, (w_in0, conv_all) = _all_gather_under_ln(
        [shard("w_in", 0), conv_pad], x[0], emb_ln_g.reshape(1, -1), emb_ln_b.reshape(1, -1), "w_in0_all_gather_emb_ln")
    rest0 = _split_start([shard(k, 0) for k in big[1:]], False, w_in0, "weights0_rest_start")
    conv_full = _cat_blocks(conv_all[:, 0:DEPTH * CONV_WIDTH, 0:64], 1).reshape(DEPTH, CONV_WIDTH, D_CONV)
    conv_full = jnp.pad(conv_full, ((0, 0), (0, 8 - CONV_WIDTH), (0, 0)))
    fetched = {}

    def layer_weights(l, ready):
        if l == 0:
            w_in_blocks = w_in0
        else:
            fetched[1] = _split_wait(*fetched["w1"][:4], ready, "weights1_wait")[1]
            w_in_blocks = fetched[1][0]
        return dict(
            w_in_t=_permute_w_in_t(w_in_blocks), conv_w=conv_full[l],
            q_norm_g=q_norm_g[l].reshape(1, -1), kv_norm_g=kv_norm_g[l].reshape(1, -1),
            w_pool=w_pool[l].astype(BF16), pool_scale=pool_scale[l].reshape(1, -1), b_out=b_out[l].reshape(1, -1),
            ln_g=ln_g[l].reshape(1, -1), ln_b=ln_b[l].reshape(1, -1))

    def layer_weights_rest(l, ready):
        token = None
        if l == 0:
            blocks = _split_wait(*rest0[:4], ready, "weights0_rest_wait")[1]
            fetched["w1"] = _split_start([shard(k, 1) for k in big], False, blocks[0], "weights1_start")
            token = fetched["w1"][4]
        else:
            blocks = fetched[1][1:]
        return dict(w_uq=_permute_w_uq(_cat_blocks(blocks[0], 1)), w_ukv=_cat_blocks(blocks[1], 1),
                    w_out=blocks[2].reshape(D_MIX, D_MODEL)), token

    by_dest = dict(
        w_in=_split_w_in_t,
        w_uq=lambda g: _unpermute_w_uq(g).reshape(Q_LORA, N_DEV, -1).transpose(1, 0, 2),
        w_ukv=lambda g: g.reshape(KV_LORA, N_DEV, -1).transpose(1, 0, 2),
        w_out=lambda g: g.reshape(N_DEV, -1, D_MODEL))
    in_flight = []

    def on_sharded_grads(l, g):
        names = [k for k in big if k in g]
        srcs = [by_dest[k](g[k]) for k in names]
        started = _split_start(srcs, True, srcs[0], "grads%d_%s_start" % (l, names[0]))
        in_flight.append((l, names, started[:4]))
        return started[4]

    small_in_flight = []

    def on_layer_grads(g, token):
        stacked = {k: jnp.stack(g[k]) for k, _ in _SMALL_LAYER}
        conv = jnp.stack([g["conv_w"][l][0:CONV_WIDTH] for l in range(DEPTH)]).reshape(CONV_ROWS, 128)
        packed = jnp.concatenate([_pack_small(stacked, _SMALL_LAYER), conv], axis=0)
        started = _split_start([packed], False, token, "layer_grads_start")
        small_in_flight.append(started[:4])
        return started[4]

    sq, grad_x, G = _local_step(x[0], positions[0], loss_target[0], emb_ln_g.reshape(1, -1),
                                emb_ln_b.reshape(1, -1), layer_weights, layer_weights_rest, on_sharded_grads,
                                on_layer_grads, first_after=rest0[4], embedded=(h0, h0b))

    res = {}
    landed = {}
    for l, names, started in in_flight:
        zones = _split_wait(*started, grad_x, "grads%d_%s_wait" % (l, names[0]))[1]
        for k, zone in zip(names, zones):
            landed[k, l] = zone
    w_in_res = None
    for l in reversed(range(DEPTH)):
        w_in_res = _adamw([landed["w_in", l]], t12(w_in), t12(m_w_in), t12(v_w_in), "adamw_w_in_%d" % l, W_SHARD, 512,
                          first_layer=l, into=w_in_res)
    res["w_in"] = tuple(t12(o) for o in w_in_res)
    for name, rows in (("w_uq", 256), ("w_ukv", 256), ("w_out", 128)):
        res[name] = _adamw([landed[name, l] for l in range(DEPTH)], weights[name], mom1[name], mom2[name],
                           "adamw_" + name, rows)

    layer_zone = _split_wait(*small_in_flight[0], grad_x, "layer_grads_wait")[1][0]
    n_emb_rows = sum(r for _, r in _SMALL_EMB)
    emb_zone = _all_gather([jnp.concatenate([_pack_small(G, _SMALL_EMB), sq], axis=0)], "emb_grads_all_gather")[0]
    loss = jnp.sum(emb_zone[:, n_emb_rows, 0]) * (0.5 / D_MODEL)
    n_layer_rows = sum(r for _, r in _SMALL_LAYER)
    l_small = jnp.concatenate([emb_zone[:, 0:n_emb_rows], layer_zone[:, 0:n_layer_rows]], axis=1)
    my_idx = 4 * lax.axis_index("x") + 2 * lax.axis_index("y") + lax.axis_index("c")
    conv_all_grads = layer_zone[:, n_layer_rows:].reshape(N_DEV, DEPTH * CONV_WIDTH, D_CONV)
    l_conv = lax.dynamic_slice_in_dim(conv_all_grads, my_idx * 64, 64, axis=2)
    l_conv = jnp.zeros((N_DEV, 8, 128), F32).at[:, 0:DEPTH * CONV_WIDTH, 0:64].set(l_conv)
    conv_shard = lambda a: jnp.zeros((8, 128), F32).at[0:DEPTH * CONV_WIDTH, 0:64].set(a.reshape(-1, 64))
    conv_res = _adamw([l_conv], conv_shard(conv_w)[None], conv_shard(m_conv_w)[None], conv_shard(v_conv_w)[None],
                      "adamw_conv_w", 8)
    res["conv_w"] = tuple(o[0, 0:DEPTH * CONV_WIDTH, 0:64].reshape(DEPTH, CONV_WIDTH, 64) for o in conv_res)
    small_res = _adamw([l_small], _pack_small(weights)[None], _pack_small(mom1)[None], _pack_small(mom2)[None],
                       "adamw_small", 392)
    shapes = {k: weights[k].shape for k, _ in _SMALL}
    unpacked = [_unpack_small(o[0], shapes) for o in small_res]
    for k, _ in _SMALL:
        res[k] = tuple(u[k] for u in unpacked)

    order = ("emb_ln_g", "emb_ln_b", "w_in", "q_norm_g", "kv_norm_g", "w_uq", "w_ukv", "w_pool", "pool_scale",
             "conv_w", "w_out", "b_out", "ln_g", "ln_b")
    return (loss, grad_x[None], *[res[k][0] for k in order], *[res[k][1] for k in order],
            *[res[k][2] for k in order], *[res[k][3] for k in order])
```

```python
import jax
import jax.numpy as jnp
from jax import lax
from jax.experimental import pallas as pl
from jax.experimental.pallas import tpu as pltpu

F32 = jnp.float32
BF16 = jnp.bfloat16

N_DEV = 8
D_MODEL = 2048
DEPTH = 2
N_HEADS = 8
NOPE = 128
ROPE = 64
V_DIM = 128
Q_LORA = 512
KV_LORA = 256
D_MLA = N_HEADS * V_DIM
D_POOL = 512
D_CONV = 512
POOL_WINDOWS = (2, 4, 8, 16)
POOL_GROUP = 128
CONV_WIDTH = 3
D_MIX = D_MLA + D_POOL + D_CONV
D_IN_PROJ = 4928
ROPE_THETA = 10000.0
LN_EPS = 1e-5
RMS_EPS = 1e-6
ALPHA = (2 * DEPTH) ** 0.25
ATTN_SCALE = (NOPE + ROPE) ** -0.5
ADAM_LR = 0.001
ADAM_B1 = 0.9
ADAM_B2 = 0.999
ADAM_EPS = 1e-08
ADAM_WD = 0.01
ADAM_STEP = 10

O_GMLA, O_PIN, O_GPOOL, O_CH, O_CB, O_CC, O_GCONV, O_QLAT, O_KVLAT, O_KROPE = (
    0, 1024, 1536, 2048, 2560, 3072, 3584, 4096, 4608, 4864)
NPP = 5120
N_GATED = O_QLAT
QC = NOPE + 2 * ROPE
HALO = 16
ATT_TILE = 512
ATT_CH = 256
LOG2E = 1.4426950408889634
EXP2_SCALE = ATTN_SCALE * LOG2E

GRAD_XFER = BF16
VMEM_LIMIT = 48 * 1024 * 1024
ATT_BWD_VMEM_LIMIT = 58 * 1024 * 1024
MESH_ID = pl.DeviceIdType.MESH


def _params(sem=None):
    return pltpu.CompilerParams(dimension_semantics=sem, vmem_limit_bytes=VMEM_LIMIT)


def _sigmoid(x):
    return 1.0 / (1.0 + jnp.exp(-x))


def _tile(dim, target):
    if dim <= target:
        return dim
    t = target - target % 128
    while dim % t:
        t -= 128
    return t


_DIMS = {"nn": (((1,), (0,)), ((), ())), "nt": (((1,), (1,)), ((), ())), "tn": (((0,), (0,)), ((), ()))}


def _mm(a, b, mode, out_dtype, name, res=None, bias=None, alpha=1.0, tm=1024, tn=1024, tk=2048, after=None):
    if mode == "nn":
        (M, K), (K2, N) = a.shape, b.shape
    elif mode == "nt":
        (M, K), (N, K2) = a.shape, b.shape
    else:
        (K, M), (K2, N) = a.shape, b.shape
    assert K == K2
    tm, tn, tk = _tile(M, tm), _tile(N, tn), _tile(K, tk)
    nk = K // tk
    has_res, has_bias = res is not None, bias is not None

    def body(*refs):
        a_ref, b_ref = refs[0], refs[1]
        pos = 2
        res_ref = bias_ref = None
        if has_res:
            res_ref = refs[pos]
            pos += 1
        if has_bias:
            bias_ref = refs[pos]
            pos += 1
        def finish(r, o_ref):
            if has_bias:
                r = r + bias_ref[...]
            if has_res:
                r = alpha * res_ref[...] + r
            o_ref[...] = r.astype(out_dtype)

        part = lax.dot_general(a_ref[...].astype(BF16), b_ref[...].astype(BF16), _DIMS[mode],
                               preferred_element_type=F32)
        if nk == 1:
            finish(part, refs[-1])
            return
        o_ref, acc_ref = refs[-2], refs[-1]
        k = pl.program_id(2)

        @pl.when(k == 0)
        def _():
            acc_ref[...] = part

        @pl.when(jnp.logical_and(k > 0, k < nk - 1))
        def _():
            acc_ref[...] += part

        @pl.when(k == nk - 1)
        def _():
            finish(acc_ref[...] + part, o_ref)

    if mode == "nn":
        in_specs = [pl.BlockSpec((tm, tk), lambda i, j, k: (i, k)), pl.BlockSpec((tk, tn), lambda i, j, k: (k, j))]
    elif mode == "nt":
        in_specs = [pl.BlockSpec((tm, tk), lambda i, j, k: (i, k)), pl.BlockSpec((tn, tk), lambda i, j, k: (j, k))]
    else:
        in_specs = [pl.BlockSpec((tk, tm), lambda i, j, k: (k, i)), pl.BlockSpec((tk, tn), lambda i, j, k: (k, j))]
    args = [a, b]
    if has_res:
        in_specs.append(pl.BlockSpec((tm, tn), lambda i, j, k: (i, j)))
        args.append(res)
    if has_bias:
        in_specs.append(pl.BlockSpec((1, tn), lambda i, j, k: (0, j)))
        args.append(bias)
    if after is not None:
        in_specs.append(pl.BlockSpec((8, 128), lambda i, j, k: (0, 0)))
        args.append(after)
    return pl.pallas_call(
        body, name=name,
        out_shape=jax.ShapeDtypeStruct((M, N), out_dtype),
        grid=(M // tm, N // tn, nk),
        in_specs=in_specs,
        out_specs=pl.BlockSpec((tm, tn), lambda i, j, k: (i, j)),
        scratch_shapes=[pltpu.VMEM((tm, tn), F32)] if nk > 1 else [],
        compiler_params=_params(("parallel", "parallel", "arbitrary")),
    )(*args)


def _ln_fwd(z, g, b, name, tq=512, for_matmul=True):
    T, D = z.shape

    def body(z_ref, g_ref, b_ref, y_ref, *yb_ref):
        zv = z_ref[...]
        mu = jnp.mean(zv, axis=1, keepdims=True)
        zc = zv - mu
        var = jnp.mean(zc * zc, axis=1, keepdims=True)
        y = zc * lax.rsqrt(var + LN_EPS) * g_ref[...] + b_ref[...]
        y_ref[...] = y
        if for_matmul:
            yb_ref[0][...] = y.astype(BF16)

    row = pl.BlockSpec((tq, D), lambda i: (i, 0))
    vec = pl.BlockSpec((1, D), lambda i: (0, 0))
    outs = pl.pallas_call(
        body, name=name,
        out_shape=(jax.ShapeDtypeStruct((T, D), F32),) + ((jax.ShapeDtypeStruct((T, D), BF16),) if for_matmul else ()),
        grid=(T // tq,), in_specs=[row, vec, vec], out_specs=(row,) * (2 if for_matmul else 1),
        compiler_params=_params(("parallel",)),
    )(z, g, b)
    return (outs[0], outs[1]) if for_matmul else (outs[0], None)


def _ln_bwd(dy, z, g, name, tq=512, target=None, for_matmul=True):
    T, D = z.shape
    with_loss = target is not None

    def body(*refs):
        dy_ref, z_ref, g_ref = refs[:3]
        outs = list(refs[4 if with_loss else 3:])
        dz_ref = outs.pop(0)
        dzb_ref, ds_ref = (outs.pop(0), outs.pop(0)) if for_matmul else (None, None)
        dg_ref, db_ref = outs.pop(0), outs.pop(0)
        sq_ref = outs.pop(0) if with_loss else None

        @pl.when(pl.program_id(0) == 0)
        def _():
            for ref in (dg_ref, db_ref, ds_ref, sq_ref):
                if ref is not None:
                    ref[...] = jnp.zeros_like(ref)

        zv, dyv = z_ref[...], dy_ref[...]
        if with_loss:
            err = dyv - refs[3][...]
            sq_ref[...] += jnp.sum(err * err)
            dyv = err * (1.0 / D)
        mu = jnp.mean(zv, axis=1, keepdims=True)
        zc = zv - mu
        var = jnp.mean(zc * zc, axis=1, keepdims=True)
        rstd = lax.rsqrt(var + LN_EPS)
        xh = zc * rstd
        u = dyv * g_ref[...]
        dz = rstd * (u - jnp.mean(u, axis=1, keepdims=True) - xh * jnp.mean(u * xh, axis=1, keepdims=True))
        dz_ref[...] = dz
        dg_ref[...] += jnp.sum(dyv * xh, axis=0, keepdims=True)
        db_ref[...] += jnp.sum(dyv, axis=0, keepdims=True)
        if for_matmul:
            dzb_ref[...] = dz.astype(BF16)
            ds_ref[...] += jnp.sum(dz, axis=0, keepdims=True)

    row = pl.BlockSpec((tq, D), lambda i: (i, 0))
    vec = pl.BlockSpec((1, D), lambda i: (0, 0))
    vshape = jax.ShapeDtypeStruct((1, D), F32)
    out_shape, out_specs = [jax.ShapeDtypeStruct((T, D), F32)], [row]
    if for_matmul:
        out_shape += [jax.ShapeDtypeStruct((T, D), BF16), vshape]
        out_specs += [row, vec]
    out_shape += [vshape, vshape]
    out_specs += [vec, vec]
    if with_loss:
        out_shape.append(jax.ShapeDtypeStruct((8, 128), F32))
        out_specs.append(pl.BlockSpec((8, 128), lambda i: (0, 0)))
    return pl.pallas_call(
        body, name=name, out_shape=tuple(out_shape), grid=(T // tq,),
        in_specs=[row, row, vec] + ([row] if with_loss else []), out_specs=tuple(out_specs),
        compiler_params=_params(("arbitrary",)),
    )(dy, z, g, *([target] if with_loss else []))


def _pblock(tq, width, offset):
    assert offset % width == 0
    blk = offset // width
    return pl.BlockSpec((tq, width), lambda i: (i, blk))


def _mix_fwd(proj, q_g, kv_g, w_pool, pool_scale, conv_w, name, tq=256):
    T = proj.shape[0]

    def body(ql_ref, kvl_ref, pin_ref, gp_ref, ch_ref, cb_ref, cc_ref, gc_ref, qg_ref, kvg_ref, wp_ref, ps_ref,
             cw_ref, qn_ref, kvn_ref, pooled_ref, cv_ref, ypc_ref, extp, extu):
        i = pl.program_id(0)
        for x_ref, g_ref, o_ref in ((ql_ref, qg_ref, qn_ref), (kvl_ref, kvg_ref, kvn_ref)):
            x = x_ref[...]
            r = lax.rsqrt(jnp.mean(x * x, axis=1, keepdims=True) + RMS_EPS)
            o_ref[...] = (x * r * g_ref[...]).astype(BF16)

        @pl.when(i == 0)
        def _():
            extp[0:HALO, :] = jnp.zeros((HALO, D_POOL), F32)
            extu[0:HALO, :] = jnp.zeros((HALO, D_CONV), F32)

        @pl.when(i > 0)
        def _():
            extp[0:HALO, :] = extp[tq:tq + HALO, :]
            extu[0:HALO, :] = extu[tq:tq + HALO, :]

        t1 = (i * tq + lax.broadcasted_iota(jnp.int32, (tq, 1), 0) + 1).astype(F32)
        for g, w in enumerate(POOL_WINDOWS):
            cols = slice(g * POOL_GROUP, (g + 1) * POOL_GROUP)
            pin = pin_ref[:, cols]
            extp[HALO:, cols] = pin
            s = extp[:, cols]
            k = 1
            while k < w:
                s = s + pltpu.roll(s, k, 0)
                k *= 2
            mean = s[HALO:, :] / jnp.minimum(t1, float(w))
            pooled = (mean - pin).astype(BF16)
            pooled_ref[:, cols] = pooled
            r = jnp.dot(pooled, wp_ref[g], preferred_element_type=F32)
            gp = gp_ref[:, cols]
            ypc_ref[:, cols] = (r * ps_ref[:, cols] * (gp * _sigmoid(gp))).astype(BF16)
        for g in range(D_CONV // 128):
            cols = slice(g * 128, (g + 1) * 128)
            u = cc_ref[:, cols] * ch_ref[:, cols]
            extu[HALO:, cols] = u
            eu = extu[:, cols]
            u1 = pltpu.roll(eu, 1, 0)[HALO:, :]
            u2 = pltpu.roll(eu, 2, 0)[HALO:, :]
            cv = cw_ref[0:1, cols] * u2 + cw_ref[1:2, cols] * u1 + cw_ref[2:3, cols] * u
            cv_ref[:, cols] = cv
            gc = gc_ref[:, cols]
            ypc_ref[:, D_POOL + g * 128:D_POOL + (g + 1) * 128] = (
                cb_ref[:, cols] * cv * (gc * _sigmoid(gc))).astype(BF16)

    full = lambda shape: pl.BlockSpec(shape, lambda i: (0,) * len(shape))
    row = lambda w: pl.BlockSpec((tq, w), lambda i: (i, 0))
    return pl.pallas_call(
        body, name=name,
        out_shape=(jax.ShapeDtypeStruct((T, Q_LORA), BF16), jax.ShapeDtypeStruct((T, KV_LORA), BF16),
                   jax.ShapeDtypeStruct((T, D_POOL), BF16), jax.ShapeDtypeStruct((T, D_CONV), F32),
                   jax.ShapeDtypeStruct((T, D_MIX), BF16)),
        grid=(T // tq,),
        in_specs=[_pblock(tq, Q_LORA, O_QLAT), _pblock(tq, KV_LORA, O_KVLAT), _pblock(tq, 512, O_PIN),
                  _pblock(tq, 512, O_GPOOL), _pblock(tq, 512, O_CH), _pblock(tq, 512, O_CB), _pblock(tq, 512, O_CC),
                  _pblock(tq, 512, O_GCONV), full((1, Q_LORA)), full((1, KV_LORA)), full((4, 128, 128)),
                  full((1, D_POOL)), full((8, D_CONV))],
        out_specs=(row(Q_LORA), row(KV_LORA), row(D_POOL), row(D_CONV),
                   pl.BlockSpec((tq, D_POOL + D_CONV), lambda i: (i, D_MLA // (D_POOL + D_CONV)))),
        scratch_shapes=[pltpu.VMEM((tq + HALO, D_POOL), F32), pltpu.VMEM((tq + HALO, D_CONV), F32)],
        compiler_params=_params(("arbitrary",)),
    )(proj, proj, proj, proj, proj, proj, proj, proj, q_g, kv_g, w_pool, pool_scale, conv_w)


def _mix_bwd(dmix, proj, o, pooled, cv, w_pool, pool_scale, conv_w, name, tq=ATT_CH):
    T = proj.shape[0]
    nt = T // tq
    n_ext = tq + HALO

    def body(dym_ref, dyp_ref, dyc_ref, gm_ref, gp_ref, ch_ref, cb_ref, cc_ref, gc_ref, o_ref, pooled_ref, cv_ref,
             wp_ref, ps_ref, cw_ref, do_ref, delta_ref, dg_ref, dwp_ref, dps_ref, dcw_ref, exte, extd):
        i = pl.program_id(0)
        tile = nt - 1 - i

        @pl.when(i == 0)
        def _():
            dwp_ref[...] = jnp.zeros_like(dwp_ref)
            dps_ref[...] = jnp.zeros_like(dps_ref)
            dcw_ref[...] = jnp.zeros_like(dcw_ref)
            exte[tq:, :] = jnp.zeros((HALO, D_POOL), F32)
            extd[tq:, :] = jnp.zeros((HALO, D_CONV), F32)

        @pl.when(i > 0)
        def _():
            exte[tq:, :] = exte[0:HALO, :]
            extd[tq:, :] = extd[0:HALO, :]

        ones = jnp.ones((8, V_DIM), F32)
        for h in range(N_HEADS):
            cols = slice(h * V_DIM, (h + 1) * V_DIM)
            gm = gm_ref[:, cols]
            sig = _sigmoid(gm)
            dym = dym_ref[:, cols]
            ov = o_ref[:, cols]
            do = dym * (gm * sig)
            do_ref[:, cols] = do.astype(BF16)
            rows = lax.dot_general(ones, do * ov, _DIMS["nt"], precision=lax.Precision.HIGHEST,
                                   preferred_element_type=F32)
            delta_ref[h, 0] = rows[0:1, :]
            dg_ref[:, O_GMLA + h * V_DIM:O_GMLA + (h + 1) * V_DIM] = (
                dym * ov * (sig * (1.0 + gm * (1.0 - sig)))).astype(BF16)

        t1 = (tile * tq + lax.broadcasted_iota(jnp.int32, (tq, 1), 0) + 1).astype(F32)
        for g, w in enumerate(POOL_WINDOWS):
            cols = slice(g * POOL_GROUP, (g + 1) * POOL_GROUP)
            pg = pooled_ref[:, cols]
            r = jnp.dot(pg, wp_ref[g], preferred_element_type=F32)
            gp = gp_ref[:, cols]
            sg = _sigmoid(gp)
            sl = gp * sg
            dyg = dyp_ref[:, cols]
            ps = ps_ref[:, cols]
            dg_ref[:, O_GPOOL + g * POOL_GROUP:O_GPOOL + (g + 1) * POOL_GROUP] = (
                dyg * (r * ps) * (sg * (1.0 + gp * (1.0 - sg)))).astype(BF16)
            dps_ref[:, cols] += jnp.sum(dyg * r * sl, axis=0, keepdims=True)
            dr = (dyg * ps * sl).astype(BF16)
            dwp_ref[g] += lax.dot_general(pg, dr, _DIMS["tn"], preferred_element_type=F32)
            dpooled = lax.dot_general(dr, wp_ref[g], _DIMS["nt"], preferred_element_type=F32)
            exte[0:tq, cols] = dpooled / jnp.minimum(t1, float(w))
            s = exte[:, cols]
            k = 1
            while k < w:
                s = s + pltpu.roll(s, n_ext - k, 0)
                k *= 2
            dg_ref[:, O_PIN + g * POOL_GROUP:O_PIN + (g + 1) * POOL_GROUP] = (s[0:tq, :] - dpooled).astype(BF16)

        for g in range(D_CONV // 128):
            cols = slice(g * 128, (g + 1) * 128)
            out = lambda base: slice(base + g * 128, base + (g + 1) * 128)
            gc = gc_ref[:, cols]
            sg = _sigmoid(gc)
            sl = gc * sg
            dyc = dyc_ref[:, cols]
            cb, cc, ch, cvv = cb_ref[:, cols], cc_ref[:, cols], ch_ref[:, cols], cv_ref[:, cols]
            dcv = dyc * cb * sl
            dg_ref[:, out(O_GCONV)] = (dyc * (cb * cvv) * (sg * (1.0 + gc * (1.0 - sg)))).astype(BF16)
            dg_ref[:, out(O_CB)] = (dyc * cvv * sl).astype(BF16)
            extd[0:tq, cols] = dcv
            ed = extd[:, cols]
            d1 = pltpu.roll(ed, n_ext - 1, 0)[0:tq, :]
            d2 = pltpu.roll(ed, n_ext - 2, 0)[0:tq, :]
            du = cw_ref[2:3, cols] * dcv + cw_ref[1:2, cols] * d1 + cw_ref[0:1, cols] * d2
            u = cc * ch
            dcw_ref[0:1, cols] += jnp.sum(u * d2, axis=0, keepdims=True)
            dcw_ref[1:2, cols] += jnp.sum(u * d1, axis=0, keepdims=True)
            dcw_ref[2:3, cols] += jnp.sum(u * dcv, axis=0, keepdims=True)
            dg_ref[:, out(O_CH)] = (du * cc).astype(BF16)
            dg_ref[:, out(O_CC)] = (du * ch).astype(BF16)

    def rblock(width, offset):
        assert offset % width == 0
        blk = offset // width
        return pl.BlockSpec((tq, width), lambda i: (nt - 1 - i, blk))

    full = lambda shape: pl.BlockSpec(shape, lambda i: (0,) * len(shape))
    return pl.pallas_call(
        body, name=name,
        out_shape=(jax.ShapeDtypeStruct((T, D_MLA), BF16), jax.ShapeDtypeStruct((N_HEADS, nt, 1, tq), F32),
                   jax.ShapeDtypeStruct((T, NPP), BF16),
                   jax.ShapeDtypeStruct((4, 128, 128), F32), jax.ShapeDtypeStruct((1, D_POOL), F32),
                   jax.ShapeDtypeStruct((8, D_CONV), F32)),
        grid=(nt,),
        in_specs=[rblock(1024, 0), rblock(512, 1024), rblock(512, 1536),
                  rblock(1024, O_GMLA), rblock(512, O_GPOOL), rblock(512, O_CH), rblock(512, O_CB),
                  rblock(512, O_CC), rblock(512, O_GCONV), rblock(1024, 0), rblock(512, 0), rblock(512, 0),
                  full((4, 128, 128)), full((1, D_POOL)), full((8, D_CONV))],
        out_specs=(rblock(1024, 0), pl.BlockSpec((N_HEADS, 1, 1, tq), lambda i: (0, nt - 1 - i, 0, 0)),
                   rblock(N_GATED, 0), full((4, 128, 128)), full((1, D_POOL)), full((8, D_CONV))),
        scratch_shapes=[pltpu.VMEM((n_ext, D_POOL), F32), pltpu.VMEM((n_ext, D_CONV), F32)],
        compiler_params=_params(("arbitrary",)),
    )(dmix, dmix, dmix, proj, proj, proj, proj, proj, proj, o, pooled, cv, w_pool, pool_scale, conv_w)


def _up_rms_bwd(proj, dq, dkv, w_uq, w_ukv, dkrope, dproj, q_g, kv_g, name, tq=512):
    T = proj.shape[0]
    n_lat = NPP - N_GATED

    def body(ql_ref, kvl_ref, dq_ref, dkv_ref, wq_ref, wkv_ref, dkr_ref, _, qg_ref, kvg_ref, dlat_ref, dqg_ref,
             dkvg_ref):
        @pl.when(pl.program_id(0) == 0)
        def _():
            dqg_ref[...] = jnp.zeros_like(dqg_ref)
            dkvg_ref[...] = jnp.zeros_like(dkvg_ref)

        dqn = lax.dot_general(dq_ref[...], wq_ref[...], _DIMS["nt"], preferred_element_type=F32)
        dkvn = lax.dot_general(dkv_ref[...], wkv_ref[...], _DIMS["nt"], preferred_element_type=F32)
        for x_ref, dy, g_ref, c0, dg_ref in ((ql_ref, dqn, qg_ref, 0, dqg_ref),
                                             (kvl_ref, dkvn, kvg_ref, Q_LORA, dkvg_ref)):
            x = x_ref[...]
            r = lax.rsqrt(jnp.mean(x * x, axis=1, keepdims=True) + RMS_EPS)
            xr = x * r
            u = dy * g_ref[...]
            dlat_ref[:, c0:c0 + x.shape[1]] = (r * (u - xr * jnp.mean(u * xr, axis=1, keepdims=True))).astype(BF16)
            dg_ref[...] += jnp.sum(dy * xr, axis=0, keepdims=True)
        dlat_ref[:, Q_LORA + KV_LORA:] = dkr_ref[...]

    row = lambda w: pl.BlockSpec((tq, w), lambda i: (i, 0))
    vec = lambda w: pl.BlockSpec((1, w), lambda i: (0, 0))
    assert N_GATED % n_lat == 0
    return pl.pallas_call(
        body, name=name,
        out_shape=(jax.ShapeDtypeStruct((T, NPP), BF16),
                   jax.ShapeDtypeStruct((1, Q_LORA), F32), jax.ShapeDtypeStruct((1, KV_LORA), F32)),
        grid=(T // tq,),
        in_specs=[_pblock(tq, Q_LORA, O_QLAT), _pblock(tq, KV_LORA, O_KVLAT), row(dq.shape[1]), row(dkv.shape[1]),
                  pl.BlockSpec(w_uq.shape, lambda i: (0, 0)), pl.BlockSpec(w_ukv.shape, lambda i: (0, 0)),
                  row(n_lat - Q_LORA - KV_LORA), pl.BlockSpec(memory_space=pl.ANY), vec(Q_LORA), vec(KV_LORA)],
        out_specs=(pl.BlockSpec((tq, n_lat), lambda i: (i, N_GATED // n_lat)), vec(Q_LORA), vec(KV_LORA)),
        input_output_aliases={7: 0},
        compiler_params=_params(("arbitrary",)),
    )(proj, proj, dq, dkv, w_uq, w_ukv, dkrope, dproj, q_g, kv_g)


def _swap_halves(x, lo):
    return jnp.where(lo, pltpu.roll(x, 96, 1), pltpu.roll(x, 32, 1))


def _up_rope_fwd(qn, kvn, w_uq, w_ukv, proj, cos_t, sin_t, name, tq=512, after=None):
    T = qn.shape[0]

    def body(qn_ref, kvn_ref, wq_ref, wkv_ref, kr_ref, c_ref, s_ref, *rest):
        kv_ref, qc_ref, kc_ref = rest[-3:]
        q = jnp.dot(qn_ref[...], wq_ref[...], preferred_element_type=F32)
        kv_ref[...] = jnp.dot(kvn_ref[...], wkv_ref[...], preferred_element_type=F32).astype(BF16)
        C, S = c_ref[...], s_ref[...]
        lane = lax.broadcasted_iota(jnp.int32, (tq, 128), 1)
        lo = (lane % ROPE) < (ROPE // 2)
        first = lane < ROPE

        def rope(x):
            return x * C + _swap_halves(x, lo) * S

        kr = jnp.where(first, rope(kr_ref[...]), 0.0).astype(BF16)
        n_nope = N_HEADS * NOPE
        for j in range(N_HEADS // 2):
            r = rope(q[:, n_nope + j * 128:n_nope + (j + 1) * 128])
            pair = (jnp.where(first, r, 0.0), jnp.where(first, pltpu.roll(r, 64, 1), 0.0))
            for hh in range(2):
                h = 2 * j + hh
                qc_ref[h, :, 0:NOPE] = q[:, h * NOPE:(h + 1) * NOPE].astype(BF16)
                qc_ref[h, :, NOPE:QC] = pair[hh].astype(BF16)
        for h in range(N_HEADS):
            kc_ref[h, :, 0:NOPE] = kv_ref[:, h * 256:h * 256 + NOPE]
            kc_ref[h, :, NOPE:QC] = kr

    out = jax.ShapeDtypeStruct((N_HEADS, T, QC), BF16)
    hblock = pl.BlockSpec((N_HEADS, tq, QC), lambda i: (0, i, 0))
    row = lambda w: pl.BlockSpec((tq, w), lambda i: (i, 0))
    full = lambda a: pl.BlockSpec(a.shape, lambda i: (0, 0))
    return pl.pallas_call(
        body, name=name, out_shape=(jax.ShapeDtypeStruct((T, 2 * D_MLA), BF16), out, out), grid=(T // tq,),
        in_specs=[row(Q_LORA), row(KV_LORA), full(w_uq), full(w_ukv), _pblock(tq, 128, O_KROPE), row(128), row(128)]
        + ([pl.BlockSpec((8, 128), lambda i: (0, 0))] if after is not None else []),
        out_specs=(row(2 * D_MLA), hblock, hblock),
        compiler_params=_params(("parallel",)),
    )(qn, kvn, w_uq, w_ukv, proj, cos_t, sin_t, *([after] if after is not None else []))


def _rope_bwd(dqc, dkr, cos_t, sin_t, name, tq=512):
    T = dqc.shape[1]

    def body(dqc_ref, dkr_ref, c_ref, s_ref, dq_ref, dk_ref):
        C, S = c_ref[...], s_ref[...]
        lane = lax.broadcasted_iota(jnp.int32, (tq, 128), 1)
        lo = (lane % ROPE) < (ROPE // 2)
        first = lane < ROPE

        def unrope(dy):
            return dy * C - _swap_halves(dy, lo) * S

        acc = dkr_ref[0]
        for h in range(1, N_HEADS):
            acc = acc + dkr_ref[h]
        dk_ref[:, 0:128] = jnp.where(first, unrope(acc), 0.0).astype(BF16)
        dk_ref[:, 128:256] = jnp.zeros((tq, 128), BF16)
        for j in range(N_HEADS // 2):
            d0 = dqc_ref[2 * j, :, NOPE:QC]
            d1 = dqc_ref[2 * j + 1, :, NOPE:QC]
            comb = jnp.where(first, d0, pltpu.roll(d1, 64, 1))
            dq_ref[:, 1024 + j * 128:1024 + (j + 1) * 128] = unrope(comb).astype(BF16)
        for h in range(N_HEADS):
            dq_ref[:, h * NOPE:(h + 1) * NOPE] = dqc_ref[h, :, 0:NOPE].astype(BF16)

    tab = pl.BlockSpec((tq, 128), lambda i: (i, 0))
    return pl.pallas_call(
        body, name=name,
        out_shape=(jax.ShapeDtypeStruct((T, 1536), BF16), jax.ShapeDtypeStruct((T, 256), BF16)),
        grid=(T // tq,),
        in_specs=[pl.BlockSpec((N_HEADS, tq, QC), lambda i: (0, i, 0)),
                  pl.BlockSpec((N_HEADS, tq, 128), lambda i: (0, i, 0)), tab, tab],
        out_specs=(pl.BlockSpec((tq, 1536), lambda i: (i, 0)), pl.BlockSpec((tq, 256), lambda i: (i, 0))),
        compiler_params=_params(("parallel",)),
    )(dqc, dkr, cos_t, sin_t)


def _flash_fwd(qc, kc, kv, proj, mix, name):
    H, T, _ = qc.shape
    tt = ATT_TILE
    nt = T // tt
    sp = tt // ATT_CH
    pairs = [(i, c, int(c == i)) for i in range(nt) for c in range(i + 1)]
    assert len(pairs) % 2 == 0
    table = jnp.asarray(pairs + [pairs[-1]], jnp.int32)

    def body(tab_ref, q_ref, k_ref, v_ref, g_ref, _, o_ref, y_ref, lse_ref, vt_sc, s_sc, acc_sc, m_sc, l_sc, bias_sc):
        def issue(p, slot):
            i, c = tab_ref[p, 0], tab_ref[p, 1]
            s_sc[slot] = lax.dot_general(k_ref[0, pl.ds(pl.multiple_of(c * tt, tt), tt), :],
                                         q_ref[0, pl.ds(pl.multiple_of(i * tt, tt), tt), :], _DIMS["nt"],
                                         preferred_element_type=F32)

        def softmax_pv(p, slot):
            i, c, diag = tab_ref[p, 0], tab_ref[p, 1], tab_ref[p, 2]
            s = s_sc[slot] + bias_sc[diag]
            m = m_sc[i]
            m_new = jnp.maximum(m, jnp.max(s, axis=0, keepdims=True))
            pr = jnp.exp2((s - m_new) * EXP2_SCALE)
            a = jnp.exp2((m - m_new) * EXP2_SCALE)
            l_sc[i] = a * l_sc[i] + jnp.sum(pr, axis=0, keepdims=True)
            acc_sc[i] = a * acc_sc[i] + jnp.dot(vt_sc[c], pr.astype(BF16), preferred_element_type=F32)
            m_sc[i] = m_new

        issue(0, 0)
        m_sc[...] = jnp.full_like(m_sc, -jnp.inf)
        l_sc[...] = jnp.zeros_like(l_sc)
        acc_sc[...] = jnp.zeros_like(acc_sc)
        krow = lax.broadcasted_iota(jnp.int32, (tt, tt), 0)
        qcol = lax.broadcasted_iota(jnp.int32, (tt, tt), 1)
        bias_sc[0] = jnp.zeros((tt, tt), F32)
        bias_sc[1] = jnp.where(krow <= qcol, 0.0, -jnp.inf)
        for c in range(nt):
            vt_sc[c] = v_ref[c * tt:(c + 1) * tt, :].astype(F32).T.astype(BF16)

        def two(u, carry):
            p = 2 * u
            issue(p + 1, 1)
            softmax_pv(p, 0)
            issue(p + 2, 0)
            softmax_pv(p + 1, 1)
            return carry

        lax.fori_loop(0, len(pairs) // 2, two, 0)
        for i in range(nt):
            rows = slice(i * tt, (i + 1) * tt)
            l = l_sc[i]
            o = (acc_sc[i] / l).T
            o_ref[rows, :] = o
            lse = m_sc[i] * ATTN_SCALE + jnp.log(l)
            for r in range(sp):
                lse_ref[0, sp * i + r] = lse[:, r * ATT_CH:(r + 1) * ATT_CH]
            g = g_ref[rows, :]
            y_ref[rows, :] = (o * (g * _sigmoid(g))).astype(BF16)

    head = lambda h, tab: (h, 0, 0)
    col = lambda h, tab: (0, h)
    return pl.pallas_call(
        body, name=name,
        out_shape=(jax.ShapeDtypeStruct((T, D_MLA), F32), jax.ShapeDtypeStruct((T, D_MIX), BF16),
                   jax.ShapeDtypeStruct((H, T // ATT_CH, 1, ATT_CH), F32)),
        grid_spec=pltpu.PrefetchScalarGridSpec(
            num_scalar_prefetch=1, grid=(H,),
            in_specs=[pl.BlockSpec((1, T, QC), head), pl.BlockSpec((1, T, QC), head),
                      pl.BlockSpec((T, V_DIM), lambda h, tab: (0, 2 * h + 1)), pl.BlockSpec((T, V_DIM), col),
                      pl.BlockSpec(memory_space=pl.ANY)],
            out_specs=(pl.BlockSpec((T, V_DIM), col), pl.BlockSpec((T, V_DIM), col),
                       pl.BlockSpec((1, T // ATT_CH, 1, ATT_CH), lambda h, tab: (h, 0, 0, 0))),
            scratch_shapes=[pltpu.VMEM((nt, V_DIM, tt), BF16), pltpu.VMEM((2, tt, tt), F32),
                            pltpu.VMEM((nt, V_DIM, tt), F32), pltpu.VMEM((nt, 1, tt), F32),
                            pltpu.VMEM((nt, 1, tt), F32), pltpu.VMEM((2, tt, tt), F32)]),
        input_output_aliases={5: 1},
        compiler_params=_params(("arbitrary",)),
    )(table, qc, kc, kv, proj, mix)


def _flash_bwd(qc, kc, kv, do, lse, delta, name):
    H, T, _ = qc.shape
    tt = ATT_TILE
    nt = T // tt
    sp = tt // ATT_CH
    pairs = [(j, c) for j in range(nt) for c in range(j, nt)]
    assert len(pairs) % 2 == 0
    table = jnp.asarray(pairs + [pairs[-1]], jnp.int32)

    def body(tab_ref, q_ref, k_ref, v_ref, do_ref, lse_ref, dl_ref, dq_ref, dkv_ref, dkr_ref, dqt_sc, dk_sc, dv_sc,
             s_sc, dp_sc, kt_sc, bias_sc):
        def operands(j, c):
            k0, q0 = pl.multiple_of(j * tt, tt), pl.multiple_of(c * tt, tt)
            return (k_ref[0, pl.ds(k0, tt), :], v_ref[pl.ds(k0, tt), :], q_ref[0, pl.ds(q0, tt), :],
                    do_ref[pl.ds(q0, tt), :])

        def stat_row(ref, c):
            return jnp.concatenate([ref[0, sp * c + r] for r in range(sp)], axis=1)

        def early(p, slot):
            k, v, q, dov = operands(tab_ref[p, 0], tab_ref[p, 1])
            s_sc[slot] = lax.dot_general(k, q, _DIMS["nt"], preferred_element_type=F32)
            dp_sc[slot] = lax.dot_general(v, dov, _DIMS["nt"], preferred_element_type=F32)

        def late(p, slot):
            j, c = tab_ref[p, 0], tab_ref[p, 1]
            _, _, q, dov = operands(j, c)
            s = s_sc[slot] + jnp.where(j == c, bias_sc[...], 0.0)
            pr = jnp.exp2(s * EXP2_SCALE - stat_row(lse_ref, c) * LOG2E)
            ds = (pr * (dp_sc[slot] - stat_row(dl_ref, c)) * ATTN_SCALE).astype(BF16)
            dv_sc[j] += jnp.dot(pr.astype(BF16), dov, preferred_element_type=F32)
            dk_sc[j] += jnp.dot(ds, q, preferred_element_type=F32)
            dqt_sc[c] += jnp.dot(kt_sc[j], ds, preferred_element_type=F32)

        early(0, 0)
        dqt_sc[...] = jnp.zeros_like(dqt_sc)
        dk_sc[...] = jnp.zeros_like(dk_sc)
        dv_sc[...] = jnp.zeros_like(dv_sc)
        krow = lax.broadcasted_iota(jnp.int32, (tt, tt), 0)
        qcol = lax.broadcasted_iota(jnp.int32, (tt, tt), 1)
        bias_sc[...] = jnp.where(krow <= qcol, 0.0, -jnp.inf)
        for j in range(nt):
            kt_sc[j] = k_ref[0, j * tt:(j + 1) * tt, :].astype(F32).T.astype(BF16)

        def two(u, carry):
            p = 2 * u
            early(p + 1, 1)
            late(p, 0)
            early(p + 2, 0)
            late(p + 1, 1)
            return carry

        lax.fori_loop(0, len(pairs) // 2, two, 0)
        for j in range(nt):
            rows = slice(j * tt, (j + 1) * tt)
            dk = dk_sc[j]
            dkv_ref[rows, 0:NOPE] = dk[:, 0:NOPE].astype(BF16)
            dkv_ref[rows, NOPE:] = dv_sc[j].astype(BF16)
            dkr_ref[0, rows, :] = dk[:, NOPE:]
            dq_ref[0, rows, :] = dqt_sc[j].T

    head = lambda h, tab: (h, 0, 0)
    stat = pl.BlockSpec((1, T // ATT_CH, 1, ATT_CH), lambda h, tab: (h, 0, 0, 0))
    return pl.pallas_call(
        body, name=name,
        out_shape=(jax.ShapeDtypeStruct((H, T, QC), F32), jax.ShapeDtypeStruct((T, 2 * D_MLA), BF16),
                   jax.ShapeDtypeStruct((H, T, 128), F32)),
        grid_spec=pltpu.PrefetchScalarGridSpec(
            num_scalar_prefetch=1, grid=(H,),
            in_specs=[pl.BlockSpec((1, T, QC), head), pl.BlockSpec((1, T, QC), head),
                      pl.BlockSpec((T, V_DIM), lambda h, tab: (0, 2 * h + 1)),
                      pl.BlockSpec((T, V_DIM), lambda h, tab: (0, h)), stat, stat],
            out_specs=(pl.BlockSpec((1, T, QC), head), pl.BlockSpec((T, 256), lambda h, tab: (0, h)),
                       pl.BlockSpec((1, T, 128), head)),
            scratch_shapes=[pltpu.VMEM((nt, QC, tt), F32), pltpu.VMEM((nt, tt, QC), F32),
                            pltpu.VMEM((nt, tt, V_DIM), F32), pltpu.VMEM((2, tt, tt), F32),
                            pltpu.VMEM((2, tt, tt), F32), pltpu.VMEM((nt, QC, tt), BF16), pltpu.VMEM((tt, tt), F32)]),
        compiler_params=pltpu.CompilerParams(dimension_semantics=("arbitrary",),
                                             vmem_limit_bytes=ATT_BWD_VMEM_LIMIT),
    )(table, qc, kc, kv, do, lse, delta)


def _adamw(lands, w, m, v, name, rows, cols=None, first_layer=0, into=None):
    layers, R, C = w.shape
    L = len(lands)
    cols = C if cols is None else cols
    assert R % rows == 0 and C % cols == 0 and first_layer + L <= layers
    nc = C // cols
    nb = (R // rows) * nc
    c1 = 1.0 - ADAM_B1 ** ADAM_STEP
    c2 = 1.0 - ADAM_B2 ** ADAM_STEP

    def body(*refs):
        land_refs = refs[:L]
        w_ref, m_ref, v_ref = refs[L:L + 3]
        g_ref, d_ref, nm_ref, nv_ref, g_sc = refs[-5:]
        for ll in range(L):
            @pl.when(pl.program_id(0) == ll)
            def _(land_ref=land_refs[ll]):
                g = land_ref[0].astype(F32)
                for s in range(1, N_DEV):
                    g = g + land_ref[s].astype(F32)
                g_sc[...] = g

        g = g_sc[...]
        nm = ADAM_B1 * m_ref[0] + (1.0 - ADAM_B1) * g
        nv = ADAM_B2 * v_ref[0] + (1.0 - ADAM_B2) * (g * g)
        g_ref[0] = g
        nm_ref[0] = nm
        nv_ref[0] = nv
        d_ref[0] = -ADAM_LR * ((nm / c1) / (jnp.sqrt(nv / c2) + ADAM_EPS) + ADAM_WD * w_ref[0])

    def land_spec(ll):
        def index(l, i):
            i = jnp.where(l < ll, 0, jnp.where(l > ll, nb - 1, i))
            return (0, i // nc, i % nc)
        return pl.BlockSpec((N_DEV, rows, cols), index)

    blk = pl.BlockSpec((1, rows, cols), lambda l, i: (first_layer + l, i // nc, i % nc))
    out = jax.ShapeDtypeStruct((layers, R, C), F32)
    extra = [] if into is None else list(into)
    return pl.pallas_call(
        body, name=name, out_shape=(out, out, out, out), grid=(L, nb),
        in_specs=[land_spec(ll) for ll in range(L)] + [blk, blk, blk] + [pl.BlockSpec(memory_space=pl.ANY)] * len(extra),
        out_specs=(blk, blk, blk, blk),
        input_output_aliases={L + 3 + i: i for i in range(len(extra))},
        scratch_shapes=[pltpu.VMEM((rows, cols), F32)],
        compiler_params=_params(("arbitrary", "arbitrary")),
    )(*lands, w, m, v, *extra)


def _mesh_pos():
    return lax.axis_index("x"), lax.axis_index("y"), lax.axis_index("c")


def _all_gather(arrays, name):
    n = len(arrays)

    def body(*refs):
        ins, outs = refs[:n], refs[n:2 * n]
        send_sems, recv_sems, local_sems = refs[2 * n:]
        x, y, c = _mesh_pos()
        me, sibling = (x, y, c), (x, y, 1 - c)
        chips = [(1 - x, y), (x, 1 - y), (1 - x, 1 - y)]

        def slot(a, pos):
            px, py, pc = pos
            return outs[a].at[4 * px + 2 * py + pc]

        def copy(a, k, block, to, src=None):
            return pltpu.make_async_remote_copy(
                src_ref=slot(a, block) if src is None else src, dst_ref=slot(a, block),
                send_sem=send_sems.at[a * 7 + k], recv_sem=recv_sems.at[a * 7 + k],
                device_id=to, device_id_type=MESH_ID)

        mine, first, passed = [], [], []
        for a in range(n):
            cp = pltpu.make_async_copy(ins[a], slot(a, me), local_sems.at[a])
            cp.start()
            mine.append(cp)
            cps = [copy(a, 0, me, sibling, src=ins[a])]
            cps += [copy(a, 1 + j, me, (*chip, c), src=ins[a]) for j, chip in enumerate(chips)]
            for cp in cps:
                cp.start()
            first += cps
        for j, chip in enumerate(chips):
            for a in range(n):
                copy(a, 1 + j, (*chip, c), me).wait_recv()
                cp = copy(a, 4 + j, (*chip, c), sibling)
                cp.start()
                passed.append(cp)
        for a in range(n):
            copy(a, 0, sibling, me).wait_recv()
            for j, chip in enumerate(chips):
                copy(a, 4 + j, (*chip, 1 - c), me).wait_recv()
        for cp in first + passed:
            cp.wait_send()
        for cp in mine:
            cp.wait()

    hbm = pl.BlockSpec(memory_space=pltpu.HBM)
    return pl.pallas_call(
        body, name=name,
        out_shape=tuple(jax.ShapeDtypeStruct((N_DEV,) + a.shape, a.dtype) for a in arrays),
        in_specs=[hbm] * n, out_specs=tuple([hbm] * n),
        scratch_shapes=[pltpu.SemaphoreType.DMA((7 * n,)), pltpu.SemaphoreType.DMA((7 * n,)),
                        pltpu.SemaphoreType.DMA((n,))],
    )(*arrays)


def _all_gather_under_ln(arrays, x, g, b, name, tq=512):
    n = len(arrays)
    T, D = x.shape
    nt = T // tq

    def body(*refs):
        x_ref, g_ref, b_ref = refs[:3]
        ins = refs[3:3 + n]
        y_ref, yb_ref = refs[3 + n:5 + n]
        outs = refs[5 + n:5 + 2 * n]
        send_sems, recv_sems, local_sems = refs[5 + 2 * n:]
        i = pl.program_id(0)
        mx, my, mc = _mesh_pos()
        me, sibling = (mx, my, mc), (mx, my, 1 - mc)
        chips = [(1 - mx, my), (mx, 1 - my), (1 - mx, 1 - my)]

        def slot(a, pos):
            px, py, pc = pos
            return outs[a].at[4 * px + 2 * py + pc]

        def copy(a, k, block, to, src=None):
            return pltpu.make_async_remote_copy(
                src_ref=slot(a, block) if src is None else src, dst_ref=slot(a, block),
                send_sem=send_sems.at[a * 7 + k], recv_sem=recv_sems.at[a * 7 + k],
                device_id=to, device_id_type=MESH_ID)

        def own(a):
            return pltpu.make_async_copy(ins[a], slot(a, me), local_sems.at[a])

        def first(a):
            return [copy(a, 0, me, sibling, src=ins[a])] + [
                copy(a, 1 + j, me, (*chip, mc), src=ins[a]) for j, chip in enumerate(chips)]

        @pl.when(i == 0)
        def _():
            for a in range(n):
                own(a).start()
                for cp in first(a):
                    cp.start()

        zv = x_ref[...]
        mu = jnp.mean(zv, axis=1, keepdims=True)
        zc = zv - mu
        var = jnp.mean(zc * zc, axis=1, keepdims=True)
        y = zc * lax.rsqrt(var + LN_EPS) * g_ref[...] + b_ref[...]
        y_ref[...] = y
        yb_ref[...] = y.astype(BF16)

        @pl.when(i == nt - 1)
        def _():
            passed = []
            for j, chip in enumerate(chips):
                for a in range(n):
                    copy(a, 1 + j, (*chip, mc), me).wait_recv()
                    cp = copy(a, 4 + j, (*chip, mc), sibling)
                    cp.start()
                    passed.append(cp)
            for a in range(n):
                copy(a, 0, sibling, me).wait_recv()
                for j, chip in enumerate(chips):
                    copy(a, 4 + j, (*chip, 1 - mc), me).wait_recv()
            for a in range(n):
                for cp in first(a):
                    cp.wait_send()
                own(a).wait()
            for cp in passed:
                cp.wait_send()

    row = pl.BlockSpec((tq, D), lambda i: (i, 0))
    vec = pl.BlockSpec((1, D), lambda i: (0, 0))
    hbm = pl.BlockSpec(memory_space=pltpu.HBM)
    outs = pl.pallas_call(
        body, name=name,
        out_shape=(jax.ShapeDtypeStruct((T, D), F32), jax.ShapeDtypeStruct((T, D), BF16))
        + tuple(jax.ShapeDtypeStruct((N_DEV,) + a.shape, a.dtype) for a in arrays),
        grid=(nt,), in_specs=[row, vec, vec] + [hbm] * n, out_specs=tuple([row, row] + [hbm] * n),
        scratch_shapes=[pltpu.SemaphoreType.DMA((7 * n,)), pltpu.SemaphoreType.DMA((7 * n,)),
                        pltpu.SemaphoreType.DMA((n,))],
        compiler_params=_params(("arbitrary",)),
    )(x, g, b, *arrays)
    return outs[0], outs[1], outs[2:]


_HBM = pl.BlockSpec(memory_space=pltpu.HBM)
_SEM = pl.BlockSpec(memory_space=pltpu.SEMAPHORE)
_EFFECT = pltpu.SideEffectType.DATAFLOW_SIDE_EFFECTING
N_PEERS = N_DEV - 1


def _peer(k):
    x, y, c = _mesh_pos()
    return (1 - x if k & 4 else x, 1 - y if k & 2 else y, 1 - c if k & 1 else c)


def _split_start(srcs, scatter, after, name):
    n = len(srcs)
    zones = [jax.ShapeDtypeStruct(s.shape if scatter else ((N_DEV,) + s.shape), s.dtype) for s in srcs]

    def body(*refs):
        src, zone = refs[:n], refs[n:2 * n]
        outs = refs[2 * n + 1:]
        send, recv, token = outs[:n], outs[n:2 * n], outs[4 * n]
        x, y, c = _mesh_pos()
        my_idx = 4 * x + 2 * y + c
        for a in range(n):
            pltpu.make_async_copy(src[a].at[my_idx] if scatter else src[a],
                                  zone[a].at[N_PEERS] if scatter else zone[a].at[my_idx], recv[a]).start()
            for k in range(1, N_DEV):
                px, py, pc = _peer(k)
                pltpu.make_async_remote_copy(
                    src_ref=src[a].at[4 * px + 2 * py + pc] if scatter else src[a],
                    dst_ref=zone[a].at[k - 1] if scatter else zone[a].at[my_idx],
                    send_sem=send[a], recv_sem=recv[a], device_id=(px, py, pc), device_id_type=MESH_ID).start()
        token[...] = jnp.zeros_like(token)

    hbm = lambda a: pltpu.with_memory_space_constraint(a, pltpu.HBM)
    outs = pl.pallas_call(
        body, name=name,
        out_shape=tuple([pltpu.SemaphoreType.DMA(())] * (2 * n)
                        + [pltpu.HBM(s.shape, s.dtype) for s in srcs]
                        + [pltpu.HBM(z.shape, z.dtype) for z in zones]
                        + [jax.ShapeDtypeStruct((8, 128), F32)]),
        in_specs=[_HBM] * (2 * n) + [pl.BlockSpec(memory_space=pl.ANY)],
        out_specs=tuple([_SEM] * (2 * n) + [_HBM] * (2 * n) + [pl.BlockSpec(memory_space=pltpu.VMEM)]),
        input_output_aliases={**{a: 2 * n + a for a in range(n)}, **{n + a: 3 * n + a for a in range(n)}},
        compiler_params=pltpu.CompilerParams(has_side_effects=_EFFECT),
    )(*[hbm(s) for s in srcs], *[hbm(lax.empty(z.shape, z.dtype)) for z in zones], after)
    return outs[:n], outs[n:2 * n], outs[2 * n:3 * n], outs[3 * n:4 * n], outs[4 * n]


def _split_wait(send, recv, srcs, zones, after, name):
    n = len(srcs)

    def body(*refs):
        zone = refs[n:2 * n]
        send_sems, recv_sems = refs[2 * n:3 * n], refs[3 * n:4 * n]
        x, y, c = _mesh_pos()
        for a in range(n):
            seven = zone[a].at[pl.ds(0, N_PEERS)]
            pltpu.make_async_remote_copy(src_ref=seven, dst_ref=seven, send_sem=send_sems[a], recv_sem=recv_sems[a],
                                         device_id=(x, y, 1 - c), device_id_type=MESH_ID).wait_send()
            pltpu.make_async_remote_copy(src_ref=zone[a], dst_ref=zone[a], send_sem=send_sems[a],
                                         recv_sem=recv_sems[a], device_id=(x, y, 1 - c),
                                         device_id_type=MESH_ID).wait_recv()

    outs = pl.pallas_call(
        body, name=name,
        out_shape=tuple([pltpu.HBM(s.shape, s.dtype) for s in srcs] + [pltpu.HBM(z.shape, z.dtype) for z in zones]),
        in_specs=[_HBM] * (2 * n) + [_SEM] * (2 * n) + [pl.BlockSpec(memory_space=pl.ANY)],
        out_specs=tuple([_HBM] * (2 * n)),
        input_output_aliases={a: a for a in range(2 * n)},
        compiler_params=pltpu.CompilerParams(has_side_effects=_EFFECT),
    )(*srcs, *zones, *send, *recv, after)
    return outs[:n], outs[n:]


def _cat_blocks(g, axis):
    return jnp.concatenate([g[d] for d in range(N_DEV)], axis=axis)


N_LATENT = Q_LORA + KV_LORA + ROPE
W_SHARD = D_IN_PROJ // N_DEV


def _ref_cols(lo, hi):
    out = []
    if lo < N_LATENT:
        out.append((N_GATED + lo, N_GATED + min(hi, N_LATENT)))
    if hi > N_LATENT:
        out.append((max(lo, N_LATENT) - N_LATENT, hi - N_LATENT))
    return out


def _permute_w_in_t(blocks):
    pieces = []
    for lo, hi in ((N_LATENT, D_IN_PROJ), (0, N_LATENT)):
        for d in range(N_DEV):
            a, b = max(lo, d * W_SHARD), min(hi, (d + 1) * W_SHARD)
            if a < b:
                pieces.append(blocks[d][a - d * W_SHARD:b - d * W_SHARD])
    pieces.append(jnp.zeros((NPP - D_IN_PROJ, blocks.shape[2]), blocks.dtype))
    return jnp.concatenate(pieces, axis=0)


def _split_w_in_t(w):
    slabs = []
    for d in range(N_DEV):
        parts = [w[a:b] for a, b in _ref_cols(d * W_SHARD, (d + 1) * W_SHARD)]
        slabs.append(parts[0] if len(parts) == 1 else jnp.concatenate(parts, axis=0))
    return jnp.stack(slabs)


def _permute_w_uq(w):
    w3 = w.reshape(w.shape[0], N_HEADS, NOPE + ROPE)
    return jnp.concatenate([w3[:, :, :NOPE].reshape(w.shape[0], -1), w3[:, :, NOPE:].reshape(w.shape[0], -1)], axis=1)


def _unpermute_w_uq(w):
    nope = w[:, :N_HEADS * NOPE].reshape(w.shape[0], N_HEADS, NOPE)
    rope = w[:, N_HEADS * NOPE:].reshape(w.shape[0], N_HEADS, ROPE)
    return jnp.concatenate([nope, rope], axis=2).reshape(w.shape[0], -1)


_SMALL_EMB = (("emb_ln_g", 16), ("emb_ln_b", 16))
_SMALL_LAYER = (("q_norm_g", 8), ("kv_norm_g", 8), ("w_pool", 1024), ("pool_scale", 8), ("b_out", 32),
                ("ln_g", 32), ("ln_b", 32))
_SMALL = _SMALL_EMB + _SMALL_LAYER
CONV_ROWS = DEPTH * CONV_WIDTH * D_CONV // 128


def _pack_small(d, entries=_SMALL):
    parts = []
    for name, rows in entries:
        flat = d[name].reshape(-1)
        flat = jnp.pad(flat, (0, rows * 128 - flat.shape[0]))
        parts.append(flat.reshape(rows, 128))
    return jnp.concatenate(parts, axis=0)


def _unpack_small(packed, shapes):
    out, r0 = {}, 0
    for name, rows in _SMALL:
        size = 1
        for s in shapes[name]:
            size *= s
        out[name] = packed[r0:r0 + rows].reshape(-1)[:size].reshape(shapes[name])
        r0 += rows
    return out


def _rope_tables(positions):
    half = ROPE // 2
    inv_freq = ROPE_THETA ** (-jnp.arange(half, dtype=F32) / half)
    ang = positions.astype(F32)[:, None] * inv_freq
    cos, sin = jnp.cos(ang), jnp.sin(ang)
    return jnp.concatenate([cos, cos, cos, cos], axis=1), jnp.concatenate([-sin, sin, -sin, sin], axis=1)


def _local_step(x, positions, target, emb_g, emb_b, layer_weights, layer_weights_rest, on_sharded_grads,
                on_layer_grads=None, first_after=None, embedded=None):
    cos_t, sin_t = _rope_tables(positions)
    h, hb = _ln_fwd(x, emb_g, emb_b, "emb_ln_fwd") if embedded is None else embedded
    saved = []
    for l in range(DEPTH):
        W = layer_weights(l, h)
        proj = _mm(hb, W["w_in_t"], "nt", F32, "proj_fwd", after=first_after if l == 0 else None)
        qn, kvn, pooled, cv, mix = _mix_fwd(proj, W["q_norm_g"], W["kv_norm_g"], W["w_pool"], W["pool_scale"],
                                            W["conv_w"], "mix_fwd")
        rest, token = layer_weights_rest(l, proj)
        W = {**W, **rest}
        kv, qc, kc = _up_rope_fwd(qn, kvn, W["w_uq"], W["w_ukv"], proj, cos_t, sin_t, "up_rope_fwd", after=token)
        o, mix, lse = _flash_fwd(qc, kc, kv, proj, mix, "flash_fwd")
        z = _mm(mix, W["w_out"], "nn", F32, "out_fwd", res=h, bias=W["b_out"], alpha=ALPHA)
        saved.append((W, hb, proj, qn, kvn, pooled, cv, kv, qc, kc, o, lse, mix, z))
        last = l == DEPTH - 1
        h, hb = _ln_fwd(z, W["ln_g"], W["ln_b"], "ln_fwd_out" if last else "ln_fwd", for_matmul=not last)
    dh, sq = h, None

    grads = {k: [None] * DEPTH for k in ("q_norm_g", "kv_norm_g", "w_pool", "pool_scale", "conv_w", "b_out", "ln_g",
                                         "ln_b")}
    for l in reversed(range(DEPTH)):
        W, hb_in, proj, qn, kvn, pooled, cv, kv, qc, kc, o, lse, mix, z = saved[l]
        sharded = {}
        if l == DEPTH - 1:
            dz, dzb, grads["b_out"][l], grads["ln_g"][l], grads["ln_b"][l], sq = _ln_bwd(
                dh, z, W["ln_g"], "ln_bwd_loss", target=target)
        else:
            dz, dzb, grads["b_out"][l], grads["ln_g"][l], grads["ln_b"][l] = _ln_bwd(dh, z, W["ln_g"], "ln_bwd")
        dmix = _mm(dzb, W["w_out"], "nt", F32, "out_bwd_x")
        sharded["w_out"] = _mm(mix, dzb, "tn", GRAD_XFER, "out_bwd_w", tk=4096)
        do, delta, dproj, grads["w_pool"][l], grads["pool_scale"][l], grads["conv_w"][l] = _mix_bwd(
            dmix, proj, o, pooled, cv, W["w_pool"], W["pool_scale"], W["conv_w"], "mix_bwd")
        dqc, dkv, dkr = _flash_bwd(qc, kc, kv, do, lse, delta, "flash_bwd")
        dq, dkrope = _rope_bwd(dqc, dkr, cos_t, sin_t, "rope_bwd")
        sharded["w_uq"] = _mm(qn, dq, "tn", GRAD_XFER, "q_up_bwd_w")
        sharded["w_ukv"] = _mm(kvn, dkv, "tn", GRAD_XFER, "kv_up_bwd_w")
        token = on_sharded_grads(l, sharded)
        dproj, grads["q_norm_g"][l], grads["kv_norm_g"][l] = _up_rms_bwd(
            proj, dq, dkv, W["w_uq"], W["w_ukv"], dkrope, dproj, W["q_norm_g"], W["kv_norm_g"], "up_rms_bwd")
        if l == 0 and on_layer_grads is not None:
            token = on_layer_grads(grads, token)
        d_w_in_t = _mm(dproj, hb_in, "tn", GRAD_XFER, "proj_bwd_w", tk=4096, after=token)
        token = on_sharded_grads(l, {"w_in": d_w_in_t})
        dh = _mm(dproj, W["w_in_t"], "nn", F32, "proj_bwd_x", res=dz, alpha=ALPHA, tk=2560, after=token)
    grad_x, grads["emb_ln_g"], grads["emb_ln_b"] = _ln_bwd(dh, x, emb_g, "emb_ln_bwd", for_matmul=False)
    return sq, grad_x, grads


def kernel(x, positions, emb_ln_g, emb_ln_b, w_in, q_norm_g, kv_norm_g, w_uq, w_ukv, w_pool, pool_scale, conv_w, w_out, b_out, ln_g, ln_b, loss_target, m_emb_ln_g, m_emb_ln_b, m_w_in, m_q_norm_g, m_kv_norm_g, m_w_uq, m_w_ukv, m_w_pool, m_pool_scale, m_conv_w, m_w_out, m_b_out, m_ln_g, m_ln_b, v_emb_ln_g, v_emb_ln_b, v_w_in, v_q_norm_g, v_kv_norm_g, v_w_uq, v_w_ukv, v_w_pool, v_pool_scale, v_conv_w, v_w_out, v_b_out, v_ln_g, v_ln_b):
    weights = dict(emb_ln_g=emb_ln_g, emb_ln_b=emb_ln_b, w_in=w_in, q_norm_g=q_norm_g, kv_norm_g=kv_norm_g,
                   w_uq=w_uq, w_ukv=w_ukv, w_pool=w_pool, pool_scale=pool_scale, conv_w=conv_w, w_out=w_out,
                   b_out=b_out, ln_g=ln_g, ln_b=ln_b)
    mom1 = dict(emb_ln_g=m_emb_ln_g, emb_ln_b=m_emb_ln_b, w_in=m_w_in, q_norm_g=m_q_norm_g, kv_norm_g=m_kv_norm_g,
                w_uq=m_w_uq, w_ukv=m_w_ukv, w_pool=m_w_pool, pool_scale=m_pool_scale, conv_w=m_conv_w,
                w_out=m_w_out, b_out=m_b_out, ln_g=m_ln_g, ln_b=m_ln_b)
    mom2 = dict(emb_ln_g=v_emb_ln_g, emb_ln_b=v_emb_ln_b, w_in=v_w_in, q_norm_g=v_q_norm_g, kv_norm_g=v_kv_norm_g,
                w_uq=v_w_uq, w_ukv=v_w_ukv, w_pool=v_w_pool, pool_scale=v_pool_scale, conv_w=v_conv_w,
                w_out=v_w_out, b_out=v_b_out, ln_g=v_ln_g, ln_b=v_ln_b)

    big = ("w_in", "w_uq", "w_ukv", "w_out")

    conv_pad = jnp.zeros((8, 128), F32).at[0:DEPTH * CONV_WIDTH, 0:64].set(conv_w.reshape(DEPTH * CONV_WIDTH, 64))
    t12 = lambda a: jnp.swapaxes(a, 1, 2)
    shard = lambda k, l: (t12(weights[k])[l] if k == "w_in" else weights[k][l]).astype(BF16)
    h0, h0b, (w_in0, conv_all) = _all_gather_under_ln(
        [shard("w_in", 0), conv_pad], x[0], emb_ln_g.reshape(1, -1), emb_ln_b.reshape(1, -1), "w_in0_all_gather_emb_ln")
    rest0 = _split_start([shard(k, 0) for k in big[1:]], False, w_in0, "weights0_rest_start")
    conv_full = _cat_blocks(conv_all[:, 0:DEPTH * CONV_WIDTH, 0:64], 1).reshape(DEPTH, CONV_WIDTH, D_CONV)
    conv_full = jnp.pad(conv_full, ((0, 0), (0, 8 - CONV_WIDTH), (0, 0)))
    fetched = {}

    def layer_weights(l, ready):
        if l == 0:
            w_in_blocks = w_in0
        else:
            fetched[1] = _split_wait(*fetched["w1"][:4], ready, "weights1_wait")[1]
            w_in_blocks = fetched[1][0]
        return dict(
            w_in_t=_permute_w_in_t(w_in_blocks), conv_w=conv_full[l],
            q_norm_g=q_norm_g[l].reshape(1, -1), kv_norm_g=kv_norm_g[l].reshape(1, -1),
            w_pool=w_pool[l].astype(BF16), pool_scale=pool_scale[l].reshape(1, -1), b_out=b_out[l].reshape(1, -1),
            ln_g=ln_g[l].reshape(1, -1), ln_b=ln_b[l].reshape(1, -1))

    def layer_weights_rest(l, ready):
        token = None
        if l == 0:
            blocks = _split_wait(*rest0[:4], ready, "weights0_rest_wait")[1]
            fetched["w1"] = _split_start([shard(k, 1) for k in big], False, blocks[0], "weights1_start")
            token = fetched["w1"][4]
        else:
            blocks = fetched[1][1:]
        return dict(w_uq=_permute_w_uq(_cat_blocks(blocks[0], 1)), w_ukv=_cat_blocks(blocks[1], 1),
                    w_out=blocks[2].reshape(D_MIX, D_MODEL)), token

    by_dest = dict(
        w_in=_split_w_in_t,
        w_uq=lambda g: _unpermute_w_uq(g).reshape(Q_LORA, N_DEV, -1).transpose(1, 0, 2),
        w_ukv=lambda g: g.reshape(KV_LORA, N_DEV, -1).transpose(1, 0, 2),
        w_out=lambda g: g.reshape(N_DEV, -1, D_MODEL))
    in_flight = []

    def on_sharded_grads(l, g):
        names = [k for k in big if k in g]
        srcs = [by_dest[k](g[k]) for k in names]
        started = _split_start(srcs, True, srcs[0], "grads%d_%s_start" % (l, names[0]))
        in_flight.append((l, names, started[:4]))
        return started[4]

    small_in_flight = []

    def on_layer_grads(g, token):
        stacked = {k: jnp.stack(g[k]) for k, _ in _SMALL_LAYER}
        conv = jnp.stack([g["conv_w"][l][0:CONV_WIDTH] for l in range(DEPTH)]).reshape(CONV_ROWS, 128)
        packed = jnp.concatenate([_pack_small(stacked, _SMALL_LAYER), conv], axis=0)
        started = _split_start([packed], False, token, "layer_grads_start")
        small_in_flight.append(started[:4])
        return started[4]

    sq, grad_x, G = _local_step(x[0], positions[0], loss_target[0], emb_ln_g.reshape(1, -1),
                                emb_ln_b.reshape(1, -1), layer_weights, layer_weights_rest, on_sharded_grads,
                                on_layer_grads, first_after=rest0[4], embedded=(h0, h0b))

    res = {}
    landed = {}
    for l, names, started in in_flight:
        zones = _split_wait(*started, grad_x, "grads%d_%s_wait" % (l, names[0]))[1]
        for k, zone in zip(names, zones):
            landed[k, l] = zone
    w_in_res = None
    for l in reversed(range(DEPTH)):
        w_in_res = _adamw([landed["w_in", l]], t12(w_in), t12(m_w_in), t12(v_w_in), "adamw_w_in_%d" % l, W_SHARD, 512,
                          first_layer=l, into=w_in_res)
    res["w_in"] = tuple(t12(o) for o in w_in_res)
    for name, rows in (("w_uq", 256), ("w_ukv", 256), ("w_out", 128)):
        res[name] = _adamw([landed[name, l] for l in range(DEPTH)], weights[name], mom1[name], mom2[name],
                           "adamw_" + name, rows)

    layer_zone = _split_wait(*small_in_flight[0], grad_x, "layer_grads_wait")[1][0]
    n_emb_rows = sum(r for _, r in _SMALL_EMB)
    emb_zone = _all_gather([jnp.concatenate([_pack_small(G, _SMALL_EMB), sq], axis=0)], "emb_grads_all_gather")[0]
    loss = jnp.sum(emb_zone[:, n_emb_rows, 0]) * (0.5 / D_MODEL)
    n_layer_rows = sum(r for _, r in _SMALL_LAYER)
    l_small = jnp.concatenate([emb_zone[:, 0:n_emb_rows], layer_zone[:, 0:n_layer_rows]], axis=1)
    my_idx = 4 * lax.axis_index("x") + 2 * lax.axis_index("y") + lax.axis_index("c")
    conv_all_grads = layer_zone[:, n_layer_rows:].reshape(N_DEV, DEPTH * CONV_WIDTH, D_CONV)
    l_conv = lax.dynamic_slice_in_dim(conv_all_grads, my_idx * 64, 64, axis=2)
    l_conv = jnp.zeros((N_DEV, 8, 128), F32).at[:, 0:DEPTH * CONV_WIDTH, 0:64].set(l_conv)
    conv_shard = lambda a: jnp.zeros((8, 128), F32).at[0:DEPTH * CONV_WIDTH, 0:64].set(a.reshape(-1, 64))
    conv_res = _adamw([l_conv], conv_shard(conv_w)[None], conv_shard(m_conv_w)[None], conv_shard(v_conv_w)[None],
                      "adamw_conv_w", 8)
    res["conv_w"] = tuple(o[0, 0:DEPTH * CONV_WIDTH, 0:64].reshape(DEPTH, CONV_WIDTH, 64) for o in conv_res)
    small_res = _adamw([l_small], _pack_small(weights)[None], _pack_small(mom1)[None], _pack_small(mom2)[None],
                       "adamw_small", 392)
    shapes = {k: weights[k].shape for k, _ in _SMALL}
    unpacked = [_unpack_small(o[0], shapes) for o in small_res]
    for k, _ in _SMALL:
        res[k] = tuple(u[k] for u in unpacked)

    order = ("emb_ln_g", "emb_ln_b", "w_in", "q_norm_g", "kv_norm_g", "w_uq", "w_ukv", "w_pool", "pool_scale",
             "conv_w", "w_out", "b_out", "ln_g", "ln_b")
    return (loss, grad_x[None], *[res[k][0] for k in order], *[res[k][1] for k in order],
            *[res[k][2] for k in order], *[res[k][3] for k in order])
```

```python
import jax
import jax.numpy as jnp
from jax import lax
from jax.experimental import pallas as pl
from jax.experimental.pallas import tpu as pltpu

F32 = jnp.float32
BF16 = jnp.bfloat16

N_DEV = 8
D_MODEL = 2048
DEPTH = 2
N_HEADS = 8
NOPE = 128
ROPE = 64
V_DIM = 128
Q_LORA = 512
KV_LORA = 256
D_MLA = N_HEADS * V_DIM
D_POOL = 512
D_CONV = 512
POOL_WINDOWS = (2, 4, 8, 16)
POOL_GROUP = 128
CONV_WIDTH = 3
D_MIX = D_MLA + D_POOL + D_CONV
D_IN_PROJ = 4928
ROPE_THETA = 10000.0
LN_EPS = 1e-5
RMS_EPS = 1e-6
ALPHA = (2 * DEPTH) ** 0.25
ATTN_SCALE = (NOPE + ROPE) ** -0.5
ADAM_LR = 0.001
ADAM_B1 = 0.9
ADAM_B2 = 0.999
ADAM_EPS = 1e-08
ADAM_WD = 0.01
ADAM_STEP = 10

O_GMLA, O_PIN, O_GPOOL, O_CH, O_CB, O_CC, O_GCONV, O_QLAT, O_KVLAT, O_KROPE = (
    0, 1024, 1536, 2048, 2560, 3072, 3584, 4096, 4608, 4864)
NPP = 5120
N_GATED = O_QLAT
QC = NOPE + 2 * ROPE
HALO = 16
ATT_TILE = 512
ATT_CH = 256
LOG2E = 1.4426950408889634
EXP2_SCALE = ATTN_SCALE * LOG2E

GRAD_XFER = BF16
VMEM_LIMIT = 48 * 1024 * 1024
ATT_BWD_VMEM_LIMIT = 58 * 1024 * 1024
MESH_ID = pl.DeviceIdType.MESH


def _params(sem=None):
    return pltpu.CompilerParams(dimension_semantics=sem, vmem_limit_bytes=VMEM_LIMIT)


def _sigmoid(x):
    return 1.0 / (1.0 + jnp.exp(-x))


def _tile(dim, target):
    if dim <= target:
        return dim
    t = target - target % 128
    while dim % t:
        t -= 128
    return t


_DIMS = {"nn": (((1,), (0,)), ((), ())), "nt": (((1,), (1,)), ((), ())), "tn": (((0,), (0,)), ((), ()))}


def _mm(a, b, mode, out_dtype, name, res=None, bias=None, alpha=1.0, tm=1024, tn=1024, tk=2048, after=None):
    if mode == "nn":
        (M, K), (K2, N) = a.shape, b.shape
    elif mode == "nt":
        (M, K), (N, K2) = a.shape, b.shape
    else:
        (K, M), (K2, N) = a.shape, b.shape
    assert K == K2
    tm, tn, tk = _tile(M, tm), _tile(N, tn), _tile(K, tk)
    nk = K // tk
    has_res, has_bias = res is not None, bias is not None

    def body(*refs):
        a_ref, b_ref = refs[0], refs[1]
        pos = 2
        res_ref = bias_ref = None
        if has_res:
            res_ref = refs[pos]
            pos += 1
        if has_bias:
            bias_ref = refs[pos]
            pos += 1
        def finish(r, o_ref):
            if has_bias:
                r = r + bias_ref[...]
            if has_res:
                r = alpha * res_ref[...] + r
            o_ref[...] = r.astype(out_dtype)

        part = lax.dot_general(a_ref[...].astype(BF16), b_ref[...].astype(BF16), _DIMS[mode],
                               preferred_element_type=F32)
        if nk == 1:
            finish(part, refs[-1])
            return
        o_ref, acc_ref = refs[-2], refs[-1]
        k = pl.program_id(2)

        @pl.when(k == 0)
        def _():
            acc_ref[...] = part

        @pl.when(jnp.logical_and(k > 0, k < nk - 1))
        def _():
            acc_ref[...] += part

        @pl.when(k == nk - 1)
        def _():
            finish(acc_ref[...] + part, o_ref)

    if mode == "nn":
        in_specs = [pl.BlockSpec((tm, tk), lambda i, j, k: (i, k)), pl.BlockSpec((tk, tn), lambda i, j, k: (k, j))]
    elif mode == "nt":
        in_specs = [pl.BlockSpec((tm, tk), lambda i, j, k: (i, k)), pl.BlockSpec((tn, tk), lambda i, j, k: (j, k))]
    else:
        in_specs = [pl.BlockSpec((tk, tm), lambda i, j, k: (k, i)), pl.BlockSpec((tk, tn), lambda i, j, k: (k, j))]
    args = [a, b]
    if has_res:
        in_specs.append(pl.BlockSpec((tm, tn), lambda i, j, k: (i, j)))
        args.append(res)
    if has_bias:
        in_specs.append(pl.BlockSpec((1, tn), lambda i, j, k: (0, j)))
        args.append(bias)
    if after is not None:
        in_specs.append(pl.BlockSpec((8, 128), lambda i, j, k: (0, 0)))
        args.append(after)
    return pl.pallas_call(
        body, name=name,
        out_shape=jax.ShapeDtypeStruct((M, N), out_dtype),
        grid=(M // tm, N // tn, nk),
        in_specs=in_specs,
        out_specs=pl.BlockSpec((tm, tn), lambda i, j, k: (i, j)),
        scratch_shapes=[pltpu.VMEM((tm, tn), F32)] if nk > 1 else [],
        compiler_params=_params(("parallel", "parallel", "arbitrary")),
    )(*args)


def _ln_fwd(z, g, b, name, tq=512, for_matmul=True):
    T, D = z.shape

    def body(z_ref, g_ref, b_ref, y_ref, *yb_ref):
        zv = z_ref[...]
        mu = jnp.mean(zv, axis=1, keepdims=True)
        zc = zv - mu
        var = jnp.mean(zc * zc, axis=1, keepdims=True)
        y = zc * lax.rsqrt(var + LN_EPS) * g_ref[...] + b_ref[...]
        y_ref[...] = y
        if for_matmul:
            yb_ref[0][...] = y.astype(BF16)

    row = pl.BlockSpec((tq, D), lambda i: (i, 0))
    vec = pl.BlockSpec((1, D), lambda i: (0, 0))
    outs = pl.pallas_call(
        body, name=name,
        out_shape=(jax.ShapeDtypeStruct((T, D), F32),) + ((jax.ShapeDtypeStruct((T, D), BF16),) if for_matmul else ()),
        grid=(T // tq,), in_specs=[row, vec, vec], out_specs=(row,) * (2 if for_matmul else 1),
        compiler_params=_params(("parallel",)),
    )(z, g, b)
    return (outs[0], outs[1]) if for_matmul else (outs[0], None)


def _ln_bwd(dy, z, g, name, tq=512, target=None, for_matmul=True):
    T, D = z.shape
    with_loss = target is not None

    def body(*refs):
        dy_ref, z_ref, g_ref = refs[:3]
        outs = list(refs[4 if with_loss else 3:])
        dz_ref = outs.pop(0)
        dzb_ref, ds_ref = (outs.pop(0), outs.pop(0)) if for_matmul else (None, None)
        dg_ref, db_ref = outs.pop(0), outs.pop(0)
        sq_ref = outs.pop(0) if with_loss else None

        @pl.when(pl.program_id(0) == 0)
        def _():
            for ref in (dg_ref, db_ref, ds_ref, sq_ref):
                if ref is not None:
                    ref[...] = jnp.zeros_like(ref)

        zv, dyv = z_ref[...], dy_ref[...]
        if with_loss:
            err = dyv - refs[3][...]
            sq_ref[...] += jnp.sum(err * err)
            dyv = err * (1.0 / D)
        mu = jnp.mean(zv, axis=1, keepdims=True)
        zc = zv - mu
        var = jnp.mean(zc * zc, axis=1, keepdims=True)
        rstd = lax.rsqrt(var + LN_EPS)
        xh = zc * rstd
        u = dyv * g_ref[...]
        dz = rstd * (u - jnp.mean(u, axis=1, keepdims=True) - xh * jnp.mean(u * xh, axis=1, keepdims=True))
        dz_ref[...] = dz
        dg_ref[...] += jnp.sum(dyv * xh, axis=0, keepdims=True)
        db_ref[...] += jnp.sum(dyv, axis=0, keepdims=True)
        if for_matmul:
            dzb_ref[...] = dz.astype(BF16)
            ds_ref[...] += jnp.sum(dz, axis=0, keepdims=True)

    row = pl.BlockSpec((tq, D), lambda i: (i, 0))
    vec = pl.BlockSpec((1, D), lambda i: (0, 0))
    vshape = jax.ShapeDtypeStruct((1, D), F32)
    out_shape, out_specs = [jax.ShapeDtypeStruct((T, D), F32)], [row]
    if for_matmul:
        out_shape += [jax.ShapeDtypeStruct((T, D), BF16), vshape]
        out_specs += [row, vec]
    out_shape += [vshape, vshape]
    out_specs += [vec, vec]
    if with_loss:
        out_shape.append(jax.ShapeDtypeStruct((8, 128), F32))
        out_specs.append(pl.BlockSpec((8, 128), lambda i: (0, 0)))
    return pl.pallas_call(
        body, name=name, out_shape=tuple(out_shape), grid=(T // tq,),
        in_specs=[row, row, vec] + ([row] if with_loss else []), out_specs=tuple(out_specs),
        compiler_params=_params(("arbitrary",)),
    )(dy, z, g, *([target] if with_loss else []))


def _pblock(tq, width, offset):
    assert offset % width == 0
    blk = offset // width
    return pl.BlockSpec((tq, width), lambda i: (i, blk))


def _mix_fwd(proj, q_g, kv_g, w_pool, pool_scale, conv_w, name, tq=512):
    T = proj.shape[0]

    def body(ql_ref, kvl_ref, pin_ref, gp_ref, ch_ref, cb_ref, cc_ref, gc_ref, qg_ref, kvg_ref, wp_ref, ps_ref,
             cw_ref, qn_ref, kvn_ref, pooled_ref, cv_ref, ypc_ref, extp, extu):
        i = pl.program_id(0)
        for x_ref, g_ref, o_ref in ((ql_ref, qg_ref, qn_ref), (kvl_ref, kvg_ref, kvn_ref)):
            x = x_ref[...]
            r = lax.rsqrt(jnp.mean(x * x, axis=1, keepdims=True) + RMS_EPS)
            o_ref[...] = (x * r * g_ref[...]).astype(BF16)

        @pl.when(i == 0)
        def _():
            extp[0:HALO, :] = jnp.zeros((HALO, D_POOL), F32)
            extu[0:HALO, :] = jnp.zeros((HALO, D_CONV), F32)

        @pl.when(i > 0)
        def _():
            extp[0:HALO, :] = extp[tq:tq + HALO, :]
            extu[0:HALO, :] = extu[tq:tq + HALO, :]

        t1 = (i * tq + lax.broadcasted_iota(jnp.int32, (tq, 1), 0) + 1).astype(F32)
        for g, w in enumerate(POOL_WINDOWS):
            cols = slice(g * POOL_GROUP, (g + 1) * POOL_GROUP)
            pin = pin_ref[:, cols]
            extp[HALO:, cols] = pin
            s = extp[:, cols]
            k = 1
            while k < w:
                s = s + pltpu.roll(s, k, 0)
                k *= 2
            mean = s[HALO:, :] / jnp.minimum(t1, float(w))
            pooled = (mean - pin).astype(BF16)
            pooled_ref[:, cols] = pooled
            r = jnp.dot(pooled, wp_ref[g], preferred_element_type=F32)
            gp = gp_ref[:, cols]
            ypc_ref[:, cols] = (r * ps_ref[:, cols] * (gp * _sigmoid(gp))).astype(BF16)
        for g in range(D_CONV // 128):
            cols = slice(g * 128, (g + 1) * 128)
            u = cc_ref[:, cols] * ch_ref[:, cols]
            extu[HALO:, cols] = u
            eu = extu[:, cols]
            u1 = pltpu.roll(eu, 1, 0)[HALO:, :]
            u2 = pltpu.roll(eu, 2, 0)[HALO:, :]
            cv = cw_ref[0:1, cols] * u2 + cw_ref[1:2, cols] * u1 + cw_ref[2:3, cols] * u
            cv_ref[:, cols] = cv
            gc = gc_ref[:, cols]
            ypc_ref[:, D_POOL + g * 128:D_POOL + (g + 1) * 128] = (
                cb_ref[:, cols] * cv * (gc * _sigmoid(gc))).astype(BF16)

    full = lambda shape: pl.BlockSpec(shape, lambda i: (0,) * len(shape))
    row = lambda w: pl.BlockSpec((tq, w), lambda i: (i, 0))
    return pl.pallas_call(
        body, name=name,
        out_shape=(jax.ShapeDtypeStruct((T, Q_LORA), BF16), jax.ShapeDtypeStruct((T, KV_LORA), BF16),
                   jax.ShapeDtypeStruct((T, D_POOL), BF16), jax.ShapeDtypeStruct((T, D_CONV), F32),
                   jax.ShapeDtypeStruct((T, D_MIX), BF16)),
        grid=(T // tq,),
        in_specs=[_pblock(tq, Q_LORA, O_QLAT), _pblock(tq, KV_LORA, O_KVLAT), _pblock(tq, 512, O_PIN),
                  _pblock(tq, 512, O_GPOOL), _pblock(tq, 512, O_CH), _pblock(tq, 512, O_CB), _pblock(tq, 512, O_CC),
                  _pblock(tq, 512, O_GCONV), full((1, Q_LORA)), full((1, KV_LORA)), full((4, 128, 128)),
                  full((1, D_POOL)), full((8, D_CONV))],
        out_specs=(row(Q_LORA), row(KV_LORA), row(D_POOL), row(D_CONV),
                   pl.BlockSpec((tq, D_POOL + D_CONV), lambda i: (i, D_MLA // (D_POOL + D_CONV)))),
        scratch_shapes=[pltpu.VMEM((tq + HALO, D_POOL), F32), pltpu.VMEM((tq + HALO, D_CONV), F32)],
        compiler_params=_params(("arbitrary",)),
    )(proj, proj, proj, proj, proj, proj, proj, proj, q_g, kv_g, w_pool, pool_scale, conv_w)


def _mix_bwd(dmix, proj, o, pooled, cv, w_pool, pool_scale, conv_w, name, tq=ATT_CH):
    T = proj.shape[0]
    nt = T // tq
    n_ext = tq + HALO

    def body(dym_ref, dyp_ref, dyc_ref, gm_ref, gp_ref, ch_ref, cb_ref, cc_ref, gc_ref, o_ref, pooled_ref, cv_ref,
             wp_ref, ps_ref, cw_ref, do_ref, delta_ref, dg_ref, dwp_ref, dps_ref, dcw_ref, exte, extd):
        i = pl.program_id(0)
        tile = nt - 1 - i

        @pl.when(i == 0)
        def _():
            dwp_ref[...] = jnp.zeros_like(dwp_ref)
            dps_ref[...] = jnp.zeros_like(dps_ref)
            dcw_ref[...] = jnp.zeros_like(dcw_ref)
            exte[tq:, :] = jnp.zeros((HALO, D_POOL), F32)
            extd[tq:, :] = jnp.zeros((HALO, D_CONV), F32)

        @pl.when(i > 0)
        def _():
            exte[tq:, :] = exte[0:HALO, :]
            extd[tq:, :] = extd[0:HALO, :]

        ones = jnp.ones((8, V_DIM), F32)
        for h in range(N_HEADS):
            cols = slice(h * V_DIM, (h + 1) * V_DIM)
            gm = gm_ref[:, cols]
            sig = _sigmoid(gm)
            dym = dym_ref[:, cols]
            ov = o_ref[:, cols]
            do = dym * (gm * sig)
            do_ref[:, cols] = do.astype(BF16)
            rows = lax.dot_general(ones, do * ov, _DIMS["nt"], precision=lax.Precision.HIGHEST,
                                   preferred_element_type=F32)
            delta_ref[h, 0] = rows[0:1, :]
            dg_ref[:, O_GMLA + h * V_DIM:O_GMLA + (h + 1) * V_DIM] = (
                dym * ov * (sig * (1.0 + gm * (1.0 - sig)))).astype(BF16)

        t1 = (tile * tq + lax.broadcasted_iota(jnp.int32, (tq, 1), 0) + 1).astype(F32)
        for g, w in enumerate(POOL_WINDOWS):
            cols = slice(g * POOL_GROUP, (g + 1) * POOL_GROUP)
            pg = pooled_ref[:, cols]
            r = jnp.dot(pg, wp_ref[g], preferred_element_type=F32)
            gp = gp_ref[:, cols]
            sg = _sigmoid(gp)
            sl = gp * sg
            dyg = dyp_ref[:, cols]
            ps = ps_ref[:, cols]
            dg_ref[:, O_GPOOL + g * POOL_GROUP:O_GPOOL + (g + 1) * POOL_GROUP] = (
                dyg * (r * ps) * (sg * (1.0 + gp * (1.0 - sg)))).astype(BF16)
            dps_ref[:, cols] += jnp.sum(dyg * r * sl, axis=0, keepdims=True)
            dr = (dyg * ps * sl).astype(BF16)
            dwp_ref[g] += lax.dot_general(pg, dr, _DIMS["tn"], preferred_element_type=F32)
            dpooled = lax.dot_general(dr, wp_ref[g], _DIMS["nt"], preferred_element_type=F32)
            exte[0:tq, cols] = dpooled / jnp.minimum(t1, float(w))
            s = exte[:, cols]
            k = 1
            while k < w:
                s = s + pltpu.roll(s, n_ext - k, 0)
                k *= 2
            dg_ref[:, O_PIN + g * POOL_GROUP:O_PIN + (g + 1) * POOL_GROUP] = (s[0:tq, :] - dpooled).astype(BF16)

        for g in range(D_CONV // 128):
            cols = slice(g * 128, (g + 1) * 128)
            out = lambda base: slice(base + g * 128, base + (g + 1) * 128)
            gc = gc_ref[:, cols]
            sg = _sigmoid(gc)
            sl = gc * sg
            dyc = dyc_ref[:, cols]
            cb, cc, ch, cvv = cb_ref[:, cols], cc_ref[:, cols], ch_ref[:, cols], cv_ref[:, cols]
            dcv = dyc * cb * sl
            dg_ref[:, out(O_GCONV)] = (dyc * (cb * cvv) * (sg * (1.0 + gc * (1.0 - sg)))).astype(BF16)
            dg_ref[:, out(O_CB)] = (dyc * cvv * sl).astype(BF16)
            extd[0:tq, cols] = dcv
            ed = extd[:, cols]
            d1 = pltpu.roll(ed, n_ext - 1, 0)[0:tq, :]
            d2 = pltpu.roll(ed, n_ext - 2, 0)[0:tq, :]
            du = cw_ref[2:3, cols] * dcv + cw_ref[1:2, cols] * d1 + cw_ref[0:1, cols] * d2
            u = cc * ch
            dcw_ref[0:1, cols] += jnp.sum(u * d2, axis=0, keepdims=True)
            dcw_ref[1:2, cols] += jnp.sum(u * d1, axis=0, keepdims=True)
            dcw_ref[2:3, cols] += jnp.sum(u * dcv, axis=0, keepdims=True)
            dg_ref[:, out(O_CH)] = (du * cc).astype(BF16)
            dg_ref[:, out(O_CC)] = (du * ch).astype(BF16)

    def rblock(width, offset):
        assert offset % width == 0
        blk = offset // width
        return pl.BlockSpec((tq, width), lambda i: (nt - 1 - i, blk))

    full = lambda shape: pl.BlockSpec(shape, lambda i: (0,) * len(shape))
    return pl.pallas_call(
        body, name=name,
        out_shape=(jax.ShapeDtypeStruct((T, D_MLA), BF16), jax.ShapeDtypeStruct((N_HEADS, nt, 1, tq), F32),
                   jax.ShapeDtypeStruct((T, NPP), BF16),
                   jax.ShapeDtypeStruct((4, 128, 128), F32), jax.ShapeDtypeStruct((1, D_POOL), F32),
                   jax.ShapeDtypeStruct((8, D_CONV), F32)),
        grid=(nt,),
        in_specs=[rblock(1024, 0), rblock(512, 1024), rblock(512, 1536),
                  rblock(1024, O_GMLA), rblock(512, O_GPOOL), rblock(512, O_CH), rblock(512, O_CB),
                  rblock(512, O_CC), rblock(512, O_GCONV), rblock(1024, 0), rblock(512, 0), rblock(512, 0),
                  full((4, 128, 128)), full((1, D_POOL)), full((8, D_CONV))],
        out_specs=(rblock(1024, 0), pl.BlockSpec((N_HEADS, 1, 1, tq), lambda i: (0, nt - 1 - i, 0, 0)),
                   rblock(N_GATED, 0), full((4, 128, 128)), full((1, D_POOL)), full((8, D_CONV))),
        scratch_shapes=[pltpu.VMEM((n_ext, D_POOL), F32), pltpu.VMEM((n_ext, D_CONV), F32)],
        compiler_params=_params(("arbitrary",)),
    )(dmix, dmix, dmix, proj, proj, proj, proj, proj, proj, o, pooled, cv, w_pool, pool_scale, conv_w)


def _up_rms_bwd(proj, dq, dkv, w_uq, w_ukv, dkrope, dproj, q_g, kv_g, name, tq=512):
    T = proj.shape[0]
    n_lat = NPP - N_GATED

    def body(ql_ref, kvl_ref, dq_ref, dkv_ref, wq_ref, wkv_ref, dkr_ref, _, qg_ref, kvg_ref, dlat_ref, dqg_ref,
             dkvg_ref):
        @pl.when(pl.program_id(0) == 0)
        def _():
            dqg_ref[...] = jnp.zeros_like(dqg_ref)
            dkvg_ref[...] = jnp.zeros_like(dkvg_ref)

        dqn = lax.dot_general(dq_ref[...], wq_ref[...], _DIMS["nt"], preferred_element_type=F32)
        dkvn = lax.dot_general(dkv_ref[...], wkv_ref[...], _DIMS["nt"], preferred_element_type=F32)
        for x_ref, dy, g_ref, c0, dg_ref in ((ql_ref, dqn, qg_ref, 0, dqg_ref),
                                             (kvl_ref, dkvn, kvg_ref, Q_LORA, dkvg_ref)):
            x = x_ref[...]
            r = lax.rsqrt(jnp.mean(x * x, axis=1, keepdims=True) + RMS_EPS)
            xr = x * r
            u = dy * g_ref[...]
            dlat_ref[:, c0:c0 + x.shape[1]] = (r * (u - xr * jnp.mean(u * xr, axis=1, keepdims=True))).astype(BF16)
            dg_ref[...] += jnp.sum(dy * xr, axis=0, keepdims=True)
        dlat_ref[:, Q_LORA + KV_LORA:] = dkr_ref[...]

    row = lambda w: pl.BlockSpec((tq, w), lambda i: (i, 0))
    vec = lambda w: pl.BlockSpec((1, w), lambda i: (0, 0))
    assert N_GATED % n_lat == 0
    return pl.pallas_call(
        body, name=name,
        out_shape=(jax.ShapeDtypeStruct((T, NPP), BF16),
                   jax.ShapeDtypeStruct((1, Q_LORA), F32), jax.ShapeDtypeStruct((1, KV_LORA), F32)),
        grid=(T // tq,),
        in_specs=[_pblock(tq, Q_LORA, O_QLAT), _pblock(tq, KV_LORA, O_KVLAT), row(dq.shape[1]), row(dkv.shape[1]),
                  pl.BlockSpec(w_uq.shape, lambda i: (0, 0)), pl.BlockSpec(w_ukv.shape, lambda i: (0, 0)),
                  row(n_lat - Q_LORA - KV_LORA), pl.BlockSpec(memory_space=pl.ANY), vec(Q_LORA), vec(KV_LORA)],
        out_specs=(pl.BlockSpec((tq, n_lat), lambda i: (i, N_GATED // n_lat)), vec(Q_LORA), vec(KV_LORA)),
        input_output_aliases={7: 0},
        compiler_params=_params(("arbitrary",)),
    )(proj, proj, dq, dkv, w_uq, w_ukv, dkrope, dproj, q_g, kv_g)


def _swap_halves(x, lo):
    return jnp.where(lo, pltpu.roll(x, 96, 1), pltpu.roll(x, 32, 1))


def _up_rope_fwd(qn, kvn, w_uq, w_ukv, proj, cos_t, sin_t, name, tq=512, after=None):
    T = qn.shape[0]

    def body(qn_ref, kvn_ref, wq_ref, wkv_ref, kr_ref, c_ref, s_ref, *rest):
        kv_ref, qc_ref, kc_ref = rest[-3:]
        q = jnp.dot(qn_ref[...], wq_ref[...], preferred_element_type=F32)
        kv_ref[...] = jnp.dot(kvn_ref[...], wkv_ref[...], preferred_element_type=F32).astype(BF16)
        C, S = c_ref[...], s_ref[...]
        lane = lax.broadcasted_iota(jnp.int32, (tq, 128), 1)
        lo = (lane % ROPE) < (ROPE // 2)
        first = lane < ROPE

        def rope(x):
            return x * C + _swap_halves(x, lo) * S

        kr = jnp.where(first, rope(kr_ref[...]), 0.0).astype(BF16)
        n_nope = N_HEADS * NOPE
        for j in range(N_HEADS // 2):
            r = rope(q[:, n_nope + j * 128:n_nope + (j + 1) * 128])
            pair = (jnp.where(first, r, 0.0), jnp.where(first, pltpu.roll(r, 64, 1), 0.0))
            for hh in range(2):
                h = 2 * j + hh
                qc_ref[h, :, 0:NOPE] = q[:, h * NOPE:(h + 1) * NOPE].astype(BF16)
                qc_ref[h, :, NOPE:QC] = pair[hh].astype(BF16)
        for h in range(N_HEADS):
            kc_ref[h, :, 0:NOPE] = kv_ref[:, h * 256:h * 256 + NOPE]
            kc_ref[h, :, NOPE:QC] = kr

    out = jax.ShapeDtypeStruct((N_HEADS, T, QC), BF16)
    hblock = pl.BlockSpec((N_HEADS, tq, QC), lambda i: (0, i, 0))
    row = lambda w: pl.BlockSpec((tq, w), lambda i: (i, 0))
    full = lambda a: pl.BlockSpec(a.shape, lambda i: (0, 0))
    return pl.pallas_call(
        body, name=name, out_shape=(jax.ShapeDtypeStruct((T, 2 * D_MLA), BF16), out, out), grid=(T // tq,),
        in_specs=[row(Q_LORA), row(KV_LORA), full(w_uq), full(w_ukv), _pblock(tq, 128, O_KROPE), row(128), row(128)]
        + ([pl.BlockSpec((8, 128), lambda i: (0, 0))] if after is not None else []),
        out_specs=(row(2 * D_MLA), hblock, hblock),
        compiler_params=_params(("parallel",)),
    )(qn, kvn, w_uq, w_ukv, proj, cos_t, sin_t, *([after] if after is not None else []))


def _rope_bwd(dqc, dkr, cos_t, sin_t, name, tq=512):
    T = dqc.shape[1]

    def body(dqc_ref, dkr_ref, c_ref, s_ref, dq_ref, dk_ref):
        C, S = c_ref[...], s_ref[...]
        lane = lax.broadcasted_iota(jnp.int32, (tq, 128), 1)
        lo = (lane % ROPE) < (ROPE // 2)
        first = lane < ROPE

        def unrope(dy):
            return dy * C - _swap_halves(dy, lo) * S

        acc = dkr_ref[0]
        for h in range(1, N_HEADS):
            acc = acc + dkr_ref[h]
        dk_ref[:, 0:128] = jnp.where(first, unrope(acc), 0.0).astype(BF16)
        dk_ref[:, 128:256] = jnp.zeros((tq, 128), BF16)
        for j in range(N_HEADS // 2):
            d0 = dqc_ref[2 * j, :, NOPE:QC]
            d1 = dqc_ref[2 * j + 1, :, NOPE:QC]
            comb = jnp.where(first, d0, pltpu.roll(d1, 64, 1))
            dq_ref[:, 1024 + j * 128:1024 + (j + 1) * 128] = unrope(comb).astype(BF16)
        for h in range(N_HEADS):
            dq_ref[:, h * NOPE:(h + 1) * NOPE] = dqc_ref[h, :, 0:NOPE].astype(BF16)

    tab = pl.BlockSpec((tq, 128), lambda i: (i, 0))
    return pl.pallas_call(
        body, name=name,
        out_shape=(jax.ShapeDtypeStruct((T, 1536), BF16), jax.ShapeDtypeStruct((T, 256), BF16)),
        grid=(T // tq,),
        in_specs=[pl.BlockSpec((N_HEADS, tq, QC), lambda i: (0, i, 0)),
                  pl.BlockSpec((N_HEADS, tq, 128), lambda i: (0, i, 0)), tab, tab],
        out_specs=(pl.BlockSpec((tq, 1536), lambda i: (i, 0)), pl.BlockSpec((tq, 256), lambda i: (i, 0))),
        compiler_params=_params(("parallel",)),
    )(dqc, dkr, cos_t, sin_t)


def _flash_fwd(qc, kc, kv, proj, mix, name):
    H, T, _ = qc.shape
    tt = ATT_TILE
    nt = T // tt
    sp = tt // ATT_CH
    pairs = [(i, c, int(c == i)) for i in range(nt) for c in range(i + 1)]
    assert len(pairs) % 2 == 0
    table = jnp.asarray(pairs + [pairs[-1]], jnp.int32)

    def body(tab_ref, q_ref, k_ref, v_ref, g_ref, _, o_ref, y_ref, lse_ref, vt_sc, s_sc, acc_sc, m_sc, l_sc, bias_sc):
        def issue(p, slot):
            i, c = tab_ref[p, 0], tab_ref[p, 1]
            s_sc[slot] = lax.dot_general(k_ref[0, pl.ds(pl.multiple_of(c * tt, tt), tt), :],
                                         q_ref[0, pl.ds(pl.multiple_of(i * tt, tt), tt), :], _DIMS["nt"],
                                         preferred_element_type=F32)

        def softmax_pv(p, slot):
            i, c, diag = tab_ref[p, 0], tab_ref[p, 1], tab_ref[p, 2]
            s = s_sc[slot] + bias_sc[diag]
            m = m_sc[i]
            m_new = jnp.maximum(m, jnp.max(s, axis=0, keepdims=True))
            pr = jnp.exp2((s - m_new) * EXP2_SCALE)
            a = jnp.exp2((m - m_new) * EXP2_SCALE)
            l_sc[i] = a * l_sc[i] + jnp.sum(pr, axis=0, keepdims=True)
            acc_sc[i] = a * acc_sc[i] + jnp.dot(vt_sc[c], pr.astype(BF16), preferred_element_type=F32)
            m_sc[i] = m_new

        issue(0, 0)
        m_sc[...] = jnp.full_like(m_sc, -jnp.inf)
        l_sc[...] = jnp.zeros_like(l_sc)
        acc_sc[...] = jnp.zeros_like(acc_sc)
        krow = lax.broadcasted_iota(jnp.int32, (tt, tt), 0)
        qcol = lax.broadcasted_iota(jnp.int32, (tt, tt), 1)
        bias_sc[0] = jnp.zeros((tt, tt), F32)
        bias_sc[1] = jnp.where(krow <= qcol, 0.0, -jnp.inf)
        for c in range(nt):
            vt_sc[c] = v_ref[c * tt:(c + 1) * tt, :].astype(F32).T.astype(BF16)

        def two(u, carry):
            p = 2 * u
            issue(p + 1, 1)
            softmax_pv(p, 0)
            issue(p + 2, 0)
            softmax_pv(p + 1, 1)
            return carry

        lax.fori_loop(0, len(pairs) // 2, two, 0)
        for i in range(nt):
            rows = slice(i * tt, (i + 1) * tt)
            l = l_sc[i]
            o = (acc_sc[i] / l).T
            o_ref[rows, :] = o
            lse = m_sc[i] * ATTN_SCALE + jnp.log(l)
            for r in range(sp):
                lse_ref[0, sp * i + r] = lse[:, r * ATT_CH:(r + 1) * ATT_CH]
            g = g_ref[rows, :]
            y_ref[rows, :] = (o * (g * _sigmoid(g))).astype(BF16)

    head = lambda h, tab: (h, 0, 0)
    col = lambda h, tab: (0, h)
    return pl.pallas_call(
        body, name=name,
        out_shape=(jax.ShapeDtypeStruct((T, D_MLA), F32), jax.ShapeDtypeStruct((T, D_MIX), BF16),
                   jax.ShapeDtypeStruct((H, T // ATT_CH, 1, ATT_CH), F32)),
        grid_spec=pltpu.PrefetchScalarGridSpec(
            num_scalar_prefetch=1, grid=(H,),
            in_specs=[pl.BlockSpec((1, T, QC), head), pl.BlockSpec((1, T, QC), head),
                      pl.BlockSpec((T, V_DIM), lambda h, tab: (0, 2 * h + 1)), pl.BlockSpec((T, V_DIM), col),
                      pl.BlockSpec(memory_space=pl.ANY)],
            out_specs=(pl.BlockSpec((T, V_DIM), col), pl.BlockSpec((T, V_DIM), col),
                       pl.BlockSpec((1, T // ATT_CH, 1, ATT_CH), lambda h, tab: (h, 0, 0, 0))),
            scratch_shapes=[pltpu.VMEM((nt, V_DIM, tt), BF16), pltpu.VMEM((2, tt, tt), F32),
                            pltpu.VMEM((nt, V_DIM, tt), F32), pltpu.VMEM((nt, 1, tt), F32),
                            pltpu.VMEM((nt, 1, tt), F32), pltpu.VMEM((2, tt, tt), F32)]),
        input_output_aliases={5: 1},
        compiler_params=_params(("arbitrary",)),
    )(table, qc, kc, kv, proj, mix)


def _flash_bwd(qc, kc, kv, do, lse, delta, name):
    H, T, _ = qc.shape
    tt = ATT_TILE
    nt = T // tt
    sp = tt // ATT_CH
    pairs = [(j, c) for j in range(nt) for c in range(j, nt)]
    assert len(pairs) % 2 == 0
    table = jnp.asarray(pairs + [pairs[-1]], jnp.int32)

    def body(tab_ref, q_ref, k_ref, v_ref, do_ref, lse_ref, dl_ref, dq_ref, dkv_ref, dkr_ref, dqt_sc, dk_sc, dv_sc,
             s_sc, dp_sc, kt_sc, bias_sc):
        def operands(j, c):
            k0, q0 = pl.multiple_of(j * tt, tt), pl.multiple_of(c * tt, tt)
            return (k_ref[0, pl.ds(k0, tt), :], v_ref[pl.ds(k0, tt), :], q_ref[0, pl.ds(q0, tt), :],
                    do_ref[pl.ds(q0, tt), :])

        def stat_row(ref, c):
            return jnp.concatenate([ref[0, sp * c + r] for r in range(sp)], axis=1)

        def early(p, slot):
            k, v, q, dov = operands(tab_ref[p, 0], tab_ref[p, 1])
            s_sc[slot] = lax.dot_general(k, q, _DIMS["nt"], preferred_element_type=F32)
            dp_sc[slot] = lax.dot_general(v, dov, _DIMS["nt"], preferred_element_type=F32)

        def late(p, slot):
            j, c = tab_ref[p, 0], tab_ref[p, 1]
            _, _, q, dov = operands(j, c)
            s = s_sc[slot] + jnp.where(j == c, bias_sc[...], 0.0)
            pr = jnp.exp2(s * EXP2_SCALE - stat_row(lse_ref, c) * LOG2E)
            ds = (pr * (dp_sc[slot] - stat_row(dl_ref, c)) * ATTN_SCALE).astype(BF16)
            dv_sc[j] += jnp.dot(pr.astype(BF16), dov, preferred_element_type=F32)
            dk_sc[j] += jnp.dot(ds, q, preferred_element_type=F32)
            dqt_sc[c] += jnp.dot(kt_sc[j], ds, preferred_element_type=F32)

        early(0, 0)
        dqt_sc[...] = jnp.zeros_like(dqt_sc)
        dk_sc[...] = jnp.zeros_like(dk_sc)
        dv_sc[...] = jnp.zeros_like(dv_sc)
        krow = lax.broadcasted_iota(jnp.int32, (tt, tt), 0)
        qcol = lax.broadcasted_iota(jnp.int32, (tt, tt), 1)
        bias_sc[...] = jnp.where(krow <= qcol, 0.0, -jnp.inf)
        for j in range(nt):
            kt_sc[j] = k_ref[0, j * tt:(j + 1) * tt, :].astype(F32).T.astype(BF16)

        def two(u, carry):
            p = 2 * u
            early(p + 1, 1)
            late(p, 0)
            early(p + 2, 0)
            late(p + 1, 1)
            return carry

        lax.fori_loop(0, len(pairs) // 2, two, 0)
        for j in range(nt):
            rows = slice(j * tt, (j + 1) * tt)
            dk = dk_sc[j]
            dkv_ref[rows, 0:NOPE] = dk[:, 0:NOPE].astype(BF16)
            dkv_ref[rows, NOPE:] = dv_sc[j].astype(BF16)
            dkr_ref[0, rows, :] = dk[:, NOPE:]
            dq_ref[0, rows, :] = dqt_sc[j].T

    head = lambda h, tab: (h, 0, 0)
    stat = pl.BlockSpec((1, T // ATT_CH, 1, ATT_CH), lambda h, tab: (h, 0, 0, 0))
    return pl.pallas_call(
        body, name=name,
        out_shape=(jax.ShapeDtypeStruct((H, T, QC), F32), jax.ShapeDtypeStruct((T, 2 * D_MLA), BF16),
                   jax.ShapeDtypeStruct((H, T, 128), F32)),
        grid_spec=pltpu.PrefetchScalarGridSpec(
            num_scalar_prefetch=1, grid=(H,),
            in_specs=[pl.BlockSpec((1, T, QC), head), pl.BlockSpec((1, T, QC), head),
                      pl.BlockSpec((T, V_DIM), lambda h, tab: (0, 2 * h + 1)),
                      pl.BlockSpec((T, V_DIM), lambda h, tab: (0, h)), stat, stat],
            out_specs=(pl.BlockSpec((1, T, QC), head), pl.BlockSpec((T, 256), lambda h, tab: (0, h)),
                       pl.BlockSpec((1, T, 128), head)),
            scratch_shapes=[pltpu.VMEM((nt, QC, tt), F32), pltpu.VMEM((nt, tt, QC), F32),
                            pltpu.VMEM((nt, tt, V_DIM), F32), pltpu.VMEM((2, tt, tt), F32),
                            pltpu.VMEM((2, tt, tt), F32), pltpu.VMEM((nt, QC, tt), BF16), pltpu.VMEM((tt, tt), F32)]),
        compiler_params=pltpu.CompilerParams(dimension_semantics=("arbitrary",),
                                             vmem_limit_bytes=ATT_BWD_VMEM_LIMIT),
    )(table, qc, kc, kv, do, lse, delta)


def _adamw(lands, w, m, v, name, rows, cols=None, first_layer=0, into=None):
    layers, R, C = w.shape
    L = len(lands)
    cols = C if cols is None else cols
    assert R % rows == 0 and C % cols == 0 and first_layer + L <= layers
    nc = C // cols
    nb = (R // rows) * nc
    c1 = 1.0 - ADAM_B1 ** ADAM_STEP
    c2 = 1.0 - ADAM_B2 ** ADAM_STEP

    def body(*refs):
        land_refs = refs[:L]
        w_ref, m_ref, v_ref = refs[L:L + 3]
        g_ref, d_ref, nm_ref, nv_ref, g_sc = refs[-5:]
        for ll in range(L):
            @pl.when(pl.program_id(0) == ll)
            def _(land_ref=land_refs[ll]):
                g = land_ref[0].astype(F32)
                for s in range(1, N_DEV):
                    g = g + land_ref[s].astype(F32)
                g_sc[...] = g

        g = g_sc[...]
        nm = ADAM_B1 * m_ref[0] + (1.0 - ADAM_B1) * g
        nv = ADAM_B2 * v_ref[0] + (1.0 - ADAM_B2) * (g * g)
        g_ref[0] = g
        nm_ref[0] = nm
        nv_ref[0] = nv
        d_ref[0] = -ADAM_LR * ((nm / c1) / (jnp.sqrt(nv / c2) + ADAM_EPS) + ADAM_WD * w_ref[0])

    def land_spec(ll):
        def index(l, i):
            i = jnp.where(l < ll, 0, jnp.where(l > ll, nb - 1, i))
            return (0, i // nc, i % nc)
        return pl.BlockSpec((N_DEV, rows, cols), index)

    blk = pl.BlockSpec((1, rows, cols), lambda l, i: (first_layer + l, i // nc, i % nc))
    out = jax.ShapeDtypeStruct((layers, R, C), F32)
    extra = [] if into is None else list(into)
    return pl.pallas_call(
        body, name=name, out_shape=(out, out, out, out), grid=(L, nb),
        in_specs=[land_spec(ll) for ll in range(L)] + [blk, blk, blk] + [pl.BlockSpec(memory_space=pl.ANY)] * len(extra),
        out_specs=(blk, blk, blk, blk),
        input_output_aliases={L + 3 + i: i for i in range(len(extra))},
        scratch_shapes=[pltpu.VMEM((rows, cols), F32)],
        compiler_params=_params(("arbitrary", "arbitrary")),
    )(*lands, w, m, v, *extra)


def _mesh_pos():
    return lax.axis_index("x"), lax.axis_index("y"), lax.axis_index("c")


def _all_gather(arrays, name):
    n = len(arrays)

    def body(*refs):
        ins, outs = refs[:n], refs[n:2 * n]
        send_sems, recv_sems, local_sems = refs[2 * n:]
        x, y, c = _mesh_pos()
        me, sibling = (x, y, c), (x, y, 1 - c)
        chips = [(1 - x, y), (x, 1 - y), (1 - x, 1 - y)]

        def slot(a, pos):
            px, py, pc = pos
            return outs[a].at[4 * px + 2 * py + pc]

        def copy(a, k, block, to, src=None):
            return pltpu.make_async_remote_copy(
                src_ref=slot(a, block) if src is None else src, dst_ref=slot(a, block),
                send_sem=send_sems.at[a * 7 + k], recv_sem=recv_sems.at[a * 7 + k],
                device_id=to, device_id_type=MESH_ID)

        mine, first, passed = [], [], []
        for a in range(n):
            cp = pltpu.make_async_copy(ins[a], slot(a, me), local_sems.at[a])
            cp.start()
            mine.append(cp)
            cps = [copy(a, 0, me, sibling, src=ins[a])]
            cps += [copy(a, 1 + j, me, (*chip, c), src=ins[a]) for j, chip in enumerate(chips)]
            for cp in cps:
                cp.start()
            first += cps
        for j, chip in enumerate(chips):
            for a in range(n):
                copy(a, 1 + j, (*chip, c), me).wait_recv()
                cp = copy(a, 4 + j, (*chip, c), sibling)
                cp.start()
                passed.append(cp)
        for a in range(n):
            copy(a, 0, sibling, me).wait_recv()
            for j, chip in enumerate(chips):
                copy(a, 4 + j, (*chip, 1 - c), me).wait_recv()
        for cp in first + passed:
            cp.wait_send()
        for cp in mine:
            cp.wait()

    hbm = pl.BlockSpec(memory_space=pltpu.HBM)
    return pl.pallas_call(
        body, name=name,
        out_shape=tuple(jax.ShapeDtypeStruct((N_DEV,) + a.shape, a.dtype) for a in arrays),
        in_specs=[hbm] * n, out_specs=tuple([hbm] * n),
        scratch_shapes=[pltpu.SemaphoreType.DMA((7 * n,)), pltpu.SemaphoreType.DMA((7 * n,)),
                        pltpu.SemaphoreType.DMA((n,))],
    )(*arrays)


def _all_gather_under_ln(arrays, x, g, b, name, tq=512):
    n = len(arrays)
    T, D = x.shape
    nt = T // tq

    def body(*refs):
        x_ref, g_ref, b_ref = refs[:3]
        ins = refs[3:3 + n]
        y_ref, yb_ref = refs[3 + n:5 + n]
        outs = refs[5 + n:5 + 2 * n]
        send_sems, recv_sems, local_sems = refs[5 + 2 * n:]
        i = pl.program_id(0)
        mx, my, mc = _mesh_pos()
        me, sibling = (mx, my, mc), (mx, my, 1 - mc)
        chips = [(1 - mx, my), (mx, 1 - my), (1 - mx, 1 - my)]

        def slot(a, pos):
            px, py, pc = pos
            return outs[a].at[4 * px + 2 * py + pc]

        def copy(a, k, block, to, src=None):
            return pltpu.make_async_remote_copy(
                src_ref=slot(a, block) if src is None else src, dst_ref=slot(a, block),
                send_sem=send_sems.at[a * 7 + k], recv_sem=recv_sems.at[a * 7 + k],
                device_id=to, device_id_type=MESH_ID)

        def own(a):
            return pltpu.make_async_copy(ins[a], slot(a, me), local_sems.at[a])

        def first(a):
            return [copy(a, 0, me, sibling, src=ins[a])] + [
                copy(a, 1 + j, me, (*chip, mc), src=ins[a]) for j, chip in enumerate(chips)]

        @pl.when(i == 0)
        def _():
            for a in range(n):
                own(a).start()
                for cp in first(a):
                    cp.start()

        zv = x_ref[...]
        mu = jnp.mean(zv, axis=1, keepdims=True)
        zc = zv - mu
        var = jnp.mean(zc * zc, axis=1, keepdims=True)
        y = zc * lax.rsqrt(var + LN_EPS) * g_ref[...] + b_ref[...]
        y_ref[...] = y
        yb_ref[...] = y.astype(BF16)

        @pl.when(i == nt - 1)
        def _():
            passed = []
            for j, chip in enumerate(chips):
                for a in range(n):
                    copy(a, 1 + j, (*chip, mc), me).wait_recv()
                    cp = copy(a, 4 + j, (*chip, mc), sibling)
                    cp.start()
                    passed.append(cp)
            for a in range(n):
                copy(a, 0, sibling, me).wait_recv()
                for j, chip in enumerate(chips):
                    copy(a, 4 + j, (*chip, 1 - mc), me).wait_recv()
            for a in range(n):
                for cp in first(a):
                    cp.wait_send()
                own(a).wait()
            for cp in passed:
                cp.wait_send()

    row = pl.BlockSpec((tq, D), lambda i: (i, 0))
    vec = pl.BlockSpec((1, D), lambda i: (0, 0))
    hbm = pl.BlockSpec(memory_space=pltpu.HBM)
    outs = pl.pallas_call(
        body, name=name,
        out_shape=(jax.ShapeDtypeStruct((T, D), F32), jax.ShapeDtypeStruct((T, D), BF16))
        + tuple(jax.ShapeDtypeStruct((N_DEV,) + a.shape, a.dtype) for a in arrays),
        grid=(nt,), in_specs=[row, vec, vec] + [hbm] * n, out_specs=tuple([row, row] + [hbm] * n),
        scratch_shapes=[pltpu.SemaphoreType.DMA((7 * n,)), pltpu.SemaphoreType.DMA((7 * n,)),
                        pltpu.SemaphoreType.DMA((n,))],
        compiler_params=_params(("arbitrary",)),
    )(x, g, b, *arrays)
    return outs[0], outs[1], outs[2:]


_HBM = pl.BlockSpec(memory_space=pltpu.HBM)
_SEM = pl.BlockSpec(memory_space=pltpu.SEMAPHORE)
_EFFECT = pltpu.SideEffectType.DATAFLOW_SIDE_EFFECTING
N_PEERS = N_DEV - 1


def _peer(k):
    x, y, c = _mesh_pos()
    return (1 - x if k & 4 else x, 1 - y if k & 2 else y, 1 - c if k & 1 else c)


def _split_start(srcs, scatter, after, name):
    n = len(srcs)
    zones = [jax.ShapeDtypeStruct(s.shape if scatter else ((N_DEV,) + s.shape), s.dtype) for s in srcs]

    def body(*refs):
        src, zone = refs[:n], refs[n:2 * n]
        outs = refs[2 * n + 1:]
        send, recv, token = outs[:n], outs[n:2 * n], outs[4 * n]
        x, y, c = _mesh_pos()
        my_idx = 4 * x + 2 * y + c
        for a in range(n):
            pltpu.make_async_copy(src[a].at[my_idx] if scatter else src[a],
                                  zone[a].at[N_PEERS] if scatter else zone[a].at[my_idx], recv[a]).start()
            for k in range(1, N_DEV):
                px, py, pc = _peer(k)
                pltpu.make_async_remote_copy(
                    src_ref=src[a].at[4 * px + 2 * py + pc] if scatter else src[a],
                    dst_ref=zone[a].at[k - 1] if scatter else zone[a].at[my_idx],
                    send_sem=send[a], recv_sem=recv[a], device_id=(px, py, pc), device_id_type=MESH_ID).start()
        token[...] = jnp.zeros_like(token)

    hbm = lambda a: pltpu.with_memory_space_constraint(a, pltpu.HBM)
    outs = pl.pallas_call(
        body, name=name,
        out_shape=tuple([pltpu.SemaphoreType.DMA(())] * (2 * n)
                        + [pltpu.HBM(s.shape, s.dtype) for s in srcs]
                        + [pltpu.HBM(z.shape, z.dtype) for z in zones]
                        + [jax.ShapeDtypeStruct((8, 128), F32)]),
        in_specs=[_HBM] * (2 * n) + [pl.BlockSpec(memory_space=pl.ANY)],
        out_specs=tuple([_SEM] * (2 * n) + [_HBM] * (2 * n) + [pl.BlockSpec(memory_space=pltpu.VMEM)]),
        input_output_aliases={**{a: 2 * n + a for a in range(n)}, **{n + a: 3 * n + a for a in range(n)}},
        compiler_params=pltpu.CompilerParams(has_side_effects=_EFFECT),
    )(*[hbm(s) for s in srcs], *[hbm(lax.empty(z.shape, z.dtype)) for z in zones], after)
    return outs[:n], outs[n:2 * n], outs[2 * n:3 * n], outs[3 * n:4 * n], outs[4 * n]


def _split_wait(send, recv, srcs, zones, after, name):
    n = len(srcs)

    def body(*refs):
        zone = refs[n:2 * n]
        send_sems, recv_sems = refs[2 * n:3 * n], refs[3 * n:4 * n]
        x, y, c = _mesh_pos()
        for a in range(n):
            seven = zone[a].at[pl.ds(0, N_PEERS)]
            pltpu.make_async_remote_copy(src_ref=seven, dst_ref=seven, send_sem=send_sems[a], recv_sem=recv_sems[a],
                                         device_id=(x, y, 1 - c), device_id_type=MESH_ID).wait_send()
            pltpu.make_async_remote_copy(src_ref=zone[a], dst_ref=zone[a], send_sem=send_sems[a],
                                         recv_sem=recv_sems[a], device_id=(x, y, 1 - c),
                                         device_id_type=MESH_ID).wait_recv()

    outs = pl.pallas_call(
        body, name=name,
        out_shape=tuple([pltpu.HBM(s.shape, s.dtype) for s in srcs] + [pltpu.HBM(z.shape, z.dtype) for z in zones]),
        in_specs=[_HBM] * (2 * n) + [_SEM] * (2 * n) + [pl.BlockSpec(memory_space=pl.ANY)],
        out_specs=tuple([_HBM] * (2 * n)),
        input_output_aliases={a: a for a in range(2 * n)},
        compiler_params=pltpu.CompilerParams(has_side_effects=_EFFECT),
    )(*srcs, *zones, *send, *recv, after)
    return outs[:n], outs[n:]


def _cat_blocks(g, axis):
    return jnp.concatenate([g[d] for d in range(N_DEV)], axis=axis)


N_LATENT = Q_LORA + KV_LORA + ROPE
W_SHARD = D_IN_PROJ // N_DEV


def _ref_cols(lo, hi):
    out = []
    if lo < N_LATENT:
        out.append((N_GATED + lo, N_GATED + min(hi, N_LATENT)))
    if hi > N_LATENT:
        out.append((max(lo, N_LATENT) - N_LATENT, hi - N_LATENT))
    return out


def _permute_w_in_t(blocks):
    pieces = []
    for lo, hi in ((N_LATENT, D_IN_PROJ), (0, N_LATENT)):
        for d in range(N_DEV):
            a, b = max(lo, d * W_SHARD), min(hi, (d + 1) * W_SHARD)
            if a < b:
                pieces.append(blocks[d][a - d * W_SHARD:b - d * W_SHARD])
    pieces.append(jnp.zeros((NPP - D_IN_PROJ, blocks.shape[2]), blocks.dtype))
    return jnp.concatenate(pieces, axis=0)


def _split_w_in_t(w):
    slabs = []
    for d in range(N_DEV):
        parts = [w[a:b] for a, b in _ref_cols(d * W_SHARD, (d + 1) * W_SHARD)]
        slabs.append(parts[0] if len(parts) == 1 else jnp.concatenate(parts, axis=0))
    return jnp.stack(slabs)


def _permute_w_uq(w):
    w3 = w.reshape(w.shape[0], N_HEADS, NOPE + ROPE)
    return jnp.concatenate([w3[:, :, :NOPE].reshape(w.shape[0], -1), w3[:, :, NOPE:].reshape(w.shape[0], -1)], axis=1)


def _unpermute_w_uq(w):
    nope = w[:, :N_HEADS * NOPE].reshape(w.shape[0], N_HEADS, NOPE)
    rope = w[:, N_HEADS * NOPE:].reshape(w.shape[0], N_HEADS, ROPE)
    return jnp.concatenate([nope, rope], axis=2).reshape(w.shape[0], -1)


_SMALL_EMB = (("emb_ln_g", 16), ("emb_ln_b", 16))
_SMALL_LAYER = (("q_norm_g", 8), ("kv_norm_g", 8), ("w_pool", 1024), ("pool_scale", 8), ("b_out", 32),
                ("ln_g", 32), ("ln_b", 32))
_SMALL = _SMALL_EMB + _SMALL_LAYER
CONV_ROWS = DEPTH * CONV_WIDTH * D_CONV // 128


def _pack_small(d, entries=_SMALL):
    parts = []
    for name, rows in entries:
        flat = d[name].reshape(-1)
        flat = jnp.pad(flat, (0, rows * 128 - flat.shape[0]))
        parts.append(flat.reshape(rows, 128))
    return jnp.concatenate(parts, axis=0)


def _unpack_small(packed, shapes):
    out, r0 = {}, 0
    for name, rows in _SMALL:
        size = 1
        for s in shapes[name]:
            size *= s
        out[name] = packed[r0:r0 + rows].reshape(-1)[:size].reshape(shapes[name])
        r0 += rows
    return out


def _rope_tables(positions):
    half = ROPE // 2
    inv_freq = ROPE_THETA ** (-jnp.arange(half, dtype=F32) / half)
    ang = positions.astype(F32)[:, None] * inv_freq
    cos, sin = jnp.cos(ang), jnp.sin(ang)
    return jnp.concatenate([cos, cos, cos, cos], axis=1), jnp.concatenate([-sin, sin, -sin, sin], axis=1)


def _local_step(x, positions, target, emb_g, emb_b, layer_weights, layer_weights_rest, on_sharded_grads,
                on_layer_grads=None, first_after=None, embedded=None):
    cos_t, sin_t = _rope_tables(positions)
    h, hb = _ln_fwd(x, emb_g, emb_b, "emb_ln_fwd") if embedded is None else embedded
    saved = []
    for l in range(DEPTH):
        W = layer_weights(l, h)
        proj = _mm(hb, W["w_in_t"], "nt", F32, "proj_fwd", after=first_after if l == 0 else None)
        qn, kvn, pooled, cv, mix = _mix_fwd(proj, W["q_norm_g"], W["kv_norm_g"], W["w_pool"], W["pool_scale"],
                                            W["conv_w"], "mix_fwd")
        rest, token = layer_weights_rest(l, proj)
        W = {**W, **rest}
        kv, qc, kc = _up_rope_fwd(qn, kvn, W["w_uq"], W["w_ukv"], proj, cos_t, sin_t, "up_rope_fwd", after=token)
        o, mix, lse = _flash_fwd(qc, kc, kv, proj, mix, "flash_fwd")
        z = _mm(mix, W["w_out"], "nn", F32, "out_fwd", res=h, bias=W["b_out"], alpha=ALPHA)
        saved.append((W, hb, proj, qn, kvn, pooled, cv, kv, qc, kc, o, lse, mix, z))
        last = l == DEPTH - 1
        h, hb = _ln_fwd(z, W["ln_g"], W["ln_b"], "ln_fwd_out" if last else "ln_fwd", for_matmul=not last)
    dh, sq = h, None

    grads = {k: [None] * DEPTH for k in ("q_norm_g", "kv_norm_g", "w_pool", "pool_scale", "conv_w", "b_out", "ln_g",
                                         "ln_b")}
    for l in reversed(range(DEPTH)):
        W, hb_in, proj, qn, kvn, pooled, cv, kv, qc, kc, o, lse, mix, z = saved[l]
        sharded = {}
        if l == DEPTH - 1:
            dz, dzb, grads["b_out"][l], grads["ln_g"][l], grads["ln_b"][l], sq = _ln_bwd(
                dh, z, W["ln_g"], "ln_bwd_loss", target=target)
        else:
            dz, dzb, grads["b_out"][l], grads["ln_g"][l], grads["ln_b"][l] = _ln_bwd(dh, z, W["ln_g"], "ln_bwd")
        dmix = _mm(dzb, W["w_out"], "nt", F32, "out_bwd_x")
        sharded["w_out"] = _mm(mix, dzb, "tn", GRAD_XFER, "out_bwd_w", tk=4096)
        do, delta, dproj, grads["w_pool"][l], grads["pool_scale"][l], grads["conv_w"][l] = _mix_bwd(
            dmix, proj, o, pooled, cv, W["w_pool"], W["pool_scale"], W["conv_w"], "mix_bwd")
        dqc, dkv, dkr = _flash_bwd(qc, kc, kv, do, lse, delta, "flash_bwd")
        dq, dkrope = _rope_bwd(dqc, dkr, cos_t, sin_t, "rope_bwd")
        sharded["w_uq"] = _mm(qn, dq, "tn", GRAD_XFER, "q_up_bwd_w")
        sharded["w_ukv"] = _mm(kvn, dkv, "tn", GRAD_XFER, "kv_up_bwd_w")
        token = on_sharded_grads(l, sharded)
        dproj, grads["q_norm_g"][l], grads["kv_norm_g"][l] = _up_rms_bwd(
            proj, dq, dkv, W["w_uq"], W["w_ukv"], dkrope, dproj, W["q_norm_g"], W["kv_norm_g"], "up_rms_bwd")
        if l == 0 and on_layer_grads is not None:
            token = on_layer_grads(grads, token)
        d_w_in_t = _mm(dproj, hb_in, "tn", GRAD_XFER, "proj_bwd_w", tk=4096, after=token)
        token = on_sharded_grads(l, {"w_in": d_w_in_t})
        dh = _mm(dproj, W["w_in_t"], "nn", F32, "proj_bwd_x", res=dz, alpha=ALPHA, tk=2560, after=token)
    grad_x, grads["emb_ln_g"], grads["emb_ln_b"] = _ln_bwd(dh, x, emb_g, "emb_ln_bwd", for_matmul=False)
    return sq, grad_x, grads


def kernel(x, positions, emb_ln_g, emb_ln_b, w_in, q_norm_g, kv_norm_g, w_uq, w_ukv, w_pool, pool_scale, conv_w, w_out, b_out, ln_g, ln_b, loss_target, m_emb_ln_g, m_emb_ln_b, m_w_in, m_q_norm_g, m_kv_norm_g, m_w_uq, m_w_ukv, m_w_pool, m_pool_scale, m_conv_w, m_w_out, m_b_out, m_ln_g, m_ln_b, v_emb_ln_g, v_emb_ln_b, v_w_in, v_q_norm_g, v_kv_norm_g, v_w_uq, v_w_ukv, v_w_pool, v_pool_scale, v_conv_w, v_w_out, v_b_out, v_ln_g, v_ln_b):
    weights = dict(emb_ln_g=emb_ln_g, emb_ln_b=emb_ln_b, w_in=w_in, q_norm_g=q_norm_g, kv_norm_g=kv_norm_g,
                   w_uq=w_uq, w_ukv=w_ukv, w_pool=w_pool, pool_scale=pool_scale, conv_w=conv_w, w_out=w_out,
                   b_out=b_out, ln_g=ln_g, ln_b=ln_b)
    mom1 = dict(emb_ln_g=m_emb_ln_g, emb_ln_b=m_emb_ln_b, w_in=m_w_in, q_norm_g=m_q_norm_g, kv_norm_g=m_kv_norm_g,
                w_uq=m_w_uq, w_ukv=m_w_ukv, w_pool=m_w_pool, pool_scale=m_pool_scale, conv_w=m_conv_w,
                w_out=m_w_out, b_out=m_b_out, ln_g=m_ln_g, ln_b=m_ln_b)
    mom2 = dict(emb_ln_g=v_emb_ln_g, emb_ln_b=v_emb_ln_b, w_in=v_w_in, q_norm_g=v_q_norm_g, kv_norm_g=v_kv_norm_g,
                w_uq=v_w_uq, w_ukv=v_w_ukv, w_pool=v_w_pool, pool_scale=v_pool_scale, conv_w=v_conv_w,
                w_out=v_w_out, b_out=v_b_out, ln_g=v_ln_g, ln_b=v_ln_b)

    big = ("w_in", "w_uq", "w_ukv", "w_out")

    conv_pad = jnp.zeros((8, 128), F32).at[0:DEPTH * CONV_WIDTH, 0:64].set(conv_w.reshape(DEPTH * CONV_WIDTH, 64))
    t12 = lambda a: jnp.swapaxes(a, 1, 2)
    shard = lambda k, l: (t12(weights[k])[l] if k == "w_in" else weights[k][l]).astype(BF16)
    h0, h0b, (w_in0, conv_all) = _all_gather_under_ln(
        [shard("w_in", 0), conv_pad], x[0], emb_ln_g.reshape(1, -1), emb_ln_b.reshape(1, -1), "w_in0_all_gather_emb_ln")
    rest0 = _split_start([shard(k, 0) for k in big[1:]], False, w_in0, "weights0_rest_start")
    conv_full = _cat_blocks(conv_all[:, 0:DEPTH * CONV_WIDTH, 0:64], 1).reshape(DEPTH, CONV_WIDTH, D_CONV)
    conv_full = jnp.pad(conv_full, ((0, 0), (0, 8 - CONV_WIDTH), (0, 0)))
    fetched = {}

    def layer_weights(l, ready):
        if l == 0:
            w_in_blocks = w_in0
        else:
            fetched[1] = _split_wait(*fetched["w1"][:4], ready, "weights1_wait")[1]
            w_in_blocks = fetched[1][0]
        return dict(
            w_in_t=_permute_w_in_t(w_in_blocks), conv_w=conv_full[l],
            q_norm_g=q_norm_g[l].reshape(1, -1), kv_norm_g=kv_norm_g[l].reshape(1, -1),
            w_pool=w_pool[l].astype(BF16), pool_scale=pool_scale[l].reshape(1, -1), b_out=b_out[l].reshape(1, -1),
            ln_g=ln_g[l].reshape(1, -1), ln_b=ln_b[l].reshape(1, -1))

    def layer_weights_rest(l, ready):
        token = None
        if l == 0:
            blocks = _split_wait(*rest0[:4], ready, "weights0_rest_wait")[1]
            fetched["w1"] = _split_start([shard(k, 1) for k in big], False, blocks[0], "weights1_start")
            token = fetched["w1"][4]
        else:
            blocks = fetched[1][1:]
        return dict(w_uq=_permute_w_uq(_cat_blocks(blocks[0], 1)), w_ukv=_cat_blocks(blocks[1], 1),
                    w_out=blocks[2].reshape(D_MIX, D_MODEL)), token

    by_dest = dict(
        w_in=_split_w_in_t,
        w_uq=lambda g: _unpermute_w_uq(g).reshape(Q_LORA, N_DEV, -1).transpose(1, 0, 2),
        w_ukv=lambda g: g.reshape(KV_LORA, N_DEV, -1).transpose(1, 0, 2),
        w_out=lambda g: g.reshape(N_DEV, -1, D_MODEL))
    in_flight = []

    def on_sharded_grads(l, g):
        names = [k for k in big if k in g]
        srcs = [by_dest[k](g[k]) for k in names]
        started = _split_start(srcs, True, srcs[0], "grads%d_%s_start" % (l, names[0]))
        in_flight.append((l, names, started[:4]))
        return started[4]

    small_in_flight = []

    def on_layer_grads(g, token):
        stacked = {k: jnp.stack(g[k]) for k, _ in _SMALL_LAYER}
        conv = jnp.stack([g["conv_w"][l][0:CONV_WIDTH] for l in range(DEPTH)]).reshape(CONV_ROWS, 128)
        packed = jnp.concatenate([_pack_small(stacked, _SMALL_LAYER), conv], axis=0)
        started = _split_start([packed], False, token, "layer_grads_start")
        small_in_flight.append(started[:4])
        return started[4]

    sq, grad_x, G = _local_step(x[0], positions[0], loss_target[0], emb_ln_g.reshape(1, -1),
                                emb_ln_b.reshape(1, -1), layer_weights, layer_weights_rest, on_sharded_grads,
                                on_layer_grads, first_after=rest0[4], embedded=(h0, h0b))

    res = {}
    landed = {}
    for l, names, started in in_flight:
        zones = _split_wait(*started, grad_x, "grads%d_%s_wait" % (l, names[0]))[1]
        for k, zone in zip(names, zones):
            landed[k, l] = zone
    w_in_res = None
    for l in reversed(range(DEPTH)):
        w_in_res = _adamw([landed["w_in", l]], t12(w_in), t12(m_w_in), t12(v_w_in), "adamw_w_in_%d" % l, W_SHARD, 512,
                          first_layer=l, into=w_in_res)
    res["w_in"] = tuple(t12(o) for o in w_in_res)
    for name, rows in (("w_uq", 256), ("w_ukv", 256), ("w_out", 128)):
        res[name] = _adamw([landed[name, l] for l in range(DEPTH)], weights[name], mom1[name], mom2[name],
                           "adamw_" + name, rows)

    layer_zone = _split_wait(*small_in_flight[0], grad_x, "layer_grads_wait")[1][0]
    n_emb_rows = sum(r for _, r in _SMALL_EMB)
    emb_zone = _all_gather([jnp.concatenate([_pack_small(G, _SMALL_EMB), sq], axis=0)], "emb_grads_all_gather")[0]
    loss = jnp.sum(emb_zone[:, n_emb_rows, 0]) * (0.5 / D_MODEL)
    n_layer_rows = sum(r for _, r in _SMALL_LAYER)
    l_small = jnp.concatenate([emb_zone[:, 0:n_emb_rows], layer_zone[:, 0:n_layer_rows]], axis=1)
    my_idx = 4 * lax.axis_index("x") + 2 * lax.axis_index("y") + lax.axis_index("c")
    conv_all_grads = layer_zone[:, n_layer_rows:].reshape(N_DEV, DEPTH * CONV_WIDTH, D_CONV)
    l_conv = lax.dynamic_slice_in_dim(conv_all_grads, my_idx * 64, 64, axis=2)
    l_conv = jnp.zeros((N_DEV, 8, 128), F32).at[:, 0:DEPTH * CONV_WIDTH, 0:64].set(l_conv)
    conv_shard = lambda a: jnp.zeros((8, 128), F32).at[0:DEPTH * CONV_WIDTH, 0:64].set(a.reshape(-1, 64))
    conv_res = _adamw([l_conv], conv_shard(conv_w)[None], conv_shard(m_conv_w)[None], conv_shard(v_conv_w)[None],
                      "adamw_conv_w", 8)
    res["conv_w"] = tuple(o[0, 0:DEPTH * CONV_WIDTH, 0:64].reshape(DEPTH, CONV_WIDTH, 64) for o in conv_res)
    small_res = _adamw([l_small], _pack_small(weights)[None], _pack_small(mom1)[None], _pack_small(mom2)[None],
                       "adamw_small", 392)
    shapes = {k: weights[k].shape for k, _ in _SMALL}
    unpacked = [_unpack_small(o[0], shapes) for o in small_res]
    for k, _ in _SMALL:
        res[k] = tuple(u[k] for u in unpacked)

    order = ("emb_ln_g", "emb_ln_b", "w_in", "q_norm_g", "kv_norm_g", "w_uq", "w_ukv", "w_pool", "pool_scale",
             "conv_w", "w_out", "b_out", "ln_g", "ln_b")
    return (loss, grad_x[None], *[res[k][0] for k in order], *[res[k][1] for k in order],
            *[res[k][2] for k in order], *[res[k][3] for k in order])
```

```python
import jax
import jax.numpy as jnp
from jax import lax
from jax.experimental import pallas as pl
from jax.experimental.pallas import tpu as pltpu

F32 = jnp.float32
BF16 = jnp.bfloat16

N_DEV = 8
D_MODEL = 2048
DEPTH = 2
N_HEADS = 8
NOPE = 128
ROPE = 64
V_DIM = 128
Q_LORA = 512
KV_LORA = 256
D_MLA = N_HEADS * V_DIM
D_POOL = 512
D_CONV = 512
POOL_WINDOWS = (2, 4, 8, 16)
POOL_GROUP = 128
CONV_WIDTH = 3
D_MIX = D_MLA + D_POOL + D_CONV
D_IN_PROJ = 4928
ROPE_THETA = 10000.0
LN_EPS = 1e-5
RMS_EPS = 1e-6
ALPHA = (2 * DEPTH) ** 0.25
ATTN_SCALE = (NOPE + ROPE) ** -0.5
ADAM_LR = 0.001
ADAM_B1 = 0.9
ADAM_B2 = 0.999
ADAM_EPS = 1e-08
ADAM_WD = 0.01
ADAM_STEP = 10

O_GMLA, O_PIN, O_GPOOL, O_CH, O_CB, O_CC, O_GCONV, O_QLAT, O_KVLAT, O_KROPE = (
    0, 1024, 1536, 2048, 2560, 3072, 3584, 4096, 4608, 4864)
NPP = 5120
N_GATED = O_QLAT
QC = NOPE + 2 * ROPE
HALO = 16
ATT_TILE = 512
ATT_CH = 256
LOG2E = 1.4426950408889634
EXP2_SCALE = ATTN_SCALE * LOG2E

GRAD_XFER = BF16
VMEM_LIMIT = 48 * 1024 * 1024
ATT_BWD_VMEM_LIMIT = 58 * 1024 * 1024
MESH_ID = pl.DeviceIdType.MESH


def _params(sem=None):
    return pltpu.CompilerParams(dimension_semantics=sem, vmem_limit_bytes=VMEM_LIMIT)


def _sigmoid(x):
    return 1.0 / (1.0 + jnp.exp(-x))


def _tile(dim, target):
    if dim <= target:
        return dim
    t = target - target % 128
    while dim % t:
        t -= 128
    return t


_DIMS = {"nn": (((1,), (0,)), ((), ())), "nt": (((1,), (1,)), ((), ())), "tn": (((0,), (0,)), ((), ()))}


def _mm(a, b, mode, out_dtype, name, res=None, bias=None, alpha=1.0, tm=1024, tn=1024, tk=2048, after=None):
    if mode == "nn":
        (M, K), (K2, N) = a.shape, b.shape
    elif mode == "nt":
        (M, K), (N, K2) = a.shape, b.shape
    else:
        (K, M), (K2, N) = a.shape, b.shape
    assert K == K2
    tm, tn, tk = _tile(M, tm), _tile(N, tn), _tile(K, tk)
    nk = K // tk
    has_res, has_bias = res is not None, bias is not None

    def body(*refs):
        a_ref, b_ref = refs[0], refs[1]
        pos = 2
        res_ref = bias_ref = None
        if has_res:
            res_ref = refs[pos]
            pos += 1
        if has_bias:
            bias_ref = refs[pos]
            pos += 1
        def finish(r, o_ref):
            if has_bias:
                r = r + bias_ref[...]
            if has_res:
                r = alpha * res_ref[...] + r
            o_ref[...] = r.astype(out_dtype)

        part = lax.dot_general(a_ref[...].astype(BF16), b_ref[...].astype(BF16), _DIMS[mode],
                               preferred_element_type=F32)
        if nk == 1:
            finish(part, refs[-1])
            return
        o_ref, acc_ref = refs[-2], refs[-1]
        k = pl.program_id(2)

        @pl.when(k == 0)
        def _():
            acc_ref[...] = part

        @pl.when(jnp.logical_and(k > 0, k < nk - 1))
        def _():
            acc_ref[...] += part

        @pl.when(k == nk - 1)
        def _():
            finish(acc_ref[...] + part, o_ref)

    if mode == "nn":
        in_specs = [pl.BlockSpec((tm, tk), lambda i, j, k: (i, k)), pl.BlockSpec((tk, tn), lambda i, j, k: (k, j))]
    elif mode == "nt":
        in_specs = [pl.BlockSpec((tm, tk), lambda i, j, k: (i, k)), pl.BlockSpec((tn, tk), lambda i, j, k: (j, k))]
    else:
        in_specs = [pl.BlockSpec((tk, tm), lambda i, j, k: (k, i)), pl.BlockSpec((tk, tn), lambda i, j, k: (k, j))]
    args = [a, b]
    if has_res:
        in_specs.append(pl.BlockSpec((tm, tn), lambda i, j, k: (i, j)))
        args.append(res)
    if has_bias:
        in_specs.append(pl.BlockSpec((1, tn), lambda i, j, k: (0, j)))
        args.append(bias)
    if after is not None:
        in_specs.append(pl.BlockSpec((8, 128), lambda i, j, k: (0, 0)))
        args.append(after)
    return pl.pallas_call(
        body, name=name,
        out_shape=jax.ShapeDtypeStruct((M, N), out_dtype),
        grid=(M // tm, N // tn, nk),
        in_specs=in_specs,
        out_specs=pl.BlockSpec((tm, tn), lambda i, j, k: (i, j)),
        scratch_shapes=[pltpu.VMEM((tm, tn), F32)] if nk > 1 else [],
        compiler_params=_params(("parallel", "parallel", "arbitrary")),
    )(*args)


def _ln_fwd(z, g, b, name, tq=512, for_matmul=True):
    T, D = z.shape

    def body(z_ref, g_ref, b_ref, y_ref, *yb_ref):
        zv = z_ref[...]
        mu = jnp.mean(zv, axis=1, keepdims=True)
        zc = zv - mu
        var = jnp.mean(zc * zc, axis=1, keepdims=True)
        y = zc * lax.rsqrt(var + LN_EPS) * g_ref[...] + b_ref[...]
        y_ref[...] = y
        if for_matmul:
            yb_ref[0][...] = y.astype(BF16)

    row = pl.BlockSpec((tq, D), lambda i: (i, 0))
    vec = pl.BlockSpec((1, D), lambda i: (0, 0))
    outs = pl.pallas_call(
        body, name=name,
        out_shape=(jax.ShapeDtypeStruct((T, D), F32),) + ((jax.ShapeDtypeStruct((T, D), BF16),) if for_matmul else ()),
        grid=(T // tq,), in_specs=[row, vec, vec], out_specs=(row,) * (2 if for_matmul else 1),
        compiler_params=_params(("parallel",)),
    )(z, g, b)
    return (outs[0], outs[1]) if for_matmul else (outs[0], None)


def _ln_bwd(dy, z, g, name, tq=512, target=None, for_matmul=True):
    T, D = z.shape
    with_loss = target is not None

    def body(*refs):
        dy_ref, z_ref, g_ref = refs[:3]
        outs = list(refs[4 if with_loss else 3:])
        dz_ref = outs.pop(0)
        dzb_ref, ds_ref = (outs.pop(0), outs.pop(0)) if for_matmul else (None, None)
        dg_ref, db_ref = outs.pop(0), outs.pop(0)
        sq_ref = outs.pop(0) if with_loss else None

        @pl.when(pl.program_id(0) == 0)
        def _():
            for ref in (dg_ref, db_ref, ds_ref, sq_ref):
                if ref is not None:
                    ref[...] = jnp.zeros_like(ref)

        zv, dyv = z_ref[...], dy_ref[...]
        if with_loss:
            err = dyv - refs[3][...]
            sq_ref[...] += jnp.sum(err * err)
            dyv = err * (1.0 / D)
        mu = jnp.mean(zv, axis=1, keepdims=True)
        zc = zv - mu
        var = jnp.mean(zc * zc, axis=1, keepdims=True)
        rstd = lax.rsqrt(var + LN_EPS)
        xh = zc * rstd
        u = dyv * g_ref[...]
        dz = rstd * (u - jnp.mean(u, axis=1, keepdims=True) - xh * jnp.mean(u * xh, axis=1, keepdims=True))
        dz_ref[...] = dz
        dg_ref[...] += jnp.sum(dyv * xh, axis=0, keepdims=True)
        db_ref[...] += jnp.sum(dyv, axis=0, keepdims=True)
        if for_matmul:
            dzb_ref[...] = dz.astype(BF16)
            ds_ref[...] += jnp.sum(dz, axis=0, keepdims=True)

    row = pl.BlockSpec((tq, D), lambda i: (i, 0))
    vec = pl.BlockSpec((1, D), lambda i: (0, 0))
    vshape = jax.ShapeDtypeStruct((1, D), F32)
    out_shape, out_specs = [jax.ShapeDtypeStruct((T, D), F32)], [row]
    if for_matmul:
        out_shape += [jax.ShapeDtypeStruct((T, D), BF16), vshape]
        out_specs += [row, vec]
    out_shape += [vshape, vshape]
    out_specs += [vec, vec]
    if with_loss:
        out_shape.append(jax.ShapeDtypeStruct((8, 128), F32))
        out_specs.append(pl.BlockSpec((8, 128), lambda i: (0, 0)))
    return pl.pallas_call(
        body, name=name, out_shape=tuple(out_shape), grid=(T // tq,),
        in_specs=[row, row, vec] + ([row] if with_loss else []), out_specs=tuple(out_specs),
        compiler_params=_params(("arbitrary",)),
    )(dy, z, g, *([target] if with_loss else []))


def _pblock(tq, width, offset):
    assert offset % width == 0
    blk = offset // width
    return pl.BlockSpec((tq, width), lambda i: (i, blk))


def _mix_fwd(proj, q_g, kv_g, w_pool, pool_scale, conv_w, name, tq=256):
    T = proj.shape[0]

    def body(ql_ref, kvl_ref, pin_ref, gp_ref, ch_ref, cb_ref, cc_ref, gc_ref, qg_ref, kvg_ref, wp_ref, ps_ref,
             cw_ref, qn_ref, kvn_ref, pooled_ref, cv_ref, ypc_ref, extp, extu):
        i = pl.program_id(0)
        for x_ref, g_ref, o_ref in ((ql_ref, qg_ref, qn_ref), (kvl_ref, kvg_ref, kvn_ref)):
            x = x_ref[...]
            r = lax.rsqrt(jnp.mean(x * x, axis=1, keepdims=True) + RMS_EPS)
            o_ref[...] = (x * r * g_ref[...]).astype(BF16)

        @pl.when(i == 0)
        def _():
            extp[0:HALO, :] = jnp.zeros((HALO, D_POOL), F32)
            extu[0:HALO, :] = jnp.zeros((HALO, D_CONV), F32)

        @pl.when(i > 0)
        def _():
            extp[0:HALO, :] = extp[tq:tq + HALO, :]
            extu[0:HALO, :] = extu[tq:tq + HALO, :]

        t1 = (i * tq + lax.broadcasted_iota(jnp.int32, (tq, 1), 0) + 1).astype(F32)
        for g, w in enumerate(POOL_WINDOWS):
            cols = slice(g * POOL_GROUP, (g + 1) * POOL_GROUP)
            pin = pin_ref[:, cols]
            extp[HALO:, cols] = pin
            s = extp[:, cols]
            k = 1
            while k < w:
                s = s + pltpu.roll(s, k, 0)
                k *= 2
            mean = s[HALO:, :] / jnp.minimum(t1, float(w))
            pooled = (mean - pin).astype(BF16)
            pooled_ref[:, cols] = pooled
            r = jnp.dot(pooled, wp_ref[g], preferred_element_type=F32)
            gp = gp_ref[:, cols]
            ypc_ref[:, cols] = (r * ps_ref[:, cols] * (gp * _sigmoid(gp))).astype(BF16)
        for g in range(D_CONV // 128):
            cols = slice(g * 128, (g + 1) * 128)
            u = cc_ref[:, cols] * ch_ref[:, cols]
            extu[HALO:, cols] = u
            eu = extu[:, cols]
            u1 = pltpu.roll(eu, 1, 0)[HALO:, :]
            u2 = pltpu.roll(eu, 2, 0)[HALO:, :]
            cv = cw_ref[0:1, cols] * u2 + cw_ref[1:2, cols] * u1 + cw_ref[2:3, cols] * u
            cv_ref[:, cols] = cv
            gc = gc_ref[:, cols]
            ypc_ref[:, D_POOL + g * 128:D_POOL + (g + 1) * 128] = (
                cb_ref[:, cols] * cv * (gc * _sigmoid(gc))).astype(BF16)

    full = lambda shape: pl.BlockSpec(shape, lambda i: (0,) * len(shape))
    row = lambda w: pl.BlockSpec((tq, w), lambda i: (i, 0))
    return pl.pallas_call(
        body, name=name,
        out_shape=(jax.ShapeDtypeStruct((T, Q_LORA), BF16), jax.ShapeDtypeStruct((T, KV_LORA), BF16),
                   jax.ShapeDtypeStruct((T, D_POOL), BF16), jax.ShapeDtypeStruct((T, D_CONV), F32),
                   jax.ShapeDtypeStruct((T, D_MIX), BF16)),
        grid=(T // tq,),
        in_specs=[_pblock(tq, Q_LORA, O_QLAT), _pblock(tq, KV_LORA, O_KVLAT), _pblock(tq, 512, O_PIN),
                  _pblock(tq, 512, O_GPOOL), _pblock(tq, 512, O_CH), _pblock(tq, 512, O_CB), _pblock(tq, 512, O_CC),
                  _pblock(tq, 512, O_GCONV), full((1, Q_LORA)), full((1, KV_LORA)), full((4, 128, 128)),
                  full((1, D_POOL)), full((8, D_CONV))],
        out_specs=(row(Q_LORA), row(KV_LORA), row(D_POOL), row(D_CONV),
                   pl.BlockSpec((tq, D_POOL + D_CONV), lambda i: (i, D_MLA // (D_POOL + D_CONV)))),
        scratch_shapes=[pltpu.VMEM((tq + HALO, D_POOL), F32), pltpu.VMEM((tq + HALO, D_CONV), F32)],
        compiler_params=_params(("arbitrary",)),
    )(proj, proj, proj, proj, proj, proj, proj, proj, q_g, kv_g, w_pool, pool_scale, conv_w)


def _mix_bwd(dmix, proj, o, pooled, cv, w_pool, pool_scale, conv_w, name, tq=ATT_CH):
    T = proj.shape[0]
    nt = T // tq
    n_ext = tq + HALO

    def body(dym_ref, dyp_ref, dyc_ref, gm_ref, gp_ref, ch_ref, cb_ref, cc_ref, gc_ref, o_ref, pooled_ref, cv_ref,
             wp_ref, ps_ref, cw_ref, do_ref, delta_ref, dg_ref, dwp_ref, dps_ref, dcw_ref, exte, extd):
        i = pl.program_id(0)
        tile = nt - 1 - i

        @pl.when(i == 0)
        def _():
            dwp_ref[...] = jnp.zeros_like(dwp_ref)
            dps_ref[...] = jnp.zeros_like(dps_ref)
            dcw_ref[...] = jnp.zeros_like(dcw_ref)
            exte[tq:, :] = jnp.zeros((HALO, D_POOL), F32)
            extd[tq:, :] = jnp.zeros((HALO, D_CONV), F32)

        @pl.when(i > 0)
        def _():
            exte[tq:, :] = exte[0:HALO, :]
            extd[tq:, :] = extd[0:HALO, :]

        ones = jnp.ones((8, V_DIM), F32)
        for h in range(N_HEADS):
            cols = slice(h * V_DIM, (h + 1) * V_DIM)
            gm = gm_ref[:, cols]
            sig = _sigmoid(gm)
            dym = dym_ref[:, cols]
            ov = o_ref[:, cols]
            do = dym * (gm * sig)
            do_ref[:, cols] = do.astype(BF16)
            rows = lax.dot_general(ones, do * ov, _DIMS["nt"], precision=lax.Precision.HIGHEST,
                                   preferred_element_type=F32)
            delta_ref[h, 0] = rows[0:1, :]
            dg_ref[:, O_GMLA + h * V_DIM:O_GMLA + (h + 1) * V_DIM] = (
                dym * ov * (sig * (1.0 + gm * (1.0 - sig)))).astype(BF16)

        t1 = (tile * tq + lax.broadcasted_iota(jnp.int32, (tq, 1), 0) + 1).astype(F32)
        for g, w in enumerate(POOL_WINDOWS):
            cols = slice(g * POOL_GROUP, (g + 1) * POOL_GROUP)
            pg = pooled_ref[:, cols]
            r = jnp.dot(pg, wp_ref[g], preferred_element_type=F32)
            gp = gp_ref[:, cols]
            sg = _sigmoid(gp)
            sl = gp * sg
            dyg = dyp_ref[:, cols]
            ps = ps_ref[:, cols]
            dg_ref[:, O_GPOOL + g * POOL_GROUP:O_GPOOL + (g + 1) * POOL_GROUP] = (
                dyg * (r * ps) * (sg * (1.0 + gp * (1.0 - sg)))).astype(BF16)
            dps_ref[:, cols] += jnp.sum(dyg * r * sl, axis=0, keepdims=True)
            dr = (dyg * ps * sl).astype(BF16)
            dwp_ref[g] += lax.dot_general(pg, dr, _DIMS["tn"], preferred_element_type=F32)
            dpooled = lax.dot_general(dr, wp_ref[g], _DIMS["nt"], preferred_element_type=F32)
            exte[0:tq, cols] = dpooled / jnp.minimum(t1, float(w))
            s = exte[:, cols]
            k = 1
            while k < w:
                s = s + pltpu.roll(s, n_ext - k, 0)
                k *= 2
            dg_ref[:, O_PIN + g * POOL_GROUP:O_PIN + (g + 1) * POOL_GROUP] = (s[0:tq, :] - dpooled).astype(BF16)

        for g in range(D_CONV // 128):
            cols = slice(g * 128, (g + 1) * 128)
            out = lambda base: slice(base + g * 128, base + (g + 1) * 128)
            gc = gc_ref[:, cols]
            sg = _sigmoid(gc)
            sl = gc * sg
            dyc = dyc_ref[:, cols]
            cb, cc, ch, cvv = cb_ref[:, cols], cc_ref[:, cols], ch_ref[:, cols], cv_ref[:, cols]
            dcv = dyc * cb * sl
            dg_ref[:, out(O_GCONV)] = (dyc * (cb * cvv) * (sg * (1.0 + gc * (1.0 - sg)))).astype(BF16)
            dg_ref[:, out(O_CB)] = (dyc * cvv * sl).astype(BF16)
            extd[0:tq, cols] = dcv
            ed = extd[:, cols]
            d1 = pltpu.roll(ed, n_ext - 1, 0)[0:tq, :]
            d2 = pltpu.roll(ed, n_ext - 2, 0)[0:tq, :]
            du = cw_ref[2:3, cols] * dcv + cw_ref[1:2, cols] * d1 + cw_ref[0:1, cols] * d2
            u = cc * ch
            dcw_ref[0:1, cols] += jnp.sum(u * d2, axis=0, keepdims=True)
            dcw_ref[1:2, cols] += jnp.sum(u * d1, axis=0, keepdims=True)
            dcw_ref[2:3, cols] += jnp.sum(u * dcv, axis=0, keepdims=True)
            dg_ref[:, out(O_CH)] = (du * cc).astype(BF16)
            dg_ref[:, out(O_CC)] = (du * ch).astype(BF16)

    def rblock(width, offset):
        assert offset % width == 0
        blk = offset // width
        return pl.BlockSpec((tq, width), lambda i: (nt - 1 - i, blk))

    full = lambda shape: pl.BlockSpec(shape, lambda i: (0,) * len(shape))
    return pl.pallas_call(
        body, name=name,
        out_shape=(jax.ShapeDtypeStruct((T, D_MLA), BF16), jax.ShapeDtypeStruct((N_HEADS, nt, 1, tq), F32),
                   jax.ShapeDtypeStruct((T, NPP), BF16),
                   jax.ShapeDtypeStruct((4, 128, 128), F32), jax.ShapeDtypeStruct((1, D_POOL), F32),
                   jax.ShapeDtypeStruct((8, D_CONV), F32)),
        grid=(nt,),
        in_specs=[rblock(1024, 0), rblock(512, 1024), rblock(512, 1536),
                  rblock(1024, O_GMLA), rblock(512, O_GPOOL), rblock(512, O_CH), rblock(512, O_CB),
                  rblock(512, O_CC), rblock(512, O_GCONV), rblock(1024, 0), rblock(512, 0), rblock(512, 0),
                  full((4, 128, 128)), full((1, D_POOL)), full((8, D_CONV))],
        out_specs=(rblock(1024, 0), pl.BlockSpec((N_HEADS, 1, 1, tq), lambda i: (0, nt - 1 - i, 0, 0)),
                   rblock(N_GATED, 0), full((4, 128, 128)), full((1, D_POOL)), full((8, D_CONV))),
        scratch_shapes=[pltpu.VMEM((n_ext, D_POOL), F32), pltpu.VMEM((n_ext, D_CONV), F32)],
        compiler_params=_params(("arbitrary",)),
    )(dmix, dmix, dmix, proj, proj, proj, proj, proj, proj, o, pooled, cv, w_pool, pool_scale, conv_w)


def _up_rms_bwd(proj, dq, dkv, w_uq, w_ukv, dkrope, dproj, q_g, kv_g, name, tq=1024):
    T = proj.shape[0]
    n_lat = NPP - N_GATED

    def body(ql_ref, kvl_ref, dq_ref, dkv_ref, wq_ref, wkv_ref, dkr_ref, _, qg_ref, kvg_ref, dlat_ref, dqg_ref,
             dkvg_ref):
        @pl.when(pl.program_id(0) == 0)
        def _():
            dqg_ref[...] = jnp.zeros_like(dqg_ref)
            dkvg_ref[...] = jnp.zeros_like(dkvg_ref)

        dqn = lax.dot_general(dq_ref[...], wq_ref[...], _DIMS["nt"], preferred_element_type=F32)
        dkvn = lax.dot_general(dkv_ref[...], wkv_ref[...], _DIMS["nt"], preferred_element_type=F32)
        for x_ref, dy, g_ref, c0, dg_ref in ((ql_ref, dqn, qg_ref, 0, dqg_ref),
                                             (kvl_ref, dkvn, kvg_ref, Q_LORA, dkvg_ref)):
            x = x_ref[...]
            r = lax.rsqrt(jnp.mean(x * x, axis=1, keepdims=True) + RMS_EPS)
            xr = x * r
            u = dy * g_ref[...]
            dlat_ref[:, c0:c0 + x.shape[1]] = (r * (u - xr * jnp.mean(u * xr, axis=1, keepdims=True))).astype(BF16)
            dg_ref[...] += jnp.sum(dy * xr, axis=0, keepdims=True)
        dlat_ref[:, Q_LORA + KV_LORA:] = dkr_ref[...]

    row = lambda w: pl.BlockSpec((tq, w), lambda i: (i, 0))
    vec = lambda w: pl.BlockSpec((1, w), lambda i: (0, 0))
    assert N_GATED % n_lat == 0
    return pl.pallas_call(
        body, name=name,
        out_shape=(jax.ShapeDtypeStruct((T, NPP), BF16),
                   jax.ShapeDtypeStruct((1, Q_LORA), F32), jax.ShapeDtypeStruct((1, KV_LORA), F32)),
        grid=(T // tq,),
        in_specs=[_pblock(tq, Q_LORA, O_QLAT), _pblock(tq, KV_LORA, O_KVLAT), row(dq.shape[1]), row(dkv.shape[1]),
                  pl.BlockSpec(w_uq.shape, lambda i: (0, 0)), pl.BlockSpec(w_ukv.shape, lambda i: (0, 0)),
                  row(n_lat - Q_LORA - KV_LORA), pl.BlockSpec(memory_space=pl.ANY), vec(Q_LORA), vec(KV_LORA)],
        out_specs=(pl.BlockSpec((tq, n_lat), lambda i: (i, N_GATED // n_lat)), vec(Q_LORA), vec(KV_LORA)),
        input_output_aliases={7: 0},
        compiler_params=_params(("arbitrary",)),
    )(proj, proj, dq, dkv, w_uq, w_ukv, dkrope, dproj, q_g, kv_g)


def _swap_halves(x, lo):
    return jnp.where(lo, pltpu.roll(x, 96, 1), pltpu.roll(x, 32, 1))


def _up_rope_fwd(qn, kvn, w_uq, w_ukv, proj, cos_t, sin_t, name, tq=1024, after=None):
    T = qn.shape[0]

    def body(qn_ref, kvn_ref, wq_ref, wkv_ref, kr_ref, c_ref, s_ref, *rest):
        kv_ref, qc_ref, kc_ref = rest[-3:]
        q = jnp.dot(qn_ref[...], wq_ref[...], preferred_element_type=F32)
        kv_ref[...] = jnp.dot(kvn_ref[...], wkv_ref[...], preferred_element_type=F32).astype(BF16)
        C, S = c_ref[...], s_ref[...]
        lane = lax.broadcasted_iota(jnp.int32, (tq, 128), 1)
        lo = (lane % ROPE) < (ROPE // 2)
        first = lane < ROPE

        def rope(x):
            return x * C + _swap_halves(x, lo) * S

        kr = jnp.where(first, rope(kr_ref[...]), 0.0).astype(BF16)
        n_nope = N_HEADS * NOPE
        for j in range(N_HEADS // 2):
            r = rope(q[:, n_nope + j * 128:n_nope + (j + 1) * 128])
            pair = (jnp.where(first, r, 0.0), jnp.where(first, pltpu.roll(r, 64, 1), 0.0))
            for hh in range(2):
                h = 2 * j + hh
                qc_ref[h, :, 0:NOPE] = q[:, h * NOPE:(h + 1) * NOPE].astype(BF16)
                qc_ref[h, :, NOPE:QC] = pair[hh].astype(BF16)
        for h in range(N_HEADS):
            kc_ref[h, :, 0:NOPE] = kv_ref[:, h * 256:h * 256 + NOPE]
            kc_ref[h, :, NOPE:QC] = kr

    out = jax.ShapeDtypeStruct((N_HEADS, T, QC), BF16)
    hblock = pl.BlockSpec((N_HEADS, tq, QC), lambda i: (0, i, 0))
    row = lambda w: pl.BlockSpec((tq, w), lambda i: (i, 0))
    full = lambda a: pl.BlockSpec(a.shape, lambda i: (0, 0))
    return pl.pallas_call(
        body, name=name, out_shape=(jax.ShapeDtypeStruct((T, 2 * D_MLA), BF16), out, out), grid=(T // tq,),
        in_specs=[row(Q_LORA), row(KV_LORA), full(w_uq), full(w_ukv), _pblock(tq, 128, O_KROPE), row(128), row(128)]
        + ([pl.BlockSpec((8, 128), lambda i: (0, 0))] if after is not None else []),
        out_specs=(row(2 * D_MLA), hblock, hblock),
        compiler_params=_params(("parallel",)),
    )(qn, kvn, w_uq, w_ukv, proj, cos_t, sin_t, *([after] if after is not None else []))


def _rope_bwd(dqc, dkr, cos_t, sin_t, name, tq=1024):
    T = dqc.shape[1]

    def body(dqc_ref, dkr_ref, c_ref, s_ref, dq_ref, dk_ref):
        C, S = c_ref[...], s_ref[...]
        lane = lax.broadcasted_iota(jnp.int32, (tq, 128), 1)
        lo = (lane % ROPE) < (ROPE // 2)
        first = lane < ROPE

        def unrope(dy):
            return dy * C - _swap_halves(dy, lo) * S

        acc = dkr_ref[0]
        for h in range(1, N_HEADS):
            acc = acc + dkr_ref[h]
        dk_ref[:, 0:128] = jnp.where(first, unrope(acc), 0.0).astype(BF16)
        dk_ref[:, 128:256] = jnp.zeros((tq, 128), BF16)
        for j in range(N_HEADS // 2):
            d0 = dqc_ref[2 * j, :, NOPE:QC]
            d1 = dqc_ref[2 * j + 1, :, NOPE:QC]
            comb = jnp.where(first, d0, pltpu.roll(d1, 64, 1))
            dq_ref[:, 1024 + j * 128:1024 + (j + 1) * 128] = unrope(comb).astype(BF16)
        for h in range(N_HEADS):
            dq_ref[:, h * NOPE:(h + 1) * NOPE] = dqc_ref[h, :, 0:NOPE].astype(BF16)

    tab = pl.BlockSpec((tq, 128), lambda i: (i, 0))
    return pl.pallas_call(
        body, name=name,
        out_shape=(jax.ShapeDtypeStruct((T, 1536), BF16), jax.ShapeDtypeStruct((T, 256), BF16)),
        grid=(T // tq,),
        in_specs=[pl.BlockSpec((N_HEADS, tq, QC), lambda i: (0, i, 0)),
                  pl.BlockSpec((N_HEADS, tq, 128), lambda i: (0, i, 0)), tab, tab],
        out_specs=(pl.BlockSpec((tq, 1536), lambda i: (i, 0)), pl.BlockSpec((tq, 256), lambda i: (i, 0))),
        compiler_params=_params(("parallel",)),
    )(dqc, dkr, cos_t, sin_t)


def _flash_fwd(qc, kc, kv, proj, mix, name):
    H, T, _ = qc.shape
    tt = ATT_TILE
    nt = T // tt
    sp = tt // ATT_CH
    pairs = [(i, c, int(c == i)) for i in range(nt) for c in range(i + 1)]
    assert len(pairs) % 2 == 0
    table = jnp.asarray(pairs + [pairs[-1]], jnp.int32)

    def body(tab_ref, q_ref, k_ref, v_ref, g_ref, _, o_ref, y_ref, lse_ref, vt_sc, s_sc, acc_sc, m_sc, l_sc, bias_sc):
        def issue(p, slot):
            i, c = tab_ref[p, 0], tab_ref[p, 1]
            s_sc[slot] = lax.dot_general(k_ref[0, pl.ds(pl.multiple_of(c * tt, tt), tt), :],
                                         q_ref[0, pl.ds(pl.multiple_of(i * tt, tt), tt), :], _DIMS["nt"],
                                         preferred_element_type=F32)

        def softmax_pv(p, slot):
            i, c, diag = tab_ref[p, 0], tab_ref[p, 1], tab_ref[p, 2]
            s = s_sc[slot] + bias_sc[diag]
            m = m_sc[i]
            m_new = jnp.maximum(m, jnp.max(s, axis=0, keepdims=True))
            pr = jnp.exp2((s - m_new) * EXP2_SCALE)
            a = jnp.exp2((m - m_new) * EXP2_SCALE)
            l_sc[i] = a * l_sc[i] + jnp.sum(pr, axis=0, keepdims=True)
            acc_sc[i] = a * acc_sc[i] + jnp.dot(vt_sc[c], pr.astype(BF16), preferred_element_type=F32)
            m_sc[i] = m_new

        issue(0, 0)
        m_sc[...] = jnp.full_like(m_sc, -jnp.inf)
        l_sc[...] = jnp.zeros_like(l_sc)
        acc_sc[...] = jnp.zeros_like(acc_sc)
        krow = lax.broadcasted_iota(jnp.int32, (tt, tt), 0)
        qcol = lax.broadcasted_iota(jnp.int32, (tt, tt), 1)
        bias_sc[0] = jnp.zeros((tt, tt), F32)
        bias_sc[1] = jnp.where(krow <= qcol, 0.0, -jnp.inf)
        for c in range(nt):
            vt_sc[c] = v_ref[c * tt:(c + 1) * tt, :].astype(F32).T.astype(BF16)

        def two(u, carry):
            p = 2 * u
            issue(p + 1, 1)
            softmax_pv(p, 0)
            issue(p + 2, 0)
            softmax_pv(p + 1, 1)
            return carry

        lax.fori_loop(0, len(pairs) // 2, two, 0)
        for i in range(nt):
            rows = slice(i * tt, (i + 1) * tt)
            l = l_sc[i]
            o = (acc_sc[i] / l).T
            o_ref[rows, :] = o
            lse = m_sc[i] * ATTN_SCALE + jnp.log(l)
            for r in range(sp):
                lse_ref[0, sp * i + r] = lse[:, r * ATT_CH:(r + 1) * ATT_CH]
            g = g_ref[rows, :]
            y_ref[rows, :] = (o * (g * _sigmoid(g))).astype(BF16)

    head = lambda h, tab: (h, 0, 0)
    col = lambda h, tab: (0, h)
    return pl.pallas_call(
        body, name=name,
        out_shape=(jax.ShapeDtypeStruct((T, D_MLA), F32), jax.ShapeDtypeStruct((T, D_MIX), BF16),
                   jax.ShapeDtypeStruct((H, T // ATT_CH, 1, ATT_CH), F32)),
        grid_spec=pltpu.PrefetchScalarGridSpec(
            num_scalar_prefetch=1, grid=(H,),
            in_specs=[pl.BlockSpec((1, T, QC), head), pl.BlockSpec((1, T, QC), head),
                      pl.BlockSpec((T, V_DIM), lambda h, tab: (0, 2 * h + 1)), pl.BlockSpec((T, V_DIM), col),
                      pl.BlockSpec(memory_space=pl.ANY)],
            out_specs=(pl.BlockSpec((T, V_DIM), col), pl.BlockSpec((T, V_DIM), col),
                       pl.BlockSpec((1, T // ATT_CH, 1, ATT_CH), lambda h, tab: (h, 0, 0, 0))),
            scratch_shapes=[pltpu.VMEM((nt, V_DIM, tt), BF16), pltpu.VMEM((2, tt, tt), F32),
                            pltpu.VMEM((nt, V_DIM, tt), F32), pltpu.VMEM((nt, 1, tt), F32),
                            pltpu.VMEM((nt, 1, tt), F32), pltpu.VMEM((2, tt, tt), F32)]),
        input_output_aliases={5: 1},
        compiler_params=_params(("arbitrary",)),
    )(table, qc, kc, kv, proj, mix)


def _flash_bwd(qc, kc, kv, do, lse, delta, name):
    H, T, _ = qc.shape
    tt = ATT_TILE
    nt = T // tt
    sp = tt // ATT_CH
    pairs = [(j, c) for j in range(nt) for c in range(j, nt)]
    assert len(pairs) % 2 == 0
    table = jnp.asarray(pairs + [pairs[-1]], jnp.int32)

    def body(tab_ref, q_ref, k_ref, v_ref, do_ref, lse_ref, dl_ref, dq_ref, dkv_ref, dkr_ref, dqt_sc, dk_sc, dv_sc,
             s_sc, dp_sc, kt_sc, bias_sc):
        def operands(j, c):
            k0, q0 = pl.multiple_of(j * tt, tt), pl.multiple_of(c * tt, tt)
            return (k_ref[0, pl.ds(k0, tt), :], v_ref[pl.ds(k0, tt), :], q_ref[0, pl.ds(q0, tt), :],
                    do_ref[pl.ds(q0, tt), :])

        def stat_row(ref, c):
            return jnp.concatenate([ref[0, sp * c + r] for r in range(sp)], axis=1)

        def early(p, slot):
            k, v, q, dov = operands(tab_ref[p, 0], tab_ref[p, 1])
            s_sc[slot] = lax.dot_general(k, q, _DIMS["nt"], preferred_element_type=F32)
            dp_sc[slot] = lax.dot_general(v, dov, _DIMS["nt"], preferred_element_type=F32)

        def late(p, slot):
            j, c = tab_ref[p, 0], tab_ref[p, 1]
            _, _, q, dov = operands(j, c)
            s = s_sc[slot] + jnp.where(j == c, bias_sc[...], 0.0)
            pr = jnp.exp2(s * EXP2_SCALE - stat_row(lse_ref, c) * LOG2E)
            ds = (pr * (dp_sc[slot] - stat_row(dl_ref, c)) * ATTN_SCALE).astype(BF16)
            dv_sc[j] += jnp.dot(pr.astype(BF16), dov, preferred_element_type=F32)
            dk_sc[j] += jnp.dot(ds, q, preferred_element_type=F32)
            dqt_sc[c] += jnp.dot(kt_sc[j], ds, preferred_element_type=F32)

        early(0, 0)
        dqt_sc[...] = jnp.zeros_like(dqt_sc)
        dk_sc[...] = jnp.zeros_like(dk_sc)
        dv_sc[...] = jnp.zeros_like(dv_sc)
        krow = lax.broadcasted_iota(jnp.int32, (tt, tt), 0)
        qcol = lax.broadcasted_iota(jnp.int32, (tt, tt), 1)
        bias_sc[...] = jnp.where(krow <= qcol, 0.0, -jnp.inf)
        for j in range(nt):
            kt_sc[j] = k_ref[0, j * tt:(j + 1) * tt, :].astype(F32).T.astype(BF16)

        def two(u, carry):
            p = 2 * u
            early(p + 1, 1)
            late(p, 0)
            early(p + 2, 0)
            late(p + 1, 1)
            return carry

        lax.fori_loop(0, len(pairs) // 2, two, 0)
        for j in range(nt):
            rows = slice(j * tt, (j + 1) * tt)
            dk = dk_sc[j]
            dkv_ref[rows, 0:NOPE] = dk[:, 0:NOPE].astype(BF16)
            dkv_ref[rows, NOPE:] = dv_sc[j].astype(BF16)
            dkr_ref[0, rows, :] = dk[:, NOPE:]
            dq_ref[0, rows, :] = dqt_sc[j].T

    head = lambda h, tab: (h, 0, 0)
    stat = pl.BlockSpec((1, T // ATT_CH, 1, ATT_CH), lambda h, tab: (h, 0, 0, 0))
    return pl.pallas_call(
        body, name=name,
        out_shape=(jax.ShapeDtypeStruct((H, T, QC), F32), jax.ShapeDtypeStruct((T, 2 * D_MLA), BF16),
                   jax.ShapeDtypeStruct((H, T, 128), F32)),
        grid_spec=pltpu.PrefetchScalarGridSpec(
            num_scalar_prefetch=1, grid=(H,),
            in_specs=[pl.BlockSpec((1, T, QC), head), pl.BlockSpec((1, T, QC), head),
                      pl.BlockSpec((T, V_DIM), lambda h, tab: (0, 2 * h + 1)),
                      pl.BlockSpec((T, V_DIM), lambda h, tab: (0, h)), stat, stat],
            out_specs=(pl.BlockSpec((1, T, QC), head), pl.BlockSpec((T, 256), lambda h, tab: (0, h)),
                       pl.BlockSpec((1, T, 128), head)),
            scratch_shapes=[pltpu.VMEM((nt, QC, tt), F32), pltpu.VMEM((nt, tt, QC), F32),
                            pltpu.VMEM((nt, tt, V_DIM), F32), pltpu.VMEM((2, tt, tt), F32),
                            pltpu.VMEM((2, tt, tt), F32), pltpu.VMEM((nt, QC, tt), BF16), pltpu.VMEM((tt, tt), F32)]),
        compiler_params=pltpu.CompilerParams(dimension_semantics=("arbitrary",),
                                             vmem_limit_bytes=ATT_BWD_VMEM_LIMIT),
    )(table, qc, kc, kv, do, lse, delta)


def _adamw(lands, w, m, v, name, rows, cols=None, first_layer=0, into=None):
    layers, R, C = w.shape
    L = len(lands)
    cols = C if cols is None else cols
    assert R % rows == 0 and C % cols == 0 and first_layer + L <= layers
    nc = C // cols
    nb = (R // rows) * nc
    c1 = 1.0 - ADAM_B1 ** ADAM_STEP
    c2 = 1.0 - ADAM_B2 ** ADAM_STEP

    def body(*refs):
        land_refs = refs[:L]
        w_ref, m_ref, v_ref = refs[L:L + 3]
        g_ref, d_ref, nm_ref, nv_ref, g_sc = refs[-5:]
        for ll in range(L):
            @pl.when(pl.program_id(0) == ll)
            def _(land_ref=land_refs[ll]):
                g = land_ref[0].astype(F32)
                for s in range(1, N_DEV):
                    g = g + land_ref[s].astype(F32)
                g_sc[...] = g

        g = g_sc[...]
        nm = ADAM_B1 * m_ref[0] + (1.0 - ADAM_B1) * g
        nv = ADAM_B2 * v_ref[0] + (1.0 - ADAM_B2) * (g * g)
        g_ref[0] = g
        nm_ref[0] = nm
        nv_ref[0] = nv
        d_ref[0] = -ADAM_LR * ((nm / c1) / (jnp.sqrt(nv / c2) + ADAM_EPS) + ADAM_WD * w_ref[0])

    def land_spec(ll):
        def index(l, i):
            i = jnp.where(l < ll, 0, jnp.where(l > ll, nb - 1, i))
            return (0, i // nc, i % nc)
        return pl.BlockSpec((N_DEV, rows, cols), index)

    blk = pl.BlockSpec((1, rows, cols), lambda l, i: (first_layer + l, i // nc, i % nc))
    out = jax.ShapeDtypeStruct((layers, R, C), F32)
    extra = [] if into is None else list(into)
    return pl.pallas_call(
        body, name=name, out_shape=(out, out, out, out), grid=(L, nb),
        in_specs=[land_spec(ll) for ll in range(L)] + [blk, blk, blk] + [pl.BlockSpec(memory_space=pl.ANY)] * len(extra),
        out_specs=(blk, blk, blk, blk),
        input_output_aliases={L + 3 + i: i for i in range(len(extra))},
        scratch_shapes=[pltpu.VMEM((rows, cols), F32)],
        compiler_params=_params(("arbitrary", "arbitrary")),
    )(*lands, w, m, v, *extra)


def _mesh_pos():
    return lax.axis_index("x"), lax.axis_index("y"), lax.axis_index("c")


def _all_gather(arrays, name):
    n = len(arrays)

    def body(*refs):
        ins, outs = refs[:n], refs[n:2 * n]
        send_sems, recv_sems, local_sems = refs[2 * n:]
        x, y, c = _mesh_pos()
        me, sibling = (x, y, c), (x, y, 1 - c)
        chips = [(1 - x, y), (x, 1 - y), (1 - x, 1 - y)]

        def slot(a, pos):
            px, py, pc = pos
            return outs[a].at[4 * px + 2 * py + pc]

        def copy(a, k, block, to, src=None):
            return pltpu.make_async_remote_copy(
                src_ref=slot(a, block) if src is None else src, dst_ref=slot(a, block),
                send_sem=send_sems.at[a * 7 + k], recv_sem=recv_sems.at[a * 7 + k],
                device_id=to, device_id_type=MESH_ID)

        mine, first, passed = [], [], []
        for a in range(n):
            cp = pltpu.make_async_copy(ins[a], slot(a, me), local_sems.at[a])
            cp.start()
            mine.append(cp)
            cps = [copy(a, 0, me, sibling, src=ins[a])]
            cps += [copy(a, 1 + j, me, (*chip, c), src=ins[a]) for j, chip in enumerate(chips)]
            for cp in cps:
                cp.start()
            first += cps
        for j, chip in enumerate(chips):
            for a in range(n):
                copy(a, 1 + j, (*chip, c), me).wait_recv()
                cp = copy(a, 4 + j, (*chip, c), sibling)
                cp.start()
                passed.append(cp)
        for a in range(n):
            copy(a, 0, sibling, me).wait_recv()
            for j, chip in enumerate(chips):
                copy(a, 4 + j, (*chip, 1 - c), me).wait_recv()
        for cp in first + passed:
            cp.wait_send()
        for cp in mine:
            cp.wait()

    hbm = pl.BlockSpec(memory_space=pltpu.HBM)
    return pl.pallas_call(
        body, name=name,
        out_shape=tuple(jax.ShapeDtypeStruct((N_DEV,) + a.shape, a.dtype) for a in arrays),
        in_specs=[hbm] * n, out_specs=tuple([hbm] * n),
        scratch_shapes=[pltpu.SemaphoreType.DMA((7 * n,)), pltpu.SemaphoreType.DMA((7 * n,)),
                        pltpu.SemaphoreType.DMA((n,))],
    )(*arrays)


def _all_gather_under_ln(arrays, x, g, b, name, tq=512):
    n = len(arrays)
    T, D = x.shape
    nt = T // tq

    def body(*refs):
        x_ref, g_ref, b_ref = refs[:3]
        ins = refs[3:3 + n]
        y_ref, yb_ref = refs[3 + n:5 + n]
        outs = refs[5 + n:5 + 2 * n]
        send_sems, recv_sems, local_sems = refs[5 + 2 * n:]
        i = pl.program_id(0)
        mx, my, mc = _mesh_pos()
        me, sibling = (mx, my, mc), (mx, my, 1 - mc)
        chips = [(1 - mx, my), (mx, 1 - my), (1 - mx, 1 - my)]

        def slot(a, pos):
            px, py, pc = pos
            return outs[a].at[4 * px + 2 * py + pc]

        def copy(a, k, block, to, src=None):
            return pltpu.make_async_remote_copy(
                src_ref=slot(a, block) if src is None else src, dst_ref=slot(a, block),
                send_sem=send_sems.at[a * 7 + k], recv_sem=recv_sems.at[a * 7 + k],
                device_id=to, device_id_type=MESH_ID)

        def own(a):
            return pltpu.make_async_copy(ins[a], slot(a, me), local_sems.at[a])

        def first(a):
            return [copy(a, 0, me, sibling, src=ins[a])] + [
                copy(a, 1 + j, me, (*chip, mc), src=ins[a]) for j, chip in enumerate(chips)]

        @pl.when(i == 0)
        def _():
            for a in range(n):
                own(a).start()
                for cp in first(a):
                    cp.start()

        zv = x_ref[...]
        mu = jnp.mean(zv, axis=1, keepdims=True)
        zc = zv - mu
        var = jnp.mean(zc * zc, axis=1, keepdims=True)
        y = zc * lax.rsqrt(var + LN_EPS) * g_ref[...] + b_ref[...]
        y_ref[...] = y
        yb_ref[...] = y.astype(BF16)

        @pl.when(i == nt - 1)
        def _():
            passed = []
            for j, chip in enumerate(chips):
                for a in range(n):
                    copy(a, 1 + j, (*chip, mc), me).wait_recv()
                    cp = copy(a, 4 + j, (*chip, mc), sibling)
                    cp.start()
                    passed.append(cp)
            for a in range(n):
                copy(a, 0, sibling, me).wait_recv()
                for j, chip in enumerate(chips):
                    copy(a, 4 + j, (*chip, 1 - mc), me).wait_recv()
            for a in range(n):
                for cp in first(a):
                    cp.wait_send()
                own(a).wait()
            for cp in passed:
                cp.wait_send()

    row = pl.BlockSpec((tq, D), lambda i: (i, 0))
    vec = pl.BlockSpec((1, D), lambda i: (0, 0))
    hbm = pl.BlockSpec(memory_space=pltpu.HBM)
    outs = pl.pallas_call(
        body, name=name,
        out_shape=(jax.ShapeDtypeStruct((T, D), F32), jax.ShapeDtypeStruct((T, D), BF16))
        + tuple(jax.ShapeDtypeStruct((N_DEV,) + a.shape, a.dtype) for a in arrays),
        grid=(nt,), in_specs=[row, vec, vec] + [hbm] * n, out_specs=tuple([row, row] + [hbm] * n),
        scratch_shapes=[pltpu.SemaphoreType.DMA((7 * n,)), pltpu.SemaphoreType.DMA((7 * n,)),
                        pltpu.SemaphoreType.DMA((n,))],
        compiler_params=_params(("arbitrary",)),
    )(x, g, b, *arrays)
    return outs[0], outs[1], outs[2:]


_HBM = pl.BlockSpec(memory_space=pltpu.HBM)
_SEM = pl.BlockSpec(memory_space=pltpu.SEMAPHORE)
_EFFECT = pltpu.SideEffectType.DATAFLOW_SIDE_EFFECTING
N_PEERS = N_DEV - 1


def _peer(k):
    x, y, c = _mesh_pos()
    return (1 - x if k & 4 else x, 1 - y if k & 2 else y, 1 - c if k & 1 else c)


def _split_start(srcs, scatter, after, name):
    n = len(srcs)
    zones = [jax.ShapeDtypeStruct(s.shape if scatter else ((N_DEV,) + s.shape), s.dtype) for s in srcs]

    def body(*refs):
        src, zone = refs[:n], refs[n:2 * n]
        outs = refs[2 * n + 1:]
        send, recv, token = outs[:n], outs[n:2 * n], outs[4 * n]
        x, y, c = _mesh_pos()
        my_idx = 4 * x + 2 * y + c
        for a in range(n):
            pltpu.make_async_copy(src[a].at[my_idx] if scatter else src[a],
                                  zone[a].at[N_PEERS] if scatter else zone[a].at[my_idx], recv[a]).start()
            for k in range(1, N_DEV):
                px, py, pc = _peer(k)
                pltpu.make_async_remote_copy(
                    src_ref=src[a].at[4 * px + 2 * py + pc] if scatter else src[a],
                    dst_ref=zone[a].at[k - 1] if scatter else zone[a].at[my_idx],
                    send_sem=send[a], recv_sem=recv[a], device_id=(px, py, pc), device_id_type=MESH_ID).start()
        token[...] = jnp.zeros_like(token)

    hbm = lambda a: pltpu.with_memory_space_constraint(a, pltpu.HBM)
    outs = pl.pallas_call(
        body, name=name,
        out_shape=tuple([pltpu.SemaphoreType.DMA(())] * (2 * n)
                        + [pltpu.HBM(s.shape, s.dtype) for s in srcs]
                        + [pltpu.HBM(z.shape, z.dtype) for z in zones]
                        + [jax.ShapeDtypeStruct((8, 128), F32)]),
        in_specs=[_HBM] * (2 * n) + [pl.BlockSpec(memory_space=pl.ANY)],
        out_specs=tuple([_SEM] * (2 * n) + [_HBM] * (2 * n) + [pl.BlockSpec(memory_space=pltpu.VMEM)]),
        input_output_aliases={**{a: 2 * n + a for a in range(n)}, **{n + a: 3 * n + a for a in range(n)}},
        compiler_params=pltpu.CompilerParams(has_side_effects=_EFFECT),
    )(*[hbm(s) for s in srcs], *[hbm(lax.empty(z.shape, z.dtype)) for z in zones], after)
    return outs[:n], outs[n:2 * n], outs[2 * n:3 * n], outs[3 * n:4 * n], outs[4 * n]


def _split_wait(send, recv, srcs, zones, after, name):
    n = len(srcs)

    def body(*refs):
        zone = refs[n:2 * n]
        send_sems, recv_sems = refs[2 * n:3 * n], refs[3 * n:4 * n]
        x, y, c = _mesh_pos()
        for a in range(n):
            seven = zone[a].at[pl.ds(0, N_PEERS)]
            pltpu.make_async_remote_copy(src_ref=seven, dst_ref=seven, send_sem=send_sems[a], recv_sem=recv_sems[a],
                                         device_id=(x, y, 1 - c), device_id_type=MESH_ID).wait_send()
            pltpu.make_async_remote_copy(src_ref=zone[a], dst_ref=zone[a], send_sem=send_sems[a],
                                         recv_sem=recv_sems[a], device_id=(x, y, 1 - c),
                                         device_id_type=MESH_ID).wait_recv()

    outs = pl.pallas_call(
        body, name=name,
        out_shape=tuple([pltpu.HBM(s.shape, s.dtype) for s in srcs] + [pltpu.HBM(z.shape, z.dtype) for z in zones]),
        in_specs=[_HBM] * (2 * n) + [_SEM] * (2 * n) + [pl.BlockSpec(memory_space=pl.ANY)],
        out_specs=tuple([_HBM] * (2 * n)),
        input_output_aliases={a: a for a in range(2 * n)},
        compiler_params=pltpu.CompilerParams(has_side_effects=_EFFECT),
    )(*srcs, *zones, *send, *recv, after)
    return outs[:n], outs[n:]


def _cat_blocks(g, axis):
    return jnp.concatenate([g[d] for d in range(N_DEV)], axis=axis)


N_LATENT = Q_LORA + KV_LORA + ROPE
W_SHARD = D_IN_PROJ // N_DEV


def _ref_cols(lo, hi):
    out = []
    if lo < N_LATENT:
        out.append((N_GATED + lo, N_GATED + min(hi, N_LATENT)))
    if hi > N_LATENT:
        out.append((max(lo, N_LATENT) - N_LATENT, hi - N_LATENT))
    return out


def _permute_w_in_t(blocks):
    pieces = []
    for lo, hi in ((N_LATENT, D_IN_PROJ), (0, N_LATENT)):
        for d in range(N_DEV):
            a, b = max(lo, d * W_SHARD), min(hi, (d + 1) * W_SHARD)
            if a < b:
                pieces.append(blocks[d][a - d * W_SHARD:b - d * W_SHARD])
    pieces.append(jnp.zeros((NPP - D_IN_PROJ, blocks.shape[2]), blocks.dtype))
    return jnp.concatenate(pieces, axis=0)


def _split_w_in_t(w):
    slabs = []
    for d in range(N_DEV):
        parts = [w[a:b] for a, b in _ref_cols(d * W_SHARD, (d + 1) * W_SHARD)]
        slabs.append(parts[0] if len(parts) == 1 else jnp.concatenate(parts, axis=0))
    return jnp.stack(slabs)


def _permute_w_uq(w):
    w3 = w.reshape(w.shape[0], N_HEADS, NOPE + ROPE)
    return jnp.concatenate([w3[:, :, :NOPE].reshape(w.shape[0], -1), w3[:, :, NOPE:].reshape(w.shape[0], -1)], axis=1)


def _unpermute_w_uq(w):
    nope = w[:, :N_HEADS * NOPE].reshape(w.shape[0], N_HEADS, NOPE)
    rope = w[:, N_HEADS * NOPE:].reshape(w.shape[0], N_HEADS, ROPE)
    return jnp.concatenate([nope, rope], axis=2).reshape(w.shape[0], -1)


_SMALL_EMB = (("emb_ln_g", 16), ("emb_ln_b", 16))
_SMALL_LAYER = (("q_norm_g", 8), ("kv_norm_g", 8), ("w_pool", 1024), ("pool_scale", 8), ("b_out", 32),
                ("ln_g", 32), ("ln_b", 32))
_SMALL = _SMALL_EMB + _SMALL_LAYER
CONV_ROWS = DEPTH * CONV_WIDTH * D_CONV // 128


def _pack_small(d, entries=_SMALL):
    parts = []
    for name, rows in entries:
        flat = d[name].reshape(-1)
        flat = jnp.pad(flat, (0, rows * 128 - flat.shape[0]))
        parts.append(flat.reshape(rows, 128))
    return jnp.concatenate(parts, axis=0)


def _unpack_small(packed, shapes):
    out, r0 = {}, 0
    for name, rows in _SMALL:
        size = 1
        for s in shapes[name]:
            size *= s
        out[name] = packed[r0:r0 + rows].reshape(-1)[:size].reshape(shapes[name])
        r0 += rows
    return out


def _rope_tables(positions):
    half = ROPE // 2
    inv_freq = ROPE_THETA ** (-jnp.arange(half, dtype=F32) / half)
    ang = positions.astype(F32)[:, None] * inv_freq
    cos, sin = jnp.cos(ang), jnp.sin(ang)
    return jnp.concatenate([cos, cos, cos, cos], axis=1), jnp.concatenate([-sin, sin, -sin, sin], axis=1)


def _local_step(x, positions, target, emb_g, emb_b, layer_weights, layer_weights_rest, on_sharded_grads,
                on_layer_grads=None, first_after=None, embedded=None):
    cos_t, sin_t = _rope_tables(positions)
    h, hb = _ln_fwd(x, emb_g, emb_b, "emb_ln_fwd") if embedded is None else embedded
    saved = []
    for l in range(DEPTH):
        W = layer_weights(l, h)
        proj = _mm(hb, W["w_in_t"], "nt", F32, "proj_fwd", after=first_after if l == 0 else None)
        qn, kvn, pooled, cv, mix = _mix_fwd(proj, W["q_norm_g"], W["kv_norm_g"], W["w_pool"], W["pool_scale"],
                                            W["conv_w"], "mix_fwd")
        rest, token = layer_weights_rest(l, proj)
        W = {**W, **rest}
        kv, qc, kc = _up_rope_fwd(qn, kvn, W["w_uq"], W["w_ukv"], proj, cos_t, sin_t, "up_rope_fwd", after=token)
        o, mix, lse = _flash_fwd(qc, kc, kv, proj, mix, "flash_fwd")
        z = _mm(mix, W["w_out"], "nn", F32, "out_fwd", res=h, bias=W["b_out"], alpha=ALPHA)
        saved.append((W, hb, proj, qn, kvn, pooled, cv, kv, qc, kc, o, lse, mix, z))
        last = l == DEPTH - 1
        h, hb = _ln_fwd(z, W["ln_g"], W["ln_b"], "ln_fwd_out" if last else "ln_fwd", for_matmul=not last)
    dh, sq = h, None

    grads = {k: [None] * DEPTH for k in ("q_norm_g", "kv_norm_g", "w_pool", "pool_scale", "conv_w", "b_out", "ln_g",
                                         "ln_b")}
    for l in reversed(range(DEPTH)):
        W, hb_in, proj, qn, kvn, pooled, cv, kv, qc, kc, o, lse, mix, z = saved[l]
        sharded = {}
        if l == DEPTH - 1:
            dz, dzb, grads["b_out"][l], grads["ln_g"][l], grads["ln_b"][l], sq = _ln_bwd(
                dh, z, W["ln_g"], "ln_bwd_loss", target=target)
        else:
            dz, dzb, grads["b_out"][l], grads["ln_g"][l], grads["ln_b"][l] = _ln_bwd(dh, z, W["ln_g"], "ln_bwd")
        dmix = _mm(dzb, W["w_out"], "nt", F32, "out_bwd_x")
        sharded["w_out"] = _mm(mix, dzb, "tn", GRAD_XFER, "out_bwd_w", tk=4096)
        do, delta, dproj, grads["w_pool"][l], grads["pool_scale"][l], grads["conv_w"][l] = _mix_bwd(
            dmix, proj, o, pooled, cv, W["w_pool"], W["pool_scale"], W["conv_w"], "mix_bwd")
        dqc, dkv, dkr = _flash_bwd(qc, kc, kv, do, lse, delta, "flash_bwd")
        dq, dkrope = _rope_bwd(dqc, dkr, cos_t, sin_t, "rope_bwd")
        sharded["w_uq"] = _mm(qn, dq, "tn", GRAD_XFER, "q_up_bwd_w")
        sharded["w_ukv"] = _mm(kvn, dkv, "tn", GRAD_XFER, "kv_up_bwd_w")
        token = on_sharded_grads(l, sharded)
        dproj, grads["q_norm_g"][l], grads["kv_norm_g"][l] = _up_rms_bwd(
            proj, dq, dkv, W["w_uq"], W["w_ukv"], dkrope, dproj, W["q_norm_g"], W["kv_norm_g"], "up_rms_bwd")
        if l == 0 and on_layer_grads is not None:
            token = on_layer_grads(grads, token)
        d_w_in_t = _mm(dproj, hb_in, "tn", GRAD_XFER, "proj_bwd_w", tk=4096, after=token)
        token = on_sharded_grads(l, {"w_in": d_w_in_t})
        dh = _mm(dproj, W["w_in_t"], "nn", F32, "proj_bwd_x", res=dz, alpha=ALPHA, tk=2560, after=token)
    grad_x, grads["emb_ln_g"], grads["emb_ln_b"] = _ln_bwd(dh, x, emb_g, "emb_ln_bwd", for_matmul=False)
    return sq, grad_x, grads


def kernel(x, positions, emb_ln_g, emb_ln_b, w_in, q_norm_g, kv_norm_g, w_uq, w_ukv, w_pool, pool_scale, conv_w, w_out, b_out, ln_g, ln_b, loss_target, m_emb_ln_g, m_emb_ln_b, m_w_in, m_q_norm_g, m_kv_norm_g, m_w_uq, m_w_ukv, m_w_pool, m_pool_scale, m_conv_w, m_w_out, m_b_out, m_ln_g, m_ln_b, v_emb_ln_g, v_emb_ln_b, v_w_in, v_q_norm_g, v_kv_norm_g, v_w_uq, v_w_ukv, v_w_pool, v_pool_scale, v_conv_w, v_w_out, v_b_out, v_ln_g, v_ln_b):
    weights = dict(emb_ln_g=emb_ln_g, emb_ln_b=emb_ln_b, w_in=w_in, q_norm_g=q_norm_g, kv_norm_g=kv_norm_g,
                   w_uq=w_uq, w_ukv=w_ukv, w_pool=w_pool, pool_scale=pool_scale, conv_w=conv_w, w_out=w_out,
                   b_out=b_out, ln_g=ln_g, ln_b=ln_b)
    mom1 = dict(emb_ln_g=m_emb_ln_g, emb_ln_b=m_emb_ln_b, w_in=m_w_in, q_norm_g=m_q_norm_g, kv_norm_g=m_kv_norm_g,
                w_uq=m_w_uq, w_ukv=m_w_ukv, w_pool=m_w_pool, pool_scale=m_pool_scale, conv_w=m_conv_w,
                w_out=m_w_out, b_out=m_b_out, ln_g=m_ln_g, ln_b=m_ln_b)
    mom2 = dict(emb_ln_g=v_emb_ln_g, emb_ln_b=v_emb_ln_b, w_in=v_w_in, q_norm_g=v_q_norm_g, kv_norm_g=v_kv_norm_g,
                w_uq=v_w_uq, w_ukv=v_w_ukv, w_pool=v_w_pool, pool_scale=v_pool_scale, conv_w=v_conv_w,
                w_out=v_w_out, b_out=v_b_out, ln_g=v_ln_g, ln_b=v_ln_b)

    big = ("w_in", "w_uq", "w_ukv", "w_out")

    conv_pad = jnp.zeros((8, 128), F32).at[0:DEPTH * CONV_WIDTH, 0:64].set(conv_w.reshape(DEPTH * CONV_WIDTH, 64))
    t12 = lambda a: jnp.swapaxes(a, 1, 2)
    shard = lambda k, l: (t12(weights[k])[l] if k == "w_in" else weights[k][l]).astype(BF16)
    h0, h0b, (w_in0, conv_all) = _all_gather_under_ln(
        [shard("w_in", 0), conv_pad], x[0], emb_ln_g.reshape(1, -1), emb_ln_b.reshape(1, -1), "w_in0_all_gather_emb_ln")
    rest0 = _split_start([shard(k, 0) for k in big[1:]], False, w_in0, "weights0_rest_start")
    conv_full = _cat_blocks(conv_all[:, 0:DEPTH * CONV_WIDTH, 0:64], 1).reshape(DEPTH, CONV_WIDTH, D_CONV)
    conv_full = jnp.pad(conv_full, ((0, 0), (0, 8 - CONV_WIDTH), (0, 0)))
    fetched = {}

    def layer_weights(l, ready):
        if l == 0:
            w_in_blocks = w_in0
        else:
            fetched[1] = _split_wait(*fetched["w1"][:4], ready, "weights1_wait")[1]
            w_in_blocks = fetched[1][0]
        return dict(
            w_in_t=_permute_w_in_t(w_in_blocks), conv_w=conv_full[l],
            q_norm_g=q_norm_g[l].reshape(1, -1), kv_norm_g=kv_norm_g[l].reshape(1, -1),
            w_pool=w_pool[l].astype(BF16), pool_scale=pool_scale[l].reshape(1, -1), b_out=b_out[l].reshape(1, -1),
            ln_g=ln_g[l].reshape(1, -1), ln_b=ln_b[l].reshape(1, -1))

    def layer_weights_rest(l, ready):
        token = None
        if l == 0:
            blocks = _split_wait(*rest0[:4], ready, "weights0_rest_wait")[1]
            fetched["w1"] = _split_start([shard(k, 1) for k in big], False, blocks[0], "weights1_start")
            token = fetched["w1"][4]
        else:
            blocks = fetched[1][1:]
        return dict(w_uq=_permute_w_uq(_cat_blocks(blocks[0], 1)), w_ukv=_cat_blocks(blocks[1], 1),
                    w_out=blocks[2].reshape(D_MIX, D_MODEL)), token

    by_dest = dict(
        w_in=_split_w_in_t,
        w_uq=lambda g: _unpermute_w_uq(g).reshape(Q_LORA, N_DEV, -1).transpose(1, 0, 2),
        w_ukv=lambda g: g.reshape(KV_LORA, N_DEV, -1).transpose(1, 0, 2),
        w_out=lambda g: g.reshape(N_DEV, -1, D_MODEL))
    in_flight = []

    def on_sharded_grads(l, g):
        names = [k for k in big if k in g]
        srcs = [by_dest[k](g[k]) for k in names]
        started = _split_start(srcs, True, srcs[0], "grads%d_%s_start" % (l, names[0]))
        in_flight.append((l, names, started[:4]))
        return started[4]

    small_in_flight = []

    def on_layer_grads(g, token):
        stacked = {k: jnp.stack(g[k]) for k, _ in _SMALL_LAYER}
        conv = jnp.stack([g["conv_w"][l][0:CONV_WIDTH] for l in range(DEPTH)]).reshape(CONV_ROWS, 128)
        packed = jnp.concatenate([_pack_small(stacked, _SMALL_LAYER), conv], axis=0)
        started = _split_start([packed], False, token, "layer_grads_start")
        small_in_flight.append(started[:4])
        return started[4]

    sq, grad_x, G = _local_step(x[0], positions[0], loss_target[0], emb_ln_g.reshape(1, -1),
                                emb_ln_b.reshape(1, -1), layer_weights, layer_weights_rest, on_sharded_grads,
                                on_layer_grads, first_after=rest0[4], embedded=(h0, h0b))

    res = {}
    landed = {}
    for l, names, started in in_flight:
        zones = _split_wait(*started, grad_x, "grads%d_%s_wait" % (l, names[0]))[1]
        for k, zone in zip(names, zones):
            landed[k, l] = zone
    w_in_res = None
    for l in reversed(range(DEPTH)):
        w_in_res = _adamw([landed["w_in", l]], t12(w_in), t12(m_w_in), t12(v_w_in), "adamw_w_in_%d" % l, W_SHARD, 512,
                          first_layer=l, into=w_in_res)
    res["w_in"] = tuple(t12(o) for o in w_in_res)
    for name, rows in (("w_uq", 256), ("w_ukv", 256), ("w_out", 128)):
        res[name] = _adamw([landed[name, l] for l in range(DEPTH)], weights[name], mom1[name], mom2[name],
                           "adamw_" + name, rows)

    layer_zone = _split_wait(*small_in_flight[0], grad_x, "layer_grads_wait")[1][0]
    n_emb_rows = sum(r for _, r in _SMALL_EMB)
    emb_zone = _all_gather([jnp.concatenate([_pack_small(G, _SMALL_EMB), sq], axis=0)], "emb_grads_all_gather")[0]
    loss = jnp.sum(emb_zone[:, n_emb_rows, 0]) * (0.5 / D_MODEL)
    n_layer_rows = sum(r for _, r in _SMALL_LAYER)
    l_small = jnp.concatenate([emb_zone[:, 0:n_emb_rows], layer_zone[:, 0:n_layer_rows]], axis=1)
    my_idx = 4 * lax.axis_index("x") + 2 * lax.axis_index("y") + lax.axis_index("c")
    conv_all_grads = layer_zone[:, n_layer_rows:].reshape(N_DEV, DEPTH * CONV_WIDTH, D_CONV)
    l_conv = lax.dynamic_slice_in_dim(conv_all_grads, my_idx * 64, 64, axis=2)
    l_conv = jnp.zeros((N_DEV, 8, 128), F32).at[:, 0:DEPTH * CONV_WIDTH, 0:64].set(l_conv)
    conv_shard = lambda a: jnp.zeros((8, 128), F32).at[0:DEPTH * CONV_WIDTH, 0:64].set(a.reshape(-1, 64))
    conv_res = _adamw([l_conv], conv_shard(conv_w)[None], conv_shard(m_conv_w)[None], conv_shard(v_conv_w)[None],
                      "adamw_conv_w", 8)
    res["conv_w"] = tuple(o[0, 0:DEPTH * CONV_WIDTH, 0:64].reshape(DEPTH, CONV_WIDTH, 64) for o in conv_res)
    small_res = _adamw([l_small], _pack_small(weights)[None], _pack_small(mom1)[None], _pack_small(mom2)[None],
                       "adamw_small", 392)
    shapes = {k: weights[k].shape for k, _ in _SMALL}
    unpacked = [_unpack_small(o[0], shapes) for o in small_res]
    for k, _ in _SMALL:
        res[k] = tuple(u[k] for u in unpacked)

    order = ("emb_ln_g", "emb_ln_b", "w_in", "q_norm_g", "kv_norm_g", "w_uq", "w_ukv", "w_pool", "pool_scale",
             "conv_w", "w_out", "b_out", "ln_g", "ln_b")
    return (loss, grad_x[None], *[res[k][0] for k in order], *[res[k][1] for k in order],
            *[res[k][2] for k in order], *[res[k][3] for k in order])
```
